```python
import jax, jax.numpy as jnp
from jax import lax
import numpy as np

D_MODEL = 1024
BATCH = 8
SEQ = 8192
DEPTH = 1

POOL_WINDOWS = (2, 4, 8, 16)
POOL_GROUPS = len(POOL_WINDOWS)
POOL_GROUP_DIM = D_MODEL // 16
POOL_WIDTH = POOL_GROUPS * POOL_GROUP_DIM
ATTN_PATTERNS = ((128, 1), (512, 4), (2048, 16))
N_ATTN_GROUPS = len(ATTN_PATTERNS)
HEAD_DIM = 64
HEADS_PER_GROUP = 4
N_ATTN_HEADS = N_ATTN_GROUPS * HEADS_PER_GROUP
ATTN_WIDTH = N_ATTN_HEADS * HEAD_DIM
ATTN_OUT_WIDTH = HEADS_PER_GROUP * HEAD_DIM
ROPE_THETA = 10000.0
BLOCK = 128
N_BRANCHES = 2
IN_WIDTH = POOL_WIDTH + 3 * ATTN_WIDTH + N_BRANCHES * D_MODEL
D_FF = ((8 * D_MODEL // 3 + 255) // 256) * 256
N_MOD = 9
EPS = 1e-6

kernel_name = "hybrid_pool_dilated_attn_macaron_block"


def rmsnorm(x, g):
    xf = x.astype(jnp.float32)
    y = xf * lax.rsqrt(jnp.mean(xf * xf, axis=-1, keepdims=True) + EPS)
    return (y * g.astype(jnp.float32)).astype(x.dtype)


def modulate(x, shift, scale):
    return x * (1 + scale) + shift


def swiglu(u, w_in, w_out):
    a, b = jnp.split(u @ w_in, 2, axis=-1)
    return (jax.nn.silu(a) * b) @ w_out


def rope_tables(positions, dtype):
    inv_freq = ROPE_THETA ** (-jnp.arange(0, HEAD_DIM, 2, dtype=jnp.float32) / HEAD_DIM)
    ang = positions.astype(jnp.float32)[..., None] * inv_freq
    return jnp.cos(ang)[:, :, None, :].astype(dtype), jnp.sin(ang)[:, :, None, :].astype(dtype)


def apply_rope(t, cos, sin):
    t1, t2 = jnp.split(t, 2, axis=-1)
    return jnp.concatenate([t1 * cos - t2 * sin, t2 * cos + t1 * sin], axis=-1)


def multiscale_pool(p, w_pool, pool_scale):
    B, S, _ = p.shape
    pf = p.astype(jnp.float32)
    cs = jnp.pad(jnp.cumsum(pf, axis=1), ((0, 0), (1, 0), (0, 0)))
    t = jnp.arange(S)
    outs = []
    for gi, w in enumerate(POOL_WINDOWS):
        sl = slice(gi * POOL_GROUP_DIM, (gi + 1) * POOL_GROUP_DIM)
        csg = cs[..., sl]
        lagged = jnp.pad(csg[:, :S + 1 - w], ((0, 0), (w - 1, 0), (0, 0)))
        count = jnp.minimum(t + 1, w).astype(jnp.float32)[None, :, None]
        outs.append((csg[:, 1:] - lagged) / count - pf[..., sl])
    d = jnp.stack(outs, axis=2).astype(p.dtype)
    y = jnp.einsum('bsgc,gcd->bsgd', d, w_pool)
    return y.reshape(B, S, POOL_WIDTH) * pool_scale


def dilated_window_attention(q, k, v, steps, dilation):
    B, S, H, Dh = q.shape
    L = S // dilation
    nb = -(-L // BLOCK)
    Lp = nb * BLOCK

    def to_strided(t):
        t = t.reshape(B, L, dilation, H, Dh).transpose(0, 3, 2, 1, 4)
        return jnp.pad(t, ((0, 0), (0, 0), (0, 0), (0, Lp - L), (0, 0)))

    qs, ks, vs = to_strided(q), to_strided(k), to_strided(v)
    qb = qs.reshape(B, H, dilation, nb, BLOCK, Dh)

    def band(t):
        tp = jnp.pad(t, ((0, 0), (0, 0), (0, 0), (BLOCK, 0), (0, 0)))
        prev = tp[..., :Lp, :].reshape(B, H, dilation, nb, BLOCK, Dh)
        cur = t.reshape(B, H, dilation, nb, BLOCK, Dh)
        return jnp.concatenate([prev, cur], axis=-2)

    kb, vb = band(ks), band(vs)
    a = jnp.arange(BLOCK)[:, None]
    cidx = jnp.arange(2 * BLOCK)[None, :]
    rel = a + BLOCK - cidx
    in_band = (rel >= 0) & (rel <= steps)
    key_pos = jnp.arange(nb)[:, None, None] * BLOCK - BLOCK + cidx[None]
    mask = in_band[None] & (key_pos >= 0)

    s = jnp.einsum('bhrnqd,bhrnkd->bhrnqk', qb, kb).astype(jnp.float32) * (HEAD_DIM ** -0.5)
    s = jnp.where(mask, s, -jnp.inf)
    lse = jax.nn.logsumexp(s, axis=-1)
    pr = jnp.exp(s - lse[..., None]).astype(v.dtype)
    o = jnp.einsum('bhrnqk,bhrnkd->bhrnqd', pr, vb)
    o = o.reshape(B, H, dilation, Lp, Dh)[:, :, :, :L]
    o = o.transpose(0, 3, 2, 1, 4).reshape(B, S, H, Dh)
    lse = lse.reshape(B, H, dilation, Lp)[:, :, :, :L].transpose(0, 3, 2, 1).reshape(B, S, H)
    return o, lse


def token_mixing(u, cos, sin, w_in, w_pool, pool_scale, w_pool_branch, w_attn_branch, w_out):
    B, S, _ = u.shape
    proj = u @ w_in
    cuts = [POOL_WIDTH, POOL_WIDTH + ATTN_WIDTH, POOL_WIDTH + 2 * ATTN_WIDTH,
            POOL_WIDTH + 3 * ATTN_WIDTH]
    p, q, k, v, gate_logits = jnp.split(proj, cuts, axis=-1)

    y_pool = multiscale_pool(p, w_pool, pool_scale)

    q = apply_rope(q.reshape(B, S, N_ATTN_HEADS, HEAD_DIM), cos, sin)
    k = apply_rope(k.reshape(B, S, N_ATTN_HEADS, HEAD_DIM), cos, sin)
    v = v.reshape(B, S, N_ATTN_HEADS, HEAD_DIM)
    outs, lses = [], []
    for gi, (window, dilation) in enumerate(ATTN_PATTERNS):
        hs = slice(gi * HEADS_PER_GROUP, (gi + 1) * HEADS_PER_GROUP)
        o, lse = dilated_window_attention(q[:, :, hs], k[:, :, hs], v[:, :, hs],
                                          window // dilation, dilation)
        outs.append(o)
        lses.append(lse)
    wts = jax.nn.softmax(jnp.stack(lses, axis=0), axis=0)
    y_attn = jnp.einsum('gbsh,gbshd->bshd', wts.astype(v.dtype), jnp.stack(outs, axis=0))
    y_attn = y_attn.reshape(B, S, ATTN_OUT_WIDTH)

    gates = jax.nn.sigmoid(gate_logits.astype(jnp.float32)).astype(u.dtype)
    g_pool, g_attn = jnp.split(gates, N_BRANCHES, axis=-1)
    merged = g_pool * (y_pool @ w_pool_branch) + g_attn * (y_attn @ w_attn_branch)
    return merged @ w_out


def _fwd_setup_inputs(seed: int = 0) -> dict:
    key = jax.random.key(seed)
    ks = jax.random.split(key, 20)
    f32 = jnp.float32

    def lin(k, shape, fan_in, scale=1.0):
        return jax.random.normal(k, shape, f32) * (scale * fan_in ** -0.5)

    def gain(k, shape):
        return 1.0 + 0.05 * jax.random.normal(k, shape, f32)

    x = jax.random.normal(ks[0], (BATCH, SEQ, D_MODEL), f32)
    c = jax.random.normal(ks[1], (BATCH, D_MODEL), f32)
    offset = jax.random.randint(ks[2], (BATCH, 1), 0, 1024, dtype=jnp.int32)
    positions = (jnp.arange(SEQ, dtype=jnp.int32)[None, :] + offset).astype(jnp.int32)
    return {
        "x": x,
        "c": c,
        "positions": positions,
        "w_ada": lin(ks[3], (DEPTH, D_MODEL, N_MOD * D_MODEL), D_MODEL, 0.5),
        "b_ada": 0.02 * jax.random.normal(ks[4], (DEPTH, N_MOD * D_MODEL), f32),
        "g_norm_ffn1": gain(ks[5], (DEPTH, D_MODEL)),
        "w_ffn1_in": lin(ks[6], (DEPTH, D_MODEL, 2 * D_FF), D_MODEL),
        "w_ffn1_out": lin(ks[7], (DEPTH, D_FF, D_MODEL), D_FF),
        "g_norm_mix": gain(ks[8], (DEPTH, D_MODEL)),
        "w_in": lin(ks[9], (DEPTH, D_MODEL, IN_WIDTH), D_MODEL),
        "w_pool": lin(ks[10], (DEPTH, POOL_GROUPS, POOL_GROUP_DIM, POOL_GROUP_DIM), POOL_GROUP_DIM),
        "pool_scale": 1.0 + 0.1 * jax.random.normal(ks[11], (DEPTH, POOL_WIDTH), f32),
        "w_pool_branch": lin(ks[12], (DEPTH, POOL_WIDTH, D_MODEL), POOL_WIDTH),
        "w_attn_branch": lin(ks[13], (DEPTH, ATTN_OUT_WIDTH, D_MODEL), ATTN_OUT_WIDTH),
        "w_out": lin(ks[14], (DEPTH, D_MODEL, D_MODEL), D_MODEL),
        "g_norm_ffn2": gain(ks[15], (DEPTH, D_MODEL)),
        "w_ffn2_in": lin(ks[16], (DEPTH, D_MODEL, 2 * D_FF), D_MODEL),
        "w_ffn2_out": lin(ks[17], (DEPTH, D_FF, D_MODEL), D_FF),
        "g_final": gain(ks[18], (D_MODEL,)),
    }


def _fwd_reference(x, c, positions, w_ada, b_ada, g_norm_ffn1, w_ffn1_in, w_ffn1_out,
              g_norm_mix, w_in, w_pool, pool_scale, w_pool_branch, w_attn_branch, w_out,
              g_norm_ffn2, w_ffn2_in, w_ffn2_out, g_final):
    cos, sin = rope_tables(positions, x.dtype)
    cond = jax.nn.silu(c)
    h = x
    for l in range(DEPTH):
        mod = cond @ w_ada[l] + b_ada[l]
        sh1, sc1, gt1, sh2, sc2, gt2, sh3, sc3, gt3 = [
            m[:, None, :] for m in jnp.split(mod, N_MOD, axis=-1)]
        u = modulate(rmsnorm(h, g_norm_ffn1[l]), sh1, sc1)
        h = h + 0.5 * gt1 * swiglu(u, w_ffn1_in[l], w_ffn1_out[l])
        u = modulate(rmsnorm(h, g_norm_mix[l]), sh2, sc2)
        h = h + gt2 * token_mixing(u, cos, sin, w_in[l], w_pool[l], pool_scale[l],
                                   w_pool_branch[l], w_attn_branch[l], w_out[l])
        u = modulate(rmsnorm(h, g_norm_ffn2[l]), sh3, sc3)
        h = h + 0.5 * gt3 * swiglu(u, w_ffn2_in[l], w_ffn2_out[l])
    return rmsnorm(h, g_final)


import jax as _jax
import jax.numpy as _jnp

TWIN_FORMAT = 'train_step'
FWD_PARAMS = ['x', 'c', 'positions', 'w_ada', 'b_ada', 'g_norm_ffn1', 'w_ffn1_in', 'w_ffn1_out', 'g_norm_mix', 'w_in', 'w_pool', 'pool_scale', 'w_pool_branch', 'w_attn_branch', 'w_out', 'g_norm_ffn2', 'w_ffn2_in', 'w_ffn2_out', 'g_final']
TWIN_WEIGHTS = ['w_ada', 'b_ada', 'g_norm_ffn1', 'w_ffn1_in', 'w_ffn1_out', 'g_norm_mix', 'w_in', 'w_pool', 'pool_scale', 'w_pool_branch', 'w_attn_branch', 'w_out', 'g_norm_ffn2', 'w_ffn2_in', 'w_ffn2_out', 'g_final']
TWIN_DIFF_INPUT = 'x'
TWIN_INPUTS = ['x', 'c', 'positions', 'w_ada', 'b_ada', 'g_norm_ffn1', 'w_ffn1_in', 'w_ffn1_out', 'g_norm_mix', 'w_in', 'w_pool', 'pool_scale', 'w_pool_branch', 'w_attn_branch', 'w_out', 'g_norm_ffn2', 'w_ffn2_in', 'w_ffn2_out', 'g_final', 'loss_target', 'm_w_ada', 'm_b_ada', 'm_g_norm_ffn1', 'm_w_ffn1_in', 'm_w_ffn1_out', 'm_g_norm_mix', 'm_w_in', 'm_w_pool', 'm_pool_scale', 'm_w_pool_branch', 'm_w_attn_branch', 'm_w_out', 'm_g_norm_ffn2', 'm_w_ffn2_in', 'm_w_ffn2_out', 'm_g_final', 'v_w_ada', 'v_b_ada', 'v_g_norm_ffn1', 'v_w_ffn1_in', 'v_w_ffn1_out', 'v_g_norm_mix', 'v_w_in', 'v_w_pool', 'v_pool_scale', 'v_w_pool_branch', 'v_w_attn_branch', 'v_w_out', 'v_g_norm_ffn2', 'v_w_ffn2_in', 'v_w_ffn2_out', 'v_g_final']
TWIN_OUTPUTS = ['loss', 'grad_x', 'grad_w_ada', 'grad_b_ada', 'grad_g_norm_ffn1', 'grad_w_ffn1_in', 'grad_w_ffn1_out', 'grad_g_norm_mix', 'grad_w_in', 'grad_w_pool', 'grad_pool_scale', 'grad_w_pool_branch', 'grad_w_attn_branch', 'grad_w_out', 'grad_g_norm_ffn2', 'grad_w_ffn2_in', 'grad_w_ffn2_out', 'grad_g_final', 'delta_w_ada', 'delta_b_ada', 'delta_g_norm_ffn1', 'delta_w_ffn1_in', 'delta_w_ffn1_out', 'delta_g_norm_mix', 'delta_w_in', 'delta_w_pool', 'delta_pool_scale', 'delta_w_pool_branch', 'delta_w_attn_branch', 'delta_w_out', 'delta_g_norm_ffn2', 'delta_w_ffn2_in', 'delta_w_ffn2_out', 'delta_g_final', 'new_m_w_ada', 'new_m_b_ada', 'new_m_g_norm_ffn1', 'new_m_w_ffn1_in', 'new_m_w_ffn1_out', 'new_m_g_norm_mix', 'new_m_w_in', 'new_m_w_pool', 'new_m_pool_scale', 'new_m_w_pool_branch', 'new_m_w_attn_branch', 'new_m_w_out', 'new_m_g_norm_ffn2', 'new_m_w_ffn2_in', 'new_m_w_ffn2_out', 'new_m_g_final', 'new_v_w_ada', 'new_v_b_ada', 'new_v_g_norm_ffn1', 'new_v_w_ffn1_in', 'new_v_w_ffn1_out', 'new_v_g_norm_mix', 'new_v_w_in', 'new_v_w_pool', 'new_v_pool_scale', 'new_v_w_pool_branch', 'new_v_w_attn_branch', 'new_v_w_out', 'new_v_g_norm_ffn2', 'new_v_w_ffn2_in', 'new_v_w_ffn2_out', 'new_v_g_final']
TWIN_LEAF_KINDS = {'loss': 'loss', 'grad_x': 'grad_x', 'grad_w_ada': 'grad_w', 'grad_b_ada': 'grad_w', 'grad_g_norm_ffn1': 'grad_w', 'grad_w_ffn1_in': 'grad_w', 'grad_w_ffn1_out': 'grad_w', 'grad_g_norm_mix': 'grad_w', 'grad_w_in': 'grad_w', 'grad_w_pool': 'grad_w', 'grad_pool_scale': 'grad_w', 'grad_w_pool_branch': 'grad_w', 'grad_w_attn_branch': 'grad_w', 'grad_w_out': 'grad_w', 'grad_g_norm_ffn2': 'grad_w', 'grad_w_ffn2_in': 'grad_w', 'grad_w_ffn2_out': 'grad_w', 'grad_g_final': 'grad_w', 'delta_w_ada': 'delta_w', 'delta_b_ada': 'delta_w', 'delta_g_norm_ffn1': 'delta_w', 'delta_w_ffn1_in': 'delta_w', 'delta_w_ffn1_out': 'delta_w', 'delta_g_norm_mix': 'delta_w', 'delta_w_in': 'delta_w', 'delta_w_pool': 'delta_w', 'delta_pool_scale': 'delta_w', 'delta_w_pool_branch': 'delta_w', 'delta_w_attn_branch': 'delta_w', 'delta_w_out': 'delta_w', 'delta_g_norm_ffn2': 'delta_w', 'delta_w_ffn2_in': 'delta_w', 'delta_w_ffn2_out': 'delta_w', 'delta_g_final': 'delta_w', 'new_m_w_ada': 'new_m', 'new_m_b_ada': 'new_m', 'new_m_g_norm_ffn1': 'new_m', 'new_m_w_ffn1_in': 'new_m', 'new_m_w_ffn1_out': 'new_m', 'new_m_g_norm_mix': 'new_m', 'new_m_w_in': 'new_m', 'new_m_w_pool': 'new_m', 'new_m_pool_scale': 'new_m', 'new_m_w_pool_branch': 'new_m', 'new_m_w_attn_branch': 'new_m', 'new_m_w_out': 'new_m', 'new_m_g_norm_ffn2': 'new_m', 'new_m_w_ffn2_in': 'new_m', 'new_m_w_ffn2_out': 'new_m', 'new_m_g_final': 'new_m', 'new_v_w_ada': 'new_v', 'new_v_b_ada': 'new_v', 'new_v_g_norm_ffn1': 'new_v', 'new_v_w_ffn1_in': 'new_v', 'new_v_w_ffn1_out': 'new_v', 'new_v_g_norm_mix': 'new_v', 'new_v_w_in': 'new_v', 'new_v_w_pool': 'new_v', 'new_v_pool_scale': 'new_v', 'new_v_w_pool_branch': 'new_v', 'new_v_w_attn_branch': 'new_v', 'new_v_w_out': 'new_v', 'new_v_g_norm_ffn2': 'new_v', 'new_v_w_ffn2_in': 'new_v', 'new_v_w_ffn2_out': 'new_v', 'new_v_g_final': 'new_v'}


def _forward(args):
    return _fwd_reference(*[args[k] for k in FWD_PARAMS])


def _output_shape():
    def fwd():
        inp = _fwd_setup_inputs(0)
        return _fwd_reference(*[inp[k] for k in FWD_PARAMS])
    out = _jax.eval_shape(fwd)
    return out.shape, out.dtype

N_MICROBATCH = 1
ADAM_LR = 0.001
ADAM_B1 = 0.9
ADAM_B2 = 0.999
ADAM_EPS = 1e-08
ADAM_WD = 0.01
ADAM_STEP = 10
PER_EXAMPLE_BATCH_AXIS = {'x': 0, 'c': 0, 'positions': 0, 'loss_target': 0}
SHARED_INPUTS = []
_WEIGHT_DTYPES = {'w_ada': _jnp.float32, 'b_ada': _jnp.float32, 'g_norm_ffn1': _jnp.float32, 'w_ffn1_in': _jnp.float32, 'w_ffn1_out': _jnp.float32, 'g_norm_mix': _jnp.float32, 'w_in': _jnp.float32, 'w_pool': _jnp.float32, 'pool_scale': _jnp.float32, 'w_pool_branch': _jnp.float32, 'w_attn_branch': _jnp.float32, 'w_out': _jnp.float32, 'g_norm_ffn2': _jnp.float32, 'w_ffn2_in': _jnp.float32, 'w_ffn2_out': _jnp.float32, 'g_final': _jnp.float32}
MOMENT_SCALE = {'w_ada': 4.726074e-02, 'b_ada': 8.856505e-02, 'g_norm_ffn1': 3.843520e-02, 'w_ffn1_in': 1.680370e-02, 'w_ffn1_out': 2.744572e-02, 'g_norm_mix': 4.547655e-02, 'w_in': 2.152662e-02, 'w_pool': 7.735252e-02, 'pool_scale': 8.634312e-02, 'w_pool_branch': 3.885883e-02, 'w_attn_branch': 1.344362e-02, 'w_out': 4.019908e-02, 'g_norm_ffn2': 3.820433e-02, 'w_ffn2_in': 1.642118e-02, 'w_ffn2_out': 2.689241e-02, 'g_final': 6.410876e+01}


def _to_microbatches(a, axis):
    t = _jnp.moveaxis(a, axis, 0)
    t = t.reshape((N_MICROBATCH, t.shape[0] // N_MICROBATCH) + t.shape[1:])
    return _jnp.moveaxis(t, 1, axis + 1)


def setup_inputs(seed: int = 0) -> dict:
    inp = _fwd_setup_inputs(seed)
    key = _jax.random.fold_in(_jax.random.key(seed), 7919)
    shape, _ = _output_shape()
    out = dict(inp)
    out["loss_target"] = _jax.random.normal(_jax.random.fold_in(key, 0), shape, _jnp.float32)
    for i, name in enumerate(TWIN_WEIGHTS):
        w = inp[name].astype(_jnp.float32)
        if MOMENT_SCALE is None:
            s = _jnp.sqrt(_jnp.mean(_jnp.square(w)) + 1e-30)
        else:
            s = MOMENT_SCALE[name]
        km, kv = _jax.random.split(_jax.random.fold_in(key, i + 1))
        out[name] = w
        out["m_" + name] = s * _jax.random.normal(km, w.shape, _jnp.float32)
        out["v_" + name] = (s * s) * _jax.random.uniform(kv, w.shape, _jnp.float32, 0.5, 1.5)
    if N_MICROBATCH > 1:
        for name, axis in PER_EXAMPLE_BATCH_AXIS.items():
            out[name] = _to_microbatches(out[name], axis)
    return {'x': out['x'], 'c': out['c'], 'positions': out['positions'], 'w_ada': out['w_ada'], 'b_ada': out['b_ada'], 'g_norm_ffn1': out['g_norm_ffn1'], 'w_ffn1_in': out['w_ffn1_in'], 'w_ffn1_out': out['w_ffn1_out'], 'g_norm_mix': out['g_norm_mix'], 'w_in': out['w_in'], 'w_pool': out['w_pool'], 'pool_scale': out['pool_scale'], 'w_pool_branch': out['w_pool_branch'], 'w_attn_branch': out['w_attn_branch'], 'w_out': out['w_out'], 'g_norm_ffn2': out['g_norm_ffn2'], 'w_ffn2_in': out['w_ffn2_in'], 'w_ffn2_out': out['w_ffn2_out'], 'g_final': out['g_final'], 'loss_target': out['loss_target'], 'm_w_ada': out['m_w_ada'], 'm_b_ada': out['m_b_ada'], 'm_g_norm_ffn1': out['m_g_norm_ffn1'], 'm_w_ffn1_in': out['m_w_ffn1_in'], 'm_w_ffn1_out': out['m_w_ffn1_out'], 'm_g_norm_mix': out['m_g_norm_mix'], 'm_w_in': out['m_w_in'], 'm_w_pool': out['m_w_pool'], 'm_pool_scale': out['m_pool_scale'], 'm_w_pool_branch': out['m_w_pool_branch'], 'm_w_attn_branch': out['m_w_attn_branch'], 'm_w_out': out['m_w_out'], 'm_g_norm_ffn2': out['m_g_norm_ffn2'], 'm_w_ffn2_in': out['m_w_ffn2_in'], 'm_w_ffn2_out': out['m_w_ffn2_out'], 'm_g_final': out['m_g_final'], 'v_w_ada': out['v_w_ada'], 'v_b_ada': out['v_b_ada'], 'v_g_norm_ffn1': out['v_g_norm_ffn1'], 'v_w_ffn1_in': out['v_w_ffn1_in'], 'v_w_ffn1_out': out['v_w_ffn1_out'], 'v_g_norm_mix': out['v_g_norm_mix'], 'v_w_in': out['v_w_in'], 'v_w_pool': out['v_w_pool'], 'v_pool_scale': out['v_pool_scale'], 'v_w_pool_branch': out['v_w_pool_branch'], 'v_w_attn_branch': out['v_w_attn_branch'], 'v_w_out': out['v_w_out'], 'v_g_norm_ffn2': out['v_g_norm_ffn2'], 'v_w_ffn2_in': out['v_w_ffn2_in'], 'v_w_ffn2_out': out['v_w_ffn2_out'], 'v_g_final': out['v_g_final']}


def _loss(weights, diff, rest, loss_target):
    with _jax.named_scope("forward"):
        args = {**rest, TWIN_DIFF_INPUT: diff, **{k: w.astype(_WEIGHT_DTYPES[k]) for k, w in weights.items()}}
        y = _forward(args)
    with _jax.named_scope("loss_head"):
        err = _jnp.square(y.astype(_jnp.float32) - loss_target)
        return 0.5 * _jnp.sum(_jnp.mean(err, axis=-1)) if err.ndim else 0.5 * err


def _adamw(w, g, m, v):
    m = ADAM_B1 * m + (1.0 - ADAM_B1) * g
    v = ADAM_B2 * v + (1.0 - ADAM_B2) * _jnp.square(g)
    m_hat = m / (1.0 - ADAM_B1 ** ADAM_STEP)
    v_hat = v / (1.0 - ADAM_B2 ** ADAM_STEP)
    delta = -ADAM_LR * (m_hat / (_jnp.sqrt(v_hat) + ADAM_EPS) + ADAM_WD * w)
    return delta, m, v


def reference(x, c, positions, w_ada, b_ada, g_norm_ffn1, w_ffn1_in, w_ffn1_out, g_norm_mix, w_in, w_pool, pool_scale, w_pool_branch, w_attn_branch, w_out, g_norm_ffn2, w_ffn2_in, w_ffn2_out, g_final, loss_target, m_w_ada, m_b_ada, m_g_norm_ffn1, m_w_ffn1_in, m_w_ffn1_out, m_g_norm_mix, m_w_in, m_w_pool, m_pool_scale, m_w_pool_branch, m_w_attn_branch, m_w_out, m_g_norm_ffn2, m_w_ffn2_in, m_w_ffn2_out, m_g_final, v_w_ada, v_b_ada, v_g_norm_ffn1, v_w_ffn1_in, v_w_ffn1_out, v_g_norm_mix, v_w_in, v_w_pool, v_pool_scale, v_w_pool_branch, v_w_attn_branch, v_w_out, v_g_norm_ffn2, v_w_ffn2_in, v_w_ffn2_out, v_g_final):
    given = dict(x=x, c=c, positions=positions, w_ada=w_ada, b_ada=b_ada, g_norm_ffn1=g_norm_ffn1, w_ffn1_in=w_ffn1_in, w_ffn1_out=w_ffn1_out, g_norm_mix=g_norm_mix, w_in=w_in, w_pool=w_pool, pool_scale=pool_scale, w_pool_branch=w_pool_branch, w_attn_branch=w_attn_branch, w_out=w_out, g_norm_ffn2=g_norm_ffn2, w_ffn2_in=w_ffn2_in, w_ffn2_out=w_ffn2_out, g_final=g_final, loss_target=loss_target, m_w_ada=m_w_ada, m_b_ada=m_b_ada, m_g_norm_ffn1=m_g_norm_ffn1, m_w_ffn1_in=m_w_ffn1_in, m_w_ffn1_out=m_w_ffn1_out, m_g_norm_mix=m_g_norm_mix, m_w_in=m_w_in, m_w_pool=m_w_pool, m_pool_scale=m_pool_scale, m_w_pool_branch=m_w_pool_branch, m_w_attn_branch=m_w_attn_branch, m_w_out=m_w_out, m_g_norm_ffn2=m_g_norm_ffn2, m_w_ffn2_in=m_w_ffn2_in, m_w_ffn2_out=m_w_ffn2_out, m_g_final=m_g_final, v_w_ada=v_w_ada, v_b_ada=v_b_ada, v_g_norm_ffn1=v_g_norm_ffn1, v_w_ffn1_in=v_w_ffn1_in, v_w_ffn1_out=v_w_ffn1_out, v_g_norm_mix=v_g_norm_mix, v_w_in=v_w_in, v_w_pool=v_w_pool, v_pool_scale=v_pool_scale, v_w_pool_branch=v_w_pool_branch, v_w_attn_branch=v_w_attn_branch, v_w_out=v_w_out, v_g_norm_ffn2=v_g_norm_ffn2, v_w_ffn2_in=v_w_ffn2_in, v_w_ffn2_out=v_w_ffn2_out, v_g_final=v_g_final)
    weights = {n: given[n] for n in TWIN_WEIGHTS}
    shared = {n: given[n] for n in SHARED_INPUTS}
    per_example = {n: given[n] for n in ['x', 'c', 'positions']}
    grad_fn = _jax.value_and_grad(_loss, argnums=(0, 1))

    def one_microbatch(ex, loss_target):
        ex = dict(ex)
        diff = ex.pop(TWIN_DIFF_INPUT)
        return grad_fn(weights, diff, {**shared, **ex}, loss_target)

    if N_MICROBATCH == 1:
        loss, (grad_w, grad_x) = one_microbatch(per_example, given["loss_target"])
    else:
        def body(carry, xs):
            loss_sum, grad_sum = carry
            l_k, (gw_k, gx_k) = one_microbatch(xs[0], xs[1])
            with _jax.named_scope("update"):
                return (loss_sum + l_k, _jax.tree.map(_jnp.add, grad_sum, gw_k)), gx_k

        init = (_jnp.zeros((), _jnp.float32), _jax.tree.map(_jnp.zeros_like, weights))
        (loss, grad_w), grad_x = _jax.lax.scan(body, init, (per_example, given["loss_target"]))
    with _jax.named_scope("update"):
        delta_w, new_m, new_v = {}, {}, {}
        for n in TWIN_WEIGHTS:
            delta_w[n], new_m[n], new_v[n] = _adamw(weights[n], grad_w[n], given["m_" + n], given["v_" + n])
    return (loss, grad_x, *[grad_w[n] for n in TWIN_WEIGHTS], *[delta_w[n] for n in TWIN_WEIGHTS],
            *[new_m[n] for n in TWIN_WEIGHTS], *[new_v[n] for n in TWIN_WEIGHTS])
```

```python
import functools

import jax
import jax.numpy as jnp
from jax import lax
from jax.experimental import pallas as pl
from jax.experimental.pallas import tpu as pltpu

F32 = jnp.float32
BF16 = jnp.bfloat16
MESH = pl.DeviceIdType.MESH
ANY = pl.BlockSpec(memory_space=pl.ANY)

N_DEV = 8
EPS = 1e-6
HEAD_DIM = 64
HEADS = 4
GW = HEADS * HEAD_DIM
DILATIONS = (1, 4, 16)
BAND = 128
QB = 128
POOL_WINDOWS = (2, 4, 8, 16)
HALO = 16
ROPE_THETA = 10000.0

ADAM_LR = 0.001
ADAM_B1 = 0.9
ADAM_B2 = 0.999
ADAM_EPS = 1e-08
ADAM_WD = 0.01
ADAM_STEP = 10

VMEM_LIMIT = 56 * 1024 * 1024
TS = 256
FC = 256

NT = (((1,), (1,)), ((), ()))
TN = (((0,), (0,)), ((), ()))


def _params(**kw):
    return pltpu.CompilerParams(vmem_limit_bytes=VMEM_LIMIT, **kw)


def _dot(a, b):
    return jnp.dot(a, b, preferred_element_type=F32)


def _dot_nt(a, b):
    return lax.dot_general(a, b, NT, preferred_element_type=F32)


def _dot_tn(a, b):
    return lax.dot_general(a, b, TN, preferred_element_type=F32)


def _load_weights(pairs, sem):
    @pl.when(pl.program_id(0) == 0)
    def _():
        copies = [pltpu.make_async_copy(src, dst, sem.at[i]) for i, (src, dst) in enumerate(pairs)]
        for cp in copies:
            cp.start()
        for cp in copies:
            cp.wait()


def _norm_mod(x, g, sc, sh):
    r = lax.rsqrt(jnp.mean(x * x, axis=-1, keepdims=True) + EPS)
    xn = x * r
    y = xn * g
    return r, xn, y, y * (1.0 + sc) + sh


def _norm_mod_bwd(du, r, xn, y, g, sc):
    dsh = jnp.sum(du, axis=0, keepdims=True)
    dsc = jnp.sum(du * y, axis=0, keepdims=True)
    dy = du * (1.0 + sc)
    dg = jnp.sum(dy * xn, axis=0, keepdims=True)
    dxn = dy * g
    dx = r * (dxn - xn * jnp.mean(dxn * xn, axis=-1, keepdims=True))
    return dx, dsh, dsc, dg


def _row_tile(ts, width):
    return pl.BlockSpec((ts, width), lambda i: (i, 0))


def _const(shape):
    return pl.BlockSpec(shape, lambda *_: (0,) * len(shape))


def _ffn_fwd(h, vec, win, wout, name):
    S, D = h.shape
    nsh, _, fs = win.shape
    nch = nsh // 2

    def body(h_ref, vec_ref, win_hbm, wout_hbm, hn_ref, u_ref, ab_ref, act_ref, f_ref, win_v, wout_v, sem):
        _load_weights([(win_hbm, win_v), (wout_hbm, wout_v)], sem)
        x = h_ref[...]
        g, sh, sc, gt = (vec_ref[k:k + 1, :] for k in range(4))
        _, _, _, u = _norm_mod(x, g, sc, sh)
        ub = u.astype(BF16)
        u_ref[...] = ub
        acc = jnp.zeros((TS, D), F32)
        for j in range(nch):
            a = _dot(ub, win_v[j])
            b = _dot(ub, win_v[nch + j])
            act = ((a * jax.nn.sigmoid(a)) * b).astype(BF16)
            ab_ref[j] = a.astype(BF16)
            ab_ref[nch + j] = b.astype(BF16)
            act_ref[j] = act
            acc = acc + _dot(act, wout_v[j])
        f_ref[...] = acc
        hn_ref[...] = x + (0.5 * gt) * acc

    return pl.pallas_call(
        body, name=name, grid=(S // TS,),
        in_specs=[_row_tile(TS, D), _const((8, D)), ANY, ANY],
        out_specs=[_row_tile(TS, D), _row_tile(TS, D), pl.BlockSpec((nsh, TS, fs), lambda i: (0, i, 0)),
                   pl.BlockSpec((nch, TS, fs), lambda i: (0, i, 0)), _row_tile(TS, D)],
        out_shape=[jax.ShapeDtypeStruct((S, D), F32), jax.ShapeDtypeStruct((S, D), BF16),
                   jax.ShapeDtypeStruct((nsh, S, fs), BF16), jax.ShapeDtypeStruct((nch, S, fs), BF16),
                   jax.ShapeDtypeStruct((S, D), F32)],
        scratch_shapes=[pltpu.VMEM(win.shape, BF16), pltpu.VMEM(wout.shape, BF16), pltpu.SemaphoreType.DMA((2,))],
        compiler_params=_params(),
    )(h, vec, win, wout)


def _ffn_bwd(dh, h, f, ab, vec, win, wout, name):
    S, D = h.shape
    nsh, _, fs = win.shape
    nch = nsh // 2

    def body(dh_ref, h_ref, f_ref, ab_ref, vec_ref, win_hbm, wout_hbm,
             dhp_ref, dab_ref, df_ref, red_ref, win_v, wout_v, sem):
        _load_weights([(win_hbm, win_v), (wout_hbm, wout_v)], sem)

        @pl.when(pl.program_id(0) == 0)
        def _():
            red_ref[...] = jnp.zeros_like(red_ref)

        dh_v = dh_ref[...]
        x = h_ref[...]
        g, sh, sc, gt = (vec_ref[k:k + 1, :] for k in range(4))
        dgt = jnp.sum((0.5 * f_ref[...]) * dh_v, axis=0, keepdims=True)
        dfb = ((0.5 * gt) * dh_v).astype(BF16)
        df_ref[...] = dfb
        du = jnp.zeros((TS, D), F32)
        for j in range(nch):
            dact = _dot_nt(dfb, wout_v[j])
            av = ab_ref[j].astype(F32)
            bv = ab_ref[nch + j].astype(F32)
            sg = jax.nn.sigmoid(av)
            da = (dact * bv * (sg * (1.0 + av * (1.0 - sg)))).astype(BF16)
            db = (dact * (av * sg)).astype(BF16)
            dab_ref[j] = da
            dab_ref[nch + j] = db
            du = du + _dot_nt(da, win_v[j]) + _dot_nt(db, win_v[nch + j])
        r, xn, y, _ = _norm_mod(x, g, sc, sh)
        dx, dsh, dsc, dg = _norm_mod_bwd(du, r, xn, y, g, sc)
        dhp_ref[...] = dh_v + dx
        red_ref[0:1, :] += dgt
        red_ref[1:2, :] += dsh
        red_ref[2:3, :] += dsc
        red_ref[3:4, :] += dg

    ab_spec = pl.BlockSpec((nsh, TS, fs), lambda i: (0, i, 0))
    return pl.pallas_call(
        body, name=name, grid=(S // TS,),
        in_specs=[_row_tile(TS, D), _row_tile(TS, D), _row_tile(TS, D), ab_spec, _const((8, D)), ANY, ANY],
        out_specs=[_row_tile(TS, D), ab_spec, _row_tile(TS, D), _const((8, D))],
        out_shape=[jax.ShapeDtypeStruct((S, D), F32), jax.ShapeDtypeStruct((nsh, S, fs), BF16),
                   jax.ShapeDtypeStruct((S, D), BF16), jax.ShapeDtypeStruct((8, D), F32)],
        scratch_shapes=[pltpu.VMEM(win.shape, BF16), pltpu.VMEM(wout.shape, BF16), pltpu.SemaphoreType.DMA((2,))],
        compiler_params=_params(),
    )(dh, h, f, ab, vec, win, wout)


def _wgrad(x, y, name, tn=None, ts=512):
    xb, yb = x.ndim == 3, y.ndim == 3
    nb = x.shape[0] if xb else (y.shape[0] if yb else 0)
    S, M = x.shape[-2:]
    N = y.shape[-1]
    tn = tn or N
    nk = S // ts

    def body(x_ref, y_ref, o_ref, acc):
        k = pl.program_id(2)

        @pl.when(k == 0)
        def _():
            acc[...] = jnp.zeros_like(acc)

        acc[...] += _dot_tn(x_ref[...], y_ref[...])

        @pl.when(k == nk - 1)
        def _():
            o_ref[...] = acc[...].astype(BF16)

    x_spec = (pl.BlockSpec((None, ts, M), lambda b, j, k: (b, k, 0)) if xb
              else pl.BlockSpec((ts, M), lambda b, j, k: (k, 0)))
    y_spec = (pl.BlockSpec((None, ts, tn), lambda b, j, k: (b, k, j)) if yb
              else pl.BlockSpec((ts, tn), lambda b, j, k: (k, j)))
    if nb:
        o_spec, o_shape = pl.BlockSpec((None, M, tn), lambda b, j, k: (b, 0, j)), (nb, M, N)
    else:
        o_spec, o_shape = pl.BlockSpec((M, tn), lambda b, j, k: (0, j)), (M, N)
    return pl.pallas_call(
        body, name=name, grid=(max(nb, 1), N // tn, nk),
        in_specs=[x_spec, y_spec], out_specs=o_spec, out_shape=jax.ShapeDtypeStruct(o_shape, BF16),
        scratch_shapes=[pltpu.VMEM((M, tn), F32)],
        compiler_params=_params(),
    )(x, y)


P_OFF, Q_OFF, K_OFF, V_OFF, G_OFF = 0, 256, 1024, 1792, 2560
IN_WIDTH = 4608


def _first_half_mask(ts):
    lane = lax.broadcasted_iota(jnp.int32, (ts, 128), 1)
    return (lane % HEAD_DIM) < (HEAD_DIM // 2)


def _rope(t, cos, sin_signed, first, sign):
    partner = jnp.where(first, pltpu.roll(t, 96, 1), pltpu.roll(t, 32, 1))
    return t * cos + sign * (partner * sin_signed)


def _mix_in_fwd(h, vec, cos, sin, win, name):
    S, D = h.shape
    grp = jax.ShapeDtypeStruct((S, GW), BF16)

    def body(h_ref, vec_ref, cos_ref, sin_ref, win_hbm, u_ref, p_ref, gates_ref, *rest):
        qkv_refs, (win_v, sem) = rest[:9], rest[9:]
        _load_weights([(win_hbm, win_v)], sem)
        g, sh, sc = (vec_ref[k:k + 1, :] for k in range(3))
        _, _, _, u = _norm_mod(h_ref[...], g, sc, sh)
        ub = u.astype(BF16)
        u_ref[...] = ub
        p_ref[...] = _dot(ub, win_v[:, P_OFF:Q_OFF])
        cosv, sinv = cos_ref[...], sin_ref[...]
        first = _first_half_mask(TS)
        for which, off in enumerate((Q_OFF, K_OFF, V_OFF)):
            t = _dot(ub, win_v[:, off:off + 3 * GW])
            for gi in range(3):
                for half in range(2):
                    c0 = gi * GW + half * 128
                    piece = t[:, c0:c0 + 128]
                    if which < 2:
                        piece = _rope(piece, cosv, sinv, first, 1.0)
                    qkv_refs[which * 3 + gi][:, half * 128:(half + 1) * 128] = piece.astype(BF16)
        gates_ref[...] = jax.nn.sigmoid(_dot(ub, win_v[:, G_OFF:IN_WIDTH]))

    return pl.pallas_call(
        body, name=name, grid=(S // TS,),
        in_specs=[_row_tile(TS, D), _const((8, D)), _row_tile(TS, 128), _row_tile(TS, 128), ANY],
        out_specs=[_row_tile(TS, D), _row_tile(TS, GW), _row_tile(TS, 2 * D)] + [_row_tile(TS, GW)] * 9,
        out_shape=[jax.ShapeDtypeStruct((S, D), BF16), jax.ShapeDtypeStruct((S, GW), F32),
                   jax.ShapeDtypeStruct((S, 2 * D), F32)] + [grp] * 9,
        scratch_shapes=[pltpu.VMEM((D, IN_WIDTH), BF16), pltpu.SemaphoreType.DMA((1,))],
        compiler_params=_params(),
    )(h, vec, cos, sin, win)


def _mix_in_bwd(dh, h, vec, cos, sin, dp, dqkv, dgl, win, name):
    S, D = h.shape

    def body(dh_ref, h_ref, vec_ref, cos_ref, sin_ref, dp_ref, *rest):
        dqkv_refs = rest[:9]
        dgl_ref, win_hbm, dhp_ref, dproj_ref, red_ref, win_v, sem = rest[9:]
        _load_weights([(win_hbm, win_v)], sem)

        @pl.when(pl.program_id(0) == 0)
        def _():
            red_ref[...] = jnp.zeros_like(red_ref)

        cosv, sinv = cos_ref[...], sin_ref[...]
        first = _first_half_mask(TS)
        dproj_ref[:, P_OFF:Q_OFF] = dp_ref[...].astype(BF16)
        for which, off in enumerate((Q_OFF, K_OFF, V_OFF)):
            for gi in range(3):
                for half in range(2):
                    piece = dqkv_refs[which * 3 + gi][:, half * 128:(half + 1) * 128]
                    if which < 2:
                        piece = _rope(piece, cosv, sinv, first, -1.0)
                    c0 = off + gi * GW + half * 128
                    dproj_ref[:, c0:c0 + 128] = piece.astype(BF16)
        dproj_ref[:, G_OFF:IN_WIDTH] = dgl_ref[...]
        du = _dot_nt(dproj_ref[...], win_v[...])
        g, sh, sc = (vec_ref[k:k + 1, :] for k in range(3))
        r, xn, y, _ = _norm_mod(h_ref[...], g, sc, sh)
        dx, dsh, dsc, dg = _norm_mod_bwd(du, r, xn, y, g, sc)
        dhp_ref[...] = dh_ref[...] + dx
        red_ref[1:2, :] += dsh
        red_ref[2:3, :] += dsc
        red_ref[3:4, :] += dg

    return pl.pallas_call(
        body, name=name, grid=(S // TS,),
        in_specs=[_row_tile(TS, D), _row_tile(TS, D), _const((8, D)), _row_tile(TS, 128), _row_tile(TS, 128),
                  _row_tile(TS, GW)] + [_row_tile(TS, GW)] * 9 + [_row_tile(TS, 2 * D), ANY],
        out_specs=[_row_tile(TS, D), _row_tile(TS, IN_WIDTH), _const((8, D))],
        out_shape=[jax.ShapeDtypeStruct((S, D), F32), jax.ShapeDtypeStruct((S, IN_WIDTH), BF16),
                   jax.ShapeDtypeStruct((8, D), F32)],
        scratch_shapes=[pltpu.VMEM((D, IN_WIDTH), BF16), pltpu.SemaphoreType.DMA((1,))],
        compiler_params=_params(),
    )(dh, h, vec, cos, sin, dp, *dqkv, dgl, win)


def _pool_lanes(rows):
    lane = lax.broadcasted_iota(jnp.int32, (rows, GW), 1)
    return lane // HEAD_DIM


def _pool_window(rows):
    grp = _pool_lanes(rows)
    w = jnp.full((rows, GW), POOL_WINDOWS[0], jnp.int32)
    for k in range(1, len(POOL_WINDOWS)):
        w = jnp.where(grp == k, POOL_WINDOWS[k], w)
    return grp, w


def _pool_fwd(p, wbd, scale, name, ts=512):
    S = p.shape[0]
    ext = ts + HALO

    def body(pc_ref, ph_ref, wbd_ref, sc_ref, d_ref, y_ref):
        i = pl.program_id(0)
        cur = pc_ref[...]
        halo = jnp.where(i > 0, ph_ref[...], 0.0)
        s = jnp.concatenate([halo, cur], axis=0)
        grp, w = _pool_window(ext)
        sel = jnp.zeros((ext, GW), F32)
        for k, wk in enumerate(POOL_WINDOWS):
            s = s + pltpu.roll(s, wk // 2, 0)
            sel = jnp.where(grp == k, s, sel)
        t = i * ts + lax.broadcasted_iota(jnp.int32, (ts, GW), 0)
        count = jnp.minimum(t + 1, w[HALO:]).astype(F32)
        d = (sel[HALO:] / count - cur).astype(BF16)
        d_ref[...] = d
        y_ref[...] = (_dot(d, wbd_ref[...]) * sc_ref[...]).astype(BF16)

    return pl.pallas_call(
        body, name=name, grid=(S // ts,),
        in_specs=[_row_tile(ts, GW),
                  pl.BlockSpec((HALO, GW), lambda i: (jnp.maximum(i * (ts // HALO) - 1, 0), 0)),
                  _const((GW, GW)), _const((1, GW))],
        out_specs=[_row_tile(ts, GW), _row_tile(ts, GW)],
        out_shape=[jax.ShapeDtypeStruct((S, GW), BF16), jax.ShapeDtypeStruct((S, GW), BF16)],
        compiler_params=_params(),
    )(p, p, wbd, scale)


def _pool_bwd(dy, d, wbd, scale, name, ts=512):
    S = dy.shape[0]
    ext = ts + HALO
    nsteps = S // ts
    last_halo = S // HALO - 1

    def body(dyc_ref, dyh_ref, d_ref, wbd_ref, sc_ref, dp_ref, dw_ref, ds_ref):
        i = pl.program_id(0)

        @pl.when(i == 0)
        def _():
            dw_ref[...] = jnp.zeros_like(dw_ref)
            ds_ref[...] = jnp.zeros_like(ds_ref)

        dyc = dyc_ref[...]
        dyh = jnp.where(i < nsteps - 1, dyh_ref[...], 0.0)
        dys = (jnp.concatenate([dyc, dyh], axis=0) * sc_ref[...]).astype(BF16)
        dd = _dot_nt(dys, wbd_ref[...])
        grp, w = _pool_window(ext)
        t = i * ts + lax.broadcasted_iota(jnp.int32, (ext, GW), 0)
        s = dd / jnp.minimum(t + 1, w).astype(F32)
        sel = jnp.zeros((ext, GW), F32)
        for k, wk in enumerate(POOL_WINDOWS):
            s = s + pltpu.roll(s, ext - wk // 2, 0)
            sel = jnp.where(grp == k, s, sel)
        dp_ref[...] = sel[:ts] - dd[:ts]
        dv = d_ref[...]
        z = _dot(dv, wbd_ref[...])
        ds_ref[0:1, :] += jnp.sum(dyc * z, axis=0, keepdims=True)
        dw_ref[...] += _dot_tn(dv, dys[:ts])

    return pl.pallas_call(
        body, name=name, grid=(nsteps,),
        in_specs=[_row_tile(ts, GW),
                  pl.BlockSpec((HALO, GW), lambda i: (jnp.minimum((i + 1) * (ts // HALO), last_halo), 0)),
                  _row_tile(ts, GW), _const((GW, GW)), _const((1, GW))],
        out_specs=[_row_tile(ts, GW), _const((GW, GW)), _const((8, GW))],
        out_shape=[jax.ShapeDtypeStruct((S, GW), F32), jax.ShapeDtypeStruct((GW, GW), F32),
                   jax.ShapeDtypeStruct((8, GW), F32)],
        compiler_params=_params(),
    )(dy, dy, d, wbd, scale)


def _head_id(rows):
    return lax.broadcasted_iota(jnp.int32, (rows, GW), 1) // HEAD_DIM


def _stack_heads(t, hid):
    return jnp.concatenate([jnp.where(hid == h, t, jnp.zeros_like(t)) for h in range(HEADS)], axis=0)


def _unstack_heads(t_all, hid):
    out = jnp.zeros((QB, GW), F32)
    for h in range(HEADS):
        out = jnp.where(hid == h, t_all[h * QB:(h + 1) * QB], out)
    return out


def _band_mask(n):
    row = lax.broadcasted_iota(jnp.int32, (HEADS * QB, 2 * QB), 0) % QB
    col = lax.broadcasted_iota(jnp.int32, (HEADS * QB, 2 * QB), 1)
    rel = row + QB - col
    return (rel >= 0) & (rel <= BAND) & ((col >= QB) | (n > 0))


def _attn_fwd(q, k, v, r, name):
    L = q.shape[0]
    nb = L // QB
    cur = pl.BlockSpec((QB, GW), lambda res, n: (n, res))
    prev = pl.BlockSpec((QB, GW), lambda res, n: (jnp.maximum(n - 1, 0), res))

    def body(q_ref, kp_ref, kc_ref, vp_ref, vc_ref, o_ref, lse_ref):
        n = pl.program_id(1)
        hid = _head_id(QB)
        qs = _stack_heads(q_ref[...], hid)
        kc = jnp.concatenate([kp_ref[...], kc_ref[...]], axis=0)
        vc = jnp.concatenate([vp_ref[...], vc_ref[...]], axis=0)
        s = _dot_nt(qs, kc) * (HEAD_DIM ** -0.5)
        s = jnp.where(_band_mask(n), s, -jnp.inf)
        m = jnp.max(s, axis=-1, keepdims=True)
        lse = m + jnp.log(jnp.sum(jnp.exp(s - m), axis=-1, keepdims=True))
        pr = jnp.exp(s - lse).astype(BF16)
        o_ref[...] = _unstack_heads(_dot(pr, vc), hid)
        lse_ref[...] = _unstack_heads(jnp.broadcast_to(lse, (HEADS * QB, GW)), hid)

    return pl.pallas_call(
        body, name=name, grid=(r, nb),
        in_specs=[cur, prev, cur, prev, cur], out_specs=[cur, cur],
        out_shape=[jax.ShapeDtypeStruct(q.shape, F32), jax.ShapeDtypeStruct(q.shape, F32)],
        compiler_params=_params(),
    )(q, k, k, v, v)


def _head_rows(t_full, hid):
    return jnp.concatenate(
        [jnp.max(jnp.where(hid == h, t_full, -jnp.inf), axis=-1, keepdims=True) for h in range(HEADS)], axis=0)


def _attn_bwd(q, k, v, do, lse, cterm, r, name):
    L = q.shape[0]
    nb = L // QB
    qside = pl.BlockSpec((QB, GW), lambda res, n: (jnp.minimum(n, nb - 1), res))
    kcur = qside
    kprev = pl.BlockSpec((QB, GW), lambda res, n: (jnp.clip(n - 1, 0, nb - 1), res))
    kout = pl.BlockSpec((QB, GW), lambda res, n: (jnp.maximum(n - 1, 0), res))

    def body(q_ref, do_ref, lse_ref, c_ref, kp_ref, kc_ref, vp_ref, vc_ref,
             dq_ref, dk_ref, dv_ref, carry_k, carry_v):
        n = pl.program_id(1)

        @pl.when(n == 0)
        def _():
            carry_k[...] = jnp.zeros_like(carry_k)
            carry_v[...] = jnp.zeros_like(carry_v)

        @pl.when(n < nb)
        def _():
            hid = _head_id(QB)
            qs = _stack_heads(q_ref[...], hid)
            dos = _stack_heads(do_ref[...], hid)
            kc = jnp.concatenate([kp_ref[...], kc_ref[...]], axis=0)
            vc = jnp.concatenate([vp_ref[...], vc_ref[...]], axis=0)
            s = _dot_nt(qs, kc) * (HEAD_DIM ** -0.5)
            s = jnp.where(_band_mask(n), s, -jnp.inf)
            p = jnp.exp(s - _head_rows(lse_ref[...], hid))
            dp = _dot_nt(dos, vc)
            ds = (p * (dp + _head_rows(c_ref[...], hid)) * (HEAD_DIM ** -0.5)).astype(BF16)
            dq_ref[...] = _unstack_heads(_dot(ds, kc), hid)
            dkc = _dot_tn(ds, qs)
            dvc = _dot_tn(p.astype(BF16), dos)
            dk_ref[...] = carry_k[...] + dkc[:QB]
            dv_ref[...] = carry_v[...] + dvc[:QB]
            carry_k[...] = dkc[QB:]
            carry_v[...] = dvc[QB:]

        @pl.when(n == nb)
        def _():
            dk_ref[...] = carry_k[...]
            dv_ref[...] = carry_v[...]

    out = jax.ShapeDtypeStruct(q.shape, F32)
    return pl.pallas_call(
        body, name=name, grid=(r, nb + 1),
        in_specs=[qside, qside, qside, qside, kprev, kcur, kprev, kcur],
        out_specs=[qside, kout, kout], out_shape=[out, out, out],
        scratch_shapes=[pltpu.VMEM((QB, GW), F32), pltpu.VMEM((QB, GW), F32)],
        compiler_params=_params(),
    )(q, do, lse, cterm, k, k, v, v)


def _group_weights(lse_refs):
    l0, l1, l2 = (ref[...] for ref in lse_refs)
    m = jnp.maximum(jnp.maximum(l0, l1), l2)
    e = [jnp.exp(l - m) for l in (l0, l1, l2)]
    den = e[0] + e[1] + e[2]
    return [ei / den for ei in e]


def _mix_out_fwd(h, vec, gates, ypool, o3, lse3, wpb, wab, wout, name):
    S, D = h.shape

    def body(h_ref, vec_ref, gates_ref, yp_ref, o0, o1, o2, l0, l1, l2, wpb_hbm, wab_hbm, wout_hbm,
             hn_ref, ya_ref, merged_ref, tm_ref, wpb_v, wab_v, wout_v, sem):
        _load_weights([(wpb_hbm, wpb_v), (wab_hbm, wab_v), (wout_hbm, wout_v)], sem)
        gt = vec_ref[3:4, :]
        wts = _group_weights((l0, l1, l2))
        ya = (wts[0] * o0[...] + wts[1] * o1[...] + wts[2] * o2[...]).astype(BF16)
        ya_ref[...] = ya
        merged = (gates_ref[:, :D] * _dot(yp_ref[...], wpb_v[...])
                  + gates_ref[:, D:] * _dot(ya, wab_v[...])).astype(BF16)
        merged_ref[...] = merged
        tm = _dot(merged, wout_v[...])
        tm_ref[...] = tm
        hn_ref[...] = h_ref[...] + gt * tm

    grp = _row_tile(TS, GW)
    return pl.pallas_call(
        body, name=name, grid=(S // TS,),
        in_specs=[_row_tile(TS, D), _const((8, D)), _row_tile(TS, 2 * D), grp] + [grp] * 6 + [ANY, ANY, ANY],
        out_specs=[_row_tile(TS, D), grp, _row_tile(TS, D), _row_tile(TS, D)],
        out_shape=[jax.ShapeDtypeStruct((S, D), F32), jax.ShapeDtypeStruct((S, GW), BF16),
                   jax.ShapeDtypeStruct((S, D), BF16), jax.ShapeDtypeStruct((S, D), F32)],
        scratch_shapes=[pltpu.VMEM((GW, D), BF16), pltpu.VMEM((GW, D), BF16), pltpu.VMEM((D, D), BF16),
                        pltpu.SemaphoreType.DMA((3,))],
        compiler_params=_params(),
    )(h, vec, gates, ypool, *o3, *lse3, wpb, wab, wout)


def _mix_out_bwd(dh, tm, vec, gates, ypool, o3, lse3, wpb, wab, wout, name):
    S, D = dh.shape

    def body(dh_ref, tm_ref, vec_ref, gates_ref, yp_ref, o0, o1, o2, l0, l1, l2, wpb_hbm, wab_hbm, wout_hbm,
             dtm_ref, dgl_ref, dypb_ref, dyab_ref, dyp_ref, do0, do1, do2, c0, c1, c2, red_ref,
             wpb_v, wab_v, wout_v, sem):
        _load_weights([(wpb_hbm, wpb_v), (wab_hbm, wab_v), (wout_hbm, wout_v)], sem)

        @pl.when(pl.program_id(0) == 0)
        def _():
            red_ref[...] = jnp.zeros_like(red_ref)

        gt = vec_ref[3:4, :]
        dh_v = dh_ref[...]
        red_ref[0:1, :] += jnp.sum(tm_ref[...] * dh_v, axis=0, keepdims=True)
        dtm = (gt * dh_v).astype(BF16)
        dtm_ref[...] = dtm
        dm = _dot_nt(dtm, wout_v[...])
        wts = _group_weights((l0, l1, l2))
        ya = wts[0] * o0[...] + wts[1] * o1[...] + wts[2] * o2[...]
        ypb = _dot(yp_ref[...], wpb_v[...])
        yab = _dot(ya.astype(BF16), wab_v[...])
        gp = gates_ref[:, :D]
        ga = gates_ref[:, D:]
        dgl_ref[:, :D] = (dm * ypb * gp * (1.0 - gp)).astype(BF16)
        dgl_ref[:, D:] = (dm * yab * ga * (1.0 - ga)).astype(BF16)
        dypb = (dm * gp).astype(BF16)
        dyab = (dm * ga).astype(BF16)
        dypb_ref[...] = dypb
        dyab_ref[...] = dyab
        dyp_ref[...] = _dot_nt(dypb, wpb_v[...])
        dya = _dot_nt(dyab, wab_v[...])
        row = lax.broadcasted_iota(jnp.int32, (GW, GW), 0) // HEAD_DIM
        col = lax.broadcasted_iota(jnp.int32, (GW, GW), 1) // HEAD_DIM
        ones = jnp.where(row == col, 1.0, 0.0).astype(F32)
        tot = jnp.dot(dya * ya, ones, preferred_element_type=F32, precision=lax.Precision.HIGHEST)
        for wg, do_ref, c_ref in zip(wts, (do0, do1, do2), (c0, c1, c2)):
            do_ref[...] = (wg * dya).astype(BF16)
            c_ref[...] = -(wg * tot)

    grp = _row_tile(TS, GW)
    gb = jax.ShapeDtypeStruct((S, GW), BF16)
    gf = jax.ShapeDtypeStruct((S, GW), F32)
    return pl.pallas_call(
        body, name=name, grid=(S // TS,),
        in_specs=[_row_tile(TS, D), _row_tile(TS, D), _const((8, D)), _row_tile(TS, 2 * D), grp] + [grp] * 6
        + [ANY, ANY, ANY],
        out_specs=[_row_tile(TS, D), _row_tile(TS, 2 * D), _row_tile(TS, D), _row_tile(TS, D), grp]
        + [grp] * 6 + [_const((8, D))],
        out_shape=[jax.ShapeDtypeStruct((S, D), BF16), jax.ShapeDtypeStruct((S, 2 * D), BF16),
                   jax.ShapeDtypeStruct((S, D), BF16), jax.ShapeDtypeStruct((S, D), BF16), gf,
                   gb, gb, gb, gf, gf, gf, jax.ShapeDtypeStruct((8, D), F32)],
        scratch_shapes=[pltpu.VMEM((GW, D), BF16), pltpu.VMEM((GW, D), BF16), pltpu.VMEM((D, D), BF16),
                        pltpu.SemaphoreType.DMA((3,))],
        compiler_params=_params(),
    )(dh, tm, vec, gates, ypool, *o3, *lse3, wpb, wab, wout)


def _final_loss(h, target, gf, name):
    S, D = h.shape

    def body(h_ref, t_ref, g_ref, dh_ref, loss_ref, dg_ref):
        @pl.when(pl.program_id(0) == 0)
        def _():
            loss_ref[...] = jnp.zeros_like(loss_ref)
            dg_ref[...] = jnp.zeros_like(dg_ref)

        x = h_ref[...]
        g = g_ref[0:1, :]
        r = lax.rsqrt(jnp.mean(x * x, axis=-1, keepdims=True) + EPS)
        xn = x * r
        err = xn * g - t_ref[...]
        loss_ref[...] += 0.5 * jnp.sum(jnp.mean(err * err, axis=-1, keepdims=True))
        dy = err * (1.0 / D)
        dg_ref[0:1, :] += jnp.sum(dy * xn, axis=0, keepdims=True)
        dxn = dy * g
        dh_ref[...] = r * (dxn - xn * jnp.mean(dxn * xn, axis=-1, keepdims=True))

    return pl.pallas_call(
        body, name=name, grid=(S // TS,),
        in_specs=[_row_tile(TS, D), _row_tile(TS, D), _const((8, D))],
        out_specs=[_row_tile(TS, D), _const((8, 128)), _const((8, D))],
        out_shape=[jax.ShapeDtypeStruct((S, D), F32), jax.ShapeDtypeStruct((8, 128), F32),
                   jax.ShapeDtypeStruct((8, D), F32)],
        compiler_params=_params(),
    )(h, target, gf)


def _ada_mod(c_all, w, b, name):
    def body(c_ref, w_ref, b_ref, cond_ref, mod_ref):
        cv = c_ref[...]
        cond = cv * jax.nn.sigmoid(cv)
        cond_ref[...] = cond
        mod_ref[...] = jnp.dot(cond, w_ref[...], preferred_element_type=F32,
                               precision=lax.Precision.HIGHEST) + b_ref[...]

    return pl.pallas_call(
        body, name=name,
        out_shape=[jax.ShapeDtypeStruct(c_all.shape, F32), jax.ShapeDtypeStruct((c_all.shape[0], w.shape[1]), F32)],
        compiler_params=_params(),
    )(c_all, w, b)


def _adamw_math(w, g, m, v):
    m = ADAM_B1 * m + (1.0 - ADAM_B1) * g
    v = ADAM_B2 * v + (1.0 - ADAM_B2) * (g * g)
    m_hat = m / (1.0 - ADAM_B1 ** ADAM_STEP)
    v_hat = v / (1.0 - ADAM_B2 ** ADAM_STEP)
    delta = -ADAM_LR * (m_hat / (jnp.sqrt(v_hat) + ADAM_EPS) + ADAM_WD * w)
    return delta, m, v


def _adamw(w, g, m, v, name):
    R, C = w.shape
    tr = R
    for cand in (256, 128, 64, 32, 16, 8):
        if R % cand == 0:
            tr = cand
            break

    def body(w_ref, g_ref, m_ref, v_ref, d_ref, mo_ref, vo_ref):
        d_ref[...], mo_ref[...], vo_ref[...] = _adamw_math(w_ref[...], g_ref[...], m_ref[...], v_ref[...])

    spec = _row_tile(tr, C)
    out = jax.ShapeDtypeStruct((R, C), F32)
    return pl.pallas_call(
        body, name=name, grid=(R // tr,), in_specs=[spec] * 4, out_specs=[spec] * 3, out_shape=[out] * 3,
        compiler_params=_params(),
    )(w, g, m, v)


def _ada_grad_adamw(cond_t, dmod, w, m, v, name, tr=256):
    R, C = w.shape
    nb = dmod.shape[0]

    def body(ct_ref, dm_ref, w_ref, m_ref, v_ref, g_ref, d_ref, mo_ref, vo_ref):
        ct = ct_ref[...]
        dm = dm_ref[...]
        g = jnp.zeros((tr, C), F32)
        for bi in range(nb):
            g = g + ct[:, bi:bi + 1] * dm[bi:bi + 1, :]
        g_ref[...] = g
        d_ref[...], mo_ref[...], vo_ref[...] = _adamw_math(w_ref[...], g, m_ref[...], v_ref[...])

    spec = _row_tile(tr, C)
    out = jax.ShapeDtypeStruct((R, C), F32)
    return pl.pallas_call(
        body, name=name, grid=(R // tr,),
        in_specs=[_row_tile(tr, nb), _const((nb, C)), spec, spec, spec],
        out_specs=[spec] * 4, out_shape=[out] * 4,
        compiler_params=_params(),
    )(cond_t, dmod, w, m, v)


def _row_step(rows, cap=256):
    for cand in range(cap, 15, -16):
        if rows % cand == 0:
            return cand
    return rows


def _slot_sum(x_ref):
    acc = x_ref[0].astype(F32)
    for k in range(1, x_ref.shape[0]):
        acc = acc + x_ref[k].astype(F32)
    return acc


def _sum_slots(x, name, out_dtype=F32):
    n, R, C = x.shape
    tr = _row_step(R)

    def body(x_ref, o_ref):
        o_ref[...] = _slot_sum(x_ref).astype(out_dtype)

    return pl.pallas_call(
        body, name=name, grid=(R // tr,),
        in_specs=[pl.BlockSpec((n, tr, C), lambda i: (0, i, 0))],
        out_specs=_row_tile(tr, C), out_shape=jax.ShapeDtypeStruct((R, C), out_dtype),
        compiler_params=_params(),
    )(x)


def _sum_adamw(parts, w, m, v, name):
    n, R, C = parts.shape
    tr = _row_step(R)

    def body(p_ref, w_ref, m_ref, v_ref, g_ref, d_ref, mo_ref, vo_ref):
        g = _slot_sum(p_ref)
        g_ref[...] = g
        d_ref[...], mo_ref[...], vo_ref[...] = _adamw_math(w_ref[...], g, m_ref[...], v_ref[...])

    spec = _row_tile(tr, C)
    out = jax.ShapeDtypeStruct((R, C), F32)
    return pl.pallas_call(
        body, name=name, grid=(R // tr,),
        in_specs=[pl.BlockSpec((n, tr, C), lambda i: (0, i, 0)), spec, spec, spec],
        out_specs=[spec] * 4, out_shape=[out] * 4,
        compiler_params=_params(),
    )(parts, w, m, v)


def _place():
    return lax.axis_index("x"), lax.axis_index("y"), lax.axis_index("c")


def _all_gather(arrs, name):
    n = len(arrs)

    def body(*refs):
        x_refs, out_refs = refs[:n], refs[n:2 * n]
        send_sems, recv_sems, local_sems = refs[2 * n:]
        x, y, c = _place()
        me, sibling = (x, y, c), (x, y, 1 - c)
        chips = [(1 - x, y), (x, 1 - y), (1 - x, 1 - y)]

        def rows(a, px, py, pc):
            return out_refs[a].at[4 * px + 2 * py + pc]

        def copy(a, k, block, to, src=None):
            return pltpu.make_async_remote_copy(
                src_ref=rows(a, *block) if src is None else src, dst_ref=rows(a, *block),
                send_sem=send_sems.at[a, k], recv_sem=recv_sems.at[a, k], device_id=to, device_id_type=MESH)

        mine = [pltpu.make_async_copy(x_refs[a], rows(a, *me), local_sems.at[a]) for a in range(n)]
        first = []
        for a in range(n):
            first.append(copy(a, 0, me, sibling, src=x_refs[a]))
            first += [copy(a, 1 + j, me, (*chip, c), src=x_refs[a]) for j, chip in enumerate(chips)]
        for cp in mine + first:
            cp.start()
        passed = []
        for j, chip in enumerate(chips):
            for a in range(n):
                copy(a, 1 + j, (*chip, c), me).wait_recv()
                passed.append(copy(a, 4 + j, (*chip, c), sibling))
                passed[-1].start()
        for a in range(n):
            copy(a, 0, sibling, me).wait_recv()
            for j, chip in enumerate(chips):
                copy(a, 4 + j, (*chip, 1 - c), me).wait_recv()
        for cp in first + passed:
            cp.wait_send()
        for cp in mine:
            cp.wait()

    return pl.pallas_call(
        body, name=name, out_shape=[jax.ShapeDtypeStruct((N_DEV,) + t.shape, t.dtype) for t in arrs],
        in_specs=[ANY] * n, out_specs=[ANY] * n,
        scratch_shapes=[pltpu.SemaphoreType.DMA((n, 7)), pltpu.SemaphoreType.DMA((n, 7)),
                        pltpu.SemaphoreType.DMA((n,))],
    )(*arrs)


def _pair_exchange(arrs, name):
    n = len(arrs)

    def body(*refs):
        g_refs, out_refs = refs[:n], refs[n:2 * n]
        send_sems, recv_sems, local_sems = refs[2 * n:]
        x, y, c = _place()
        every = pl.ds(0, 4)
        keep = [pltpu.make_async_copy(g_refs[a].at[every, c], out_refs[a].at[c], local_sems.at[a]) for a in range(n)]
        give = [pltpu.make_async_remote_copy(
            src_ref=g_refs[a].at[every, 1 - c], dst_ref=out_refs[a].at[c], send_sem=send_sems.at[a],
            recv_sem=recv_sems.at[a], device_id=(x, y, 1 - c), device_id_type=MESH) for a in range(n)]
        for cp in give + keep:
            cp.start()
        for cp in give + keep:
            cp.wait()

    return pl.pallas_call(
        body, name=name,
        out_shape=[jax.ShapeDtypeStruct((2, 4) + t.shape[2:], t.dtype) for t in arrs],
        in_specs=[ANY] * n, out_specs=[ANY] * n,
        scratch_shapes=[pltpu.SemaphoreType.DMA((n,)), pltpu.SemaphoreType.DMA((n,)), pltpu.SemaphoreType.DMA((n,))],
    )(*arrs)


def _chip_exchange(arrs, name):
    n = len(arrs)

    def body(*refs):
        p_refs, out_refs = refs[:n], refs[n:2 * n]
        send_sems, recv_sems, local_sems = refs[2 * n:]
        x, y, c = _place()
        mine = 2 * x + y
        chips = [(1 - x, y), (x, 1 - y), (1 - x, 1 - y)]
        keep = [pltpu.make_async_copy(p_refs[a].at[mine], out_refs[a].at[mine], local_sems.at[a]) for a in range(n)]
        copies = [pltpu.make_async_remote_copy(
            src_ref=p_refs[a].at[2 * px + py], dst_ref=out_refs[a].at[mine], send_sem=send_sems.at[a, j],
            recv_sem=recv_sems.at[a, j], device_id=(px, py, c), device_id_type=MESH)
            for a in range(n) for j, (px, py) in enumerate(chips)]
        for cp in copies + keep:
            cp.start()
        for a in range(n):
            for j, (px, py) in enumerate(chips):
                pltpu.make_async_remote_copy(
                    src_ref=p_refs[a].at[mine], dst_ref=out_refs[a].at[2 * px + py], send_sem=send_sems.at[a, j],
                    recv_sem=recv_sems.at[a, j], device_id=(px, py, c), device_id_type=MESH).wait_recv()
        for cp in copies:
            cp.wait_send()
        for cp in keep:
            cp.wait()

    return pl.pallas_call(
        body, name=name, out_shape=[jax.ShapeDtypeStruct(t.shape, t.dtype) for t in arrs],
        in_specs=[ANY] * n, out_specs=[ANY] * n,
        scratch_shapes=[pltpu.SemaphoreType.DMA((n, 3)), pltpu.SemaphoreType.DMA((n, 3)),
                        pltpu.SemaphoreType.DMA((n,))],
    )(*arrs)


def _rope_tables(positions):
    inv_freq = ROPE_THETA ** (-jnp.arange(0, HEAD_DIM, 2, dtype=F32) / HEAD_DIM)
    ang = positions.astype(F32)[:, None] * inv_freq
    cos, sin = jnp.cos(ang), jnp.sin(ang)
    return jnp.tile(cos, (1, 4)), jnp.tile(jnp.concatenate([-sin, sin], axis=1), (1, 2))


def _vec(g, shift, scale, gate):
    z = jnp.zeros_like(g)
    return jnp.stack([g, shift, scale, gate, z, z, z, z])


def _strided(t, r):
    return t.reshape(t.shape[0] // r, r * GW)


def _local_step(x, target, positions, mod, small, W):
    S, D = x.shape
    sh1, sc1, gt1, sh2, sc2, gt2, sh3, sc3, gt3 = (mod[k] for k in range(9))
    v1 = _vec(small["g1"], sh1, sc1, gt1)
    v2 = _vec(small["g2"], sh2, sc2, gt2)
    v3 = _vec(small["g3"], sh3, sc3, gt3)
    vf = _vec(small["gf"], small["gf"], small["gf"], small["gf"])
    cos, sin = _rope_tables(positions)
    wbd = jax.scipy.linalg.block_diag(*[small["w_pool"][k] for k in range(4)]).astype(BF16)
    pscale = small["pool_scale"].reshape(1, GW)

    h1, u1, ab1, act1, f1 = _ffn_fwd(x, v1, W["w1in"], W["w1out"], "ffn1_fwd")
    u2, p, gates, *qkv = _mix_in_fwd(h1, v2, cos, sin, W["win"], "mix_in_fwd")
    dpool, ypool = _pool_fwd(p, wbd, pscale, "pool_fwd")
    o3, lse3 = [], []
    for gi, r in enumerate(DILATIONS):
        o, lse = _attn_fwd(_strided(qkv[gi], r), _strided(qkv[3 + gi], r), _strided(qkv[6 + gi], r), r,
                           f"attn_fwd_{gi}")
        o3.append(o.reshape(S, GW))
        lse3.append(lse.reshape(S, GW))
    h2, ya, merged, tm = _mix_out_fwd(h1, v2, gates, ypool, o3, lse3, W["wpb"], W["wab"], W["wout"], "mix_out_fwd")
    h3, u3, ab3, act3, f3 = _ffn_fwd(h2, v3, W["w2in"], W["w2out"], "ffn2_fwd")
    dh3, loss_blk, dgf = _final_loss(h3, target, vf, "final_loss")

    dh2, dab3, df3, red3 = _ffn_bwd(dh3, h2, f3, ab3, v3, W["w2in"], W["w2out"], "ffn2_bwd")
    (dtm, dgl, dypb, dyab, dyp, do0, do1, do2, c0, c1, c2, red2o) = _mix_out_bwd(
        dh2, tm, v2, gates, ypool, o3, lse3, W["wpb"], W["wab"], W["wout"], "mix_out_bwd")
    dq3, dk3, dv3 = [], [], []
    for gi, (r, do, ct) in enumerate(zip(DILATIONS, (do0, do1, do2), (c0, c1, c2))):
        dq, dk, dv = _attn_bwd(_strided(qkv[gi], r), _strided(qkv[3 + gi], r), _strided(qkv[6 + gi], r),
                               _strided(do, r), _strided(lse3[gi], r), _strided(ct, r), r, f"attn_bwd_{gi}")
        dq3.append(dq.reshape(S, GW))
        dk3.append(dk.reshape(S, GW))
        dv3.append(dv.reshape(S, GW))
    dp, dwbd, dps = _pool_bwd(dyp, dpool, wbd, pscale, "pool_bwd")
    dh1, dproj, red2i = _mix_in_bwd(dh2, h1, v2, cos, sin, dp, dq3 + dk3 + dv3, dgl, W["win"], "mix_in_bwd")
    dx, dab1, df1, red1 = _ffn_bwd(dh1, x, f1, ab1, v1, W["w1in"], W["w1out"], "ffn1_bwd")

    G = {
        "w1in": _wgrad(u1, dab1, "wgrad_1in"), "w1out": _wgrad(act1, df1, "wgrad_1out"),
        "win": _wgrad(u2, dproj, "wgrad_in", tn=2304),
        "wpb": _wgrad(ypool, dypb, "wgrad_pb"), "wab": _wgrad(ya, dyab, "wgrad_ab"),
        "wout": _wgrad(merged, dtm, "wgrad_out"),
        "w2in": _wgrad(u3, dab3, "wgrad_2in"), "w2out": _wgrad(act3, df3, "wgrad_2out"),
    }
    dmod = jnp.stack([red1[1], red1[2], red1[0], red2i[1], red2i[2], red2o[0], red3[1], red3[2], red3[0]])
    dsmall = {
        "g1": red1[3], "g2": red2i[3], "g3": red3[3], "gf": dgf[0],
        "w_pool": jnp.stack([dwbd[k * 64:(k + 1) * 64, k * 64:(k + 1) * 64] for k in range(4)]),
        "pool_scale": dps[0],
    }
    return loss_blk[0, 0], dx, G, dmod, dsmall


SHARDED = ("w_ffn1_in", "w_ffn1_out", "w_in", "w_pool_branch", "w_attn_branch", "w_out", "w_ffn2_in", "w_ffn2_out")


def _cols_to_full(g):
    return jnp.concatenate([g[j] for j in range(N_DEV)], axis=1)


def _full_to_cols(t):
    c = t.shape[1] // N_DEV
    return jnp.stack([t[:, j * c:(j + 1) * c] for j in range(N_DEV)])


SMALL =(("b_ada", 9216), ("g_norm_ffn1", 1024), ("g_norm_mix", 1024), ("g_norm_ffn2", 1024), ("g_final", 1024),
         ("w_pool", 16384), ("pool_scale", 256))
SMALL_ROWS = 240


def _pack_small(vals, loss=None):
    flat = jnp.concatenate([vals[name].reshape(-1) for name, _ in SMALL])
    tail = jnp.zeros((SMALL_ROWS * 128 - flat.shape[0],), F32)
    if loss is not None:
        tail = tail.at[0].set(loss)
    return jnp.concatenate([flat, tail]).reshape(SMALL_ROWS, 128)


def _unpack_small(slab, shapes):
    flat, out, off = slab.reshape(-1), {}, 0
    for name, n in SMALL:
        out[name] = flat[off:off + n].reshape(shapes[name])
        off += n
    return out, flat[off]


def kernel(x, c, positions, w_ada, b_ada, g_norm_ffn1, w_ffn1_in, w_ffn1_out, g_norm_mix, w_in, w_pool, pool_scale, w_pool_branch, w_attn_branch, w_out, g_norm_ffn2, w_ffn2_in, w_ffn2_out, g_final, loss_target, m_w_ada, m_b_ada, m_g_norm_ffn1, m_w_ffn1_in, m_w_ffn1_out, m_g_norm_mix, m_w_in, m_w_pool, m_pool_scale, m_w_pool_branch, m_w_attn_branch, m_w_out, m_g_norm_ffn2, m_w_ffn2_in, m_w_ffn2_out, m_g_final, v_w_ada, v_b_ada, v_g_norm_ffn1, v_w_ffn1_in, v_w_ffn1_out, v_g_norm_mix, v_w_in, v_w_pool, v_pool_scale, v_w_pool_branch, v_w_attn_branch, v_w_out, v_g_norm_ffn2, v_w_ffn2_in, v_w_ffn2_out, v_g_final):
    names = ["w_ada", "b_ada", "g_norm_ffn1", "w_ffn1_in", "w_ffn1_out", "g_norm_mix", "w_in", "w_pool", "pool_scale",
             "w_pool_branch", "w_attn_branch", "w_out", "g_norm_ffn2", "w_ffn2_in", "w_ffn2_out", "g_final"]
    w = dict(w_ada=w_ada, b_ada=b_ada, g_norm_ffn1=g_norm_ffn1, w_ffn1_in=w_ffn1_in, w_ffn1_out=w_ffn1_out,
             g_norm_mix=g_norm_mix, w_in=w_in, w_pool=w_pool, pool_scale=pool_scale, w_pool_branch=w_pool_branch,
             w_attn_branch=w_attn_branch, w_out=w_out, g_norm_ffn2=g_norm_ffn2, w_ffn2_in=w_ffn2_in,
             w_ffn2_out=w_ffn2_out, g_final=g_final)
    m = dict(w_ada=m_w_ada, b_ada=m_b_ada, g_norm_ffn1=m_g_norm_ffn1, w_ffn1_in=m_w_ffn1_in, w_ffn1_out=m_w_ffn1_out,
             g_norm_mix=m_g_norm_mix, w_in=m_w_in, w_pool=m_w_pool, pool_scale=m_pool_scale,
             w_pool_branch=m_w_pool_branch, w_attn_branch=m_w_attn_branch, w_out=m_w_out, g_norm_ffn2=m_g_norm_ffn2,
             w_ffn2_in=m_w_ffn2_in, w_ffn2_out=m_w_ffn2_out, g_final=m_g_final)
    v = dict(w_ada=v_w_ada, b_ada=v_b_ada, g_norm_ffn1=v_g_norm_ffn1, w_ffn1_in=v_w_ffn1_in, w_ffn1_out=v_w_ffn1_out,
             g_norm_mix=v_g_norm_mix, w_in=v_w_in, w_pool=v_w_pool, pool_scale=v_pool_scale,
             w_pool_branch=v_w_pool_branch, w_attn_branch=v_w_attn_branch, w_out=v_w_out, g_norm_ffn2=v_g_norm_ffn2,
             w_ffn2_in=v_w_ffn2_in, w_ffn2_out=v_w_ffn2_out, g_final=v_g_final)
    shapes = {n: w[n].shape for n in names}
    me = 4 * lax.axis_index("x") + 2 * lax.axis_index("y") + lax.axis_index("c")
    D = x.shape[-1]
    n_mod = w_ada.shape[-1] * N_DEV // D

    (c_all,) = _all_gather([c.reshape(D // 128, 128)], "gather_c")
    ada_cols = w_ada.shape[-1]
    b_mine = lax.dynamic_slice_in_dim(b_ada, me * ada_cols, ada_cols, axis=1)
    cond, mod_part = _ada_mod(c_all.reshape(N_DEV, D), w_ada[0], b_mine, "ada_mod")
    (mod_all,) = _all_gather([mod_part.reshape(-1, 128)], "gather_mod")
    mod_all = mod_all.reshape(N_DEV, N_DEV, ada_cols)
    mod = lax.dynamic_index_in_dim(mod_all, me, axis=1, keepdims=False).reshape(n_mod, D)

    gathered = dict(zip(SHARDED, _all_gather([w[name][0].astype(BF16) for name in SHARDED], "gather_weights")))
    n_out = gathered["w_ffn1_out"].shape
    ffn_out = (N_DEV // 2, 2 * n_out[1], n_out[2])
    W = dict(w1in=gathered["w_ffn1_in"], w1out=gathered["w_ffn1_out"].reshape(ffn_out),
             win=_cols_to_full(gathered["w_in"]), wpb=_cols_to_full(gathered["w_pool_branch"]),
             wab=_cols_to_full(gathered["w_attn_branch"]), wout=gathered["w_out"].reshape(D, D),
             w2in=gathered["w_ffn2_in"], w2out=gathered["w_ffn2_out"].reshape(ffn_out))
    small = dict(g1=g_norm_ffn1[0], g2=g_norm_mix[0], g3=g_norm_ffn2[0], gf=g_final, w_pool=w_pool[0],
                 pool_scale=pool_scale[0])

    loss_part, dx, G, dmod, dsmall = _local_step(x[0], loss_target[0], positions[0], mod, small, W)

    by_owner = dict(w_ffn1_in=G["w1in"], w_ffn1_out=G["w1out"].reshape(n_out), w_in=_full_to_cols(G["win"]),
                    w_pool_branch=_full_to_cols(G["wpb"]), w_attn_branch=_full_to_cols(G["wab"]),
                    w_out=G["wout"].reshape(gathered["w_out"].shape),
                    w_ffn2_in=G["w2in"], w_ffn2_out=G["w2out"].reshape(n_out))
    pairs = _pair_exchange([by_owner[name].reshape((4, 2) + by_owner[name].shape[1:]) for name in SHARDED],
                           "reduce_pair")
    chip_parts = []
    for name, t in zip(SHARDED, pairs):
        _, _, r, cc = t.shape
        chip_parts.append(_sum_slots(t.reshape(2, 4 * r, cc), "sum_pair_" + name, BF16).reshape(4, r, cc))
    quads = dict(zip(SHARDED, _chip_exchange(chip_parts, "reduce_chips")))

    part = _pack_small(dict(b_ada=dmod, g_norm_ffn1=dsmall["g1"], g_norm_mix=dsmall["g2"], g_norm_ffn2=dsmall["g3"],
                            g_final=dsmall["gf"], w_pool=dsmall["w_pool"], pool_scale=dsmall["pool_scale"]),
                       loss=loss_part)
    (parts,) = _all_gather([part], "gather_small")
    gsmall, loss = _unpack_small(_sum_slots(parts, "sum_small"), shapes)
    rows_mine = ada_cols // 128
    dmod_mine = lax.dynamic_slice_in_dim(parts, me * rows_mine, rows_mine, axis=1).reshape(N_DEV, ada_cols)

    grads, delta, new_m, new_v = {}, {}, {}, {}
    grads["w_ada"], delta["w_ada"], new_m["w_ada"], new_v["w_ada"] = (
        t[None] for t in _ada_grad_adamw(cond.T, dmod_mine, w_ada[0], m_w_ada[0], v_w_ada[0], "ada_grad_adamw"))
    for name in SHARDED:
        grads[name], delta[name], new_m[name], new_v[name] = (
            t[None] for t in _sum_adamw(quads[name], w[name][0], m[name][0], v[name][0], "adamw_" + name))
    sd, sm, sv = _adamw(_pack_small(w), _pack_small(gsmall), _pack_small(m), _pack_small(v), "adamw_small")
    for dst, src in ((delta, sd), (new_m, sm), (new_v, sv)):
        dst.update(_unpack_small(src, shapes)[0])
    grads.update(gsmall)

    return (loss, dx[None], *[grads[n] for n in names], *[delta[n] for n in names],
            *[new_m[n] for n in names], *[new_v[n] for n in names])
```

```python
import functools

import jax
import jax.numpy as jnp
from jax import lax
from jax.experimental import pallas as pl
from jax.experimental.pallas import tpu as pltpu

F32 = jnp.float32
BF16 = jnp.bfloat16
MESH = pl.DeviceIdType.MESH
ANY = pl.BlockSpec(memory_space=pl.ANY)

N_DEV = 8
EPS = 1e-6
HEAD_DIM = 64
HEADS = 4
GW = HEADS * HEAD_DIM
DILATIONS = (1, 4, 16)
BAND = 128
QB = 128
POOL_WINDOWS = (2, 4, 8, 16)
HALO = 16
ROPE_THETA = 10000.0

ADAM_LR = 0.001
ADAM_B1 = 0.9
ADAM_B2 = 0.999
ADAM_EPS = 1e-08
ADAM_WD = 0.01
ADAM_STEP = 10

VMEM_LIMIT = 56 * 1024 * 1024
TS = 256
FC = 256

NT = (((1,), (1,)), ((), ()))
TN = (((0,), (0,)), ((), ()))


def _params(**kw):
    return pltpu.CompilerParams(vmem_limit_bytes=VMEM_LIMIT, **kw)


def _dot(a, b):
    return jnp.dot(a, b, preferred_element_type=F32)


def _dot_nt(a, b):
    return lax.dot_general(a, b, NT, preferred_element_type=F32)


def _dot_tn(a, b):
    return lax.dot_general(a, b, TN, preferred_element_type=F32)


def _load_weights(pairs, sem):
    @pl.when(pl.program_id(0) == 0)
    def _():
        copies = [pltpu.make_async_copy(src, dst, sem.at[i]) for i, (src, dst) in enumerate(pairs)]
        for cp in copies:
            cp.start()
        for cp in copies:
            cp.wait()


def _norm_mod(x, g, sc, sh):
    r = lax.rsqrt(jnp.mean(x * x, axis=-1, keepdims=True) + EPS)
    xn = x * r
    y = xn * g
    return r, xn, y, y * (1.0 + sc) + sh


def _norm_mod_bwd(du, r, xn, y, g, sc):
    dsh = jnp.sum(du, axis=0, keepdims=True)
    dsc = jnp.sum(du * y, axis=0, keepdims=True)
    dy = du * (1.0 + sc)
    dg = jnp.sum(dy * xn, axis=0, keepdims=True)
    dxn = dy * g
    dx = r * (dxn - xn * jnp.mean(dxn * xn, axis=-1, keepdims=True))
    return dx, dsh, dsc, dg


def _row_tile(ts, width):
    return pl.BlockSpec((ts, width), lambda i: (i, 0))


def _const(shape):
    return pl.BlockSpec(shape, lambda *_: (0,) * len(shape))


def _ffn_fwd(h, vec, win, wout, name):
    S, D = h.shape
    nsh, _, fs = win.shape
    nch = nsh // 2

    def body(h_ref, vec_ref, win_hbm, wout_hbm, hn_ref, u_ref, ab_ref, act_ref, f_ref, win_v, wout_v, sem):
        _load_weights([(win_hbm, win_v), (wout_hbm, wout_v)], sem)
        x = h_ref[...]
        g, sh, sc, gt = (vec_ref[k:k + 1, :] for k in range(4))
        _, _, _, u = _norm_mod(x, g, sc, sh)
        ub = u.astype(BF16)
        u_ref[...] = ub
        acc = jnp.zeros((TS, D), F32)
        for j in range(nch):
            a = _dot(ub, win_v[j])
            b = _dot(ub, win_v[nch + j])
            act = ((a * jax.nn.sigmoid(a)) * b).astype(BF16)
            ab_ref[j] = a.astype(BF16)
            ab_ref[nch + j] = b.astype(BF16)
            act_ref[j] = act
            acc = acc + _dot(act, wout_v[j])
        f_ref[...] = acc
        hn_ref[...] = x + (0.5 * gt) * acc

    return pl.pallas_call(
        body, name=name, grid=(S // TS,),
        in_specs=[_row_tile(TS, D), _const((8, D)), ANY, ANY],
        out_specs=[_row_tile(TS, D), _row_tile(TS, D), pl.BlockSpec((nsh, TS, fs), lambda i: (0, i, 0)),
                   pl.BlockSpec((nch, TS, fs), lambda i: (0, i, 0)), _row_tile(TS, D)],
        out_shape=[jax.ShapeDtypeStruct((S, D), F32), jax.ShapeDtypeStruct((S, D), BF16),
                   jax.ShapeDtypeStruct((nsh, S, fs), BF16), jax.ShapeDtypeStruct((nch, S, fs), BF16),
                   jax.ShapeDtypeStruct((S, D), F32)],
        scratch_shapes=[pltpu.VMEM(win.shape, BF16), pltpu.VMEM(wout.shape, BF16), pltpu.SemaphoreType.DMA((2,))],
        compiler_params=_params(),
    )(h, vec, win, wout)


def _ffn_bwd(dh, h, f, ab, vec, win, wout, name):
    S, D = h.shape
    nsh, _, fs = win.shape
    nch = nsh // 2

    def body(dh_ref, h_ref, f_ref, ab_ref, vec_ref, win_hbm, wout_hbm,
             dhp_ref, dab_ref, df_ref, red_ref, win_v, wout_v, sem):
        _load_weights([(win_hbm, win_v), (wout_hbm, wout_v)], sem)

        @pl.when(pl.program_id(0) == 0)
        def _():
            red_ref[...] = jnp.zeros_like(red_ref)

        dh_v = dh_ref[...]
        x = h_ref[...]
        g, sh, sc, gt = (vec_ref[k:k + 1, :] for k in range(4))
        dgt = jnp.sum((0.5 * f_ref[...]) * dh_v, axis=0, keepdims=True)
        dfb = ((0.5 * gt) * dh_v).astype(BF16)
        df_ref[...] = dfb
        du = jnp.zeros((TS, D), F32)
        for j in range(nch):
            dact = _dot_nt(dfb, wout_v[j])
            av = ab_ref[j].astype(F32)
            bv = ab_ref[nch + j].astype(F32)
            sg = jax.nn.sigmoid(av)
            da = (dact * bv * (sg * (1.0 + av * (1.0 - sg)))).astype(BF16)
            db = (dact * (av * sg)).astype(BF16)
            dab_ref[j] = da
            dab_ref[nch + j] = db
            du = du + _dot_nt(da, win_v[j]) + _dot_nt(db, win_v[nch + j])
        r, xn, y, _ = _norm_mod(x, g, sc, sh)
        dx, dsh, dsc, dg = _norm_mod_bwd(du, r, xn, y, g, sc)
        dhp_ref[...] = dh_v + dx
        red_ref[0:1, :] += dgt
        red_ref[1:2, :] += dsh
        red_ref[2:3, :] += dsc
        red_ref[3:4, :] += dg

    ab_spec = pl.BlockSpec((nsh, TS, fs), lambda i: (0, i, 0))
    return pl.pallas_call(
        body, name=name, grid=(S // TS,),
        in_specs=[_row_tile(TS, D), _row_tile(TS, D), _row_tile(TS, D), ab_spec, _const((8, D)), ANY, ANY],
        out_specs=[_row_tile(TS, D), ab_spec, _row_tile(TS, D), _const((8, D))],
        out_shape=[jax.ShapeDtypeStruct((S, D), F32), jax.ShapeDtypeStruct((nsh, S, fs), BF16),
                   jax.ShapeDtypeStruct((S, D), BF16), jax.ShapeDtypeStruct((8, D), F32)],
        scratch_shapes=[pltpu.VMEM(win.shape, BF16), pltpu.VMEM(wout.shape, BF16), pltpu.SemaphoreType.DMA((2,))],
        compiler_params=_params(),
    )(dh, h, f, ab, vec, win, wout)


def _wgrad(x, y, name, tn=None, ts=512):
    xb, yb = x.ndim == 3, y.ndim == 3
    nb = x.shape[0] if xb else (y.shape[0] if yb else 0)
    S, M = x.shape[-2:]
    N = y.shape[-1]
    tn = tn or N
    nk = S // ts

    def body(x_ref, y_ref, o_ref, acc):
        k = pl.program_id(2)

        @pl.when(k == 0)
        def _():
            acc[...] = jnp.zeros_like(acc)

        acc[...] += _dot_tn(x_ref[...], y_ref[...])

        @pl.when(k == nk - 1)
        def _():
            o_ref[...] = acc[...].astype(BF16)

    x_spec = (pl.BlockSpec((None, ts, M), lambda b, j, k: (b, k, 0)) if xb
              else pl.BlockSpec((ts, M), lambda b, j, k: (k, 0)))
    y_spec = (pl.BlockSpec((None, ts, tn), lambda b, j, k: (b, k, j)) if yb
              else pl.BlockSpec((ts, tn), lambda b, j, k: (k, j)))
    if nb:
        o_spec, o_shape = pl.BlockSpec((None, M, tn), lambda b, j, k: (b, 0, j)), (nb, M, N)
    else:
        o_spec, o_shape = pl.BlockSpec((M, tn), lambda b, j, k: (0, j)), (M, N)
    return pl.pallas_call(
        body, name=name, grid=(max(nb, 1), N // tn, nk),
        in_specs=[x_spec, y_spec], out_specs=o_spec, out_shape=jax.ShapeDtypeStruct(o_shape, BF16),
        scratch_shapes=[pltpu.VMEM((M, tn), F32)],
        compiler_params=_params(),
    )(x, y)


P_OFF, Q_OFF, K_OFF, V_OFF, G_OFF = 0, 256, 1024, 1792, 2560
IN_WIDTH = 4608


def _first_half_mask(ts):
    lane = lax.broadcasted_iota(jnp.int32, (ts, 128), 1)
    return (lane % HEAD_DIM) < (HEAD_DIM // 2)


def _rope(t, cos, sin_signed, first, sign):
    partner = jnp.where(first, pltpu.roll(t, 96, 1), pltpu.roll(t, 32, 1))
    return t * cos + sign * (partner * sin_signed)


def _mix_in_fwd(h, vec, cos, sin, win, name):
    S, D = h.shape
    grp = jax.ShapeDtypeStruct((S, GW), BF16)

    def body(h_ref, vec_ref, cos_ref, sin_ref, win_hbm, u_ref, p_ref, gates_ref, *rest):
        qkv_refs, (win_v, sem) = rest[:9], rest[9:]
        _load_weights([(win_hbm, win_v)], sem)
        g, sh, sc = (vec_ref[k:k + 1, :] for k in range(3))
        _, _, _, u = _norm_mod(h_ref[...], g, sc, sh)
        ub = u.astype(BF16)
        u_ref[...] = ub
        p_ref[...] = _dot(ub, win_v[:, P_OFF:Q_OFF])
        cosv, sinv = cos_ref[...], sin_ref[...]
        first = _first_half_mask(TS)
        for which, off in enumerate((Q_OFF, K_OFF, V_OFF)):
            t = _dot(ub, win_v[:, off:off + 3 * GW])
            for gi in range(3):
                for half in range(2):
                    c0 = gi * GW + half * 128
                    piece = t[:, c0:c0 + 128]
                    if which < 2:
                        piece = _rope(piece, cosv, sinv, first, 1.0)
                    qkv_refs[which * 3 + gi][:, half * 128:(half + 1) * 128] = piece.astype(BF16)
        gates_ref[...] = jax.nn.sigmoid(_dot(ub, win_v[:, G_OFF:IN_WIDTH]))

    return pl.pallas_call(
        body, name=name, grid=(S // TS,),
        in_specs=[_row_tile(TS, D), _const((8, D)), _row_tile(TS, 128), _row_tile(TS, 128), ANY],
        out_specs=[_row_tile(TS, D), _row_tile(TS, GW), _row_tile(TS, 2 * D)] + [_row_tile(TS, GW)] * 9,
        out_shape=[jax.ShapeDtypeStruct((S, D), BF16), jax.ShapeDtypeStruct((S, GW), F32),
                   jax.ShapeDtypeStruct((S, 2 * D), F32)] + [grp] * 9,
        scratch_shapes=[pltpu.VMEM((D, IN_WIDTH), BF16), pltpu.SemaphoreType.DMA((1,))],
        compiler_params=_params(),
    )(h, vec, cos, sin, win)


def _mix_in_bwd(dh, h, vec, cos, sin, dp, dqkv, dgl, win, name):
    S, D = h.shape

    def body(dh_ref, h_ref, vec_ref, cos_ref, sin_ref, dp_ref, *rest):
        dqkv_refs = rest[:9]
        dgl_ref, win_hbm, dhp_ref, dproj_ref, red_ref, win_v, sem = rest[9:]
        _load_weights([(win_hbm, win_v)], sem)

        @pl.when(pl.program_id(0) == 0)
        def _():
            red_ref[...] = jnp.zeros_like(red_ref)

        cosv, sinv = cos_ref[...], sin_ref[...]
        first = _first_half_mask(TS)
        dproj_ref[:, P_OFF:Q_OFF] = dp_ref[...].astype(BF16)
        for which, off in enumerate((Q_OFF, K_OFF, V_OFF)):
            for gi in range(3):
                for half in range(2):
                    piece = dqkv_refs[which * 3 + gi][:, half * 128:(half + 1) * 128]
                    if which < 2:
                        piece = _rope(piece, cosv, sinv, first, -1.0)
                    c0 = off + gi * GW + half * 128
                    dproj_ref[:, c0:c0 + 128] = piece.astype(BF16)
        dproj_ref[:, G_OFF:IN_WIDTH] = dgl_ref[...]
        du = _dot_nt(dproj_ref[...], win_v[...])
        g, sh, sc = (vec_ref[k:k + 1, :] for k in range(3))
        r, xn, y, _ = _norm_mod(h_ref[...], g, sc, sh)
        dx, dsh, dsc, dg = _norm_mod_bwd(du, r, xn, y, g, sc)
        dhp_ref[...] = dh_ref[...] + dx
        red_ref[1:2, :] += dsh
        red_ref[2:3, :] += dsc
        red_ref[3:4, :] += dg

    return pl.pallas_call(
        body, name=name, grid=(S // TS,),
        in_specs=[_row_tile(TS, D), _row_tile(TS, D), _const((8, D)), _row_tile(TS, 128), _row_tile(TS, 128),
                  _row_tile(TS, GW)] + [_row_tile(TS, GW)] * 9 + [_row_tile(TS, 2 * D), ANY],
        out_specs=[_row_tile(TS, D), _row_tile(TS, IN_WIDTH), _const((8, D))],
        out_shape=[jax.ShapeDtypeStruct((S, D), F32), jax.ShapeDtypeStruct((S, IN_WIDTH), BF16),
                   jax.ShapeDtypeStruct((8, D), F32)],
        scratch_shapes=[pltpu.VMEM((D, IN_WIDTH), BF16), pltpu.SemaphoreType.DMA((1,))],
        compiler_params=_params(),
    )(dh, h, vec, cos, sin, dp, *dqkv, dgl, win)


def _pool_lanes(rows):
    lane = lax.broadcasted_iota(jnp.int32, (rows, GW), 1)
    return lane // HEAD_DIM


def _pool_window(rows):
    grp = _pool_lanes(rows)
    w = jnp.full((rows, GW), POOL_WINDOWS[0], jnp.int32)
    for k in range(1, len(POOL_WINDOWS)):
        w = jnp.where(grp == k, POOL_WINDOWS[k], w)
    return grp, w


def _pool_fwd(p, wbd, scale, name, ts=512):
    S = p.shape[0]
    ext = ts + HALO

    def body(pc_ref, ph_ref, wbd_ref, sc_ref, d_ref, y_ref):
        i = pl.program_id(0)
        cur = pc_ref[...]
        halo = jnp.where(i > 0, ph_ref[...], 0.0)
        s = jnp.concatenate([halo, cur], axis=0)
        grp, w = _pool_window(ext)
        sel = jnp.zeros((ext, GW), F32)
        for k, wk in enumerate(POOL_WINDOWS):
            s = s + pltpu.roll(s, wk // 2, 0)
            sel = jnp.where(grp == k, s, sel)
        t = i * ts + lax.broadcasted_iota(jnp.int32, (ts, GW), 0)
        count = jnp.minimum(t + 1, w[HALO:]).astype(F32)
        d = (sel[HALO:] / count - cur).astype(BF16)
        d_ref[...] = d
        y_ref[...] = (_dot(d, wbd_ref[...]) * sc_ref[...]).astype(BF16)

    return pl.pallas_call(
        body, name=name, grid=(S // ts,),
        in_specs=[_row_tile(ts, GW),
                  pl.BlockSpec((HALO, GW), lambda i: (jnp.maximum(i * (ts // HALO) - 1, 0), 0)),
                  _const((GW, GW)), _const((1, GW))],
        out_specs=[_row_tile(ts, GW), _row_tile(ts, GW)],
        out_shape=[jax.ShapeDtypeStruct((S, GW), BF16), jax.ShapeDtypeStruct((S, GW), BF16)],
        compiler_params=_params(),
    )(p, p, wbd, scale)


def _pool_bwd(dy, d, wbd, scale, name, ts=512):
    S = dy.shape[0]
    ext = ts + HALO
    nsteps = S // ts
    last_halo = S // HALO - 1

    def body(dyc_ref, dyh_ref, d_ref, wbd_ref, sc_ref, dp_ref, dw_ref, ds_ref):
        i = pl.program_id(0)

        @pl.when(i == 0)
        def _():
            dw_ref[...] = jnp.zeros_like(dw_ref)
            ds_ref[...] = jnp.zeros_like(ds_ref)

        dyc = dyc_ref[...]
        dyh = jnp.where(i < nsteps - 1, dyh_ref[...], 0.0)
        dys = (jnp.concatenate([dyc, dyh], axis=0) * sc_ref[...]).astype(BF16)
        dd = _dot_nt(dys, wbd_ref[...])
        grp, w = _pool_window(ext)
        t = i * ts + lax.broadcasted_iota(jnp.int32, (ext, GW), 0)
        s = dd / jnp.minimum(t + 1, w).astype(F32)
        sel = jnp.zeros((ext, GW), F32)
        for k, wk in enumerate(POOL_WINDOWS):
            s = s + pltpu.roll(s, ext - wk // 2, 0)
            sel = jnp.where(grp == k, s, sel)
        dp_ref[...] = sel[:ts] - dd[:ts]
        dv = d_ref[...]
        z = _dot(dv, wbd_ref[...])
        ds_ref[0:1, :] += jnp.sum(dyc * z, axis=0, keepdims=True)
        dw_ref[...] += _dot_tn(dv, dys[:ts])

    return pl.pallas_call(
        body, name=name, grid=(nsteps,),
        in_specs=[_row_tile(ts, GW),
                  pl.BlockSpec((HALO, GW), lambda i: (jnp.minimum((i + 1) * (ts // HALO), last_halo), 0)),
                  _row_tile(ts, GW), _const((GW, GW)), _const((1, GW))],
        out_specs=[_row_tile(ts, GW), _const((GW, GW)), _const((8, GW))],
        out_shape=[jax.ShapeDtypeStruct((S, GW), F32), jax.ShapeDtypeStruct((GW, GW), F32),
                   jax.ShapeDtypeStruct((8, GW), F32)],
        compiler_params=_params(),
    )(dy, dy, d, wbd, scale)


def _head_id(rows):
    return lax.broadcasted_iota(jnp.int32, (rows, GW), 1) // HEAD_DIM


def _stack_heads(t, hid):
    return jnp.concatenate([jnp.where(hid == h, t, jnp.zeros_like(t)) for h in range(HEADS)], axis=0)


def _unstack_heads(t_all, hid):
    out = jnp.zeros((QB, GW), F32)
    for h in range(HEADS):
        out = jnp.where(hid == h, t_all[h * QB:(h + 1) * QB], out)
    return out


def _band_mask(n):
    row = lax.broadcasted_iota(jnp.int32, (HEADS * QB, 2 * QB), 0) % QB
    col = lax.broadcasted_iota(jnp.int32, (HEADS * QB, 2 * QB), 1)
    rel = row + QB - col
    return (rel >= 0) & (rel <= BAND) & ((col >= QB) | (n > 0))


def _attn_fwd(q, k, v, r, name):
    L = q.shape[0]
    nb = L // QB
    cur = pl.BlockSpec((QB, GW), lambda res, n: (n, res))
    prev = pl.BlockSpec((QB, GW), lambda res, n: (jnp.maximum(n - 1, 0), res))

    def body(q_ref, kp_ref, kc_ref, vp_ref, vc_ref, o_ref, lse_ref):
        n = pl.program_id(1)
        hid = _head_id(QB)
        qs = _stack_heads(q_ref[...], hid)
        kc = jnp.concatenate([kp_ref[...], kc_ref[...]], axis=0)
        vc = jnp.concatenate([vp_ref[...], vc_ref[...]], axis=0)
        s = _dot_nt(qs, kc) * (HEAD_DIM ** -0.5)
        s = jnp.where(_band_mask(n), s, -jnp.inf)
        m = jnp.max(s, axis=-1, keepdims=True)
        lse = m + jnp.log(jnp.sum(jnp.exp(s - m), axis=-1, keepdims=True))
        pr = jnp.exp(s - lse).astype(BF16)
        o_ref[...] = _unstack_heads(_dot(pr, vc), hid)
        lse_ref[...] = _unstack_heads(jnp.broadcast_to(lse, (HEADS * QB, GW)), hid)

    return pl.pallas_call(
        body, name=name, grid=(r, nb),
        in_specs=[cur, prev, cur, prev, cur], out_specs=[cur, cur],
        out_shape=[jax.ShapeDtypeStruct(q.shape, F32), jax.ShapeDtypeStruct(q.shape, F32)],
        compiler_params=_params(),
    )(q, k, k, v, v)


def _head_rows(t_full, hid):
    return jnp.concatenate(
        [jnp.max(jnp.where(hid == h, t_full, -jnp.inf), axis=-1, keepdims=True) for h in range(HEADS)], axis=0)


def _attn_bwd(q, k, v, do, lse, cterm, r, name):
    L = q.shape[0]
    nb = L // QB
    qside = pl.BlockSpec((QB, GW), lambda res, n: (jnp.minimum(n, nb - 1), res))
    kcur = qside
    kprev = pl.BlockSpec((QB, GW), lambda res, n: (jnp.clip(n - 1, 0, nb - 1), res))
    kout = pl.BlockSpec((QB, GW), lambda res, n: (jnp.maximum(n - 1, 0), res))

    def body(q_ref, do_ref, lse_ref, c_ref, kp_ref, kc_ref, vp_ref, vc_ref,
             dq_ref, dk_ref, dv_ref, carry_k, carry_v):
        n = pl.program_id(1)

        @pl.when(n == 0)
        def _():
            carry_k[...] = jnp.zeros_like(carry_k)
            carry_v[...] = jnp.zeros_like(carry_v)

        @pl.when(n < nb)
        def _():
            hid = _head_id(QB)
            qs = _stack_heads(q_ref[...], hid)
            dos = _stack_heads(do_ref[...], hid)
            kc = jnp.concatenate([kp_ref[...], kc_ref[...]], axis=0)
            vc = jnp.concatenate([vp_ref[...], vc_ref[...]], axis=0)
            s = _dot_nt(qs, kc) * (HEAD_DIM ** -0.5)
            s = jnp.where(_band_mask(n), s, -jnp.inf)
            p = jnp.exp(s - _head_rows(lse_ref[...], hid))
            dp = _dot_nt(dos, vc)
            ds = (p * (dp + _head_rows(c_ref[...], hid)) * (HEAD_DIM ** -0.5)).astype(BF16)
            dq_ref[...] = _unstack_heads(_dot(ds, kc), hid)
            dkc = _dot_tn(ds, qs)
            dvc = _dot_tn(p.astype(BF16), dos)
            dk_ref[...] = carry_k[...] + dkc[:QB]
            dv_ref[...] = carry_v[...] + dvc[:QB]
            carry_k[...] = dkc[QB:]
            carry_v[...] = dvc[QB:]

        @pl.when(n == nb)
        def _():
            dk_ref[...] = carry_k[...]
            dv_ref[...] = carry_v[...]

    out = jax.ShapeDtypeStruct(q.shape, F32)
    return pl.pallas_call(
        body, name=name, grid=(r, nb + 1),
        in_specs=[qside, qside, qside, qside, kprev, kcur, kprev, kcur],
        out_specs=[qside, kout, kout], out_shape=[out, out, out],
        scratch_shapes=[pltpu.VMEM((QB, GW), F32), pltpu.VMEM((QB, GW), F32)],
        compiler_params=_params(),
    )(q, do, lse, cterm, k, k, v, v)


def _group_weights(lse_refs):
    l0, l1, l2 = (ref[...] for ref in lse_refs)
    m = jnp.maximum(jnp.maximum(l0, l1), l2)
    e = [jnp.exp(l - m) for l in (l0, l1, l2)]
    den = e[0] + e[1] + e[2]
    return [ei / den for ei in e]


def _mix_out_fwd(h, vec, gates, ypool, o3, lse3, wpb, wab, wout, name):
    S, D = h.shape

    def body(h_ref, vec_ref, gates_ref, yp_ref, o0, o1, o2, l0, l1, l2, wpb_hbm, wab_hbm, wout_hbm,
             hn_ref, ya_ref, merged_ref, tm_ref, wpb_v, wab_v, wout_v, sem):
        _load_weights([(wpb_hbm, wpb_v), (wab_hbm, wab_v), (wout_hbm, wout_v)], sem)
        gt = vec_ref[3:4, :]
        wts = _group_weights((l0, l1, l2))
        ya = (wts[0] * o0[...] + wts[1] * o1[...] + wts[2] * o2[...]).astype(BF16)
        ya_ref[...] = ya
        merged = (gates_ref[:, :D] * _dot(yp_ref[...], wpb_v[...])
                  + gates_ref[:, D:] * _dot(ya, wab_v[...])).astype(BF16)
        merged_ref[...] = merged
        tm = _dot(merged, wout_v[...])
        tm_ref[...] = tm
        hn_ref[...] = h_ref[...] + gt * tm

    grp = _row_tile(TS, GW)
    return pl.pallas_call(
        body, name=name, grid=(S // TS,),
        in_specs=[_row_tile(TS, D), _const((8, D)), _row_tile(TS, 2 * D), grp] + [grp] * 6 + [ANY, ANY, ANY],
        out_specs=[_row_tile(TS, D), grp, _row_tile(TS, D), _row_tile(TS, D)],
        out_shape=[jax.ShapeDtypeStruct((S, D), F32), jax.ShapeDtypeStruct((S, GW), BF16),
                   jax.ShapeDtypeStruct((S, D), BF16), jax.ShapeDtypeStruct((S, D), F32)],
        scratch_shapes=[pltpu.VMEM((GW, D), BF16), pltpu.VMEM((GW, D), BF16), pltpu.VMEM((D, D), BF16),
                        pltpu.SemaphoreType.DMA((3,))],
        compiler_params=_params(),
    )(h, vec, gates, ypool, *o3, *lse3, wpb, wab, wout)


def _mix_out_bwd(dh, tm, vec, gates, ypool, o3, lse3, wpb, wab, wout, name):
    S, D = dh.shape

    def body(dh_ref, tm_ref, vec_ref, gates_ref, yp_ref, o0, o1, o2, l0, l1, l2, wpb_hbm, wab_hbm, wout_hbm,
             dtm_ref, dgl_ref, dypb_ref, dyab_ref, dyp_ref, do0, do1, do2, c0, c1, c2, red_ref,
             wpb_v, wab_v, wout_v, sem):
        _load_weights([(wpb_hbm, wpb_v), (wab_hbm, wab_v), (wout_hbm, wout_v)], sem)

        @pl.when(pl.program_id(0) == 0)
        def _():
            red_ref[...] = jnp.zeros_like(red_ref)

        gt = vec_ref[3:4, :]
        dh_v = dh_ref[...]
        red_ref[0:1, :] += jnp.sum(tm_ref[...] * dh_v, axis=0, keepdims=True)
        dtm = (gt * dh_v).astype(BF16)
        dtm_ref[...] = dtm
        dm = _dot_nt(dtm, wout_v[...])
        wts = _group_weights((l0, l1, l2))
        ya = wts[0] * o0[...] + wts[1] * o1[...] + wts[2] * o2[...]
        ypb = _dot(yp_ref[...], wpb_v[...])
        yab = _dot(ya.astype(BF16), wab_v[...])
        gp = gates_ref[:, :D]
        ga = gates_ref[:, D:]
        dgl_ref[:, :D] = (dm * ypb * gp * (1.0 - gp)).astype(BF16)
        dgl_ref[:, D:] = (dm * yab * ga * (1.0 - ga)).astype(BF16)
        dypb = (dm * gp).astype(BF16)
        dyab = (dm * ga).astype(BF16)
        dypb_ref[...] = dypb
        dyab_ref[...] = dyab
        dyp_ref[...] = _dot_nt(dypb, wpb_v[...])
        dya = _dot_nt(dyab, wab_v[...])
        row = lax.broadcasted_iota(jnp.int32, (GW, GW), 0) // HEAD_DIM
        col = lax.broadcasted_iota(jnp.int32, (GW, GW), 1) // HEAD_DIM
        ones = jnp.where(row == col, 1.0, 0.0).astype(F32)
        tot = jnp.dot(dya * ya, ones, preferred_element_type=F32, precision=lax.Precision.HIGHEST)
        for wg, do_ref, c_ref in zip(wts, (do0, do1, do2), (c0, c1, c2)):
            do_ref[...] = (wg * dya).astype(BF16)
            c_ref[...] = -(wg * tot)

    grp = _row_tile(TS, GW)
    gb = jax.ShapeDtypeStruct((S, GW), BF16)
    gf = jax.ShapeDtypeStruct((S, GW), F32)
    return pl.pallas_call(
        body, name=name, grid=(S // TS,),
        in_specs=[_row_tile(TS, D), _row_tile(TS, D), _const((8, D)), _row_tile(TS, 2 * D), grp] + [grp] * 6
        + [ANY, ANY, ANY],
        out_specs=[_row_tile(TS, D), _row_tile(TS, 2 * D), _row_tile(TS, D), _row_tile(TS, D), grp]
        + [grp] * 6 + [_const((8, D))],
        out_shape=[jax.ShapeDtypeStruct((S, D), BF16), jax.ShapeDtypeStruct((S, 2 * D), BF16),
                   jax.ShapeDtypeStruct((S, D), BF16), jax.ShapeDtypeStruct((S, D), BF16), gf,
                   gb, gb, gb, gf, gf, gf, jax.ShapeDtypeStruct((8, D), F32)],
        scratch_shapes=[pltpu.VMEM((GW, D), BF16), pltpu.VMEM((GW, D), BF16), pltpu.VMEM((D, D), BF16),
                        pltpu.SemaphoreType.DMA((3,))],
        compiler_params=_params(),
    )(dh, tm, vec, gates, ypool, *o3, *lse3, wpb, wab, wout)


def _final_loss(h, target, gf, name):
    S, D = h.shape

    def body(h_ref, t_ref, g_ref, dh_ref, loss_ref, dg_ref):
        @pl.when(pl.program_id(0) == 0)
        def _():
            loss_ref[...] = jnp.zeros_like(loss_ref)
            dg_ref[...] = jnp.zeros_like(dg_ref)

        x = h_ref[...]
        g = g_ref[0:1, :]
        r = lax.rsqrt(jnp.mean(x * x, axis=-1, keepdims=True) + EPS)
        xn = x * r
        err = xn * g - t_ref[...]
        loss_ref[...] += 0.5 * jnp.sum(jnp.mean(err * err, axis=-1, keepdims=True))
        dy = err * (1.0 / D)
        dg_ref[0:1, :] += jnp.sum(dy * xn, axis=0, keepdims=True)
        dxn = dy * g
        dh_ref[...] = r * (dxn - xn * jnp.mean(dxn * xn, axis=-1, keepdims=True))

    return pl.pallas_call(
        body, name=name, grid=(S // TS,),
        in_specs=[_row_tile(TS, D), _row_tile(TS, D), _const((8, D))],
        out_specs=[_row_tile(TS, D), _const((8, 128)), _const((8, D))],
        out_shape=[jax.ShapeDtypeStruct((S, D), F32), jax.ShapeDtypeStruct((8, 128), F32),
                   jax.ShapeDtypeStruct((8, D), F32)],
        compiler_params=_params(),
    )(h, target, gf)


def _ada_mod(c_all, w, b, name):
    def body(c_ref, w_ref, b_ref, cond_ref, mod_ref):
        cv = c_ref[...]
        cond = cv * jax.nn.sigmoid(cv)
        cond_ref[...] = cond
        mod_ref[...] = jnp.dot(cond, w_ref[...], preferred_element_type=F32,
                               precision=lax.Precision.HIGHEST) + b_ref[...]

    return pl.pallas_call(
        body, name=name,
        out_shape=[jax.ShapeDtypeStruct(c_all.shape, F32), jax.ShapeDtypeStruct((c_all.shape[0], w.shape[1]), F32)],
        compiler_params=_params(),
    )(c_all, w, b)


def _adamw_math(w, g, m, v):
    m = ADAM_B1 * m + (1.0 - ADAM_B1) * g
    v = ADAM_B2 * v + (1.0 - ADAM_B2) * (g * g)
    m_hat = m / (1.0 - ADAM_B1 ** ADAM_STEP)
    v_hat = v / (1.0 - ADAM_B2 ** ADAM_STEP)
    delta = -ADAM_LR * (m_hat / (jnp.sqrt(v_hat) + ADAM_EPS) + ADAM_WD * w)
    return delta, m, v


def _adamw(w, g, m, v, name):
    R, C = w.shape
    tr = R
    for cand in (256, 128, 64, 32, 16, 8):
        if R % cand == 0:
            tr = cand
            break

    def body(w_ref, g_ref, m_ref, v_ref, d_ref, mo_ref, vo_ref):
        d_ref[...], mo_ref[...], vo_ref[...] = _adamw_math(w_ref[...], g_ref[...], m_ref[...], v_ref[...])

    spec = _row_tile(tr, C)
    out = jax.ShapeDtypeStruct((R, C), F32)
    return pl.pallas_call(
        body, name=name, grid=(R // tr,), in_specs=[spec] * 4, out_specs=[spec] * 3, out_shape=[out] * 3,
        compiler_params=_params(),
    )(w, g, m, v)


def _ada_grad_adamw(cond_t, dmod, w, m, v, name, tr=256):
    R, C = w.shape
    nb = dmod.shape[0]

    def body(ct_ref, dm_ref, w_ref, m_ref, v_ref, g_ref, d_ref, mo_ref, vo_ref):
        ct = ct_ref[...]
        dm = dm_ref[...]
        g = jnp.zeros((tr, C), F32)
        for bi in range(nb):
            g = g + ct[:, bi:bi + 1] * dm[bi:bi + 1, :]
        g_ref[...] = g
        d_ref[...], mo_ref[...], vo_ref[...] = _adamw_math(w_ref[...], g, m_ref[...], v_ref[...])

    spec = _row_tile(tr, C)
    out = jax.ShapeDtypeStruct((R, C), F32)
    return pl.pallas_call(
        body, name=name, grid=(R // tr,),
        in_specs=[_row_tile(tr, nb), _const((nb, C)), spec, spec, spec],
        out_specs=[spec] * 4, out_shape=[out] * 4,
        compiler_params=_params(),
    )(cond_t, dmod, w, m, v)


def _row_step(rows, cap=256):
    for cand in range(cap, 15, -16):
        if rows % cand == 0:
            return cand
    return rows


def _slot_sum(x_ref):
    acc = x_ref[0].astype(F32)
    for k in range(1, x_ref.shape[0]):
        acc = acc + x_ref[k].astype(F32)
    return acc


def _sum_slots(x, name, out_dtype=F32):
    n, R, C = x.shape
    tr = _row_step(R)

    def body(x_ref, o_ref):
        o_ref[...] = _slot_sum(x_ref).astype(out_dtype)

    return pl.pallas_call(
        body, name=name, grid=(R // tr,),
        in_specs=[pl.BlockSpec((n, tr, C), lambda i: (0, i, 0))],
        out_specs=_row_tile(tr, C), out_shape=jax.ShapeDtypeStruct((R, C), out_dtype),
        compiler_params=_params(),
    )(x)


def _sum_pair(core, g, recv, name):
    _, _, R, C = g.shape
    tr = _row_step(R)

    def body(core_ref, g_ref, r_ref, o_ref):
        o_ref[...] = (g_ref[...].astype(F32) + r_ref[...].astype(F32)).astype(BF16)

    return pl.pallas_call(
        body, name=name, out_shape=jax.ShapeDtypeStruct((4, R, C), BF16),
        grid_spec=pltpu.PrefetchScalarGridSpec(
            num_scalar_prefetch=1, grid=(4, R // tr),
            in_specs=[pl.BlockSpec((None, None, tr, C), lambda k, i, core_ref: (k, core_ref[0], i, 0)),
                      pl.BlockSpec((None, tr, C), lambda k, i, core_ref: (k, i, 0))],
            out_specs=pl.BlockSpec((None, tr, C), lambda k, i, core_ref: (k, i, 0))),
        compiler_params=_params(),
    )(core, g, recv)


def _sum_adamw(chip, own, recv, w, m, v, name):
    _, R, C = own.shape
    tr = _row_step(R)

    def body(chip_ref, own_ref, r_ref, w_ref, m_ref, v_ref, g_ref, d_ref, mo_ref, vo_ref):
        g = own_ref[...].astype(F32) + _slot_sum(r_ref)
        g_ref[...] = g
        d_ref[...], mo_ref[...], vo_ref[...] = _adamw_math(w_ref[...], g, m_ref[...], v_ref[...])

    spec = pl.BlockSpec((tr, C), lambda i, chip_ref: (i, 0))
    out = jax.ShapeDtypeStruct((R, C), F32)
    return pl.pallas_call(
        body, name=name, out_shape=[out] * 4,
        grid_spec=pltpu.PrefetchScalarGridSpec(
            num_scalar_prefetch=1, grid=(R // tr,),
            in_specs=[pl.BlockSpec((None, tr, C), lambda i, chip_ref: (chip_ref[0], i, 0)),
                      pl.BlockSpec((3, tr, C), lambda i, chip_ref: (0, i, 0)), spec, spec, spec],
            out_specs=[spec] * 4),
        compiler_params=_params(),
    )(chip, own, recv, w, m, v)


def _place():
    return lax.axis_index("x"), lax.axis_index("y"), lax.axis_index("c")


def _all_gather(arrs, name, own=True):
    n = len(arrs)

    def body(*refs):
        x_refs, out_refs = refs[:n], refs[n:2 * n]
        send_sems, recv_sems, local_sems = refs[2 * n:]
        x, y, c = _place()
        me, sibling = (x, y, c), (x, y, 1 - c)
        chips = [(1 - x, y), (x, 1 - y), (1 - x, 1 - y)]

        def rows(a, px, py, pc):
            return out_refs[a].at[4 * px + 2 * py + pc]

        def copy(a, k, block, to, src=None):
            return pltpu.make_async_remote_copy(
                src_ref=rows(a, *block) if src is None else src, dst_ref=rows(a, *block),
                send_sem=send_sems.at[a, k], recv_sem=recv_sems.at[a, k], device_id=to, device_id_type=MESH)

        mine = [pltpu.make_async_copy(x_refs[a], rows(a, *me), local_sems.at[a]) for a in range(n)] if own else []
        first = []
        for a in range(n):
            first.append(copy(a, 0, me, sibling, src=x_refs[a]))
            first += [copy(a, 1 + j, me, (*chip, c), src=x_refs[a]) for j, chip in enumerate(chips)]
        for cp in mine + first:
            cp.start()
        passed = []
        for j, chip in enumerate(chips):
            for a in range(n):
                copy(a, 1 + j, (*chip, c), me).wait_recv()
                passed.append(copy(a, 4 + j, (*chip, c), sibling))
                passed[-1].start()
        for a in range(n):
            copy(a, 0, sibling, me).wait_recv()
            for j, chip in enumerate(chips):
                copy(a, 4 + j, (*chip, 1 - c), me).wait_recv()
        for cp in first + passed:
            cp.wait_send()
        for cp in mine:
            cp.wait()

    return pl.pallas_call(
        body, name=name, out_shape=[jax.ShapeDtypeStruct((N_DEV,) + t.shape, t.dtype) for t in arrs],
        in_specs=[ANY] * n, out_specs=[ANY] * n,
        scratch_shapes=[pltpu.SemaphoreType.DMA((n, 7)), pltpu.SemaphoreType.DMA((n, 7)),
                        pltpu.SemaphoreType.DMA((n,))],
    )(*arrs)


def _pair_exchange(arrs, name):
    n = len(arrs)

    def body(*refs):
        g_refs, out_refs = refs[:n], refs[n:2 * n]
        send_sems, recv_sems = refs[2 * n:]
        x, y, c = _place()
        give = [pltpu.make_async_remote_copy(
            src_ref=g_refs[a].at[pl.ds(0, 4), 1 - c], dst_ref=out_refs[a], send_sem=send_sems.at[a],
            recv_sem=recv_sems.at[a], device_id=(x, y, 1 - c), device_id_type=MESH) for a in range(n)]
        for cp in give:
            cp.start()
        for cp in give:
            cp.wait()

    return pl.pallas_call(
        body, name=name,
        out_shape=[jax.ShapeDtypeStruct((4,) + t.shape[2:], t.dtype) for t in arrs],
        in_specs=[ANY] * n, out_specs=[ANY] * n,
        scratch_shapes=[pltpu.SemaphoreType.DMA((n,)), pltpu.SemaphoreType.DMA((n,))],
    )(*arrs)


def _chip_exchange(arrs, name):
    n = len(arrs)

    def body(*refs):
        p_refs, out_refs = refs[:n], refs[n:2 * n]
        send_sems, recv_sems = refs[2 * n:]
        x, y, c = _place()
        chips = [(1 - x, y), (x, 1 - y), (1 - x, 1 - y)]
        copies = [pltpu.make_async_remote_copy(
            src_ref=p_refs[a].at[2 * px + py], dst_ref=out_refs[a].at[j], send_sem=send_sems.at[a, j],
            recv_sem=recv_sems.at[a, j], device_id=(px, py, c), device_id_type=MESH)
            for a in range(n) for j, (px, py) in enumerate(chips)]
        for cp in copies:
            cp.start()
        for cp in copies:
            cp.wait()

    return pl.pallas_call(
        body, name=name, out_shape=[jax.ShapeDtypeStruct((3,) + t.shape[1:], t.dtype) for t in arrs],
        in_specs=[ANY] * n, out_specs=[ANY] * n,
        scratch_shapes=[pltpu.SemaphoreType.DMA((n, 3)), pltpu.SemaphoreType.DMA((n, 3))],
    )(*arrs)


def _rope_tables(positions):
    inv_freq = ROPE_THETA ** (-jnp.arange(0, HEAD_DIM, 2, dtype=F32) / HEAD_DIM)
    ang = positions.astype(F32)[:, None] * inv_freq
    cos, sin = jnp.cos(ang), jnp.sin(ang)
    return jnp.tile(cos, (1, 4)), jnp.tile(jnp.concatenate([-sin, sin], axis=1), (1, 2))


def _vec(g, shift, scale, gate):
    z = jnp.zeros_like(g)
    return jnp.stack([g, shift, scale, gate, z, z, z, z])


def _strided(t, r):
    return t.reshape(t.shape[0] // r, r * GW)


def _local_step(x, target, positions, mod, small, W):
    S, D = x.shape
    sh1, sc1, gt1, sh2, sc2, gt2, sh3, sc3, gt3 = (mod[k] for k in range(9))
    v1 = _vec(small["g1"], sh1, sc1, gt1)
    v2 = _vec(small["g2"], sh2, sc2, gt2)
    v3 = _vec(small["g3"], sh3, sc3, gt3)
    vf = _vec(small["gf"], small["gf"], small["gf"], small["gf"])
    cos, sin = _rope_tables(positions)
    wbd = jax.scipy.linalg.block_diag(*[small["w_pool"][k] for k in range(4)]).astype(BF16)
    pscale = small["pool_scale"].reshape(1, GW)

    h1, u1, ab1, act1, f1 = _ffn_fwd(x, v1, W["w1in"], W["w1out"], "ffn1_fwd")
    u2, p, gates, *qkv = _mix_in_fwd(h1, v2, cos, sin, W["win"], "mix_in_fwd")
    dpool, ypool = _pool_fwd(p, wbd, pscale, "pool_fwd")
    o3, lse3 = [], []
    for gi, r in enumerate(DILATIONS):
        o, lse = _attn_fwd(_strided(qkv[gi], r), _strided(qkv[3 + gi], r), _strided(qkv[6 + gi], r), r,
                           f"attn_fwd_{gi}")
        o3.append(o.reshape(S, GW))
        lse3.append(lse.reshape(S, GW))
    h2, ya, merged, tm = _mix_out_fwd(h1, v2, gates, ypool, o3, lse3, W["wpb"], W["wab"], W["wout"], "mix_out_fwd")
    h3, u3, ab3, act3, f3 = _ffn_fwd(h2, v3, W["w2in"], W["w2out"], "ffn2_fwd")
    dh3, loss_blk, dgf = _final_loss(h3, target, vf, "final_loss")

    dh2, dab3, df3, red3 = _ffn_bwd(dh3, h2, f3, ab3, v3, W["w2in"], W["w2out"], "ffn2_bwd")
    (dtm, dgl, dypb, dyab, dyp, do0, do1, do2, c0, c1, c2, red2o) = _mix_out_bwd(
        dh2, tm, v2, gates, ypool, o3, lse3, W["wpb"], W["wab"], W["wout"], "mix_out_bwd")
    dq3, dk3, dv3 = [], [], []
    for gi, (r, do, ct) in enumerate(zip(DILATIONS, (do0, do1, do2), (c0, c1, c2))):
        dq, dk, dv = _attn_bwd(_strided(qkv[gi], r), _strided(qkv[3 + gi], r), _strided(qkv[6 + gi], r),
                               _strided(do, r), _strided(lse3[gi], r), _strided(ct, r), r, f"attn_bwd_{gi}")
        dq3.append(dq.reshape(S, GW))
        dk3.append(dk.reshape(S, GW))
        dv3.append(dv.reshape(S, GW))
    dp, dwbd, dps = _pool_bwd(dyp, dpool, wbd, pscale, "pool_bwd")
    dh1, dproj, red2i = _mix_in_bwd(dh2, h1, v2, cos, sin, dp, dq3 + dk3 + dv3, dgl, W["win"], "mix_in_bwd")
    dx, dab1, df1, red1 = _ffn_bwd(dh1, x, f1, ab1, v1, W["w1in"], W["w1out"], "ffn1_bwd")

    G = {
        "w1in": _wgrad(u1, dab1, "wgrad_1in"), "w1out": _wgrad(act1, df1, "wgrad_1out"),
        "win": _wgrad(u2, dproj, "wgrad_in", tn=2304),
        "wpb": _wgrad(ypool, dypb, "wgrad_pb"), "wab": _wgrad(ya, dyab, "wgrad_ab"),
        "wout": _wgrad(merged, dtm, "wgrad_out"),
        "w2in": _wgrad(u3, dab3, "wgrad_2in"), "w2out": _wgrad(act3, df3, "wgrad_2out"),
    }
    dmod = jnp.stack([red1[1], red1[2], red1[0], red2i[1], red2i[2], red2o[0], red3[1], red3[2], red3[0]])
    dsmall = {
        "g1": red1[3], "g2": red2i[3], "g3": red3[3], "gf": dgf[0],
        "w_pool": jnp.stack([dwbd[k * 64:(k + 1) * 64, k * 64:(k + 1) * 64] for k in range(4)]),
        "pool_scale": dps[0],
    }
    return loss_blk[0, 0], dx, G, dmod, dsmall


SHARDED = ("w_ffn1_in", "w_ffn1_out", "w_in", "w_pool_branch", "w_attn_branch", "w_out", "w_ffn2_in", "w_ffn2_out")


def _cols_to_full(g):
    return jnp.concatenate([g[j] for j in range(N_DEV)], axis=1)


def _full_to_cols(t):
    c = t.shape[1] // N_DEV
    return jnp.stack([t[:, j * c:(j + 1) * c] for j in range(N_DEV)])


SMALL =(("b_ada", 9216), ("g_norm_ffn1", 1024), ("g_norm_mix", 1024), ("g_norm_ffn2", 1024), ("g_final", 1024),
         ("w_pool", 16384), ("pool_scale", 256))
SMALL_ROWS = 240


def _pack_small(vals, loss=None):
    flat = jnp.concatenate([vals[name].reshape(-1) for name, _ in SMALL])
    tail = jnp.zeros((SMALL_ROWS * 128 - flat.shape[0],), F32)
    if loss is not None:
        tail = tail.at[0].set(loss)
    return jnp.concatenate([flat, tail]).reshape(SMALL_ROWS, 128)


def _unpack_small(slab, shapes):
    flat, out, off = slab.reshape(-1), {}, 0
    for name, n in SMALL:
        out[name] = flat[off:off + n].reshape(shapes[name])
        off += n
    return out, flat[off]


def kernel(x, c, positions, w_ada, b_ada, g_norm_ffn1, w_ffn1_in, w_ffn1_out, g_norm_mix, w_in, w_pool, pool_scale, w_pool_branch, w_attn_branch, w_out, g_norm_ffn2, w_ffn2_in, w_ffn2_out, g_final, loss_target, m_w_ada, m_b_ada, m_g_norm_ffn1, m_w_ffn1_in, m_w_ffn1_out, m_g_norm_mix, m_w_in, m_w_pool, m_pool_scale, m_w_pool_branch, m_w_attn_branch, m_w_out, m_g_norm_ffn2, m_w_ffn2_in, m_w_ffn2_out, m_g_final, v_w_ada, v_b_ada, v_g_norm_ffn1, v_w_ffn1_in, v_w_ffn1_out, v_g_norm_mix, v_w_in, v_w_pool, v_pool_scale, v_w_pool_branch, v_w_attn_branch, v_w_out, v_g_norm_ffn2, v_w_ffn2_in, v_w_ffn2_out, v_g_final):
    names = ["w_ada", "b_ada", "g_norm_ffn1", "w_ffn1_in", "w_ffn1_out", "g_norm_mix", "w_in", "w_pool", "pool_scale",
             "w_pool_branch", "w_attn_branch", "w_out", "g_norm_ffn2", "w_ffn2_in", "w_ffn2_out", "g_final"]
    w = dict(w_ada=w_ada, b_ada=b_ada, g_norm_ffn1=g_norm_ffn1, w_ffn1_in=w_ffn1_in, w_ffn1_out=w_ffn1_out,
             g_norm_mix=g_norm_mix, w_in=w_in, w_pool=w_pool, pool_scale=pool_scale, w_pool_branch=w_pool_branch,
             w_attn_branch=w_attn_branch, w_out=w_out, g_norm_ffn2=g_norm_ffn2, w_ffn2_in=w_ffn2_in,
             w_ffn2_out=w_ffn2_out, g_final=g_final)
    m = dict(w_ada=m_w_ada, b_ada=m_b_ada, g_norm_ffn1=m_g_norm_ffn1, w_ffn1_in=m_w_ffn1_in, w_ffn1_out=m_w_ffn1_out,
             g_norm_mix=m_g_norm_mix, w_in=m_w_in, w_pool=m_w_pool, pool_scale=m_pool_scale,
             w_pool_branch=m_w_pool_branch, w_attn_branch=m_w_attn_branch, w_out=m_w_out, g_norm_ffn2=m_g_norm_ffn2,
             w_ffn2_in=m_w_ffn2_in, w_ffn2_out=m_w_ffn2_out, g_final=m_g_final)
    v = dict(w_ada=v_w_ada, b_ada=v_b_ada, g_norm_ffn1=v_g_norm_ffn1, w_ffn1_in=v_w_ffn1_in, w_ffn1_out=v_w_ffn1_out,
             g_norm_mix=v_g_norm_mix, w_in=v_w_in, w_pool=v_w_pool, pool_scale=v_pool_scale,
             w_pool_branch=v_w_pool_branch, w_attn_branch=v_w_attn_branch, w_out=v_w_out, g_norm_ffn2=v_g_norm_ffn2,
             w_ffn2_in=v_w_ffn2_in, w_ffn2_out=v_w_ffn2_out, g_final=v_g_final)
    shapes = {n: w[n].shape for n in names}
    me = 4 * lax.axis_index("x") + 2 * lax.axis_index("y") + lax.axis_index("c")
    D = x.shape[-1]
    n_mod = w_ada.shape[-1] * N_DEV // D

    (c_all,) = _all_gather([c.reshape(D // 128, 128)], "gather_c")
    ada_cols = w_ada.shape[-1]
    b_mine = lax.dynamic_slice_in_dim(b_ada, me * ada_cols, ada_cols, axis=1)
    cond, mod_part = _ada_mod(c_all.reshape(N_DEV, D), w_ada[0], b_mine, "ada_mod")
    (mod_all,) = _all_gather([mod_part.reshape(-1, 128)], "gather_mod")
    mod_all = mod_all.reshape(N_DEV, N_DEV, ada_cols)
    mod = lax.dynamic_index_in_dim(mod_all, me, axis=1, keepdims=False).reshape(n_mod, D)

    shards = [w[name][0].astype(BF16) for name in SHARDED]
    gathered = {name: lax.dynamic_update_index_in_dim(full, shard, me, axis=0)
                for name, shard, full in zip(SHARDED, shards, _all_gather(shards, "gather_weights", own=False))}
    n_out = gathered["w_ffn1_out"].shape
    ffn_out = (N_DEV // 2, 2 * n_out[1], n_out[2])
    W = dict(w1in=gathered["w_ffn1_in"], w1out=gathered["w_ffn1_out"].reshape(ffn_out),
             win=_cols_to_full(gathered["w_in"]), wpb=_cols_to_full(gathered["w_pool_branch"]),
             wab=_cols_to_full(gathered["w_attn_branch"]), wout=gathered["w_out"].reshape(D, D),
             w2in=gathered["w_ffn2_in"], w2out=gathered["w_ffn2_out"].reshape(ffn_out))
    small = dict(g1=g_norm_ffn1[0], g2=g_norm_mix[0], g3=g_norm_ffn2[0], gf=g_final, w_pool=w_pool[0],
                 pool_scale=pool_scale[0])

    loss_part, dx, G, dmod, dsmall = _local_step(x[0], loss_target[0], positions[0], mod, small, W)

    by_owner = dict(w_ffn1_in=G["w1in"], w_ffn1_out=G["w1out"].reshape(n_out), w_in=_full_to_cols(G["win"]),
                    w_pool_branch=_full_to_cols(G["wpb"]), w_attn_branch=_full_to_cols(G["wab"]),
                    w_out=G["wout"].reshape(gathered["w_out"].shape),
                    w_ffn2_in=G["w2in"], w_ffn2_out=G["w2out"].reshape(n_out))
    core = lax.axis_index("c").astype(jnp.int32).reshape(1)
    chip = (2 * lax.axis_index("x") + lax.axis_index("y")).astype(jnp.int32).reshape(1)
    mine = [by_owner[name].reshape((4, 2) + by_owner[name].shape[1:]) for name in SHARDED]
    theirs = _pair_exchange(mine, "reduce_pair")
    chip_parts = [_sum_pair(core, g, r, "sum_pair_" + name) for name, g, r in zip(SHARDED, mine, theirs)]
    others = dict(zip(SHARDED, _chip_exchange(chip_parts, "reduce_chips")))
    chip_parts = dict(zip(SHARDED, chip_parts))

    part = _pack_small(dict(b_ada=dmod, g_norm_ffn1=dsmall["g1"], g_norm_mix=dsmall["g2"], g_norm_ffn2=dsmall["g3"],
                            g_final=dsmall["gf"], w_pool=dsmall["w_pool"], pool_scale=dsmall["pool_scale"]),
                       loss=loss_part)
    (parts,) = _all_gather([part], "gather_small")
    gsmall, loss = _unpack_small(_sum_slots(parts, "sum_small"), shapes)
    rows_mine = ada_cols // 128
    dmod_mine = lax.dynamic_slice_in_dim(parts, me * rows_mine, rows_mine, axis=1).reshape(N_DEV, ada_cols)

    grads, delta, new_m, new_v = {}, {}, {}, {}
    grads["w_ada"], delta["w_ada"], new_m["w_ada"], new_v["w_ada"] = (
        t[None] for t in _ada_grad_adamw(cond.T, dmod_mine, w_ada[0], m_w_ada[0], v_w_ada[0], "ada_grad_adamw"))
    for name in SHARDED:
        grads[name], delta[name], new_m[name], new_v[name] = (
            t[None] for t in _sum_adamw(chip, chip_parts[name], others[name], w[name][0], m[name][0], v[name][0],
                                        "adamw_" + name))
    sd, sm, sv = _adamw(_pack_small(w), _pack_small(gsmall), _pack_small(m), _pack_small(v), "adamw_small")
    for dst, src in ((delta, sd), (new_m, sm), (new_v, sv)):
        dst.update(_unpack_small(src, shapes)[0])
    grads.update(gsmall)

    return (loss, dx[None], *[grads[n] for n in names], *[delta[n] for n in names],
            *[new_m[n] for n in names], *[new_v[n] for n in names])
```

```python
import functools

import jax
import jax.numpy as jnp
from jax import lax
from jax.experimental import pallas as pl
from jax.experimental.pallas import tpu as pltpu

F32 = jnp.float32
BF16 = jnp.bfloat16
MESH = pl.DeviceIdType.MESH
ANY = pl.BlockSpec(memory_space=pl.ANY)

N_DEV = 8
EPS = 1e-6
HEAD_DIM = 64
HEADS = 4
GW = HEADS * HEAD_DIM
DILATIONS = (1, 4, 16)
BAND = 128
QB = 128
POOL_WINDOWS = (2, 4, 8, 16)
HALO = 16
ROPE_THETA = 10000.0

ADAM_LR = 0.001
ADAM_B1 = 0.9
ADAM_B2 = 0.999
ADAM_EPS = 1e-08
ADAM_WD = 0.01
ADAM_STEP = 10

VMEM_LIMIT = 56 * 1024 * 1024
TS = 256
FC = 256

NT = (((1,), (1,)), ((), ()))
TN = (((0,), (0,)), ((), ()))


def _params(**kw):
    return pltpu.CompilerParams(vmem_limit_bytes=VMEM_LIMIT, **kw)


def _dot(a, b):
    return jnp.dot(a, b, preferred_element_type=F32)


def _dot_nt(a, b):
    return lax.dot_general(a, b, NT, preferred_element_type=F32)


def _dot_tn(a, b):
    return lax.dot_general(a, b, TN, preferred_element_type=F32)


def _load_weights(pairs, sem):
    @pl.when(pl.program_id(0) == 0)
    def _():
        copies = [pltpu.make_async_copy(src, dst, sem.at[i]) for i, (src, dst) in enumerate(pairs)]
        for cp in copies:
            cp.start()
        for cp in copies:
            cp.wait()


def _norm_mod(x, g, sc, sh):
    r = lax.rsqrt(jnp.mean(x * x, axis=-1, keepdims=True) + EPS)
    xn = x * r
    y = xn * g
    return r, xn, y, y * (1.0 + sc) + sh


def _norm_mod_bwd(du, r, xn, y, g, sc):
    dsh = jnp.sum(du, axis=0, keepdims=True)
    dsc = jnp.sum(du * y, axis=0, keepdims=True)
    dy = du * (1.0 + sc)
    dg = jnp.sum(dy * xn, axis=0, keepdims=True)
    dxn = dy * g
    dx = r * (dxn - xn * jnp.mean(dxn * xn, axis=-1, keepdims=True))
    return dx, dsh, dsc, dg


def _row_tile(ts, width):
    return pl.BlockSpec((ts, width), lambda i: (i, 0))


def _const(shape):
    return pl.BlockSpec(shape, lambda *_: (0,) * len(shape))


def _ffn_fwd(h, vec, win, wout, name):
    S, D = h.shape
    nsh, fs, _ = win.shape
    nch = nsh // 2

    def body(h_ref, vec_ref, win_hbm, wout_hbm, hn_ref, u_ref, ab_ref, act_ref, f_ref, win_v, wout_v, sem):
        _load_weights([(win_hbm, win_v), (wout_hbm, wout_v)], sem)
        x = h_ref[...]
        g, sh, sc, gt = (vec_ref[k:k + 1, :] for k in range(4))
        _, _, _, u = _norm_mod(x, g, sc, sh)
        ub = u.astype(BF16)
        u_ref[...] = ub
        acc = jnp.zeros((TS, D), F32)
        for j in range(nch):
            a = _dot_nt(ub, win_v[j])
            b = _dot_nt(ub, win_v[nch + j])
            act = ((a * jax.nn.sigmoid(a)) * b).astype(BF16)
            ab_ref[j] = a.astype(BF16)
            ab_ref[nch + j] = b.astype(BF16)
            act_ref[j] = act
            acc = acc + _dot(act, wout_v[j])
        f_ref[...] = acc
        hn_ref[...] = x + (0.5 * gt) * acc

    return pl.pallas_call(
        body, name=name, grid=(S // TS,),
        in_specs=[_row_tile(TS, D), _const((8, D)), ANY, ANY],
        out_specs=[_row_tile(TS, D), _row_tile(TS, D), pl.BlockSpec((nsh, TS, fs), lambda i: (0, i, 0)),
                   pl.BlockSpec((nch, TS, fs), lambda i: (0, i, 0)), _row_tile(TS, D)],
        out_shape=[jax.ShapeDtypeStruct((S, D), F32), jax.ShapeDtypeStruct((S, D), BF16),
                   jax.ShapeDtypeStruct((nsh, S, fs), BF16), jax.ShapeDtypeStruct((nch, S, fs), BF16),
                   jax.ShapeDtypeStruct((S, D), F32)],
        scratch_shapes=[pltpu.VMEM(win.shape, BF16), pltpu.VMEM(wout.shape, BF16), pltpu.SemaphoreType.DMA((2,))],
        compiler_params=_params(),
    )(h, vec, win, wout)


def _ffn_bwd(dh, h, f, ab, vec, win, wout, name):
    S, D = h.shape
    nsh, fs, _ = win.shape
    nch = nsh // 2

    def body(dh_ref, h_ref, f_ref, ab_ref, vec_ref, win_hbm, wout_hbm,
             dhp_ref, dab_ref, df_ref, red_ref, win_v, wout_v, sem):
        _load_weights([(win_hbm, win_v), (wout_hbm, wout_v)], sem)

        @pl.when(pl.program_id(0) == 0)
        def _():
            red_ref[...] = jnp.zeros_like(red_ref)

        dh_v = dh_ref[...]
        x = h_ref[...]
        g, sh, sc, gt = (vec_ref[k:k + 1, :] for k in range(4))
        dgt = jnp.sum((0.5 * f_ref[...]) * dh_v, axis=0, keepdims=True)
        dfb = ((0.5 * gt) * dh_v).astype(BF16)
        df_ref[...] = dfb
        du = jnp.zeros((TS, D), F32)
        for j in range(nch):
            dact = _dot_nt(dfb, wout_v[j])
            av = ab_ref[j].astype(F32)
            bv = ab_ref[nch + j].astype(F32)
            sg = jax.nn.sigmoid(av)
            da = (dact * bv * (sg * (1.0 + av * (1.0 - sg)))).astype(BF16)
            db = (dact * (av * sg)).astype(BF16)
            dab_ref[j] = da
            dab_ref[nch + j] = db
            du = du + _dot(da, win_v[j]) + _dot(db, win_v[nch + j])
        r, xn, y, _ = _norm_mod(x, g, sc, sh)
        dx, dsh, dsc, dg = _norm_mod_bwd(du, r, xn, y, g, sc)
        dhp_ref[...] = dh_v + dx
        red_ref[0:1, :] += dgt
        red_ref[1:2, :] += dsh
        red_ref[2:3, :] += dsc
        red_ref[3:4, :] += dg

    ab_spec = pl.BlockSpec((nsh, TS, fs), lambda i: (0, i, 0))
    return pl.pallas_call(
        body, name=name, grid=(S // TS,),
        in_specs=[_row_tile(TS, D), _row_tile(TS, D), _row_tile(TS, D), ab_spec, _const((8, D)), ANY, ANY],
        out_specs=[_row_tile(TS, D), ab_spec, _row_tile(TS, D), _const((8, D))],
        out_shape=[jax.ShapeDtypeStruct((S, D), F32), jax.ShapeDtypeStruct((nsh, S, fs), BF16),
                   jax.ShapeDtypeStruct((S, D), BF16), jax.ShapeDtypeStruct((8, D), F32)],
        scratch_shapes=[pltpu.VMEM(win.shape, BF16), pltpu.VMEM(wout.shape, BF16), pltpu.SemaphoreType.DMA((2,))],
        compiler_params=_params(),
    )(dh, h, f, ab, vec, win, wout)


def _wgrad(x, y, name, tm=None, ts=2048):
    xb = x.ndim == 3
    nb = x.shape[0] if xb else 0
    S, M = x.shape[-2:]
    N = y.shape[-1]
    tm = tm or M
    ts = min(ts, S)
    nk = S // ts

    def body(x_ref, y_ref, o_ref, acc):
        k = pl.program_id(2)

        @pl.when(k == 0)
        def _():
            acc[...] = jnp.zeros_like(acc)

        acc[...] += _dot_tn(x_ref[...], y_ref[...])

        @pl.when(k == nk - 1)
        def _():
            o_ref[...] = acc[...].astype(BF16)

    x_spec = (pl.BlockSpec((None, ts, tm), lambda b, i, k: (b, k, i)) if xb
              else pl.BlockSpec((ts, tm), lambda b, i, k: (k, i)))
    y_spec = pl.BlockSpec((ts, N), lambda b, i, k: (k, 0))
    if xb:
        o_spec, o_shape = pl.BlockSpec((None, tm, N), lambda b, i, k: (b, i, 0)), (nb, M, N)
    else:
        o_spec, o_shape = pl.BlockSpec((tm, N), lambda b, i, k: (i, 0)), (M, N)
    return pl.pallas_call(
        body, name=name, grid=(max(nb, 1), M // tm, nk),
        in_specs=[x_spec, y_spec], out_specs=o_spec, out_shape=jax.ShapeDtypeStruct(o_shape, BF16),
        scratch_shapes=[pltpu.VMEM((tm, N), F32)],
        compiler_params=_params(),
    )(x, y)


P_OFF, Q_OFF, K_OFF, V_OFF, G_OFF = 0, 256, 1024, 1792, 2560
IN_WIDTH = 4608


def _first_half_mask(ts):
    lane = lax.broadcasted_iota(jnp.int32, (ts, 128), 1)
    return (lane % HEAD_DIM) < (HEAD_DIM // 2)


def _rope(t, cos, sin_signed, first, sign):
    partner = jnp.where(first, pltpu.roll(t, 96, 1), pltpu.roll(t, 32, 1))
    return t * cos + sign * (partner * sin_signed)


def _mix_in_fwd(h, vec, cos, sin, win, name):
    S, D = h.shape
    grp = jax.ShapeDtypeStruct((S, GW), BF16)

    def body(h_ref, vec_ref, cos_ref, sin_ref, win_hbm, u_ref, p_ref, gates_ref, *rest):
        qkv_refs, (win_v, sem) = rest[:9], rest[9:]
        _load_weights([(win_hbm, win_v)], sem)
        g, sh, sc = (vec_ref[k:k + 1, :] for k in range(3))
        _, _, _, u = _norm_mod(h_ref[...], g, sc, sh)
        ub = u.astype(BF16)
        u_ref[...] = ub
        p_ref[...] = _dot_nt(ub, win_v[P_OFF:Q_OFF, :])
        cosv, sinv = cos_ref[...], sin_ref[...]
        first = _first_half_mask(TS)
        for which, off in enumerate((Q_OFF, K_OFF, V_OFF)):
            t = _dot_nt(ub, win_v[off:off + 3 * GW, :])
            for gi in range(3):
                for half in range(2):
                    c0 = gi * GW + half * 128
                    piece = t[:, c0:c0 + 128]
                    if which < 2:
                        piece = _rope(piece, cosv, sinv, first, 1.0)
                    qkv_refs[which * 3 + gi][:, half * 128:(half + 1) * 128] = piece.astype(BF16)
        gates_ref[...] = jax.nn.sigmoid(_dot_nt(ub, win_v[G_OFF:IN_WIDTH, :]))

    return pl.pallas_call(
        body, name=name, grid=(S // TS,),
        in_specs=[_row_tile(TS, D), _const((8, D)), _row_tile(TS, 128), _row_tile(TS, 128), ANY],
        out_specs=[_row_tile(TS, D), _row_tile(TS, GW), _row_tile(TS, 2 * D)] + [_row_tile(TS, GW)] * 9,
        out_shape=[jax.ShapeDtypeStruct((S, D), BF16), jax.ShapeDtypeStruct((S, GW), F32),
                   jax.ShapeDtypeStruct((S, 2 * D), F32)] + [grp] * 9,
        scratch_shapes=[pltpu.VMEM((IN_WIDTH, D), BF16), pltpu.SemaphoreType.DMA((1,))],
        compiler_params=_params(),
    )(h, vec, cos, sin, win)


def _mix_in_bwd(dh, h, vec, cos, sin, dp, dqkv, dgl, win, name):
    S, D = h.shape

    def body(dh_ref, h_ref, vec_ref, cos_ref, sin_ref, dp_ref, *rest):
        dqkv_refs = rest[:9]
        dgl_ref, win_hbm, dhp_ref, dproj_ref, red_ref, win_v, sem = rest[9:]
        _load_weights([(win_hbm, win_v)], sem)

        @pl.when(pl.program_id(0) == 0)
        def _():
            red_ref[...] = jnp.zeros_like(red_ref)

        cosv, sinv = cos_ref[...], sin_ref[...]
        first = _first_half_mask(TS)
        dproj_ref[:, P_OFF:Q_OFF] = dp_ref[...].astype(BF16)
        for which, off in enumerate((Q_OFF, K_OFF, V_OFF)):
            for gi in range(3):
                for half in range(2):
                    piece = dqkv_refs[which * 3 + gi][:, half * 128:(half + 1) * 128]
                    if which < 2:
                        piece = _rope(piece, cosv, sinv, first, -1.0)
                    c0 = off + gi * GW + half * 128
                    dproj_ref[:, c0:c0 + 128] = piece.astype(BF16)
        dproj_ref[:, G_OFF:IN_WIDTH] = dgl_ref[...]
        du = _dot(dproj_ref[...], win_v[...])
        g, sh, sc = (vec_ref[k:k + 1, :] for k in range(3))
        r, xn, y, _ = _norm_mod(h_ref[...], g, sc, sh)
        dx, dsh, dsc, dg = _norm_mod_bwd(du, r, xn, y, g, sc)
        dhp_ref[...] = dh_ref[...] + dx
        red_ref[1:2, :] += dsh
        red_ref[2:3, :] += dsc
        red_ref[3:4, :] += dg

    return pl.pallas_call(
        body, name=name, grid=(S // TS,),
        in_specs=[_row_tile(TS, D), _row_tile(TS, D), _const((8, D)), _row_tile(TS, 128), _row_tile(TS, 128),
                  _row_tile(TS, GW)] + [_row_tile(TS, GW)] * 9 + [_row_tile(TS, 2 * D), ANY],
        out_specs=[_row_tile(TS, D), _row_tile(TS, IN_WIDTH), _const((8, D))],
        out_shape=[jax.ShapeDtypeStruct((S, D), F32), jax.ShapeDtypeStruct((S, IN_WIDTH), BF16),
                   jax.ShapeDtypeStruct((8, D), F32)],
        scratch_shapes=[pltpu.VMEM((IN_WIDTH, D), BF16), pltpu.SemaphoreType.DMA((1,))],
        compiler_params=_params(),
    )(dh, h, vec, cos, sin, dp, *dqkv, dgl, win)


def _pool_lanes(rows):
    lane = lax.broadcasted_iota(jnp.int32, (rows, GW), 1)
    return lane // HEAD_DIM


def _pool_window(rows):
    grp = _pool_lanes(rows)
    w = jnp.full((rows, GW), POOL_WINDOWS[0], jnp.int32)
    for k in range(1, len(POOL_WINDOWS)):
        w = jnp.where(grp == k, POOL_WINDOWS[k], w)
    return grp, w


def _pool_fwd(p, wbd, scale, name, ts=512):
    S = p.shape[0]
    ext = ts + HALO

    def body(pc_ref, ph_ref, wbd_ref, sc_ref, d_ref, y_ref):
        i = pl.program_id(0)
        cur = pc_ref[...]
        halo = jnp.where(i > 0, ph_ref[...], 0.0)
        s = jnp.concatenate([halo, cur], axis=0)
        grp, w = _pool_window(ext)
        sel = jnp.zeros((ext, GW), F32)
        for k, wk in enumerate(POOL_WINDOWS):
            s = s + pltpu.roll(s, wk // 2, 0)
            sel = jnp.where(grp == k, s, sel)
        t = i * ts + lax.broadcasted_iota(jnp.int32, (ts, GW), 0)
        count = jnp.minimum(t + 1, w[HALO:]).astype(F32)
        d = (sel[HALO:] / count - cur).astype(BF16)
        d_ref[...] = d
        y_ref[...] = (_dot(d, wbd_ref[...]) * sc_ref[...]).astype(BF16)

    return pl.pallas_call(
        body, name=name, grid=(S // ts,),
        in_specs=[_row_tile(ts, GW),
                  pl.BlockSpec((HALO, GW), lambda i: (jnp.maximum(i * (ts // HALO) - 1, 0), 0)),
                  _const((GW, GW)), _const((1, GW))],
        out_specs=[_row_tile(ts, GW), _row_tile(ts, GW)],
        out_shape=[jax.ShapeDtypeStruct((S, GW), BF16), jax.ShapeDtypeStruct((S, GW), BF16)],
        compiler_params=_params(),
    )(p, p, wbd, scale)


def _pool_bwd(dy, d, wbd, scale, name, ts=512):
    S = dy.shape[0]
    ext = ts + HALO
    nsteps = S // ts
    last_halo = S // HALO - 1

    def body(dyc_ref, dyh_ref, d_ref, wbd_ref, sc_ref, dp_ref, dw_ref, ds_ref):
        i = pl.program_id(0)

        @pl.when(i == 0)
        def _():
            dw_ref[...] = jnp.zeros_like(dw_ref)
            ds_ref[...] = jnp.zeros_like(ds_ref)

        dyc = dyc_ref[...]
        dyh = jnp.where(i < nsteps - 1, dyh_ref[...], 0.0)
        dys = (jnp.concatenate([dyc, dyh], axis=0) * sc_ref[...]).astype(BF16)
        dd = _dot_nt(dys, wbd_ref[...])
        grp, w = _pool_window(ext)
        t = i * ts + lax.broadcasted_iota(jnp.int32, (ext, GW), 0)
        s = dd / jnp.minimum(t + 1, w).astype(F32)
        sel = jnp.zeros((ext, GW), F32)
        for k, wk in enumerate(POOL_WINDOWS):
            s = s + pltpu.roll(s, ext - wk // 2, 0)
            sel = jnp.where(grp == k, s, sel)
        dp_ref[...] = sel[:ts] - dd[:ts]
        dv = d_ref[...]
        z = _dot(dv, wbd_ref[...])
        ds_ref[0:1, :] += jnp.sum(dyc * z, axis=0, keepdims=True)
        dw_ref[...] += _dot_tn(dv, dys[:ts])

    return pl.pallas_call(
        body, name=name, grid=(nsteps,),
        in_specs=[_row_tile(ts, GW),
                  pl.BlockSpec((HALO, GW), lambda i: (jnp.minimum((i + 1) * (ts // HALO), last_halo), 0)),
                  _row_tile(ts, GW), _const((GW, GW)), _const((1, GW))],
        out_specs=[_row_tile(ts, GW), _const((GW, GW)), _const((8, GW))],
        out_shape=[jax.ShapeDtypeStruct((S, GW), F32), jax.ShapeDtypeStruct((GW, GW), F32),
                   jax.ShapeDtypeStruct((8, GW), F32)],
        compiler_params=_params(),
    )(dy, dy, d, wbd, scale)


def _head_id(rows):
    return lax.broadcasted_iota(jnp.int32, (rows, GW), 1) // HEAD_DIM


def _stack_heads(t, hid):
    return jnp.concatenate([jnp.where(hid == h, t, jnp.zeros_like(t)) for h in range(HEADS)], axis=0)


def _unstack_heads(t_all, hid):
    out = jnp.zeros((QB, GW), F32)
    for h in range(HEADS):
        out = jnp.where(hid == h, t_all[h * QB:(h + 1) * QB], out)
    return out


def _band_mask(n):
    row = lax.broadcasted_iota(jnp.int32, (HEADS * QB, 2 * QB), 0) % QB
    col = lax.broadcasted_iota(jnp.int32, (HEADS * QB, 2 * QB), 1)
    rel = row + QB - col
    return (rel >= 0) & (rel <= BAND) & ((col >= QB) | (n > 0))


def _attn_fwd(q, k, v, r, name):
    L = q.shape[0]
    nb = L // QB
    cur = pl.BlockSpec((QB, GW), lambda res, n: (n, res))
    prev = pl.BlockSpec((QB, GW), lambda res, n: (jnp.maximum(n - 1, 0), res))

    def body(q_ref, kp_ref, kc_ref, vp_ref, vc_ref, o_ref, lse_ref):
        n = pl.program_id(1)
        hid = _head_id(QB)
        qs = _stack_heads(q_ref[...], hid)
        kc = jnp.concatenate([kp_ref[...], kc_ref[...]], axis=0)
        vc = jnp.concatenate([vp_ref[...], vc_ref[...]], axis=0)
        s = _dot_nt(qs, kc) * (HEAD_DIM ** -0.5)
        s = jnp.where(_band_mask(n), s, -jnp.inf)
        m = jnp.max(s, axis=-1, keepdims=True)
        lse = m + jnp.log(jnp.sum(jnp.exp(s - m), axis=-1, keepdims=True))
        pr = jnp.exp(s - lse).astype(BF16)
        o_ref[...] = _unstack_heads(_dot(pr, vc), hid)
        lse_ref[...] = _unstack_heads(jnp.broadcast_to(lse, (HEADS * QB, GW)), hid)

    return pl.pallas_call(
        body, name=name, grid=(r, nb),
        in_specs=[cur, prev, cur, prev, cur], out_specs=[cur, cur],
        out_shape=[jax.ShapeDtypeStruct(q.shape, F32), jax.ShapeDtypeStruct(q.shape, F32)],
        compiler_params=_params(),
    )(q, k, k, v, v)


def _head_rows(t_full, hid):
    return jnp.concatenate(
        [jnp.max(jnp.where(hid == h, t_full, -jnp.inf), axis=-1, keepdims=True) for h in range(HEADS)], axis=0)


def _attn_bwd(q, k, v, do, lse, cterm, r, name):
    L = q.shape[0]
    nb = L // QB
    qside = pl.BlockSpec((QB, GW), lambda res, n: (jnp.minimum(n, nb - 1), res))
    kcur = qside
    kprev = pl.BlockSpec((QB, GW), lambda res, n: (jnp.clip(n - 1, 0, nb - 1), res))
    kout = pl.BlockSpec((QB, GW), lambda res, n: (jnp.maximum(n - 1, 0), res))

    def body(q_ref, do_ref, lse_ref, c_ref, kp_ref, kc_ref, vp_ref, vc_ref,
             dq_ref, dk_ref, dv_ref, carry_k, carry_v):
        n = pl.program_id(1)

        @pl.when(n == 0)
        def _():
            carry_k[...] = jnp.zeros_like(carry_k)
            carry_v[...] = jnp.zeros_like(carry_v)

        @pl.when(n < nb)
        def _():
            hid = _head_id(QB)
            qs = _stack_heads(q_ref[...], hid)
            dos = _stack_heads(do_ref[...], hid)
            kc = jnp.concatenate([kp_ref[...], kc_ref[...]], axis=0)
            vc = jnp.concatenate([vp_ref[...], vc_ref[...]], axis=0)
            s = _dot_nt(qs, kc) * (HEAD_DIM ** -0.5)
            s = jnp.where(_band_mask(n), s, -jnp.inf)
            p = jnp.exp(s - _head_rows(lse_ref[...], hid))
            dp = _dot_nt(dos, vc)
            ds = (p * (dp + _head_rows(c_ref[...], hid)) * (HEAD_DIM ** -0.5)).astype(BF16)
            dq_ref[...] = _unstack_heads(_dot(ds, kc), hid)
            dkc = _dot_tn(ds, qs)
            dvc = _dot_tn(p.astype(BF16), dos)
            dk_ref[...] = carry_k[...] + dkc[:QB]
            dv_ref[...] = carry_v[...] + dvc[:QB]
            carry_k[...] = dkc[QB:]
            carry_v[...] = dvc[QB:]

        @pl.when(n == nb)
        def _():
            dk_ref[...] = carry_k[...]
            dv_ref[...] = carry_v[...]

    out = jax.ShapeDtypeStruct(q.shape, F32)
    return pl.pallas_call(
        body, name=name, grid=(r, nb + 1),
        in_specs=[qside, qside, qside, qside, kprev, kcur, kprev, kcur],
        out_specs=[qside, kout, kout], out_shape=[out, out, out],
        scratch_shapes=[pltpu.VMEM((QB, GW), F32), pltpu.VMEM((QB, GW), F32)],
        compiler_params=_params(),
    )(q, do, lse, cterm, k, k, v, v)


def _group_weights(lse_refs):
    l0, l1, l2 = (ref[...] for ref in lse_refs)
    m = jnp.maximum(jnp.maximum(l0, l1), l2)
    e = [jnp.exp(l - m) for l in (l0, l1, l2)]
    den = e[0] + e[1] + e[2]
    return [ei / den for ei in e]


def _mix_out_fwd(h, vec, gates, ypool, o3, lse3, wpb, wab, wout, name):
    S, D = h.shape

    def body(h_ref, vec_ref, gates_ref, yp_ref, o0, o1, o2, l0, l1, l2, wpb_hbm, wab_hbm, wout_hbm,
             hn_ref, ya_ref, merged_ref, tm_ref, wpb_v, wab_v, wout_v, sem):
        _load_weights([(wpb_hbm, wpb_v), (wab_hbm, wab_v), (wout_hbm, wout_v)], sem)
        gt = vec_ref[3:4, :]
        wts = _group_weights((l0, l1, l2))
        ya = (wts[0] * o0[...] + wts[1] * o1[...] + wts[2] * o2[...]).astype(BF16)
        ya_ref[...] = ya
        merged = (gates_ref[:, :D] * _dot(yp_ref[...], wpb_v[...])
                  + gates_ref[:, D:] * _dot(ya, wab_v[...])).astype(BF16)
        merged_ref[...] = merged
        tm = _dot(merged, wout_v[...])
        tm_ref[...] = tm
        hn_ref[...] = h_ref[...] + gt * tm

    grp = _row_tile(TS, GW)
    return pl.pallas_call(
        body, name=name, grid=(S // TS,),
        in_specs=[_row_tile(TS, D), _const((8, D)), _row_tile(TS, 2 * D), grp] + [grp] * 6 + [ANY, ANY, ANY],
        out_specs=[_row_tile(TS, D), grp, _row_tile(TS, D), _row_tile(TS, D)],
        out_shape=[jax.ShapeDtypeStruct((S, D), F32), jax.ShapeDtypeStruct((S, GW), BF16),
                   jax.ShapeDtypeStruct((S, D), BF16), jax.ShapeDtypeStruct((S, D), F32)],
        scratch_shapes=[pltpu.VMEM((GW, D), BF16), pltpu.VMEM((GW, D), BF16), pltpu.VMEM((D, D), BF16),
                        pltpu.SemaphoreType.DMA((3,))],
        compiler_params=_params(),
    )(h, vec, gates, ypool, *o3, *lse3, wpb, wab, wout)


def _mix_out_bwd(dh, tm, vec, gates, ypool, o3, lse3, wpb, wab, wout, name):
    S, D = dh.shape

    def body(dh_ref, tm_ref, vec_ref, gates_ref, yp_ref, o0, o1, o2, l0, l1, l2, wpb_hbm, wab_hbm, wout_hbm,
             dtm_ref, dgl_ref, dypb_ref, dyab_ref, dyp_ref, do0, do1, do2, c0, c1, c2, red_ref,
             wpb_v, wab_v, wout_v, sem):
        _load_weights([(wpb_hbm, wpb_v), (wab_hbm, wab_v), (wout_hbm, wout_v)], sem)

        @pl.when(pl.program_id(0) == 0)
        def _():
            red_ref[...] = jnp.zeros_like(red_ref)

        gt = vec_ref[3:4, :]
        dh_v = dh_ref[...]
        red_ref[0:1, :] += jnp.sum(tm_ref[...] * dh_v, axis=0, keepdims=True)
        dtm = (gt * dh_v).astype(BF16)
        dtm_ref[...] = dtm
        dm = _dot_nt(dtm, wout_v[...])
        wts = _group_weights((l0, l1, l2))
        ya = wts[0] * o0[...] + wts[1] * o1[...] + wts[2] * o2[...]
        ypb = _dot(yp_ref[...], wpb_v[...])
        yab = _dot(ya.astype(BF16), wab_v[...])
        gp = gates_ref[:, :D]
        ga = gates_ref[:, D:]
        dgl_ref[:, :D] = (dm * ypb * gp * (1.0 - gp)).astype(BF16)
        dgl_ref[:, D:] = (dm * yab * ga * (1.0 - ga)).astype(BF16)
        dypb = (dm * gp).astype(BF16)
        dyab = (dm * ga).astype(BF16)
        dypb_ref[...] = dypb
        dyab_ref[...] = dyab
        dyp_ref[...] = _dot_nt(dypb, wpb_v[...])
        dya = _dot_nt(dyab, wab_v[...])
        row = lax.broadcasted_iota(jnp.int32, (GW, GW), 0) // HEAD_DIM
        col = lax.broadcasted_iota(jnp.int32, (GW, GW), 1) // HEAD_DIM
        ones = jnp.where(row == col, 1.0, 0.0).astype(F32)
        tot = jnp.dot(dya * ya, ones, preferred_element_type=F32, precision=lax.Precision.HIGHEST)
        for wg, do_ref, c_ref in zip(wts, (do0, do1, do2), (c0, c1, c2)):
            do_ref[...] = (wg * dya).astype(BF16)
            c_ref[...] = -(wg * tot)

    grp = _row_tile(TS, GW)
    gb = jax.ShapeDtypeStruct((S, GW), BF16)
    gf = jax.ShapeDtypeStruct((S, GW), F32)
    return pl.pallas_call(
        body, name=name, grid=(S // TS,),
        in_specs=[_row_tile(TS, D), _row_tile(TS, D), _const((8, D)), _row_tile(TS, 2 * D), grp] + [grp] * 6
        + [ANY, ANY, ANY],
        out_specs=[_row_tile(TS, D), _row_tile(TS, 2 * D), _row_tile(TS, D), _row_tile(TS, D), grp]
        + [grp] * 6 + [_const((8, D))],
        out_shape=[jax.ShapeDtypeStruct((S, D), BF16), jax.ShapeDtypeStruct((S, 2 * D), BF16),
                   jax.ShapeDtypeStruct((S, D), BF16), jax.ShapeDtypeStruct((S, D), BF16), gf,
                   gb, gb, gb, gf, gf, gf, jax.ShapeDtypeStruct((8, D), F32)],
        scratch_shapes=[pltpu.VMEM((GW, D), BF16), pltpu.VMEM((GW, D), BF16), pltpu.VMEM((D, D), BF16),
                        pltpu.SemaphoreType.DMA((3,))],
        compiler_params=_params(),
    )(dh, tm, vec, gates, ypool, *o3, *lse3, wpb, wab, wout)


def _final_loss(h, target, gf, name):
    S, D = h.shape

    def body(h_ref, t_ref, g_ref, dh_ref, loss_ref, dg_ref):
        @pl.when(pl.program_id(0) == 0)
        def _():
            loss_ref[...] = jnp.zeros_like(loss_ref)
            dg_ref[...] = jnp.zeros_like(dg_ref)

        x = h_ref[...]
        g = g_ref[0:1, :]
        r = lax.rsqrt(jnp.mean(x * x, axis=-1, keepdims=True) + EPS)
        xn = x * r
        err = xn * g - t_ref[...]
        loss_ref[...] += 0.5 * jnp.sum(jnp.mean(err * err, axis=-1, keepdims=True))
        dy = err * (1.0 / D)
        dg_ref[0:1, :] += jnp.sum(dy * xn, axis=0, keepdims=True)
        dxn = dy * g
        dh_ref[...] = r * (dxn - xn * jnp.mean(dxn * xn, axis=-1, keepdims=True))

    return pl.pallas_call(
        body, name=name, grid=(S // TS,),
        in_specs=[_row_tile(TS, D), _row_tile(TS, D), _const((8, D))],
        out_specs=[_row_tile(TS, D), _const((8, 128)), _const((8, D))],
        out_shape=[jax.ShapeDtypeStruct((S, D), F32), jax.ShapeDtypeStruct((8, 128), F32),
                   jax.ShapeDtypeStruct((8, D), F32)],
        compiler_params=_params(),
    )(h, target, gf)


def _ada_mod(c_all, w, b, name):
    def body(c_ref, w_ref, b_ref, cond_ref, mod_ref):
        cv = c_ref[...]
        cond = cv * jax.nn.sigmoid(cv)
        cond_ref[...] = cond
        mod_ref[...] = jnp.dot(cond, w_ref[...], preferred_element_type=F32,
                               precision=lax.Precision.HIGHEST) + b_ref[...]

    return pl.pallas_call(
        body, name=name,
        out_shape=[jax.ShapeDtypeStruct(c_all.shape, F32), jax.ShapeDtypeStruct((c_all.shape[0], w.shape[1]), F32)],
        compiler_params=_params(),
    )(c_all, w, b)


def _adamw_math(w, g, m, v):
    m = ADAM_B1 * m + (1.0 - ADAM_B1) * g
    v = ADAM_B2 * v + (1.0 - ADAM_B2) * (g * g)
    m_hat = m / (1.0 - ADAM_B1 ** ADAM_STEP)
    v_hat = v / (1.0 - ADAM_B2 ** ADAM_STEP)
    delta = -ADAM_LR * (m_hat / (jnp.sqrt(v_hat) + ADAM_EPS) + ADAM_WD * w)
    return delta, m, v


def _adamw(w, g, m, v, name):
    R, C = w.shape
    tr = R
    for cand in (256, 128, 64, 32, 16, 8):
        if R % cand == 0:
            tr = cand
            break

    def body(w_ref, g_ref, m_ref, v_ref, d_ref, mo_ref, vo_ref):
        d_ref[...], mo_ref[...], vo_ref[...] = _adamw_math(w_ref[...], g_ref[...], m_ref[...], v_ref[...])

    spec = _row_tile(tr, C)
    out = jax.ShapeDtypeStruct((R, C), F32)
    return pl.pallas_call(
        body, name=name, grid=(R // tr,), in_specs=[spec] * 4, out_specs=[spec] * 3, out_shape=[out] * 3,
        compiler_params=_params(),
    )(w, g, m, v)


def _ada_grad_adamw(cond_t, dmod, w, m, v, name, tr=256):
    R, C = w.shape
    nb = dmod.shape[0]

    def body(ct_ref, dm_ref, w_ref, m_ref, v_ref, g_ref, d_ref, mo_ref, vo_ref):
        ct = ct_ref[...]
        dm = dm_ref[...]
        g = jnp.zeros((tr, C), F32)
        for bi in range(nb):
            g = g + ct[:, bi:bi + 1] * dm[bi:bi + 1, :]
        g_ref[...] = g
        d_ref[...], mo_ref[...], vo_ref[...] = _adamw_math(w_ref[...], g, m_ref[...], v_ref[...])

    spec = _row_tile(tr, C)
    out = jax.ShapeDtypeStruct((R, C), F32)
    return pl.pallas_call(
        body, name=name, grid=(R // tr,),
        in_specs=[_row_tile(tr, nb), _const((nb, C)), spec, spec, spec],
        out_specs=[spec] * 4, out_shape=[out] * 4,
        compiler_params=_params(),
    )(cond_t, dmod, w, m, v)


def _row_step(rows, cap=256):
    for cand in range(cap, 15, -16):
        if rows % cand == 0:
            return cand
    return rows


def _slot_sum(x_ref):
    acc = x_ref[0].astype(F32)
    for k in range(1, x_ref.shape[0]):
        acc = acc + x_ref[k].astype(F32)
    return acc


def _sum_slots(x, name, out_dtype=F32):
    n, R, C = x.shape
    tr = _row_step(R)

    def body(x_ref, o_ref):
        o_ref[...] = _slot_sum(x_ref).astype(out_dtype)

    return pl.pallas_call(
        body, name=name, grid=(R // tr,),
        in_specs=[pl.BlockSpec((n, tr, C), lambda i: (0, i, 0))],
        out_specs=_row_tile(tr, C), out_shape=jax.ShapeDtypeStruct((R, C), out_dtype),
        compiler_params=_params(),
    )(x)


def _sum_pair(core, g, recv, name):
    _, _, R, C = g.shape
    tr = _row_step(R)

    def body(core_ref, g_ref, r_ref, o_ref):
        o_ref[...] = (g_ref[...].astype(F32) + r_ref[...].astype(F32)).astype(BF16)

    return pl.pallas_call(
        body, name=name, out_shape=jax.ShapeDtypeStruct((4, R, C), BF16),
        grid_spec=pltpu.PrefetchScalarGridSpec(
            num_scalar_prefetch=1, grid=(4, R // tr),
            in_specs=[pl.BlockSpec((None, None, tr, C), lambda k, i, core_ref: (k, core_ref[0], i, 0)),
                      pl.BlockSpec((None, tr, C), lambda k, i, core_ref: (k, i, 0))],
            out_specs=pl.BlockSpec((None, tr, C), lambda k, i, core_ref: (k, i, 0))),
        compiler_params=_params(),
    )(core, g, recv)


def _sum_adamw(chip, own, recv, w, m, v, name):
    _, R, C = own.shape
    tr = _row_step(R)

    def body(chip_ref, own_ref, r_ref, w_ref, m_ref, v_ref, g_ref, d_ref, mo_ref, vo_ref):
        g = own_ref[...].astype(F32) + _slot_sum(r_ref)
        g_ref[...] = g
        d_ref[...], mo_ref[...], vo_ref[...] = _adamw_math(w_ref[...], g, m_ref[...], v_ref[...])

    spec = pl.BlockSpec((tr, C), lambda i, chip_ref: (i, 0))
    out = jax.ShapeDtypeStruct((R, C), F32)
    return pl.pallas_call(
        body, name=name, out_shape=[out] * 4,
        grid_spec=pltpu.PrefetchScalarGridSpec(
            num_scalar_prefetch=1, grid=(R // tr,),
            in_specs=[pl.BlockSpec((None, tr, C), lambda i, chip_ref: (chip_ref[0], i, 0)),
                      pl.BlockSpec((3, tr, C), lambda i, chip_ref: (0, i, 0)), spec, spec, spec],
            out_specs=[spec] * 4),
        compiler_params=_params(),
    )(chip, own, recv, w, m, v)


def _place():
    return lax.axis_index("x"), lax.axis_index("y"), lax.axis_index("c")


def _all_gather(arrs, name, own=True):
    n = len(arrs)

    def body(*refs):
        x_refs, out_refs = refs[:n], refs[n:2 * n]
        send_sems, recv_sems, local_sems = refs[2 * n:]
        x, y, c = _place()
        me, sibling = (x, y, c), (x, y, 1 - c)
        chips = [(1 - x, y), (x, 1 - y), (1 - x, 1 - y)]

        def rows(a, px, py, pc):
            return out_refs[a].at[4 * px + 2 * py + pc]

        def copy(a, k, block, to, src=None):
            return pltpu.make_async_remote_copy(
                src_ref=rows(a, *block) if src is None else src, dst_ref=rows(a, *block),
                send_sem=send_sems.at[a, k], recv_sem=recv_sems.at[a, k], device_id=to, device_id_type=MESH)

        mine = [pltpu.make_async_copy(x_refs[a], rows(a, *me), local_sems.at[a]) for a in range(n)] if own else []
        first = []
        for a in range(n):
            first.append(copy(a, 0, me, sibling, src=x_refs[a]))
            first += [copy(a, 1 + j, me, (*chip, c), src=x_refs[a]) for j, chip in enumerate(chips)]
        for cp in mine + first:
            cp.start()
        passed = []
        for j, chip in enumerate(chips):
            for a in range(n):
                copy(a, 1 + j, (*chip, c), me).wait_recv()
                passed.append(copy(a, 4 + j, (*chip, c), sibling))
                passed[-1].start()
        for a in range(n):
            copy(a, 0, sibling, me).wait_recv()
            for j, chip in enumerate(chips):
                copy(a, 4 + j, (*chip, 1 - c), me).wait_recv()
        for cp in first + passed:
            cp.wait_send()
        for cp in mine:
            cp.wait()

    return pl.pallas_call(
        body, name=name, out_shape=[jax.ShapeDtypeStruct((N_DEV,) + t.shape, t.dtype) for t in arrs],
        in_specs=[ANY] * n, out_specs=[ANY] * n,
        scratch_shapes=[pltpu.SemaphoreType.DMA((n, 7)), pltpu.SemaphoreType.DMA((n, 7)),
                        pltpu.SemaphoreType.DMA((n,))],
    )(*arrs)


def _pair_exchange(arrs, name):
    n = len(arrs)

    def body(*refs):
        g_refs, out_refs = refs[:n], refs[n:2 * n]
        send_sems, recv_sems = refs[2 * n:]
        x, y, c = _place()
        give = [pltpu.make_async_remote_copy(
            src_ref=g_refs[a].at[pl.ds(0, 4), 1 - c], dst_ref=out_refs[a], send_sem=send_sems.at[a],
            recv_sem=recv_sems.at[a], device_id=(x, y, 1 - c), device_id_type=MESH) for a in range(n)]
        for cp in give:
            cp.start()
        for cp in give:
            cp.wait()

    return pl.pallas_call(
        body, name=name,
        out_shape=[jax.ShapeDtypeStruct((4,) + t.shape[2:], t.dtype) for t in arrs],
        in_specs=[ANY] * n, out_specs=[ANY] * n,
        scratch_shapes=[pltpu.SemaphoreType.DMA((n,)), pltpu.SemaphoreType.DMA((n,))],
    )(*arrs)


def _chip_exchange(arrs, name):
    n = len(arrs)

    def body(*refs):
        p_refs, out_refs = refs[:n], refs[n:2 * n]
        send_sems, recv_sems = refs[2 * n:]
        x, y, c = _place()
        chips = [(1 - x, y), (x, 1 - y), (1 - x, 1 - y)]
        copies = [pltpu.make_async_remote_copy(
            src_ref=p_refs[a].at[2 * px + py], dst_ref=out_refs[a].at[j], send_sem=send_sems.at[a, j],
            recv_sem=recv_sems.at[a, j], device_id=(px, py, c), device_id_type=MESH)
            for a in range(n) for j, (px, py) in enumerate(chips)]
        for cp in copies:
            cp.start()
        for cp in copies:
            cp.wait()

    return pl.pallas_call(
        body, name=name, out_shape=[jax.ShapeDtypeStruct((3,) + t.shape[1:], t.dtype) for t in arrs],
        in_specs=[ANY] * n, out_specs=[ANY] * n,
        scratch_shapes=[pltpu.SemaphoreType.DMA((n, 3)), pltpu.SemaphoreType.DMA((n, 3))],
    )(*arrs)


def _rope_tables(positions):
    inv_freq = ROPE_THETA ** (-jnp.arange(0, HEAD_DIM, 2, dtype=F32) / HEAD_DIM)
    ang = positions.astype(F32)[:, None] * inv_freq
    cos, sin = jnp.cos(ang), jnp.sin(ang)
    return jnp.tile(cos, (1, 4)), jnp.tile(jnp.concatenate([-sin, sin], axis=1), (1, 2))


def _vec(g, shift, scale, gate):
    z = jnp.zeros_like(g)
    return jnp.stack([g, shift, scale, gate, z, z, z, z])


def _strided(t, r):
    return t.reshape(t.shape[0] // r, r * GW)


def _local_step(x, target, positions, mod, small, W):
    S, D = x.shape
    sh1, sc1, gt1, sh2, sc2, gt2, sh3, sc3, gt3 = (mod[k] for k in range(9))
    v1 = _vec(small["g1"], sh1, sc1, gt1)
    v2 = _vec(small["g2"], sh2, sc2, gt2)
    v3 = _vec(small["g3"], sh3, sc3, gt3)
    vf = _vec(small["gf"], small["gf"], small["gf"], small["gf"])
    cos, sin = _rope_tables(positions)
    wbd = jax.scipy.linalg.block_diag(*[small["w_pool"][k] for k in range(4)]).astype(BF16)
    pscale = small["pool_scale"].reshape(1, GW)

    h1, u1, ab1, act1, f1 = _ffn_fwd(x, v1, W["w1in"], W["w1out"], "ffn1_fwd")
    u2, p, gates, *qkv = _mix_in_fwd(h1, v2, cos, sin, W["win"], "mix_in_fwd")
    dpool, ypool = _pool_fwd(p, wbd, pscale, "pool_fwd")
    o3, lse3 = [], []
    for gi, r in enumerate(DILATIONS):
        o, lse = _attn_fwd(_strided(qkv[gi], r), _strided(qkv[3 + gi], r), _strided(qkv[6 + gi], r), r,
                           f"attn_fwd_{gi}")
        o3.append(o.reshape(S, GW))
        lse3.append(lse.reshape(S, GW))
    h2, ya, merged, tm = _mix_out_fwd(h1, v2, gates, ypool, o3, lse3, W["wpb"], W["wab"], W["wout"], "mix_out_fwd")
    h3, u3, ab3, act3, f3 = _ffn_fwd(h2, v3, W["w2in"], W["w2out"], "ffn2_fwd")
    dh3, loss_blk, dgf = _final_loss(h3, target, vf, "final_loss")

    dh2, dab3, df3, red3 = _ffn_bwd(dh3, h2, f3, ab3, v3, W["w2in"], W["w2out"], "ffn2_bwd")
    (dtm, dgl, dypb, dyab, dyp, do0, do1, do2, c0, c1, c2, red2o) = _mix_out_bwd(
        dh2, tm, v2, gates, ypool, o3, lse3, W["wpb"], W["wab"], W["wout"], "mix_out_bwd")
    dq3, dk3, dv3 = [], [], []
    for gi, (r, do, ct) in enumerate(zip(DILATIONS, (do0, do1, do2), (c0, c1, c2))):
        dq, dk, dv = _attn_bwd(_strided(qkv[gi], r), _strided(qkv[3 + gi], r), _strided(qkv[6 + gi], r),
                               _strided(do, r), _strided(lse3[gi], r), _strided(ct, r), r, f"attn_bwd_{gi}")
        dq3.append(dq.reshape(S, GW))
        dk3.append(dk.reshape(S, GW))
        dv3.append(dv.reshape(S, GW))
    dp, dwbd, dps = _pool_bwd(dyp, dpool, wbd, pscale, "pool_bwd")
    dh1, dproj, red2i = _mix_in_bwd(dh2, h1, v2, cos, sin, dp, dq3 + dk3 + dv3, dgl, W["win"], "mix_in_bwd")
    dx, dab1, df1, red1 = _ffn_bwd(dh1, x, f1, ab1, v1, W["w1in"], W["w1out"], "ffn1_bwd")

    G = {
        "w1in": _wgrad(dab1, u1, "wgrad_1in"), "w1out": _wgrad(act1, df1, "wgrad_1out"),
        "win": _wgrad(dproj, u2, "wgrad_in", tm=1152),
        "wpb": _wgrad(ypool, dypb, "wgrad_pb"), "wab": _wgrad(ya, dyab, "wgrad_ab"),
        "wout": _wgrad(merged, dtm, "wgrad_out"),
        "w2in": _wgrad(dab3, u3, "wgrad_2in"), "w2out": _wgrad(act3, df3, "wgrad_2out"),
    }
    dmod = jnp.stack([red1[1], red1[2], red1[0], red2i[1], red2i[2], red2o[0], red3[1], red3[2], red3[0]])
    dsmall = {
        "g1": red1[3], "g2": red2i[3], "g3": red3[3], "gf": dgf[0],
        "w_pool": jnp.stack([dwbd[k * 64:(k + 1) * 64, k * 64:(k + 1) * 64] for k in range(4)]),
        "pool_scale": dps[0],
    }
    return loss_blk[0, 0], dx, G, dmod, dsmall


SHARDED = ("w_ffn1_in", "w_ffn1_out", "w_in", "w_pool_branch", "w_attn_branch", "w_out", "w_ffn2_in", "w_ffn2_out")
TRANSPOSED = ("w_ffn1_in", "w_in", "w_ffn2_in")


def _cols_to_full(g):
    return jnp.concatenate([g[j] for j in range(N_DEV)], axis=1)


def _full_to_cols(t):
    c = t.shape[1] // N_DEV
    return jnp.stack([t[:, j * c:(j + 1) * c] for j in range(N_DEV)])


SMALL =(("b_ada", 9216), ("g_norm_ffn1", 1024), ("g_norm_mix", 1024), ("g_norm_ffn2", 1024), ("g_final", 1024),
         ("w_pool", 16384), ("pool_scale", 256))
SMALL_ROWS = 240


def _pack_small(vals, loss=None):
    flat = jnp.concatenate([vals[name].reshape(-1) for name, _ in SMALL])
    tail = jnp.zeros((SMALL_ROWS * 128 - flat.shape[0],), F32)
    if loss is not None:
        tail = tail.at[0].set(loss)
    return jnp.concatenate([flat, tail]).reshape(SMALL_ROWS, 128)


def _unpack_small(slab, shapes):
    flat, out, off = slab.reshape(-1), {}, 0
    for name, n in SMALL:
        out[name] = flat[off:off + n].reshape(shapes[name])
        off += n
    return out, flat[off]


def kernel(x, c, positions, w_ada, b_ada, g_norm_ffn1, w_ffn1_in, w_ffn1_out, g_norm_mix, w_in, w_pool, pool_scale, w_pool_branch, w_attn_branch, w_out, g_norm_ffn2, w_ffn2_in, w_ffn2_out, g_final, loss_target, m_w_ada, m_b_ada, m_g_norm_ffn1, m_w_ffn1_in, m_w_ffn1_out, m_g_norm_mix, m_w_in, m_w_pool, m_pool_scale, m_w_pool_branch, m_w_attn_branch, m_w_out, m_g_norm_ffn2, m_w_ffn2_in, m_w_ffn2_out, m_g_final, v_w_ada, v_b_ada, v_g_norm_ffn1, v_w_ffn1_in, v_w_ffn1_out, v_g_norm_mix, v_w_in, v_w_pool, v_pool_scale, v_w_pool_branch, v_w_attn_branch, v_w_out, v_g_norm_ffn2, v_w_ffn2_in, v_w_ffn2_out, v_g_final):
    names = ["w_ada", "b_ada", "g_norm_ffn1", "w_ffn1_in", "w_ffn1_out", "g_norm_mix", "w_in", "w_pool", "pool_scale",
             "w_pool_branch", "w_attn_branch", "w_out", "g_norm_ffn2", "w_ffn2_in", "w_ffn2_out", "g_final"]
    w = dict(w_ada=w_ada, b_ada=b_ada, g_norm_ffn1=g_norm_ffn1, w_ffn1_in=w_ffn1_in, w_ffn1_out=w_ffn1_out,
             g_norm_mix=g_norm_mix, w_in=w_in, w_pool=w_pool, pool_scale=pool_scale, w_pool_branch=w_pool_branch,
             w_attn_branch=w_attn_branch, w_out=w_out, g_norm_ffn2=g_norm_ffn2, w_ffn2_in=w_ffn2_in,
             w_ffn2_out=w_ffn2_out, g_final=g_final)
    m = dict(w_ada=m_w_ada, b_ada=m_b_ada, g_norm_ffn1=m_g_norm_ffn1, w_ffn1_in=m_w_ffn1_in, w_ffn1_out=m_w_ffn1_out,
             g_norm_mix=m_g_norm_mix, w_in=m_w_in, w_pool=m_w_pool, pool_scale=m_pool_scale,
             w_pool_branch=m_w_pool_branch, w_attn_branch=m_w_attn_branch, w_out=m_w_out, g_norm_ffn2=m_g_norm_ffn2,
             w_ffn2_in=m_w_ffn2_in, w_ffn2_out=m_w_ffn2_out, g_final=m_g_final)
    v = dict(w_ada=v_w_ada, b_ada=v_b_ada, g_norm_ffn1=v_g_norm_ffn1, w_ffn1_in=v_w_ffn1_in, w_ffn1_out=v_w_ffn1_out,
             g_norm_mix=v_g_norm_mix, w_in=v_w_in, w_pool=v_w_pool, pool_scale=v_pool_scale,
             w_pool_branch=v_w_pool_branch, w_attn_branch=v_w_attn_branch, w_out=v_w_out, g_norm_ffn2=v_g_norm_ffn2,
             w_ffn2_in=v_w_ffn2_in, w_ffn2_out=v_w_ffn2_out, g_final=v_g_final)
    shapes = {n: w[n].shape for n in names}
    me = 4 * lax.axis_index("x") + 2 * lax.axis_index("y") + lax.axis_index("c")
    D = x.shape[-1]
    n_mod = w_ada.shape[-1] * N_DEV // D

    (c_all,) = _all_gather([c.reshape(D // 128, 128)], "gather_c")
    ada_cols = w_ada.shape[-1]
    b_mine = lax.dynamic_slice_in_dim(b_ada, me * ada_cols, ada_cols, axis=1)
    cond, mod_part = _ada_mod(c_all.reshape(N_DEV, D), w_ada[0], b_mine, "ada_mod")
    (mod_all,) = _all_gather([mod_part.reshape(-1, 128)], "gather_mod")
    mod_all = mod_all.reshape(N_DEV, N_DEV, ada_cols)
    mod = lax.dynamic_index_in_dim(mod_all, me, axis=1, keepdims=False).reshape(n_mod, D)

    def local(t, name):
        return t[name][0].T if name in TRANSPOSED else t[name][0]

    shards = [local(w, name).astype(BF16) for name in SHARDED]
    gathered = {name: lax.dynamic_update_index_in_dim(full, shard, me, axis=0)
                for name, shard, full in zip(SHARDED, shards, _all_gather(shards, "gather_weights", own=False))}
    n_out = gathered["w_ffn1_out"].shape
    ffn_out = (N_DEV // 2, 2 * n_out[1], n_out[2])
    W = dict(w1in=gathered["w_ffn1_in"], w1out=gathered["w_ffn1_out"].reshape(ffn_out),
             win=gathered["w_in"].reshape(-1, D), wpb=_cols_to_full(gathered["w_pool_branch"]),
             wab=_cols_to_full(gathered["w_attn_branch"]), wout=gathered["w_out"].reshape(D, D),
             w2in=gathered["w_ffn2_in"], w2out=gathered["w_ffn2_out"].reshape(ffn_out))
    small = dict(g1=g_norm_ffn1[0], g2=g_norm_mix[0], g3=g_norm_ffn2[0], gf=g_final, w_pool=w_pool[0],
                 pool_scale=pool_scale[0])

    loss_part, dx, G, dmod, dsmall = _local_step(x[0], loss_target[0], positions[0], mod, small, W)

    by_owner = dict(w_ffn1_in=G["w1in"], w_ffn1_out=G["w1out"].reshape(n_out),
                    w_in=G["win"].reshape(gathered["w_in"].shape),
                    w_pool_branch=_full_to_cols(G["wpb"]), w_attn_branch=_full_to_cols(G["wab"]),
                    w_out=G["wout"].reshape(gathered["w_out"].shape),
                    w_ffn2_in=G["w2in"], w_ffn2_out=G["w2out"].reshape(n_out))
    core = lax.axis_index("c").astype(jnp.int32).reshape(1)
    chip = (2 * lax.axis_index("x") + lax.axis_index("y")).astype(jnp.int32).reshape(1)
    mine = [by_owner[name].reshape((4, 2) + by_owner[name].shape[1:]) for name in SHARDED]
    theirs = _pair_exchange(mine, "reduce_pair")
    chip_parts = [_sum_pair(core, g, r, "sum_pair_" + name) for name, g, r in zip(SHARDED, mine, theirs)]
    others = dict(zip(SHARDED, _chip_exchange(chip_parts, "reduce_chips")))
    chip_parts = dict(zip(SHARDED, chip_parts))

    part = _pack_small(dict(b_ada=dmod, g_norm_ffn1=dsmall["g1"], g_norm_mix=dsmall["g2"], g_norm_ffn2=dsmall["g3"],
                            g_final=dsmall["gf"], w_pool=dsmall["w_pool"], pool_scale=dsmall["pool_scale"]),
                       loss=loss_part)
    (parts,) = _all_gather([part], "gather_small")
    gsmall, loss = _unpack_small(_sum_slots(parts, "sum_small"), shapes)
    rows_mine = ada_cols // 128
    dmod_mine = lax.dynamic_slice_in_dim(parts, me * rows_mine, rows_mine, axis=1).reshape(N_DEV, ada_cols)

    grads, delta, new_m, new_v = {}, {}, {}, {}
    grads["w_ada"], delta["w_ada"], new_m["w_ada"], new_v["w_ada"] = (
        t[None] for t in _ada_grad_adamw(cond.T, dmod_mine, w_ada[0], m_w_ada[0], v_w_ada[0], "ada_grad_adamw"))
    for name in SHARDED:
        res = _sum_adamw(chip, chip_parts[name], others[name], local(w, name), local(m, name), local(v, name),
                         "adamw_" + name)
        grads[name], delta[name], new_m[name], new_v[name] = (
            (t.T if name in TRANSPOSED else t)[None] for t in res)
    sd, sm, sv = _adamw(_pack_small(w), _pack_small(gsmall), _pack_small(m), _pack_small(v), "adamw_small")
    for dst, src in ((delta, sd), (new_m, sm), (new_v, sv)):
        dst.update(_unpack_small(src, shapes)[0])
    grads.update(gsmall)

    return (loss, dx[None], *[grads[n] for n in names], *[delta[n] for n in names],
            *[new_m[n] for n in names], *[new_v[n] for n in names])
```

```python
import functools

import jax
import jax.numpy as jnp
from jax import lax
from jax.experimental import pallas as pl
from jax.experimental.pallas import tpu as pltpu

F32 = jnp.float32
BF16 = jnp.bfloat16
MESH = pl.DeviceIdType.MESH
ANY = pl.BlockSpec(memory_space=pl.ANY)

N_DEV = 8
EPS = 1e-6
HEAD_DIM = 64
HEADS = 4
GW = HEADS * HEAD_DIM
DILATIONS = (1, 4, 16)
BAND = 128
QB = 128
POOL_WINDOWS = (2, 4, 8, 16)
HALO = 16
ROPE_THETA = 10000.0

ADAM_LR = 0.001
ADAM_B1 = 0.9
ADAM_B2 = 0.999
ADAM_EPS = 1e-08
ADAM_WD = 0.01
ADAM_STEP = 10

VMEM_LIMIT = 56 * 1024 * 1024
TS = 256
FC = 256

NT = (((1,), (1,)), ((), ()))
TN = (((0,), (0,)), ((), ()))


def _params(**kw):
    return pltpu.CompilerParams(vmem_limit_bytes=VMEM_LIMIT, **kw)


def _dot(a, b):
    return jnp.dot(a, b, preferred_element_type=F32)


def _dot_nt(a, b):
    return lax.dot_general(a, b, NT, preferred_element_type=F32)


def _dot_tn(a, b):
    return lax.dot_general(a, b, TN, preferred_element_type=F32)


def _load_weights(pairs, sem):
    @pl.when(pl.program_id(0) == 0)
    def _():
        copies = [pltpu.make_async_copy(src, dst, sem.at[i]) for i, (src, dst) in enumerate(pairs)]
        for cp in copies:
            cp.start()
        for cp in copies:
            cp.wait()


def _norm_mod(x, g, sc, sh):
    r = lax.rsqrt(jnp.mean(x * x, axis=-1, keepdims=True) + EPS)
    xn = x * r
    y = xn * g
    return r, xn, y, y * (1.0 + sc) + sh


def _norm_mod_bwd(du, r, xn, y, g, sc):
    dsh = jnp.sum(du, axis=0, keepdims=True)
    dsc = jnp.sum(du * y, axis=0, keepdims=True)
    dy = du * (1.0 + sc)
    dg = jnp.sum(dy * xn, axis=0, keepdims=True)
    dxn = dy * g
    dx = r * (dxn - xn * jnp.mean(dxn * xn, axis=-1, keepdims=True))
    return dx, dsh, dsc, dg


def _row_tile(ts, width):
    return pl.BlockSpec((ts, width), lambda i: (i, 0))


def _const(shape):
    return pl.BlockSpec(shape, lambda *_: (0,) * len(shape))


def _ffn_fwd(h, vec, win, wout, name, rider=None):
    S, D = h.shape
    nsh, fs, _ = win.shape
    nch = nsh // 2

    def body(*refs):
        if rider is None:
            return compute(*refs)
        host, mine = rider.split(refs, 4, 5)
        rider.head(mine, pl.program_id(0))
        compute(*host)
        rider.tail(mine, pl.program_id(0), S // TS)

    def compute(h_ref, vec_ref, win_hbm, wout_hbm, hn_ref, u_ref, ab_ref, act_ref, f_ref, win_v, wout_v, sem):
        _load_weights([(win_hbm, win_v), (wout_hbm, wout_v)], sem)
        x = h_ref[...]
        g, sh, sc, gt = (vec_ref[k:k + 1, :] for k in range(4))
        _, _, _, u = _norm_mod(x, g, sc, sh)
        ub = u.astype(BF16)
        u_ref[...] = ub
        acc = jnp.zeros((TS, D), F32)
        for j in range(nch):
            a = _dot_nt(ub, win_v[j])
            b = _dot_nt(ub, win_v[nch + j])
            act = ((a * jax.nn.sigmoid(a)) * b).astype(BF16)
            ab_ref[j] = a.astype(BF16)
            ab_ref[nch + j] = b.astype(BF16)
            act_ref[j] = act
            acc = acc + _dot(act, wout_v[j])
        f_ref[...] = acc
        hn_ref[...] = x + (0.5 * gt) * acc

    specs = (
        [_row_tile(TS, D), _const((8, D)), ANY, ANY],
        [_row_tile(TS, D), _row_tile(TS, D), pl.BlockSpec((nsh, TS, fs), lambda i: (0, i, 0)),
         pl.BlockSpec((nch, TS, fs), lambda i: (0, i, 0)), _row_tile(TS, D)],
        [jax.ShapeDtypeStruct((S, D), F32), jax.ShapeDtypeStruct((S, D), BF16),
         jax.ShapeDtypeStruct((nsh, S, fs), BF16), jax.ShapeDtypeStruct((nch, S, fs), BF16),
         jax.ShapeDtypeStruct((S, D), F32)],
        [pltpu.VMEM(win.shape, BF16), pltpu.VMEM(wout.shape, BF16), pltpu.SemaphoreType.DMA((2,))])
    in_specs, out_specs, out_shape, scratch = specs if rider is None else rider.specs(*specs)
    outs = pl.pallas_call(
        body, name=name, grid=(S // TS,), in_specs=in_specs, out_specs=out_specs, out_shape=out_shape,
        scratch_shapes=scratch, compiler_params=_params(),
    )(h, vec, win, wout, *(rider.arrays if rider else []))
    return outs if rider is None else (outs[:5], outs[5:])


def _ffn_bwd(dh, h, f, ab, vec, win, wout, name):
    S, D = h.shape
    nsh, fs, _ = win.shape
    nch = nsh // 2

    def body(dh_ref, h_ref, f_ref, ab_ref, vec_ref, win_hbm, wout_hbm,
             dhp_ref, dab_ref, df_ref, red_ref, win_v, wout_v, sem):
        _load_weights([(win_hbm, win_v), (wout_hbm, wout_v)], sem)

        @pl.when(pl.program_id(0) == 0)
        def _():
            red_ref[...] = jnp.zeros_like(red_ref)

        dh_v = dh_ref[...]
        x = h_ref[...]
        g, sh, sc, gt = (vec_ref[k:k + 1, :] for k in range(4))
        dgt = jnp.sum((0.5 * f_ref[...]) * dh_v, axis=0, keepdims=True)
        dfb = ((0.5 * gt) * dh_v).astype(BF16)
        df_ref[...] = dfb
        du = jnp.zeros((TS, D), F32)
        for j in range(nch):
            dact = _dot_nt(dfb, wout_v[j])
            av = ab_ref[j].astype(F32)
            bv = ab_ref[nch + j].astype(F32)
            sg = jax.nn.sigmoid(av)
            da = (dact * bv * (sg * (1.0 + av * (1.0 - sg)))).astype(BF16)
            db = (dact * (av * sg)).astype(BF16)
            dab_ref[j] = da
            dab_ref[nch + j] = db
            du = du + _dot(da, win_v[j]) + _dot(db, win_v[nch + j])
        r, xn, y, _ = _norm_mod(x, g, sc, sh)
        dx, dsh, dsc, dg = _norm_mod_bwd(du, r, xn, y, g, sc)
        dhp_ref[...] = dh_v + dx
        red_ref[0:1, :] += dgt
        red_ref[1:2, :] += dsh
        red_ref[2:3, :] += dsc
        red_ref[3:4, :] += dg

    ab_spec = pl.BlockSpec((nsh, TS, fs), lambda i: (0, i, 0))
    return pl.pallas_call(
        body, name=name, grid=(S // TS,),
        in_specs=[_row_tile(TS, D), _row_tile(TS, D), _row_tile(TS, D), ab_spec, _const((8, D)), ANY, ANY],
        out_specs=[_row_tile(TS, D), ab_spec, _row_tile(TS, D), _const((8, D))],
        out_shape=[jax.ShapeDtypeStruct((S, D), F32), jax.ShapeDtypeStruct((nsh, S, fs), BF16),
                   jax.ShapeDtypeStruct((S, D), BF16), jax.ShapeDtypeStruct((8, D), F32)],
        scratch_shapes=[pltpu.VMEM(win.shape, BF16), pltpu.VMEM(wout.shape, BF16), pltpu.SemaphoreType.DMA((2,))],
        compiler_params=_params(),
    )(dh, h, f, ab, vec, win, wout)


def _wgrad(x, y, name, tm=None, ts=2048, rider=None):
    xb = x.ndim == 3
    nb = x.shape[0] if xb else 0
    S, M = x.shape[-2:]
    N = y.shape[-1]
    tm = tm or M
    ts = min(ts, S)
    nk = S // ts
    grid = (max(nb, 1), M // tm, nk)

    def body(*refs):
        if rider is None:
            return compute(*refs)
        host, mine = rider.split(refs, 2, 1)
        step = (pl.program_id(0) * grid[1] + pl.program_id(1)) * grid[2] + pl.program_id(2)
        rider.head(mine, step)
        compute(*host)
        rider.tail(mine, step, grid[0] * grid[1] * grid[2])

    def compute(x_ref, y_ref, o_ref, acc):
        k = pl.program_id(2)

        @pl.when(k == 0)
        def _():
            acc[...] = jnp.zeros_like(acc)

        acc[...] += _dot_tn(x_ref[...], y_ref[...])

        @pl.when(k == nk - 1)
        def _():
            o_ref[...] = acc[...].astype(BF16)

    x_spec = (pl.BlockSpec((None, ts, tm), lambda b, i, k: (b, k, i)) if xb
              else pl.BlockSpec((ts, tm), lambda b, i, k: (k, i)))
    y_spec = pl.BlockSpec((ts, N), lambda b, i, k: (k, 0))
    if xb:
        o_spec, o_shape = pl.BlockSpec((None, tm, N), lambda b, i, k: (b, i, 0)), (nb, M, N)
    else:
        o_spec, o_shape = pl.BlockSpec((tm, N), lambda b, i, k: (i, 0)), (M, N)
    specs = ([x_spec, y_spec], [o_spec], [jax.ShapeDtypeStruct(o_shape, BF16)], [pltpu.VMEM((tm, N), F32)])
    in_specs, out_specs, out_shape, scratch = specs if rider is None else rider.specs(*specs)
    outs = pl.pallas_call(
        body, name=name, grid=grid, in_specs=in_specs, out_specs=out_specs, out_shape=out_shape,
        scratch_shapes=scratch, compiler_params=_params(),
    )(x, y, *(rider.arrays if rider else []))
    return outs[0] if rider is None else (outs[0], outs[1:])


P_OFF, Q_OFF, K_OFF, V_OFF, G_OFF = 0, 256, 1024, 1792, 2560
IN_WIDTH = 4608


def _first_half_mask(ts):
    lane = lax.broadcasted_iota(jnp.int32, (ts, 128), 1)
    return (lane % HEAD_DIM) < (HEAD_DIM // 2)


def _rope(t, cos, sin_signed, first, sign):
    partner = jnp.where(first, pltpu.roll(t, 96, 1), pltpu.roll(t, 32, 1))
    return t * cos + sign * (partner * sin_signed)


def _mix_in_fwd(h, vec, cos, sin, win, name):
    S, D = h.shape
    grp = jax.ShapeDtypeStruct((S, GW), BF16)

    def body(h_ref, vec_ref, cos_ref, sin_ref, win_hbm, u_ref, p_ref, gates_ref, *rest):
        qkv_refs, (win_v, sem) = rest[:9], rest[9:]
        _load_weights([(win_hbm, win_v)], sem)
        g, sh, sc = (vec_ref[k:k + 1, :] for k in range(3))
        _, _, _, u = _norm_mod(h_ref[...], g, sc, sh)
        ub = u.astype(BF16)
        u_ref[...] = ub
        p_ref[...] = _dot_nt(ub, win_v[P_OFF:Q_OFF, :])
        cosv, sinv = cos_ref[...], sin_ref[...]
        first = _first_half_mask(TS)
        for which, off in enumerate((Q_OFF, K_OFF, V_OFF)):
            t = _dot_nt(ub, win_v[off:off + 3 * GW, :])
            for gi in range(3):
                for half in range(2):
                    c0 = gi * GW + half * 128
                    piece = t[:, c0:c0 + 128]
                    if which < 2:
                        piece = _rope(piece, cosv, sinv, first, 1.0)
                    qkv_refs[which * 3 + gi][:, half * 128:(half + 1) * 128] = piece.astype(BF16)
        gates_ref[...] = jax.nn.sigmoid(_dot_nt(ub, win_v[G_OFF:IN_WIDTH, :]))

    return pl.pallas_call(
        body, name=name, grid=(S // TS,),
        in_specs=[_row_tile(TS, D), _const((8, D)), _row_tile(TS, 128), _row_tile(TS, 128), ANY],
        out_specs=[_row_tile(TS, D), _row_tile(TS, GW), _row_tile(TS, 2 * D)] + [_row_tile(TS, GW)] * 9,
        out_shape=[jax.ShapeDtypeStruct((S, D), BF16), jax.ShapeDtypeStruct((S, GW), F32),
                   jax.ShapeDtypeStruct((S, 2 * D), F32)] + [grp] * 9,
        scratch_shapes=[pltpu.VMEM((IN_WIDTH, D), BF16), pltpu.SemaphoreType.DMA((1,))],
        compiler_params=_params(),
    )(h, vec, cos, sin, win)


def _mix_in_bwd(dh, h, vec, cos, sin, dp, dqkv, dgl, win, name):
    S, D = h.shape

    def body(dh_ref, h_ref, vec_ref, cos_ref, sin_ref, dp_ref, *rest):
        dqkv_refs = rest[:9]
        dgl_ref, win_hbm, dhp_ref, dproj_ref, red_ref, win_v, sem = rest[9:]
        _load_weights([(win_hbm, win_v)], sem)

        @pl.when(pl.program_id(0) == 0)
        def _():
            red_ref[...] = jnp.zeros_like(red_ref)

        cosv, sinv = cos_ref[...], sin_ref[...]
        first = _first_half_mask(TS)
        dproj_ref[:, P_OFF:Q_OFF] = dp_ref[...].astype(BF16)
        for which, off in enumerate((Q_OFF, K_OFF, V_OFF)):
            for gi in range(3):
                for half in range(2):
                    piece = dqkv_refs[which * 3 + gi][:, half * 128:(half + 1) * 128]
                    if which < 2:
                        piece = _rope(piece, cosv, sinv, first, -1.0)
                    c0 = off + gi * GW + half * 128
                    dproj_ref[:, c0:c0 + 128] = piece.astype(BF16)
        dproj_ref[:, G_OFF:IN_WIDTH] = dgl_ref[...]
        du = _dot(dproj_ref[...], win_v[...])
        g, sh, sc = (vec_ref[k:k + 1, :] for k in range(3))
        r, xn, y, _ = _norm_mod(h_ref[...], g, sc, sh)
        dx, dsh, dsc, dg = _norm_mod_bwd(du, r, xn, y, g, sc)
        dhp_ref[...] = dh_ref[...] + dx
        red_ref[1:2, :] += dsh
        red_ref[2:3, :] += dsc
        red_ref[3:4, :] += dg

    return pl.pallas_call(
        body, name=name, grid=(S // TS,),
        in_specs=[_row_tile(TS, D), _row_tile(TS, D), _const((8, D)), _row_tile(TS, 128), _row_tile(TS, 128),
                  _row_tile(TS, GW)] + [_row_tile(TS, GW)] * 9 + [_row_tile(TS, 2 * D), ANY],
        out_specs=[_row_tile(TS, D), _row_tile(TS, IN_WIDTH), _const((8, D))],
        out_shape=[jax.ShapeDtypeStruct((S, D), F32), jax.ShapeDtypeStruct((S, IN_WIDTH), BF16),
                   jax.ShapeDtypeStruct((8, D), F32)],
        scratch_shapes=[pltpu.VMEM((IN_WIDTH, D), BF16), pltpu.SemaphoreType.DMA((1,))],
        compiler_params=_params(),
    )(dh, h, vec, cos, sin, dp, *dqkv, dgl, win)


def _pool_lanes(rows):
    lane = lax.broadcasted_iota(jnp.int32, (rows, GW), 1)
    return lane // HEAD_DIM


def _pool_window(rows):
    grp = _pool_lanes(rows)
    w = jnp.full((rows, GW), POOL_WINDOWS[0], jnp.int32)
    for k in range(1, len(POOL_WINDOWS)):
        w = jnp.where(grp == k, POOL_WINDOWS[k], w)
    return grp, w


def _pool_fwd(p, wbd, scale, name, ts=512):
    S = p.shape[0]
    ext = ts + HALO

    def body(pc_ref, ph_ref, wbd_ref, sc_ref, d_ref, y_ref):
        i = pl.program_id(0)
        cur = pc_ref[...]
        halo = jnp.where(i > 0, ph_ref[...], 0.0)
        s = jnp.concatenate([halo, cur], axis=0)
        grp, w = _pool_window(ext)
        sel = jnp.zeros((ext, GW), F32)
        for k, wk in enumerate(POOL_WINDOWS):
            s = s + pltpu.roll(s, wk // 2, 0)
            sel = jnp.where(grp == k, s, sel)
        t = i * ts + lax.broadcasted_iota(jnp.int32, (ts, GW), 0)
        count = jnp.minimum(t + 1, w[HALO:]).astype(F32)
        d = (sel[HALO:] / count - cur).astype(BF16)
        d_ref[...] = d
        y_ref[...] = (_dot(d, wbd_ref[...]) * sc_ref[...]).astype(BF16)

    return pl.pallas_call(
        body, name=name, grid=(S // ts,),
        in_specs=[_row_tile(ts, GW),
                  pl.BlockSpec((HALO, GW), lambda i: (jnp.maximum(i * (ts // HALO) - 1, 0), 0)),
                  _const((GW, GW)), _const((1, GW))],
        out_specs=[_row_tile(ts, GW), _row_tile(ts, GW)],
        out_shape=[jax.ShapeDtypeStruct((S, GW), BF16), jax.ShapeDtypeStruct((S, GW), BF16)],
        compiler_params=_params(),
    )(p, p, wbd, scale)


def _pool_bwd(dy, d, wbd, scale, name, ts=512):
    S = dy.shape[0]
    ext = ts + HALO
    nsteps = S // ts
    last_halo = S // HALO - 1

    def body(dyc_ref, dyh_ref, d_ref, wbd_ref, sc_ref, dp_ref, dw_ref, ds_ref):
        i = pl.program_id(0)

        @pl.when(i == 0)
        def _():
            dw_ref[...] = jnp.zeros_like(dw_ref)
            ds_ref[...] = jnp.zeros_like(ds_ref)

        dyc = dyc_ref[...]
        dyh = jnp.where(i < nsteps - 1, dyh_ref[...], 0.0)
        dys = (jnp.concatenate([dyc, dyh], axis=0) * sc_ref[...]).astype(BF16)
        dd = _dot_nt(dys, wbd_ref[...])
        grp, w = _pool_window(ext)
        t = i * ts + lax.broadcasted_iota(jnp.int32, (ext, GW), 0)
        s = dd / jnp.minimum(t + 1, w).astype(F32)
        sel = jnp.zeros((ext, GW), F32)
        for k, wk in enumerate(POOL_WINDOWS):
            s = s + pltpu.roll(s, ext - wk // 2, 0)
            sel = jnp.where(grp == k, s, sel)
        dp_ref[...] = sel[:ts] - dd[:ts]
        dv = d_ref[...]
        z = _dot(dv, wbd_ref[...])
        ds_ref[0:1, :] += jnp.sum(dyc * z, axis=0, keepdims=True)
        dw_ref[...] += _dot_tn(dv, dys[:ts])

    return pl.pallas_call(
        body, name=name, grid=(nsteps,),
        in_specs=[_row_tile(ts, GW),
                  pl.BlockSpec((HALO, GW), lambda i: (jnp.minimum((i + 1) * (ts // HALO), last_halo), 0)),
                  _row_tile(ts, GW), _const((GW, GW)), _const((1, GW))],
        out_specs=[_row_tile(ts, GW), _const((GW, GW)), _const((8, GW))],
        out_shape=[jax.ShapeDtypeStruct((S, GW), F32), jax.ShapeDtypeStruct((GW, GW), F32),
                   jax.ShapeDtypeStruct((8, GW), F32)],
        compiler_params=_params(),
    )(dy, dy, d, wbd, scale)


def _head_id(rows):
    return lax.broadcasted_iota(jnp.int32, (rows, GW), 1) // HEAD_DIM


def _stack_heads(t, hid):
    return jnp.concatenate([jnp.where(hid == h, t, jnp.zeros_like(t)) for h in range(HEADS)], axis=0)


def _unstack_heads(t_all, hid):
    out = jnp.zeros((QB, GW), F32)
    for h in range(HEADS):
        out = jnp.where(hid == h, t_all[h * QB:(h + 1) * QB], out)
    return out


def _band_mask(n):
    row = lax.broadcasted_iota(jnp.int32, (HEADS * QB, 2 * QB), 0) % QB
    col = lax.broadcasted_iota(jnp.int32, (HEADS * QB, 2 * QB), 1)
    rel = row + QB - col
    return (rel >= 0) & (rel <= BAND) & ((col >= QB) | (n > 0))


def _attn_fwd(q, k, v, r, name):
    L = q.shape[0]
    nb = L // QB
    cur = pl.BlockSpec((QB, GW), lambda res, n: (n, res))
    prev = pl.BlockSpec((QB, GW), lambda res, n: (jnp.maximum(n - 1, 0), res))

    def body(q_ref, kp_ref, kc_ref, vp_ref, vc_ref, o_ref, lse_ref):
        n = pl.program_id(1)
        hid = _head_id(QB)
        qs = _stack_heads(q_ref[...], hid)
        kc = jnp.concatenate([kp_ref[...], kc_ref[...]], axis=0)
        vc = jnp.concatenate([vp_ref[...], vc_ref[...]], axis=0)
        s = _dot_nt(qs, kc) * (HEAD_DIM ** -0.5)
        s = jnp.where(_band_mask(n), s, -jnp.inf)
        m = jnp.max(s, axis=-1, keepdims=True)
        lse = m + jnp.log(jnp.sum(jnp.exp(s - m), axis=-1, keepdims=True))
        pr = jnp.exp(s - lse).astype(BF16)
        o_ref[...] = _unstack_heads(_dot(pr, vc), hid)
        lse_ref[...] = _unstack_heads(jnp.broadcast_to(lse, (HEADS * QB, GW)), hid)

    return pl.pallas_call(
        body, name=name, grid=(r, nb),
        in_specs=[cur, prev, cur, prev, cur], out_specs=[cur, cur],
        out_shape=[jax.ShapeDtypeStruct(q.shape, F32), jax.ShapeDtypeStruct(q.shape, F32)],
        compiler_params=_params(),
    )(q, k, k, v, v)


def _head_rows(t_full, hid):
    return jnp.concatenate(
        [jnp.max(jnp.where(hid == h, t_full, -jnp.inf), axis=-1, keepdims=True) for h in range(HEADS)], axis=0)


def _attn_bwd(q, k, v, do, lse, cterm, r, name):
    L = q.shape[0]
    nb = L // QB
    qside = pl.BlockSpec((QB, GW), lambda res, n: (jnp.minimum(n, nb - 1), res))
    kcur = qside
    kprev = pl.BlockSpec((QB, GW), lambda res, n: (jnp.clip(n - 1, 0, nb - 1), res))
    kout = pl.BlockSpec((QB, GW), lambda res, n: (jnp.maximum(n - 1, 0), res))

    def body(q_ref, do_ref, lse_ref, c_ref, kp_ref, kc_ref, vp_ref, vc_ref,
             dq_ref, dk_ref, dv_ref, carry_k, carry_v):
        n = pl.program_id(1)

        @pl.when(n == 0)
        def _():
            carry_k[...] = jnp.zeros_like(carry_k)
            carry_v[...] = jnp.zeros_like(carry_v)

        @pl.when(n < nb)
        def _():
            hid = _head_id(QB)
            qs = _stack_heads(q_ref[...], hid)
            dos = _stack_heads(do_ref[...], hid)
            kc = jnp.concatenate([kp_ref[...], kc_ref[...]], axis=0)
            vc = jnp.concatenate([vp_ref[...], vc_ref[...]], axis=0)
            s = _dot_nt(qs, kc) * (HEAD_DIM ** -0.5)
            s = jnp.where(_band_mask(n), s, -jnp.inf)
            p = jnp.exp(s - _head_rows(lse_ref[...], hid))
            dp = _dot_nt(dos, vc)
            ds = (p * (dp + _head_rows(c_ref[...], hid)) * (HEAD_DIM ** -0.5)).astype(BF16)
            dq_ref[...] = _unstack_heads(_dot(ds, kc), hid)
            dkc = _dot_tn(ds, qs)
            dvc = _dot_tn(p.astype(BF16), dos)
            dk_ref[...] = carry_k[...] + dkc[:QB]
            dv_ref[...] = carry_v[...] + dvc[:QB]
            carry_k[...] = dkc[QB:]
            carry_v[...] = dvc[QB:]

        @pl.when(n == nb)
        def _():
            dk_ref[...] = carry_k[...]
            dv_ref[...] = carry_v[...]

    out = jax.ShapeDtypeStruct(q.shape, F32)
    return pl.pallas_call(
        body, name=name, grid=(r, nb + 1),
        in_specs=[qside, qside, qside, qside, kprev, kcur, kprev, kcur],
        out_specs=[qside, kout, kout], out_shape=[out, out, out],
        scratch_shapes=[pltpu.VMEM((QB, GW), F32), pltpu.VMEM((QB, GW), F32)],
        compiler_params=_params(),
    )(q, do, lse, cterm, k, k, v, v)


def _group_weights(lse_refs):
    l0, l1, l2 = (ref[...] for ref in lse_refs)
    m = jnp.maximum(jnp.maximum(l0, l1), l2)
    e = [jnp.exp(l - m) for l in (l0, l1, l2)]
    den = e[0] + e[1] + e[2]
    return [ei / den for ei in e]


def _mix_out_fwd(h, vec, gates, ypool, o3, lse3, wpb, wab, wout, name):
    S, D = h.shape

    def body(h_ref, vec_ref, gates_ref, yp_ref, o0, o1, o2, l0, l1, l2, wpb_hbm, wab_hbm, wout_hbm,
             hn_ref, ya_ref, merged_ref, tm_ref, wpb_v, wab_v, wout_v, sem):
        _load_weights([(wpb_hbm, wpb_v), (wab_hbm, wab_v), (wout_hbm, wout_v)], sem)
        gt = vec_ref[3:4, :]
        wts = _group_weights((l0, l1, l2))
        ya = (wts[0] * o0[...] + wts[1] * o1[...] + wts[2] * o2[...]).astype(BF16)
        ya_ref[...] = ya
        merged = (gates_ref[:, :D] * _dot(yp_ref[...], wpb_v[...])
                  + gates_ref[:, D:] * _dot(ya, wab_v[...])).astype(BF16)
        merged_ref[...] = merged
        tm = _dot(merged, wout_v[...])
        tm_ref[...] = tm
        hn_ref[...] = h_ref[...] + gt * tm

    grp = _row_tile(TS, GW)
    return pl.pallas_call(
        body, name=name, grid=(S // TS,),
        in_specs=[_row_tile(TS, D), _const((8, D)), _row_tile(TS, 2 * D), grp] + [grp] * 6 + [ANY, ANY, ANY],
        out_specs=[_row_tile(TS, D), grp, _row_tile(TS, D), _row_tile(TS, D)],
        out_shape=[jax.ShapeDtypeStruct((S, D), F32), jax.ShapeDtypeStruct((S, GW), BF16),
                   jax.ShapeDtypeStruct((S, D), BF16), jax.ShapeDtypeStruct((S, D), F32)],
        scratch_shapes=[pltpu.VMEM((GW, D), BF16), pltpu.VMEM((GW, D), BF16), pltpu.VMEM((D, D), BF16),
                        pltpu.SemaphoreType.DMA((3,))],
        compiler_params=_params(),
    )(h, vec, gates, ypool, *o3, *lse3, wpb, wab, wout)


def _mix_out_bwd(dh, tm, vec, gates, ypool, o3, lse3, wpb, wab, wout, name, rider=None):
    S, D = dh.shape

    def body(*refs):
        if rider is None:
            return compute(*refs)
        host, mine = rider.split(refs, 14, 12)
        rider.head(mine, pl.program_id(0))
        compute(*host)
        rider.tail(mine, pl.program_id(0), S // TS)

    def compute(dh_ref, tm_ref, vec_ref, gates_ref, yp_ref, o0, o1, o2, l0, l1, l2, wpb_hbm, wab_hbm, wout_hbm,
                dtm_ref, dgl_ref, dypb_ref, dyab_ref, dyp_ref, do0, do1, do2, c0, c1, c2, red_ref,
                wpb_v, wab_v, wout_v, sem):
        _load_weights([(wpb_hbm, wpb_v), (wab_hbm, wab_v), (wout_hbm, wout_v)], sem)

        @pl.when(pl.program_id(0) == 0)
        def _():
            red_ref[...] = jnp.zeros_like(red_ref)

        gt = vec_ref[3:4, :]
        dh_v = dh_ref[...]
        red_ref[0:1, :] += jnp.sum(tm_ref[...] * dh_v, axis=0, keepdims=True)
        dtm = (gt * dh_v).astype(BF16)
        dtm_ref[...] = dtm
        dm = _dot_nt(dtm, wout_v[...])
        wts = _group_weights((l0, l1, l2))
        ya = wts[0] * o0[...] + wts[1] * o1[...] + wts[2] * o2[...]
        ypb = _dot(yp_ref[...], wpb_v[...])
        yab = _dot(ya.astype(BF16), wab_v[...])
        gp = gates_ref[:, :D]
        ga = gates_ref[:, D:]
        dgl_ref[:, :D] = (dm * ypb * gp * (1.0 - gp)).astype(BF16)
        dgl_ref[:, D:] = (dm * yab * ga * (1.0 - ga)).astype(BF16)
        dypb = (dm * gp).astype(BF16)
        dyab = (dm * ga).astype(BF16)
        dypb_ref[...] = dypb
        dyab_ref[...] = dyab
        dyp_ref[...] = _dot_nt(dypb, wpb_v[...])
        dya = _dot_nt(dyab, wab_v[...])
        row = lax.broadcasted_iota(jnp.int32, (GW, GW), 0) // HEAD_DIM
        col = lax.broadcasted_iota(jnp.int32, (GW, GW), 1) // HEAD_DIM
        ones = jnp.where(row == col, 1.0, 0.0).astype(F32)
        tot = jnp.dot(dya * ya, ones, preferred_element_type=F32, precision=lax.Precision.HIGHEST)
        for wg, do_ref, c_ref in zip(wts, (do0, do1, do2), (c0, c1, c2)):
            do_ref[...] = (wg * dya).astype(BF16)
            c_ref[...] = -(wg * tot)

    grp = _row_tile(TS, GW)
    gb = jax.ShapeDtypeStruct((S, GW), BF16)
    gf = jax.ShapeDtypeStruct((S, GW), F32)
    specs = (
        [_row_tile(TS, D), _row_tile(TS, D), _const((8, D)), _row_tile(TS, 2 * D), grp] + [grp] * 6
        + [ANY, ANY, ANY],
        [_row_tile(TS, D), _row_tile(TS, 2 * D), _row_tile(TS, D), _row_tile(TS, D), grp]
        + [grp] * 6 + [_const((8, D))],
        [jax.ShapeDtypeStruct((S, D), BF16), jax.ShapeDtypeStruct((S, 2 * D), BF16),
         jax.ShapeDtypeStruct((S, D), BF16), jax.ShapeDtypeStruct((S, D), BF16), gf,
         gb, gb, gb, gf, gf, gf, jax.ShapeDtypeStruct((8, D), F32)],
        [pltpu.VMEM((GW, D), BF16), pltpu.VMEM((GW, D), BF16), pltpu.VMEM((D, D), BF16),
         pltpu.SemaphoreType.DMA((3,))])
    in_specs, out_specs, out_shape, scratch = specs if rider is None else rider.specs(*specs)
    outs = pl.pallas_call(
        body, name=name, grid=(S // TS,), in_specs=in_specs, out_specs=out_specs, out_shape=out_shape,
        scratch_shapes=scratch, compiler_params=_params(),
    )(dh, tm, vec, gates, ypool, *o3, *lse3, wpb, wab, wout, *(rider.arrays if rider else []))
    return outs if rider is None else (outs[:12], outs[12:])


def _final_loss(h, target, gf, name):
    S, D = h.shape

    def body(h_ref, t_ref, g_ref, dh_ref, loss_ref, dg_ref):
        @pl.when(pl.program_id(0) == 0)
        def _():
            loss_ref[...] = jnp.zeros_like(loss_ref)
            dg_ref[...] = jnp.zeros_like(dg_ref)

        x = h_ref[...]
        g = g_ref[0:1, :]
        r = lax.rsqrt(jnp.mean(x * x, axis=-1, keepdims=True) + EPS)
        xn = x * r
        err = xn * g - t_ref[...]
        loss_ref[...] += 0.5 * jnp.sum(jnp.mean(err * err, axis=-1, keepdims=True))
        dy = err * (1.0 / D)
        dg_ref[0:1, :] += jnp.sum(dy * xn, axis=0, keepdims=True)
        dxn = dy * g
        dh_ref[...] = r * (dxn - xn * jnp.mean(dxn * xn, axis=-1, keepdims=True))

    return pl.pallas_call(
        body, name=name, grid=(S // TS,),
        in_specs=[_row_tile(TS, D), _row_tile(TS, D), _const((8, D))],
        out_specs=[_row_tile(TS, D), _const((8, 128)), _const((8, D))],
        out_shape=[jax.ShapeDtypeStruct((S, D), F32), jax.ShapeDtypeStruct((8, 128), F32),
                   jax.ShapeDtypeStruct((8, D), F32)],
        compiler_params=_params(),
    )(h, target, gf)


def _ada_mod(c_all, w, b, name):
    def body(c_ref, w_ref, b_ref, cond_ref, mod_ref):
        cv = c_ref[...]
        cond = cv * jax.nn.sigmoid(cv)
        cond_ref[...] = cond
        mod_ref[...] = jnp.dot(cond, w_ref[...], preferred_element_type=F32,
                               precision=lax.Precision.HIGHEST) + b_ref[...]

    return pl.pallas_call(
        body, name=name,
        out_shape=[jax.ShapeDtypeStruct(c_all.shape, F32), jax.ShapeDtypeStruct((c_all.shape[0], w.shape[1]), F32)],
        compiler_params=_params(),
    )(c_all, w, b)


def _adamw_math(w, g, m, v):
    m = ADAM_B1 * m + (1.0 - ADAM_B1) * g
    v = ADAM_B2 * v + (1.0 - ADAM_B2) * (g * g)
    m_hat = m / (1.0 - ADAM_B1 ** ADAM_STEP)
    v_hat = v / (1.0 - ADAM_B2 ** ADAM_STEP)
    delta = -ADAM_LR * (m_hat / (jnp.sqrt(v_hat) + ADAM_EPS) + ADAM_WD * w)
    return delta, m, v


def _adamw(w, g, m, v, name):
    R, C = w.shape
    tr = R
    for cand in (256, 128, 64, 32, 16, 8):
        if R % cand == 0:
            tr = cand
            break

    def body(w_ref, g_ref, m_ref, v_ref, d_ref, mo_ref, vo_ref):
        d_ref[...], mo_ref[...], vo_ref[...] = _adamw_math(w_ref[...], g_ref[...], m_ref[...], v_ref[...])

    spec = _row_tile(tr, C)
    out = jax.ShapeDtypeStruct((R, C), F32)
    return pl.pallas_call(
        body, name=name, grid=(R // tr,), in_specs=[spec] * 4, out_specs=[spec] * 3, out_shape=[out] * 3,
        compiler_params=_params(),
    )(w, g, m, v)


def _ada_grad_adamw(cond_t, dmod, w, m, v, name, tr=256):
    R, C = w.shape
    nb = dmod.shape[0]

    def body(ct_ref, dm_ref, w_ref, m_ref, v_ref, g_ref, d_ref, mo_ref, vo_ref):
        ct = ct_ref[...]
        dm = dm_ref[...]
        g = jnp.zeros((tr, C), F32)
        for bi in range(nb):
            g = g + ct[:, bi:bi + 1] * dm[bi:bi + 1, :]
        g_ref[...] = g
        d_ref[...], mo_ref[...], vo_ref[...] = _adamw_math(w_ref[...], g, m_ref[...], v_ref[...])

    spec = _row_tile(tr, C)
    out = jax.ShapeDtypeStruct((R, C), F32)
    return pl.pallas_call(
        body, name=name, grid=(R // tr,),
        in_specs=[_row_tile(tr, nb), _const((nb, C)), spec, spec, spec],
        out_specs=[spec] * 4, out_shape=[out] * 4,
        compiler_params=_params(),
    )(cond_t, dmod, w, m, v)


def _row_step(rows, cap=256):
    for cand in range(cap, 15, -16):
        if rows % cand == 0:
            return cand
    return rows


def _slot_sum(x_ref):
    acc = x_ref[0].astype(F32)
    for k in range(1, x_ref.shape[0]):
        acc = acc + x_ref[k].astype(F32)
    return acc


def _sum_slots(x, name, out_dtype=F32):
    n, R, C = x.shape
    tr = _row_step(R)

    def body(x_ref, o_ref):
        o_ref[...] = _slot_sum(x_ref).astype(out_dtype)

    return pl.pallas_call(
        body, name=name, grid=(R // tr,),
        in_specs=[pl.BlockSpec((n, tr, C), lambda i: (0, i, 0))],
        out_specs=_row_tile(tr, C), out_shape=jax.ShapeDtypeStruct((R, C), out_dtype),
        compiler_params=_params(),
    )(x)


def _sum_pair(core, g, recv, name):
    _, _, R, C = g.shape
    tr = _row_step(R)

    def body(core_ref, g_ref, r_ref, o_ref):
        o_ref[...] = (g_ref[...].astype(F32) + r_ref[...].astype(F32)).astype(BF16)

    return pl.pallas_call(
        body, name=name, out_shape=jax.ShapeDtypeStruct((4, R, C), BF16),
        grid_spec=pltpu.PrefetchScalarGridSpec(
            num_scalar_prefetch=1, grid=(4, R // tr),
            in_specs=[pl.BlockSpec((None, None, tr, C), lambda k, i, core_ref: (k, core_ref[0], i, 0)),
                      pl.BlockSpec((None, tr, C), lambda k, i, core_ref: (k, i, 0))],
            out_specs=pl.BlockSpec((None, tr, C), lambda k, i, core_ref: (k, i, 0))),
        compiler_params=_params(),
    )(core, g, recv)


def _sum_adamw(chip, own, recv, w, m, v, name):
    _, R, C = own.shape
    tr = _row_step(R)

    def body(chip_ref, own_ref, r_ref, w_ref, m_ref, v_ref, g_ref, d_ref, mo_ref, vo_ref):
        g = own_ref[...].astype(F32) + _slot_sum(r_ref)
        g_ref[...] = g
        d_ref[...], mo_ref[...], vo_ref[...] = _adamw_math(w_ref[...], g, m_ref[...], v_ref[...])

    spec = pl.BlockSpec((tr, C), lambda i, chip_ref: (i, 0))
    out = jax.ShapeDtypeStruct((R, C), F32)
    return pl.pallas_call(
        body, name=name, out_shape=[out] * 4,
        grid_spec=pltpu.PrefetchScalarGridSpec(
            num_scalar_prefetch=1, grid=(R // tr,),
            in_specs=[pl.BlockSpec((None, tr, C), lambda i, chip_ref: (chip_ref[0], i, 0)),
                      pl.BlockSpec((3, tr, C), lambda i, chip_ref: (0, i, 0)), spec, spec, spec],
            out_specs=[spec] * 4),
        compiler_params=_params(),
    )(chip, own, recv, w, m, v)


def _place():
    return lax.axis_index("x"), lax.axis_index("y"), lax.axis_index("c")


def _gather_steps(x_refs, out_refs, send_sems, recv_sems):
    n = len(x_refs)
    x, y, c = _place()
    me, sibling = (x, y, c), (x, y, 1 - c)
    chips = [(1 - x, y), (x, 1 - y), (1 - x, 1 - y)]

    def rows(a, px, py, pc):
        return out_refs[a].at[4 * px + 2 * py + pc]

    def copy(a, k, block, to, src=None):
        return pltpu.make_async_remote_copy(
            src_ref=rows(a, *block) if src is None else src, dst_ref=rows(a, *block),
            send_sem=send_sems.at[a, k], recv_sem=recv_sems.at[a, k], device_id=to, device_id_type=MESH)

    def first(a):
        return [copy(a, 0, me, sibling, src=x_refs[a])] + [
            copy(a, 1 + j, me, (*chip, c), src=x_refs[a]) for j, chip in enumerate(chips)]

    def passed(a, j):
        return copy(a, 4 + j, (*chips[j], c), sibling)

    def start():
        for a in range(n):
            for cp in first(a):
                cp.start()

    def relay():
        for j, chip in enumerate(chips):
            for a in range(n):
                copy(a, 1 + j, (*chip, c), me).wait_recv()
                passed(a, j).start()

    def finish():
        for a in range(n):
            copy(a, 0, sibling, me).wait_recv()
            for j, chip in enumerate(chips):
                copy(a, 4 + j, (*chip, 1 - c), me).wait_recv()
        for a in range(n):
            for cp in first(a) + [passed(a, j) for j in range(3)]:
                cp.wait_send()

    return start, relay, finish


def _all_gather(arrs, name, own=True):
    n = len(arrs)

    def body(*refs):
        x_refs, out_refs = refs[:n], refs[n:2 * n]
        send_sems, recv_sems, local_sems = refs[2 * n:]
        me = 4 * lax.axis_index("x") + 2 * lax.axis_index("y") + lax.axis_index("c")
        mine = [pltpu.make_async_copy(x_refs[a], out_refs[a].at[me], local_sems.at[a]) for a in range(n)] if own else []
        for cp in mine:
            cp.start()
        for step in _gather_steps(x_refs, out_refs, send_sems, recv_sems):
            step()
        for cp in mine:
            cp.wait()

    return pl.pallas_call(
        body, name=name, out_shape=[jax.ShapeDtypeStruct((N_DEV,) + t.shape, t.dtype) for t in arrs],
        in_specs=[ANY] * n, out_specs=[ANY] * n,
        scratch_shapes=[pltpu.SemaphoreType.DMA((n, 7)), pltpu.SemaphoreType.DMA((n, 7)),
                        pltpu.SemaphoreType.DMA((n,))],
    )(*arrs)


def _pair_exchange(arrs, name):
    n = len(arrs)

    def body(*refs):
        g_refs, out_refs = refs[:n], refs[n:2 * n]
        send_sems, recv_sems = refs[2 * n:]
        x, y, c = _place()
        give = [pltpu.make_async_remote_copy(
            src_ref=g_refs[a].at[pl.ds(0, 4), 1 - c], dst_ref=out_refs[a], send_sem=send_sems.at[a],
            recv_sem=recv_sems.at[a], device_id=(x, y, 1 - c), device_id_type=MESH) for a in range(n)]
        for cp in give:
            cp.start()
        for cp in give:
            cp.wait()

    return pl.pallas_call(
        body, name=name,
        out_shape=[jax.ShapeDtypeStruct((4,) + t.shape[2:], t.dtype) for t in arrs],
        in_specs=[ANY] * n, out_specs=[ANY] * n,
        scratch_shapes=[pltpu.SemaphoreType.DMA((n,)), pltpu.SemaphoreType.DMA((n,))],
    )(*arrs)


def _chip_exchange_steps(p_refs, out_refs, send_sems, recv_sems):
    x, y, c = _place()
    chips = [(1 - x, y), (x, 1 - y), (1 - x, 1 - y)]

    def copies():
        return [pltpu.make_async_remote_copy(
            src_ref=p_refs[a].at[2 * px + py], dst_ref=out_refs[a].at[j], send_sem=send_sems.at[a, j],
            recv_sem=recv_sems.at[a, j], device_id=(px, py, c), device_id_type=MESH)
            for a in range(len(p_refs)) for j, (px, py) in enumerate(chips)]

    def start():
        for cp in copies():
            cp.start()

    def finish():
        for cp in copies():
            cp.wait()

    return start, finish


def _chip_exchange(arrs, name):
    n = len(arrs)

    def body(*refs):
        for step in _chip_exchange_steps(refs[:n], refs[n:2 * n], *refs[2 * n:]):
            step()

    return pl.pallas_call(
        body, name=name, out_shape=[jax.ShapeDtypeStruct((3,) + t.shape[1:], t.dtype) for t in arrs],
        in_specs=[ANY] * n, out_specs=[ANY] * n,
        scratch_shapes=[pltpu.SemaphoreType.DMA((n, 3)), pltpu.SemaphoreType.DMA((n, 3))],
    )(*arrs)


class _Rider:
    def __init__(self, arrays, out_shape, sems, steps, relay_before_end=None):
        self.arrays, self.out_shape, self.scratch, self.steps = list(arrays), out_shape, sems, steps
        self.n = len(self.arrays)
        self.relay_before_end = relay_before_end

    def specs(self, in_specs, out_specs, out_shape, scratch):
        extra = [ANY] * self.n
        return in_specs + extra, out_specs + extra, out_shape + self.out_shape, scratch + self.scratch

    def split(self, refs, n_in, n_out):
        k = self.n
        a, b = n_in + k, n_in + k + n_out
        return refs[:n_in] + refs[a:b] + refs[b + k:-2], (refs[n_in:a], refs[b:b + k], refs[-2:])

    def head(self, mine, step):
        pl.when(step == 0)(self.steps(mine[0], mine[1], *mine[2])[0])

    def tail(self, mine, step, nsteps):
        steps = self.steps(mine[0], mine[1], *mine[2])
        if self.relay_before_end is not None:
            pl.when(step == nsteps - 1 - self.relay_before_end)(steps[1])
        pl.when(step == nsteps - 1)(steps[-1])


def _gather_rider(arrs, relay_before_end=4):
    n = len(arrs)
    return _Rider(arrs, [jax.ShapeDtypeStruct((N_DEV,) + t.shape, t.dtype) for t in arrs],
                  [pltpu.SemaphoreType.DMA((n, 7)), pltpu.SemaphoreType.DMA((n, 7))], _gather_steps,
                  relay_before_end)


def _chip_exchange_rider(arrs):
    n = len(arrs)
    return _Rider(arrs, [jax.ShapeDtypeStruct((3,) + t.shape[1:], t.dtype) for t in arrs],
                  [pltpu.SemaphoreType.DMA((n, 3)), pltpu.SemaphoreType.DMA((n, 3))], _chip_exchange_steps)


def _rope_tables(positions):
    inv_freq = ROPE_THETA ** (-jnp.arange(0, HEAD_DIM, 2, dtype=F32) / HEAD_DIM)
    ang = positions.astype(F32)[:, None] * inv_freq
    cos, sin = jnp.cos(ang), jnp.sin(ang)
    return jnp.tile(cos, (1, 4)), jnp.tile(jnp.concatenate([-sin, sin], axis=1), (1, 2))


def _vec(g, shift, scale, gate):
    z = jnp.zeros_like(g)
    return jnp.stack([g, shift, scale, gate, z, z, z, z])


def _strided(t, r):
    return t.reshape(t.shape[0] // r, r * GW)


class _GradReducer:
    def __init__(self):
        self.core = lax.axis_index("c").astype(jnp.int32).reshape(1)
        self.own, self.others, self.waiting, self.riding = {}, {}, [], []

    def pair(self, named):
        keys = list(named)
        mine = [named[k].reshape((4, 2) + named[k].shape[1:]) for k in keys]
        theirs = _pair_exchange(mine, "reduce_pair_" + keys[0])
        for k, g, r in zip(keys, mine, theirs):
            self.own[k] = _sum_pair(self.core, g, r, "sum_pair_" + k)
        self.waiting += keys

    def rider(self):
        self.riding, self.waiting = self.waiting, []
        return _chip_exchange_rider([self.own[k] for k in self.riding])

    def landed(self, results):
        self.others.update(zip(self.riding, results))

    def flush(self, name):
        keys, self.waiting = self.waiting, []
        self.others.update(zip(keys, _chip_exchange([self.own[k] for k in keys], name)))


def _by_owner(g):
    if g.ndim == 3:
        return g if g.shape[0] == N_DEV else g.reshape(N_DEV, g.shape[1] * g.shape[0] // N_DEV, g.shape[2])
    return g.reshape(N_DEV, g.shape[0] // N_DEV, g.shape[1])


def _local_step(x, target, positions, mod, small, W, late=None, red=None):
    S, D = x.shape
    sh1, sc1, gt1, sh2, sc2, gt2, sh3, sc3, gt3 = (mod[k] for k in range(9))
    v1 = _vec(small["g1"], sh1, sc1, gt1)
    v2 = _vec(small["g2"], sh2, sc2, gt2)
    v3 = _vec(small["g3"], sh3, sc3, gt3)
    vf = _vec(small["gf"], small["gf"], small["gf"], small["gf"])
    cos, sin = _rope_tables(positions)
    wbd = jax.scipy.linalg.block_diag(*[small["w_pool"][k] for k in range(4)]).astype(BF16)
    pscale = small["pool_scale"].reshape(1, GW)

    if late is None:
        h1, u1, ab1, act1, f1 = _ffn_fwd(x, v1, W["w1in"], W["w1out"], "ffn1_fwd")
    else:
        (h1, u1, ab1, act1, f1), landed = _ffn_fwd(x, v1, W["w1in"], W["w1out"], "ffn1_fwd", rider=late[0])
        W = {**W, **late[1](landed)}
    u2, p, gates, *qkv = _mix_in_fwd(h1, v2, cos, sin, W["win"], "mix_in_fwd")
    dpool, ypool = _pool_fwd(p, wbd, pscale, "pool_fwd")
    o3, lse3 = [], []
    for gi, r in enumerate(DILATIONS):
        o, lse = _attn_fwd(_strided(qkv[gi], r), _strided(qkv[3 + gi], r), _strided(qkv[6 + gi], r), r,
                           f"attn_fwd_{gi}")
        o3.append(o.reshape(S, GW))
        lse3.append(lse.reshape(S, GW))
    h2, ya, merged, tm = _mix_out_fwd(h1, v2, gates, ypool, o3, lse3, W["wpb"], W["wab"], W["wout"], "mix_out_fwd")
    h3, u3, ab3, act3, f3 = _ffn_fwd(h2, v3, W["w2in"], W["w2out"], "ffn2_fwd")
    dh3, loss_blk, dgf = _final_loss(h3, target, vf, "final_loss")

    dh2, dab3, df3, red3 = _ffn_bwd(dh3, h2, f3, ab3, v3, W["w2in"], W["w2out"], "ffn2_bwd")
    G = {"w2in": _wgrad(dab3, u3, "wgrad_2in"), "w2out": _by_owner(_wgrad(act3, df3, "wgrad_2out"))}
    mix_out_args = (dh2, tm, v2, gates, ypool, o3, lse3, W["wpb"], W["wab"], W["wout"], "mix_out_bwd")
    if red is None:
        mix_out = _mix_out_bwd(*mix_out_args)
    else:
        red.pair({k: G[k] for k in ("w2in", "w2out")})
        mix_out, landed = _mix_out_bwd(*mix_out_args, rider=red.rider())
        red.landed(landed)
    (dtm, dgl, dypb, dyab, dyp, do0, do1, do2, c0, c1, c2, red2o) = mix_out
    dq3, dk3, dv3 = [], [], []
    for gi, (r, do, ct) in enumerate(zip(DILATIONS, (do0, do1, do2), (c0, c1, c2))):
        dq, dk, dv = _attn_bwd(_strided(qkv[gi], r), _strided(qkv[3 + gi], r), _strided(qkv[6 + gi], r),
                               _strided(do, r), _strided(lse3[gi], r), _strided(ct, r), r, f"attn_bwd_{gi}")
        dq3.append(dq.reshape(S, GW))
        dk3.append(dk.reshape(S, GW))
        dv3.append(dv.reshape(S, GW))
    dp, dwbd, dps = _pool_bwd(dyp, dpool, wbd, pscale, "pool_bwd")
    dh1, dproj, red2i = _mix_in_bwd(dh2, h1, v2, cos, sin, dp, dq3 + dk3 + dv3, dgl, W["win"], "mix_in_bwd")
    G["win"] = _by_owner(_wgrad(dproj, u2, "wgrad_in", tm=1152))
    G["wpb"] = _full_to_cols(_wgrad(ypool, dypb, "wgrad_pb"))
    G["wab"] = _full_to_cols(_wgrad(ya, dyab, "wgrad_ab"))
    G["wout"] = _by_owner(_wgrad(merged, dtm, "wgrad_out"))
    if red is not None:
        red.pair({k: G[k] for k in ("win", "wpb", "wab", "wout")})
    dx, dab1, df1, red1 = _ffn_bwd(dh1, x, f1, ab1, v1, W["w1in"], W["w1out"], "ffn1_bwd")
    G["w1out"] = _by_owner(_wgrad(act1, df1, "wgrad_1out"))
    if red is None:
        G["w1in"] = _wgrad(dab1, u1, "wgrad_1in")
    else:
        red.pair({"w1out": G["w1out"]})
        G["w1in"], landed = _wgrad(dab1, u1, "wgrad_1in", rider=red.rider())
        red.landed(landed)
        red.pair({"w1in": G["w1in"]})
        red.flush("reduce_chips_w1in")
    dmod = jnp.stack([red1[1], red1[2], red1[0], red2i[1], red2i[2], red2o[0], red3[1], red3[2], red3[0]])
    dsmall = {
        "g1": red1[3], "g2": red2i[3], "g3": red3[3], "gf": dgf[0],
        "w_pool": jnp.stack([dwbd[k * 64:(k + 1) * 64, k * 64:(k + 1) * 64] for k in range(4)]),
        "pool_scale": dps[0],
    }
    return loss_blk[0, 0], dx, G, dmod, dsmall


SHARDED = ("w_ffn1_in", "w_ffn1_out", "w_in", "w_pool_branch", "w_attn_branch", "w_out", "w_ffn2_in", "w_ffn2_out")
TRANSPOSED = ("w_ffn1_in", "w_in", "w_ffn2_in")
FIRST = ("w_ffn1_in", "w_ffn1_out")
LATER = tuple(n for n in SHARDED if n not in FIRST)
GRAD_KEY = dict(w_ffn1_in="w1in", w_ffn1_out="w1out", w_in="win", w_pool_branch="wpb", w_attn_branch="wab",
                w_out="wout", w_ffn2_in="w2in", w_ffn2_out="w2out")


def _cols_to_full(g):
    return jnp.concatenate([g[j] for j in range(N_DEV)], axis=1)


def _full_to_cols(t):
    c = t.shape[1] // N_DEV
    return jnp.stack([t[:, j * c:(j + 1) * c] for j in range(N_DEV)])


SMALL =(("b_ada", 9216), ("g_norm_ffn1", 1024), ("g_norm_mix", 1024), ("g_norm_ffn2", 1024), ("g_final", 1024),
         ("w_pool", 16384), ("pool_scale", 256))
SMALL_ROWS = 240


def _pack_small(vals, loss=None):
    flat = jnp.concatenate([vals[name].reshape(-1) for name, _ in SMALL])
    tail = jnp.zeros((SMALL_ROWS * 128 - flat.shape[0],), F32)
    if loss is not None:
        tail = tail.at[0].set(loss)
    return jnp.concatenate([flat, tail]).reshape(SMALL_ROWS, 128)


def _unpack_small(slab, shapes):
    flat, out, off = slab.reshape(-1), {}, 0
    for name, n in SMALL:
        out[name] = flat[off:off + n].reshape(shapes[name])
        off += n
    return out, flat[off]


def kernel(x, c, positions, w_ada, b_ada, g_norm_ffn1, w_ffn1_in, w_ffn1_out, g_norm_mix, w_in, w_pool, pool_scale, w_pool_branch, w_attn_branch, w_out, g_norm_ffn2, w_ffn2_in, w_ffn2_out, g_final, loss_target, m_w_ada, m_b_ada, m_g_norm_ffn1, m_w_ffn1_in, m_w_ffn1_out, m_g_norm_mix, m_w_in, m_w_pool, m_pool_scale, m_w_pool_branch, m_w_attn_branch, m_w_out, m_g_norm_ffn2, m_w_ffn2_in, m_w_ffn2_out, m_g_final, v_w_ada, v_b_ada, v_g_norm_ffn1, v_w_ffn1_in, v_w_ffn1_out, v_g_norm_mix, v_w_in, v_w_pool, v_pool_scale, v_w_pool_branch, v_w_attn_branch, v_w_out, v_g_norm_ffn2, v_w_ffn2_in, v_w_ffn2_out, v_g_final):
    names = ["w_ada", "b_ada", "g_norm_ffn1", "w_ffn1_in", "w_ffn1_out", "g_norm_mix", "w_in", "w_pool", "pool_scale",
             "w_pool_branch", "w_attn_branch", "w_out", "g_norm_ffn2", "w_ffn2_in", "w_ffn2_out", "g_final"]
    w = dict(w_ada=w_ada, b_ada=b_ada, g_norm_ffn1=g_norm_ffn1, w_ffn1_in=w_ffn1_in, w_ffn1_out=w_ffn1_out,
             g_norm_mix=g_norm_mix, w_in=w_in, w_pool=w_pool, pool_scale=pool_scale, w_pool_branch=w_pool_branch,
             w_attn_branch=w_attn_branch, w_out=w_out, g_norm_ffn2=g_norm_ffn2, w_ffn2_in=w_ffn2_in,
             w_ffn2_out=w_ffn2_out, g_final=g_final)
    m = dict(w_ada=m_w_ada, b_ada=m_b_ada, g_norm_ffn1=m_g_norm_ffn1, w_ffn1_in=m_w_ffn1_in, w_ffn1_out=m_w_ffn1_out,
             g_norm_mix=m_g_norm_mix, w_in=m_w_in, w_pool=m_w_pool, pool_scale=m_pool_scale,
             w_pool_branch=m_w_pool_branch, w_attn_branch=m_w_attn_branch, w_out=m_w_out, g_norm_ffn2=m_g_norm_ffn2,
             w_ffn2_in=m_w_ffn2_in, w_ffn2_out=m_w_ffn2_out, g_final=m_g_final)
    v = dict(w_ada=v_w_ada, b_ada=v_b_ada, g_norm_ffn1=v_g_norm_ffn1, w_ffn1_in=v_w_ffn1_in, w_ffn1_out=v_w_ffn1_out,
             g_norm_mix=v_g_norm_mix, w_in=v_w_in, w_pool=v_w_pool, pool_scale=v_pool_scale,
             w_pool_branch=v_w_pool_branch, w_attn_branch=v_w_attn_branch, w_out=v_w_out, g_norm_ffn2=v_g_norm_ffn2,
             w_ffn2_in=v_w_ffn2_in, w_ffn2_out=v_w_ffn2_out, g_final=v_g_final)
    shapes = {n: w[n].shape for n in names}
    me = 4 * lax.axis_index("x") + 2 * lax.axis_index("y") + lax.axis_index("c")
    D = x.shape[-1]
    n_mod = w_ada.shape[-1] * N_DEV // D

    (c_all,) = _all_gather([c.reshape(D // 128, 128)], "gather_c")
    ada_cols = w_ada.shape[-1]
    b_mine = lax.dynamic_slice_in_dim(b_ada, me * ada_cols, ada_cols, axis=1)
    cond, mod_part = _ada_mod(c_all.reshape(N_DEV, D), w_ada[0], b_mine, "ada_mod")
    (mod_all,) = _all_gather([mod_part.reshape(-1, 128)], "gather_mod")
    mod_all = mod_all.reshape(N_DEV, N_DEV, ada_cols)
    mod = lax.dynamic_index_in_dim(mod_all, me, axis=1, keepdims=False).reshape(n_mod, D)

    def local(t, name):
        return t[name][0].T if name in TRANSPOSED else t[name][0]

    shards = {name: local(w, name).astype(BF16) for name in SHARDED}

    def gather_done(names, fulls):
        return {name: lax.dynamic_update_index_in_dim(full, shards[name], me, axis=0)
                for name, full in zip(names, fulls)}

    def ffn_weights(g, pre):
        out = g["w_ffn%s_out" % pre]
        return {"w%sin" % pre: g["w_ffn%s_in" % pre],
                "w%sout" % pre: out.reshape(N_DEV // 2, 2 * out.shape[1], out.shape[2])}

    def later_weights(fulls):
        g = gather_done(LATER, fulls)
        return dict(win=g["w_in"].reshape(-1, D), wpb=_cols_to_full(g["w_pool_branch"]),
                    wab=_cols_to_full(g["w_attn_branch"]), wout=g["w_out"].reshape(D, D), **ffn_weights(g, "2"))

    W = ffn_weights(gather_done(FIRST, _all_gather([shards[n] for n in FIRST], "gather_ffn1", own=False)), "1")
    small = dict(g1=g_norm_ffn1[0], g2=g_norm_mix[0], g3=g_norm_ffn2[0], gf=g_final, w_pool=w_pool[0],
                 pool_scale=pool_scale[0])
    red = _GradReducer()
    loss_part, dx, G, dmod, dsmall = _local_step(
        x[0], loss_target[0], positions[0], mod, small, W,
        late=(_gather_rider([shards[n] for n in LATER]), later_weights), red=red)
    chip = (2 * lax.axis_index("x") + lax.axis_index("y")).astype(jnp.int32).reshape(1)

    part = _pack_small(dict(b_ada=dmod, g_norm_ffn1=dsmall["g1"], g_norm_mix=dsmall["g2"], g_norm_ffn2=dsmall["g3"],
                            g_final=dsmall["gf"], w_pool=dsmall["w_pool"], pool_scale=dsmall["pool_scale"]),
                       loss=loss_part)
    (parts,) = _all_gather([part], "gather_small")
    gsmall, loss = _unpack_small(_sum_slots(parts, "sum_small"), shapes)
    rows_mine = ada_cols // 128
    dmod_mine = lax.dynamic_slice_in_dim(parts, me * rows_mine, rows_mine, axis=1).reshape(N_DEV, ada_cols)

    grads, delta, new_m, new_v = {}, {}, {}, {}
    grads["w_ada"], delta["w_ada"], new_m["w_ada"], new_v["w_ada"] = (
        t[None] for t in _ada_grad_adamw(cond.T, dmod_mine, w_ada[0], m_w_ada[0], v_w_ada[0], "ada_grad_adamw"))
    for name in SHARDED:
        key = GRAD_KEY[name]
        res = _sum_adamw(chip, red.own[key], red.others[key], local(w, name), local(m, name), local(v, name),
                         "adamw_" + name)
        grads[name], delta[name], new_m[name], new_v[name] = (
            (t.T if name in TRANSPOSED else t)[None] for t in res)
    sd, sm, sv = _adamw(_pack_small(w), _pack_small(gsmall), _pack_small(m), _pack_small(v), "adamw_small")
    for dst, src in ((delta, sd), (new_m, sm), (new_v, sv)):
        dst.update(_unpack_small(src, shapes)[0])
    grads.update(gsmall)

    return (loss, dx[None], *[grads[n] for n in names], *[delta[n] for n in names],
            *[new_m[n] for n in names], *[new_v[n] for n in names])
```

```python
import functools

import jax
import jax.numpy as jnp
from jax import lax
from jax.experimental import pallas as pl
from jax.experimental.pallas import tpu as pltpu

F32 = jnp.float32
BF16 = jnp.bfloat16
MESH = pl.DeviceIdType.MESH
ANY = pl.BlockSpec(memory_space=pl.ANY)

N_DEV = 8
EPS = 1e-6
HEAD_DIM = 64
HEADS = 4
GW = HEADS * HEAD_DIM
DILATIONS = (1, 4, 16)
BAND = 128
QB = 128
POOL_WINDOWS = (2, 4, 8, 16)
HALO = 16
ROPE_THETA = 10000.0

ADAM_LR = 0.001
ADAM_B1 = 0.9
ADAM_B2 = 0.999
ADAM_EPS = 1e-08
ADAM_WD = 0.01
ADAM_STEP = 10

VMEM_LIMIT = 56 * 1024 * 1024
TS = 256
FC = 256

NT = (((1,), (1,)), ((), ()))
TN = (((0,), (0,)), ((), ()))


def _params(**kw):
    return pltpu.CompilerParams(vmem_limit_bytes=VMEM_LIMIT, **kw)


def _dot(a, b):
    return jnp.dot(a, b, preferred_element_type=F32)


def _dot_nt(a, b):
    return lax.dot_general(a, b, NT, preferred_element_type=F32)


def _dot_tn(a, b):
    return lax.dot_general(a, b, TN, preferred_element_type=F32)


def _load_weights(pairs, sem):
    @pl.when(pl.program_id(0) == 0)
    def _():
        copies = [pltpu.make_async_copy(src, dst, sem.at[i]) for i, (src, dst) in enumerate(pairs)]
        for cp in copies:
            cp.start()
        for cp in copies:
            cp.wait()


def _norm_mod(x, g, sc, sh):
    r = lax.rsqrt(jnp.mean(x * x, axis=-1, keepdims=True) + EPS)
    xn = x * r
    y = xn * g
    return r, xn, y, y * (1.0 + sc) + sh


def _norm_mod_bwd(du, r, xn, y, g, sc):
    dsh = jnp.sum(du, axis=0, keepdims=True)
    dsc = jnp.sum(du * y, axis=0, keepdims=True)
    dy = du * (1.0 + sc)
    dg = jnp.sum(dy * xn, axis=0, keepdims=True)
    dxn = dy * g
    dx = r * (dxn - xn * jnp.mean(dxn * xn, axis=-1, keepdims=True))
    return dx, dsh, dsc, dg


def _row_tile(ts, width):
    return pl.BlockSpec((ts, width), lambda i: (i, 0))


def _const(shape):
    return pl.BlockSpec(shape, lambda *_: (0,) * len(shape))


def _ffn_fwd(h, vec, win, wout, name, rider=None):
    S, D = h.shape
    nsh, fs, _ = win.shape
    nch = nsh // 2

    def body(*refs):
        if rider is None:
            return compute(*refs)
        host, mine = rider.split(refs, 4, 5)
        rider.head(mine, pl.program_id(0))
        compute(*host)
        rider.tail(mine, pl.program_id(0), S // TS)

    def compute(h_ref, vec_ref, win_hbm, wout_hbm, hn_ref, u_ref, ab_ref, act_ref, f_ref, win_v, wout_v, sem):
        _load_weights([(win_hbm, win_v), (wout_hbm, wout_v)], sem)
        x = h_ref[...]
        g, sh, sc, gt = (vec_ref[k:k + 1, :] for k in range(4))
        _, _, _, u = _norm_mod(x, g, sc, sh)
        ub = u.astype(BF16)
        u_ref[...] = ub
        acc = jnp.zeros((TS, D), F32)
        for j in range(nch):
            a = _dot_nt(ub, win_v[j])
            b = _dot_nt(ub, win_v[nch + j])
            act = ((a * jax.nn.sigmoid(a)) * b).astype(BF16)
            ab_ref[j] = a.astype(BF16)
            ab_ref[nch + j] = b.astype(BF16)
            act_ref[j] = act
            acc = acc + _dot(act, wout_v[j])
        f_ref[...] = acc
        hn_ref[...] = x + (0.5 * gt) * acc

    specs = (
        [_row_tile(TS, D), _const((8, D)), ANY, ANY],
        [_row_tile(TS, D), _row_tile(TS, D), pl.BlockSpec((nsh, TS, fs), lambda i: (0, i, 0)),
         pl.BlockSpec((nch, TS, fs), lambda i: (0, i, 0)), _row_tile(TS, D)],
        [jax.ShapeDtypeStruct((S, D), F32), jax.ShapeDtypeStruct((S, D), BF16),
         jax.ShapeDtypeStruct((nsh, S, fs), BF16), jax.ShapeDtypeStruct((nch, S, fs), BF16),
         jax.ShapeDtypeStruct((S, D), F32)],
        [pltpu.VMEM(win.shape, BF16), pltpu.VMEM(wout.shape, BF16), pltpu.SemaphoreType.DMA((2,))])
    in_specs, out_specs, out_shape, scratch = specs if rider is None else rider.specs(*specs)
    outs = pl.pallas_call(
        body, name=name, grid=(S // TS,), in_specs=in_specs, out_specs=out_specs, out_shape=out_shape,
        scratch_shapes=scratch, compiler_params=_params(),
    )(h, vec, win, wout, *(rider.arrays if rider else []))
    return outs if rider is None else (outs[:5], outs[5:])


def _ffn_bwd(dh, h, f, ab, vec, win, wout, name):
    S, D = h.shape
    nsh, fs, _ = win.shape
    nch = nsh // 2

    def body(dh_ref, h_ref, f_ref, ab_ref, vec_ref, win_hbm, wout_hbm,
             dhp_ref, dab_ref, df_ref, red_ref, win_v, wout_v, sem):
        _load_weights([(win_hbm, win_v), (wout_hbm, wout_v)], sem)

        @pl.when(pl.program_id(0) == 0)
        def _():
            red_ref[...] = jnp.zeros_like(red_ref)

        dh_v = dh_ref[...]
        x = h_ref[...]
        g, sh, sc, gt = (vec_ref[k:k + 1, :] for k in range(4))
        dgt = jnp.sum((0.5 * f_ref[...]) * dh_v, axis=0, keepdims=True)
        dfb = ((0.5 * gt) * dh_v).astype(BF16)
        df_ref[...] = dfb
        du = jnp.zeros((TS, D), F32)
        for j in range(nch):
            dact = _dot_nt(dfb, wout_v[j])
            av = ab_ref[j].astype(F32)
            bv = ab_ref[nch + j].astype(F32)
            sg = jax.nn.sigmoid(av)
            da = (dact * bv * (sg * (1.0 + av * (1.0 - sg)))).astype(BF16)
            db = (dact * (av * sg)).astype(BF16)
            dab_ref[j] = da
            dab_ref[nch + j] = db
            du = du + _dot(da, win_v[j]) + _dot(db, win_v[nch + j])
        r, xn, y, _ = _norm_mod(x, g, sc, sh)
        dx, dsh, dsc, dg = _norm_mod_bwd(du, r, xn, y, g, sc)
        dhp_ref[...] = dh_v + dx
        red_ref[0:1, :] += dgt
        red_ref[1:2, :] += dsh
        red_ref[2:3, :] += dsc
        red_ref[3:4, :] += dg

    ab_spec = pl.BlockSpec((nsh, TS, fs), lambda i: (0, i, 0))
    return pl.pallas_call(
        body, name=name, grid=(S // TS,),
        in_specs=[_row_tile(TS, D), _row_tile(TS, D), _row_tile(TS, D), ab_spec, _const((8, D)), ANY, ANY],
        out_specs=[_row_tile(TS, D), ab_spec, _row_tile(TS, D), _const((8, D))],
        out_shape=[jax.ShapeDtypeStruct((S, D), F32), jax.ShapeDtypeStruct((nsh, S, fs), BF16),
                   jax.ShapeDtypeStruct((S, D), BF16), jax.ShapeDtypeStruct((8, D), F32)],
        scratch_shapes=[pltpu.VMEM(win.shape, BF16), pltpu.VMEM(wout.shape, BF16), pltpu.SemaphoreType.DMA((2,))],
        compiler_params=_params(),
    )(dh, h, f, ab, vec, win, wout)


def _wgrad(x, y, name, tm=None, ts=2048, rider=None):
    xb = x.ndim == 3
    nb = x.shape[0] if xb else 0
    S, M = x.shape[-2:]
    N = y.shape[-1]
    tm = tm or M
    ts = min(ts, S)
    nk = S // ts
    grid = (max(nb, 1), M // tm, nk)

    def body(*refs):
        if rider is None:
            return compute(*refs)
        host, mine = rider.split(refs, 2, 1)
        step = (pl.program_id(0) * grid[1] + pl.program_id(1)) * grid[2] + pl.program_id(2)
        rider.head(mine, step)
        compute(*host)
        rider.tail(mine, step, grid[0] * grid[1] * grid[2])

    def compute(x_ref, y_ref, o_ref, acc):
        k = pl.program_id(2)

        @pl.when(k == 0)
        def _():
            acc[...] = jnp.zeros_like(acc)

        acc[...] += _dot_tn(x_ref[...], y_ref[...])

        @pl.when(k == nk - 1)
        def _():
            o_ref[...] = acc[...].astype(BF16)

    x_spec = (pl.BlockSpec((None, ts, tm), lambda b, i, k: (b, k, i)) if xb
              else pl.BlockSpec((ts, tm), lambda b, i, k: (k, i)))
    y_spec = pl.BlockSpec((ts, N), lambda b, i, k: (k, 0))
    if xb:
        o_spec, o_shape = pl.BlockSpec((None, tm, N), lambda b, i, k: (b, i, 0)), (nb, M, N)
    else:
        o_spec, o_shape = pl.BlockSpec((tm, N), lambda b, i, k: (i, 0)), (M, N)
    specs = ([x_spec, y_spec], [o_spec], [jax.ShapeDtypeStruct(o_shape, BF16)], [pltpu.VMEM((tm, N), F32)])
    in_specs, out_specs, out_shape, scratch = specs if rider is None else rider.specs(*specs)
    outs = pl.pallas_call(
        body, name=name, grid=grid, in_specs=in_specs, out_specs=out_specs, out_shape=out_shape,
        scratch_shapes=scratch, compiler_params=_params(),
    )(x, y, *(rider.arrays if rider else []))
    return outs[0] if rider is None else (outs[0], outs[1:])


P_OFF, Q_OFF, K_OFF, V_OFF, G_OFF = 0, 256, 1024, 1792, 2560
IN_WIDTH = 4608


def _first_half_mask(ts):
    lane = lax.broadcasted_iota(jnp.int32, (ts, 128), 1)
    return (lane % HEAD_DIM) < (HEAD_DIM // 2)


def _rope(t, cos, sin_signed, first, sign):
    partner = jnp.where(first, pltpu.roll(t, 96, 1), pltpu.roll(t, 32, 1))
    return t * cos + sign * (partner * sin_signed)


def _res_spec(r):
    return pl.BlockSpec((r, TS // r, GW), lambda i: (0, i, 0))


def _res_shape(S, r, dtype):
    return jax.ShapeDtypeStruct((r, S // r, GW), dtype)


def _to_residues(piece, out_ref, lanes, r, scr):
    if r == 1:
        out_ref[0, :, lanes] = piece.astype(out_ref.dtype)
        return
    for h in range(piece.shape[1] // 128):
        scr[h] = piece[:, h * 128:(h + 1) * 128]
        at = slice(lanes.start + h * 128, lanes.start + (h + 1) * 128)
        for res in range(r):
            out_ref[res, :, at] = scr[h, pl.ds(res, TS // r, stride=r), :].astype(out_ref.dtype)


def _from_residues(in_ref, lanes, r, scr):
    if r == 1:
        return in_ref[0, :, lanes].astype(F32)
    halves = (lanes.stop - lanes.start) // 128
    for h in range(halves):
        at = slice(lanes.start + h * 128, lanes.start + (h + 1) * 128)
        for res in range(r):
            scr[h, pl.ds(res, TS // r, stride=r), :] = in_ref[res, :, at].astype(F32)
    return scr[0] if halves == 1 else jnp.concatenate([scr[0], scr[1]], axis=1)


RES_SCRATCH = (2, TS, 128)


def _mix_in_fwd(h, vec, cos, sin, win, name):
    S, D = h.shape

    def body(h_ref, vec_ref, cos_ref, sin_ref, win_hbm, u_ref, p_ref, gates_ref, *rest):
        qkv_refs, (win_v, sem, scr) = rest[:9], rest[9:]
        _load_weights([(win_hbm, win_v)], sem)
        g, sh, sc = (vec_ref[k:k + 1, :] for k in range(3))
        _, _, _, u = _norm_mod(h_ref[...], g, sc, sh)
        ub = u.astype(BF16)
        u_ref[...] = ub
        p_ref[...] = _dot_nt(ub, win_v[P_OFF:Q_OFF, :])
        cosv, sinv = cos_ref[...], sin_ref[...]
        first = _first_half_mask(TS)
        for which, off in enumerate((Q_OFF, K_OFF, V_OFF)):
            t = _dot_nt(ub, win_v[off:off + 3 * GW, :])
            for gi in range(3):
                for half in range(2):
                    c0 = gi * GW + half * 128
                    piece = t[:, c0:c0 + 128]
                    if which < 2:
                        piece = _rope(piece, cosv, sinv, first, 1.0)
                    _to_residues(piece, qkv_refs[which * 3 + gi], slice(half * 128, (half + 1) * 128),
                                 DILATIONS[gi], scr)
        gates_ref[...] = jax.nn.sigmoid(_dot_nt(ub, win_v[G_OFF:IN_WIDTH, :]))

    return pl.pallas_call(
        body, name=name, grid=(S // TS,),
        in_specs=[_row_tile(TS, D), _const((8, D)), _row_tile(TS, 128), _row_tile(TS, 128), ANY],
        out_specs=[_row_tile(TS, D), _row_tile(TS, GW), _row_tile(TS, 2 * D)] + [_res_spec(r) for r in DILATIONS] * 3,
        out_shape=[jax.ShapeDtypeStruct((S, D), BF16), jax.ShapeDtypeStruct((S, GW), F32),
                   jax.ShapeDtypeStruct((S, 2 * D), F32)] + [_res_shape(S, r, BF16) for r in DILATIONS] * 3,
        scratch_shapes=[pltpu.VMEM((IN_WIDTH, D), BF16), pltpu.SemaphoreType.DMA((1,)), pltpu.VMEM(RES_SCRATCH, F32)],
        compiler_params=_params(),
    )(h, vec, cos, sin, win)


def _mix_in_bwd(dh, h, vec, cos, sin, dp, dqkv, dgl, win, name):
    S, D = h.shape

    def body(dh_ref, h_ref, vec_ref, cos_ref, sin_ref, dp_ref, *rest):
        dqkv_refs = rest[:9]
        dgl_ref, win_hbm, dhp_ref, dproj_ref, red_ref, win_v, sem, scr = rest[9:]
        _load_weights([(win_hbm, win_v)], sem)

        @pl.when(pl.program_id(0) == 0)
        def _():
            red_ref[...] = jnp.zeros_like(red_ref)

        cosv, sinv = cos_ref[...], sin_ref[...]
        first = _first_half_mask(TS)
        dproj_ref[:, P_OFF:Q_OFF] = dp_ref[...].astype(BF16)
        for which, off in enumerate((Q_OFF, K_OFF, V_OFF)):
            for gi in range(3):
                for half in range(2):
                    piece = _from_residues(dqkv_refs[which * 3 + gi], slice(half * 128, (half + 1) * 128),
                                           DILATIONS[gi], scr)
                    if which < 2:
                        piece = _rope(piece, cosv, sinv, first, -1.0)
                    c0 = off + gi * GW + half * 128
                    dproj_ref[:, c0:c0 + 128] = piece.astype(BF16)
        dproj_ref[:, G_OFF:IN_WIDTH] = dgl_ref[...]
        du = _dot(dproj_ref[...], win_v[...])
        g, sh, sc = (vec_ref[k:k + 1, :] for k in range(3))
        r, xn, y, _ = _norm_mod(h_ref[...], g, sc, sh)
        dx, dsh, dsc, dg = _norm_mod_bwd(du, r, xn, y, g, sc)
        dhp_ref[...] = dh_ref[...] + dx
        red_ref[1:2, :] += dsh
        red_ref[2:3, :] += dsc
        red_ref[3:4, :] += dg

    return pl.pallas_call(
        body, name=name, grid=(S // TS,),
        in_specs=[_row_tile(TS, D), _row_tile(TS, D), _const((8, D)), _row_tile(TS, 128), _row_tile(TS, 128),
                  _row_tile(TS, GW)] + [_res_spec(r) for r in DILATIONS] * 3 + [_row_tile(TS, 2 * D), ANY],
        out_specs=[_row_tile(TS, D), _row_tile(TS, IN_WIDTH), _const((8, D))],
        out_shape=[jax.ShapeDtypeStruct((S, D), F32), jax.ShapeDtypeStruct((S, IN_WIDTH), BF16),
                   jax.ShapeDtypeStruct((8, D), F32)],
        scratch_shapes=[pltpu.VMEM((IN_WIDTH, D), BF16), pltpu.SemaphoreType.DMA((1,)), pltpu.VMEM(RES_SCRATCH, F32)],
        compiler_params=_params(),
    )(dh, h, vec, cos, sin, dp, *dqkv, dgl, win)


def _pool_lanes(rows):
    lane = lax.broadcasted_iota(jnp.int32, (rows, GW), 1)
    return lane // HEAD_DIM


def _pool_window(rows):
    grp = _pool_lanes(rows)
    w = jnp.full((rows, GW), POOL_WINDOWS[0], jnp.int32)
    for k in range(1, len(POOL_WINDOWS)):
        w = jnp.where(grp == k, POOL_WINDOWS[k], w)
    return grp, w


def _pool_fwd(p, wbd, scale, name, ts=512):
    S = p.shape[0]
    ext = ts + HALO

    def body(pc_ref, ph_ref, wbd_ref, sc_ref, d_ref, y_ref):
        i = pl.program_id(0)
        cur = pc_ref[...]
        halo = jnp.where(i > 0, ph_ref[...], 0.0)
        s = jnp.concatenate([halo, cur], axis=0)
        grp, w = _pool_window(ext)
        sel = jnp.zeros((ext, GW), F32)
        for k, wk in enumerate(POOL_WINDOWS):
            s = s + pltpu.roll(s, wk // 2, 0)
            sel = jnp.where(grp == k, s, sel)
        t = i * ts + lax.broadcasted_iota(jnp.int32, (ts, GW), 0)
        count = jnp.minimum(t + 1, w[HALO:]).astype(F32)
        d = (sel[HALO:] / count - cur).astype(BF16)
        d_ref[...] = d
        y_ref[...] = (_dot(d, wbd_ref[...]) * sc_ref[...]).astype(BF16)

    return pl.pallas_call(
        body, name=name, grid=(S // ts,),
        in_specs=[_row_tile(ts, GW),
                  pl.BlockSpec((HALO, GW), lambda i: (jnp.maximum(i * (ts // HALO) - 1, 0), 0)),
                  _const((GW, GW)), _const((1, GW))],
        out_specs=[_row_tile(ts, GW), _row_tile(ts, GW)],
        out_shape=[jax.ShapeDtypeStruct((S, GW), BF16), jax.ShapeDtypeStruct((S, GW), BF16)],
        compiler_params=_params(),
    )(p, p, wbd, scale)


def _pool_bwd(dy, d, wbd, scale, name, ts=512):
    S = dy.shape[0]
    ext = ts + HALO
    nsteps = S // ts
    last_halo = S // HALO - 1

    def body(dyc_ref, dyh_ref, d_ref, wbd_ref, sc_ref, dp_ref, dw_ref, ds_ref):
        i = pl.program_id(0)

        @pl.when(i == 0)
        def _():
            dw_ref[...] = jnp.zeros_like(dw_ref)
            ds_ref[...] = jnp.zeros_like(ds_ref)

        dyc = dyc_ref[...]
        dyh = jnp.where(i < nsteps - 1, dyh_ref[...], 0.0)
        dys = (jnp.concatenate([dyc, dyh], axis=0) * sc_ref[...]).astype(BF16)
        dd = _dot_nt(dys, wbd_ref[...])
        grp, w = _pool_window(ext)
        t = i * ts + lax.broadcasted_iota(jnp.int32, (ext, GW), 0)
        s = dd / jnp.minimum(t + 1, w).astype(F32)
        sel = jnp.zeros((ext, GW), F32)
        for k, wk in enumerate(POOL_WINDOWS):
            s = s + pltpu.roll(s, ext - wk // 2, 0)
            sel = jnp.where(grp == k, s, sel)
        dp_ref[...] = sel[:ts] - dd[:ts]
        dv = d_ref[...]
        z = _dot(dv, wbd_ref[...])
        ds_ref[0:1, :] += jnp.sum(dyc * z, axis=0, keepdims=True)
        dw_ref[...] += _dot_tn(dv, dys[:ts])

    return pl.pallas_call(
        body, name=name, grid=(nsteps,),
        in_specs=[_row_tile(ts, GW),
                  pl.BlockSpec((HALO, GW), lambda i: (jnp.minimum((i + 1) * (ts // HALO), last_halo), 0)),
                  _row_tile(ts, GW), _const((GW, GW)), _const((1, GW))],
        out_specs=[_row_tile(ts, GW), _const((GW, GW)), _const((8, GW))],
        out_shape=[jax.ShapeDtypeStruct((S, GW), F32), jax.ShapeDtypeStruct((GW, GW), F32),
                   jax.ShapeDtypeStruct((8, GW), F32)],
        compiler_params=_params(),
    )(dy, dy, d, wbd, scale)


def _head_id(rows):
    return lax.broadcasted_iota(jnp.int32, (rows, GW), 1) // HEAD_DIM


def _stack_heads(t, hid):
    return jnp.concatenate([jnp.where(hid == h, t, jnp.zeros_like(t)) for h in range(HEADS)], axis=0)


def _unstack_heads(t_all, hid):
    out = jnp.zeros((QB, GW), F32)
    for h in range(HEADS):
        out = jnp.where(hid == h, t_all[h * QB:(h + 1) * QB], out)
    return out


def _band_mask(n):
    row = lax.broadcasted_iota(jnp.int32, (HEADS * QB, 2 * QB), 0) % QB
    col = lax.broadcasted_iota(jnp.int32, (HEADS * QB, 2 * QB), 1)
    rel = row + QB - col
    return (rel >= 0) & (rel <= BAND) & ((col >= QB) | (n > 0))


def _attn_fwd(q, k, v, name):
    r, L, _ = q.shape
    nb = L // QB
    cur = pl.BlockSpec((None, QB, GW), lambda res, n: (res, n, 0))
    prev = pl.BlockSpec((None, QB, GW), lambda res, n: (res, jnp.maximum(n - 1, 0), 0))

    def body(q_ref, kp_ref, kc_ref, vp_ref, vc_ref, o_ref, lse_ref):
        n = pl.program_id(1)
        hid = _head_id(QB)
        qs = _stack_heads(q_ref[...], hid)
        kc = jnp.concatenate([kp_ref[...], kc_ref[...]], axis=0)
        vc = jnp.concatenate([vp_ref[...], vc_ref[...]], axis=0)
        s = _dot_nt(qs, kc) * (HEAD_DIM ** -0.5)
        s = jnp.where(_band_mask(n), s, -jnp.inf)
        m = jnp.max(s, axis=-1, keepdims=True)
        lse = m + jnp.log(jnp.sum(jnp.exp(s - m), axis=-1, keepdims=True))
        pr = jnp.exp(s - lse).astype(BF16)
        o_ref[...] = _unstack_heads(_dot(pr, vc), hid)
        lse_ref[...] = _unstack_heads(jnp.broadcast_to(lse, (HEADS * QB, GW)), hid)

    return pl.pallas_call(
        body, name=name, grid=(r, nb),
        in_specs=[cur, prev, cur, prev, cur], out_specs=[cur, cur],
        out_shape=[jax.ShapeDtypeStruct(q.shape, F32), jax.ShapeDtypeStruct(q.shape, F32)],
        compiler_params=_params(),
    )(q, k, k, v, v)


def _head_rows(t_full, hid):
    return jnp.concatenate(
        [jnp.max(jnp.where(hid == h, t_full, -jnp.inf), axis=-1, keepdims=True) for h in range(HEADS)], axis=0)


def _attn_bwd(q, k, v, do, lse, cterm, name):
    r, L, _ = q.shape
    nb = L // QB
    qside = pl.BlockSpec((None, QB, GW), lambda res, n: (res, jnp.minimum(n, nb - 1), 0))
    kcur = qside
    kprev = pl.BlockSpec((None, QB, GW), lambda res, n: (res, jnp.clip(n - 1, 0, nb - 1), 0))
    kout = pl.BlockSpec((None, QB, GW), lambda res, n: (res, jnp.maximum(n - 1, 0), 0))

    def body(q_ref, do_ref, lse_ref, c_ref, kp_ref, kc_ref, vp_ref, vc_ref,
             dq_ref, dk_ref, dv_ref, carry_k, carry_v):
        n = pl.program_id(1)

        @pl.when(n == 0)
        def _():
            carry_k[...] = jnp.zeros_like(carry_k)
            carry_v[...] = jnp.zeros_like(carry_v)

        @pl.when(n < nb)
        def _():
            hid = _head_id(QB)
            qs = _stack_heads(q_ref[...], hid)
            dos = _stack_heads(do_ref[...], hid)
            kc = jnp.concatenate([kp_ref[...], kc_ref[...]], axis=0)
            vc = jnp.concatenate([vp_ref[...], vc_ref[...]], axis=0)
            s = _dot_nt(qs, kc) * (HEAD_DIM ** -0.5)
            s = jnp.where(_band_mask(n), s, -jnp.inf)
            p = jnp.exp(s - _head_rows(lse_ref[...], hid))
            dp = _dot_nt(dos, vc)
            ds = (p * (dp + _head_rows(c_ref[...], hid)) * (HEAD_DIM ** -0.5)).astype(BF16)
            dq_ref[...] = _unstack_heads(_dot(ds, kc), hid)
            dkc = _dot_tn(ds, qs)
            dvc = _dot_tn(p.astype(BF16), dos)
            dk_ref[...] = carry_k[...] + dkc[:QB]
            dv_ref[...] = carry_v[...] + dvc[:QB]
            carry_k[...] = dkc[QB:]
            carry_v[...] = dvc[QB:]

        @pl.when(n == nb)
        def _():
            dk_ref[...] = carry_k[...]
            dv_ref[...] = carry_v[...]

    out = jax.ShapeDtypeStruct(q.shape, F32)
    return pl.pallas_call(
        body, name=name, grid=(r, nb + 1),
        in_specs=[qside, qside, qside, qside, kprev, kcur, kprev, kcur],
        out_specs=[qside, kout, kout], out_shape=[out, out, out],
        scratch_shapes=[pltpu.VMEM((QB, GW), F32), pltpu.VMEM((QB, GW), F32)],
        compiler_params=_params(),
    )(q, do, lse, cterm, k, k, v, v)


def _token_order(refs, scr):
    return [_from_residues(ref, slice(0, GW), r, scr) for ref, r in zip(refs, DILATIONS)]


def _group_weights(lses):
    l0, l1, l2 = lses
    m = jnp.maximum(jnp.maximum(l0, l1), l2)
    e = [jnp.exp(l - m) for l in (l0, l1, l2)]
    den = e[0] + e[1] + e[2]
    return [ei / den for ei in e]


def _mix_out_fwd(h, vec, gates, ypool, o3, lse3, wpb, wab, wout, name):
    S, D = h.shape

    def body(h_ref, vec_ref, gates_ref, yp_ref, o0, o1, o2, l0, l1, l2, wpb_hbm, wab_hbm, wout_hbm,
             hn_ref, ya_ref, merged_ref, tm_ref, wpb_v, wab_v, wout_v, sem, scr):
        _load_weights([(wpb_hbm, wpb_v), (wab_hbm, wab_v), (wout_hbm, wout_v)], sem)
        gt = vec_ref[3:4, :]
        wts = _group_weights(_token_order((l0, l1, l2), scr))
        og = _token_order((o0, o1, o2), scr)
        ya = (wts[0] * og[0] + wts[1] * og[1] + wts[2] * og[2]).astype(BF16)
        ya_ref[...] = ya
        merged = (gates_ref[:, :D] * _dot(yp_ref[...], wpb_v[...])
                  + gates_ref[:, D:] * _dot(ya, wab_v[...])).astype(BF16)
        merged_ref[...] = merged
        tm = _dot(merged, wout_v[...])
        tm_ref[...] = tm
        hn_ref[...] = h_ref[...] + gt * tm

    grp = _row_tile(TS, GW)
    res = [_res_spec(r) for r in DILATIONS]
    return pl.pallas_call(
        body, name=name, grid=(S // TS,),
        in_specs=[_row_tile(TS, D), _const((8, D)), _row_tile(TS, 2 * D), grp] + res * 2 + [ANY, ANY, ANY],
        out_specs=[_row_tile(TS, D), grp, _row_tile(TS, D), _row_tile(TS, D)],
        out_shape=[jax.ShapeDtypeStruct((S, D), F32), jax.ShapeDtypeStruct((S, GW), BF16),
                   jax.ShapeDtypeStruct((S, D), BF16), jax.ShapeDtypeStruct((S, D), F32)],
        scratch_shapes=[pltpu.VMEM((GW, D), BF16), pltpu.VMEM((GW, D), BF16), pltpu.VMEM((D, D), BF16),
                        pltpu.SemaphoreType.DMA((3,)), pltpu.VMEM(RES_SCRATCH, F32)],
        compiler_params=_params(),
    )(h, vec, gates, ypool, *o3, *lse3, wpb, wab, wout)


def _mix_out_bwd(dh, tm, vec, gates, ypool, o3, lse3, wpb, wab, wout, name, rider=None):
    S, D = dh.shape

    def body(*refs):
        if rider is None:
            return compute(*refs)
        host, mine = rider.split(refs, 14, 12)
        rider.head(mine, pl.program_id(0))
        compute(*host)
        rider.tail(mine, pl.program_id(0), S // TS)

    def compute(dh_ref, tm_ref, vec_ref, gates_ref, yp_ref, o0, o1, o2, l0, l1, l2, wpb_hbm, wab_hbm, wout_hbm,
                dtm_ref, dgl_ref, dypb_ref, dyab_ref, dyp_ref, do0, do1, do2, c0, c1, c2, red_ref,
                wpb_v, wab_v, wout_v, sem, scr):
        _load_weights([(wpb_hbm, wpb_v), (wab_hbm, wab_v), (wout_hbm, wout_v)], sem)

        @pl.when(pl.program_id(0) == 0)
        def _():
            red_ref[...] = jnp.zeros_like(red_ref)

        gt = vec_ref[3:4, :]
        dh_v = dh_ref[...]
        red_ref[0:1, :] += jnp.sum(tm_ref[...] * dh_v, axis=0, keepdims=True)
        dtm = (gt * dh_v).astype(BF16)
        dtm_ref[...] = dtm
        dm = _dot_nt(dtm, wout_v[...])
        wts = _group_weights(_token_order((l0, l1, l2), scr))
        og = _token_order((o0, o1, o2), scr)
        ya = wts[0] * og[0] + wts[1] * og[1] + wts[2] * og[2]
        ypb = _dot(yp_ref[...], wpb_v[...])
        yab = _dot(ya.astype(BF16), wab_v[...])
        gp = gates_ref[:, :D]
        ga = gates_ref[:, D:]
        dgl_ref[:, :D] = (dm * ypb * gp * (1.0 - gp)).astype(BF16)
        dgl_ref[:, D:] = (dm * yab * ga * (1.0 - ga)).astype(BF16)
        dypb = (dm * gp).astype(BF16)
        dyab = (dm * ga).astype(BF16)
        dypb_ref[...] = dypb
        dyab_ref[...] = dyab
        dyp_ref[...] = _dot_nt(dypb, wpb_v[...])
        dya = _dot_nt(dyab, wab_v[...])
        row = lax.broadcasted_iota(jnp.int32, (GW, GW), 0) // HEAD_DIM
        col = lax.broadcasted_iota(jnp.int32, (GW, GW), 1) // HEAD_DIM
        ones = jnp.where(row == col, 1.0, 0.0).astype(F32)
        tot = jnp.dot(dya * ya, ones, preferred_element_type=F32, precision=lax.Precision.HIGHEST)
        for wg, do_ref, c_ref, r in zip(wts, (do0, do1, do2), (c0, c1, c2), DILATIONS):
            _to_residues(wg * dya, do_ref, slice(0, GW), r, scr)
            _to_residues(-(wg * tot), c_ref, slice(0, GW), r, scr)

    grp = _row_tile(TS, GW)
    res = [_res_spec(r) for r in DILATIONS]
    specs = (
        [_row_tile(TS, D), _row_tile(TS, D), _const((8, D)), _row_tile(TS, 2 * D), grp] + res * 2
        + [ANY, ANY, ANY],
        [_row_tile(TS, D), _row_tile(TS, 2 * D), _row_tile(TS, D), _row_tile(TS, D), grp]
        + res * 2 + [_const((8, D))],
        [jax.ShapeDtypeStruct((S, D), BF16), jax.ShapeDtypeStruct((S, 2 * D), BF16),
         jax.ShapeDtypeStruct((S, D), BF16), jax.ShapeDtypeStruct((S, D), BF16), jax.ShapeDtypeStruct((S, GW), F32)]
        + [_res_shape(S, r, BF16) for r in DILATIONS] + [_res_shape(S, r, F32) for r in DILATIONS]
        + [jax.ShapeDtypeStruct((8, D), F32)],
        [pltpu.VMEM((GW, D), BF16), pltpu.VMEM((GW, D), BF16), pltpu.VMEM((D, D), BF16),
         pltpu.SemaphoreType.DMA((3,)), pltpu.VMEM(RES_SCRATCH, F32)])
    in_specs, out_specs, out_shape, scratch = specs if rider is None else rider.specs(*specs)
    outs = pl.pallas_call(
        body, name=name, grid=(S // TS,), in_specs=in_specs, out_specs=out_specs, out_shape=out_shape,
        scratch_shapes=scratch, compiler_params=_params(),
    )(dh, tm, vec, gates, ypool, *o3, *lse3, wpb, wab, wout, *(rider.arrays if rider else []))
    return outs if rider is None else (outs[:12], outs[12:])


def _final_loss(h, target, gf, name):
    S, D = h.shape

    def body(h_ref, t_ref, g_ref, dh_ref, loss_ref, dg_ref):
        @pl.when(pl.program_id(0) == 0)
        def _():
            loss_ref[...] = jnp.zeros_like(loss_ref)
            dg_ref[...] = jnp.zeros_like(dg_ref)

        x = h_ref[...]
        g = g_ref[0:1, :]
        r = lax.rsqrt(jnp.mean(x * x, axis=-1, keepdims=True) + EPS)
        xn = x * r
        err = xn * g - t_ref[...]
        loss_ref[...] += 0.5 * jnp.sum(jnp.mean(err * err, axis=-1, keepdims=True))
        dy = err * (1.0 / D)
        dg_ref[0:1, :] += jnp.sum(dy * xn, axis=0, keepdims=True)
        dxn = dy * g
        dh_ref[...] = r * (dxn - xn * jnp.mean(dxn * xn, axis=-1, keepdims=True))

    return pl.pallas_call(
        body, name=name, grid=(S // TS,),
        in_specs=[_row_tile(TS, D), _row_tile(TS, D), _const((8, D))],
        out_specs=[_row_tile(TS, D), _const((8, 128)), _const((8, D))],
        out_shape=[jax.ShapeDtypeStruct((S, D), F32), jax.ShapeDtypeStruct((8, 128), F32),
                   jax.ShapeDtypeStruct((8, D), F32)],
        compiler_params=_params(),
    )(h, target, gf)


def _ada_mod(c_all, w, b, name):
    def body(c_ref, w_ref, b_ref, cond_ref, mod_ref):
        cv = c_ref[...]
        cond = cv * jax.nn.sigmoid(cv)
        cond_ref[...] = cond
        mod_ref[...] = jnp.dot(cond, w_ref[...], preferred_element_type=F32,
                               precision=lax.Precision.HIGHEST) + b_ref[...]

    return pl.pallas_call(
        body, name=name,
        out_shape=[jax.ShapeDtypeStruct(c_all.shape, F32), jax.ShapeDtypeStruct((c_all.shape[0], w.shape[1]), F32)],
        compiler_params=_params(),
    )(c_all, w, b)


def _adamw_math(w, g, m, v):
    m = ADAM_B1 * m + (1.0 - ADAM_B1) * g
    v = ADAM_B2 * v + (1.0 - ADAM_B2) * (g * g)
    m_hat = m / (1.0 - ADAM_B1 ** ADAM_STEP)
    v_hat = v / (1.0 - ADAM_B2 ** ADAM_STEP)
    delta = -ADAM_LR * (m_hat / (jnp.sqrt(v_hat) + ADAM_EPS) + ADAM_WD * w)
    return delta, m, v


def _adamw(w, g, m, v, name):
    R, C = w.shape
    tr = R
    for cand in (256, 128, 64, 32, 16, 8):
        if R % cand == 0:
            tr = cand
            break

    def body(w_ref, g_ref, m_ref, v_ref, d_ref, mo_ref, vo_ref):
        d_ref[...], mo_ref[...], vo_ref[...] = _adamw_math(w_ref[...], g_ref[...], m_ref[...], v_ref[...])

    spec = _row_tile(tr, C)
    out = jax.ShapeDtypeStruct((R, C), F32)
    return pl.pallas_call(
        body, name=name, grid=(R // tr,), in_specs=[spec] * 4, out_specs=[spec] * 3, out_shape=[out] * 3,
        compiler_params=_params(),
    )(w, g, m, v)


def _ada_grad_adamw(cond_t, dmod, w, m, v, name, tr=256):
    R, C = w.shape
    nb = dmod.shape[0]

    def body(ct_ref, dm_ref, w_ref, m_ref, v_ref, g_ref, d_ref, mo_ref, vo_ref):
        ct = ct_ref[...]
        dm = dm_ref[...]
        g = jnp.zeros((tr, C), F32)
        for bi in range(nb):
            g = g + ct[:, bi:bi + 1] * dm[bi:bi + 1, :]
        g_ref[...] = g
        d_ref[...], mo_ref[...], vo_ref[...] = _adamw_math(w_ref[...], g, m_ref[...], v_ref[...])

    spec = _row_tile(tr, C)
    out = jax.ShapeDtypeStruct((R, C), F32)
    return pl.pallas_call(
        body, name=name, grid=(R // tr,),
        in_specs=[_row_tile(tr, nb), _const((nb, C)), spec, spec, spec],
        out_specs=[spec] * 4, out_shape=[out] * 4,
        compiler_params=_params(),
    )(cond_t, dmod, w, m, v)


def _row_step(rows, cap=256):
    for cand in range(cap, 15, -16):
        if rows % cand == 0:
            return cand
    return rows


def _slot_sum(x_ref):
    acc = x_ref[0].astype(F32)
    for k in range(1, x_ref.shape[0]):
        acc = acc + x_ref[k].astype(F32)
    return acc


def _sum_slots(x, name, out_dtype=F32):
    n, R, C = x.shape
    tr = _row_step(R)

    def body(x_ref, o_ref):
        o_ref[...] = _slot_sum(x_ref).astype(out_dtype)

    return pl.pallas_call(
        body, name=name, grid=(R // tr,),
        in_specs=[pl.BlockSpec((n, tr, C), lambda i: (0, i, 0))],
        out_specs=_row_tile(tr, C), out_shape=jax.ShapeDtypeStruct((R, C), out_dtype),
        compiler_params=_params(),
    )(x)


def _sum_pair(core, g, recv, name):
    _, _, R, C = g.shape
    tr = _row_step(R)

    def body(core_ref, g_ref, r_ref, o_ref):
        o_ref[...] = (g_ref[...].astype(F32) + r_ref[...].astype(F32)).astype(BF16)

    return pl.pallas_call(
        body, name=name, out_shape=jax.ShapeDtypeStruct((4, R, C), BF16),
        grid_spec=pltpu.PrefetchScalarGridSpec(
            num_scalar_prefetch=1, grid=(4, R // tr),
            in_specs=[pl.BlockSpec((None, None, tr, C), lambda k, i, core_ref: (k, core_ref[0], i, 0)),
                      pl.BlockSpec((None, tr, C), lambda k, i, core_ref: (k, i, 0))],
            out_specs=pl.BlockSpec((None, tr, C), lambda k, i, core_ref: (k, i, 0))),
        compiler_params=_params(),
    )(core, g, recv)


def _sum_adamw(chip, own, recv, w, m, v, name):
    _, R, C = own.shape
    tr = _row_step(R)

    def body(chip_ref, own_ref, r_ref, w_ref, m_ref, v_ref, g_ref, d_ref, mo_ref, vo_ref):
        g = own_ref[...].astype(F32) + _slot_sum(r_ref)
        g_ref[...] = g
        d_ref[...], mo_ref[...], vo_ref[...] = _adamw_math(w_ref[...], g, m_ref[...], v_ref[...])

    spec = pl.BlockSpec((tr, C), lambda i, chip_ref: (i, 0))
    out = jax.ShapeDtypeStruct((R, C), F32)
    return pl.pallas_call(
        body, name=name, out_shape=[out] * 4,
        grid_spec=pltpu.PrefetchScalarGridSpec(
            num_scalar_prefetch=1, grid=(R // tr,),
            in_specs=[pl.BlockSpec((None, tr, C), lambda i, chip_ref: (chip_ref[0], i, 0)),
                      pl.BlockSpec((3, tr, C), lambda i, chip_ref: (0, i, 0)), spec, spec, spec],
            out_specs=[spec] * 4),
        compiler_params=_params(),
    )(chip, own, recv, w, m, v)


def _place():
    return lax.axis_index("x"), lax.axis_index("y"), lax.axis_index("c")


def _gather_steps(x_refs, out_refs, send_sems, recv_sems):
    n = len(x_refs)
    x, y, c = _place()
    me, sibling = (x, y, c), (x, y, 1 - c)
    chips = [(1 - x, y), (x, 1 - y), (1 - x, 1 - y)]

    def rows(a, px, py, pc):
        return out_refs[a].at[4 * px + 2 * py + pc]

    def copy(a, k, block, to, src=None):
        return pltpu.make_async_remote_copy(
            src_ref=rows(a, *block) if src is None else src, dst_ref=rows(a, *block),
            send_sem=send_sems.at[a, k], recv_sem=recv_sems.at[a, k], device_id=to, device_id_type=MESH)

    def first(a):
        return [copy(a, 0, me, sibling, src=x_refs[a])] + [
            copy(a, 1 + j, me, (*chip, c), src=x_refs[a]) for j, chip in enumerate(chips)]

    def passed(a, j):
        return copy(a, 4 + j, (*chips[j], c), sibling)

    def start():
        for a in range(n):
            for cp in first(a):
                cp.start()

    def relay():
        for j, chip in enumerate(chips):
            for a in range(n):
                copy(a, 1 + j, (*chip, c), me).wait_recv()
                passed(a, j).start()

    def finish():
        for a in range(n):
            copy(a, 0, sibling, me).wait_recv()
            for j, chip in enumerate(chips):
                copy(a, 4 + j, (*chip, 1 - c), me).wait_recv()
        for a in range(n):
            for cp in first(a) + [passed(a, j) for j in range(3)]:
                cp.wait_send()

    return start, relay, finish


def _all_gather(arrs, name, own=True):
    n = len(arrs)

    def body(*refs):
        x_refs, out_refs = refs[:n], refs[n:2 * n]
        send_sems, recv_sems, local_sems = refs[2 * n:]
        me = 4 * lax.axis_index("x") + 2 * lax.axis_index("y") + lax.axis_index("c")
        mine = [pltpu.make_async_copy(x_refs[a], out_refs[a].at[me], local_sems.at[a]) for a in range(n)] if own else []
        for cp in mine:
            cp.start()
        for step in _gather_steps(x_refs, out_refs, send_sems, recv_sems):
            step()
        for cp in mine:
            cp.wait()

    return pl.pallas_call(
        body, name=name, out_shape=[jax.ShapeDtypeStruct((N_DEV,) + t.shape, t.dtype) for t in arrs],
        in_specs=[ANY] * n, out_specs=[ANY] * n,
        scratch_shapes=[pltpu.SemaphoreType.DMA((n, 7)), pltpu.SemaphoreType.DMA((n, 7)),
                        pltpu.SemaphoreType.DMA((n,))],
    )(*arrs)


def _pair_exchange(arrs, name):
    n = len(arrs)

    def body(*refs):
        g_refs, out_refs = refs[:n], refs[n:2 * n]
        send_sems, recv_sems = refs[2 * n:]
        x, y, c = _place()
        give = [pltpu.make_async_remote_copy(
            src_ref=g_refs[a].at[pl.ds(0, 4), 1 - c], dst_ref=out_refs[a], send_sem=send_sems.at[a],
            recv_sem=recv_sems.at[a], device_id=(x, y, 1 - c), device_id_type=MESH) for a in range(n)]
        for cp in give:
            cp.start()
        for cp in give:
            cp.wait()

    return pl.pallas_call(
        body, name=name,
        out_shape=[jax.ShapeDtypeStruct((4,) + t.shape[2:], t.dtype) for t in arrs],
        in_specs=[ANY] * n, out_specs=[ANY] * n,
        scratch_shapes=[pltpu.SemaphoreType.DMA((n,)), pltpu.SemaphoreType.DMA((n,))],
    )(*arrs)


def _chip_exchange_steps(p_refs, out_refs, send_sems, recv_sems):
    x, y, c = _place()
    chips = [(1 - x, y), (x, 1 - y), (1 - x, 1 - y)]

    def copies():
        return [pltpu.make_async_remote_copy(
            src_ref=p_refs[a].at[2 * px + py], dst_ref=out_refs[a].at[j], send_sem=send_sems.at[a, j],
            recv_sem=recv_sems.at[a, j], device_id=(px, py, c), device_id_type=MESH)
            for a in range(len(p_refs)) for j, (px, py) in enumerate(chips)]

    def start():
        for cp in copies():
            cp.start()

    def finish():
        for cp in copies():
            cp.wait()

    return start, finish


def _chip_exchange(arrs, name):
    n = len(arrs)

    def body(*refs):
        for step in _chip_exchange_steps(refs[:n], refs[n:2 * n], *refs[2 * n:]):
            step()

    return pl.pallas_call(
        body, name=name, out_shape=[jax.ShapeDtypeStruct((3,) + t.shape[1:], t.dtype) for t in arrs],
        in_specs=[ANY] * n, out_specs=[ANY] * n,
        scratch_shapes=[pltpu.SemaphoreType.DMA((n, 3)), pltpu.SemaphoreType.DMA((n, 3))],
    )(*arrs)


class _Rider:
    def __init__(self, arrays, out_shape, sems, steps, relay_before_end=None):
        self.arrays, self.out_shape, self.scratch, self.steps = list(arrays), out_shape, sems, steps
        self.n = len(self.arrays)
        self.relay_before_end = relay_before_end

    def specs(self, in_specs, out_specs, out_shape, scratch):
        extra = [ANY] * self.n
        return in_specs + extra, out_specs + extra, out_shape + self.out_shape, scratch + self.scratch

    def split(self, refs, n_in, n_out):
        k = self.n
        a, b = n_in + k, n_in + k + n_out
        return refs[:n_in] + refs[a:b] + refs[b + k:-2], (refs[n_in:a], refs[b:b + k], refs[-2:])

    def head(self, mine, step):
        pl.when(step == 0)(self.steps(mine[0], mine[1], *mine[2])[0])

    def tail(self, mine, step, nsteps):
        steps = self.steps(mine[0], mine[1], *mine[2])
        if self.relay_before_end is not None:
            pl.when(step == nsteps - 1 - self.relay_before_end)(steps[1])
        pl.when(step == nsteps - 1)(steps[-1])


def _gather_rider(arrs, relay_before_end=4):
    n = len(arrs)
    return _Rider(arrs, [jax.ShapeDtypeStruct((N_DEV,) + t.shape, t.dtype) for t in arrs],
                  [pltpu.SemaphoreType.DMA((n, 7)), pltpu.SemaphoreType.DMA((n, 7))], _gather_steps,
                  relay_before_end)


def _chip_exchange_rider(arrs):
    n = len(arrs)
    return _Rider(arrs, [jax.ShapeDtypeStruct((3,) + t.shape[1:], t.dtype) for t in arrs],
                  [pltpu.SemaphoreType.DMA((n, 3)), pltpu.SemaphoreType.DMA((n, 3))], _chip_exchange_steps)


def _rope_tables(positions):
    inv_freq = ROPE_THETA ** (-jnp.arange(0, HEAD_DIM, 2, dtype=F32) / HEAD_DIM)
    ang = positions.astype(F32)[:, None] * inv_freq
    cos, sin = jnp.cos(ang), jnp.sin(ang)
    return jnp.tile(cos, (1, 4)), jnp.tile(jnp.concatenate([-sin, sin], axis=1), (1, 2))


def _vec(g, shift, scale, gate):
    z = jnp.zeros_like(g)
    return jnp.stack([g, shift, scale, gate, z, z, z, z])


class _GradReducer:
    def __init__(self):
        self.core = lax.axis_index("c").astype(jnp.int32).reshape(1)
        self.own, self.others, self.waiting, self.riding = {}, {}, [], []

    def pair(self, named):
        keys = list(named)
        mine = [named[k].reshape((4, 2) + named[k].shape[1:]) for k in keys]
        theirs = _pair_exchange(mine, "reduce_pair_" + keys[0])
        for k, g, r in zip(keys, mine, theirs):
            self.own[k] = _sum_pair(self.core, g, r, "sum_pair_" + k)
        self.waiting += keys

    def rider(self):
        self.riding, self.waiting = self.waiting, []
        return _chip_exchange_rider([self.own[k] for k in self.riding])

    def landed(self, results):
        self.others.update(zip(self.riding, results))

    def flush(self, name):
        keys, self.waiting = self.waiting, []
        self.others.update(zip(keys, _chip_exchange([self.own[k] for k in keys], name)))


def _by_owner(g):
    if g.ndim == 3:
        return g if g.shape[0] == N_DEV else g.reshape(N_DEV, g.shape[1] * g.shape[0] // N_DEV, g.shape[2])
    return g.reshape(N_DEV, g.shape[0] // N_DEV, g.shape[1])


def _local_step(x, target, positions, mod, small, W, late=None, red=None):
    S, D = x.shape
    sh1, sc1, gt1, sh2, sc2, gt2, sh3, sc3, gt3 = (mod[k] for k in range(9))
    v1 = _vec(small["g1"], sh1, sc1, gt1)
    v2 = _vec(small["g2"], sh2, sc2, gt2)
    v3 = _vec(small["g3"], sh3, sc3, gt3)
    vf = _vec(small["gf"], small["gf"], small["gf"], small["gf"])
    cos, sin = _rope_tables(positions)
    wbd = jax.scipy.linalg.block_diag(*[small["w_pool"][k] for k in range(4)]).astype(BF16)
    pscale = small["pool_scale"].reshape(1, GW)

    if late is None:
        h1, u1, ab1, act1, f1 = _ffn_fwd(x, v1, W["w1in"], W["w1out"], "ffn1_fwd")
    else:
        (h1, u1, ab1, act1, f1), landed = _ffn_fwd(x, v1, W["w1in"], W["w1out"], "ffn1_fwd", rider=late[0])
        W = {**W, **late[1](landed)}
    u2, p, gates, *qkv = _mix_in_fwd(h1, v2, cos, sin, W["win"], "mix_in_fwd")
    dpool, ypool = _pool_fwd(p, wbd, pscale, "pool_fwd")
    o3, lse3 = [], []
    for gi in range(len(DILATIONS)):
        o, lse = _attn_fwd(qkv[gi], qkv[3 + gi], qkv[6 + gi], f"attn_fwd_{gi}")
        o3.append(o)
        lse3.append(lse)
    h2, ya, merged, tm = _mix_out_fwd(h1, v2, gates, ypool, o3, lse3, W["wpb"], W["wab"], W["wout"], "mix_out_fwd")
    h3, u3, ab3, act3, f3 = _ffn_fwd(h2, v3, W["w2in"], W["w2out"], "ffn2_fwd")
    dh3, loss_blk, dgf = _final_loss(h3, target, vf, "final_loss")

    dh2, dab3, df3, red3 = _ffn_bwd(dh3, h2, f3, ab3, v3, W["w2in"], W["w2out"], "ffn2_bwd")
    G = {"w2in": _wgrad(dab3, u3, "wgrad_2in"), "w2out": _by_owner(_wgrad(act3, df3, "wgrad_2out"))}
    mix_out_args = (dh2, tm, v2, gates, ypool, o3, lse3, W["wpb"], W["wab"], W["wout"], "mix_out_bwd")
    if red is None:
        mix_out = _mix_out_bwd(*mix_out_args)
    else:
        red.pair({k: G[k] for k in ("w2in", "w2out")})
        mix_out, landed = _mix_out_bwd(*mix_out_args, rider=red.rider())
        red.landed(landed)
    (dtm, dgl, dypb, dyab, dyp, do0, do1, do2, c0, c1, c2, red2o) = mix_out
    dq3, dk3, dv3 = [], [], []
    for gi, (do, ct) in enumerate(zip((do0, do1, do2), (c0, c1, c2))):
        dq, dk, dv = _attn_bwd(qkv[gi], qkv[3 + gi], qkv[6 + gi], do, lse3[gi], ct, f"attn_bwd_{gi}")
        dq3.append(dq)
        dk3.append(dk)
        dv3.append(dv)
    dp, dwbd, dps = _pool_bwd(dyp, dpool, wbd, pscale, "pool_bwd")
    dh1, dproj, red2i = _mix_in_bwd(dh2, h1, v2, cos, sin, dp, dq3 + dk3 + dv3, dgl, W["win"], "mix_in_bwd")
    G["win"] = _by_owner(_wgrad(dproj, u2, "wgrad_in", tm=1152))
    G["wpb"] = _full_to_cols(_wgrad(ypool, dypb, "wgrad_pb"))
    G["wab"] = _full_to_cols(_wgrad(ya, dyab, "wgrad_ab"))
    G["wout"] = _by_owner(_wgrad(merged, dtm, "wgrad_out"))
    if red is not None:
        red.pair({k: G[k] for k in ("win", "wpb", "wab", "wout")})
    dx, dab1, df1, red1 = _ffn_bwd(dh1, x, f1, ab1, v1, W["w1in"], W["w1out"], "ffn1_bwd")
    G["w1out"] = _by_owner(_wgrad(act1, df1, "wgrad_1out"))
    if red is None:
        G["w1in"] = _wgrad(dab1, u1, "wgrad_1in")
    else:
        red.pair({"w1out": G["w1out"]})
        G["w1in"], landed = _wgrad(dab1, u1, "wgrad_1in", rider=red.rider())
        red.landed(landed)
        red.pair({"w1in": G["w1in"]})
        red.flush("reduce_chips_w1in")
    dmod = jnp.stack([red1[1], red1[2], red1[0], red2i[1], red2i[2], red2o[0], red3[1], red3[2], red3[0]])
    dsmall = {
        "g1": red1[3], "g2": red2i[3], "g3": red3[3], "gf": dgf[0],
        "w_pool": jnp.stack([dwbd[k * 64:(k + 1) * 64, k * 64:(k + 1) * 64] for k in range(4)]),
        "pool_scale": dps[0],
    }
    return loss_blk[0, 0], dx, G, dmod, dsmall


SHARDED = ("w_ffn1_in", "w_ffn1_out", "w_in", "w_pool_branch", "w_attn_branch", "w_out", "w_ffn2_in", "w_ffn2_out")
TRANSPOSED = ("w_ffn1_in", "w_in", "w_ffn2_in")
FIRST = ("w_ffn1_in", "w_ffn1_out")
LATER = tuple(n for n in SHARDED if n not in FIRST)
GRAD_KEY = dict(w_ffn1_in="w1in", w_ffn1_out="w1out", w_in="win", w_pool_branch="wpb", w_attn_branch="wab",
                w_out="wout", w_ffn2_in="w2in", w_ffn2_out="w2out")


def _cols_to_full(g):
    return jnp.concatenate([g[j] for j in range(N_DEV)], axis=1)


def _full_to_cols(t):
    c = t.shape[1] // N_DEV
    return jnp.stack([t[:, j * c:(j + 1) * c] for j in range(N_DEV)])


SMALL =(("b_ada", 9216), ("g_norm_ffn1", 1024), ("g_norm_mix", 1024), ("g_norm_ffn2", 1024), ("g_final", 1024),
         ("w_pool", 16384), ("pool_scale", 256))
SMALL_ROWS = 240


def _pack_small(vals, loss=None):
    flat = jnp.concatenate([vals[name].reshape(-1) for name, _ in SMALL])
    tail = jnp.zeros((SMALL_ROWS * 128 - flat.shape[0],), F32)
    if loss is not None:
        tail = tail.at[0].set(loss)
    return jnp.concatenate([flat, tail]).reshape(SMALL_ROWS, 128)


def _unpack_small(slab, shapes):
    flat, out, off = slab.reshape(-1), {}, 0
    for name, n in SMALL:
        out[name] = flat[off:off + n].reshape(shapes[name])
        off += n
    return out, flat[off]


def kernel(x, c, positions, w_ada, b_ada, g_norm_ffn1, w_ffn1_in, w_ffn1_out, g_norm_mix, w_in, w_pool, pool_scale, w_pool_branch, w_attn_branch, w_out, g_norm_ffn2, w_ffn2_in, w_ffn2_out, g_final, loss_target, m_w_ada, m_b_ada, m_g_norm_ffn1, m_w_ffn1_in, m_w_ffn1_out, m_g_norm_mix, m_w_in, m_w_pool, m_pool_scale, m_w_pool_branch, m_w_attn_branch, m_w_out, m_g_norm_ffn2, m_w_ffn2_in, m_w_ffn2_out, m_g_final, v_w_ada, v_b_ada, v_g_norm_ffn1, v_w_ffn1_in, v_w_ffn1_out, v_g_norm_mix, v_w_in, v_w_pool, v_pool_scale, v_w_pool_branch, v_w_attn_branch, v_w_out, v_g_norm_ffn2, v_w_ffn2_in, v_w_ffn2_out, v_g_final):
    names = ["w_ada", "b_ada", "g_norm_ffn1", "w_ffn1_in", "w_ffn1_out", "g_norm_mix", "w_in", "w_pool", "pool_scale",
             "w_pool_branch", "w_attn_branch", "w_out", "g_norm_ffn2", "w_ffn2_in", "w_ffn2_out", "g_final"]
    w = dict(w_ada=w_ada, b_ada=b_ada, g_norm_ffn1=g_norm_ffn1, w_ffn1_in=w_ffn1_in, w_ffn1_out=w_ffn1_out,
             g_norm_mix=g_norm_mix, w_in=w_in, w_pool=w_pool, pool_scale=pool_scale, w_pool_branch=w_pool_branch,
             w_attn_branch=w_attn_branch, w_out=w_out, g_norm_ffn2=g_norm_ffn2, w_ffn2_in=w_ffn2_in,
             w_ffn2_out=w_ffn2_out, g_final=g_final)
    m = dict(w_ada=m_w_ada, b_ada=m_b_ada, g_norm_ffn1=m_g_norm_ffn1, w_ffn1_in=m_w_ffn1_in, w_ffn1_out=m_w_ffn1_out,
             g_norm_mix=m_g_norm_mix, w_in=m_w_in, w_pool=m_w_pool, pool_scale=m_pool_scale,
             w_pool_branch=m_w_pool_branch, w_attn_branch=m_w_attn_branch, w_out=m_w_out, g_norm_ffn2=m_g_norm_ffn2,
             w_ffn2_in=m_w_ffn2_in, w_ffn2_out=m_w_ffn2_out, g_final=m_g_final)
    v = dict(w_ada=v_w_ada, b_ada=v_b_ada, g_norm_ffn1=v_g_norm_ffn1, w_ffn1_in=v_w_ffn1_in, w_ffn1_out=v_w_ffn1_out,
             g_norm_mix=v_g_norm_mix, w_in=v_w_in, w_pool=v_w_pool, pool_scale=v_pool_scale,
             w_pool_branch=v_w_pool_branch, w_attn_branch=v_w_attn_branch, w_out=v_w_out, g_norm_ffn2=v_g_norm_ffn2,
             w_ffn2_in=v_w_ffn2_in, w_ffn2_out=v_w_ffn2_out, g_final=v_g_final)
    shapes = {n: w[n].shape for n in names}
    me = 4 * lax.axis_index("x") + 2 * lax.axis_index("y") + lax.axis_index("c")
    D = x.shape[-1]
    n_mod = w_ada.shape[-1] * N_DEV // D

    (c_all,) = _all_gather([c.reshape(D // 128, 128)], "gather_c")
    ada_cols = w_ada.shape[-1]
    b_mine = lax.dynamic_slice_in_dim(b_ada, me * ada_cols, ada_cols, axis=1)
    cond, mod_part = _ada_mod(c_all.reshape(N_DEV, D), w_ada[0], b_mine, "ada_mod")
    (mod_all,) = _all_gather([mod_part.reshape(-1, 128)], "gather_mod")
    mod_all = mod_all.reshape(N_DEV, N_DEV, ada_cols)
    mod = lax.dynamic_index_in_dim(mod_all, me, axis=1, keepdims=False).reshape(n_mod, D)

    def local(t, name):
        return t[name][0].T if name in TRANSPOSED else t[name][0]

    shards = {name: local(w, name).astype(BF16) for name in SHARDED}

    def gather_done(names, fulls):
        return {name: lax.dynamic_update_index_in_dim(full, shards[name], me, axis=0)
                for name, full in zip(names, fulls)}

    def ffn_weights(g, pre):
        out = g["w_ffn%s_out" % pre]
        return {"w%sin" % pre: g["w_ffn%s_in" % pre],
                "w%sout" % pre: out.reshape(N_DEV // 2, 2 * out.shape[1], out.shape[2])}

    def later_weights(fulls):
        g = gather_done(LATER, fulls)
        return dict(win=g["w_in"].reshape(-1, D), wpb=_cols_to_full(g["w_pool_branch"]),
                    wab=_cols_to_full(g["w_attn_branch"]), wout=g["w_out"].reshape(D, D), **ffn_weights(g, "2"))

    W = ffn_weights(gather_done(FIRST, _all_gather([shards[n] for n in FIRST], "gather_ffn1", own=False)), "1")
    small = dict(g1=g_norm_ffn1[0], g2=g_norm_mix[0], g3=g_norm_ffn2[0], gf=g_final, w_pool=w_pool[0],
                 pool_scale=pool_scale[0])
    red = _GradReducer()
    loss_part, dx, G, dmod, dsmall = _local_step(
        x[0], loss_target[0], positions[0], mod, small, W,
        late=(_gather_rider([shards[n] for n in LATER]), later_weights), red=red)
    chip = (2 * lax.axis_index("x") + lax.axis_index("y")).astype(jnp.int32).reshape(1)

    part = _pack_small(dict(b_ada=dmod, g_norm_ffn1=dsmall["g1"], g_norm_mix=dsmall["g2"], g_norm_ffn2=dsmall["g3"],
                            g_final=dsmall["gf"], w_pool=dsmall["w_pool"], pool_scale=dsmall["pool_scale"]),
                       loss=loss_part)
    (parts,) = _all_gather([part], "gather_small")
    gsmall, loss = _unpack_small(_sum_slots(parts, "sum_small"), shapes)
    rows_mine = ada_cols // 128
    dmod_mine = lax.dynamic_slice_in_dim(parts, me * rows_mine, rows_mine, axis=1).reshape(N_DEV, ada_cols)

    grads, delta, new_m, new_v = {}, {}, {}, {}
    grads["w_ada"], delta["w_ada"], new_m["w_ada"], new_v["w_ada"] = (
        t[None] for t in _ada_grad_adamw(cond.T, dmod_mine, w_ada[0], m_w_ada[0], v_w_ada[0], "ada_grad_adamw"))
    for name in SHARDED:
        key = GRAD_KEY[name]
        res = _sum_adamw(chip, red.own[key], red.others[key], local(w, name), local(m, name), local(v, name),
                         "adamw_" + name)
        grads[name], delta[name], new_m[name], new_v[name] = (
            (t.T if name in TRANSPOSED else t)[None] for t in res)
    sd, sm, sv = _adamw(_pack_small(w), _pack_small(gsmall), _pack_small(m), _pack_small(v), "adamw_small")
    for dst, src in ((delta, sd), (new_m, sm), (new_v, sv)):
        dst.update(_unpack_small(src, shapes)[0])
    grads.update(gsmall)

    return (loss, dx[None], *[grads[n] for n in names], *[delta[n] for n in names],
            *[new_m[n] for n in names], *[new_v[n] for n in names])
```

```python
import functools

import jax
import jax.numpy as jnp
from jax import lax
from jax.experimental import pallas as pl
from jax.experimental.pallas import tpu as pltpu

F32 = jnp.float32
BF16 = jnp.bfloat16
MESH = pl.DeviceIdType.MESH
ANY = pl.BlockSpec(memory_space=pl.ANY)

N_DEV = 8
EPS = 1e-6
HEAD_DIM = 64
HEADS = 4
GW = HEADS * HEAD_DIM
DILATIONS = (1, 4, 16)
BAND = 128
QB = 128
POOL_WINDOWS = (2, 4, 8, 16)
HALO = 16
ROPE_THETA = 10000.0

ADAM_LR = 0.001
ADAM_B1 = 0.9
ADAM_B2 = 0.999
ADAM_EPS = 1e-08
ADAM_WD = 0.01
ADAM_STEP = 10

VMEM_LIMIT = 56 * 1024 * 1024
TS = 256
FC = 2816

NT = (((1,), (1,)), ((), ()))
TN = (((0,), (0,)), ((), ()))


def _params(**kw):
    return pltpu.CompilerParams(vmem_limit_bytes=VMEM_LIMIT, **kw)


def _dot(a, b):
    return jnp.dot(a, b, preferred_element_type=F32)


def _dot_nt(a, b):
    return lax.dot_general(a, b, NT, preferred_element_type=F32)


def _dot_tn(a, b):
    return lax.dot_general(a, b, TN, preferred_element_type=F32)


def _load_weights(pairs, sem):
    @pl.when(pl.program_id(0) == 0)
    def _():
        copies = [pltpu.make_async_copy(src, dst, sem.at[i]) for i, (src, dst) in enumerate(pairs)]
        for cp in copies:
            cp.start()
        for cp in copies:
            cp.wait()


def _norm_mod(x, g, sc, sh):
    r = lax.rsqrt(jnp.mean(x * x, axis=-1, keepdims=True) + EPS)
    xn = x * r
    y = xn * g
    return r, xn, y, y * (1.0 + sc) + sh


def _norm_mod_bwd(du, r, xn, y, g, sc):
    dsh = jnp.sum(du, axis=0, keepdims=True)
    dsc = jnp.sum(du * y, axis=0, keepdims=True)
    dy = du * (1.0 + sc)
    dg = jnp.sum(dy * xn, axis=0, keepdims=True)
    dxn = dy * g
    dx = r * (dxn - xn * jnp.mean(dxn * xn, axis=-1, keepdims=True))
    return dx, dsh, dsc, dg


def _row_tile(ts, width):
    return pl.BlockSpec((ts, width), lambda i: (i, 0))


def _const(shape):
    return pl.BlockSpec(shape, lambda *_: (0,) * len(shape))


def _ffn_fwd(h, vec, win, wout, name, rider=None):
    S, D = h.shape
    _, Fd, _ = win.shape
    nch = Fd // FC

    def body(*refs):
        if rider is None:
            return compute(*refs)
        host, mine = rider.split(refs, 4, 5)
        rider.head(mine, pl.program_id(0))
        compute(*host)
        rider.tail(mine, pl.program_id(0), S // TS)

    def compute(h_ref, vec_ref, win_hbm, wout_hbm, hn_ref, u_ref, ab_ref, act_ref, f_ref, win_v, wout_v, sem):
        _load_weights([(win_hbm, win_v), (wout_hbm, wout_v)], sem)
        x = h_ref[...]
        g, sh, sc, gt = (vec_ref[k:k + 1, :] for k in range(4))
        _, _, _, u = _norm_mod(x, g, sc, sh)
        ub = u.astype(BF16)
        u_ref[...] = ub
        acc = jnp.zeros((TS, D), F32)
        for j in range(nch):
            sl = slice(j * FC, (j + 1) * FC)
            a = _dot_nt(ub, win_v[0, sl, :])
            b = _dot_nt(ub, win_v[1, sl, :])
            act = ((a * jax.nn.sigmoid(a)) * b).astype(BF16)
            ab_ref[0, :, sl] = a.astype(BF16)
            ab_ref[1, :, sl] = b.astype(BF16)
            act_ref[:, sl] = act
            acc = acc + _dot(act, wout_v[sl, :])
        f_ref[...] = acc
        hn_ref[...] = x + (0.5 * gt) * acc

    specs = (
        [_row_tile(TS, D), _const((8, D)), ANY, ANY],
        [_row_tile(TS, D), _row_tile(TS, D), pl.BlockSpec((2, TS, Fd), lambda i: (0, i, 0)),
         _row_tile(TS, Fd), _row_tile(TS, D)],
        [jax.ShapeDtypeStruct((S, D), F32), jax.ShapeDtypeStruct((S, D), BF16),
         jax.ShapeDtypeStruct((2, S, Fd), BF16), jax.ShapeDtypeStruct((S, Fd), BF16),
         jax.ShapeDtypeStruct((S, D), F32)],
        [pltpu.VMEM(win.shape, BF16), pltpu.VMEM(wout.shape, BF16), pltpu.SemaphoreType.DMA((2,))])
    in_specs, out_specs, out_shape, scratch = specs if rider is None else rider.specs(*specs)
    outs = pl.pallas_call(
        body, name=name, grid=(S // TS,), in_specs=in_specs, out_specs=out_specs, out_shape=out_shape,
        scratch_shapes=scratch, compiler_params=_params(),
    )(h, vec, win, wout, *(rider.arrays if rider else []))
    return outs if rider is None else (outs[:5], outs[5:])


def _ffn_bwd(dh, h, f, ab, vec, win, wout, name):
    S, D = h.shape
    _, Fd, _ = win.shape
    nch = Fd // FC

    def body(dh_ref, h_ref, f_ref, ab_ref, vec_ref, win_hbm, wout_hbm,
             dhp_ref, dab_ref, df_ref, red_ref, win_v, wout_v, sem):
        _load_weights([(win_hbm, win_v), (wout_hbm, wout_v)], sem)

        @pl.when(pl.program_id(0) == 0)
        def _():
            red_ref[...] = jnp.zeros_like(red_ref)

        dh_v = dh_ref[...]
        x = h_ref[...]
        g, sh, sc, gt = (vec_ref[k:k + 1, :] for k in range(4))
        dgt = jnp.sum((0.5 * f_ref[...]) * dh_v, axis=0, keepdims=True)
        dfb = ((0.5 * gt) * dh_v).astype(BF16)
        df_ref[...] = dfb
        du = jnp.zeros((TS, D), F32)
        for j in range(nch):
            sl = slice(j * FC, (j + 1) * FC)
            dact = _dot_nt(dfb, wout_v[sl, :])
            av = ab_ref[0, :, sl].astype(F32)
            bv = ab_ref[1, :, sl].astype(F32)
            sg = jax.nn.sigmoid(av)
            da = (dact * bv * (sg * (1.0 + av * (1.0 - sg)))).astype(BF16)
            db = (dact * (av * sg)).astype(BF16)
            dab_ref[0, :, sl] = da
            dab_ref[1, :, sl] = db
            du = du + _dot(da, win_v[0, sl, :]) + _dot(db, win_v[1, sl, :])
        r, xn, y, _ = _norm_mod(x, g, sc, sh)
        dx, dsh, dsc, dg = _norm_mod_bwd(du, r, xn, y, g, sc)
        dhp_ref[...] = dh_v + dx
        red_ref[0:1, :] += dgt
        red_ref[1:2, :] += dsh
        red_ref[2:3, :] += dsc
        red_ref[3:4, :] += dg

    ab_spec = pl.BlockSpec((2, TS, Fd), lambda i: (0, i, 0))
    return pl.pallas_call(
        body, name=name, grid=(S // TS,),
        in_specs=[_row_tile(TS, D), _row_tile(TS, D), _row_tile(TS, D), ab_spec, _const((8, D)), ANY, ANY],
        out_specs=[_row_tile(TS, D), ab_spec, _row_tile(TS, D), _const((8, D))],
        out_shape=[jax.ShapeDtypeStruct((S, D), F32), jax.ShapeDtypeStruct((2, S, Fd), BF16),
                   jax.ShapeDtypeStruct((S, D), BF16), jax.ShapeDtypeStruct((8, D), F32)],
        scratch_shapes=[pltpu.VMEM(win.shape, BF16), pltpu.VMEM(wout.shape, BF16), pltpu.SemaphoreType.DMA((2,))],
        compiler_params=_params(),
    )(dh, h, f, ab, vec, win, wout)


def _wgrad(x, y, name, tm=None, ts=2048, rider=None):
    xb = x.ndim == 3
    nb = x.shape[0] if xb else 0
    S, M = x.shape[-2:]
    N = y.shape[-1]
    tm = tm or M
    ts = min(ts, S)
    nk = S // ts
    grid = (max(nb, 1), M // tm, nk)

    def body(*refs):
        if rider is None:
            return compute(*refs)
        host, mine = rider.split(refs, 2, 1)
        step = (pl.program_id(0) * grid[1] + pl.program_id(1)) * grid[2] + pl.program_id(2)
        rider.head(mine, step)
        compute(*host)
        rider.tail(mine, step, grid[0] * grid[1] * grid[2])

    def compute(x_ref, y_ref, o_ref, acc):
        k = pl.program_id(2)

        @pl.when(k == 0)
        def _():
            acc[...] = jnp.zeros_like(acc)

        acc[...] += _dot_tn(x_ref[...], y_ref[...])

        @pl.when(k == nk - 1)
        def _():
            o_ref[...] = acc[...].astype(BF16)

    x_spec = (pl.BlockSpec((None, ts, tm), lambda b, i, k: (b, k, i)) if xb
              else pl.BlockSpec((ts, tm), lambda b, i, k: (k, i)))
    y_spec = pl.BlockSpec((ts, N), lambda b, i, k: (k, 0))
    if xb:
        o_spec, o_shape = pl.BlockSpec((None, tm, N), lambda b, i, k: (b, i, 0)), (nb, M, N)
    else:
        o_spec, o_shape = pl.BlockSpec((tm, N), lambda b, i, k: (i, 0)), (M, N)
    specs = ([x_spec, y_spec], [o_spec], [jax.ShapeDtypeStruct(o_shape, BF16)], [pltpu.VMEM((tm, N), F32)])
    in_specs, out_specs, out_shape, scratch = specs if rider is None else rider.specs(*specs)
    outs = pl.pallas_call(
        body, name=name, grid=grid, in_specs=in_specs, out_specs=out_specs, out_shape=out_shape,
        scratch_shapes=scratch, compiler_params=_params(),
    )(x, y, *(rider.arrays if rider else []))
    return outs[0] if rider is None else (outs[0], outs[1:])


P_OFF, Q_OFF, K_OFF, V_OFF, G_OFF = 0, 256, 1024, 1792, 2560
IN_WIDTH = 4608


def _first_half_mask(ts):
    lane = lax.broadcasted_iota(jnp.int32, (ts, 128), 1)
    return (lane % HEAD_DIM) < (HEAD_DIM // 2)


def _rope(t, cos, sin_signed, first, sign):
    partner = jnp.where(first, pltpu.roll(t, 96, 1), pltpu.roll(t, 32, 1))
    return t * cos + sign * (partner * sin_signed)


def _res_spec(r):
    return pl.BlockSpec((r, TS // r, GW), lambda i: (0, i, 0))


def _res_shape(S, r, dtype):
    return jax.ShapeDtypeStruct((r, S // r, GW), dtype)


def _to_residues(piece, out_ref, lanes, r, scr):
    if r == 1:
        out_ref[0, :, lanes] = piece.astype(out_ref.dtype)
        return
    for h in range(piece.shape[1] // 128):
        scr[h] = piece[:, h * 128:(h + 1) * 128]
        at = slice(lanes.start + h * 128, lanes.start + (h + 1) * 128)
        for res in range(r):
            out_ref[res, :, at] = scr[h, pl.ds(res, TS // r, stride=r), :].astype(out_ref.dtype)


def _from_residues(in_ref, lanes, r, scr):
    if r == 1:
        return in_ref[0, :, lanes].astype(F32)
    halves = (lanes.stop - lanes.start) // 128
    for h in range(halves):
        at = slice(lanes.start + h * 128, lanes.start + (h + 1) * 128)
        for res in range(r):
            scr[h, pl.ds(res, TS // r, stride=r), :] = in_ref[res, :, at].astype(F32)
    return scr[0] if halves == 1 else jnp.concatenate([scr[0], scr[1]], axis=1)


RES_SCRATCH = (2, TS, 128)


def _mix_in_fwd(h, vec, cos, sin, win, name):
    S, D = h.shape

    def body(h_ref, vec_ref, cos_ref, sin_ref, win_hbm, u_ref, p_ref, gates_ref, *rest):
        qkv_refs, (win_v, sem, scr) = rest[:9], rest[9:]
        _load_weights([(win_hbm, win_v)], sem)
        g, sh, sc = (vec_ref[k:k + 1, :] for k in range(3))
        _, _, _, u = _norm_mod(h_ref[...], g, sc, sh)
        ub = u.astype(BF16)
        u_ref[...] = ub
        p_ref[...] = _dot_nt(ub, win_v[P_OFF:Q_OFF, :])
        cosv, sinv = cos_ref[...], sin_ref[...]
        first = _first_half_mask(TS)
        for which, off in enumerate((Q_OFF, K_OFF, V_OFF)):
            t = _dot_nt(ub, win_v[off:off + 3 * GW, :])
            for gi in range(3):
                for half in range(2):
                    c0 = gi * GW + half * 128
                    piece = t[:, c0:c0 + 128]
                    if which < 2:
                        piece = _rope(piece, cosv, sinv, first, 1.0)
                    _to_residues(piece, qkv_refs[which * 3 + gi], slice(half * 128, (half + 1) * 128),
                                 DILATIONS[gi], scr)
        gates_ref[...] = jax.nn.sigmoid(_dot_nt(ub, win_v[G_OFF:IN_WIDTH, :]))

    return pl.pallas_call(
        body, name=name, grid=(S // TS,),
        in_specs=[_row_tile(TS, D), _const((8, D)), _row_tile(TS, 128), _row_tile(TS, 128), ANY],
        out_specs=[_row_tile(TS, D), _row_tile(TS, GW), _row_tile(TS, 2 * D)] + [_res_spec(r) for r in DILATIONS] * 3,
        out_shape=[jax.ShapeDtypeStruct((S, D), BF16), jax.ShapeDtypeStruct((S, GW), F32),
                   jax.ShapeDtypeStruct((S, 2 * D), F32)] + [_res_shape(S, r, BF16) for r in DILATIONS] * 3,
        scratch_shapes=[pltpu.VMEM((IN_WIDTH, D), BF16), pltpu.SemaphoreType.DMA((1,)), pltpu.VMEM(RES_SCRATCH, F32)],
        compiler_params=_params(),
    )(h, vec, cos, sin, win)


def _mix_in_bwd(dh, h, vec, cos, sin, dp, dqkv, dgl, win, name):
    S, D = h.shape

    def body(dh_ref, h_ref, vec_ref, cos_ref, sin_ref, dp_ref, *rest):
        dqkv_refs = rest[:9]
        dgl_ref, win_hbm, dhp_ref, dproj_ref, red_ref, win_v, sem, scr = rest[9:]
        _load_weights([(win_hbm, win_v)], sem)

        @pl.when(pl.program_id(0) == 0)
        def _():
            red_ref[...] = jnp.zeros_like(red_ref)

        cosv, sinv = cos_ref[...], sin_ref[...]
        first = _first_half_mask(TS)
        dproj_ref[:, P_OFF:Q_OFF] = dp_ref[...].astype(BF16)
        for which, off in enumerate((Q_OFF, K_OFF, V_OFF)):
            for gi in range(3):
                for half in range(2):
                    piece = _from_residues(dqkv_refs[which * 3 + gi], slice(half * 128, (half + 1) * 128),
                                           DILATIONS[gi], scr)
                    if which < 2:
                        piece = _rope(piece, cosv, sinv, first, -1.0)
                    c0 = off + gi * GW + half * 128
                    dproj_ref[:, c0:c0 + 128] = piece.astype(BF16)
        dproj_ref[:, G_OFF:IN_WIDTH] = dgl_ref[...]
        du = _dot(dproj_ref[...], win_v[...])
        g, sh, sc = (vec_ref[k:k + 1, :] for k in range(3))
        r, xn, y, _ = _norm_mod(h_ref[...], g, sc, sh)
        dx, dsh, dsc, dg = _norm_mod_bwd(du, r, xn, y, g, sc)
        dhp_ref[...] = dh_ref[...] + dx
        red_ref[1:2, :] += dsh
        red_ref[2:3, :] += dsc
        red_ref[3:4, :] += dg

    return pl.pallas_call(
        body, name=name, grid=(S // TS,),
        in_specs=[_row_tile(TS, D), _row_tile(TS, D), _const((8, D)), _row_tile(TS, 128), _row_tile(TS, 128),
                  _row_tile(TS, GW)] + [_res_spec(r) for r in DILATIONS] * 3 + [_row_tile(TS, 2 * D), ANY],
        out_specs=[_row_tile(TS, D), _row_tile(TS, IN_WIDTH), _const((8, D))],
        out_shape=[jax.ShapeDtypeStruct((S, D), F32), jax.ShapeDtypeStruct((S, IN_WIDTH), BF16),
                   jax.ShapeDtypeStruct((8, D), F32)],
        scratch_shapes=[pltpu.VMEM((IN_WIDTH, D), BF16), pltpu.SemaphoreType.DMA((1,)), pltpu.VMEM(RES_SCRATCH, F32)],
        compiler_params=_params(),
    )(dh, h, vec, cos, sin, dp, *dqkv, dgl, win)


def _pool_lanes(rows):
    lane = lax.broadcasted_iota(jnp.int32, (rows, GW), 1)
    return lane // HEAD_DIM


def _pool_window(rows):
    grp = _pool_lanes(rows)
    w = jnp.full((rows, GW), POOL_WINDOWS[0], jnp.int32)
    for k in range(1, len(POOL_WINDOWS)):
        w = jnp.where(grp == k, POOL_WINDOWS[k], w)
    return grp, w


def _pool_fwd(p, wbd, scale, name, ts=512):
    S = p.shape[0]
    ext = ts + HALO

    def body(pc_ref, ph_ref, wbd_ref, sc_ref, d_ref, y_ref):
        i = pl.program_id(0)
        cur = pc_ref[...]
        halo = jnp.where(i > 0, ph_ref[...], 0.0)
        s = jnp.concatenate([halo, cur], axis=0)
        grp, w = _pool_window(ext)
        sel = jnp.zeros((ext, GW), F32)
        for k, wk in enumerate(POOL_WINDOWS):
            s = s + pltpu.roll(s, wk // 2, 0)
            sel = jnp.where(grp == k, s, sel)
        t = i * ts + lax.broadcasted_iota(jnp.int32, (ts, GW), 0)
        count = jnp.minimum(t + 1, w[HALO:]).astype(F32)
        d = (sel[HALO:] / count - cur).astype(BF16)
        d_ref[...] = d
        y_ref[...] = (_dot(d, wbd_ref[...]) * sc_ref[...]).astype(BF16)

    return pl.pallas_call(
        body, name=name, grid=(S // ts,),
        in_specs=[_row_tile(ts, GW),
                  pl.BlockSpec((HALO, GW), lambda i: (jnp.maximum(i * (ts // HALO) - 1, 0), 0)),
                  _const((GW, GW)), _const((1, GW))],
        out_specs=[_row_tile(ts, GW), _row_tile(ts, GW)],
        out_shape=[jax.ShapeDtypeStruct((S, GW), BF16), jax.ShapeDtypeStruct((S, GW), BF16)],
        compiler_params=_params(),
    )(p, p, wbd, scale)


def _pool_bwd(dy, d, wbd, scale, name, ts=512):
    S = dy.shape[0]
    ext = ts + HALO
    nsteps = S // ts
    last_halo = S // HALO - 1

    def body(dyc_ref, dyh_ref, d_ref, wbd_ref, sc_ref, dp_ref, dw_ref, ds_ref):
        i = pl.program_id(0)

        @pl.when(i == 0)
        def _():
            dw_ref[...] = jnp.zeros_like(dw_ref)
            ds_ref[...] = jnp.zeros_like(ds_ref)

        dyc = dyc_ref[...]
        dyh = jnp.where(i < nsteps - 1, dyh_ref[...], 0.0)
        dys = (jnp.concatenate([dyc, dyh], axis=0) * sc_ref[...]).astype(BF16)
        dd = _dot_nt(dys, wbd_ref[...])
        grp, w = _pool_window(ext)
        t = i * ts + lax.broadcasted_iota(jnp.int32, (ext, GW), 0)
        s = dd / jnp.minimum(t + 1, w).astype(F32)
        sel = jnp.zeros((ext, GW), F32)
        for k, wk in enumerate(POOL_WINDOWS):
            s = s + pltpu.roll(s, ext - wk // 2, 0)
            sel = jnp.where(grp == k, s, sel)
        dp_ref[...] = sel[:ts] - dd[:ts]
        dv = d_ref[...]
        z = _dot(dv, wbd_ref[...])
        ds_ref[0:1, :] += jnp.sum(dyc * z, axis=0, keepdims=True)
        dw_ref[...] += _dot_tn(dv, dys[:ts])

    return pl.pallas_call(
        body, name=name, grid=(nsteps,),
        in_specs=[_row_tile(ts, GW),
                  pl.BlockSpec((HALO, GW), lambda i: (jnp.minimum((i + 1) * (ts // HALO), last_halo), 0)),
                  _row_tile(ts, GW), _const((GW, GW)), _const((1, GW))],
        out_specs=[_row_tile(ts, GW), _const((GW, GW)), _const((8, GW))],
        out_shape=[jax.ShapeDtypeStruct((S, GW), F32), jax.ShapeDtypeStruct((GW, GW), F32),
                   jax.ShapeDtypeStruct((8, GW), F32)],
        compiler_params=_params(),
    )(dy, dy, d, wbd, scale)


def _head_id(rows):
    return lax.broadcasted_iota(jnp.int32, (rows, GW), 1) // HEAD_DIM


def _stack_heads(t, hid):
    return jnp.concatenate([jnp.where(hid == h, t, jnp.zeros_like(t)) for h in range(HEADS)], axis=0)


def _unstack_heads(t_all, hid):
    out = jnp.zeros((QB, GW), F32)
    for h in range(HEADS):
        out = jnp.where(hid == h, t_all[h * QB:(h + 1) * QB], out)
    return out


def _band_mask(n):
    row = lax.broadcasted_iota(jnp.int32, (HEADS * QB, 2 * QB), 0) % QB
    col = lax.broadcasted_iota(jnp.int32, (HEADS * QB, 2 * QB), 1)
    rel = row + QB - col
    return (rel >= 0) & (rel <= BAND) & ((col >= QB) | (n > 0))


def _attn_fwd(q, k, v, name):
    r, L, _ = q.shape
    nb = L // QB
    cur = pl.BlockSpec((None, QB, GW), lambda res, n: (res, n, 0))
    prev = pl.BlockSpec((None, QB, GW), lambda res, n: (res, jnp.maximum(n - 1, 0), 0))

    def body(q_ref, kp_ref, kc_ref, vp_ref, vc_ref, o_ref, lse_ref):
        n = pl.program_id(1)
        hid = _head_id(QB)
        qs = _stack_heads(q_ref[...], hid)
        kc = jnp.concatenate([kp_ref[...], kc_ref[...]], axis=0)
        vc = jnp.concatenate([vp_ref[...], vc_ref[...]], axis=0)
        s = _dot_nt(qs, kc) * (HEAD_DIM ** -0.5)
        s = jnp.where(_band_mask(n), s, -jnp.inf)
        m = jnp.max(s, axis=-1, keepdims=True)
        e = jnp.exp(s - m)
        den = jnp.sum(e, axis=-1, keepdims=True)
        lse = m + jnp.log(den)
        pr = (e * (1.0 / den)).astype(BF16)
        o_ref[...] = _unstack_heads(_dot(pr, vc), hid)
        lse_ref[...] = _unstack_heads(jnp.broadcast_to(lse, (HEADS * QB, GW)), hid)

    return pl.pallas_call(
        body, name=name, grid=(r, nb),
        in_specs=[cur, prev, cur, prev, cur], out_specs=[cur, cur],
        out_shape=[jax.ShapeDtypeStruct(q.shape, F32), jax.ShapeDtypeStruct(q.shape, F32)],
        compiler_params=_params(),
    )(q, k, k, v, v)


def _head_rows(t_full, hid):
    return jnp.concatenate(
        [jnp.max(jnp.where(hid == h, t_full, -jnp.inf), axis=-1, keepdims=True) for h in range(HEADS)], axis=0)


def _attn_bwd(q, k, v, do, lse, cterm, name):
    r, L, _ = q.shape
    nb = L // QB
    qside = pl.BlockSpec((None, QB, GW), lambda res, n: (res, jnp.minimum(n, nb - 1), 0))
    kcur = qside
    kprev = pl.BlockSpec((None, QB, GW), lambda res, n: (res, jnp.clip(n - 1, 0, nb - 1), 0))
    kout = pl.BlockSpec((None, QB, GW), lambda res, n: (res, jnp.maximum(n - 1, 0), 0))

    def body(q_ref, do_ref, lse_ref, c_ref, kp_ref, kc_ref, vp_ref, vc_ref,
             dq_ref, dk_ref, dv_ref, carry_k, carry_v):
        n = pl.program_id(1)

        @pl.when(n == 0)
        def _():
            carry_k[...] = jnp.zeros_like(carry_k)
            carry_v[...] = jnp.zeros_like(carry_v)

        @pl.when(n < nb)
        def _():
            hid = _head_id(QB)
            qs = _stack_heads(q_ref[...], hid)
            dos = _stack_heads(do_ref[...], hid)
            kc = jnp.concatenate([kp_ref[...], kc_ref[...]], axis=0)
            vc = jnp.concatenate([vp_ref[...], vc_ref[...]], axis=0)
            s = _dot_nt(qs, kc) * (HEAD_DIM ** -0.5)
            s = jnp.where(_band_mask(n), s, -jnp.inf)
            p = jnp.exp(s - _head_rows(lse_ref[...], hid))
            dp = _dot_nt(dos, vc)
            ds = (p * (dp + _head_rows(c_ref[...], hid)) * (HEAD_DIM ** -0.5)).astype(BF16)
            dq_ref[...] = _unstack_heads(_dot(ds, kc), hid)
            dkc = _dot_tn(ds, qs)
            dvc = _dot_tn(p.astype(BF16), dos)
            dk_ref[...] = carry_k[...] + dkc[:QB]
            dv_ref[...] = carry_v[...] + dvc[:QB]
            carry_k[...] = dkc[QB:]
            carry_v[...] = dvc[QB:]

        @pl.when(n == nb)
        def _():
            dk_ref[...] = carry_k[...]
            dv_ref[...] = carry_v[...]

    out = jax.ShapeDtypeStruct(q.shape, F32)
    return pl.pallas_call(
        body, name=name, grid=(r, nb + 1),
        in_specs=[qside, qside, qside, qside, kprev, kcur, kprev, kcur],
        out_specs=[qside, kout, kout], out_shape=[out, out, out],
        scratch_shapes=[pltpu.VMEM((QB, GW), F32), pltpu.VMEM((QB, GW), F32)],
        compiler_params=_params(),
    )(q, do, lse, cterm, k, k, v, v)


def _token_order(refs, scr):
    return [_from_residues(ref, slice(0, GW), r, scr) for ref, r in zip(refs, DILATIONS)]


def _group_weights(lses):
    l0, l1, l2 = lses
    m = jnp.maximum(jnp.maximum(l0, l1), l2)
    e = [jnp.exp(l - m) for l in (l0, l1, l2)]
    den = e[0] + e[1] + e[2]
    return [ei / den for ei in e]


def _mix_out_fwd(h, vec, gates, ypool, o3, lse3, wpb, wab, wout, name):
    S, D = h.shape

    def body(h_ref, vec_ref, gates_ref, yp_ref, o0, o1, o2, l0, l1, l2, wpb_hbm, wab_hbm, wout_hbm,
             hn_ref, ya_ref, merged_ref, tm_ref, wpb_v, wab_v, wout_v, sem, scr):
        _load_weights([(wpb_hbm, wpb_v), (wab_hbm, wab_v), (wout_hbm, wout_v)], sem)
        gt = vec_ref[3:4, :]
        wts = _group_weights(_token_order((l0, l1, l2), scr))
        og = _token_order((o0, o1, o2), scr)
        ya = (wts[0] * og[0] + wts[1] * og[1] + wts[2] * og[2]).astype(BF16)
        ya_ref[...] = ya
        merged = (gates_ref[:, :D] * _dot(yp_ref[...], wpb_v[...])
                  + gates_ref[:, D:] * _dot(ya, wab_v[...])).astype(BF16)
        merged_ref[...] = merged
        tm = _dot(merged, wout_v[...])
        tm_ref[...] = tm
        hn_ref[...] = h_ref[...] + gt * tm

    grp = _row_tile(TS, GW)
    res = [_res_spec(r) for r in DILATIONS]
    return pl.pallas_call(
        body, name=name, grid=(S // TS,),
        in_specs=[_row_tile(TS, D), _const((8, D)), _row_tile(TS, 2 * D), grp] + res * 2 + [ANY, ANY, ANY],
        out_specs=[_row_tile(TS, D), grp, _row_tile(TS, D), _row_tile(TS, D)],
        out_shape=[jax.ShapeDtypeStruct((S, D), F32), jax.ShapeDtypeStruct((S, GW), BF16),
                   jax.ShapeDtypeStruct((S, D), BF16), jax.ShapeDtypeStruct((S, D), F32)],
        scratch_shapes=[pltpu.VMEM((GW, D), BF16), pltpu.VMEM((GW, D), BF16), pltpu.VMEM((D, D), BF16),
                        pltpu.SemaphoreType.DMA((3,)), pltpu.VMEM(RES_SCRATCH, F32)],
        compiler_params=_params(),
    )(h, vec, gates, ypool, *o3, *lse3, wpb, wab, wout)


def _mix_out_bwd(dh, tm, vec, gates, ypool, o3, lse3, wpb, wab, wout, name, rider=None):
    S, D = dh.shape

    def body(*refs):
        if rider is None:
            return compute(*refs)
        host, mine = rider.split(refs, 14, 12)
        rider.head(mine, pl.program_id(0))
        compute(*host)
        rider.tail(mine, pl.program_id(0), S // TS)

    def compute(dh_ref, tm_ref, vec_ref, gates_ref, yp_ref, o0, o1, o2, l0, l1, l2, wpb_hbm, wab_hbm, wout_hbm,
                dtm_ref, dgl_ref, dypb_ref, dyab_ref, dyp_ref, do0, do1, do2, c0, c1, c2, red_ref,
                wpb_v, wab_v, wout_v, sem, scr):
        _load_weights([(wpb_hbm, wpb_v), (wab_hbm, wab_v), (wout_hbm, wout_v)], sem)

        @pl.when(pl.program_id(0) == 0)
        def _():
            red_ref[...] = jnp.zeros_like(red_ref)

        gt = vec_ref[3:4, :]
        dh_v = dh_ref[...]
        red_ref[0:1, :] += jnp.sum(tm_ref[...] * dh_v, axis=0, keepdims=True)
        dtm = (gt * dh_v).astype(BF16)
        dtm_ref[...] = dtm
        dm = _dot_nt(dtm, wout_v[...])
        wts = _group_weights(_token_order((l0, l1, l2), scr))
        og = _token_order((o0, o1, o2), scr)
        ya = wts[0] * og[0] + wts[1] * og[1] + wts[2] * og[2]
        ypb = _dot(yp_ref[...], wpb_v[...])
        yab = _dot(ya.astype(BF16), wab_v[...])
        gp = gates_ref[:, :D]
        ga = gates_ref[:, D:]
        dgl_ref[:, :D] = (dm * ypb * gp * (1.0 - gp)).astype(BF16)
        dgl_ref[:, D:] = (dm * yab * ga * (1.0 - ga)).astype(BF16)
        dypb = (dm * gp).astype(BF16)
        dyab = (dm * ga).astype(BF16)
        dypb_ref[...] = dypb
        dyab_ref[...] = dyab
        dyp_ref[...] = _dot_nt(dypb, wpb_v[...])
        dya = _dot_nt(dyab, wab_v[...])
        row = lax.broadcasted_iota(jnp.int32, (GW, GW), 0) // HEAD_DIM
        col = lax.broadcasted_iota(jnp.int32, (GW, GW), 1) // HEAD_DIM
        ones = jnp.where(row == col, 1.0, 0.0).astype(F32)
        tot = jnp.dot(dya * ya, ones, preferred_element_type=F32, precision=lax.Precision.HIGHEST)
        for wg, do_ref, c_ref, r in zip(wts, (do0, do1, do2), (c0, c1, c2), DILATIONS):
            _to_residues(wg * dya, do_ref, slice(0, GW), r, scr)
            _to_residues(-(wg * tot), c_ref, slice(0, GW), r, scr)

    grp = _row_tile(TS, GW)
    res = [_res_spec(r) for r in DILATIONS]
    specs = (
        [_row_tile(TS, D), _row_tile(TS, D), _const((8, D)), _row_tile(TS, 2 * D), grp] + res * 2
        + [ANY, ANY, ANY],
        [_row_tile(TS, D), _row_tile(TS, 2 * D), _row_tile(TS, D), _row_tile(TS, D), grp]
        + res * 2 + [_const((8, D))],
        [jax.ShapeDtypeStruct((S, D), BF16), jax.ShapeDtypeStruct((S, 2 * D), BF16),
         jax.ShapeDtypeStruct((S, D), BF16), jax.ShapeDtypeStruct((S, D), BF16), jax.ShapeDtypeStruct((S, GW), F32)]
        + [_res_shape(S, r, BF16) for r in DILATIONS] + [_res_shape(S, r, F32) for r in DILATIONS]
        + [jax.ShapeDtypeStruct((8, D), F32)],
        [pltpu.VMEM((GW, D), BF16), pltpu.VMEM((GW, D), BF16), pltpu.VMEM((D, D), BF16),
         pltpu.SemaphoreType.DMA((3,)), pltpu.VMEM(RES_SCRATCH, F32)])
    in_specs, out_specs, out_shape, scratch = specs if rider is None else rider.specs(*specs)
    outs = pl.pallas_call(
        body, name=name, grid=(S // TS,), in_specs=in_specs, out_specs=out_specs, out_shape=out_shape,
        scratch_shapes=scratch, compiler_params=_params(),
    )(dh, tm, vec, gates, ypool, *o3, *lse3, wpb, wab, wout, *(rider.arrays if rider else []))
    return outs if rider is None else (outs[:12], outs[12:])


def _final_loss(h, target, gf, name):
    S, D = h.shape

    def body(h_ref, t_ref, g_ref, dh_ref, loss_ref, dg_ref):
        @pl.when(pl.program_id(0) == 0)
        def _():
            loss_ref[...] = jnp.zeros_like(loss_ref)
            dg_ref[...] = jnp.zeros_like(dg_ref)

        x = h_ref[...]
        g = g_ref[0:1, :]
        r = lax.rsqrt(jnp.mean(x * x, axis=-1, keepdims=True) + EPS)
        xn = x * r
        err = xn * g - t_ref[...]
        loss_ref[...] += 0.5 * jnp.sum(jnp.mean(err * err, axis=-1, keepdims=True))
        dy = err * (1.0 / D)
        dg_ref[0:1, :] += jnp.sum(dy * xn, axis=0, keepdims=True)
        dxn = dy * g
        dh_ref[...] = r * (dxn - xn * jnp.mean(dxn * xn, axis=-1, keepdims=True))

    return pl.pallas_call(
        body, name=name, grid=(S // TS,),
        in_specs=[_row_tile(TS, D), _row_tile(TS, D), _const((8, D))],
        out_specs=[_row_tile(TS, D), _const((8, 128)), _const((8, D))],
        out_shape=[jax.ShapeDtypeStruct((S, D), F32), jax.ShapeDtypeStruct((8, 128), F32),
                   jax.ShapeDtypeStruct((8, D), F32)],
        compiler_params=_params(),
    )(h, target, gf)


def _ada_mod(c_all, w, b, name):
    def body(c_ref, w_ref, b_ref, cond_ref, mod_ref):
        cv = c_ref[...]
        cond = cv * jax.nn.sigmoid(cv)
        cond_ref[...] = cond
        mod_ref[...] = jnp.dot(cond, w_ref[...], preferred_element_type=F32,
                               precision=lax.Precision.HIGHEST) + b_ref[...]

    return pl.pallas_call(
        body, name=name,
        out_shape=[jax.ShapeDtypeStruct(c_all.shape, F32), jax.ShapeDtypeStruct((c_all.shape[0], w.shape[1]), F32)],
        compiler_params=_params(),
    )(c_all, w, b)


def _adamw_math(w, g, m, v):
    m = ADAM_B1 * m + (1.0 - ADAM_B1) * g
    v = ADAM_B2 * v + (1.0 - ADAM_B2) * (g * g)
    m_hat = m / (1.0 - ADAM_B1 ** ADAM_STEP)
    v_hat = v / (1.0 - ADAM_B2 ** ADAM_STEP)
    delta = -ADAM_LR * (m_hat / (jnp.sqrt(v_hat) + ADAM_EPS) + ADAM_WD * w)
    return delta, m, v


def _adamw(w, g, m, v, name):
    R, C = w.shape
    tr = R
    for cand in (256, 128, 64, 32, 16, 8):
        if R % cand == 0:
            tr = cand
            break

    def body(w_ref, g_ref, m_ref, v_ref, d_ref, mo_ref, vo_ref):
        d_ref[...], mo_ref[...], vo_ref[...] = _adamw_math(w_ref[...], g_ref[...], m_ref[...], v_ref[...])

    spec = _row_tile(tr, C)
    out = jax.ShapeDtypeStruct((R, C), F32)
    return pl.pallas_call(
        body, name=name, grid=(R // tr,), in_specs=[spec] * 4, out_specs=[spec] * 3, out_shape=[out] * 3,
        compiler_params=_params(),
    )(w, g, m, v)


def _ada_grad_adamw(cond_t, dmod, w, m, v, name, tr=256):
    R, C = w.shape
    nb = dmod.shape[0]

    def body(ct_ref, dm_ref, w_ref, m_ref, v_ref, g_ref, d_ref, mo_ref, vo_ref):
        ct = ct_ref[...]
        dm = dm_ref[...]
        g = jnp.zeros((tr, C), F32)
        for bi in range(nb):
            g = g + ct[:, bi:bi + 1] * dm[bi:bi + 1, :]
        g_ref[...] = g
        d_ref[...], mo_ref[...], vo_ref[...] = _adamw_math(w_ref[...], g, m_ref[...], v_ref[...])

    spec = _row_tile(tr, C)
    out = jax.ShapeDtypeStruct((R, C), F32)
    return pl.pallas_call(
        body, name=name, grid=(R // tr,),
        in_specs=[_row_tile(tr, nb), _const((nb, C)), spec, spec, spec],
        out_specs=[spec] * 4, out_shape=[out] * 4,
        compiler_params=_params(),
    )(cond_t, dmod, w, m, v)


def _row_step(rows, cap=256):
    for cand in range(cap, 15, -16):
        if rows % cand == 0:
            return cand
    return rows


def _slot_sum(x_ref):
    acc = x_ref[0].astype(F32)
    for k in range(1, x_ref.shape[0]):
        acc = acc + x_ref[k].astype(F32)
    return acc


def _sum_slots(x, name, out_dtype=F32):
    n, R, C = x.shape
    tr = _row_step(R)

    def body(x_ref, o_ref):
        o_ref[...] = _slot_sum(x_ref).astype(out_dtype)

    return pl.pallas_call(
        body, name=name, grid=(R // tr,),
        in_specs=[pl.BlockSpec((n, tr, C), lambda i: (0, i, 0))],
        out_specs=_row_tile(tr, C), out_shape=jax.ShapeDtypeStruct((R, C), out_dtype),
        compiler_params=_params(),
    )(x)


def _sum_pair(core, g, recv, name):
    _, _, R, C = g.shape
    tr = _row_step(R)

    def body(core_ref, g_ref, r_ref, o_ref):
        o_ref[...] = (g_ref[...].astype(F32) + r_ref[...].astype(F32)).astype(BF16)

    return pl.pallas_call(
        body, name=name, out_shape=jax.ShapeDtypeStruct((4, R, C), BF16),
        grid_spec=pltpu.PrefetchScalarGridSpec(
            num_scalar_prefetch=1, grid=(4, R // tr),
            in_specs=[pl.BlockSpec((None, None, tr, C), lambda k, i, core_ref: (k, core_ref[0], i, 0)),
                      pl.BlockSpec((None, tr, C), lambda k, i, core_ref: (k, i, 0))],
            out_specs=pl.BlockSpec((None, tr, C), lambda k, i, core_ref: (k, i, 0))),
        compiler_params=_params(),
    )(core, g, recv)


def _sum_adamw(chip, own, recv, w, m, v, name):
    _, R, C = own.shape
    tr = _row_step(R)

    def body(chip_ref, own_ref, r_ref, w_ref, m_ref, v_ref, g_ref, d_ref, mo_ref, vo_ref):
        g = own_ref[...].astype(F32) + _slot_sum(r_ref)
        g_ref[...] = g
        d_ref[...], mo_ref[...], vo_ref[...] = _adamw_math(w_ref[...], g, m_ref[...], v_ref[...])

    spec = pl.BlockSpec((tr, C), lambda i, chip_ref: (i, 0))
    out = jax.ShapeDtypeStruct((R, C), F32)
    return pl.pallas_call(
        body, name=name, out_shape=[out] * 4,
        grid_spec=pltpu.PrefetchScalarGridSpec(
            num_scalar_prefetch=1, grid=(R // tr,),
            in_specs=[pl.BlockSpec((None, tr, C), lambda i, chip_ref: (chip_ref[0], i, 0)),
                      pl.BlockSpec((3, tr, C), lambda i, chip_ref: (0, i, 0)), spec, spec, spec],
            out_specs=[spec] * 4),
        compiler_params=_params(),
    )(chip, own, recv, w, m, v)


def _place():
    return lax.axis_index("x"), lax.axis_index("y"), lax.axis_index("c")


def _gather_steps(x_refs, out_refs, send_sems, recv_sems):
    n = len(x_refs)
    x, y, c = _place()
    me, sibling = (x, y, c), (x, y, 1 - c)
    chips = [(1 - x, y), (x, 1 - y), (1 - x, 1 - y)]

    def rows(a, px, py, pc):
        return out_refs[a].at[4 * px + 2 * py + pc]

    def copy(a, k, block, to, src=None):
        return pltpu.make_async_remote_copy(
            src_ref=rows(a, *block) if src is None else src, dst_ref=rows(a, *block),
            send_sem=send_sems.at[a, k], recv_sem=recv_sems.at[a, k], device_id=to, device_id_type=MESH)

    def first(a):
        return [copy(a, 0, me, sibling, src=x_refs[a])] + [
            copy(a, 1 + j, me, (*chip, c), src=x_refs[a]) for j, chip in enumerate(chips)]

    def passed(a, j):
        return copy(a, 4 + j, (*chips[j], c), sibling)

    def start():
        for a in range(n):
            for cp in first(a):
                cp.start()

    def relay():
        for j, chip in enumerate(chips):
            for a in range(n):
                copy(a, 1 + j, (*chip, c), me).wait_recv()
                passed(a, j).start()

    def finish():
        for a in range(n):
            copy(a, 0, sibling, me).wait_recv()
            for j, chip in enumerate(chips):
                copy(a, 4 + j, (*chip, 1 - c), me).wait_recv()
        for a in range(n):
            for cp in first(a) + [passed(a, j) for j in range(3)]:
                cp.wait_send()

    return start, relay, finish


def _all_gather(arrs, name, own=True):
    n = len(arrs)

    def body(*refs):
        x_refs, out_refs = refs[:n], refs[n:2 * n]
        send_sems, recv_sems, local_sems = refs[2 * n:]
        me = 4 * lax.axis_index("x") + 2 * lax.axis_index("y") + lax.axis_index("c")
        mine = [pltpu.make_async_copy(x_refs[a], out_refs[a].at[me], local_sems.at[a]) for a in range(n)] if own else []
        for cp in mine:
            cp.start()
        for step in _gather_steps(x_refs, out_refs, send_sems, recv_sems):
            step()
        for cp in mine:
            cp.wait()

    return pl.pallas_call(
        body, name=name, out_shape=[jax.ShapeDtypeStruct((N_DEV,) + t.shape, t.dtype) for t in arrs],
        in_specs=[ANY] * n, out_specs=[ANY] * n,
        scratch_shapes=[pltpu.SemaphoreType.DMA((n, 7)), pltpu.SemaphoreType.DMA((n, 7)),
                        pltpu.SemaphoreType.DMA((n,))],
    )(*arrs)


def _pair_exchange(arrs, name):
    n = len(arrs)

    def body(*refs):
        g_refs, out_refs = refs[:n], refs[n:2 * n]
        send_sems, recv_sems = refs[2 * n:]
        x, y, c = _place()
        give = [pltpu.make_async_remote_copy(
            src_ref=g_refs[a].at[pl.ds(0, 4), 1 - c], dst_ref=out_refs[a], send_sem=send_sems.at[a],
            recv_sem=recv_sems.at[a], device_id=(x, y, 1 - c), device_id_type=MESH) for a in range(n)]
        for cp in give:
            cp.start()
        for cp in give:
            cp.wait()

    return pl.pallas_call(
        body, name=name,
        out_shape=[jax.ShapeDtypeStruct((4,) + t.shape[2:], t.dtype) for t in arrs],
        in_specs=[ANY] * n, out_specs=[ANY] * n,
        scratch_shapes=[pltpu.SemaphoreType.DMA((n,)), pltpu.SemaphoreType.DMA((n,))],
    )(*arrs)


def _chip_exchange_steps(p_refs, out_refs, send_sems, recv_sems):
    x, y, c = _place()
    chips = [(1 - x, y), (x, 1 - y), (1 - x, 1 - y)]

    def copies():
        return [pltpu.make_async_remote_copy(
            src_ref=p_refs[a].at[2 * px + py], dst_ref=out_refs[a].at[j], send_sem=send_sems.at[a, j],
            recv_sem=recv_sems.at[a, j], device_id=(px, py, c), device_id_type=MESH)
            for a in range(len(p_refs)) for j, (px, py) in enumerate(chips)]

    def start():
        for cp in copies():
            cp.start()

    def finish():
        for cp in copies():
            cp.wait()

    return start, finish


def _chip_exchange(arrs, name):
    n = len(arrs)

    def body(*refs):
        for step in _chip_exchange_steps(refs[:n], refs[n:2 * n], *refs[2 * n:]):
            step()

    return pl.pallas_call(
        body, name=name, out_shape=[jax.ShapeDtypeStruct((3,) + t.shape[1:], t.dtype) for t in arrs],
        in_specs=[ANY] * n, out_specs=[ANY] * n,
        scratch_shapes=[pltpu.SemaphoreType.DMA((n, 3)), pltpu.SemaphoreType.DMA((n, 3))],
    )(*arrs)


class _Rider:
    def __init__(self, arrays, out_shape, sems, steps, relay_before_end=None):
        self.arrays, self.out_shape, self.scratch, self.steps = list(arrays), out_shape, sems, steps
        self.n = len(self.arrays)
        self.relay_before_end = relay_before_end

    def specs(self, in_specs, out_specs, out_shape, scratch):
        extra = [ANY] * self.n
        return in_specs + extra, out_specs + extra, out_shape + self.out_shape, scratch + self.scratch

    def split(self, refs, n_in, n_out):
        k = self.n
        a, b = n_in + k, n_in + k + n_out
        return refs[:n_in] + refs[a:b] + refs[b + k:-2], (refs[n_in:a], refs[b:b + k], refs[-2:])

    def head(self, mine, step):
        pl.when(step == 0)(self.steps(mine[0], mine[1], *mine[2])[0])

    def tail(self, mine, step, nsteps):
        steps = self.steps(mine[0], mine[1], *mine[2])
        if self.relay_before_end is not None:
            pl.when(step == nsteps - 1 - self.relay_before_end)(steps[1])
        pl.when(step == nsteps - 1)(steps[-1])


def _gather_rider(arrs, relay_before_end=4):
    n = len(arrs)
    return _Rider(arrs, [jax.ShapeDtypeStruct((N_DEV,) + t.shape, t.dtype) for t in arrs],
                  [pltpu.SemaphoreType.DMA((n, 7)), pltpu.SemaphoreType.DMA((n, 7))], _gather_steps,
                  relay_before_end)


def _chip_exchange_rider(arrs):
    n = len(arrs)
    return _Rider(arrs, [jax.ShapeDtypeStruct((3,) + t.shape[1:], t.dtype) for t in arrs],
                  [pltpu.SemaphoreType.DMA((n, 3)), pltpu.SemaphoreType.DMA((n, 3))], _chip_exchange_steps)


def _rope_tables(positions):
    inv_freq = ROPE_THETA ** (-jnp.arange(0, HEAD_DIM, 2, dtype=F32) / HEAD_DIM)
    ang = positions.astype(F32)[:, None] * inv_freq
    cos, sin = jnp.cos(ang), jnp.sin(ang)
    return jnp.tile(cos, (1, 4)), jnp.tile(jnp.concatenate([-sin, sin], axis=1), (1, 2))


def _vec(g, shift, scale, gate):
    z = jnp.zeros_like(g)
    return jnp.stack([g, shift, scale, gate, z, z, z, z])


class _GradReducer:
    def __init__(self):
        self.core = lax.axis_index("c").astype(jnp.int32).reshape(1)
        self.own, self.others, self.waiting, self.riding = {}, {}, [], []

    def pair(self, named):
        keys = list(named)
        mine = [named[k].reshape((4, 2) + named[k].shape[1:]) for k in keys]
        theirs = _pair_exchange(mine, "reduce_pair_" + keys[0])
        for k, g, r in zip(keys, mine, theirs):
            self.own[k] = _sum_pair(self.core, g, r, "sum_pair_" + k)
        self.waiting += keys

    def rider(self):
        self.riding, self.waiting = self.waiting, []
        return _chip_exchange_rider([self.own[k] for k in self.riding])

    def landed(self, results):
        self.others.update(zip(self.riding, results))

    def flush(self, name):
        keys, self.waiting = self.waiting, []
        self.others.update(zip(keys, _chip_exchange([self.own[k] for k in keys], name)))


def _by_owner(g):
    if g.ndim == 3:
        return g if g.shape[0] == N_DEV else g.reshape(N_DEV, g.shape[1] * g.shape[0] // N_DEV, g.shape[2])
    return g.reshape(N_DEV, g.shape[0] // N_DEV, g.shape[1])


def _local_step(x, target, positions, mod, small, W, late=None, red=None):
    S, D = x.shape
    sh1, sc1, gt1, sh2, sc2, gt2, sh3, sc3, gt3 = (mod[k] for k in range(9))
    v1 = _vec(small["g1"], sh1, sc1, gt1)
    v2 = _vec(small["g2"], sh2, sc2, gt2)
    v3 = _vec(small["g3"], sh3, sc3, gt3)
    vf = _vec(small["gf"], small["gf"], small["gf"], small["gf"])
    cos, sin = _rope_tables(positions)
    wbd = jax.scipy.linalg.block_diag(*[small["w_pool"][k] for k in range(4)]).astype(BF16)
    pscale = small["pool_scale"].reshape(1, GW)

    if late is None:
        h1, u1, ab1, act1, f1 = _ffn_fwd(x, v1, W["w1in"], W["w1out"], "ffn1_fwd")
    else:
        (h1, u1, ab1, act1, f1), landed = _ffn_fwd(x, v1, W["w1in"], W["w1out"], "ffn1_fwd", rider=late[0])
        W = {**W, **late[1](landed)}
    u2, p, gates, *qkv = _mix_in_fwd(h1, v2, cos, sin, W["win"], "mix_in_fwd")
    dpool, ypool = _pool_fwd(p, wbd, pscale, "pool_fwd")
    o3, lse3 = [], []
    for gi in range(len(DILATIONS)):
        o, lse = _attn_fwd(qkv[gi], qkv[3 + gi], qkv[6 + gi], f"attn_fwd_{gi}")
        o3.append(o)
        lse3.append(lse)
    h2, ya, merged, tm = _mix_out_fwd(h1, v2, gates, ypool, o3, lse3, W["wpb"], W["wab"], W["wout"], "mix_out_fwd")
    h3, u3, ab3, act3, f3 = _ffn_fwd(h2, v3, W["w2in"], W["w2out"], "ffn2_fwd")
    dh3, loss_blk, dgf = _final_loss(h3, target, vf, "final_loss")

    dh2, dab3, df3, red3 = _ffn_bwd(dh3, h2, f3, ab3, v3, W["w2in"], W["w2out"], "ffn2_bwd")
    half_f = ab3.shape[2] // 2
    G = {"w2in": _by_owner(_wgrad(dab3, u3, "wgrad_2in", tm=half_f)),
         "w2out": _by_owner(_wgrad(act3, df3, "wgrad_2out", tm=half_f))}
    mix_out_args = (dh2, tm, v2, gates, ypool, o3, lse3, W["wpb"], W["wab"], W["wout"], "mix_out_bwd")
    if red is None:
        mix_out = _mix_out_bwd(*mix_out_args)
    else:
        red.pair({k: G[k] for k in ("w2in", "w2out")})
        mix_out, landed = _mix_out_bwd(*mix_out_args, rider=red.rider())
        red.landed(landed)
    (dtm, dgl, dypb, dyab, dyp, do0, do1, do2, c0, c1, c2, red2o) = mix_out
    dq3, dk3, dv3 = [], [], []
    for gi, (do, ct) in enumerate(zip((do0, do1, do2), (c0, c1, c2))):
        dq, dk, dv = _attn_bwd(qkv[gi], qkv[3 + gi], qkv[6 + gi], do, lse3[gi], ct, f"attn_bwd_{gi}")
        dq3.append(dq)
        dk3.append(dk)
        dv3.append(dv)
    dp, dwbd, dps = _pool_bwd(dyp, dpool, wbd, pscale, "pool_bwd")
    dh1, dproj, red2i = _mix_in_bwd(dh2, h1, v2, cos, sin, dp, dq3 + dk3 + dv3, dgl, W["win"], "mix_in_bwd")
    G["win"] = _by_owner(_wgrad(dproj, u2, "wgrad_in", tm=1152))
    G["wpb"] = _full_to_cols(_wgrad(ypool, dypb, "wgrad_pb"))
    G["wab"] = _full_to_cols(_wgrad(ya, dyab, "wgrad_ab"))
    G["wout"] = _by_owner(_wgrad(merged, dtm, "wgrad_out"))
    if red is not None:
        red.pair({k: G[k] for k in ("win", "wpb", "wab", "wout")})
    dx, dab1, df1, red1 = _ffn_bwd(dh1, x, f1, ab1, v1, W["w1in"], W["w1out"], "ffn1_bwd")
    if red is None:
        G["w1in"] = _by_owner(_wgrad(dab1, u1, "wgrad_1in", tm=half_f))
        G["w1out"] = _by_owner(_wgrad(act1, df1, "wgrad_1out", tm=half_f))
    else:
        g1in, landed = _wgrad(dab1, u1, "wgrad_1in", tm=half_f, rider=red.rider())
        red.landed(landed)
        G["w1in"] = _by_owner(g1in)
        red.pair({"w1in": G["w1in"]})
        g1out, landed = _wgrad(act1, df1, "wgrad_1out", tm=half_f, rider=red.rider())
        red.landed(landed)
        G["w1out"] = _by_owner(g1out)
        red.pair({"w1out": G["w1out"]})
        red.flush("reduce_chips_w1out")
    dmod = jnp.stack([red1[1], red1[2], red1[0], red2i[1], red2i[2], red2o[0], red3[1], red3[2], red3[0]])
    dsmall = {
        "g1": red1[3], "g2": red2i[3], "g3": red3[3], "gf": dgf[0],
        "w_pool": jnp.stack([dwbd[k * 64:(k + 1) * 64, k * 64:(k + 1) * 64] for k in range(4)]),
        "pool_scale": dps[0],
    }
    return loss_blk[0, 0], dx, G, dmod, dsmall


SHARDED = ("w_ffn1_in", "w_ffn1_out", "w_in", "w_pool_branch", "w_attn_branch", "w_out", "w_ffn2_in", "w_ffn2_out")
TRANSPOSED = ("w_ffn1_in", "w_in", "w_ffn2_in")
FIRST = ("w_ffn1_in", "w_ffn1_out")
LATER = tuple(n for n in SHARDED if n not in FIRST)
GRAD_KEY = dict(w_ffn1_in="w1in", w_ffn1_out="w1out", w_in="win", w_pool_branch="wpb", w_attn_branch="wab",
                w_out="wout", w_ffn2_in="w2in", w_ffn2_out="w2out")


def _cols_to_full(g):
    return jnp.concatenate([g[j] for j in range(N_DEV)], axis=1)


def _full_to_cols(t):
    c = t.shape[1] // N_DEV
    return jnp.stack([t[:, j * c:(j + 1) * c] for j in range(N_DEV)])


SMALL =(("b_ada", 9216), ("g_norm_ffn1", 1024), ("g_norm_mix", 1024), ("g_norm_ffn2", 1024), ("g_final", 1024),
         ("w_pool", 16384), ("pool_scale", 256))
SMALL_ROWS = 240


def _pack_small(vals, loss=None):
    flat = jnp.concatenate([vals[name].reshape(-1) for name, _ in SMALL])
    tail = jnp.zeros((SMALL_ROWS * 128 - flat.shape[0],), F32)
    if loss is not None:
        tail = tail.at[0].set(loss)
    return jnp.concatenate([flat, tail]).reshape(SMALL_ROWS, 128)


def _unpack_small(slab, shapes):
    flat, out, off = slab.reshape(-1), {}, 0
    for name, n in SMALL:
        out[name] = flat[off:off + n].reshape(shapes[name])
        off += n
    return out, flat[off]


def kernel(x, c, positions, w_ada, b_ada, g_norm_ffn1, w_ffn1_in, w_ffn1_out, g_norm_mix, w_in, w_pool, pool_scale, w_pool_branch, w_attn_branch, w_out, g_norm_ffn2, w_ffn2_in, w_ffn2_out, g_final, loss_target, m_w_ada, m_b_ada, m_g_norm_ffn1, m_w_ffn1_in, m_w_ffn1_out, m_g_norm_mix, m_w_in, m_w_pool, m_pool_scale, m_w_pool_branch, m_w_attn_branch, m_w_out, m_g_norm_ffn2, m_w_ffn2_in, m_w_ffn2_out, m_g_final, v_w_ada, v_b_ada, v_g_norm_ffn1, v_w_ffn1_in, v_w_ffn1_out, v_g_norm_mix, v_w_in, v_w_pool, v_pool_scale, v_w_pool_branch, v_w_attn_branch, v_w_out, v_g_norm_ffn2, v_w_ffn2_in, v_w_ffn2_out, v_g_final):
    names = ["w_ada", "b_ada", "g_norm_ffn1", "w_ffn1_in", "w_ffn1_out", "g_norm_mix", "w_in", "w_pool", "pool_scale",
             "w_pool_branch", "w_attn_branch", "w_out", "g_norm_ffn2", "w_ffn2_in", "w_ffn2_out", "g_final"]
    w = dict(w_ada=w_ada, b_ada=b_ada, g_norm_ffn1=g_norm_ffn1, w_ffn1_in=w_ffn1_in, w_ffn1_out=w_ffn1_out,
             g_norm_mix=g_norm_mix, w_in=w_in, w_pool=w_pool, pool_scale=pool_scale, w_pool_branch=w_pool_branch,
             w_attn_branch=w_attn_branch, w_out=w_out, g_norm_ffn2=g_norm_ffn2, w_ffn2_in=w_ffn2_in,
             w_ffn2_out=w_ffn2_out, g_final=g_final)
    m = dict(w_ada=m_w_ada, b_ada=m_b_ada, g_norm_ffn1=m_g_norm_ffn1, w_ffn1_in=m_w_ffn1_in, w_ffn1_out=m_w_ffn1_out,
             g_norm_mix=m_g_norm_mix, w_in=m_w_in, w_pool=m_w_pool, pool_scale=m_pool_scale,
             w_pool_branch=m_w_pool_branch, w_attn_branch=m_w_attn_branch, w_out=m_w_out, g_norm_ffn2=m_g_norm_ffn2,
             w_ffn2_in=m_w_ffn2_in, w_ffn2_out=m_w_ffn2_out, g_final=m_g_final)
    v = dict(w_ada=v_w_ada, b_ada=v_b_ada, g_norm_ffn1=v_g_norm_ffn1, w_ffn1_in=v_w_ffn1_in, w_ffn1_out=v_w_ffn1_out,
             g_norm_mix=v_g_norm_mix, w_in=v_w_in, w_pool=v_w_pool, pool_scale=v_pool_scale,
             w_pool_branch=v_w_pool_branch, w_attn_branch=v_w_attn_branch, w_out=v_w_out, g_norm_ffn2=v_g_norm_ffn2,
             w_ffn2_in=v_w_ffn2_in, w_ffn2_out=v_w_ffn2_out, g_final=v_g_final)
    shapes = {n: w[n].shape for n in names}
    me = 4 * lax.axis_index("x") + 2 * lax.axis_index("y") + lax.axis_index("c")
    D = x.shape[-1]
    n_mod = w_ada.shape[-1] * N_DEV // D

    (c_all,) = _all_gather([c.reshape(D // 128, 128)], "gather_c")
    ada_cols = w_ada.shape[-1]
    b_mine = lax.dynamic_slice_in_dim(b_ada, me * ada_cols, ada_cols, axis=1)
    cond, mod_part = _ada_mod(c_all.reshape(N_DEV, D), w_ada[0], b_mine, "ada_mod")
    (mod_all,) = _all_gather([mod_part.reshape(-1, 128)], "gather_mod")
    mod_all = mod_all.reshape(N_DEV, N_DEV, ada_cols)
    mod = lax.dynamic_index_in_dim(mod_all, me, axis=1, keepdims=False).reshape(n_mod, D)

    def local(t, name):
        return t[name][0].T if name in TRANSPOSED else t[name][0]

    shards = {name: local(w, name).astype(BF16) for name in SHARDED}

    def gather_done(names, fulls):
        return {name: lax.dynamic_update_index_in_dim(full, shards[name], me, axis=0)
                for name, full in zip(names, fulls)}

    def ffn_weights(g, pre):
        return {"w%sin" % pre: g["w_ffn%s_in" % pre].reshape(2, -1, D),
                "w%sout" % pre: g["w_ffn%s_out" % pre].reshape(-1, D)}

    def later_weights(fulls):
        g = gather_done(LATER, fulls)
        return dict(win=g["w_in"].reshape(-1, D), wpb=_cols_to_full(g["w_pool_branch"]),
                    wab=_cols_to_full(g["w_attn_branch"]), wout=g["w_out"].reshape(D, D), **ffn_weights(g, "2"))

    W = ffn_weights(gather_done(FIRST, _all_gather([shards[n] for n in FIRST], "gather_ffn1", own=False)), "1")
    small = dict(g1=g_norm_ffn1[0], g2=g_norm_mix[0], g3=g_norm_ffn2[0], gf=g_final, w_pool=w_pool[0],
                 pool_scale=pool_scale[0])
    red = _GradReducer()
    loss_part, dx, G, dmod, dsmall = _local_step(
        x[0], loss_target[0], positions[0], mod, small, W,
        late=(_gather_rider([shards[n] for n in LATER]), later_weights), red=red)
    chip = (2 * lax.axis_index("x") + lax.axis_index("y")).astype(jnp.int32).reshape(1)

    part = _pack_small(dict(b_ada=dmod, g_norm_ffn1=dsmall["g1"], g_norm_mix=dsmall["g2"], g_norm_ffn2=dsmall["g3"],
                            g_final=dsmall["gf"], w_pool=dsmall["w_pool"], pool_scale=dsmall["pool_scale"]),
                       loss=loss_part)
    (parts,) = _all_gather([part], "gather_small")
    gsmall, loss = _unpack_small(_sum_slots(parts, "sum_small"), shapes)
    rows_mine = ada_cols // 128
    dmod_mine = lax.dynamic_slice_in_dim(parts, me * rows_mine, rows_mine, axis=1).reshape(N_DEV, ada_cols)

    grads, delta, new_m, new_v = {}, {}, {}, {}
    grads["w_ada"], delta["w_ada"], new_m["w_ada"], new_v["w_ada"] = (
        t[None] for t in _ada_grad_adamw(cond.T, dmod_mine, w_ada[0], m_w_ada[0], v_w_ada[0], "ada_grad_adamw"))
    for name in SHARDED:
        key = GRAD_KEY[name]
        res = _sum_adamw(chip, red.own[key], red.others[key], local(w, name), local(m, name), local(v, name),
                         "adamw_" + name)
        grads[name], delta[name], new_m[name], new_v[name] = (
            (t.T if name in TRANSPOSED else t)[None] for t in res)
    sd, sm, sv = _adamw(_pack_small(w), _pack_small(gsmall), _pack_small(m), _pack_small(v), "adamw_small")
    for dst, src in ((delta, sd), (new_m, sm), (new_v, sv)):
        dst.update(_unpack_small(src, shapes)[0])
    grads.update(gsmall)

    return (loss, dx[None], *[grads[n] for n in names], *[delta[n] for n in names],
            *[new_m[n] for n in names], *[new_v[n] for n in names])
```

```python
import functools

import jax
import jax.numpy as jnp
from jax import lax
from jax.experimental import pallas as pl
from jax.experimental.pallas import tpu as pltpu

F32 = jnp.float32
BF16 = jnp.bfloat16
MESH = pl.DeviceIdType.MESH
ANY = pl.BlockSpec(memory_space=pl.ANY)

N_DEV = 8
EPS = 1e-6
HEAD_DIM = 64
HEADS = 4
GW = HEADS * HEAD_DIM
DILATIONS = (1, 4, 16)
BAND = 128
QB = 128
POOL_WINDOWS = (2, 4, 8, 16)
HALO = 16
ROPE_THETA = 10000.0

ADAM_LR = 0.001
ADAM_B1 = 0.9
ADAM_B2 = 0.999
ADAM_EPS = 1e-08
ADAM_WD = 0.01
ADAM_STEP = 10

VMEM_LIMIT = 56 * 1024 * 1024
TS = 512
FFN_TS = 256
FC = 2816

NT = (((1,), (1,)), ((), ()))
TN = (((0,), (0,)), ((), ()))


def _params(**kw):
    return pltpu.CompilerParams(vmem_limit_bytes=VMEM_LIMIT, **kw)


def _dot(a, b):
    return jnp.dot(a, b, preferred_element_type=F32)


def _dot_nt(a, b):
    return lax.dot_general(a, b, NT, preferred_element_type=F32)


def _dot_tn(a, b):
    return lax.dot_general(a, b, TN, preferred_element_type=F32)


def _load_weights(pairs, sem):
    @pl.when(pl.program_id(0) == 0)
    def _():
        copies = [pltpu.make_async_copy(src, dst, sem.at[i]) for i, (src, dst) in enumerate(pairs)]
        for cp in copies:
            cp.start()
        for cp in copies:
            cp.wait()


def _norm_mod(x, g, sc, sh):
    r = lax.rsqrt(jnp.mean(x * x, axis=-1, keepdims=True) + EPS)
    xn = x * r
    y = xn * g
    return r, xn, y, y * (1.0 + sc) + sh


def _norm_mod_bwd(du, r, xn, y, g, sc):
    dsh = jnp.sum(du, axis=0, keepdims=True)
    dsc = jnp.sum(du * y, axis=0, keepdims=True)
    dy = du * (1.0 + sc)
    dg = jnp.sum(dy * xn, axis=0, keepdims=True)
    dxn = dy * g
    dx = r * (dxn - xn * jnp.mean(dxn * xn, axis=-1, keepdims=True))
    return dx, dsh, dsc, dg


def _row_tile(ts, width):
    return pl.BlockSpec((ts, width), lambda i: (i, 0))


def _const(shape):
    return pl.BlockSpec(shape, lambda *_: (0,) * len(shape))


def _ffn_fwd(h, vec, win, wout, name, rider=None):
    TS = FFN_TS
    S, D = h.shape
    _, Fd, _ = win.shape
    nch = Fd // FC

    def body(*refs):
        if rider is None:
            return compute(*refs)
        host, mine = rider.split(refs, 4, 5)
        rider.head(mine, pl.program_id(0))
        compute(*host)
        rider.tail(mine, pl.program_id(0), S // TS)

    def compute(h_ref, vec_ref, win_hbm, wout_hbm, hn_ref, u_ref, ab_ref, act_ref, f_ref, win_v, wout_v, sem):
        _load_weights([(win_hbm, win_v), (wout_hbm, wout_v)], sem)
        x = h_ref[...]
        g, sh, sc, gt = (vec_ref[k:k + 1, :] for k in range(4))
        _, _, _, u = _norm_mod(x, g, sc, sh)
        ub = u.astype(BF16)
        u_ref[...] = ub
        acc = jnp.zeros((TS, D), F32)
        for j in range(nch):
            sl = slice(j * FC, (j + 1) * FC)
            a = _dot_nt(ub, win_v[0, sl, :])
            b = _dot_nt(ub, win_v[1, sl, :])
            act = ((a * jax.nn.sigmoid(a)) * b).astype(BF16)
            ab_ref[0, :, sl] = a.astype(BF16)
            ab_ref[1, :, sl] = b.astype(BF16)
            act_ref[:, sl] = act
            acc = acc + _dot(act, wout_v[sl, :])
        f_ref[...] = acc
        hn_ref[...] = x + (0.5 * gt) * acc

    specs = (
        [_row_tile(TS, D), _const((8, D)), ANY, ANY],
        [_row_tile(TS, D), _row_tile(TS, D), pl.BlockSpec((2, TS, Fd), lambda i: (0, i, 0)),
         _row_tile(TS, Fd), _row_tile(TS, D)],
        [jax.ShapeDtypeStruct((S, D), F32), jax.ShapeDtypeStruct((S, D), BF16),
         jax.ShapeDtypeStruct((2, S, Fd), BF16), jax.ShapeDtypeStruct((S, Fd), BF16),
         jax.ShapeDtypeStruct((S, D), F32)],
        [pltpu.VMEM(win.shape, BF16), pltpu.VMEM(wout.shape, BF16), pltpu.SemaphoreType.DMA((2,))])
    in_specs, out_specs, out_shape, scratch = specs if rider is None else rider.specs(*specs)
    outs = pl.pallas_call(
        body, name=name, grid=(S // TS,), in_specs=in_specs, out_specs=out_specs, out_shape=out_shape,
        scratch_shapes=scratch, compiler_params=_params(),
    )(h, vec, win, wout, *(rider.arrays if rider else []))
    return outs if rider is None else (outs[:5], outs[5:])


def _ffn_bwd(dh, h, f, ab, vec, win, wout, name):
    TS = FFN_TS
    S, D = h.shape
    _, Fd, _ = win.shape
    nch = Fd // FC

    def body(dh_ref, h_ref, f_ref, ab_ref, vec_ref, win_hbm, wout_hbm,
             dhp_ref, dab_ref, df_ref, red_ref, win_v, wout_v, sem):
        _load_weights([(win_hbm, win_v), (wout_hbm, wout_v)], sem)

        @pl.when(pl.program_id(0) == 0)
        def _():
            red_ref[...] = jnp.zeros_like(red_ref)

        dh_v = dh_ref[...]
        x = h_ref[...]
        g, sh, sc, gt = (vec_ref[k:k + 1, :] for k in range(4))
        dgt = jnp.sum((0.5 * f_ref[...]) * dh_v, axis=0, keepdims=True)
        dfb = ((0.5 * gt) * dh_v).astype(BF16)
        df_ref[...] = dfb
        du = jnp.zeros((TS, D), F32)
        for j in range(nch):
            sl = slice(j * FC, (j + 1) * FC)
            dact = _dot_nt(dfb, wout_v[sl, :])
            av = ab_ref[0, :, sl].astype(F32)
            bv = ab_ref[1, :, sl].astype(F32)
            sg = jax.nn.sigmoid(av)
            da = (dact * bv * (sg * (1.0 + av * (1.0 - sg)))).astype(BF16)
            db = (dact * (av * sg)).astype(BF16)
            dab_ref[0, :, sl] = da
            dab_ref[1, :, sl] = db
            du = du + _dot(da, win_v[0, sl, :]) + _dot(db, win_v[1, sl, :])
        r, xn, y, _ = _norm_mod(x, g, sc, sh)
        dx, dsh, dsc, dg = _norm_mod_bwd(du, r, xn, y, g, sc)
        dhp_ref[...] = dh_v + dx
        red_ref[0:1, :] += dgt
        red_ref[1:2, :] += dsh
        red_ref[2:3, :] += dsc
        red_ref[3:4, :] += dg

    ab_spec = pl.BlockSpec((2, TS, Fd), lambda i: (0, i, 0))
    return pl.pallas_call(
        body, name=name, grid=(S // TS,),
        in_specs=[_row_tile(TS, D), _row_tile(TS, D), _row_tile(TS, D), ab_spec, _const((8, D)), ANY, ANY],
        out_specs=[_row_tile(TS, D), ab_spec, _row_tile(TS, D), _const((8, D))],
        out_shape=[jax.ShapeDtypeStruct((S, D), F32), jax.ShapeDtypeStruct((2, S, Fd), BF16),
                   jax.ShapeDtypeStruct((S, D), BF16), jax.ShapeDtypeStruct((8, D), F32)],
        scratch_shapes=[pltpu.VMEM(win.shape, BF16), pltpu.VMEM(wout.shape, BF16), pltpu.SemaphoreType.DMA((2,))],
        compiler_params=_params(),
    )(dh, h, f, ab, vec, win, wout)


def _wgrad(x, y, name, tm=None, ts=2048, rider=None):
    xb = x.ndim == 3
    nb = x.shape[0] if xb else 0
    S, M = x.shape[-2:]
    N = y.shape[-1]
    tm = tm or M
    ts = min(ts, S)
    nk = S // ts
    grid = (max(nb, 1), M // tm, nk)

    def body(*refs):
        if rider is None:
            return compute(*refs)
        host, mine = rider.split(refs, 2, 1)
        step = (pl.program_id(0) * grid[1] + pl.program_id(1)) * grid[2] + pl.program_id(2)
        rider.head(mine, step)
        compute(*host)
        rider.tail(mine, step, grid[0] * grid[1] * grid[2])

    def compute(x_ref, y_ref, o_ref, acc):
        k = pl.program_id(2)

        @pl.when(k == 0)
        def _():
            acc[...] = jnp.zeros_like(acc)

        acc[...] += _dot_tn(x_ref[...], y_ref[...])

        @pl.when(k == nk - 1)
        def _():
            o_ref[...] = acc[...].astype(BF16)

    x_spec = (pl.BlockSpec((None, ts, tm), lambda b, i, k: (b, k, i)) if xb
              else pl.BlockSpec((ts, tm), lambda b, i, k: (k, i)))
    y_spec = pl.BlockSpec((ts, N), lambda b, i, k: (k, 0))
    if xb:
        o_spec, o_shape = pl.BlockSpec((None, tm, N), lambda b, i, k: (b, i, 0)), (nb, M, N)
    else:
        o_spec, o_shape = pl.BlockSpec((tm, N), lambda b, i, k: (i, 0)), (M, N)
    specs = ([x_spec, y_spec], [o_spec], [jax.ShapeDtypeStruct(o_shape, BF16)], [pltpu.VMEM((tm, N), F32)])
    in_specs, out_specs, out_shape, scratch = specs if rider is None else rider.specs(*specs)
    outs = pl.pallas_call(
        body, name=name, grid=grid, in_specs=in_specs, out_specs=out_specs, out_shape=out_shape,
        scratch_shapes=scratch, compiler_params=_params(),
    )(x, y, *(rider.arrays if rider else []))
    return outs[0] if rider is None else (outs[0], outs[1:])


P_OFF, Q_OFF, K_OFF, V_OFF, G_OFF = 0, 256, 1024, 1792, 2560
IN_WIDTH = 4608


def _first_half_mask(ts):
    lane = lax.broadcasted_iota(jnp.int32, (ts, 128), 1)
    return (lane % HEAD_DIM) < (HEAD_DIM // 2)


def _rope(t, cos, sin_signed, first, sign):
    partner = jnp.where(first, pltpu.roll(t, 96, 1), pltpu.roll(t, 32, 1))
    return t * cos + sign * (partner * sin_signed)


def _res_spec(r):
    return pl.BlockSpec((r, TS // r, GW), lambda i: (0, i, 0))


def _res_shape(S, r, dtype):
    return jax.ShapeDtypeStruct((r, S // r, GW), dtype)


def _to_residues(piece, out_ref, lanes, r, scr):
    if r == 1:
        out_ref[0, :, lanes] = piece.astype(out_ref.dtype)
        return
    for h in range(piece.shape[1] // 128):
        scr[h] = piece[:, h * 128:(h + 1) * 128]
        at = slice(lanes.start + h * 128, lanes.start + (h + 1) * 128)
        for res in range(r):
            out_ref[res, :, at] = scr[h, pl.ds(res, TS // r, stride=r), :].astype(out_ref.dtype)


def _from_residues(in_ref, lanes, r, scr):
    if r == 1:
        return in_ref[0, :, lanes].astype(F32)
    halves = (lanes.stop - lanes.start) // 128
    for h in range(halves):
        at = slice(lanes.start + h * 128, lanes.start + (h + 1) * 128)
        for res in range(r):
            scr[h, pl.ds(res, TS // r, stride=r), :] = in_ref[res, :, at].astype(F32)
    return scr[0] if halves == 1 else jnp.concatenate([scr[0], scr[1]], axis=1)


RES_SCRATCH = (2, TS, 128)


def _mix_in_fwd(h, vec, cos, sin, win, name):
    S, D = h.shape

    def body(h_ref, vec_ref, cos_ref, sin_ref, win_hbm, u_ref, p_ref, gates_ref, *rest):
        qkv_refs, (win_v, sem, scr) = rest[:9], rest[9:]
        _load_weights([(win_hbm, win_v)], sem)
        g, sh, sc = (vec_ref[k:k + 1, :] for k in range(3))
        _, _, _, u = _norm_mod(h_ref[...], g, sc, sh)
        ub = u.astype(BF16)
        u_ref[...] = ub
        p_ref[...] = _dot_nt(ub, win_v[P_OFF:Q_OFF, :])
        cosv, sinv = cos_ref[...], sin_ref[...]
        first = _first_half_mask(TS)
        for which, off in enumerate((Q_OFF, K_OFF, V_OFF)):
            t = _dot_nt(ub, win_v[off:off + 3 * GW, :])
            for gi in range(3):
                for half in range(2):
                    c0 = gi * GW + half * 128
                    piece = t[:, c0:c0 + 128]
                    if which < 2:
                        piece = _rope(piece, cosv, sinv, first, 1.0)
                    _to_residues(piece, qkv_refs[which * 3 + gi], slice(half * 128, (half + 1) * 128),
                                 DILATIONS[gi], scr)
        gates_ref[...] = jax.nn.sigmoid(_dot_nt(ub, win_v[G_OFF:IN_WIDTH, :]))

    return pl.pallas_call(
        body, name=name, grid=(S // TS,),
        in_specs=[_row_tile(TS, D), _const((8, D)), _row_tile(TS, 128), _row_tile(TS, 128), ANY],
        out_specs=[_row_tile(TS, D), _row_tile(TS, GW), _row_tile(TS, 2 * D)] + [_res_spec(r) for r in DILATIONS] * 3,
        out_shape=[jax.ShapeDtypeStruct((S, D), BF16), jax.ShapeDtypeStruct((S, GW), F32),
                   jax.ShapeDtypeStruct((S, 2 * D), F32)] + [_res_shape(S, r, BF16) for r in DILATIONS] * 3,
        scratch_shapes=[pltpu.VMEM((IN_WIDTH, D), BF16), pltpu.SemaphoreType.DMA((1,)), pltpu.VMEM(RES_SCRATCH, F32)],
        compiler_params=_params(),
    )(h, vec, cos, sin, win)


def _mix_in_bwd(dh, h, vec, cos, sin, dp, dqkv, dgl, win, name):
    S, D = h.shape

    def body(dh_ref, h_ref, vec_ref, cos_ref, sin_ref, dp_ref, *rest):
        dqkv_refs = rest[:9]
        dgl_ref, win_hbm, dhp_ref, dproj_ref, red_ref, win_v, sem, scr = rest[9:]
        _load_weights([(win_hbm, win_v)], sem)

        @pl.when(pl.program_id(0) == 0)
        def _():
            red_ref[...] = jnp.zeros_like(red_ref)

        cosv, sinv = cos_ref[...], sin_ref[...]
        first = _first_half_mask(TS)
        dproj_ref[:, P_OFF:Q_OFF] = dp_ref[...].astype(BF16)
        for which, off in enumerate((Q_OFF, K_OFF, V_OFF)):
            for gi in range(3):
                for half in range(2):
                    piece = _from_residues(dqkv_refs[which * 3 + gi], slice(half * 128, (half + 1) * 128),
                                           DILATIONS[gi], scr)
                    if which < 2:
                        piece = _rope(piece, cosv, sinv, first, -1.0)
                    c0 = off + gi * GW + half * 128
                    dproj_ref[:, c0:c0 + 128] = piece.astype(BF16)
        dproj_ref[:, G_OFF:IN_WIDTH] = dgl_ref[...]
        du = _dot(dproj_ref[...], win_v[...])
        g, sh, sc = (vec_ref[k:k + 1, :] for k in range(3))
        r, xn, y, _ = _norm_mod(h_ref[...], g, sc, sh)
        dx, dsh, dsc, dg = _norm_mod_bwd(du, r, xn, y, g, sc)
        dhp_ref[...] = dh_ref[...] + dx
        red_ref[1:2, :] += dsh
        red_ref[2:3, :] += dsc
        red_ref[3:4, :] += dg

    return pl.pallas_call(
        body, name=name, grid=(S // TS,),
        in_specs=[_row_tile(TS, D), _row_tile(TS, D), _const((8, D)), _row_tile(TS, 128), _row_tile(TS, 128),
                  _row_tile(TS, GW)] + [_res_spec(r) for r in DILATIONS] * 3 + [_row_tile(TS, 2 * D), ANY],
        out_specs=[_row_tile(TS, D), _row_tile(TS, IN_WIDTH), _const((8, D))],
        out_shape=[jax.ShapeDtypeStruct((S, D), F32), jax.ShapeDtypeStruct((S, IN_WIDTH), BF16),
                   jax.ShapeDtypeStruct((8, D), F32)],
        scratch_shapes=[pltpu.VMEM((IN_WIDTH, D), BF16), pltpu.SemaphoreType.DMA((1,)), pltpu.VMEM(RES_SCRATCH, F32)],
        compiler_params=_params(),
    )(dh, h, vec, cos, sin, dp, *dqkv, dgl, win)


def _pool_lanes(rows):
    lane = lax.broadcasted_iota(jnp.int32, (rows, GW), 1)
    return lane // HEAD_DIM


def _pool_window(rows):
    grp = _pool_lanes(rows)
    w = jnp.full((rows, GW), POOL_WINDOWS[0], jnp.int32)
    for k in range(1, len(POOL_WINDOWS)):
        w = jnp.where(grp == k, POOL_WINDOWS[k], w)
    return grp, w


def _pool_fwd(p, wbd, scale, name, ts=512):
    S = p.shape[0]
    ext = ts + HALO

    def body(pc_ref, ph_ref, wbd_ref, sc_ref, d_ref, y_ref):
        i = pl.program_id(0)
        cur = pc_ref[...]
        halo = jnp.where(i > 0, ph_ref[...], 0.0)
        s = jnp.concatenate([halo, cur], axis=0)
        grp, w = _pool_window(ext)
        sel = jnp.zeros((ext, GW), F32)
        for k, wk in enumerate(POOL_WINDOWS):
            s = s + pltpu.roll(s, wk // 2, 0)
            sel = jnp.where(grp == k, s, sel)
        t = i * ts + lax.broadcasted_iota(jnp.int32, (ts, GW), 0)
        count = jnp.minimum(t + 1, w[HALO:]).astype(F32)
        d = (sel[HALO:] / count - cur).astype(BF16)
        d_ref[...] = d
        y_ref[...] = (_dot(d, wbd_ref[...]) * sc_ref[...]).astype(BF16)

    return pl.pallas_call(
        body, name=name, grid=(S // ts,),
        in_specs=[_row_tile(ts, GW),
                  pl.BlockSpec((HALO, GW), lambda i: (jnp.maximum(i * (ts // HALO) - 1, 0), 0)),
                  _const((GW, GW)), _const((1, GW))],
        out_specs=[_row_tile(ts, GW), _row_tile(ts, GW)],
        out_shape=[jax.ShapeDtypeStruct((S, GW), BF16), jax.ShapeDtypeStruct((S, GW), BF16)],
        compiler_params=_params(),
    )(p, p, wbd, scale)


def _pool_bwd(dy, d, wbd, scale, name, ts=512):
    S = dy.shape[0]
    ext = ts + HALO
    nsteps = S // ts
    last_halo = S // HALO - 1

    def body(dyc_ref, dyh_ref, d_ref, wbd_ref, sc_ref, dp_ref, dw_ref, ds_ref):
        i = pl.program_id(0)

        @pl.when(i == 0)
        def _():
            dw_ref[...] = jnp.zeros_like(dw_ref)
            ds_ref[...] = jnp.zeros_like(ds_ref)

        dyc = dyc_ref[...]
        dyh = jnp.where(i < nsteps - 1, dyh_ref[...], 0.0)
        dys = (jnp.concatenate([dyc, dyh], axis=0) * sc_ref[...]).astype(BF16)
        dd = _dot_nt(dys, wbd_ref[...])
        grp, w = _pool_window(ext)
        t = i * ts + lax.broadcasted_iota(jnp.int32, (ext, GW), 0)
        s = dd / jnp.minimum(t + 1, w).astype(F32)
        sel = jnp.zeros((ext, GW), F32)
        for k, wk in enumerate(POOL_WINDOWS):
            s = s + pltpu.roll(s, ext - wk // 2, 0)
            sel = jnp.where(grp == k, s, sel)
        dp_ref[...] = sel[:ts] - dd[:ts]
        dv = d_ref[...]
        z = _dot(dv, wbd_ref[...])
        ds_ref[0:1, :] += jnp.sum(dyc * z, axis=0, keepdims=True)
        dw_ref[...] += _dot_tn(dv, dys[:ts])

    return pl.pallas_call(
        body, name=name, grid=(nsteps,),
        in_specs=[_row_tile(ts, GW),
                  pl.BlockSpec((HALO, GW), lambda i: (jnp.minimum((i + 1) * (ts // HALO), last_halo), 0)),
                  _row_tile(ts, GW), _const((GW, GW)), _const((1, GW))],
        out_specs=[_row_tile(ts, GW), _const((GW, GW)), _const((8, GW))],
        out_shape=[jax.ShapeDtypeStruct((S, GW), F32), jax.ShapeDtypeStruct((GW, GW), F32),
                   jax.ShapeDtypeStruct((8, GW), F32)],
        compiler_params=_params(),
    )(dy, dy, d, wbd, scale)


def _head_id(rows):
    return lax.broadcasted_iota(jnp.int32, (rows, GW), 1) // HEAD_DIM


def _stack_heads(t, hid):
    return jnp.concatenate([jnp.where(hid == h, t, jnp.zeros_like(t)) for h in range(HEADS)], axis=0)


def _unstack_heads(t_all, hid):
    out = jnp.zeros((QB, GW), F32)
    for h in range(HEADS):
        out = jnp.where(hid == h, t_all[h * QB:(h + 1) * QB], out)
    return out


def _band_mask(n):
    row = lax.broadcasted_iota(jnp.int32, (HEADS * QB, 2 * QB), 0) % QB
    col = lax.broadcasted_iota(jnp.int32, (HEADS * QB, 2 * QB), 1)
    rel = row + QB - col
    return (rel >= 0) & (rel <= BAND) & ((col >= QB) | (n > 0))


def _streams(r, nb, halves_ok=True):
    if r >= 2:
        return (r // 2, nb), [(lambda rb, l=l: 2 * rb + l, 0) for l in range(2)]
    if halves_ok:
        return (1, nb // 2), [(lambda rb: 0, l * (nb // 2)) for l in range(2)]
    return (1, nb), [(lambda rb: 0, 0)]


def _attn_fwd(q, k, v, name):
    r, L, _ = q.shape
    grid, streams = _streams(r, L // QB)

    def cur(res, off):
        return pl.BlockSpec((None, QB, GW), lambda rb, n: (res(rb), n + off, 0))

    def prev(res, off):
        return pl.BlockSpec((None, QB, GW), lambda rb, n: (res(rb), jnp.maximum(n + off - 1, 0), 0))

    def body(*refs):
        n = pl.program_id(1)
        hid = _head_id(QB)
        o_ref, lse_ref = refs[5 * len(streams):]
        for l, (_, off) in enumerate(streams):
            q_ref, kp_ref, kc_ref, vp_ref, vc_ref = refs[5 * l:5 * l + 5]
            qs = _stack_heads(q_ref[...], hid)
            kc = jnp.concatenate([kp_ref[...], kc_ref[...]], axis=0)
            vc = jnp.concatenate([vp_ref[...], vc_ref[...]], axis=0)
            s = _dot_nt(qs, kc) * (HEAD_DIM ** -0.5)
            s = jnp.where(_band_mask(n + off), s, -jnp.inf)
            m = jnp.max(s, axis=-1, keepdims=True)
            e = jnp.exp(s - m)
            den = jnp.sum(e, axis=-1, keepdims=True)
            lse = m + jnp.log(den)
            pr = (e * (1.0 / den)).astype(BF16)
            o_ref[l] = _unstack_heads(_dot(pr, vc), hid)
            lse_ref[l] = _unstack_heads(jnp.broadcast_to(lse, (HEADS * QB, GW)), hid)

    in_specs, args = [], []
    for res, off in streams:
        in_specs += [cur(res, off), prev(res, off), cur(res, off), prev(res, off), cur(res, off)]
        args += [q, k, k, v, v]
    out = jax.ShapeDtypeStruct((2 * grid[0], grid[1] * QB, GW), F32)
    both = pl.BlockSpec((2, QB, GW), lambda rb, n: (rb, n, 0))
    o, lse = pl.pallas_call(
        body, name=name, grid=grid, in_specs=in_specs, out_specs=[both, both], out_shape=[out, out],
        compiler_params=_params(),
    )(*args)
    return o.reshape(q.shape), lse.reshape(q.shape)


def _head_rows(t_full, hid):
    return jnp.concatenate(
        [jnp.max(jnp.where(hid == h, t_full, -jnp.inf), axis=-1, keepdims=True) for h in range(HEADS)], axis=0)


def _attn_bwd(q, k, v, do, lse, cterm, name):
    r, L, _ = q.shape
    nb = L // QB
    (nrb, _), streams = _streams(r, nb, halves_ok=False)
    ns = len(streams)

    def spec(res, index):
        return pl.BlockSpec((None, QB, GW), lambda rb, n: (res(rb), index(n), 0))

    def qside(n):
        return jnp.minimum(n, nb - 1)

    def kprev(n):
        return jnp.clip(n - 1, 0, nb - 1)

    def body(*refs):
        dq_ref, dk_ref, dv_ref, carry_k, carry_v = refs[8 * ns:]
        n = pl.program_id(1)

        @pl.when(n == 0)
        def _():
            carry_k[...] = jnp.zeros_like(carry_k)
            carry_v[...] = jnp.zeros_like(carry_v)

        @pl.when(n < nb)
        def _():
            hid = _head_id(QB)
            for l in range(ns):
                q_ref, do_ref, lse_ref, c_ref, kp_ref, kc_ref, vp_ref, vc_ref = refs[8 * l:8 * l + 8]
                qs = _stack_heads(q_ref[...], hid)
                dos = _stack_heads(do_ref[...], hid)
                kc = jnp.concatenate([kp_ref[...], kc_ref[...]], axis=0)
                vc = jnp.concatenate([vp_ref[...], vc_ref[...]], axis=0)
                s = _dot_nt(qs, kc) * (HEAD_DIM ** -0.5)
                s = jnp.where(_band_mask(n), s, -jnp.inf)
                p = jnp.exp(s - _head_rows(lse_ref[...], hid))
                dp = _dot_nt(dos, vc)
                ds = (p * (dp + _head_rows(c_ref[...], hid)) * (HEAD_DIM ** -0.5)).astype(BF16)
                dq_ref[l] = _unstack_heads(_dot(ds, kc), hid)
                dkc = _dot_tn(ds, qs)
                dvc = _dot_tn(p.astype(BF16), dos)
                dk_ref[l] = carry_k[l] + dkc[:QB]
                dv_ref[l] = carry_v[l] + dvc[:QB]
                carry_k[l] = dkc[QB:]
                carry_v[l] = dvc[QB:]

        @pl.when(n == nb)
        def _():
            dk_ref[...] = carry_k[...]
            dv_ref[...] = carry_v[...]

    in_specs, args = [], []
    for res, _ in streams:
        in_specs += [spec(res, qside)] * 4 + [spec(res, kprev), spec(res, qside)] * 2
        args += [q, do, lse, cterm, k, k, v, v]
    out = jax.ShapeDtypeStruct(q.shape, F32)
    qout = pl.BlockSpec((ns, QB, GW), lambda rb, n: (rb, jnp.minimum(n, nb - 1), 0))
    kout = pl.BlockSpec((ns, QB, GW), lambda rb, n: (rb, jnp.maximum(n - 1, 0), 0))
    return pl.pallas_call(
        body, name=name, grid=(nrb, nb + 1),
        in_specs=in_specs, out_specs=[qout, kout, kout], out_shape=[out, out, out],
        scratch_shapes=[pltpu.VMEM((ns, QB, GW), F32), pltpu.VMEM((ns, QB, GW), F32)],
        compiler_params=_params(),
    )(*args)


def _token_order(refs, scr):
    return [_from_residues(ref, slice(0, GW), r, scr) for ref, r in zip(refs, DILATIONS)]


def _group_weights(lses):
    l0, l1, l2 = lses
    m = jnp.maximum(jnp.maximum(l0, l1), l2)
    e = [jnp.exp(l - m) for l in (l0, l1, l2)]
    den = e[0] + e[1] + e[2]
    return [ei / den for ei in e]


def _mix_out_fwd(h, vec, gates, ypool, o3, lse3, wpb, wab, wout, name):
    S, D = h.shape

    def body(h_ref, vec_ref, gates_ref, yp_ref, o0, o1, o2, l0, l1, l2, wpb_hbm, wab_hbm, wout_hbm,
             hn_ref, ya_ref, merged_ref, tm_ref, wpb_v, wab_v, wout_v, sem, scr):
        _load_weights([(wpb_hbm, wpb_v), (wab_hbm, wab_v), (wout_hbm, wout_v)], sem)
        gt = vec_ref[3:4, :]
        wts = _group_weights(_token_order((l0, l1, l2), scr))
        og = _token_order((o0, o1, o2), scr)
        ya = (wts[0] * og[0] + wts[1] * og[1] + wts[2] * og[2]).astype(BF16)
        ya_ref[...] = ya
        merged = (gates_ref[:, :D] * _dot(yp_ref[...], wpb_v[...])
                  + gates_ref[:, D:] * _dot(ya, wab_v[...])).astype(BF16)
        merged_ref[...] = merged
        tm = _dot(merged, wout_v[...])
        tm_ref[...] = tm
        hn_ref[...] = h_ref[...] + gt * tm

    grp = _row_tile(TS, GW)
    res = [_res_spec(r) for r in DILATIONS]
    return pl.pallas_call(
        body, name=name, grid=(S // TS,),
        in_specs=[_row_tile(TS, D), _const((8, D)), _row_tile(TS, 2 * D), grp] + res * 2 + [ANY, ANY, ANY],
        out_specs=[_row_tile(TS, D), grp, _row_tile(TS, D), _row_tile(TS, D)],
        out_shape=[jax.ShapeDtypeStruct((S, D), F32), jax.ShapeDtypeStruct((S, GW), BF16),
                   jax.ShapeDtypeStruct((S, D), BF16), jax.ShapeDtypeStruct((S, D), F32)],
        scratch_shapes=[pltpu.VMEM((GW, D), BF16), pltpu.VMEM((GW, D), BF16), pltpu.VMEM((D, D), BF16),
                        pltpu.SemaphoreType.DMA((3,)), pltpu.VMEM(RES_SCRATCH, F32)],
        compiler_params=_params(),
    )(h, vec, gates, ypool, *o3, *lse3, wpb, wab, wout)


def _mix_out_bwd(dh, tm, vec, gates, ypool, o3, lse3, wpb, wab, wout, name, rider=None):
    S, D = dh.shape

    def body(*refs):
        if rider is None:
            return compute(*refs)
        host, mine = rider.split(refs, 14, 12)
        rider.head(mine, pl.program_id(0))
        compute(*host)
        rider.tail(mine, pl.program_id(0), S // TS)

    def compute(dh_ref, tm_ref, vec_ref, gates_ref, yp_ref, o0, o1, o2, l0, l1, l2, wpb_hbm, wab_hbm, wout_hbm,
                dtm_ref, dgl_ref, dypb_ref, dyab_ref, dyp_ref, do0, do1, do2, c0, c1, c2, red_ref,
                wpb_v, wab_v, wout_v, sem, scr):
        _load_weights([(wpb_hbm, wpb_v), (wab_hbm, wab_v), (wout_hbm, wout_v)], sem)

        @pl.when(pl.program_id(0) == 0)
        def _():
            red_ref[...] = jnp.zeros_like(red_ref)

        gt = vec_ref[3:4, :]
        dh_v = dh_ref[...]
        red_ref[0:1, :] += jnp.sum(tm_ref[...] * dh_v, axis=0, keepdims=True)
        dtm = (gt * dh_v).astype(BF16)
        dtm_ref[...] = dtm
        dm = _dot_nt(dtm, wout_v[...])
        wts = _group_weights(_token_order((l0, l1, l2), scr))
        og = _token_order((o0, o1, o2), scr)
        ya = wts[0] * og[0] + wts[1] * og[1] + wts[2] * og[2]
        ypb = _dot(yp_ref[...], wpb_v[...])
        yab = _dot(ya.astype(BF16), wab_v[...])
        gp = gates_ref[:, :D]
        ga = gates_ref[:, D:]
        dgl_ref[:, :D] = (dm * ypb * gp * (1.0 - gp)).astype(BF16)
        dgl_ref[:, D:] = (dm * yab * ga * (1.0 - ga)).astype(BF16)
        dypb = (dm * gp).astype(BF16)
        dyab = (dm * ga).astype(BF16)
        dypb_ref[...] = dypb
        dyab_ref[...] = dyab
        dyp_ref[...] = _dot_nt(dypb, wpb_v[...])
        dya = _dot_nt(dyab, wab_v[...])
        row = lax.broadcasted_iota(jnp.int32, (GW, GW), 0) // HEAD_DIM
        col = lax.broadcasted_iota(jnp.int32, (GW, GW), 1) // HEAD_DIM
        ones = jnp.where(row == col, 1.0, 0.0).astype(F32)
        tot = jnp.dot(dya * ya, ones, preferred_element_type=F32, precision=lax.Precision.HIGHEST)
        for wg, do_ref, c_ref, r in zip(wts, (do0, do1, do2), (c0, c1, c2), DILATIONS):
            _to_residues(wg * dya, do_ref, slice(0, GW), r, scr)
            _to_residues(-(wg * tot), c_ref, slice(0, GW), r, scr)

    grp = _row_tile(TS, GW)
    res = [_res_spec(r) for r in DILATIONS]
    specs = (
        [_row_tile(TS, D), _row_tile(TS, D), _const((8, D)), _row_tile(TS, 2 * D), grp] + res * 2
        + [ANY, ANY, ANY],
        [_row_tile(TS, D), _row_tile(TS, 2 * D), _row_tile(TS, D), _row_tile(TS, D), grp]
        + res * 2 + [_const((8, D))],
        [jax.ShapeDtypeStruct((S, D), BF16), jax.ShapeDtypeStruct((S, 2 * D), BF16),
         jax.ShapeDtypeStruct((S, D), BF16), jax.ShapeDtypeStruct((S, D), BF16), jax.ShapeDtypeStruct((S, GW), F32)]
        + [_res_shape(S, r, BF16) for r in DILATIONS] + [_res_shape(S, r, F32) for r in DILATIONS]
        + [jax.ShapeDtypeStruct((8, D), F32)],
        [pltpu.VMEM((GW, D), BF16), pltpu.VMEM((GW, D), BF16), pltpu.VMEM((D, D), BF16),
         pltpu.SemaphoreType.DMA((3,)), pltpu.VMEM(RES_SCRATCH, F32)])
    in_specs, out_specs, out_shape, scratch = specs if rider is None else rider.specs(*specs)
    outs = pl.pallas_call(
        body, name=name, grid=(S // TS,), in_specs=in_specs, out_specs=out_specs, out_shape=out_shape,
        scratch_shapes=scratch, compiler_params=_params(),
    )(dh, tm, vec, gates, ypool, *o3, *lse3, wpb, wab, wout, *(rider.arrays if rider else []))
    return outs if rider is None else (outs[:12], outs[12:])


def _final_loss(h, target, gf, name):
    S, D = h.shape

    def body(h_ref, t_ref, g_ref, dh_ref, loss_ref, dg_ref):
        @pl.when(pl.program_id(0) == 0)
        def _():
            loss_ref[...] = jnp.zeros_like(loss_ref)
            dg_ref[...] = jnp.zeros_like(dg_ref)

        x = h_ref[...]
        g = g_ref[0:1, :]
        r = lax.rsqrt(jnp.mean(x * x, axis=-1, keepdims=True) + EPS)
        xn = x * r
        err = xn * g - t_ref[...]
        loss_ref[...] += 0.5 * jnp.sum(jnp.mean(err * err, axis=-1, keepdims=True))
        dy = err * (1.0 / D)
        dg_ref[0:1, :] += jnp.sum(dy * xn, axis=0, keepdims=True)
        dxn = dy * g
        dh_ref[...] = r * (dxn - xn * jnp.mean(dxn * xn, axis=-1, keepdims=True))

    return pl.pallas_call(
        body, name=name, grid=(S // TS,),
        in_specs=[_row_tile(TS, D), _row_tile(TS, D), _const((8, D))],
        out_specs=[_row_tile(TS, D), _const((8, 128)), _const((8, D))],
        out_shape=[jax.ShapeDtypeStruct((S, D), F32), jax.ShapeDtypeStruct((8, 128), F32),
                   jax.ShapeDtypeStruct((8, D), F32)],
        compiler_params=_params(),
    )(h, target, gf)


def _ada_mod(c_all, w, b, name):
    def body(c_ref, w_ref, b_ref, cond_ref, mod_ref):
        cv = c_ref[...]
        cond = cv * jax.nn.sigmoid(cv)
        cond_ref[...] = cond
        mod_ref[...] = jnp.dot(cond, w_ref[...], preferred_element_type=F32,
                               precision=lax.Precision.HIGHEST) + b_ref[...]

    return pl.pallas_call(
        body, name=name,
        out_shape=[jax.ShapeDtypeStruct(c_all.shape, F32), jax.ShapeDtypeStruct((c_all.shape[0], w.shape[1]), F32)],
        compiler_params=_params(),
    )(c_all, w, b)


def _adamw_math(w, g, m, v):
    m = ADAM_B1 * m + (1.0 - ADAM_B1) * g
    v = ADAM_B2 * v + (1.0 - ADAM_B2) * (g * g)
    m_hat = m / (1.0 - ADAM_B1 ** ADAM_STEP)
    v_hat = v / (1.0 - ADAM_B2 ** ADAM_STEP)
    delta = -ADAM_LR * (m_hat / (jnp.sqrt(v_hat) + ADAM_EPS) + ADAM_WD * w)
    return delta, m, v


def _adamw(w, g, m, v, name):
    R, C = w.shape
    tr = R
    for cand in (256, 128, 64, 32, 16, 8):
        if R % cand == 0:
            tr = cand
            break

    def body(w_ref, g_ref, m_ref, v_ref, d_ref, mo_ref, vo_ref):
        d_ref[...], mo_ref[...], vo_ref[...] = _adamw_math(w_ref[...], g_ref[...], m_ref[...], v_ref[...])

    spec = _row_tile(tr, C)
    out = jax.ShapeDtypeStruct((R, C), F32)
    return pl.pallas_call(
        body, name=name, grid=(R // tr,), in_specs=[spec] * 4, out_specs=[spec] * 3, out_shape=[out] * 3,
        compiler_params=_params(),
    )(w, g, m, v)


def _ada_grad_adamw(cond_t, dmod, w, m, v, name, tr=256):
    R, C = w.shape
    nb = dmod.shape[0]

    def body(ct_ref, dm_ref, w_ref, m_ref, v_ref, g_ref, d_ref, mo_ref, vo_ref):
        ct = ct_ref[...]
        dm = dm_ref[...]
        g = jnp.zeros((tr, C), F32)
        for bi in range(nb):
            g = g + ct[:, bi:bi + 1] * dm[bi:bi + 1, :]
        g_ref[...] = g
        d_ref[...], mo_ref[...], vo_ref[...] = _adamw_math(w_ref[...], g, m_ref[...], v_ref[...])

    spec = _row_tile(tr, C)
    out = jax.ShapeDtypeStruct((R, C), F32)
    return pl.pallas_call(
        body, name=name, grid=(R // tr,),
        in_specs=[_row_tile(tr, nb), _const((nb, C)), spec, spec, spec],
        out_specs=[spec] * 4, out_shape=[out] * 4,
        compiler_params=_params(),
    )(cond_t, dmod, w, m, v)


def _row_step(rows, cap=256):
    for cand in range(cap, 15, -16):
        if rows % cand == 0:
            return cand
    return rows


def _slot_sum(x_ref):
    acc = x_ref[0].astype(F32)
    for k in range(1, x_ref.shape[0]):
        acc = acc + x_ref[k].astype(F32)
    return acc


def _sum_slots(x, name, out_dtype=F32):
    n, R, C = x.shape
    tr = _row_step(R)

    def body(x_ref, o_ref):
        o_ref[...] = _slot_sum(x_ref).astype(out_dtype)

    return pl.pallas_call(
        body, name=name, grid=(R // tr,),
        in_specs=[pl.BlockSpec((n, tr, C), lambda i: (0, i, 0))],
        out_specs=_row_tile(tr, C), out_shape=jax.ShapeDtypeStruct((R, C), out_dtype),
        compiler_params=_params(),
    )(x)


def _sum_pair(core, g, recv, name):
    _, _, R, C = g.shape
    tr = _row_step(R)

    def body(core_ref, g_ref, r_ref, o_ref):
        o_ref[...] = (g_ref[...].astype(F32) + r_ref[...].astype(F32)).astype(BF16)

    return pl.pallas_call(
        body, name=name, out_shape=jax.ShapeDtypeStruct((4, R, C), BF16),
        grid_spec=pltpu.PrefetchScalarGridSpec(
            num_scalar_prefetch=1, grid=(4, R // tr),
            in_specs=[pl.BlockSpec((None, None, tr, C), lambda k, i, core_ref: (k, core_ref[0], i, 0)),
                      pl.BlockSpec((None, tr, C), lambda k, i, core_ref: (k, i, 0))],
            out_specs=pl.BlockSpec((None, tr, C), lambda k, i, core_ref: (k, i, 0))),
        compiler_params=_params(),
    )(core, g, recv)


def _sum_adamw(chip, own, recv, w, m, v, name):
    _, R, C = own.shape
    tr = _row_step(R)

    def body(chip_ref, own_ref, r_ref, w_ref, m_ref, v_ref, g_ref, d_ref, mo_ref, vo_ref):
        g = own_ref[...].astype(F32) + _slot_sum(r_ref)
        g_ref[...] = g
        d_ref[...], mo_ref[...], vo_ref[...] = _adamw_math(w_ref[...], g, m_ref[...], v_ref[...])

    spec = pl.BlockSpec((tr, C), lambda i, chip_ref: (i, 0))
    out = jax.ShapeDtypeStruct((R, C), F32)
    return pl.pallas_call(
        body, name=name, out_shape=[out] * 4,
        grid_spec=pltpu.PrefetchScalarGridSpec(
            num_scalar_prefetch=1, grid=(R // tr,),
            in_specs=[pl.BlockSpec((None, tr, C), lambda i, chip_ref: (chip_ref[0], i, 0)),
                      pl.BlockSpec((3, tr, C), lambda i, chip_ref: (0, i, 0)), spec, spec, spec],
            out_specs=[spec] * 4),
        compiler_params=_params(),
    )(chip, own, recv, w, m, v)


def _place():
    return lax.axis_index("x"), lax.axis_index("y"), lax.axis_index("c")


def _gather_steps(x_refs, out_refs, send_sems, recv_sems):
    n = len(x_refs)
    x, y, c = _place()
    me, sibling = (x, y, c), (x, y, 1 - c)
    chips = [(1 - x, y), (x, 1 - y), (1 - x, 1 - y)]

    def rows(a, px, py, pc):
        return out_refs[a].at[4 * px + 2 * py + pc]

    def copy(a, k, block, to, src=None):
        return pltpu.make_async_remote_copy(
            src_ref=rows(a, *block) if src is None else src, dst_ref=rows(a, *block),
            send_sem=send_sems.at[a, k], recv_sem=recv_sems.at[a, k], device_id=to, device_id_type=MESH)

    def first(a):
        return [copy(a, 0, me, sibling, src=x_refs[a])] + [
            copy(a, 1 + j, me, (*chip, c), src=x_refs[a]) for j, chip in enumerate(chips)]

    def passed(a, j):
        return copy(a, 4 + j, (*chips[j], c), sibling)

    def start():
        for a in range(n):
            for cp in first(a):
                cp.start()

    def relay():
        for j, chip in enumerate(chips):
            for a in range(n):
                copy(a, 1 + j, (*chip, c), me).wait_recv()
                passed(a, j).start()

    def finish():
        for a in range(n):
            copy(a, 0, sibling, me).wait_recv()
            for j, chip in enumerate(chips):
                copy(a, 4 + j, (*chip, 1 - c), me).wait_recv()
        for a in range(n):
            for cp in first(a) + [passed(a, j) for j in range(3)]:
                cp.wait_send()

    return start, relay, finish


def _all_gather(arrs, name, own=True):
    n = len(arrs)

    def body(*refs):
        x_refs, out_refs = refs[:n], refs[n:2 * n]
        send_sems, recv_sems, local_sems = refs[2 * n:]
        me = 4 * lax.axis_index("x") + 2 * lax.axis_index("y") + lax.axis_index("c")
        mine = [pltpu.make_async_copy(x_refs[a], out_refs[a].at[me], local_sems.at[a]) for a in range(n)] if own else []
        for cp in mine:
            cp.start()
        for step in _gather_steps(x_refs, out_refs, send_sems, recv_sems):
            step()
        for cp in mine:
            cp.wait()

    return pl.pallas_call(
        body, name=name, out_shape=[jax.ShapeDtypeStruct((N_DEV,) + t.shape, t.dtype) for t in arrs],
        in_specs=[ANY] * n, out_specs=[ANY] * n,
        scratch_shapes=[pltpu.SemaphoreType.DMA((n, 7)), pltpu.SemaphoreType.DMA((n, 7)),
                        pltpu.SemaphoreType.DMA((n,))],
    )(*arrs)


def _pair_exchange(arrs, name):
    n = len(arrs)

    def body(*refs):
        g_refs, out_refs = refs[:n], refs[n:2 * n]
        send_sems, recv_sems = refs[2 * n:]
        x, y, c = _place()
        give = [pltpu.make_async_remote_copy(
            src_ref=g_refs[a].at[pl.ds(0, 4), 1 - c], dst_ref=out_refs[a], send_sem=send_sems.at[a],
            recv_sem=recv_sems.at[a], device_id=(x, y, 1 - c), device_id_type=MESH) for a in range(n)]
        for cp in give:
            cp.start()
        for cp in give:
            cp.wait()

    return pl.pallas_call(
        body, name=name,
        out_shape=[jax.ShapeDtypeStruct((4,) + t.shape[2:], t.dtype) for t in arrs],
        in_specs=[ANY] * n, out_specs=[ANY] * n,
        scratch_shapes=[pltpu.SemaphoreType.DMA((n,)), pltpu.SemaphoreType.DMA((n,))],
    )(*arrs)


def _chip_exchange_steps(p_refs, out_refs, send_sems, recv_sems):
    x, y, c = _place()
    chips = [(1 - x, y), (x, 1 - y), (1 - x, 1 - y)]

    def copies():
        return [pltpu.make_async_remote_copy(
            src_ref=p_refs[a].at[2 * px + py], dst_ref=out_refs[a].at[j], send_sem=send_sems.at[a, j],
            recv_sem=recv_sems.at[a, j], device_id=(px, py, c), device_id_type=MESH)
            for a in range(len(p_refs)) for j, (px, py) in enumerate(chips)]

    def start():
        for cp in copies():
            cp.start()

    def finish():
        for cp in copies():
            cp.wait()

    return start, finish


def _chip_exchange(arrs, name):
    n = len(arrs)

    def body(*refs):
        for step in _chip_exchange_steps(refs[:n], refs[n:2 * n], *refs[2 * n:]):
            step()

    return pl.pallas_call(
        body, name=name, out_shape=[jax.ShapeDtypeStruct((3,) + t.shape[1:], t.dtype) for t in arrs],
        in_specs=[ANY] * n, out_specs=[ANY] * n,
        scratch_shapes=[pltpu.SemaphoreType.DMA((n, 3)), pltpu.SemaphoreType.DMA((n, 3))],
    )(*arrs)


class _Rider:
    def __init__(self, arrays, out_shape, sems, steps, relay_before_end=None):
        self.arrays, self.out_shape, self.scratch, self.steps = list(arrays), out_shape, sems, steps
        self.n = len(self.arrays)
        self.relay_before_end = relay_before_end

    def specs(self, in_specs, out_specs, out_shape, scratch):
        extra = [ANY] * self.n
        return in_specs + extra, out_specs + extra, out_shape + self.out_shape, scratch + self.scratch

    def split(self, refs, n_in, n_out):
        k = self.n
        a, b = n_in + k, n_in + k + n_out
        return refs[:n_in] + refs[a:b] + refs[b + k:-2], (refs[n_in:a], refs[b:b + k], refs[-2:])

    def head(self, mine, step):
        pl.when(step == 0)(self.steps(mine[0], mine[1], *mine[2])[0])

    def tail(self, mine, step, nsteps):
        steps = self.steps(mine[0], mine[1], *mine[2])
        if self.relay_before_end is not None:
            pl.when(step == nsteps - 1 - self.relay_before_end)(steps[1])
        pl.when(step == nsteps - 1)(steps[-1])


def _gather_rider(arrs, relay_before_end=4):
    n = len(arrs)
    return _Rider(arrs, [jax.ShapeDtypeStruct((N_DEV,) + t.shape, t.dtype) for t in arrs],
                  [pltpu.SemaphoreType.DMA((n, 7)), pltpu.SemaphoreType.DMA((n, 7))], _gather_steps,
                  relay_before_end)


def _chip_exchange_rider(arrs):
    n = len(arrs)
    return _Rider(arrs, [jax.ShapeDtypeStruct((3,) + t.shape[1:], t.dtype) for t in arrs],
                  [pltpu.SemaphoreType.DMA((n, 3)), pltpu.SemaphoreType.DMA((n, 3))], _chip_exchange_steps)


def _rope_tables(positions):
    inv_freq = ROPE_THETA ** (-jnp.arange(0, HEAD_DIM, 2, dtype=F32) / HEAD_DIM)
    ang = positions.astype(F32)[:, None] * inv_freq
    cos, sin = jnp.cos(ang), jnp.sin(ang)
    return jnp.tile(cos, (1, 4)), jnp.tile(jnp.concatenate([-sin, sin], axis=1), (1, 2))


def _vec(g, shift, scale, gate):
    z = jnp.zeros_like(g)
    return jnp.stack([g, shift, scale, gate, z, z, z, z])


class _GradReducer:
    def __init__(self):
        self.core = lax.axis_index("c").astype(jnp.int32).reshape(1)
        self.own, self.others, self.waiting, self.riding = {}, {}, [], []

    def pair(self, named):
        keys = list(named)
        mine = [named[k].reshape((4, 2) + named[k].shape[1:]) for k in keys]
        theirs = _pair_exchange(mine, "reduce_pair_" + keys[0])
        for k, g, r in zip(keys, mine, theirs):
            self.own[k] = _sum_pair(self.core, g, r, "sum_pair_" + k)
        self.waiting += keys

    def rider(self):
        self.riding, self.waiting = self.waiting, []
        return _chip_exchange_rider([self.own[k] for k in self.riding])

    def landed(self, results):
        self.others.update(zip(self.riding, results))

    def flush(self, name):
        keys, self.waiting = self.waiting, []
        self.others.update(zip(keys, _chip_exchange([self.own[k] for k in keys], name)))


def _by_owner(g):
    if g.ndim == 3:
        return g if g.shape[0] == N_DEV else g.reshape(N_DEV, g.shape[1] * g.shape[0] // N_DEV, g.shape[2])
    return g.reshape(N_DEV, g.shape[0] // N_DEV, g.shape[1])


def _local_step(x, target, positions, mod, small, W, late=None, red=None):
    S, D = x.shape
    sh1, sc1, gt1, sh2, sc2, gt2, sh3, sc3, gt3 = (mod[k] for k in range(9))
    v1 = _vec(small["g1"], sh1, sc1, gt1)
    v2 = _vec(small["g2"], sh2, sc2, gt2)
    v3 = _vec(small["g3"], sh3, sc3, gt3)
    vf = _vec(small["gf"], small["gf"], small["gf"], small["gf"])
    cos, sin = _rope_tables(positions)
    wbd = jax.scipy.linalg.block_diag(*[small["w_pool"][k] for k in range(4)]).astype(BF16)
    pscale = small["pool_scale"].reshape(1, GW)

    if late is None:
        h1, u1, ab1, act1, f1 = _ffn_fwd(x, v1, W["w1in"], W["w1out"], "ffn1_fwd")
    else:
        (h1, u1, ab1, act1, f1), landed = _ffn_fwd(x, v1, W["w1in"], W["w1out"], "ffn1_fwd", rider=late[0])
        W = {**W, **late[1](landed)}
    u2, p, gates, *qkv = _mix_in_fwd(h1, v2, cos, sin, W["win"], "mix_in_fwd")
    dpool, ypool = _pool_fwd(p, wbd, pscale, "pool_fwd")
    o3, lse3 = [], []
    for gi in range(len(DILATIONS)):
        o, lse = _attn_fwd(qkv[gi], qkv[3 + gi], qkv[6 + gi], f"attn_fwd_{gi}")
        o3.append(o)
        lse3.append(lse)
    h2, ya, merged, tm = _mix_out_fwd(h1, v2, gates, ypool, o3, lse3, W["wpb"], W["wab"], W["wout"], "mix_out_fwd")
    h3, u3, ab3, act3, f3 = _ffn_fwd(h2, v3, W["w2in"], W["w2out"], "ffn2_fwd")
    dh3, loss_blk, dgf = _final_loss(h3, target, vf, "final_loss")

    dh2, dab3, df3, red3 = _ffn_bwd(dh3, h2, f3, ab3, v3, W["w2in"], W["w2out"], "ffn2_bwd")
    half_f = ab3.shape[2] // 2
    G = {"w2in": _by_owner(_wgrad(dab3, u3, "wgrad_2in", tm=half_f)),
         "w2out": _by_owner(_wgrad(act3, df3, "wgrad_2out", tm=half_f))}
    mix_out_args = (dh2, tm, v2, gates, ypool, o3, lse3, W["wpb"], W["wab"], W["wout"], "mix_out_bwd")
    if red is None:
        mix_out = _mix_out_bwd(*mix_out_args)
    else:
        red.pair({k: G[k] for k in ("w2in", "w2out")})
        mix_out, landed = _mix_out_bwd(*mix_out_args, rider=red.rider())
        red.landed(landed)
    (dtm, dgl, dypb, dyab, dyp, do0, do1, do2, c0, c1, c2, red2o) = mix_out
    dq3, dk3, dv3 = [], [], []
    for gi, (do, ct) in enumerate(zip((do0, do1, do2), (c0, c1, c2))):
        dq, dk, dv = _attn_bwd(qkv[gi], qkv[3 + gi], qkv[6 + gi], do, lse3[gi], ct, f"attn_bwd_{gi}")
        dq3.append(dq)
        dk3.append(dk)
        dv3.append(dv)
    dp, dwbd, dps = _pool_bwd(dyp, dpool, wbd, pscale, "pool_bwd")
    dh1, dproj, red2i = _mix_in_bwd(dh2, h1, v2, cos, sin, dp, dq3 + dk3 + dv3, dgl, W["win"], "mix_in_bwd")
    G["win"] = _by_owner(_wgrad(dproj, u2, "wgrad_in", tm=1152))
    G["wpb"] = _full_to_cols(_wgrad(ypool, dypb, "wgrad_pb"))
    G["wab"] = _full_to_cols(_wgrad(ya, dyab, "wgrad_ab"))
    G["wout"] = _by_owner(_wgrad(merged, dtm, "wgrad_out"))
    if red is not None:
        red.pair({k: G[k] for k in ("win", "wpb", "wab", "wout")})
    dx, dab1, df1, red1 = _ffn_bwd(dh1, x, f1, ab1, v1, W["w1in"], W["w1out"], "ffn1_bwd")
    if red is None:
        G["w1in"] = _by_owner(_wgrad(dab1, u1, "wgrad_1in", tm=half_f))
        G["w1out"] = _by_owner(_wgrad(act1, df1, "wgrad_1out", tm=half_f))
    else:
        g1in, landed = _wgrad(dab1, u1, "wgrad_1in", tm=half_f, rider=red.rider())
        red.landed(landed)
        G["w1in"] = _by_owner(g1in)
        red.pair({"w1in": G["w1in"]})
        g1out, landed = _wgrad(act1, df1, "wgrad_1out", tm=half_f, rider=red.rider())
        red.landed(landed)
        G["w1out"] = _by_owner(g1out)
        red.pair({"w1out": G["w1out"]})
        red.flush("reduce_chips_w1out")
    dmod = jnp.stack([red1[1], red1[2], red1[0], red2i[1], red2i[2], red2o[0], red3[1], red3[2], red3[0]])
    dsmall = {
        "g1": red1[3], "g2": red2i[3], "g3": red3[3], "gf": dgf[0],
        "w_pool": jnp.stack([dwbd[k * 64:(k + 1) * 64, k * 64:(k + 1) * 64] for k in range(4)]),
        "pool_scale": dps[0],
    }
    return loss_blk[0, 0], dx, G, dmod, dsmall


SHARDED = ("w_ffn1_in", "w_ffn1_out", "w_in", "w_pool_branch", "w_attn_branch", "w_out", "w_ffn2_in", "w_ffn2_out")
TRANSPOSED = ("w_ffn1_in", "w_in", "w_ffn2_in")
FIRST = ("w_ffn1_in", "w_ffn1_out")
LATER = tuple(n for n in SHARDED if n not in FIRST)
GRAD_KEY = dict(w_ffn1_in="w1in", w_ffn1_out="w1out", w_in="win", w_pool_branch="wpb", w_attn_branch="wab",
                w_out="wout", w_ffn2_in="w2in", w_ffn2_out="w2out")


def _cols_to_full(g):
    return jnp.concatenate([g[j] for j in range(N_DEV)], axis=1)


def _full_to_cols(t):
    c = t.shape[1] // N_DEV
    return jnp.stack([t[:, j * c:(j + 1) * c] for j in range(N_DEV)])


SMALL =(("b_ada", 9216), ("g_norm_ffn1", 1024), ("g_norm_mix", 1024), ("g_norm_ffn2", 1024), ("g_final", 1024),
         ("w_pool", 16384), ("pool_scale", 256))
SMALL_ROWS = 240


def _pack_small(vals, loss=None):
    flat = jnp.concatenate([vals[name].reshape(-1) for name, _ in SMALL])
    tail = jnp.zeros((SMALL_ROWS * 128 - flat.shape[0],), F32)
    if loss is not None:
        tail = tail.at[0].set(loss)
    return jnp.concatenate([flat, tail]).reshape(SMALL_ROWS, 128)


def _unpack_small(slab, shapes):
    flat, out, off = slab.reshape(-1), {}, 0
    for name, n in SMALL:
        out[name] = flat[off:off + n].reshape(shapes[name])
        off += n
    return out, flat[off]


def kernel(x, c, positions, w_ada, b_ada, g_norm_ffn1, w_ffn1_in, w_ffn1_out, g_norm_mix, w_in, w_pool, pool_scale, w_pool_branch, w_attn_branch, w_out, g_norm_ffn2, w_ffn2_in, w_ffn2_out, g_final, loss_target, m_w_ada, m_b_ada, m_g_norm_ffn1, m_w_ffn1_in, m_w_ffn1_out, m_g_norm_mix, m_w_in, m_w_pool, m_pool_scale, m_w_pool_branch, m_w_attn_branch, m_w_out, m_g_norm_ffn2, m_w_ffn2_in, m_w_ffn2_out, m_g_final, v_w_ada, v_b_ada, v_g_norm_ffn1, v_w_ffn1_in, v_w_ffn1_out, v_g_norm_mix, v_w_in, v_w_pool, v_pool_scale, v_w_pool_branch, v_w_attn_branch, v_w_out, v_g_norm_ffn2, v_w_ffn2_in, v_w_ffn2_out, v_g_final):
    names = ["w_ada", "b_ada", "g_norm_ffn1", "w_ffn1_in", "w_ffn1_out", "g_norm_mix", "w_in", "w_pool", "pool_scale",
             "w_pool_branch", "w_attn_branch", "w_out", "g_norm_ffn2", "w_ffn2_in", "w_ffn2_out", "g_final"]
    w = dict(w_ada=w_ada, b_ada=b_ada, g_norm_ffn1=g_norm_ffn1, w_ffn1_in=w_ffn1_in, w_ffn1_out=w_ffn1_out,
             g_norm_mix=g_norm_mix, w_in=w_in, w_pool=w_pool, pool_scale=pool_scale, w_pool_branch=w_pool_branch,
             w_attn_branch=w_attn_branch, w_out=w_out, g_norm_ffn2=g_norm_ffn2, w_ffn2_in=w_ffn2_in,
             w_ffn2_out=w_ffn2_out, g_final=g_final)
    m = dict(w_ada=m_w_ada, b_ada=m_b_ada, g_norm_ffn1=m_g_norm_ffn1, w_ffn1_in=m_w_ffn1_in, w_ffn1_out=m_w_ffn1_out,
             g_norm_mix=m_g_norm_mix, w_in=m_w_in, w_pool=m_w_pool, pool_scale=m_pool_scale,
             w_pool_branch=m_w_pool_branch, w_attn_branch=m_w_attn_branch, w_out=m_w_out, g_norm_ffn2=m_g_norm_ffn2,
             w_ffn2_in=m_w_ffn2_in, w_ffn2_out=m_w_ffn2_out, g_final=m_g_final)
    v = dict(w_ada=v_w_ada, b_ada=v_b_ada, g_norm_ffn1=v_g_norm_ffn1, w_ffn1_in=v_w_ffn1_in, w_ffn1_out=v_w_ffn1_out,
             g_norm_mix=v_g_norm_mix, w_in=v_w_in, w_pool=v_w_pool, pool_scale=v_pool_scale,
             w_pool_branch=v_w_pool_branch, w_attn_branch=v_w_attn_branch, w_out=v_w_out, g_norm_ffn2=v_g_norm_ffn2,
             w_ffn2_in=v_w_ffn2_in, w_ffn2_out=v_w_ffn2_out, g_final=v_g_final)
    shapes = {n: w[n].shape for n in names}
    me = 4 * lax.axis_index("x") + 2 * lax.axis_index("y") + lax.axis_index("c")
    D = x.shape[-1]
    n_mod = w_ada.shape[-1] * N_DEV // D

    (c_all,) = _all_gather([c.reshape(D // 128, 128)], "gather_c")
    ada_cols = w_ada.shape[-1]
    b_mine = lax.dynamic_slice_in_dim(b_ada, me * ada_cols, ada_cols, axis=1)
    cond, mod_part = _ada_mod(c_all.reshape(N_DEV, D), w_ada[0], b_mine, "ada_mod")
    (mod_all,) = _all_gather([mod_part.reshape(-1, 128)], "gather_mod")
    mod_all = mod_all.reshape(N_DEV, N_DEV, ada_cols)
    mod = lax.dynamic_index_in_dim(mod_all, me, axis=1, keepdims=False).reshape(n_mod, D)

    def local(t, name):
        return t[name][0].T if name in TRANSPOSED else t[name][0]

    shards = {name: local(w, name).astype(BF16) for name in SHARDED}

    def gather_done(names, fulls):
        return {name: lax.dynamic_update_index_in_dim(full, shards[name], me, axis=0)
                for name, full in zip(names, fulls)}

    def ffn_weights(g, pre):
        return {"w%sin" % pre: g["w_ffn%s_in" % pre].reshape(2, -1, D),
                "w%sout" % pre: g["w_ffn%s_out" % pre].reshape(-1, D)}

    def later_weights(fulls):
        g = gather_done(LATER, fulls)
        return dict(win=g["w_in"].reshape(-1, D), wpb=_cols_to_full(g["w_pool_branch"]),
                    wab=_cols_to_full(g["w_attn_branch"]), wout=g["w_out"].reshape(D, D), **ffn_weights(g, "2"))

    W = ffn_weights(gather_done(FIRST, _all_gather([shards[n] for n in FIRST], "gather_ffn1", own=False)), "1")
    small = dict(g1=g_norm_ffn1[0], g2=g_norm_mix[0], g3=g_norm_ffn2[0], gf=g_final, w_pool=w_pool[0],
                 pool_scale=pool_scale[0])
    red = _GradReducer()
    loss_part, dx, G, dmod, dsmall = _local_step(
        x[0], loss_target[0], positions[0], mod, small, W,
        late=(_gather_rider([shards[n] for n in LATER]), later_weights), red=red)
    chip = (2 * lax.axis_index("x") + lax.axis_index("y")).astype(jnp.int32).reshape(1)

    part = _pack_small(dict(b_ada=dmod, g_norm_ffn1=dsmall["g1"], g_norm_mix=dsmall["g2"], g_norm_ffn2=dsmall["g3"],
                            g_final=dsmall["gf"], w_pool=dsmall["w_pool"], pool_scale=dsmall["pool_scale"]),
                       loss=loss_part)
    (parts,) = _all_gather([part], "gather_small")
    gsmall, loss = _unpack_small(_sum_slots(parts, "sum_small"), shapes)
    rows_mine = ada_cols // 128
    dmod_mine = lax.dynamic_slice_in_dim(parts, me * rows_mine, rows_mine, axis=1).reshape(N_DEV, ada_cols)

    grads, delta, new_m, new_v = {}, {}, {}, {}
    grads["w_ada"], delta["w_ada"], new_m["w_ada"], new_v["w_ada"] = (
        t[None] for t in _ada_grad_adamw(cond.T, dmod_mine, w_ada[0], m_w_ada[0], v_w_ada[0], "ada_grad_adamw"))
    for name in SHARDED:
        key = GRAD_KEY[name]
        res = _sum_adamw(chip, red.own[key], red.others[key], local(w, name), local(m, name), local(v, name),
                         "adamw_" + name)
        grads[name], delta[name], new_m[name], new_v[name] = (
            (t.T if name in TRANSPOSED else t)[None] for t in res)
    sd, sm, sv = _adamw(_pack_small(w), _pack_small(gsmall), _pack_small(m), _pack_small(v), "adamw_small")
    for dst, src in ((delta, sd), (new_m, sm), (new_v, sv)):
        dst.update(_unpack_small(src, shapes)[0])
    grads.update(gsmall)

    return (loss, dx[None], *[grads[n] for n in names], *[delta[n] for n in names],
            *[new_m[n] for n in names], *[new_v[n] for n in names])
```

```python
import functools

import jax
import jax.numpy as jnp
from jax import lax
from jax.experimental import pallas as pl
from jax.experimental.pallas import tpu as pltpu

F32 = jnp.float32
BF16 = jnp.bfloat16
MESH = pl.DeviceIdType.MESH
ANY = pl.BlockSpec(memory_space=pl.ANY)

N_DEV = 8
EPS = 1e-6
HEAD_DIM = 64
HEADS = 4
GW = HEADS * HEAD_DIM
DILATIONS = (1, 4, 16)
BAND = 128
QB = 128
POOL_WINDOWS = (2, 4, 8, 16)
HALO = 16
ROPE_THETA = 10000.0

ADAM_LR = 0.001
ADAM_B1 = 0.9
ADAM_B2 = 0.999
ADAM_EPS = 1e-08
ADAM_WD = 0.01
ADAM_STEP = 10

VMEM_LIMIT = 56 * 1024 * 1024
TS = 512
FFN_TS = 256
FC = 2816

NT = (((1,), (1,)), ((), ()))
TN = (((0,), (0,)), ((), ()))


def _params(**kw):
    return pltpu.CompilerParams(vmem_limit_bytes=VMEM_LIMIT, **kw)


def _dot(a, b):
    return jnp.dot(a, b, preferred_element_type=F32)


def _dot_nt(a, b):
    return lax.dot_general(a, b, NT, preferred_element_type=F32)


def _dot_tn(a, b):
    return lax.dot_general(a, b, TN, preferred_element_type=F32)


def _load_weights(pairs, sem):
    @pl.when(pl.program_id(0) == 0)
    def _():
        copies = [pltpu.make_async_copy(src, dst, sem.at[i]) for i, (src, dst) in enumerate(pairs)]
        for cp in copies:
            cp.start()
        for cp in copies:
            cp.wait()


def _norm_mod(x, g, sc, sh):
    r = lax.rsqrt(jnp.mean(x * x, axis=-1, keepdims=True) + EPS)
    xn = x * r
    y = xn * g
    return r, xn, y, y * (1.0 + sc) + sh


def _norm_mod_bwd(du, r, xn, y, g, sc):
    dsh = jnp.sum(du, axis=0, keepdims=True)
    dsc = jnp.sum(du * y, axis=0, keepdims=True)
    dy = du * (1.0 + sc)
    dg = jnp.sum(dy * xn, axis=0, keepdims=True)
    dxn = dy * g
    dx = r * (dxn - xn * jnp.mean(dxn * xn, axis=-1, keepdims=True))
    return dx, dsh, dsc, dg


def _row_tile(ts, width):
    return pl.BlockSpec((ts, width), lambda i: (i, 0))


def _const(shape):
    return pl.BlockSpec(shape, lambda *_: (0,) * len(shape))


def _ffn_fwd(h, vec, win, wout, name, rider=None):
    TS = FFN_TS
    S, D = h.shape
    _, Fd, _ = win.shape
    nch = Fd // FC

    def body(*refs):
        if rider is None:
            return compute(*refs)
        host, mine = rider.split(refs, 4, 5)
        rider.head(mine, pl.program_id(0))
        compute(*host)
        rider.tail(mine, pl.program_id(0), S // TS)

    def compute(h_ref, vec_ref, win_hbm, wout_hbm, hn_ref, u_ref, ab_ref, act_ref, f_ref, win_v, wout_v, sem):
        _load_weights([(win_hbm, win_v), (wout_hbm, wout_v)], sem)
        x = h_ref[...]
        g, sh, sc, gt = (vec_ref[k:k + 1, :] for k in range(4))
        _, _, _, u = _norm_mod(x, g, sc, sh)
        ub = u.astype(BF16)
        u_ref[...] = ub
        acc = jnp.zeros((TS, D), F32)
        for j in range(nch):
            sl = slice(j * FC, (j + 1) * FC)
            a = _dot_nt(ub, win_v[0, sl, :])
            b = _dot_nt(ub, win_v[1, sl, :])
            act = ((a * jax.nn.sigmoid(a)) * b).astype(BF16)
            ab_ref[0, :, sl] = a.astype(BF16)
            ab_ref[1, :, sl] = b.astype(BF16)
            act_ref[:, sl] = act
            acc = acc + _dot(act, wout_v[sl, :])
        f_ref[...] = acc
        hn_ref[...] = x + (0.5 * gt) * acc

    specs = (
        [_row_tile(TS, D), _const((8, D)), ANY, ANY],
        [_row_tile(TS, D), _row_tile(TS, D), pl.BlockSpec((2, TS, Fd), lambda i: (0, i, 0)),
         _row_tile(TS, Fd), _row_tile(TS, D)],
        [jax.ShapeDtypeStruct((S, D), F32), jax.ShapeDtypeStruct((S, D), BF16),
         jax.ShapeDtypeStruct((2, S, Fd), BF16), jax.ShapeDtypeStruct((S, Fd), BF16),
         jax.ShapeDtypeStruct((S, D), F32)],
        [pltpu.VMEM(win.shape, BF16), pltpu.VMEM(wout.shape, BF16), pltpu.SemaphoreType.DMA((2,))])
    in_specs, out_specs, out_shape, scratch = specs if rider is None else rider.specs(*specs)
    outs = pl.pallas_call(
        body, name=name, grid=(S // TS,), in_specs=in_specs, out_specs=out_specs, out_shape=out_shape,
        scratch_shapes=scratch, compiler_params=_params(),
    )(h, vec, win, wout, *(rider.arrays if rider else []))
    return outs if rider is None else (outs[:5], outs[5:])


def _ffn_bwd(dh, h, f, ab, vec, win, wout, name):
    TS = FFN_TS
    S, D = h.shape
    _, Fd, _ = win.shape
    nch = Fd // FC

    def body(dh_ref, h_ref, f_ref, ab_ref, vec_ref, win_hbm, wout_hbm,
             dhp_ref, dab_ref, df_ref, red_ref, win_v, wout_v, sem):
        _load_weights([(win_hbm, win_v), (wout_hbm, wout_v)], sem)

        @pl.when(pl.program_id(0) == 0)
        def _():
            red_ref[...] = jnp.zeros_like(red_ref)

        dh_v = dh_ref[...]
        x = h_ref[...]
        g, sh, sc, gt = (vec_ref[k:k + 1, :] for k in range(4))
        dgt = jnp.sum((0.5 * f_ref[...]) * dh_v, axis=0, keepdims=True)
        dfb = ((0.5 * gt) * dh_v).astype(BF16)
        df_ref[...] = dfb
        du = jnp.zeros((TS, D), F32)
        for j in range(nch):
            sl = slice(j * FC, (j + 1) * FC)
            dact = _dot_nt(dfb, wout_v[sl, :])
            av = ab_ref[0, :, sl].astype(F32)
            bv = ab_ref[1, :, sl].astype(F32)
            sg = jax.nn.sigmoid(av)
            da = (dact * bv * (sg * (1.0 + av * (1.0 - sg)))).astype(BF16)
            db = (dact * (av * sg)).astype(BF16)
            dab_ref[0, :, sl] = da
            dab_ref[1, :, sl] = db
            du = du + _dot(da, win_v[0, sl, :]) + _dot(db, win_v[1, sl, :])
        r, xn, y, _ = _norm_mod(x, g, sc, sh)
        dx, dsh, dsc, dg = _norm_mod_bwd(du, r, xn, y, g, sc)
        dhp_ref[...] = dh_v + dx
        red_ref[0:1, :] += dgt
        red_ref[1:2, :] += dsh
        red_ref[2:3, :] += dsc
        red_ref[3:4, :] += dg

    ab_spec = pl.BlockSpec((2, TS, Fd), lambda i: (0, i, 0))
    return pl.pallas_call(
        body, name=name, grid=(S // TS,),
        in_specs=[_row_tile(TS, D), _row_tile(TS, D), _row_tile(TS, D), ab_spec, _const((8, D)), ANY, ANY],
        out_specs=[_row_tile(TS, D), ab_spec, _row_tile(TS, D), _const((8, D))],
        out_shape=[jax.ShapeDtypeStruct((S, D), F32), jax.ShapeDtypeStruct((2, S, Fd), BF16),
                   jax.ShapeDtypeStruct((S, D), BF16), jax.ShapeDtypeStruct((8, D), F32)],
        scratch_shapes=[pltpu.VMEM(win.shape, BF16), pltpu.VMEM(wout.shape, BF16), pltpu.SemaphoreType.DMA((2,))],
        compiler_params=_params(),
    )(dh, h, f, ab, vec, win, wout)


def _wgrad(x, y, name, tm=None, ts=2048, rider=None):
    xb = x.ndim == 3
    nb = x.shape[0] if xb else 0
    S, M = x.shape[-2:]
    N = y.shape[-1]
    tm = tm or M
    ts = min(ts, S)
    nk = S // ts
    grid = (max(nb, 1), M // tm, nk)

    def body(*refs):
        if rider is None:
            return compute(*refs)
        host, mine = rider.split(refs, 2, 1)
        step = (pl.program_id(0) * grid[1] + pl.program_id(1)) * grid[2] + pl.program_id(2)
        rider.head(mine, step)
        compute(*host)
        rider.tail(mine, step, grid[0] * grid[1] * grid[2])

    def compute(x_ref, y_ref, o_ref, acc):
        k = pl.program_id(2)

        @pl.when(k == 0)
        def _():
            acc[...] = jnp.zeros_like(acc)

        acc[...] += _dot_tn(x_ref[...], y_ref[...])

        @pl.when(k == nk - 1)
        def _():
            o_ref[...] = acc[...].astype(BF16)

    x_spec = (pl.BlockSpec((None, ts, tm), lambda b, i, k: (b, k, i)) if xb
              else pl.BlockSpec((ts, tm), lambda b, i, k: (k, i)))
    y_spec = pl.BlockSpec((ts, N), lambda b, i, k: (k, 0))
    if xb:
        o_spec, o_shape = pl.BlockSpec((None, tm, N), lambda b, i, k: (b, i, 0)), (nb, M, N)
    else:
        o_spec, o_shape = pl.BlockSpec((tm, N), lambda b, i, k: (i, 0)), (M, N)
    specs = ([x_spec, y_spec], [o_spec], [jax.ShapeDtypeStruct(o_shape, BF16)], [pltpu.VMEM((tm, N), F32)])
    in_specs, out_specs, out_shape, scratch = specs if rider is None else rider.specs(*specs)
    outs = pl.pallas_call(
        body, name=name, grid=grid, in_specs=in_specs, out_specs=out_specs, out_shape=out_shape,
        scratch_shapes=scratch, compiler_params=_params(),
    )(x, y, *(rider.arrays if rider else []))
    return outs[0] if rider is None else (outs[0], outs[1:])


P_OFF, Q_OFF, K_OFF, V_OFF, G_OFF = 0, 256, 1024, 1792, 2560
IN_WIDTH = 4608


def _first_half_mask(ts):
    lane = lax.broadcasted_iota(jnp.int32, (ts, 128), 1)
    return (lane % HEAD_DIM) < (HEAD_DIM // 2)


def _rope(t, cos, sin_signed, first, sign):
    partner = jnp.where(first, pltpu.roll(t, 96, 1), pltpu.roll(t, 32, 1))
    return t * cos + sign * (partner * sin_signed)


def _res_spec(r):
    return pl.BlockSpec((r, TS // r, GW), lambda i: (0, i, 0))


def _res_shape(S, r, dtype):
    return jax.ShapeDtypeStruct((r, S // r, GW), dtype)


def _to_residues(piece, out_ref, lanes, r, scr):
    if r == 1:
        out_ref[0, :, lanes] = piece.astype(out_ref.dtype)
        return
    for h in range(piece.shape[1] // 128):
        scr[h] = piece[:, h * 128:(h + 1) * 128]
        at = slice(lanes.start + h * 128, lanes.start + (h + 1) * 128)
        for res in range(r):
            out_ref[res, :, at] = scr[h, pl.ds(res, TS // r, stride=r), :].astype(out_ref.dtype)


def _from_residues(in_ref, lanes, r, scr):
    if r == 1:
        return in_ref[0, :, lanes].astype(F32)
    halves = (lanes.stop - lanes.start) // 128
    for h in range(halves):
        at = slice(lanes.start + h * 128, lanes.start + (h + 1) * 128)
        for res in range(r):
            scr[h, pl.ds(res, TS // r, stride=r), :] = in_ref[res, :, at].astype(F32)
    return scr[0] if halves == 1 else jnp.concatenate([scr[0], scr[1]], axis=1)


RES_SCRATCH = (2, TS, 128)


def _mix_in_fwd(h, vec, cos, sin, win, name):
    S, D = h.shape

    def body(h_ref, vec_ref, cos_ref, sin_ref, win_hbm, u_ref, p_ref, gates_ref, *rest):
        qkv_refs, (win_v, sem, scr) = rest[:9], rest[9:]
        _load_weights([(win_hbm, win_v)], sem)
        g, sh, sc = (vec_ref[k:k + 1, :] for k in range(3))
        _, _, _, u = _norm_mod(h_ref[...], g, sc, sh)
        ub = u.astype(BF16)
        u_ref[...] = ub
        p_ref[...] = _dot_nt(ub, win_v[P_OFF:Q_OFF, :])
        cosv, sinv = cos_ref[...], sin_ref[...]
        first = _first_half_mask(TS)
        for which, off in enumerate((Q_OFF, K_OFF, V_OFF)):
            t = _dot_nt(ub, win_v[off:off + 3 * GW, :])
            for gi in range(3):
                for half in range(2):
                    c0 = gi * GW + half * 128
                    piece = t[:, c0:c0 + 128]
                    if which < 2:
                        piece = _rope(piece, cosv, sinv, first, 1.0)
                    _to_residues(piece, qkv_refs[which * 3 + gi], slice(half * 128, (half + 1) * 128),
                                 DILATIONS[gi], scr)
        gates_ref[...] = jax.nn.sigmoid(_dot_nt(ub, win_v[G_OFF:IN_WIDTH, :])).astype(BF16)

    return pl.pallas_call(
        body, name=name, grid=(S // TS,),
        in_specs=[_row_tile(TS, D), _const((8, D)), _row_tile(TS, 128), _row_tile(TS, 128), ANY],
        out_specs=[_row_tile(TS, D), _row_tile(TS, GW), _row_tile(TS, 2 * D)] + [_res_spec(r) for r in DILATIONS] * 3,
        out_shape=[jax.ShapeDtypeStruct((S, D), BF16), jax.ShapeDtypeStruct((S, GW), F32),
                   jax.ShapeDtypeStruct((S, 2 * D), BF16)] + [_res_shape(S, r, BF16) for r in DILATIONS] * 3,
        scratch_shapes=[pltpu.VMEM((IN_WIDTH, D), BF16), pltpu.SemaphoreType.DMA((1,)), pltpu.VMEM(RES_SCRATCH, F32)],
        compiler_params=_params(),
    )(h, vec, cos, sin, win)


def _mix_in_bwd(dh, h, vec, cos, sin, dp, dqkv, dgl, win, name):
    S, D = h.shape

    def body(dh_ref, h_ref, vec_ref, cos_ref, sin_ref, dp_ref, *rest):
        dqkv_refs = rest[:9]
        dgl_ref, win_hbm, dhp_ref, dproj_ref, red_ref, win_v, sem, scr = rest[9:]
        _load_weights([(win_hbm, win_v)], sem)

        @pl.when(pl.program_id(0) == 0)
        def _():
            red_ref[...] = jnp.zeros_like(red_ref)

        cosv, sinv = cos_ref[...], sin_ref[...]
        first = _first_half_mask(TS)
        dproj_ref[:, P_OFF:Q_OFF] = dp_ref[...].astype(BF16)
        for which, off in enumerate((Q_OFF, K_OFF, V_OFF)):
            for gi in range(3):
                for half in range(2):
                    piece = _from_residues(dqkv_refs[which * 3 + gi], slice(half * 128, (half + 1) * 128),
                                           DILATIONS[gi], scr)
                    if which < 2:
                        piece = _rope(piece, cosv, sinv, first, -1.0)
                    c0 = off + gi * GW + half * 128
                    dproj_ref[:, c0:c0 + 128] = piece.astype(BF16)
        dproj_ref[:, G_OFF:IN_WIDTH] = dgl_ref[...]
        du = _dot(dproj_ref[...], win_v[...])
        g, sh, sc = (vec_ref[k:k + 1, :] for k in range(3))
        r, xn, y, _ = _norm_mod(h_ref[...], g, sc, sh)
        dx, dsh, dsc, dg = _norm_mod_bwd(du, r, xn, y, g, sc)
        dhp_ref[...] = dh_ref[...] + dx
        red_ref[1:2, :] += dsh
        red_ref[2:3, :] += dsc
        red_ref[3:4, :] += dg

    return pl.pallas_call(
        body, name=name, grid=(S // TS,),
        in_specs=[_row_tile(TS, D), _row_tile(TS, D), _const((8, D)), _row_tile(TS, 128), _row_tile(TS, 128),
                  _row_tile(TS, GW)] + [_res_spec(r) for r in DILATIONS] * 3 + [_row_tile(TS, 2 * D), ANY],
        out_specs=[_row_tile(TS, D), _row_tile(TS, IN_WIDTH), _const((8, D))],
        out_shape=[jax.ShapeDtypeStruct((S, D), F32), jax.ShapeDtypeStruct((S, IN_WIDTH), BF16),
                   jax.ShapeDtypeStruct((8, D), F32)],
        scratch_shapes=[pltpu.VMEM((IN_WIDTH, D), BF16), pltpu.SemaphoreType.DMA((1,)), pltpu.VMEM(RES_SCRATCH, F32)],
        compiler_params=_params(),
    )(dh, h, vec, cos, sin, dp, *dqkv, dgl, win)


def _pool_lanes(rows):
    lane = lax.broadcasted_iota(jnp.int32, (rows, GW), 1)
    return lane // HEAD_DIM


def _pool_window(rows):
    grp = _pool_lanes(rows)
    w = jnp.full((rows, GW), POOL_WINDOWS[0], jnp.int32)
    for k in range(1, len(POOL_WINDOWS)):
        w = jnp.where(grp == k, POOL_WINDOWS[k], w)
    return grp, w


def _pool_fwd(p, wbd, scale, name, ts=512):
    S = p.shape[0]
    ext = ts + HALO

    def body(pc_ref, ph_ref, wbd_ref, sc_ref, d_ref, y_ref):
        i = pl.program_id(0)
        cur = pc_ref[...]
        halo = jnp.where(i > 0, ph_ref[...], 0.0)
        s = jnp.concatenate([halo, cur], axis=0)
        grp, w = _pool_window(ext)
        sel = jnp.zeros((ext, GW), F32)
        for k, wk in enumerate(POOL_WINDOWS):
            s = s + pltpu.roll(s, wk // 2, 0)
            sel = jnp.where(grp == k, s, sel)
        t = i * ts + lax.broadcasted_iota(jnp.int32, (ts, GW), 0)
        count = jnp.minimum(t + 1, w[HALO:]).astype(F32)
        d = (sel[HALO:] / count - cur).astype(BF16)
        d_ref[...] = d
        y_ref[...] = (_dot(d, wbd_ref[...]) * sc_ref[...]).astype(BF16)

    return pl.pallas_call(
        body, name=name, grid=(S // ts,),
        in_specs=[_row_tile(ts, GW),
                  pl.BlockSpec((HALO, GW), lambda i: (jnp.maximum(i * (ts // HALO) - 1, 0), 0)),
                  _const((GW, GW)), _const((1, GW))],
        out_specs=[_row_tile(ts, GW), _row_tile(ts, GW)],
        out_shape=[jax.ShapeDtypeStruct((S, GW), BF16), jax.ShapeDtypeStruct((S, GW), BF16)],
        compiler_params=_params(),
    )(p, p, wbd, scale)


def _pool_bwd(dy, d, wbd, scale, name, ts=512):
    S = dy.shape[0]
    ext = ts + HALO
    nsteps = S // ts
    last_halo = S // HALO - 1

    def body(dyc_ref, dyh_ref, d_ref, wbd_ref, sc_ref, dp_ref, dw_ref, ds_ref):
        i = pl.program_id(0)

        @pl.when(i == 0)
        def _():
            dw_ref[...] = jnp.zeros_like(dw_ref)
            ds_ref[...] = jnp.zeros_like(ds_ref)

        dyc = dyc_ref[...]
        dyh = jnp.where(i < nsteps - 1, dyh_ref[...], 0.0)
        dys = (jnp.concatenate([dyc, dyh], axis=0) * sc_ref[...]).astype(BF16)
        dd = _dot_nt(dys, wbd_ref[...])
        grp, w = _pool_window(ext)
        t = i * ts + lax.broadcasted_iota(jnp.int32, (ext, GW), 0)
        s = dd / jnp.minimum(t + 1, w).astype(F32)
        sel = jnp.zeros((ext, GW), F32)
        for k, wk in enumerate(POOL_WINDOWS):
            s = s + pltpu.roll(s, ext - wk // 2, 0)
            sel = jnp.where(grp == k, s, sel)
        dp_ref[...] = sel[:ts] - dd[:ts]
        dv = d_ref[...]
        z = _dot(dv, wbd_ref[...])
        ds_ref[0:1, :] += jnp.sum(dyc * z, axis=0, keepdims=True)
        dw_ref[...] += _dot_tn(dv, dys[:ts])

    return pl.pallas_call(
        body, name=name, grid=(nsteps,),
        in_specs=[_row_tile(ts, GW),
                  pl.BlockSpec((HALO, GW), lambda i: (jnp.minimum((i + 1) * (ts // HALO), last_halo), 0)),
                  _row_tile(ts, GW), _const((GW, GW)), _const((1, GW))],
        out_specs=[_row_tile(ts, GW), _const((GW, GW)), _const((8, GW))],
        out_shape=[jax.ShapeDtypeStruct((S, GW), F32), jax.ShapeDtypeStruct((GW, GW), F32),
                   jax.ShapeDtypeStruct((8, GW), F32)],
        compiler_params=_params(),
    )(dy, dy, d, wbd, scale)


def _head_id(rows):
    return lax.broadcasted_iota(jnp.int32, (rows, GW), 1) // HEAD_DIM


def _stack_heads(t, hid):
    return jnp.concatenate([jnp.where(hid == h, t, jnp.zeros_like(t)) for h in range(HEADS)], axis=0)


def _unstack_heads(t_all, hid):
    out = jnp.zeros((QB, GW), F32)
    for h in range(HEADS):
        out = jnp.where(hid == h, t_all[h * QB:(h + 1) * QB], out)
    return out


def _band_mask(n):
    row = lax.broadcasted_iota(jnp.int32, (HEADS * QB, 2 * QB), 0) % QB
    col = lax.broadcasted_iota(jnp.int32, (HEADS * QB, 2 * QB), 1)
    rel = row + QB - col
    return (rel >= 0) & (rel <= BAND) & ((col >= QB) | (n > 0))


N_STREAMS = 4


def _streams(r, nb):
    if r >= N_STREAMS:
        return nb, [(lambda rb, l=l: N_STREAMS * rb + l, 0) for l in range(N_STREAMS)]
    assert r == 1
    return nb // N_STREAMS, [(lambda rb: 0, l * (nb // N_STREAMS)) for l in range(N_STREAMS)]


def _attn_fwd(q, k, v, name):
    r, L, _ = q.shape
    nbs, streams = _streams(r, L // QB)
    ns = len(streams)
    grid = (max(r // ns, 1), nbs)

    def cur(res, off):
        return pl.BlockSpec((None, QB, GW), lambda rb, n: (res(rb), n + off, 0))

    def prev(res, off):
        return pl.BlockSpec((None, QB, GW), lambda rb, n: (res(rb), jnp.maximum(n + off - 1, 0), 0))

    def body(*refs):
        n = pl.program_id(1)
        hid = _head_id(QB)
        o_ref, lse_ref = refs[5 * len(streams):]
        for l, (_, off) in enumerate(streams):
            q_ref, kp_ref, kc_ref, vp_ref, vc_ref = refs[5 * l:5 * l + 5]
            qs = _stack_heads(q_ref[...], hid)
            kc = jnp.concatenate([kp_ref[...], kc_ref[...]], axis=0)
            vc = jnp.concatenate([vp_ref[...], vc_ref[...]], axis=0)
            s = _dot_nt(qs, kc) * (HEAD_DIM ** -0.5)
            s = jnp.where(_band_mask(n + off), s, -jnp.inf)
            m = jnp.max(s, axis=-1, keepdims=True)
            e = jnp.exp(s - m)
            den = jnp.sum(e, axis=-1, keepdims=True)
            lse = m + jnp.log(den)
            pr = (e * (1.0 / den)).astype(BF16)
            o_ref[l] = _unstack_heads(_dot(pr, vc), hid)
            lse_ref[l] = _unstack_heads(jnp.broadcast_to(lse, (HEADS * QB, GW)), hid)

    in_specs, args = [], []
    for res, off in streams:
        in_specs += [cur(res, off), prev(res, off), cur(res, off), prev(res, off), cur(res, off)]
        args += [q, k, k, v, v]
    out = jax.ShapeDtypeStruct((ns * grid[0], nbs * QB, GW), F32)
    both = pl.BlockSpec((ns, QB, GW), lambda rb, n: (rb, n, 0))
    o, lse = pl.pallas_call(
        body, name=name, grid=grid, in_specs=in_specs, out_specs=[both, both], out_shape=[out, out],
        compiler_params=_params(),
    )(*args)
    return o.reshape(q.shape), lse.reshape(q.shape)


def _head_rows(t_full, hid):
    return jnp.concatenate(
        [jnp.max(jnp.where(hid == h, t_full, -jnp.inf), axis=-1, keepdims=True) for h in range(HEADS)], axis=0)


def _attn_bwd(q, k, v, do, lse, cterm, name):
    r, L, _ = q.shape
    nbs, streams = _streams(r, L // QB)
    ns = len(streams)
    parts = r == 1

    def spec(res, index):
        return pl.BlockSpec((None, QB, GW), lambda rb, n: (res(rb), index(n), 0))

    def body(*refs):
        dq_ref, dk_ref, dv_ref, carry_k, carry_v, seam_k, seam_v = refs[8 * ns:]
        n = pl.program_id(1)

        @pl.when(n == 0)
        def _():
            carry_k[...] = jnp.zeros_like(carry_k)
            carry_v[...] = jnp.zeros_like(carry_v)

        @pl.when(n < nbs)
        def _():
            hid = _head_id(QB)
            for l, (_, off) in enumerate(streams):
                q_ref, do_ref, lse_ref, c_ref, kp_ref, kc_ref, vp_ref, vc_ref = refs[8 * l:8 * l + 8]
                qs = _stack_heads(q_ref[...], hid)
                dos = _stack_heads(do_ref[...], hid)
                kc = jnp.concatenate([kp_ref[...], kc_ref[...]], axis=0)
                vc = jnp.concatenate([vp_ref[...], vc_ref[...]], axis=0)
                s = _dot_nt(qs, kc) * (HEAD_DIM ** -0.5)
                s = jnp.where(_band_mask(n + off), s, -jnp.inf)
                p = jnp.exp(s - _head_rows(lse_ref[...], hid))
                dp = _dot_nt(dos, vc)
                ds = (p * (dp + _head_rows(c_ref[...], hid)) * (HEAD_DIM ** -0.5)).astype(BF16)
                dq_ref[l] = _unstack_heads(_dot(ds, kc), hid)
                dkc = _dot_tn(ds, qs)
                dvc = _dot_tn(p.astype(BF16), dos)
                if parts and l > 0:
                    @pl.when(n == 0)
                    def _():
                        seam_k[l] = dkc[:QB]
                        seam_v[l] = dvc[:QB]
                dk_ref[l] = carry_k[l] + dkc[:QB]
                dv_ref[l] = carry_v[l] + dvc[:QB]
                carry_k[l] = dkc[QB:]
                carry_v[l] = dvc[QB:]

        @pl.when(n == nbs)
        def _():
            for l in range(ns):
                if parts and l + 1 < ns:
                    dk_ref[l] = carry_k[l] + seam_k[l + 1]
                    dv_ref[l] = carry_v[l] + seam_v[l + 1]
                else:
                    dk_ref[l] = carry_k[l]
                    dv_ref[l] = carry_v[l]

    in_specs, args = [], []
    for res, off in streams:
        qside = functools.partial(lambda n, off: jnp.minimum(n, nbs - 1) + off, off=off)
        kprev = functools.partial(lambda n, off: jnp.maximum(jnp.minimum(n, nbs) - 1 + off, 0), off=off)
        in_specs += [spec(res, qside)] * 4 + [spec(res, kprev), spec(res, qside)] * 2
        args += [q, do, lse, cterm, k, k, v, v]
    out = jax.ShapeDtypeStruct((ns * max(r // ns, 1), nbs * QB, GW), F32)
    qout = pl.BlockSpec((ns, QB, GW), lambda rb, n: (rb, jnp.minimum(n, nbs - 1), 0))
    kout = pl.BlockSpec((ns, QB, GW), lambda rb, n: (rb, jnp.maximum(n - 1, 0), 0))
    buf = pltpu.VMEM((ns, QB, GW), F32)
    outs = pl.pallas_call(
        body, name=name, grid=(max(r // ns, 1), nbs + 1),
        in_specs=in_specs, out_specs=[qout, kout, kout], out_shape=[out, out, out],
        scratch_shapes=[buf, buf, buf, buf],
        compiler_params=_params(),
    )(*args)
    return [t.reshape(q.shape) for t in outs]


def _token_order(refs, scr):
    return [_from_residues(ref, slice(0, GW), r, scr) for ref, r in zip(refs, DILATIONS)]


def _group_weights(lses):
    l0, l1, l2 = lses
    m = jnp.maximum(jnp.maximum(l0, l1), l2)
    e = [jnp.exp(l - m) for l in (l0, l1, l2)]
    den = e[0] + e[1] + e[2]
    return [ei / den for ei in e]


def _mix_out_fwd(h, vec, gates, ypool, o3, lse3, wpb, wab, wout, name):
    S, D = h.shape

    def body(h_ref, vec_ref, gates_ref, yp_ref, o0, o1, o2, l0, l1, l2, wpb_hbm, wab_hbm, wout_hbm,
             hn_ref, ya_ref, merged_ref, tm_ref, wpb_v, wab_v, wout_v, sem, scr):
        _load_weights([(wpb_hbm, wpb_v), (wab_hbm, wab_v), (wout_hbm, wout_v)], sem)
        gt = vec_ref[3:4, :]
        wts = _group_weights(_token_order((l0, l1, l2), scr))
        og = _token_order((o0, o1, o2), scr)
        ya = (wts[0] * og[0] + wts[1] * og[1] + wts[2] * og[2]).astype(BF16)
        ya_ref[...] = ya
        merged = (gates_ref[:, :D].astype(F32) * _dot(yp_ref[...], wpb_v[...])
                  + gates_ref[:, D:].astype(F32) * _dot(ya, wab_v[...])).astype(BF16)
        merged_ref[...] = merged
        tm = _dot(merged, wout_v[...])
        tm_ref[...] = tm.astype(BF16)
        hn_ref[...] = h_ref[...] + gt * tm

    grp = _row_tile(TS, GW)
    res = [_res_spec(r) for r in DILATIONS]
    return pl.pallas_call(
        body, name=name, grid=(S // TS,),
        in_specs=[_row_tile(TS, D), _const((8, D)), _row_tile(TS, 2 * D), grp] + res * 2 + [ANY, ANY, ANY],
        out_specs=[_row_tile(TS, D), grp, _row_tile(TS, D), _row_tile(TS, D)],
        out_shape=[jax.ShapeDtypeStruct((S, D), F32), jax.ShapeDtypeStruct((S, GW), BF16),
                   jax.ShapeDtypeStruct((S, D), BF16), jax.ShapeDtypeStruct((S, D), BF16)],
        scratch_shapes=[pltpu.VMEM((GW, D), BF16), pltpu.VMEM((GW, D), BF16), pltpu.VMEM((D, D), BF16),
                        pltpu.SemaphoreType.DMA((3,)), pltpu.VMEM(RES_SCRATCH, F32)],
        compiler_params=_params(),
    )(h, vec, gates, ypool, *o3, *lse3, wpb, wab, wout)


def _mix_out_bwd(dh, tm, vec, gates, ypool, o3, lse3, wpb, wab, wout, name, rider=None):
    S, D = dh.shape

    def body(*refs):
        if rider is None:
            return compute(*refs)
        host, mine = rider.split(refs, 14, 12)
        rider.head(mine, pl.program_id(0))
        compute(*host)
        rider.tail(mine, pl.program_id(0), S // TS)

    def compute(dh_ref, tm_ref, vec_ref, gates_ref, yp_ref, o0, o1, o2, l0, l1, l2, wpb_hbm, wab_hbm, wout_hbm,
                dtm_ref, dgl_ref, dypb_ref, dyab_ref, dyp_ref, do0, do1, do2, c0, c1, c2, red_ref,
                wpb_v, wab_v, wout_v, sem, scr):
        _load_weights([(wpb_hbm, wpb_v), (wab_hbm, wab_v), (wout_hbm, wout_v)], sem)

        @pl.when(pl.program_id(0) == 0)
        def _():
            red_ref[...] = jnp.zeros_like(red_ref)

        gt = vec_ref[3:4, :]
        dh_v = dh_ref[...]
        red_ref[0:1, :] += jnp.sum(tm_ref[...].astype(F32) * dh_v, axis=0, keepdims=True)
        dtm = (gt * dh_v).astype(BF16)
        dtm_ref[...] = dtm
        dm = _dot_nt(dtm, wout_v[...])
        wts = _group_weights(_token_order((l0, l1, l2), scr))
        og = _token_order((o0, o1, o2), scr)
        ya = wts[0] * og[0] + wts[1] * og[1] + wts[2] * og[2]
        ypb = _dot(yp_ref[...], wpb_v[...])
        yab = _dot(ya.astype(BF16), wab_v[...])
        gp = gates_ref[:, :D].astype(F32)
        ga = gates_ref[:, D:].astype(F32)
        dgl_ref[:, :D] = (dm * ypb * gp * (1.0 - gp)).astype(BF16)
        dgl_ref[:, D:] = (dm * yab * ga * (1.0 - ga)).astype(BF16)
        dypb = (dm * gp).astype(BF16)
        dyab = (dm * ga).astype(BF16)
        dypb_ref[...] = dypb
        dyab_ref[...] = dyab
        dyp_ref[...] = _dot_nt(dypb, wpb_v[...])
        dya = _dot_nt(dyab, wab_v[...])
        row = lax.broadcasted_iota(jnp.int32, (GW, GW), 0) // HEAD_DIM
        col = lax.broadcasted_iota(jnp.int32, (GW, GW), 1) // HEAD_DIM
        ones = jnp.where(row == col, 1.0, 0.0).astype(F32)
        tot = jnp.dot(dya * ya, ones, preferred_element_type=F32, precision=lax.Precision.HIGHEST)
        for wg, do_ref, c_ref, r in zip(wts, (do0, do1, do2), (c0, c1, c2), DILATIONS):
            _to_residues(wg * dya, do_ref, slice(0, GW), r, scr)
            _to_residues(-(wg * tot), c_ref, slice(0, GW), r, scr)

    grp = _row_tile(TS, GW)
    res = [_res_spec(r) for r in DILATIONS]
    specs = (
        [_row_tile(TS, D), _row_tile(TS, D), _const((8, D)), _row_tile(TS, 2 * D), grp] + res * 2
        + [ANY, ANY, ANY],
        [_row_tile(TS, D), _row_tile(TS, 2 * D), _row_tile(TS, D), _row_tile(TS, D), grp]
        + res * 2 + [_const((8, D))],
        [jax.ShapeDtypeStruct((S, D), BF16), jax.ShapeDtypeStruct((S, 2 * D), BF16),
         jax.ShapeDtypeStruct((S, D), BF16), jax.ShapeDtypeStruct((S, D), BF16), jax.ShapeDtypeStruct((S, GW), F32)]
        + [_res_shape(S, r, BF16) for r in DILATIONS] + [_res_shape(S, r, F32) for r in DILATIONS]
        + [jax.ShapeDtypeStruct((8, D), F32)],
        [pltpu.VMEM((GW, D), BF16), pltpu.VMEM((GW, D), BF16), pltpu.VMEM((D, D), BF16),
         pltpu.SemaphoreType.DMA((3,)), pltpu.VMEM(RES_SCRATCH, F32)])
    in_specs, out_specs, out_shape, scratch = specs if rider is None else rider.specs(*specs)
    outs = pl.pallas_call(
        body, name=name, grid=(S // TS,), in_specs=in_specs, out_specs=out_specs, out_shape=out_shape,
        scratch_shapes=scratch, compiler_params=_params(),
    )(dh, tm, vec, gates, ypool, *o3, *lse3, wpb, wab, wout, *(rider.arrays if rider else []))
    return outs if rider is None else (outs[:12], outs[12:])


def _final_loss(h, target, gf, name):
    S, D = h.shape

    def body(h_ref, t_ref, g_ref, dh_ref, loss_ref, dg_ref):
        @pl.when(pl.program_id(0) == 0)
        def _():
            loss_ref[...] = jnp.zeros_like(loss_ref)
            dg_ref[...] = jnp.zeros_like(dg_ref)

        x = h_ref[...]
        g = g_ref[0:1, :]
        r = lax.rsqrt(jnp.mean(x * x, axis=-1, keepdims=True) + EPS)
        xn = x * r
        err = xn * g - t_ref[...]
        loss_ref[...] += 0.5 * jnp.sum(jnp.mean(err * err, axis=-1, keepdims=True))
        dy = err * (1.0 / D)
        dg_ref[0:1, :] += jnp.sum(dy * xn, axis=0, keepdims=True)
        dxn = dy * g
        dh_ref[...] = r * (dxn - xn * jnp.mean(dxn * xn, axis=-1, keepdims=True))

    return pl.pallas_call(
        body, name=name, grid=(S // TS,),
        in_specs=[_row_tile(TS, D), _row_tile(TS, D), _const((8, D))],
        out_specs=[_row_tile(TS, D), _const((8, 128)), _const((8, D))],
        out_shape=[jax.ShapeDtypeStruct((S, D), F32), jax.ShapeDtypeStruct((8, 128), F32),
                   jax.ShapeDtypeStruct((8, D), F32)],
        compiler_params=_params(),
    )(h, target, gf)


def _ada_mod(c_all, w, b, name):
    def body(c_ref, w_ref, b_ref, cond_ref, mod_ref):
        cv = c_ref[...]
        cond = cv * jax.nn.sigmoid(cv)
        cond_ref[...] = cond
        mod_ref[...] = jnp.dot(cond, w_ref[...], preferred_element_type=F32,
                               precision=lax.Precision.HIGHEST) + b_ref[...]

    return pl.pallas_call(
        body, name=name,
        out_shape=[jax.ShapeDtypeStruct(c_all.shape, F32), jax.ShapeDtypeStruct((c_all.shape[0], w.shape[1]), F32)],
        compiler_params=_params(),
    )(c_all, w, b)


def _adamw_math(w, g, m, v):
    m = ADAM_B1 * m + (1.0 - ADAM_B1) * g
    v = ADAM_B2 * v + (1.0 - ADAM_B2) * (g * g)
    m_hat = m / (1.0 - ADAM_B1 ** ADAM_STEP)
    v_hat = v / (1.0 - ADAM_B2 ** ADAM_STEP)
    delta = -ADAM_LR * (m_hat / (jnp.sqrt(v_hat) + ADAM_EPS) + ADAM_WD * w)
    return delta, m, v


def _adamw(w, g, m, v, name):
    R, C = w.shape
    tr = R
    for cand in (256, 128, 64, 32, 16, 8):
        if R % cand == 0:
            tr = cand
            break

    def body(w_ref, g_ref, m_ref, v_ref, d_ref, mo_ref, vo_ref):
        d_ref[...], mo_ref[...], vo_ref[...] = _adamw_math(w_ref[...], g_ref[...], m_ref[...], v_ref[...])

    spec = _row_tile(tr, C)
    out = jax.ShapeDtypeStruct((R, C), F32)
    return pl.pallas_call(
        body, name=name, grid=(R // tr,), in_specs=[spec] * 4, out_specs=[spec] * 3, out_shape=[out] * 3,
        compiler_params=_params(),
    )(w, g, m, v)


def _ada_grad_adamw(cond_t, dmod, w, m, v, name, tr=256):
    R, C = w.shape
    nb = dmod.shape[0]

    def body(ct_ref, dm_ref, w_ref, m_ref, v_ref, g_ref, d_ref, mo_ref, vo_ref):
        ct = ct_ref[...]
        dm = dm_ref[...]
        g = jnp.zeros((tr, C), F32)
        for bi in range(nb):
            g = g + ct[:, bi:bi + 1] * dm[bi:bi + 1, :]
        g_ref[...] = g
        d_ref[...], mo_ref[...], vo_ref[...] = _adamw_math(w_ref[...], g, m_ref[...], v_ref[...])

    spec = _row_tile(tr, C)
    out = jax.ShapeDtypeStruct((R, C), F32)
    return pl.pallas_call(
        body, name=name, grid=(R // tr,),
        in_specs=[_row_tile(tr, nb), _const((nb, C)), spec, spec, spec],
        out_specs=[spec] * 4, out_shape=[out] * 4,
        compiler_params=_params(),
    )(cond_t, dmod, w, m, v)


def _row_step(rows, cap=256):
    for cand in range(cap, 15, -16):
        if rows % cand == 0:
            return cand
    return rows


def _slot_sum(x_ref):
    acc = x_ref[0].astype(F32)
    for k in range(1, x_ref.shape[0]):
        acc = acc + x_ref[k].astype(F32)
    return acc


def _sum_slots(x, name, out_dtype=F32):
    n, R, C = x.shape
    tr = _row_step(R)

    def body(x_ref, o_ref):
        o_ref[...] = _slot_sum(x_ref).astype(out_dtype)

    return pl.pallas_call(
        body, name=name, grid=(R // tr,),
        in_specs=[pl.BlockSpec((n, tr, C), lambda i: (0, i, 0))],
        out_specs=_row_tile(tr, C), out_shape=jax.ShapeDtypeStruct((R, C), out_dtype),
        compiler_params=_params(),
    )(x)


def _sum_pair(core, g, recv, name):
    _, _, R, C = g.shape
    tr = _row_step(R)

    def body(core_ref, g_ref, r_ref, o_ref):
        o_ref[...] = (g_ref[...].astype(F32) + r_ref[...].astype(F32)).astype(BF16)

    return pl.pallas_call(
        body, name=name, out_shape=jax.ShapeDtypeStruct((4, R, C), BF16),
        grid_spec=pltpu.PrefetchScalarGridSpec(
            num_scalar_prefetch=1, grid=(4, R // tr),
            in_specs=[pl.BlockSpec((None, None, tr, C), lambda k, i, core_ref: (k, core_ref[0], i, 0)),
                      pl.BlockSpec((None, tr, C), lambda k, i, core_ref: (k, i, 0))],
            out_specs=pl.BlockSpec((None, tr, C), lambda k, i, core_ref: (k, i, 0))),
        compiler_params=_params(),
    )(core, g, recv)


def _sum_adamw(chip, own, recv, w, m, v, name):
    _, R, C = own.shape
    tr = _row_step(R)

    def body(chip_ref, own_ref, r_ref, w_ref, m_ref, v_ref, g_ref, d_ref, mo_ref, vo_ref):
        g = own_ref[...].astype(F32) + _slot_sum(r_ref)
        g_ref[...] = g
        d_ref[...], mo_ref[...], vo_ref[...] = _adamw_math(w_ref[...], g, m_ref[...], v_ref[...])

    spec = pl.BlockSpec((tr, C), lambda i, chip_ref: (i, 0))
    out = jax.ShapeDtypeStruct((R, C), F32)
    return pl.pallas_call(
        body, name=name, out_shape=[out] * 4,
        grid_spec=pltpu.PrefetchScalarGridSpec(
            num_scalar_prefetch=1, grid=(R // tr,),
            in_specs=[pl.BlockSpec((None, tr, C), lambda i, chip_ref: (chip_ref[0], i, 0)),
                      pl.BlockSpec((3, tr, C), lambda i, chip_ref: (0, i, 0)), spec, spec, spec],
            out_specs=[spec] * 4),
        compiler_params=_params(),
    )(chip, own, recv, w, m, v)


def _place():
    return lax.axis_index("x"), lax.axis_index("y"), lax.axis_index("c")


def _gather_steps(x_refs, out_refs, send_sems, recv_sems):
    n = len(x_refs)
    x, y, c = _place()
    me, sibling = (x, y, c), (x, y, 1 - c)
    chips = [(1 - x, y), (x, 1 - y), (1 - x, 1 - y)]

    def rows(a, px, py, pc):
        return out_refs[a].at[4 * px + 2 * py + pc]

    def copy(a, k, block, to, src=None):
        return pltpu.make_async_remote_copy(
            src_ref=rows(a, *block) if src is None else src, dst_ref=rows(a, *block),
            send_sem=send_sems.at[a, k], recv_sem=recv_sems.at[a, k], device_id=to, device_id_type=MESH)

    def first(a):
        return [copy(a, 0, me, sibling, src=x_refs[a])] + [
            copy(a, 1 + j, me, (*chip, c), src=x_refs[a]) for j, chip in enumerate(chips)]

    def passed(a, j):
        return copy(a, 4 + j, (*chips[j], c), sibling)

    def start():
        for a in range(n):
            for cp in first(a):
                cp.start()

    def relay():
        for j, chip in enumerate(chips):
            for a in range(n):
                copy(a, 1 + j, (*chip, c), me).wait_recv()
                passed(a, j).start()

    def finish():
        for a in range(n):
            copy(a, 0, sibling, me).wait_recv()
            for j, chip in enumerate(chips):
                copy(a, 4 + j, (*chip, 1 - c), me).wait_recv()
        for a in range(n):
            for cp in first(a) + [passed(a, j) for j in range(3)]:
                cp.wait_send()

    return start, relay, finish


def _all_gather(arrs, name, own=True):
    n = len(arrs)

    def body(*refs):
        x_refs, out_refs = refs[:n], refs[n:2 * n]
        send_sems, recv_sems, local_sems = refs[2 * n:]
        me = 4 * lax.axis_index("x") + 2 * lax.axis_index("y") + lax.axis_index("c")
        mine = [pltpu.make_async_copy(x_refs[a], out_refs[a].at[me], local_sems.at[a]) for a in range(n)] if own else []
        for cp in mine:
            cp.start()
        for step in _gather_steps(x_refs, out_refs, send_sems, recv_sems):
            step()
        for cp in mine:
            cp.wait()

    return pl.pallas_call(
        body, name=name, out_shape=[jax.ShapeDtypeStruct((N_DEV,) + t.shape, t.dtype) for t in arrs],
        in_specs=[ANY] * n, out_specs=[ANY] * n,
        scratch_shapes=[pltpu.SemaphoreType.DMA((n, 7)), pltpu.SemaphoreType.DMA((n, 7)),
                        pltpu.SemaphoreType.DMA((n,))],
    )(*arrs)


def _pair_exchange(arrs, name):
    n = len(arrs)

    def body(*refs):
        g_refs, out_refs = refs[:n], refs[n:2 * n]
        send_sems, recv_sems = refs[2 * n:]
        x, y, c = _place()
        give = [pltpu.make_async_remote_copy(
            src_ref=g_refs[a].at[pl.ds(0, 4), 1 - c], dst_ref=out_refs[a], send_sem=send_sems.at[a],
            recv_sem=recv_sems.at[a], device_id=(x, y, 1 - c), device_id_type=MESH) for a in range(n)]
        for cp in give:
            cp.start()
        for cp in give:
            cp.wait()

    return pl.pallas_call(
        body, name=name,
        out_shape=[jax.ShapeDtypeStruct((4,) + t.shape[2:], t.dtype) for t in arrs],
        in_specs=[ANY] * n, out_specs=[ANY] * n,
        scratch_shapes=[pltpu.SemaphoreType.DMA((n,)), pltpu.SemaphoreType.DMA((n,))],
    )(*arrs)


def _chip_exchange_steps(p_refs, out_refs, send_sems, recv_sems):
    x, y, c = _place()
    chips = [(1 - x, y), (x, 1 - y), (1 - x, 1 - y)]

    def copies():
        return [pltpu.make_async_remote_copy(
            src_ref=p_refs[a].at[2 * px + py], dst_ref=out_refs[a].at[j], send_sem=send_sems.at[a, j],
            recv_sem=recv_sems.at[a, j], device_id=(px, py, c), device_id_type=MESH)
            for a in range(len(p_refs)) for j, (px, py) in enumerate(chips)]

    def start():
        for cp in copies():
            cp.start()

    def finish():
        for cp in copies():
            cp.wait()

    return start, finish


def _chip_exchange(arrs, name):
    n = len(arrs)

    def body(*refs):
        for step in _chip_exchange_steps(refs[:n], refs[n:2 * n], *refs[2 * n:]):
            step()

    return pl.pallas_call(
        body, name=name, out_shape=[jax.ShapeDtypeStruct((3,) + t.shape[1:], t.dtype) for t in arrs],
        in_specs=[ANY] * n, out_specs=[ANY] * n,
        scratch_shapes=[pltpu.SemaphoreType.DMA((n, 3)), pltpu.SemaphoreType.DMA((n, 3))],
    )(*arrs)


class _Rider:
    def __init__(self, arrays, out_shape, sems, steps, relay_before_end=None):
        self.arrays, self.out_shape, self.scratch, self.steps = list(arrays), out_shape, sems, steps
        self.n = len(self.arrays)
        self.relay_before_end = relay_before_end

    def specs(self, in_specs, out_specs, out_shape, scratch):
        extra = [ANY] * self.n
        return in_specs + extra, out_specs + extra, out_shape + self.out_shape, scratch + self.scratch

    def split(self, refs, n_in, n_out):
        k = self.n
        a, b = n_in + k, n_in + k + n_out
        return refs[:n_in] + refs[a:b] + refs[b + k:-2], (refs[n_in:a], refs[b:b + k], refs[-2:])

    def head(self, mine, step):
        pl.when(step == 0)(self.steps(mine[0], mine[1], *mine[2])[0])

    def tail(self, mine, step, nsteps):
        steps = self.steps(mine[0], mine[1], *mine[2])
        if self.relay_before_end is not None:
            pl.when(step == nsteps - 1 - self.relay_before_end)(steps[1])
        pl.when(step == nsteps - 1)(steps[-1])


def _gather_rider(arrs, relay_before_end=4):
    n = len(arrs)
    return _Rider(arrs, [jax.ShapeDtypeStruct((N_DEV,) + t.shape, t.dtype) for t in arrs],
                  [pltpu.SemaphoreType.DMA((n, 7)), pltpu.SemaphoreType.DMA((n, 7))], _gather_steps,
                  relay_before_end)


def _chip_exchange_rider(arrs):
    n = len(arrs)
    return _Rider(arrs, [jax.ShapeDtypeStruct((3,) + t.shape[1:], t.dtype) for t in arrs],
                  [pltpu.SemaphoreType.DMA((n, 3)), pltpu.SemaphoreType.DMA((n, 3))], _chip_exchange_steps)


def _rope_tables(positions):
    inv_freq = ROPE_THETA ** (-jnp.arange(0, HEAD_DIM, 2, dtype=F32) / HEAD_DIM)
    ang = positions.astype(F32)[:, None] * inv_freq
    cos, sin = jnp.cos(ang), jnp.sin(ang)
    return jnp.tile(cos, (1, 4)), jnp.tile(jnp.concatenate([-sin, sin], axis=1), (1, 2))


def _vec(g, shift, scale, gate):
    z = jnp.zeros_like(g)
    return jnp.stack([g, shift, scale, gate, z, z, z, z])


class _GradReducer:
    def __init__(self):
        self.core = lax.axis_index("c").astype(jnp.int32).reshape(1)
        self.own, self.others, self.waiting, self.riding = {}, {}, [], []

    def pair(self, named):
        keys = list(named)
        mine = [named[k].reshape((4, 2) + named[k].shape[1:]) for k in keys]
        theirs = _pair_exchange(mine, "reduce_pair_" + keys[0])
        for k, g, r in zip(keys, mine, theirs):
            self.own[k] = _sum_pair(self.core, g, r, "sum_pair_" + k)
        self.waiting += keys

    def rider(self):
        self.riding, self.waiting = self.waiting, []
        return _chip_exchange_rider([self.own[k] for k in self.riding])

    def landed(self, results):
        self.others.update(zip(self.riding, results))

    def flush(self, name):
        keys, self.waiting = self.waiting, []
        self.others.update(zip(keys, _chip_exchange([self.own[k] for k in keys], name)))


def _by_owner(g):
    if g.ndim == 3:
        return g if g.shape[0] == N_DEV else g.reshape(N_DEV, g.shape[1] * g.shape[0] // N_DEV, g.shape[2])
    return g.reshape(N_DEV, g.shape[0] // N_DEV, g.shape[1])


def _local_step(x, target, positions, mod, small, W, late=None, red=None):
    S, D = x.shape
    sh1, sc1, gt1, sh2, sc2, gt2, sh3, sc3, gt3 = (mod[k] for k in range(9))
    v1 = _vec(small["g1"], sh1, sc1, gt1)
    v2 = _vec(small["g2"], sh2, sc2, gt2)
    v3 = _vec(small["g3"], sh3, sc3, gt3)
    vf = _vec(small["gf"], small["gf"], small["gf"], small["gf"])
    cos, sin = _rope_tables(positions)
    wbd = jax.scipy.linalg.block_diag(*[small["w_pool"][k] for k in range(4)]).astype(BF16)
    pscale = small["pool_scale"].reshape(1, GW)

    if late is None:
        h1, u1, ab1, act1, f1 = _ffn_fwd(x, v1, W["w1in"], W["w1out"], "ffn1_fwd")
    else:
        (h1, u1, ab1, act1, f1), landed = _ffn_fwd(x, v1, W["w1in"], W["w1out"], "ffn1_fwd", rider=late[0])
        W = {**W, **late[1](landed)}
    u2, p, gates, *qkv = _mix_in_fwd(h1, v2, cos, sin, W["win"], "mix_in_fwd")
    dpool, ypool = _pool_fwd(p, wbd, pscale, "pool_fwd")
    o3, lse3 = [], []
    for gi in range(len(DILATIONS)):
        o, lse = _attn_fwd(qkv[gi], qkv[3 + gi], qkv[6 + gi], f"attn_fwd_{gi}")
        o3.append(o)
        lse3.append(lse)
    h2, ya, merged, tm = _mix_out_fwd(h1, v2, gates, ypool, o3, lse3, W["wpb"], W["wab"], W["wout"], "mix_out_fwd")
    h3, u3, ab3, act3, f3 = _ffn_fwd(h2, v3, W["w2in"], W["w2out"], "ffn2_fwd")
    dh3, loss_blk, dgf = _final_loss(h3, target, vf, "final_loss")

    dh2, dab3, df3, red3 = _ffn_bwd(dh3, h2, f3, ab3, v3, W["w2in"], W["w2out"], "ffn2_bwd")
    half_f = ab3.shape[2] // 2
    G = {"w2in": _by_owner(_wgrad(dab3, u3, "wgrad_2in", tm=half_f)),
         "w2out": _by_owner(_wgrad(act3, df3, "wgrad_2out", tm=half_f))}
    mix_out_args = (dh2, tm, v2, gates, ypool, o3, lse3, W["wpb"], W["wab"], W["wout"], "mix_out_bwd")
    if red is None:
        mix_out = _mix_out_bwd(*mix_out_args)
    else:
        red.pair({k: G[k] for k in ("w2in", "w2out")})
        mix_out, landed = _mix_out_bwd(*mix_out_args, rider=red.rider())
        red.landed(landed)
    (dtm, dgl, dypb, dyab, dyp, do0, do1, do2, c0, c1, c2, red2o) = mix_out
    dq3, dk3, dv3 = [], [], []
    for gi, (do, ct) in enumerate(zip((do0, do1, do2), (c0, c1, c2))):
        dq, dk, dv = _attn_bwd(qkv[gi], qkv[3 + gi], qkv[6 + gi], do, lse3[gi], ct, f"attn_bwd_{gi}")
        dq3.append(dq)
        dk3.append(dk)
        dv3.append(dv)
    dp, dwbd, dps = _pool_bwd(dyp, dpool, wbd, pscale, "pool_bwd")
    dh1, dproj, red2i = _mix_in_bwd(dh2, h1, v2, cos, sin, dp, dq3 + dk3 + dv3, dgl, W["win"], "mix_in_bwd")
    G["win"] = _by_owner(_wgrad(dproj, u2, "wgrad_in", tm=1152))
    G["wpb"] = _full_to_cols(_wgrad(ypool, dypb, "wgrad_pb"))
    G["wab"] = _full_to_cols(_wgrad(ya, dyab, "wgrad_ab"))
    G["wout"] = _by_owner(_wgrad(merged, dtm, "wgrad_out"))
    if red is not None:
        red.pair({k: G[k] for k in ("win", "wpb", "wab", "wout")})
    dx, dab1, df1, red1 = _ffn_bwd(dh1, x, f1, ab1, v1, W["w1in"], W["w1out"], "ffn1_bwd")
    if red is None:
        G["w1in"] = _by_owner(_wgrad(dab1, u1, "wgrad_1in", tm=half_f))
        G["w1out"] = _by_owner(_wgrad(act1, df1, "wgrad_1out", tm=half_f))
    else:
        g1in, landed = _wgrad(dab1, u1, "wgrad_1in", tm=half_f, rider=red.rider())
        red.landed(landed)
        G["w1in"] = _by_owner(g1in)
        red.pair({"w1in": G["w1in"]})
        g1out, landed = _wgrad(act1, df1, "wgrad_1out", tm=half_f, rider=red.rider())
        red.landed(landed)
        G["w1out"] = _by_owner(g1out)
        red.pair({"w1out": G["w1out"]})
        red.flush("reduce_chips_w1out")
    dmod = jnp.stack([red1[1], red1[2], red1[0], red2i[1], red2i[2], red2o[0], red3[1], red3[2], red3[0]])
    dsmall = {
        "g1": red1[3], "g2": red2i[3], "g3": red3[3], "gf": dgf[0],
        "w_pool": jnp.stack([dwbd[k * 64:(k + 1) * 64, k * 64:(k + 1) * 64] for k in range(4)]),
        "pool_scale": dps[0],
    }
    return loss_blk[0, 0], dx, G, dmod, dsmall


SHARDED = ("w_ffn1_in", "w_ffn1_out", "w_in", "w_pool_branch", "w_attn_branch", "w_out", "w_ffn2_in", "w_ffn2_out")
TRANSPOSED = ("w_ffn1_in", "w_in", "w_ffn2_in")
FIRST = ("w_ffn1_in", "w_ffn1_out")
LATER = tuple(n for n in SHARDED if n not in FIRST)
GRAD_KEY = dict(w_ffn1_in="w1in", w_ffn1_out="w1out", w_in="win", w_pool_branch="wpb", w_attn_branch="wab",
                w_out="wout", w_ffn2_in="w2in", w_ffn2_out="w2out")


def _cols_to_full(g):
    return jnp.concatenate([g[j] for j in range(N_DEV)], axis=1)


def _full_to_cols(t):
    c = t.shape[1] // N_DEV
    return jnp.stack([t[:, j * c:(j + 1) * c] for j in range(N_DEV)])


SMALL =(("b_ada", 9216), ("g_norm_ffn1", 1024), ("g_norm_mix", 1024), ("g_norm_ffn2", 1024), ("g_final", 1024),
         ("w_pool", 16384), ("pool_scale", 256))
SMALL_ROWS = 240


def _pack_small(vals, loss=None):
    flat = jnp.concatenate([vals[name].reshape(-1) for name, _ in SMALL])
    tail = jnp.zeros((SMALL_ROWS * 128 - flat.shape[0],), F32)
    if loss is not None:
        tail = tail.at[0].set(loss)
    return jnp.concatenate([flat, tail]).reshape(SMALL_ROWS, 128)


def _unpack_small(slab, shapes):
    flat, out, off = slab.reshape(-1), {}, 0
    for name, n in SMALL:
        out[name] = flat[off:off + n].reshape(shapes[name])
        off += n
    return out, flat[off]


def kernel(x, c, positions, w_ada, b_ada, g_norm_ffn1, w_ffn1_in, w_ffn1_out, g_norm_mix, w_in, w_pool, pool_scale, w_pool_branch, w_attn_branch, w_out, g_norm_ffn2, w_ffn2_in, w_ffn2_out, g_final, loss_target, m_w_ada, m_b_ada, m_g_norm_ffn1, m_w_ffn1_in, m_w_ffn1_out, m_g_norm_mix, m_w_in, m_w_pool, m_pool_scale, m_w_pool_branch, m_w_attn_branch, m_w_out, m_g_norm_ffn2, m_w_ffn2_in, m_w_ffn2_out, m_g_final, v_w_ada, v_b_ada, v_g_norm_ffn1, v_w_ffn1_in, v_w_ffn1_out, v_g_norm_mix, v_w_in, v_w_pool, v_pool_scale, v_w_pool_branch, v_w_attn_branch, v_w_out, v_g_norm_ffn2, v_w_ffn2_in, v_w_ffn2_out, v_g_final):
    names = ["w_ada", "b_ada", "g_norm_ffn1", "w_ffn1_in", "w_ffn1_out", "g_norm_mix", "w_in", "w_pool", "pool_scale",
             "w_pool_branch", "w_attn_branch", "w_out", "g_norm_ffn2", "w_ffn2_in", "w_ffn2_out", "g_final"]
    w = dict(w_ada=w_ada, b_ada=b_ada, g_norm_ffn1=g_norm_ffn1, w_ffn1_in=w_ffn1_in, w_ffn1_out=w_ffn1_out,
             g_norm_mix=g_norm_mix, w_in=w_in, w_pool=w_pool, pool_scale=pool_scale, w_pool_branch=w_pool_branch,
             w_attn_branch=w_attn_branch, w_out=w_out, g_norm_ffn2=g_norm_ffn2, w_ffn2_in=w_ffn2_in,
             w_ffn2_out=w_ffn2_out, g_final=g_final)
    m = dict(w_ada=m_w_ada, b_ada=m_b_ada, g_norm_ffn1=m_g_norm_ffn1, w_ffn1_in=m_w_ffn1_in, w_ffn1_out=m_w_ffn1_out,
             g_norm_mix=m_g_norm_mix, w_in=m_w_in, w_pool=m_w_pool, pool_scale=m_pool_scale,
             w_pool_branch=m_w_pool_branch, w_attn_branch=m_w_attn_branch, w_out=m_w_out, g_norm_ffn2=m_g_norm_ffn2,
             w_ffn2_in=m_w_ffn2_in, w_ffn2_out=m_w_ffn2_out, g_final=m_g_final)
    v = dict(w_ada=v_w_ada, b_ada=v_b_ada, g_norm_ffn1=v_g_norm_ffn1, w_ffn1_in=v_w_ffn1_in, w_ffn1_out=v_w_ffn1_out,
             g_norm_mix=v_g_norm_mix, w_in=v_w_in, w_pool=v_w_pool, pool_scale=v_pool_scale,
             w_pool_branch=v_w_pool_branch, w_attn_branch=v_w_attn_branch, w_out=v_w_out, g_norm_ffn2=v_g_norm_ffn2,
             w_ffn2_in=v_w_ffn2_in, w_ffn2_out=v_w_ffn2_out, g_final=v_g_final)
    shapes = {n: w[n].shape for n in names}
    me = 4 * lax.axis_index("x") + 2 * lax.axis_index("y") + lax.axis_index("c")
    D = x.shape[-1]
    n_mod = w_ada.shape[-1] * N_DEV // D

    (c_all,) = _all_gather([c.reshape(D // 128, 128)], "gather_c")
    ada_cols = w_ada.shape[-1]
    b_mine = lax.dynamic_slice_in_dim(b_ada, me * ada_cols, ada_cols, axis=1)
    cond, mod_part = _ada_mod(c_all.reshape(N_DEV, D), w_ada[0], b_mine, "ada_mod")
    (mod_all,) = _all_gather([mod_part.reshape(-1, 128)], "gather_mod")
    mod_all = mod_all.reshape(N_DEV, N_DEV, ada_cols)
    mod = lax.dynamic_index_in_dim(mod_all, me, axis=1, keepdims=False).reshape(n_mod, D)

    def local(t, name):
        return t[name][0].T if name in TRANSPOSED else t[name][0]

    shards = {name: local(w, name).astype(BF16) for name in SHARDED}

    def gather_done(names, fulls):
        return {name: lax.dynamic_update_index_in_dim(full, shards[name], me, axis=0)
                for name, full in zip(names, fulls)}

    def ffn_weights(g, pre):
        return {"w%sin" % pre: g["w_ffn%s_in" % pre].reshape(2, -1, D),
                "w%sout" % pre: g["w_ffn%s_out" % pre].reshape(-1, D)}

    def later_weights(fulls):
        g = gather_done(LATER, fulls)
        return dict(win=g["w_in"].reshape(-1, D), wpb=_cols_to_full(g["w_pool_branch"]),
                    wab=_cols_to_full(g["w_attn_branch"]), wout=g["w_out"].reshape(D, D), **ffn_weights(g, "2"))

    W = ffn_weights(gather_done(FIRST, _all_gather([shards[n] for n in FIRST], "gather_ffn1", own=False)), "1")
    small = dict(g1=g_norm_ffn1[0], g2=g_norm_mix[0], g3=g_norm_ffn2[0], gf=g_final, w_pool=w_pool[0],
                 pool_scale=pool_scale[0])
    red = _GradReducer()
    loss_part, dx, G, dmod, dsmall = _local_step(
        x[0], loss_target[0], positions[0], mod, small, W,
        late=(_gather_rider([shards[n] for n in LATER]), later_weights), red=red)
    chip = (2 * lax.axis_index("x") + lax.axis_index("y")).astype(jnp.int32).reshape(1)

    part = _pack_small(dict(b_ada=dmod, g_norm_ffn1=dsmall["g1"], g_norm_mix=dsmall["g2"], g_norm_ffn2=dsmall["g3"],
                            g_final=dsmall["gf"], w_pool=dsmall["w_pool"], pool_scale=dsmall["pool_scale"]),
                       loss=loss_part)
    (parts,) = _all_gather([part], "gather_small")
    gsmall, loss = _unpack_small(_sum_slots(parts, "sum_small"), shapes)
    rows_mine = ada_cols // 128
    dmod_mine = lax.dynamic_slice_in_dim(parts, me * rows_mine, rows_mine, axis=1).reshape(N_DEV, ada_cols)

    grads, delta, new_m, new_v = {}, {}, {}, {}
    grads["w_ada"], delta["w_ada"], new_m["w_ada"], new_v["w_ada"] = (
        t[None] for t in _ada_grad_adamw(cond.T, dmod_mine, w_ada[0], m_w_ada[0], v_w_ada[0], "ada_grad_adamw"))
    for name in SHARDED:
        key = GRAD_KEY[name]
        res = _sum_adamw(chip, red.own[key], red.others[key], local(w, name), local(m, name), local(v, name),
                         "adamw_" + name)
        grads[name], delta[name], new_m[name], new_v[name] = (
            (t.T if name in TRANSPOSED else t)[None] for t in res)
    sd, sm, sv = _adamw(_pack_small(w), _pack_small(gsmall), _pack_small(m), _pack_small(v), "adamw_small")
    for dst, src in ((delta, sd), (new_m, sm), (new_v, sv)):
        dst.update(_unpack_small(src, shapes)[0])
    grads.update(gsmall)

    return (loss, dx[None], *[grads[n] for n in names], *[delta[n] for n in names],
            *[new_m[n] for n in names], *[new_v[n] for n in names])
```

```python
import functools

import jax
import jax.numpy as jnp
from jax import lax
from jax.experimental import pallas as pl
from jax.experimental.pallas import tpu as pltpu

F32 = jnp.float32
BF16 = jnp.bfloat16
MESH = pl.DeviceIdType.MESH
ANY = pl.BlockSpec(memory_space=pl.ANY)

N_DEV = 8
EPS = 1e-6
HEAD_DIM = 64
HEADS = 4
GW = HEADS * HEAD_DIM
DILATIONS = (1, 4, 16)
BAND = 128
QB = 128
POOL_WINDOWS = (2, 4, 8, 16)
HALO = 16
ROPE_THETA = 10000.0

ADAM_LR = 0.001
ADAM_B1 = 0.9
ADAM_B2 = 0.999
ADAM_EPS = 1e-08
ADAM_WD = 0.01
ADAM_STEP = 10

VMEM_LIMIT = 56 * 1024 * 1024
TS = 512
FFN_TS = 256
FC = 2816

NT = (((1,), (1,)), ((), ()))
TN = (((0,), (0,)), ((), ()))


def _params(**kw):
    return pltpu.CompilerParams(vmem_limit_bytes=VMEM_LIMIT, **kw)


def _dot(a, b):
    return jnp.dot(a, b, preferred_element_type=F32)


def _dot_nt(a, b):
    return lax.dot_general(a, b, NT, preferred_element_type=F32)


def _dot_tn(a, b):
    return lax.dot_general(a, b, TN, preferred_element_type=F32)


def _load_weights(pairs, sem):
    @pl.when(pl.program_id(0) == 0)
    def _():
        copies = [pltpu.make_async_copy(src, dst, sem.at[i]) for i, (src, dst) in enumerate(pairs)]
        for cp in copies:
            cp.start()
        for cp in copies:
            cp.wait()


def _norm_mod(x, g, sc, sh):
    r = lax.rsqrt(jnp.mean(x * x, axis=-1, keepdims=True) + EPS)
    xn = x * r
    y = xn * g
    return r, xn, y, y * (1.0 + sc) + sh


def _norm_mod_bwd(du, r, xn, y, g, sc):
    dsh = jnp.sum(du, axis=0, keepdims=True)
    dsc = jnp.sum(du * y, axis=0, keepdims=True)
    dy = du * (1.0 + sc)
    dg = jnp.sum(dy * xn, axis=0, keepdims=True)
    dxn = dy * g
    dx = r * (dxn - xn * jnp.mean(dxn * xn, axis=-1, keepdims=True))
    return dx, dsh, dsc, dg


def _row_tile(ts, width):
    return pl.BlockSpec((ts, width), lambda i: (i, 0))


def _const(shape):
    return pl.BlockSpec(shape, lambda *_: (0,) * len(shape))


def _final_tile(x, g, target):
    r = lax.rsqrt(jnp.mean(x * x, axis=-1, keepdims=True) + EPS)
    xn = x * r
    err = xn * g - target
    loss = 0.5 * jnp.sum(jnp.mean(err * err, axis=-1, keepdims=True))
    dy = err * (1.0 / x.shape[-1])
    dg = jnp.sum(dy * xn, axis=0, keepdims=True)
    dxn = dy * g
    return r * (dxn - xn * jnp.mean(dxn * xn, axis=-1, keepdims=True)), loss, dg


def _ffn_fwd(h, vec, win, wout, name, rider=None, final=None):
    TS = FFN_TS
    S, D = h.shape
    _, Fd, _ = win.shape
    nch = Fd // FC
    n_in, n_out = (6, 7) if final else (4, 5)

    def body(*refs):
        if rider is None:
            return compute(*refs)
        host, mine = rider.split(refs, n_in, n_out)
        rider.head(mine, pl.program_id(0))
        compute(*host)
        rider.tail(mine, pl.program_id(0), S // TS)

    def compute(*refs):
        h_ref, vec_ref, win_hbm, wout_hbm = refs[:4]
        hn_ref, u_ref, ab_ref, act_ref, f_ref = refs[n_in:n_in + 5]
        win_v, wout_v, sem = refs[n_in + n_out:]
        _load_weights([(win_hbm, win_v), (wout_hbm, wout_v)], sem)
        x = h_ref[...]
        g, sh, sc, gt = (vec_ref[k:k + 1, :] for k in range(4))
        _, _, _, u = _norm_mod(x, g, sc, sh)
        ub = u.astype(BF16)
        u_ref[...] = ub
        acc = jnp.zeros((TS, D), F32)
        for j in range(nch):
            sl = slice(j * FC, (j + 1) * FC)
            a = _dot_nt(ub, win_v[0, sl, :])
            b = _dot_nt(ub, win_v[1, sl, :])
            act = ((a * jax.nn.sigmoid(a)) * b).astype(BF16)
            ab_ref[0, :, sl] = a.astype(BF16)
            ab_ref[1, :, sl] = b.astype(BF16)
            act_ref[:, sl] = act
            acc = acc + _dot(act, wout_v[sl, :])
        f_ref[...] = acc
        hn = x + (0.5 * gt) * acc
        if not final:
            hn_ref[...] = hn
            return
        t_ref, gf_ref = refs[4:6]
        loss_ref, dgf_ref = refs[n_in + 5:n_in + 7]

        @pl.when(pl.program_id(0) == 0)
        def _():
            loss_ref[...] = jnp.zeros_like(loss_ref)
            dgf_ref[...] = jnp.zeros_like(dgf_ref)

        hn_ref[...], loss, dg = _final_tile(hn, gf_ref[0:1, :], t_ref[...])
        loss_ref[...] += loss
        dgf_ref[0:1, :] += dg

    specs = (
        [_row_tile(TS, D), _const((8, D)), ANY, ANY] + ([_row_tile(TS, D), _const((8, D))] if final else []),
        [_row_tile(TS, D), _row_tile(TS, D), pl.BlockSpec((2, TS, Fd), lambda i: (0, i, 0)),
         _row_tile(TS, Fd), _row_tile(TS, D)] + ([_const((8, 128)), _const((8, D))] if final else []),
        [jax.ShapeDtypeStruct((S, D), F32), jax.ShapeDtypeStruct((S, D), BF16),
         jax.ShapeDtypeStruct((2, S, Fd), BF16), jax.ShapeDtypeStruct((S, Fd), BF16),
         jax.ShapeDtypeStruct((S, D), F32)]
        + ([jax.ShapeDtypeStruct((8, 128), F32), jax.ShapeDtypeStruct((8, D), F32)] if final else []),
        [pltpu.VMEM(win.shape, BF16), pltpu.VMEM(wout.shape, BF16), pltpu.SemaphoreType.DMA((2,))])
    in_specs, out_specs, out_shape, scratch = specs if rider is None else rider.specs(*specs)
    outs = pl.pallas_call(
        body, name=name, grid=(S // TS,), in_specs=in_specs, out_specs=out_specs, out_shape=out_shape,
        scratch_shapes=scratch, compiler_params=_params(),
    )(h, vec, win, wout, *(final or ()), *(rider.arrays if rider else []))
    return outs if rider is None else (outs[:n_out], outs[n_out:])


def _ffn_bwd(dh, h, f, ab, vec, win, wout, name):
    TS = FFN_TS
    S, D = h.shape
    _, Fd, _ = win.shape
    nch = Fd // FC

    def body(dh_ref, h_ref, f_ref, ab_ref, vec_ref, win_hbm, wout_hbm,
             dhp_ref, dab_ref, df_ref, red_ref, win_v, wout_v, sem):
        _load_weights([(win_hbm, win_v), (wout_hbm, wout_v)], sem)

        @pl.when(pl.program_id(0) == 0)
        def _():
            red_ref[...] = jnp.zeros_like(red_ref)

        dh_v = dh_ref[...]
        x = h_ref[...]
        g, sh, sc, gt = (vec_ref[k:k + 1, :] for k in range(4))
        dgt = jnp.sum((0.5 * f_ref[...]) * dh_v, axis=0, keepdims=True)
        dfb = ((0.5 * gt) * dh_v).astype(BF16)
        df_ref[...] = dfb
        du = jnp.zeros((TS, D), F32)
        for j in range(nch):
            sl = slice(j * FC, (j + 1) * FC)
            dact = _dot_nt(dfb, wout_v[sl, :])
            av = ab_ref[0, :, sl].astype(F32)
            bv = ab_ref[1, :, sl].astype(F32)
            sg = jax.nn.sigmoid(av)
            da = (dact * bv * (sg * (1.0 + av * (1.0 - sg)))).astype(BF16)
            db = (dact * (av * sg)).astype(BF16)
            dab_ref[0, :, sl] = da
            dab_ref[1, :, sl] = db
            du = du + _dot(da, win_v[0, sl, :]) + _dot(db, win_v[1, sl, :])
        r, xn, y, _ = _norm_mod(x, g, sc, sh)
        dx, dsh, dsc, dg = _norm_mod_bwd(du, r, xn, y, g, sc)
        dhp_ref[...] = dh_v + dx
        red_ref[0:1, :] += dgt
        red_ref[1:2, :] += dsh
        red_ref[2:3, :] += dsc
        red_ref[3:4, :] += dg

    ab_spec = pl.BlockSpec((2, TS, Fd), lambda i: (0, i, 0))
    return pl.pallas_call(
        body, name=name, grid=(S // TS,),
        in_specs=[_row_tile(TS, D), _row_tile(TS, D), _row_tile(TS, D), ab_spec, _const((8, D)), ANY, ANY],
        out_specs=[_row_tile(TS, D), ab_spec, _row_tile(TS, D), _const((8, D))],
        out_shape=[jax.ShapeDtypeStruct((S, D), F32), jax.ShapeDtypeStruct((2, S, Fd), BF16),
                   jax.ShapeDtypeStruct((S, D), BF16), jax.ShapeDtypeStruct((8, D), F32)],
        scratch_shapes=[pltpu.VMEM(win.shape, BF16), pltpu.VMEM(wout.shape, BF16), pltpu.SemaphoreType.DMA((2,))],
        compiler_params=_params(),
    )(dh, h, f, ab, vec, win, wout)


def _wgrad(x, y, name, tm=None, ts=2048, rider=None):
    xb = x.ndim == 3
    nb = x.shape[0] if xb else 0
    S, M = x.shape[-2:]
    N = y.shape[-1]
    tm = tm or M
    ts = min(ts, S)
    nk = S // ts
    grid = (max(nb, 1), M // tm, nk)

    def body(*refs):
        if rider is None:
            return compute(*refs)
        host, mine = rider.split(refs, 2, 1)
        step = (pl.program_id(0) * grid[1] + pl.program_id(1)) * grid[2] + pl.program_id(2)
        rider.head(mine, step)
        compute(*host)
        rider.tail(mine, step, grid[0] * grid[1] * grid[2])

    def compute(x_ref, y_ref, o_ref, acc):
        k = pl.program_id(2)

        @pl.when(k == 0)
        def _():
            acc[...] = jnp.zeros_like(acc)

        acc[...] += _dot_tn(x_ref[...], y_ref[...])

        @pl.when(k == nk - 1)
        def _():
            o_ref[...] = acc[...].astype(BF16)

    x_spec = (pl.BlockSpec((None, ts, tm), lambda b, i, k: (b, k, i)) if xb
              else pl.BlockSpec((ts, tm), lambda b, i, k: (k, i)))
    y_spec = pl.BlockSpec((ts, N), lambda b, i, k: (k, 0))
    if xb:
        o_spec, o_shape = pl.BlockSpec((None, tm, N), lambda b, i, k: (b, i, 0)), (nb, M, N)
    else:
        o_spec, o_shape = pl.BlockSpec((tm, N), lambda b, i, k: (i, 0)), (M, N)
    specs = ([x_spec, y_spec], [o_spec], [jax.ShapeDtypeStruct(o_shape, BF16)], [pltpu.VMEM((tm, N), F32)])
    in_specs, out_specs, out_shape, scratch = specs if rider is None else rider.specs(*specs)
    outs = pl.pallas_call(
        body, name=name, grid=grid, in_specs=in_specs, out_specs=out_specs, out_shape=out_shape,
        scratch_shapes=scratch, compiler_params=_params(),
    )(x, y, *(rider.arrays if rider else []))
    return outs[0] if rider is None else (outs[0], outs[1:])


P_OFF, Q_OFF, K_OFF, V_OFF, G_OFF = 0, 256, 1024, 1792, 2560
IN_WIDTH = 4608


def _first_half_mask(ts):
    lane = lax.broadcasted_iota(jnp.int32, (ts, 128), 1)
    return (lane % HEAD_DIM) < (HEAD_DIM // 2)


def _rope(t, cos, sin_signed, first, sign):
    partner = jnp.where(first, pltpu.roll(t, 96, 1), pltpu.roll(t, 32, 1))
    return t * cos + sign * (partner * sin_signed)


def _res_spec(r):
    return pl.BlockSpec((r, TS // r, GW), lambda i: (0, i, 0))


def _res_shape(S, r, dtype):
    return jax.ShapeDtypeStruct((r, S // r, GW), dtype)


def _to_residues(piece, out_ref, lanes, r, scr):
    if r == 1:
        out_ref[0, :, lanes] = piece.astype(out_ref.dtype)
        return
    for h in range(piece.shape[1] // 128):
        scr[h] = piece[:, h * 128:(h + 1) * 128]
        at = slice(lanes.start + h * 128, lanes.start + (h + 1) * 128)
        for res in range(r):
            out_ref[res, :, at] = scr[h, pl.ds(res, TS // r, stride=r), :].astype(out_ref.dtype)


def _from_residues(in_ref, lanes, r, scr):
    if r == 1:
        return in_ref[0, :, lanes].astype(F32)
    halves = (lanes.stop - lanes.start) // 128
    for h in range(halves):
        at = slice(lanes.start + h * 128, lanes.start + (h + 1) * 128)
        for res in range(r):
            scr[h, pl.ds(res, TS // r, stride=r), :] = in_ref[res, :, at].astype(F32)
    return scr[0] if halves == 1 else jnp.concatenate([scr[0], scr[1]], axis=1)


RES_SCRATCH = (2, TS, 128)


def _mix_in_fwd(h, vec, cos, sin, win, name):
    S, D = h.shape

    def body(h_ref, vec_ref, cos_ref, sin_ref, win_hbm, u_ref, p_ref, gates_ref, *rest):
        qkv_refs, (win_v, sem, scr) = rest[:9], rest[9:]
        _load_weights([(win_hbm, win_v)], sem)
        g, sh, sc = (vec_ref[k:k + 1, :] for k in range(3))
        _, _, _, u = _norm_mod(h_ref[...], g, sc, sh)
        ub = u.astype(BF16)
        u_ref[...] = ub
        p_ref[...] = _dot_nt(ub, win_v[P_OFF:Q_OFF, :])
        cosv, sinv = cos_ref[...], sin_ref[...]
        first = _first_half_mask(TS)
        for which, off in enumerate((Q_OFF, K_OFF, V_OFF)):
            t = _dot_nt(ub, win_v[off:off + 3 * GW, :])
            for gi in range(3):
                for half in range(2):
                    c0 = gi * GW + half * 128
                    piece = t[:, c0:c0 + 128]
                    if which < 2:
                        piece = _rope(piece, cosv, sinv, first, 1.0)
                    _to_residues(piece, qkv_refs[which * 3 + gi], slice(half * 128, (half + 1) * 128),
                                 DILATIONS[gi], scr)
        gates_ref[...] = jax.nn.sigmoid(_dot_nt(ub, win_v[G_OFF:IN_WIDTH, :])).astype(BF16)

    return pl.pallas_call(
        body, name=name, grid=(S // TS,),
        in_specs=[_row_tile(TS, D), _const((8, D)), _row_tile(TS, 128), _row_tile(TS, 128), ANY],
        out_specs=[_row_tile(TS, D), _row_tile(TS, GW), _row_tile(TS, 2 * D)] + [_res_spec(r) for r in DILATIONS] * 3,
        out_shape=[jax.ShapeDtypeStruct((S, D), BF16), jax.ShapeDtypeStruct((S, GW), F32),
                   jax.ShapeDtypeStruct((S, 2 * D), BF16)] + [_res_shape(S, r, BF16) for r in DILATIONS] * 3,
        scratch_shapes=[pltpu.VMEM((IN_WIDTH, D), BF16), pltpu.SemaphoreType.DMA((1,)), pltpu.VMEM(RES_SCRATCH, F32)],
        compiler_params=_params(),
    )(h, vec, cos, sin, win)


def _mix_in_bwd(dh, h, vec, cos, sin, dp, dqkv, dgl, win, name):
    S, D = h.shape

    def body(dh_ref, h_ref, vec_ref, cos_ref, sin_ref, dp_ref, *rest):
        dqkv_refs = rest[:9]
        dgl_ref, win_hbm, dhp_ref, dproj_ref, red_ref, win_v, sem, scr = rest[9:]
        _load_weights([(win_hbm, win_v)], sem)

        @pl.when(pl.program_id(0) == 0)
        def _():
            red_ref[...] = jnp.zeros_like(red_ref)

        cosv, sinv = cos_ref[...], sin_ref[...]
        first = _first_half_mask(TS)
        dproj_ref[:, P_OFF:Q_OFF] = dp_ref[...].astype(BF16)
        for which, off in enumerate((Q_OFF, K_OFF, V_OFF)):
            for gi in range(3):
                for half in range(2):
                    piece = _from_residues(dqkv_refs[which * 3 + gi], slice(half * 128, (half + 1) * 128),
                                           DILATIONS[gi], scr)
                    if which < 2:
                        piece = _rope(piece, cosv, sinv, first, -1.0)
                    c0 = off + gi * GW + half * 128
                    dproj_ref[:, c0:c0 + 128] = piece.astype(BF16)
        dproj_ref[:, G_OFF:IN_WIDTH] = dgl_ref[...]
        du = _dot(dproj_ref[...], win_v[...])
        g, sh, sc = (vec_ref[k:k + 1, :] for k in range(3))
        r, xn, y, _ = _norm_mod(h_ref[...], g, sc, sh)
        dx, dsh, dsc, dg = _norm_mod_bwd(du, r, xn, y, g, sc)
        dhp_ref[...] = dh_ref[...] + dx
        red_ref[1:2, :] += dsh
        red_ref[2:3, :] += dsc
        red_ref[3:4, :] += dg

    return pl.pallas_call(
        body, name=name, grid=(S // TS,),
        in_specs=[_row_tile(TS, D), _row_tile(TS, D), _const((8, D)), _row_tile(TS, 128), _row_tile(TS, 128),
                  _row_tile(TS, GW)] + [_res_spec(r) for r in DILATIONS] * 3 + [_row_tile(TS, 2 * D), ANY],
        out_specs=[_row_tile(TS, D), _row_tile(TS, IN_WIDTH), _const((8, D))],
        out_shape=[jax.ShapeDtypeStruct((S, D), F32), jax.ShapeDtypeStruct((S, IN_WIDTH), BF16),
                   jax.ShapeDtypeStruct((8, D), F32)],
        scratch_shapes=[pltpu.VMEM((IN_WIDTH, D), BF16), pltpu.SemaphoreType.DMA((1,)), pltpu.VMEM(RES_SCRATCH, F32)],
        compiler_params=_params(),
    )(dh, h, vec, cos, sin, dp, *dqkv, dgl, win)


def _pool_lanes(rows):
    lane = lax.broadcasted_iota(jnp.int32, (rows, GW), 1)
    return lane // HEAD_DIM


def _pool_window(rows):
    grp = _pool_lanes(rows)
    w = jnp.full((rows, GW), POOL_WINDOWS[0], jnp.int32)
    for k in range(1, len(POOL_WINDOWS)):
        w = jnp.where(grp == k, POOL_WINDOWS[k], w)
    return grp, w


def _pool_fwd(p, wbd, scale, name, ts=512):
    S = p.shape[0]
    ext = ts + HALO

    def body(pc_ref, ph_ref, wbd_ref, sc_ref, d_ref, y_ref):
        i = pl.program_id(0)
        cur = pc_ref[...]
        halo = jnp.where(i > 0, ph_ref[...], 0.0)
        s = jnp.concatenate([halo, cur], axis=0)
        grp, w = _pool_window(ext)
        sel = jnp.zeros((ext, GW), F32)
        for k, wk in enumerate(POOL_WINDOWS):
            s = s + pltpu.roll(s, wk // 2, 0)
            sel = jnp.where(grp == k, s, sel)
        t = i * ts + lax.broadcasted_iota(jnp.int32, (ts, GW), 0)
        count = jnp.minimum(t + 1, w[HALO:]).astype(F32)
        d = (sel[HALO:] / count - cur).astype(BF16)
        d_ref[...] = d
        y_ref[...] = (_dot(d, wbd_ref[...]) * sc_ref[...]).astype(BF16)

    return pl.pallas_call(
        body, name=name, grid=(S // ts,),
        in_specs=[_row_tile(ts, GW),
                  pl.BlockSpec((HALO, GW), lambda i: (jnp.maximum(i * (ts // HALO) - 1, 0), 0)),
                  _const((GW, GW)), _const((1, GW))],
        out_specs=[_row_tile(ts, GW), _row_tile(ts, GW)],
        out_shape=[jax.ShapeDtypeStruct((S, GW), BF16), jax.ShapeDtypeStruct((S, GW), BF16)],
        compiler_params=_params(),
    )(p, p, wbd, scale)


def _pool_bwd(dy, d, wbd, scale, name, ts=512):
    S = dy.shape[0]
    ext = ts + HALO
    nsteps = S // ts
    last_halo = S // HALO - 1

    def body(dyc_ref, dyh_ref, d_ref, wbd_ref, sc_ref, dp_ref, dw_ref, ds_ref):
        i = pl.program_id(0)

        @pl.when(i == 0)
        def _():
            dw_ref[...] = jnp.zeros_like(dw_ref)
            ds_ref[...] = jnp.zeros_like(ds_ref)

        dyc = dyc_ref[...]
        dyh = jnp.where(i < nsteps - 1, dyh_ref[...], 0.0)
        dys = (jnp.concatenate([dyc, dyh], axis=0) * sc_ref[...]).astype(BF16)
        dd = _dot_nt(dys, wbd_ref[...])
        grp, w = _pool_window(ext)
        t = i * ts + lax.broadcasted_iota(jnp.int32, (ext, GW), 0)
        s = dd / jnp.minimum(t + 1, w).astype(F32)
        sel = jnp.zeros((ext, GW), F32)
        for k, wk in enumerate(POOL_WINDOWS):
            s = s + pltpu.roll(s, ext - wk // 2, 0)
            sel = jnp.where(grp == k, s, sel)
        dp_ref[...] = sel[:ts] - dd[:ts]
        dv = d_ref[...]
        z = _dot(dv, wbd_ref[...])
        ds_ref[0:1, :] += jnp.sum(dyc * z, axis=0, keepdims=True)
        dw_ref[...] += _dot_tn(dv, dys[:ts])

    return pl.pallas_call(
        body, name=name, grid=(nsteps,),
        in_specs=[_row_tile(ts, GW),
                  pl.BlockSpec((HALO, GW), lambda i: (jnp.minimum((i + 1) * (ts // HALO), last_halo), 0)),
                  _row_tile(ts, GW), _const((GW, GW)), _const((1, GW))],
        out_specs=[_row_tile(ts, GW), _const((GW, GW)), _const((8, GW))],
        out_shape=[jax.ShapeDtypeStruct((S, GW), F32), jax.ShapeDtypeStruct((GW, GW), F32),
                   jax.ShapeDtypeStruct((8, GW), F32)],
        compiler_params=_params(),
    )(dy, dy, d, wbd, scale)


def _head_id(rows):
    return lax.broadcasted_iota(jnp.int32, (rows, GW), 1) // HEAD_DIM


def _stack_heads(t, hid):
    return jnp.concatenate([jnp.where(hid == h, t, jnp.zeros_like(t)) for h in range(HEADS)], axis=0)


def _unstack_heads(t_all, hid):
    out = jnp.zeros((QB, GW), F32)
    for h in range(HEADS):
        out = jnp.where(hid == h, t_all[h * QB:(h + 1) * QB], out)
    return out


def _band_mask(n):
    row = lax.broadcasted_iota(jnp.int32, (HEADS * QB, 2 * QB), 0) % QB
    col = lax.broadcasted_iota(jnp.int32, (HEADS * QB, 2 * QB), 1)
    rel = row + QB - col
    return (rel >= 0) & (rel <= BAND) & ((col >= QB) | (n > 0))


MAX_STREAMS = 8


def _streams(r, nb):
    if r > 1:
        ns = min(r, MAX_STREAMS)
        return nb, [(lambda rb, l=l: ns * rb + l, 0) for l in range(ns)]
    ns = min(MAX_STREAMS, nb)
    return nb // ns, [(lambda rb: 0, l * (nb // ns)) for l in range(ns)]


def _attn_fwd(q, k, v, name):
    r, L, _ = q.shape
    nbs, streams = _streams(r, L // QB)
    ns = len(streams)
    grid = (max(r // ns, 1), nbs)

    def cur(res, off):
        return pl.BlockSpec((None, QB, GW), lambda rb, n: (res(rb), n + off, 0))

    def prev(res, off):
        return pl.BlockSpec((None, QB, GW), lambda rb, n: (res(rb), jnp.maximum(n + off - 1, 0), 0))

    def body(*refs):
        n = pl.program_id(1)
        hid = _head_id(QB)
        o_ref, lse_ref = refs[5 * len(streams):]
        for l, (_, off) in enumerate(streams):
            q_ref, kp_ref, kc_ref, vp_ref, vc_ref = refs[5 * l:5 * l + 5]
            qs = _stack_heads(q_ref[...], hid)
            kc = jnp.concatenate([kp_ref[...], kc_ref[...]], axis=0)
            vc = jnp.concatenate([vp_ref[...], vc_ref[...]], axis=0)
            s = _dot_nt(qs, kc) * (HEAD_DIM ** -0.5)
            s = jnp.where(_band_mask(n + off), s, -jnp.inf)
            m = jnp.max(s, axis=-1, keepdims=True)
            e = jnp.exp(s - m)
            den = jnp.sum(e, axis=-1, keepdims=True)
            lse = m + jnp.log(den)
            pr = (e * (1.0 / den)).astype(BF16)
            o_ref[l] = _unstack_heads(_dot(pr, vc), hid)
            lse_ref[l] = _unstack_heads(jnp.broadcast_to(lse, (HEADS * QB, GW)), hid)

    in_specs, args = [], []
    for res, off in streams:
        in_specs += [cur(res, off), prev(res, off), cur(res, off), prev(res, off), cur(res, off)]
        args += [q, k, k, v, v]
    out = jax.ShapeDtypeStruct((ns * grid[0], nbs * QB, GW), F32)
    both = pl.BlockSpec((ns, QB, GW), lambda rb, n: (rb, n, 0))
    o, lse = pl.pallas_call(
        body, name=name, grid=grid, in_specs=in_specs, out_specs=[both, both], out_shape=[out, out],
        compiler_params=_params(),
    )(*args)
    return o.reshape(q.shape), lse.reshape(q.shape)


def _head_rows(t_full, hid):
    return jnp.concatenate(
        [jnp.max(jnp.where(hid == h, t_full, -jnp.inf), axis=-1, keepdims=True) for h in range(HEADS)], axis=0)


def _attn_bwd(q, k, v, do, lse, cterm, name):
    r, L, _ = q.shape
    nbs, streams = _streams(r, L // QB)
    ns = len(streams)
    parts = r == 1

    def spec(res, index):
        return pl.BlockSpec((None, QB, GW), lambda rb, n: (res(rb), index(n), 0))

    def body(*refs):
        dq_ref, dk_ref, dv_ref, carry_k, carry_v, seam_k, seam_v = refs[8 * ns:]
        n = pl.program_id(1)

        @pl.when(n == 0)
        def _():
            carry_k[...] = jnp.zeros_like(carry_k)
            carry_v[...] = jnp.zeros_like(carry_v)

        @pl.when(n < nbs)
        def _():
            hid = _head_id(QB)
            for l, (_, off) in enumerate(streams):
                q_ref, do_ref, lse_ref, c_ref, kp_ref, kc_ref, vp_ref, vc_ref = refs[8 * l:8 * l + 8]
                qs = _stack_heads(q_ref[...], hid)
                dos = _stack_heads(do_ref[...], hid)
                kc = jnp.concatenate([kp_ref[...], kc_ref[...]], axis=0)
                vc = jnp.concatenate([vp_ref[...], vc_ref[...]], axis=0)
                s = _dot_nt(qs, kc) * (HEAD_DIM ** -0.5)
                s = jnp.where(_band_mask(n + off), s, -jnp.inf)
                p = jnp.exp(s - _head_rows(lse_ref[...], hid))
                dp = _dot_nt(dos, vc)
                ds = (p * (dp + _head_rows(c_ref[...], hid)) * (HEAD_DIM ** -0.5)).astype(BF16)
                dq_ref[l] = _unstack_heads(_dot(ds, kc), hid)
                dkc = _dot_tn(ds, qs)
                dvc = _dot_tn(p.astype(BF16), dos)
                if parts and l > 0:
                    @pl.when(n == 0)
                    def _():
                        seam_k[l] = dkc[:QB]
                        seam_v[l] = dvc[:QB]
                dk_ref[l] = carry_k[l] + dkc[:QB]
                dv_ref[l] = carry_v[l] + dvc[:QB]
                carry_k[l] = dkc[QB:]
                carry_v[l] = dvc[QB:]

        @pl.when(n == nbs)
        def _():
            for l in range(ns):
                if parts and l + 1 < ns:
                    dk_ref[l] = carry_k[l] + seam_k[l + 1]
                    dv_ref[l] = carry_v[l] + seam_v[l + 1]
                else:
                    dk_ref[l] = carry_k[l]
                    dv_ref[l] = carry_v[l]

    in_specs, args = [], []
    for res, off in streams:
        qside = functools.partial(lambda n, off: jnp.minimum(n, nbs - 1) + off, off=off)
        kprev = functools.partial(lambda n, off: jnp.maximum(jnp.minimum(n, nbs) - 1 + off, 0), off=off)
        in_specs += [spec(res, qside)] * 4 + [spec(res, kprev), spec(res, qside)] * 2
        args += [q, do, lse, cterm, k, k, v, v]
    out = jax.ShapeDtypeStruct((ns * max(r // ns, 1), nbs * QB, GW), F32)
    qout = pl.BlockSpec((ns, QB, GW), lambda rb, n: (rb, jnp.minimum(n, nbs - 1), 0))
    kout = pl.BlockSpec((ns, QB, GW), lambda rb, n: (rb, jnp.maximum(n - 1, 0), 0))
    buf = pltpu.VMEM((ns, QB, GW), F32)
    outs = pl.pallas_call(
        body, name=name, grid=(max(r // ns, 1), nbs + 1),
        in_specs=in_specs, out_specs=[qout, kout, kout], out_shape=[out, out, out],
        scratch_shapes=[buf, buf, buf, buf],
        compiler_params=_params(),
    )(*args)
    return [t.reshape(q.shape) for t in outs]


def _token_order(refs, scr):
    return [_from_residues(ref, slice(0, GW), r, scr) for ref, r in zip(refs, DILATIONS)]


def _group_weights(lses):
    l0, l1, l2 = lses
    m = jnp.maximum(jnp.maximum(l0, l1), l2)
    e = [jnp.exp(l - m) for l in (l0, l1, l2)]
    den = e[0] + e[1] + e[2]
    return [ei / den for ei in e]


def _mix_out_fwd(h, vec, gates, ypool, o3, lse3, wpb, wab, wout, name):
    S, D = h.shape

    def body(h_ref, vec_ref, gates_ref, yp_ref, o0, o1, o2, l0, l1, l2, wpb_hbm, wab_hbm, wout_hbm,
             hn_ref, ya_ref, merged_ref, tm_ref, wpb_v, wab_v, wout_v, sem, scr):
        _load_weights([(wpb_hbm, wpb_v), (wab_hbm, wab_v), (wout_hbm, wout_v)], sem)
        gt = vec_ref[3:4, :]
        wts = _group_weights(_token_order((l0, l1, l2), scr))
        og = _token_order((o0, o1, o2), scr)
        ya = (wts[0] * og[0] + wts[1] * og[1] + wts[2] * og[2]).astype(BF16)
        ya_ref[...] = ya
        merged = (gates_ref[:, :D].astype(F32) * _dot(yp_ref[...], wpb_v[...])
                  + gates_ref[:, D:].astype(F32) * _dot(ya, wab_v[...])).astype(BF16)
        merged_ref[...] = merged
        tm = _dot(merged, wout_v[...])
        tm_ref[...] = tm.astype(BF16)
        hn_ref[...] = h_ref[...] + gt * tm

    grp = _row_tile(TS, GW)
    res = [_res_spec(r) for r in DILATIONS]
    return pl.pallas_call(
        body, name=name, grid=(S // TS,),
        in_specs=[_row_tile(TS, D), _const((8, D)), _row_tile(TS, 2 * D), grp] + res * 2 + [ANY, ANY, ANY],
        out_specs=[_row_tile(TS, D), grp, _row_tile(TS, D), _row_tile(TS, D)],
        out_shape=[jax.ShapeDtypeStruct((S, D), F32), jax.ShapeDtypeStruct((S, GW), BF16),
                   jax.ShapeDtypeStruct((S, D), BF16), jax.ShapeDtypeStruct((S, D), BF16)],
        scratch_shapes=[pltpu.VMEM((GW, D), BF16), pltpu.VMEM((GW, D), BF16), pltpu.VMEM((D, D), BF16),
                        pltpu.SemaphoreType.DMA((3,)), pltpu.VMEM(RES_SCRATCH, F32)],
        compiler_params=_params(),
    )(h, vec, gates, ypool, *o3, *lse3, wpb, wab, wout)


def _mix_out_bwd(dh, tm, vec, gates, ypool, o3, lse3, wpb, wab, wout, name, rider=None):
    S, D = dh.shape

    def body(*refs):
        if rider is None:
            return compute(*refs)
        host, mine = rider.split(refs, 14, 12)
        rider.head(mine, pl.program_id(0))
        compute(*host)
        rider.tail(mine, pl.program_id(0), S // TS)

    def compute(dh_ref, tm_ref, vec_ref, gates_ref, yp_ref, o0, o1, o2, l0, l1, l2, wpb_hbm, wab_hbm, wout_hbm,
                dtm_ref, dgl_ref, dypb_ref, dyab_ref, dyp_ref, do0, do1, do2, c0, c1, c2, red_ref,
                wpb_v, wab_v, wout_v, sem, scr):
        _load_weights([(wpb_hbm, wpb_v), (wab_hbm, wab_v), (wout_hbm, wout_v)], sem)

        @pl.when(pl.program_id(0) == 0)
        def _():
            red_ref[...] = jnp.zeros_like(red_ref)

        gt = vec_ref[3:4, :]
        dh_v = dh_ref[...]
        red_ref[0:1, :] += jnp.sum(tm_ref[...].astype(F32) * dh_v, axis=0, keepdims=True)
        dtm = (gt * dh_v).astype(BF16)
        dtm_ref[...] = dtm
        dm = _dot_nt(dtm, wout_v[...])
        wts = _group_weights(_token_order((l0, l1, l2), scr))
        og = _token_order((o0, o1, o2), scr)
        ya = wts[0] * og[0] + wts[1] * og[1] + wts[2] * og[2]
        ypb = _dot(yp_ref[...], wpb_v[...])
        yab = _dot(ya.astype(BF16), wab_v[...])
        gp = gates_ref[:, :D].astype(F32)
        ga = gates_ref[:, D:].astype(F32)
        dgl_ref[:, :D] = (dm * ypb * gp * (1.0 - gp)).astype(BF16)
        dgl_ref[:, D:] = (dm * yab * ga * (1.0 - ga)).astype(BF16)
        dypb = (dm * gp).astype(BF16)
        dyab = (dm * ga).astype(BF16)
        dypb_ref[...] = dypb
        dyab_ref[...] = dyab
        dyp_ref[...] = _dot_nt(dypb, wpb_v[...])
        dya = _dot_nt(dyab, wab_v[...])
        row = lax.broadcasted_iota(jnp.int32, (GW, GW), 0) // HEAD_DIM
        col = lax.broadcasted_iota(jnp.int32, (GW, GW), 1) // HEAD_DIM
        ones = jnp.where(row == col, 1.0, 0.0).astype(F32)
        tot = jnp.dot(dya * ya, ones, preferred_element_type=F32, precision=lax.Precision.HIGHEST)
        for wg, do_ref, c_ref, r in zip(wts, (do0, do1, do2), (c0, c1, c2), DILATIONS):
            _to_residues(wg * dya, do_ref, slice(0, GW), r, scr)
            _to_residues(-(wg * tot), c_ref, slice(0, GW), r, scr)

    grp = _row_tile(TS, GW)
    res = [_res_spec(r) for r in DILATIONS]
    specs = (
        [_row_tile(TS, D), _row_tile(TS, D), _const((8, D)), _row_tile(TS, 2 * D), grp] + res * 2
        + [ANY, ANY, ANY],
        [_row_tile(TS, D), _row_tile(TS, 2 * D), _row_tile(TS, D), _row_tile(TS, D), grp]
        + res * 2 + [_const((8, D))],
        [jax.ShapeDtypeStruct((S, D), BF16), jax.ShapeDtypeStruct((S, 2 * D), BF16),
         jax.ShapeDtypeStruct((S, D), BF16), jax.ShapeDtypeStruct((S, D), BF16), jax.ShapeDtypeStruct((S, GW), F32)]
        + [_res_shape(S, r, BF16) for r in DILATIONS] + [_res_shape(S, r, F32) for r in DILATIONS]
        + [jax.ShapeDtypeStruct((8, D), F32)],
        [pltpu.VMEM((GW, D), BF16), pltpu.VMEM((GW, D), BF16), pltpu.VMEM((D, D), BF16),
         pltpu.SemaphoreType.DMA((3,)), pltpu.VMEM(RES_SCRATCH, F32)])
    in_specs, out_specs, out_shape, scratch = specs if rider is None else rider.specs(*specs)
    outs = pl.pallas_call(
        body, name=name, grid=(S // TS,), in_specs=in_specs, out_specs=out_specs, out_shape=out_shape,
        scratch_shapes=scratch, compiler_params=_params(),
    )(dh, tm, vec, gates, ypool, *o3, *lse3, wpb, wab, wout, *(rider.arrays if rider else []))
    return outs if rider is None else (outs[:12], outs[12:])


def _ada_mod(c_all, w, b, name):
    def body(c_ref, w_ref, b_ref, cond_ref, mod_ref):
        cv = c_ref[...]
        cond = cv * jax.nn.sigmoid(cv)
        cond_ref[...] = cond
        mod_ref[...] = jnp.dot(cond, w_ref[...], preferred_element_type=F32,
                               precision=lax.Precision.HIGHEST) + b_ref[...]

    return pl.pallas_call(
        body, name=name,
        out_shape=[jax.ShapeDtypeStruct(c_all.shape, F32), jax.ShapeDtypeStruct((c_all.shape[0], w.shape[1]), F32)],
        compiler_params=_params(),
    )(c_all, w, b)


def _adamw_math(w, g, m, v):
    m = ADAM_B1 * m + (1.0 - ADAM_B1) * g
    v = ADAM_B2 * v + (1.0 - ADAM_B2) * (g * g)
    m_hat = m / (1.0 - ADAM_B1 ** ADAM_STEP)
    v_hat = v / (1.0 - ADAM_B2 ** ADAM_STEP)
    delta = -ADAM_LR * (m_hat / (jnp.sqrt(v_hat) + ADAM_EPS) + ADAM_WD * w)
    return delta, m, v


def _adamw(w, g, m, v, name):
    R, C = w.shape
    tr = R
    for cand in (256, 128, 64, 32, 16, 8):
        if R % cand == 0:
            tr = cand
            break

    def body(w_ref, g_ref, m_ref, v_ref, d_ref, mo_ref, vo_ref):
        d_ref[...], mo_ref[...], vo_ref[...] = _adamw_math(w_ref[...], g_ref[...], m_ref[...], v_ref[...])

    spec = _row_tile(tr, C)
    out = jax.ShapeDtypeStruct((R, C), F32)
    return pl.pallas_call(
        body, name=name, grid=(R // tr,), in_specs=[spec] * 4, out_specs=[spec] * 3, out_shape=[out] * 3,
        compiler_params=_params(),
    )(w, g, m, v)


def _ada_grad_adamw(cond_t, dmod, w, m, v, name, tr=256):
    R, C = w.shape
    nb = dmod.shape[0]

    def body(ct_ref, dm_ref, w_ref, m_ref, v_ref, g_ref, d_ref, mo_ref, vo_ref):
        ct = ct_ref[...]
        dm = dm_ref[...]
        g = jnp.zeros((tr, C), F32)
        for bi in range(nb):
            g = g + ct[:, bi:bi + 1] * dm[bi:bi + 1, :]
        g_ref[...] = g
        d_ref[...], mo_ref[...], vo_ref[...] = _adamw_math(w_ref[...], g, m_ref[...], v_ref[...])

    spec = _row_tile(tr, C)
    out = jax.ShapeDtypeStruct((R, C), F32)
    return pl.pallas_call(
        body, name=name, grid=(R // tr,),
        in_specs=[_row_tile(tr, nb), _const((nb, C)), spec, spec, spec],
        out_specs=[spec] * 4, out_shape=[out] * 4,
        compiler_params=_params(),
    )(cond_t, dmod, w, m, v)


def _row_step(rows, cap=256):
    for cand in range(cap, 15, -16):
        if rows % cand == 0:
            return cand
    return rows


def _slot_sum(x_ref):
    acc = x_ref[0].astype(F32)
    for k in range(1, x_ref.shape[0]):
        acc = acc + x_ref[k].astype(F32)
    return acc


def _sum_slots(x, name, out_dtype=F32):
    n, R, C = x.shape
    tr = _row_step(R)

    def body(x_ref, o_ref):
        o_ref[...] = _slot_sum(x_ref).astype(out_dtype)

    return pl.pallas_call(
        body, name=name, grid=(R // tr,),
        in_specs=[pl.BlockSpec((n, tr, C), lambda i: (0, i, 0))],
        out_specs=_row_tile(tr, C), out_shape=jax.ShapeDtypeStruct((R, C), out_dtype),
        compiler_params=_params(),
    )(x)


def _sum_pair(core, g, recv, name):
    _, _, R, C = g.shape
    tr = _row_step(R, cap=1024)

    def body(core_ref, g_ref, r_ref, o_ref):
        o_ref[...] = (g_ref[...].astype(F32) + r_ref[...].astype(F32)).astype(BF16)

    return pl.pallas_call(
        body, name=name, out_shape=jax.ShapeDtypeStruct((4, R, C), BF16),
        grid_spec=pltpu.PrefetchScalarGridSpec(
            num_scalar_prefetch=1, grid=(4, R // tr),
            in_specs=[pl.BlockSpec((None, None, tr, C), lambda k, i, core_ref: (k, core_ref[0], i, 0)),
                      pl.BlockSpec((None, tr, C), lambda k, i, core_ref: (k, i, 0))],
            out_specs=pl.BlockSpec((None, tr, C), lambda k, i, core_ref: (k, i, 0))),
        compiler_params=_params(),
    )(core, g, recv)


def _sum_adamw(chip, own, recv, w, m, v, name):
    _, R, C = own.shape
    tr = _row_step(R, cap=512)

    def body(chip_ref, own_ref, r_ref, w_ref, m_ref, v_ref, g_ref, d_ref, mo_ref, vo_ref):
        g = own_ref[...].astype(F32) + _slot_sum(r_ref)
        g_ref[...] = g
        d_ref[...], mo_ref[...], vo_ref[...] = _adamw_math(w_ref[...], g, m_ref[...], v_ref[...])

    spec = pl.BlockSpec((tr, C), lambda i, chip_ref: (i, 0))
    out = jax.ShapeDtypeStruct((R, C), F32)
    return pl.pallas_call(
        body, name=name, out_shape=[out] * 4,
        grid_spec=pltpu.PrefetchScalarGridSpec(
            num_scalar_prefetch=1, grid=(R // tr,),
            in_specs=[pl.BlockSpec((None, tr, C), lambda i, chip_ref: (chip_ref[0], i, 0)),
                      pl.BlockSpec((3, tr, C), lambda i, chip_ref: (0, i, 0)), spec, spec, spec],
            out_specs=[spec] * 4),
        compiler_params=_params(),
    )(chip, own, recv, w, m, v)


def _place():
    return lax.axis_index("x"), lax.axis_index("y"), lax.axis_index("c")


def _gather_steps(x_refs, out_refs, send_sems, recv_sems):
    n = len(x_refs)
    x, y, c = _place()
    me, sibling = (x, y, c), (x, y, 1 - c)
    chips = [(1 - x, y), (x, 1 - y), (1 - x, 1 - y)]

    def rows(a, px, py, pc):
        return out_refs[a].at[4 * px + 2 * py + pc]

    def copy(a, k, block, to, src=None):
        return pltpu.make_async_remote_copy(
            src_ref=rows(a, *block) if src is None else src, dst_ref=rows(a, *block),
            send_sem=send_sems.at[a, k], recv_sem=recv_sems.at[a, k], device_id=to, device_id_type=MESH)

    def first(a):
        return [copy(a, 0, me, sibling, src=x_refs[a])] + [
            copy(a, 1 + j, me, (*chip, c), src=x_refs[a]) for j, chip in enumerate(chips)]

    def passed(a, j):
        return copy(a, 4 + j, (*chips[j], c), sibling)

    def start():
        for a in range(n):
            for cp in first(a):
                cp.start()

    def relay():
        for j, chip in enumerate(chips):
            for a in range(n):
                copy(a, 1 + j, (*chip, c), me).wait_recv()
                passed(a, j).start()

    def finish():
        for a in range(n):
            copy(a, 0, sibling, me).wait_recv()
            for j, chip in enumerate(chips):
                copy(a, 4 + j, (*chip, 1 - c), me).wait_recv()
        for a in range(n):
            for cp in first(a) + [passed(a, j) for j in range(3)]:
                cp.wait_send()

    return start, relay, finish


def _all_gather(arrs, name, own=True):
    n = len(arrs)

    def body(*refs):
        x_refs, out_refs = refs[:n], refs[n:2 * n]
        send_sems, recv_sems, local_sems = refs[2 * n:]
        me = 4 * lax.axis_index("x") + 2 * lax.axis_index("y") + lax.axis_index("c")
        mine = [pltpu.make_async_copy(x_refs[a], out_refs[a].at[me], local_sems.at[a]) for a in range(n)] if own else []
        for cp in mine:
            cp.start()
        for step in _gather_steps(x_refs, out_refs, send_sems, recv_sems):
            step()
        for cp in mine:
            cp.wait()

    return pl.pallas_call(
        body, name=name, out_shape=[jax.ShapeDtypeStruct((N_DEV,) + t.shape, t.dtype) for t in arrs],
        in_specs=[ANY] * n, out_specs=[ANY] * n,
        scratch_shapes=[pltpu.SemaphoreType.DMA((n, 7)), pltpu.SemaphoreType.DMA((n, 7)),
                        pltpu.SemaphoreType.DMA((n,))],
    )(*arrs)


def _pair_exchange(arrs, name):
    n = len(arrs)

    def body(*refs):
        g_refs, out_refs = refs[:n], refs[n:2 * n]
        send_sems, recv_sems = refs[2 * n:]
        x, y, c = _place()
        give = [pltpu.make_async_remote_copy(
            src_ref=g_refs[a].at[pl.ds(0, 4), 1 - c], dst_ref=out_refs[a], send_sem=send_sems.at[a],
            recv_sem=recv_sems.at[a], device_id=(x, y, 1 - c), device_id_type=MESH) for a in range(n)]
        for cp in give:
            cp.start()
        for cp in give:
            cp.wait()

    return pl.pallas_call(
        body, name=name,
        out_shape=[jax.ShapeDtypeStruct((4,) + t.shape[2:], t.dtype) for t in arrs],
        in_specs=[ANY] * n, out_specs=[ANY] * n,
        scratch_shapes=[pltpu.SemaphoreType.DMA((n,)), pltpu.SemaphoreType.DMA((n,))],
    )(*arrs)


def _chip_exchange_steps(p_refs, out_refs, send_sems, recv_sems):
    x, y, c = _place()
    chips = [(1 - x, y), (x, 1 - y), (1 - x, 1 - y)]

    def copies():
        return [pltpu.make_async_remote_copy(
            src_ref=p_refs[a].at[2 * px + py], dst_ref=out_refs[a].at[j], send_sem=send_sems.at[a, j],
            recv_sem=recv_sems.at[a, j], device_id=(px, py, c), device_id_type=MESH)
            for a in range(len(p_refs)) for j, (px, py) in enumerate(chips)]

    def start():
        for cp in copies():
            cp.start()

    def finish():
        for cp in copies():
            cp.wait()

    return start, finish


def _chip_exchange(arrs, name):
    n = len(arrs)

    def body(*refs):
        for step in _chip_exchange_steps(refs[:n], refs[n:2 * n], *refs[2 * n:]):
            step()

    return pl.pallas_call(
        body, name=name, out_shape=[jax.ShapeDtypeStruct((3,) + t.shape[1:], t.dtype) for t in arrs],
        in_specs=[ANY] * n, out_specs=[ANY] * n,
        scratch_shapes=[pltpu.SemaphoreType.DMA((n, 3)), pltpu.SemaphoreType.DMA((n, 3))],
    )(*arrs)


class _Rider:
    def __init__(self, arrays, out_shape, sems, steps, relay_before_end=None):
        self.arrays, self.out_shape, self.scratch, self.steps = list(arrays), out_shape, sems, steps
        self.n = len(self.arrays)
        self.relay_before_end = relay_before_end

    def specs(self, in_specs, out_specs, out_shape, scratch):
        extra = [ANY] * self.n
        return in_specs + extra, out_specs + extra, out_shape + self.out_shape, scratch + self.scratch

    def split(self, refs, n_in, n_out):
        k = self.n
        a, b = n_in + k, n_in + k + n_out
        return refs[:n_in] + refs[a:b] + refs[b + k:-2], (refs[n_in:a], refs[b:b + k], refs[-2:])

    def head(self, mine, step):
        pl.when(step == 0)(self.steps(mine[0], mine[1], *mine[2])[0])

    def tail(self, mine, step, nsteps):
        steps = self.steps(mine[0], mine[1], *mine[2])
        if self.relay_before_end is not None:
            pl.when(step == nsteps - 1 - self.relay_before_end)(steps[1])
        pl.when(step == nsteps - 1)(steps[-1])


def _gather_rider(arrs, relay_before_end=4):
    n = len(arrs)
    return _Rider(arrs, [jax.ShapeDtypeStruct((N_DEV,) + t.shape, t.dtype) for t in arrs],
                  [pltpu.SemaphoreType.DMA((n, 7)), pltpu.SemaphoreType.DMA((n, 7))], _gather_steps,
                  relay_before_end)


def _chip_exchange_rider(arrs):
    n = len(arrs)
    return _Rider(arrs, [jax.ShapeDtypeStruct((3,) + t.shape[1:], t.dtype) for t in arrs],
                  [pltpu.SemaphoreType.DMA((n, 3)), pltpu.SemaphoreType.DMA((n, 3))], _chip_exchange_steps)


def _rope_tables(positions):
    inv_freq = ROPE_THETA ** (-jnp.arange(0, HEAD_DIM, 2, dtype=F32) / HEAD_DIM)
    ang = positions.astype(F32)[:, None] * inv_freq
    cos, sin = jnp.cos(ang), jnp.sin(ang)
    return jnp.tile(cos, (1, 4)), jnp.tile(jnp.concatenate([-sin, sin], axis=1), (1, 2))


def _vec(g, shift, scale, gate):
    z = jnp.zeros_like(g)
    return jnp.stack([g, shift, scale, gate, z, z, z, z])


class _GradReducer:
    def __init__(self):
        self.core = lax.axis_index("c").astype(jnp.int32).reshape(1)
        self.own, self.others, self.waiting, self.riding = {}, {}, [], []

    def pair(self, named):
        keys = list(named)
        mine = [named[k].reshape((4, 2) + named[k].shape[1:]) for k in keys]
        theirs = _pair_exchange(mine, "reduce_pair_" + keys[0])
        for k, g, r in zip(keys, mine, theirs):
            self.own[k] = _sum_pair(self.core, g, r, "sum_pair_" + k)
        self.waiting += keys

    def rider(self):
        self.riding, self.waiting = self.waiting, []
        return _chip_exchange_rider([self.own[k] for k in self.riding])

    def landed(self, results):
        self.others.update(zip(self.riding, results))

    def flush(self, name):
        keys, self.waiting = self.waiting, []
        self.others.update(zip(keys, _chip_exchange([self.own[k] for k in keys], name)))


def _by_owner(g):
    if g.ndim == 3:
        return g if g.shape[0] == N_DEV else g.reshape(N_DEV, g.shape[1] * g.shape[0] // N_DEV, g.shape[2])
    return g.reshape(N_DEV, g.shape[0] // N_DEV, g.shape[1])


def _local_step(x, target, positions, mod, small, W, late=None, red=None):
    S, D = x.shape
    sh1, sc1, gt1, sh2, sc2, gt2, sh3, sc3, gt3 = (mod[k] for k in range(9))
    v1 = _vec(small["g1"], sh1, sc1, gt1)
    v2 = _vec(small["g2"], sh2, sc2, gt2)
    v3 = _vec(small["g3"], sh3, sc3, gt3)
    vf = _vec(small["gf"], small["gf"], small["gf"], small["gf"])
    cos, sin = _rope_tables(positions)
    wbd = jax.scipy.linalg.block_diag(*[small["w_pool"][k] for k in range(4)]).astype(BF16)
    pscale = small["pool_scale"].reshape(1, GW)

    if late is None:
        h1, u1, ab1, act1, f1 = _ffn_fwd(x, v1, W["w1in"], W["w1out"], "ffn1_fwd")
    else:
        (h1, u1, ab1, act1, f1), landed = _ffn_fwd(x, v1, W["w1in"], W["w1out"], "ffn1_fwd", rider=late[0])
        W = {**W, **late[1](landed)}
    u2, p, gates, *qkv = _mix_in_fwd(h1, v2, cos, sin, W["win"], "mix_in_fwd")
    dpool, ypool = _pool_fwd(p, wbd, pscale, "pool_fwd")
    o3, lse3 = [], []
    for gi in range(len(DILATIONS)):
        o, lse = _attn_fwd(qkv[gi], qkv[3 + gi], qkv[6 + gi], f"attn_fwd_{gi}")
        o3.append(o)
        lse3.append(lse)
    h2, ya, merged, tm = _mix_out_fwd(h1, v2, gates, ypool, o3, lse3, W["wpb"], W["wab"], W["wout"], "mix_out_fwd")
    dh3, u3, ab3, act3, f3, loss_blk, dgf = _ffn_fwd(h2, v3, W["w2in"], W["w2out"], "ffn2_fwd", final=(target, vf))

    dh2, dab3, df3, red3 = _ffn_bwd(dh3, h2, f3, ab3, v3, W["w2in"], W["w2out"], "ffn2_bwd")
    half_f = ab3.shape[2] // 2
    G = {"w2in": _by_owner(_wgrad(dab3, u3, "wgrad_2in", tm=half_f)),
         "w2out": _by_owner(_wgrad(act3, df3, "wgrad_2out", tm=half_f))}
    mix_out_args = (dh2, tm, v2, gates, ypool, o3, lse3, W["wpb"], W["wab"], W["wout"], "mix_out_bwd")
    if red is None:
        mix_out = _mix_out_bwd(*mix_out_args)
    else:
        red.pair({k: G[k] for k in ("w2in", "w2out")})
        mix_out, landed = _mix_out_bwd(*mix_out_args, rider=red.rider())
        red.landed(landed)
    (dtm, dgl, dypb, dyab, dyp, do0, do1, do2, c0, c1, c2, red2o) = mix_out
    dq3, dk3, dv3 = [], [], []
    for gi, (do, ct) in enumerate(zip((do0, do1, do2), (c0, c1, c2))):
        dq, dk, dv = _attn_bwd(qkv[gi], qkv[3 + gi], qkv[6 + gi], do, lse3[gi], ct, f"attn_bwd_{gi}")
        dq3.append(dq)
        dk3.append(dk)
        dv3.append(dv)
    dp, dwbd, dps = _pool_bwd(dyp, dpool, wbd, pscale, "pool_bwd")
    dh1, dproj, red2i = _mix_in_bwd(dh2, h1, v2, cos, sin, dp, dq3 + dk3 + dv3, dgl, W["win"], "mix_in_bwd")
    G["win"] = _by_owner(_wgrad(dproj, u2, "wgrad_in", tm=1152))
    G["wpb"] = _full_to_cols(_wgrad(ypool, dypb, "wgrad_pb"))
    G["wab"] = _full_to_cols(_wgrad(ya, dyab, "wgrad_ab"))
    G["wout"] = _by_owner(_wgrad(merged, dtm, "wgrad_out"))
    if red is not None:
        red.pair({k: G[k] for k in ("win", "wpb", "wab", "wout")})
    dx, dab1, df1, red1 = _ffn_bwd(dh1, x, f1, ab1, v1, W["w1in"], W["w1out"], "ffn1_bwd")
    if red is None:
        G["w1in"] = _by_owner(_wgrad(dab1, u1, "wgrad_1in", tm=half_f))
        G["w1out"] = _by_owner(_wgrad(act1, df1, "wgrad_1out", tm=half_f))
    else:
        g1in, landed = _wgrad(dab1, u1, "wgrad_1in", tm=half_f, rider=red.rider())
        red.landed(landed)
        G["w1in"] = _by_owner(g1in)
        red.pair({"w1in": G["w1in"]})
        g1out, landed = _wgrad(act1, df1, "wgrad_1out", tm=half_f, rider=red.rider())
        red.landed(landed)
        G["w1out"] = _by_owner(g1out)
        red.pair({"w1out": G["w1out"]})
        red.flush("reduce_chips_w1out")
    dmod = jnp.stack([red1[1], red1[2], red1[0], red2i[1], red2i[2], red2o[0], red3[1], red3[2], red3[0]])
    dsmall = {
        "g1": red1[3], "g2": red2i[3], "g3": red3[3], "gf": dgf[0],
        "w_pool": jnp.stack([dwbd[k * 64:(k + 1) * 64, k * 64:(k + 1) * 64] for k in range(4)]),
        "pool_scale": dps[0],
    }
    return loss_blk[0, 0], dx, G, dmod, dsmall


SHARDED = ("w_ffn1_in", "w_ffn1_out", "w_in", "w_pool_branch", "w_attn_branch", "w_out", "w_ffn2_in", "w_ffn2_out")
TRANSPOSED = ("w_ffn1_in", "w_in", "w_ffn2_in")
FIRST = ("w_ffn1_in", "w_ffn1_out")
LATER = tuple(n for n in SHARDED if n not in FIRST)
GRAD_KEY = dict(w_ffn1_in="w1in", w_ffn1_out="w1out", w_in="win", w_pool_branch="wpb", w_attn_branch="wab",
                w_out="wout", w_ffn2_in="w2in", w_ffn2_out="w2out")


def _cols_to_full(g):
    return jnp.concatenate([g[j] for j in range(N_DEV)], axis=1)


def _full_to_cols(t):
    c = t.shape[1] // N_DEV
    return jnp.stack([t[:, j * c:(j + 1) * c] for j in range(N_DEV)])


SMALL =(("b_ada", 9216), ("g_norm_ffn1", 1024), ("g_norm_mix", 1024), ("g_norm_ffn2", 1024), ("g_final", 1024),
         ("w_pool", 16384), ("pool_scale", 256))
SMALL_ROWS = 240


def _pack_small(vals, loss=None):
    flat = jnp.concatenate([vals[name].reshape(-1) for name, _ in SMALL])
    tail = jnp.zeros((SMALL_ROWS * 128 - flat.shape[0],), F32)
    if loss is not None:
        tail = tail.at[0].set(loss)
    return jnp.concatenate([flat, tail]).reshape(SMALL_ROWS, 128)


def _unpack_small(slab, shapes):
    flat, out, off = slab.reshape(-1), {}, 0
    for name, n in SMALL:
        out[name] = flat[off:off + n].reshape(shapes[name])
        off += n
    return out, flat[off]


def kernel(x, c, positions, w_ada, b_ada, g_norm_ffn1, w_ffn1_in, w_ffn1_out, g_norm_mix, w_in, w_pool, pool_scale, w_pool_branch, w_attn_branch, w_out, g_norm_ffn2, w_ffn2_in, w_ffn2_out, g_final, loss_target, m_w_ada, m_b_ada, m_g_norm_ffn1, m_w_ffn1_in, m_w_ffn1_out, m_g_norm_mix, m_w_in, m_w_pool, m_pool_scale, m_w_pool_branch, m_w_attn_branch, m_w_out, m_g_norm_ffn2, m_w_ffn2_in, m_w_ffn2_out, m_g_final, v_w_ada, v_b_ada, v_g_norm_ffn1, v_w_ffn1_in, v_w_ffn1_out, v_g_norm_mix, v_w_in, v_w_pool, v_pool_scale, v_w_pool_branch, v_w_attn_branch, v_w_out, v_g_norm_ffn2, v_w_ffn2_in, v_w_ffn2_out, v_g_final):
    names = ["w_ada", "b_ada", "g_norm_ffn1", "w_ffn1_in", "w_ffn1_out", "g_norm_mix", "w_in", "w_pool", "pool_scale",
             "w_pool_branch", "w_attn_branch", "w_out", "g_norm_ffn2", "w_ffn2_in", "w_ffn2_out", "g_final"]
    w = dict(w_ada=w_ada, b_ada=b_ada, g_norm_ffn1=g_norm_ffn1, w_ffn1_in=w_ffn1_in, w_ffn1_out=w_ffn1_out,
             g_norm_mix=g_norm_mix, w_in=w_in, w_pool=w_pool, pool_scale=pool_scale, w_pool_branch=w_pool_branch,
             w_attn_branch=w_attn_branch, w_out=w_out, g_norm_ffn2=g_norm_ffn2, w_ffn2_in=w_ffn2_in,
             w_ffn2_out=w_ffn2_out, g_final=g_final)
    m = dict(w_ada=m_w_ada, b_ada=m_b_ada, g_norm_ffn1=m_g_norm_ffn1, w_ffn1_in=m_w_ffn1_in, w_ffn1_out=m_w_ffn1_out,
             g_norm_mix=m_g_norm_mix, w_in=m_w_in, w_pool=m_w_pool, pool_scale=m_pool_scale,
             w_pool_branch=m_w_pool_branch, w_attn_branch=m_w_attn_branch, w_out=m_w_out, g_norm_ffn2=m_g_norm_ffn2,
             w_ffn2_in=m_w_ffn2_in, w_ffn2_out=m_w_ffn2_out, g_final=m_g_final)
    v = dict(w_ada=v_w_ada, b_ada=v_b_ada, g_norm_ffn1=v_g_norm_ffn1, w_ffn1_in=v_w_ffn1_in, w_ffn1_out=v_w_ffn1_out,
             g_norm_mix=v_g_norm_mix, w_in=v_w_in, w_pool=v_w_pool, pool_scale=v_pool_scale,
             w_pool_branch=v_w_pool_branch, w_attn_branch=v_w_attn_branch, w_out=v_w_out, g_norm_ffn2=v_g_norm_ffn2,
             w_ffn2_in=v_w_ffn2_in, w_ffn2_out=v_w_ffn2_out, g_final=v_g_final)
    shapes = {n: w[n].shape for n in names}
    me = 4 * lax.axis_index("x") + 2 * lax.axis_index("y") + lax.axis_index("c")
    D = x.shape[-1]
    n_mod = w_ada.shape[-1] * N_DEV // D

    (c_all,) = _all_gather([c.reshape(D // 128, 128)], "gather_c")
    ada_cols = w_ada.shape[-1]
    b_mine = lax.dynamic_slice_in_dim(b_ada, me * ada_cols, ada_cols, axis=1)
    cond, mod_part = _ada_mod(c_all.reshape(N_DEV, D), w_ada[0], b_mine, "ada_mod")
    (mod_all,) = _all_gather([mod_part.reshape(-1, 128)], "gather_mod")
    mod_all = mod_all.reshape(N_DEV, N_DEV, ada_cols)
    mod = lax.dynamic_index_in_dim(mod_all, me, axis=1, keepdims=False).reshape(n_mod, D)

    def local(t, name):
        return t[name][0].T if name in TRANSPOSED else t[name][0]

    shards = {name: local(w, name).astype(BF16) for name in SHARDED}

    def gather_done(names, fulls):
        return {name: lax.dynamic_update_index_in_dim(full, shards[name], me, axis=0)
                for name, full in zip(names, fulls)}

    def ffn_weights(g, pre):
        return {"w%sin" % pre: g["w_ffn%s_in" % pre].reshape(2, -1, D),
                "w%sout" % pre: g["w_ffn%s_out" % pre].reshape(-1, D)}

    def later_weights(fulls):
        g = gather_done(LATER, fulls)
        return dict(win=g["w_in"].reshape(-1, D), wpb=_cols_to_full(g["w_pool_branch"]),
                    wab=_cols_to_full(g["w_attn_branch"]), wout=g["w_out"].reshape(D, D), **ffn_weights(g, "2"))

    W = ffn_weights(gather_done(FIRST, _all_gather([shards[n] for n in FIRST], "gather_ffn1", own=False)), "1")
    small = dict(g1=g_norm_ffn1[0], g2=g_norm_mix[0], g3=g_norm_ffn2[0], gf=g_final, w_pool=w_pool[0],
                 pool_scale=pool_scale[0])
    red = _GradReducer()
    loss_part, dx, G, dmod, dsmall = _local_step(
        x[0], loss_target[0], positions[0], mod, small, W,
        late=(_gather_rider([shards[n] for n in LATER]), later_weights), red=red)
    chip = (2 * lax.axis_index("x") + lax.axis_index("y")).astype(jnp.int32).reshape(1)

    part = _pack_small(dict(b_ada=dmod, g_norm_ffn1=dsmall["g1"], g_norm_mix=dsmall["g2"], g_norm_ffn2=dsmall["g3"],
                            g_final=dsmall["gf"], w_pool=dsmall["w_pool"], pool_scale=dsmall["pool_scale"]),
                       loss=loss_part)
    (parts,) = _all_gather([part], "gather_small")
    gsmall, loss = _unpack_small(_sum_slots(parts, "sum_small"), shapes)
    rows_mine = ada_cols // 128
    dmod_mine = lax.dynamic_slice_in_dim(parts, me * rows_mine, rows_mine, axis=1).reshape(N_DEV, ada_cols)

    grads, delta, new_m, new_v = {}, {}, {}, {}
    grads["w_ada"], delta["w_ada"], new_m["w_ada"], new_v["w_ada"] = (
        t[None] for t in _ada_grad_adamw(cond.T, dmod_mine, w_ada[0], m_w_ada[0], v_w_ada[0], "ada_grad_adamw"))
    for name in SHARDED:
        key = GRAD_KEY[name]
        res = _sum_adamw(chip, red.own[key], red.others[key], local(w, name), local(m, name), local(v, name),
                         "adamw_" + name)
        grads[name], delta[name], new_m[name], new_v[name] = (
            (t.T if name in TRANSPOSED else t)[None] for t in res)
    sd, sm, sv = _adamw(_pack_small(w), _pack_small(gsmall), _pack_small(m), _pack_small(v), "adamw_small")
    for dst, src in ((delta, sd), (new_m, sm), (new_v, sv)):
        dst.update(_unpack_small(src, shapes)[0])
    grads.update(gsmall)

    return (loss, dx[None], *[grads[n] for n in names], *[delta[n] for n in names],
            *[new_m[n] for n in names], *[new_v[n] for n in names])
```

```python
import functools

import jax
import jax.numpy as jnp
from jax import lax
from jax.experimental import pallas as pl
from jax.experimental.pallas import tpu as pltpu

F32 = jnp.float32
BF16 = jnp.bfloat16
MESH = pl.DeviceIdType.MESH
ANY = pl.BlockSpec(memory_space=pl.ANY)

N_DEV = 8
EPS = 1e-6
HEAD_DIM = 64
HEADS = 4
GW = HEADS * HEAD_DIM
DILATIONS = (1, 4, 16)
BAND = 128
QB = 128
POOL_WINDOWS = (2, 4, 8, 16)
HALO = 16
ROPE_THETA = 10000.0

ADAM_LR = 0.001
ADAM_B1 = 0.9
ADAM_B2 = 0.999
ADAM_EPS = 1e-08
ADAM_WD = 0.01
ADAM_STEP = 10

VMEM_LIMIT = 56 * 1024 * 1024
TS = 512
FFN_TS = 256
FC = 2816

NT = (((1,), (1,)), ((), ()))
TN = (((0,), (0,)), ((), ()))


def _params(**kw):
    return pltpu.CompilerParams(vmem_limit_bytes=VMEM_LIMIT, **kw)


def _dot(a, b):
    return jnp.dot(a, b, preferred_element_type=F32)


def _dot_nt(a, b):
    return lax.dot_general(a, b, NT, preferred_element_type=F32)


def _dot_tn(a, b):
    return lax.dot_general(a, b, TN, preferred_element_type=F32)


def _load_weights(pairs, sem):
    @pl.when(pl.program_id(0) == 0)
    def _():
        copies = [pltpu.make_async_copy(src, dst, sem.at[i]) for i, (src, dst) in enumerate(pairs)]
        for cp in copies:
            cp.start()
        for cp in copies:
            cp.wait()


def _norm_mod(x, g, sc, sh):
    r = lax.rsqrt(jnp.mean(x * x, axis=-1, keepdims=True) + EPS)
    xn = x * r
    y = xn * g
    return r, xn, y, y * (1.0 + sc) + sh


def _norm_mod_bwd(du, r, xn, y, g, sc):
    dsh = jnp.sum(du, axis=0, keepdims=True)
    dsc = jnp.sum(du * y, axis=0, keepdims=True)
    dy = du * (1.0 + sc)
    dg = jnp.sum(dy * xn, axis=0, keepdims=True)
    dxn = dy * g
    dx = r * (dxn - xn * jnp.mean(dxn * xn, axis=-1, keepdims=True))
    return dx, dsh, dsc, dg


def _row_tile(ts, width):
    return pl.BlockSpec((ts, width), lambda i: (i, 0))


def _const(shape):
    return pl.BlockSpec(shape, lambda *_: (0,) * len(shape))


def _final_tile(x, g, target):
    r = lax.rsqrt(jnp.mean(x * x, axis=-1, keepdims=True) + EPS)
    xn = x * r
    err = xn * g - target
    loss = 0.5 * jnp.sum(jnp.mean(err * err, axis=-1, keepdims=True))
    dy = err * (1.0 / x.shape[-1])
    dg = jnp.sum(dy * xn, axis=0, keepdims=True)
    dxn = dy * g
    return r * (dxn - xn * jnp.mean(dxn * xn, axis=-1, keepdims=True)), loss, dg


def _ffn_fwd(h, vec, win, wout, name, rider=None, final=None):
    TS = FFN_TS
    S, D = h.shape
    _, Fd, _ = win.shape
    nch = Fd // FC
    n_in, n_out = (6, 7) if final else (4, 5)

    def body(*refs):
        if rider is None:
            return compute(*refs)
        host, mine = rider.split(refs, n_in, n_out)
        rider.head(mine, pl.program_id(0))
        compute(*host)
        rider.tail(mine, pl.program_id(0), S // TS)

    def compute(*refs):
        h_ref, vec_ref, win_hbm, wout_hbm = refs[:4]
        hn_ref, u_ref, ab_ref, act_ref, f_ref = refs[n_in:n_in + 5]
        win_v, wout_v, sem = refs[n_in + n_out:]
        _load_weights([(win_hbm, win_v), (wout_hbm, wout_v)], sem)
        x = h_ref[...]
        g, sh, sc, gt = (vec_ref[k:k + 1, :] for k in range(4))
        _, _, _, u = _norm_mod(x, g, sc, sh)
        ub = u.astype(BF16)
        u_ref[...] = ub
        acc = jnp.zeros((TS, D), F32)
        for j in range(nch):
            sl = slice(j * FC, (j + 1) * FC)
            a = _dot_nt(ub, win_v[0, sl, :])
            b = _dot_nt(ub, win_v[1, sl, :])
            act = ((a * jax.nn.sigmoid(a)) * b).astype(BF16)
            ab_ref[0, :, sl] = a.astype(BF16)
            ab_ref[1, :, sl] = b.astype(BF16)
            act_ref[:, sl] = act
            acc = acc + _dot(act, wout_v[sl, :])
        f_ref[...] = acc
        hn = x + (0.5 * gt) * acc
        if not final:
            hn_ref[...] = hn
            return
        t_ref, gf_ref = refs[4:6]
        loss_ref, dgf_ref = refs[n_in + 5:n_in + 7]

        @pl.when(pl.program_id(0) == 0)
        def _():
            loss_ref[...] = jnp.zeros_like(loss_ref)
            dgf_ref[...] = jnp.zeros_like(dgf_ref)

        hn_ref[...], loss, dg = _final_tile(hn, gf_ref[0:1, :], t_ref[...])
        loss_ref[...] += loss
        dgf_ref[0:1, :] += dg

    specs = (
        [_row_tile(TS, D), _const((8, D)), ANY, ANY] + ([_row_tile(TS, D), _const((8, D))] if final else []),
        [_row_tile(TS, D), _row_tile(TS, D), pl.BlockSpec((2, TS, Fd), lambda i: (0, i, 0)),
         _row_tile(TS, Fd), _row_tile(TS, D)] + ([_const((8, 128)), _const((8, D))] if final else []),
        [jax.ShapeDtypeStruct((S, D), F32), jax.ShapeDtypeStruct((S, D), BF16),
         jax.ShapeDtypeStruct((2, S, Fd), BF16), jax.ShapeDtypeStruct((S, Fd), BF16),
         jax.ShapeDtypeStruct((S, D), F32)]
        + ([jax.ShapeDtypeStruct((8, 128), F32), jax.ShapeDtypeStruct((8, D), F32)] if final else []),
        [pltpu.VMEM(win.shape, BF16), pltpu.VMEM(wout.shape, BF16), pltpu.SemaphoreType.DMA((2,))])
    in_specs, out_specs, out_shape, scratch = specs if rider is None else rider.specs(*specs)
    outs = pl.pallas_call(
        body, name=name, grid=(S // TS,), in_specs=in_specs, out_specs=out_specs, out_shape=out_shape,
        scratch_shapes=scratch, compiler_params=_params(),
    )(h, vec, win, wout, *(final or ()), *(rider.arrays if rider else []))
    return outs if rider is None else (outs[:n_out], outs[n_out:])


def _ffn_bwd(dh, h, f, ab, vec, win, wout, name):
    TS = FFN_TS
    S, D = h.shape
    _, Fd, _ = win.shape
    nch = Fd // FC

    def body(dh_ref, h_ref, f_ref, ab_ref, vec_ref, win_hbm, wout_hbm,
             dhp_ref, dab_ref, df_ref, red_ref, win_v, wout_v, sem):
        _load_weights([(win_hbm, win_v), (wout_hbm, wout_v)], sem)

        @pl.when(pl.program_id(0) == 0)
        def _():
            red_ref[...] = jnp.zeros_like(red_ref)

        dh_v = dh_ref[...]
        x = h_ref[...]
        g, sh, sc, gt = (vec_ref[k:k + 1, :] for k in range(4))
        dgt = jnp.sum((0.5 * f_ref[...]) * dh_v, axis=0, keepdims=True)
        dfb = ((0.5 * gt) * dh_v).astype(BF16)
        df_ref[...] = dfb
        du = jnp.zeros((TS, D), F32)
        for j in range(nch):
            sl = slice(j * FC, (j + 1) * FC)
            dact = _dot_nt(dfb, wout_v[sl, :])
            av = ab_ref[0, :, sl].astype(F32)
            bv = ab_ref[1, :, sl].astype(F32)
            sg = jax.nn.sigmoid(av)
            da = (dact * bv * (sg * (1.0 + av * (1.0 - sg)))).astype(BF16)
            db = (dact * (av * sg)).astype(BF16)
            dab_ref[0, :, sl] = da
            dab_ref[1, :, sl] = db
            du = du + _dot(da, win_v[0, sl, :]) + _dot(db, win_v[1, sl, :])
        r, xn, y, _ = _norm_mod(x, g, sc, sh)
        dx, dsh, dsc, dg = _norm_mod_bwd(du, r, xn, y, g, sc)
        dhp_ref[...] = dh_v + dx
        red_ref[0:1, :] += dgt
        red_ref[1:2, :] += dsh
        red_ref[2:3, :] += dsc
        red_ref[3:4, :] += dg

    ab_spec = pl.BlockSpec((2, TS, Fd), lambda i: (0, i, 0))
    return pl.pallas_call(
        body, name=name, grid=(S // TS,),
        in_specs=[_row_tile(TS, D), _row_tile(TS, D), _row_tile(TS, D), ab_spec, _const((8, D)), ANY, ANY],
        out_specs=[_row_tile(TS, D), ab_spec, _row_tile(TS, D), _const((8, D))],
        out_shape=[jax.ShapeDtypeStruct((S, D), F32), jax.ShapeDtypeStruct((2, S, Fd), BF16),
                   jax.ShapeDtypeStruct((S, D), BF16), jax.ShapeDtypeStruct((8, D), F32)],
        scratch_shapes=[pltpu.VMEM(win.shape, BF16), pltpu.VMEM(wout.shape, BF16), pltpu.SemaphoreType.DMA((2,))],
        compiler_params=_params(),
    )(dh, h, f, ab, vec, win, wout)


def _wgrad(x, y, name, tm=None, ts=2048, rider=None):
    xb = x.ndim == 3
    nb = x.shape[0] if xb else 0
    S, M = x.shape[-2:]
    N = y.shape[-1]
    tm = tm or M
    ts = min(ts, S)
    nk = S // ts
    grid = (max(nb, 1), M // tm, nk)

    def body(*refs):
        if rider is None:
            return compute(*refs)
        host, mine = rider.split(refs, 2, 1)
        step = (pl.program_id(0) * grid[1] + pl.program_id(1)) * grid[2] + pl.program_id(2)
        rider.head(mine, step)
        compute(*host)
        rider.tail(mine, step, grid[0] * grid[1] * grid[2])

    def compute(x_ref, y_ref, o_ref, acc):
        k = pl.program_id(2)

        @pl.when(k == 0)
        def _():
            acc[...] = jnp.zeros_like(acc)

        acc[...] += _dot_tn(x_ref[...], y_ref[...])

        @pl.when(k == nk - 1)
        def _():
            o_ref[...] = acc[...].astype(BF16)

    x_spec = (pl.BlockSpec((None, ts, tm), lambda b, i, k: (b, k, i)) if xb
              else pl.BlockSpec((ts, tm), lambda b, i, k: (k, i)))
    y_spec = pl.BlockSpec((ts, N), lambda b, i, k: (k, 0))
    if xb:
        o_spec, o_shape = pl.BlockSpec((None, tm, N), lambda b, i, k: (b, i, 0)), (nb, M, N)
    else:
        o_spec, o_shape = pl.BlockSpec((tm, N), lambda b, i, k: (i, 0)), (M, N)
    specs = ([x_spec, y_spec], [o_spec], [jax.ShapeDtypeStruct(o_shape, BF16)], [pltpu.VMEM((tm, N), F32)])
    in_specs, out_specs, out_shape, scratch = specs if rider is None else rider.specs(*specs)
    outs = pl.pallas_call(
        body, name=name, grid=grid, in_specs=in_specs, out_specs=out_specs, out_shape=out_shape,
        scratch_shapes=scratch, compiler_params=_params(),
    )(x, y, *(rider.arrays if rider else []))
    return outs[0] if rider is None else (outs[0], outs[1:])


P_OFF, Q_OFF, K_OFF, V_OFF, G_OFF = 0, 256, 1024, 1792, 2560
IN_WIDTH = 4608


def _first_half_mask(ts):
    lane = lax.broadcasted_iota(jnp.int32, (ts, 128), 1)
    return (lane % HEAD_DIM) < (HEAD_DIM // 2)


def _rope(t, cos, sin_signed, first, sign):
    partner = jnp.where(first, pltpu.roll(t, 96, 1), pltpu.roll(t, 32, 1))
    return t * cos + sign * (partner * sin_signed)


def _res_spec(r):
    return pl.BlockSpec((r, TS // r, GW), lambda i: (0, i, 0))


def _res_shape(S, r, dtype):
    return jax.ShapeDtypeStruct((r, S // r, GW), dtype)


def _to_residues(piece, out_ref, lanes, r, scr):
    if r == 1:
        out_ref[0, :, lanes] = piece.astype(out_ref.dtype)
        return
    for h in range(piece.shape[1] // 128):
        scr[h] = piece[:, h * 128:(h + 1) * 128]
        at = slice(lanes.start + h * 128, lanes.start + (h + 1) * 128)
        for res in range(r):
            out_ref[res, :, at] = scr[h, pl.ds(res, TS // r, stride=r), :].astype(out_ref.dtype)


def _from_residues(in_ref, lanes, r, scr):
    if r == 1:
        return in_ref[0, :, lanes].astype(F32)
    halves = (lanes.stop - lanes.start) // 128
    for h in range(halves):
        at = slice(lanes.start + h * 128, lanes.start + (h + 1) * 128)
        for res in range(r):
            scr[h, pl.ds(res, TS // r, stride=r), :] = in_ref[res, :, at].astype(F32)
    return scr[0] if halves == 1 else jnp.concatenate([scr[0], scr[1]], axis=1)


RES_SCRATCH = (2, TS, 128)


def _mix_in_fwd(h, vec, cos, sin, win, name):
    S, D = h.shape

    def body(h_ref, vec_ref, cos_ref, sin_ref, win_hbm, u_ref, p_ref, gates_ref, *rest):
        qkv_refs, (win_v, sem, scr) = rest[:9], rest[9:]
        _load_weights([(win_hbm, win_v)], sem)
        g, sh, sc = (vec_ref[k:k + 1, :] for k in range(3))
        _, _, _, u = _norm_mod(h_ref[...], g, sc, sh)
        ub = u.astype(BF16)
        u_ref[...] = ub
        p_ref[...] = _dot_nt(ub, win_v[P_OFF:Q_OFF, :])
        cosv, sinv = cos_ref[...], sin_ref[...]
        first = _first_half_mask(TS)
        for which, off in enumerate((Q_OFF, K_OFF, V_OFF)):
            t = _dot_nt(ub, win_v[off:off + 3 * GW, :])
            for gi in range(3):
                for half in range(2):
                    c0 = gi * GW + half * 128
                    piece = t[:, c0:c0 + 128]
                    if which < 2:
                        piece = _rope(piece, cosv, sinv, first, 1.0)
                    _to_residues(piece, qkv_refs[which * 3 + gi], slice(half * 128, (half + 1) * 128),
                                 DILATIONS[gi], scr)
        gates_ref[...] = jax.nn.sigmoid(_dot_nt(ub, win_v[G_OFF:IN_WIDTH, :])).astype(BF16)

    return pl.pallas_call(
        body, name=name, grid=(S // TS,),
        in_specs=[_row_tile(TS, D), _const((8, D)), _row_tile(TS, 128), _row_tile(TS, 128), ANY],
        out_specs=[_row_tile(TS, D), _row_tile(TS, GW), _row_tile(TS, 2 * D)] + [_res_spec(r) for r in DILATIONS] * 3,
        out_shape=[jax.ShapeDtypeStruct((S, D), BF16), jax.ShapeDtypeStruct((S, GW), F32),
                   jax.ShapeDtypeStruct((S, 2 * D), BF16)] + [_res_shape(S, r, BF16) for r in DILATIONS] * 3,
        scratch_shapes=[pltpu.VMEM((IN_WIDTH, D), BF16), pltpu.SemaphoreType.DMA((1,)), pltpu.VMEM(RES_SCRATCH, F32)],
        compiler_params=_params(),
    )(h, vec, cos, sin, win)


def _mix_in_bwd(dh, h, vec, cos, sin, dp, dqkv, dgl, win, name):
    S, D = h.shape

    def body(dh_ref, h_ref, vec_ref, cos_ref, sin_ref, dp_ref, *rest):
        dqkv_refs = rest[:9]
        dgl_ref, win_hbm, dhp_ref, dproj_ref, red_ref, win_v, sem, scr = rest[9:]
        _load_weights([(win_hbm, win_v)], sem)

        @pl.when(pl.program_id(0) == 0)
        def _():
            red_ref[...] = jnp.zeros_like(red_ref)

        cosv, sinv = cos_ref[...], sin_ref[...]
        first = _first_half_mask(TS)
        dproj_ref[:, P_OFF:Q_OFF] = dp_ref[...].astype(BF16)
        for which, off in enumerate((Q_OFF, K_OFF, V_OFF)):
            for gi in range(3):
                for half in range(2):
                    piece = _from_residues(dqkv_refs[which * 3 + gi], slice(half * 128, (half + 1) * 128),
                                           DILATIONS[gi], scr)
                    if which < 2:
                        piece = _rope(piece, cosv, sinv, first, -1.0)
                    c0 = off + gi * GW + half * 128
                    dproj_ref[:, c0:c0 + 128] = piece.astype(BF16)
        dproj_ref[:, G_OFF:IN_WIDTH] = dgl_ref[...]
        du = _dot(dproj_ref[...], win_v[...])
        g, sh, sc = (vec_ref[k:k + 1, :] for k in range(3))
        r, xn, y, _ = _norm_mod(h_ref[...], g, sc, sh)
        dx, dsh, dsc, dg = _norm_mod_bwd(du, r, xn, y, g, sc)
        dhp_ref[...] = dh_ref[...] + dx
        red_ref[1:2, :] += dsh
        red_ref[2:3, :] += dsc
        red_ref[3:4, :] += dg

    return pl.pallas_call(
        body, name=name, grid=(S // TS,),
        in_specs=[_row_tile(TS, D), _row_tile(TS, D), _const((8, D)), _row_tile(TS, 128), _row_tile(TS, 128),
                  _row_tile(TS, GW)] + [_res_spec(r) for r in DILATIONS] * 3 + [_row_tile(TS, 2 * D), ANY],
        out_specs=[_row_tile(TS, D), _row_tile(TS, IN_WIDTH), _const((8, D))],
        out_shape=[jax.ShapeDtypeStruct((S, D), F32), jax.ShapeDtypeStruct((S, IN_WIDTH), BF16),
                   jax.ShapeDtypeStruct((8, D), F32)],
        scratch_shapes=[pltpu.VMEM((IN_WIDTH, D), BF16), pltpu.SemaphoreType.DMA((1,)), pltpu.VMEM(RES_SCRATCH, F32)],
        compiler_params=_params(),
    )(dh, h, vec, cos, sin, dp, *dqkv, dgl, win)


def _pool_lanes(rows):
    lane = lax.broadcasted_iota(jnp.int32, (rows, GW), 1)
    return lane // HEAD_DIM


def _pool_window(rows):
    grp = _pool_lanes(rows)
    w = jnp.full((rows, GW), POOL_WINDOWS[0], jnp.int32)
    for k in range(1, len(POOL_WINDOWS)):
        w = jnp.where(grp == k, POOL_WINDOWS[k], w)
    return grp, w


def _pool_fwd(p, wbd, scale, name, ts=512):
    S = p.shape[0]
    ext = ts + HALO

    def body(pc_ref, ph_ref, wbd_ref, sc_ref, d_ref, y_ref):
        i = pl.program_id(0)
        cur = pc_ref[...]
        halo = jnp.where(i > 0, ph_ref[...], 0.0)
        s = jnp.concatenate([halo, cur], axis=0)
        grp, w = _pool_window(ext)
        sel = jnp.zeros((ext, GW), F32)
        for k, wk in enumerate(POOL_WINDOWS):
            s = s + pltpu.roll(s, wk // 2, 0)
            sel = jnp.where(grp == k, s, sel)
        t = i * ts + lax.broadcasted_iota(jnp.int32, (ts, GW), 0)
        count = jnp.minimum(t + 1, w[HALO:]).astype(F32)
        d = (sel[HALO:] / count - cur).astype(BF16)
        d_ref[...] = d
        y_ref[...] = (_dot(d, wbd_ref[...]) * sc_ref[...]).astype(BF16)

    return pl.pallas_call(
        body, name=name, grid=(S // ts,),
        in_specs=[_row_tile(ts, GW),
                  pl.BlockSpec((HALO, GW), lambda i: (jnp.maximum(i * (ts // HALO) - 1, 0), 0)),
                  _const((GW, GW)), _const((1, GW))],
        out_specs=[_row_tile(ts, GW), _row_tile(ts, GW)],
        out_shape=[jax.ShapeDtypeStruct((S, GW), BF16), jax.ShapeDtypeStruct((S, GW), BF16)],
        compiler_params=_params(),
    )(p, p, wbd, scale)


def _pool_bwd(dy, d, wbd, scale, name, ts=512):
    S = dy.shape[0]
    ext = ts + HALO
    nsteps = S // ts
    last_halo = S // HALO - 1

    def body(dyc_ref, dyh_ref, d_ref, wbd_ref, sc_ref, dp_ref, dw_ref, ds_ref):
        i = pl.program_id(0)

        @pl.when(i == 0)
        def _():
            dw_ref[...] = jnp.zeros_like(dw_ref)
            ds_ref[...] = jnp.zeros_like(ds_ref)

        dyc = dyc_ref[...]
        dyh = jnp.where(i < nsteps - 1, dyh_ref[...], 0.0)
        dys = (jnp.concatenate([dyc, dyh], axis=0) * sc_ref[...]).astype(BF16)
        dd = _dot_nt(dys, wbd_ref[...])
        grp, w = _pool_window(ext)
        t = i * ts + lax.broadcasted_iota(jnp.int32, (ext, GW), 0)
        s = dd / jnp.minimum(t + 1, w).astype(F32)
        sel = jnp.zeros((ext, GW), F32)
        for k, wk in enumerate(POOL_WINDOWS):
            s = s + pltpu.roll(s, ext - wk // 2, 0)
            sel = jnp.where(grp == k, s, sel)
        dp_ref[...] = sel[:ts] - dd[:ts]
        dv = d_ref[...]
        z = _dot(dv, wbd_ref[...])
        ds_ref[0:1, :] += jnp.sum(dyc * z, axis=0, keepdims=True)
        dw_ref[...] += _dot_tn(dv, dys[:ts])

    return pl.pallas_call(
        body, name=name, grid=(nsteps,),
        in_specs=[_row_tile(ts, GW),
                  pl.BlockSpec((HALO, GW), lambda i: (jnp.minimum((i + 1) * (ts // HALO), last_halo), 0)),
                  _row_tile(ts, GW), _const((GW, GW)), _const((1, GW))],
        out_specs=[_row_tile(ts, GW), _const((GW, GW)), _const((8, GW))],
        out_shape=[jax.ShapeDtypeStruct((S, GW), F32), jax.ShapeDtypeStruct((GW, GW), F32),
                   jax.ShapeDtypeStruct((8, GW), F32)],
        compiler_params=_params(),
    )(dy, dy, d, wbd, scale)


def _head_id(rows):
    return lax.broadcasted_iota(jnp.int32, (rows, GW), 1) // HEAD_DIM


def _stack_heads(t, hid):
    return jnp.concatenate([jnp.where(hid == h, t, jnp.zeros_like(t)) for h in range(HEADS)], axis=0)


def _unstack_heads(t_all, hid):
    out = jnp.zeros((QB, GW), F32)
    for h in range(HEADS):
        out = jnp.where(hid == h, t_all[h * QB:(h + 1) * QB], out)
    return out


def _band_mask(n):
    row = lax.broadcasted_iota(jnp.int32, (HEADS * QB, 2 * QB), 0) % QB
    col = lax.broadcasted_iota(jnp.int32, (HEADS * QB, 2 * QB), 1)
    rel = row + QB - col
    return (rel >= 0) & (rel <= BAND) & ((col >= QB) | (n > 0))


MAX_STREAMS = 8


def _streams(r, nb):
    if r > 1:
        ns = min(r, MAX_STREAMS)
        return nb, [(lambda rb, l=l: ns * rb + l, 0) for l in range(ns)]
    ns = min(MAX_STREAMS, nb)
    return nb // ns, [(lambda rb: 0, l * (nb // ns)) for l in range(ns)]


def _attn_fwd(q, k, v, name):
    r, L, _ = q.shape
    nbs, streams = _streams(r, L // QB)
    ns = len(streams)
    grid = (max(r // ns, 1), nbs)

    def cur(res, off):
        return pl.BlockSpec((None, QB, GW), lambda rb, n: (res(rb), n + off, 0))

    def prev(res, off):
        return pl.BlockSpec((None, QB, GW), lambda rb, n: (res(rb), jnp.maximum(n + off - 1, 0), 0))

    def body(*refs):
        n = pl.program_id(1)
        hid = _head_id(QB)
        o_ref, lse_ref = refs[5 * len(streams):]
        for l, (_, off) in enumerate(streams):
            q_ref, kp_ref, kc_ref, vp_ref, vc_ref = refs[5 * l:5 * l + 5]
            qs = _stack_heads(q_ref[...], hid)
            kc = jnp.concatenate([kp_ref[...], kc_ref[...]], axis=0)
            vc = jnp.concatenate([vp_ref[...], vc_ref[...]], axis=0)
            s = _dot_nt(qs, kc) * (HEAD_DIM ** -0.5)
            s = jnp.where(_band_mask(n + off), s, -jnp.inf)
            m = jnp.max(s, axis=-1, keepdims=True)
            e = jnp.exp(s - m)
            den = jnp.sum(e, axis=-1, keepdims=True)
            lse = m + jnp.log(den)
            pr = (e * (1.0 / den)).astype(BF16)
            o_ref[l] = _unstack_heads(_dot(pr, vc), hid)
            lse_ref[l] = _unstack_heads(jnp.broadcast_to(lse, (HEADS * QB, GW)), hid)

    in_specs, args = [], []
    for res, off in streams:
        in_specs += [cur(res, off), prev(res, off), cur(res, off), prev(res, off), cur(res, off)]
        args += [q, k, k, v, v]
    out = jax.ShapeDtypeStruct((ns * grid[0], nbs * QB, GW), F32)
    both = pl.BlockSpec((ns, QB, GW), lambda rb, n: (rb, n, 0))
    o, lse = pl.pallas_call(
        body, name=name, grid=grid, in_specs=in_specs, out_specs=[both, both], out_shape=[out, out],
        compiler_params=_params(),
    )(*args)
    return o.reshape(q.shape), lse.reshape(q.shape)


def _head_rows(t_full, hid):
    return jnp.concatenate(
        [jnp.max(jnp.where(hid == h, t_full, -jnp.inf), axis=-1, keepdims=True) for h in range(HEADS)], axis=0)


def _attn_bwd(q, k, v, do, lse, cterm, name):
    r, L, _ = q.shape
    nbs, streams = _streams(r, L // QB)
    ns = len(streams)
    parts = r == 1

    def spec(res, index):
        return pl.BlockSpec((None, QB, GW), lambda rb, n: (res(rb), index(n), 0))

    def body(*refs):
        dq_ref, dk_ref, dv_ref, carry_k, carry_v, seam_k, seam_v = refs[8 * ns:]
        n = pl.program_id(1)

        @pl.when(n == 0)
        def _():
            carry_k[...] = jnp.zeros_like(carry_k)
            carry_v[...] = jnp.zeros_like(carry_v)

        @pl.when(n < nbs)
        def _():
            hid = _head_id(QB)
            for l, (_, off) in enumerate(streams):
                q_ref, do_ref, lse_ref, c_ref, kp_ref, kc_ref, vp_ref, vc_ref = refs[8 * l:8 * l + 8]
                qs = _stack_heads(q_ref[...], hid)
                dos = _stack_heads(do_ref[...], hid)
                kc = jnp.concatenate([kp_ref[...], kc_ref[...]], axis=0)
                vc = jnp.concatenate([vp_ref[...], vc_ref[...]], axis=0)
                s = _dot_nt(qs, kc) * (HEAD_DIM ** -0.5)
                s = jnp.where(_band_mask(n + off), s, -jnp.inf)
                p = jnp.exp(s - _head_rows(lse_ref[...], hid))
                dp = _dot_nt(dos, vc)
                ds = (p * (dp + _head_rows(c_ref[...], hid)) * (HEAD_DIM ** -0.5)).astype(BF16)
                dq_ref[l] = _unstack_heads(_dot(ds, kc), hid).astype(BF16)
                dkc = _dot_tn(ds, qs)
                dvc = _dot_tn(p.astype(BF16), dos)
                if parts and l > 0:
                    @pl.when(n == 0)
                    def _():
                        seam_k[l] = dkc[:QB]
                        seam_v[l] = dvc[:QB]
                dk_ref[l] = (carry_k[l] + dkc[:QB]).astype(BF16)
                dv_ref[l] = (carry_v[l] + dvc[:QB]).astype(BF16)
                carry_k[l] = dkc[QB:]
                carry_v[l] = dvc[QB:]

        @pl.when(n == nbs)
        def _():
            for l in range(ns):
                if parts and l + 1 < ns:
                    dk_ref[l] = (carry_k[l] + seam_k[l + 1]).astype(BF16)
                    dv_ref[l] = (carry_v[l] + seam_v[l + 1]).astype(BF16)
                else:
                    dk_ref[l] = carry_k[l].astype(BF16)
                    dv_ref[l] = carry_v[l].astype(BF16)

    in_specs, args = [], []
    for res, off in streams:
        qside = functools.partial(lambda n, off: jnp.minimum(n, nbs - 1) + off, off=off)
        kprev = functools.partial(lambda n, off: jnp.maximum(jnp.minimum(n, nbs) - 1 + off, 0), off=off)
        in_specs += [spec(res, qside)] * 4 + [spec(res, kprev), spec(res, qside)] * 2
        args += [q, do, lse, cterm, k, k, v, v]
    out = jax.ShapeDtypeStruct((ns * max(r // ns, 1), nbs * QB, GW), BF16)
    qout = pl.BlockSpec((ns, QB, GW), lambda rb, n: (rb, jnp.minimum(n, nbs - 1), 0))
    kout = pl.BlockSpec((ns, QB, GW), lambda rb, n: (rb, jnp.maximum(n - 1, 0), 0))
    buf = pltpu.VMEM((ns, QB, GW), F32)
    outs = pl.pallas_call(
        body, name=name, grid=(max(r // ns, 1), nbs + 1),
        in_specs=in_specs, out_specs=[qout, kout, kout], out_shape=[out, out, out],
        scratch_shapes=[buf, buf, buf, buf],
        compiler_params=_params(),
    )(*args)
    return [t.reshape(q.shape) for t in outs]


def _token_order(refs, scr):
    return [_from_residues(ref, slice(0, GW), r, scr) for ref, r in zip(refs, DILATIONS)]


def _group_weights(lses):
    l0, l1, l2 = lses
    m = jnp.maximum(jnp.maximum(l0, l1), l2)
    e = [jnp.exp(l - m) for l in (l0, l1, l2)]
    den = e[0] + e[1] + e[2]
    return [ei / den for ei in e]


def _mix_out_fwd(h, vec, gates, ypool, o3, lse3, wpb, wab, wout, name):
    S, D = h.shape

    def body(h_ref, vec_ref, gates_ref, yp_ref, o0, o1, o2, l0, l1, l2, wpb_hbm, wab_hbm, wout_hbm,
             hn_ref, ya_ref, merged_ref, tm_ref, wpb_v, wab_v, wout_v, sem, scr):
        _load_weights([(wpb_hbm, wpb_v), (wab_hbm, wab_v), (wout_hbm, wout_v)], sem)
        gt = vec_ref[3:4, :]
        wts = _group_weights(_token_order((l0, l1, l2), scr))
        og = _token_order((o0, o1, o2), scr)
        ya = (wts[0] * og[0] + wts[1] * og[1] + wts[2] * og[2]).astype(BF16)
        ya_ref[...] = ya
        merged = (gates_ref[:, :D].astype(F32) * _dot(yp_ref[...], wpb_v[...])
                  + gates_ref[:, D:].astype(F32) * _dot(ya, wab_v[...])).astype(BF16)
        merged_ref[...] = merged
        tm = _dot(merged, wout_v[...])
        tm_ref[...] = tm.astype(BF16)
        hn_ref[...] = h_ref[...] + gt * tm

    grp = _row_tile(TS, GW)
    res = [_res_spec(r) for r in DILATIONS]
    return pl.pallas_call(
        body, name=name, grid=(S // TS,),
        in_specs=[_row_tile(TS, D), _const((8, D)), _row_tile(TS, 2 * D), grp] + res * 2 + [ANY, ANY, ANY],
        out_specs=[_row_tile(TS, D), grp, _row_tile(TS, D), _row_tile(TS, D)],
        out_shape=[jax.ShapeDtypeStruct((S, D), F32), jax.ShapeDtypeStruct((S, GW), BF16),
                   jax.ShapeDtypeStruct((S, D), BF16), jax.ShapeDtypeStruct((S, D), BF16)],
        scratch_shapes=[pltpu.VMEM((GW, D), BF16), pltpu.VMEM((GW, D), BF16), pltpu.VMEM((D, D), BF16),
                        pltpu.SemaphoreType.DMA((3,)), pltpu.VMEM(RES_SCRATCH, F32)],
        compiler_params=_params(),
    )(h, vec, gates, ypool, *o3, *lse3, wpb, wab, wout)


def _mix_out_bwd(dh, tm, vec, gates, ypool, o3, lse3, wpb, wab, wout, name, rider=None):
    S, D = dh.shape

    def body(*refs):
        if rider is None:
            return compute(*refs)
        host, mine = rider.split(refs, 14, 12)
        rider.head(mine, pl.program_id(0))
        compute(*host)
        rider.tail(mine, pl.program_id(0), S // TS)

    def compute(dh_ref, tm_ref, vec_ref, gates_ref, yp_ref, o0, o1, o2, l0, l1, l2, wpb_hbm, wab_hbm, wout_hbm,
                dtm_ref, dgl_ref, dypb_ref, dyab_ref, dyp_ref, do0, do1, do2, c0, c1, c2, red_ref,
                wpb_v, wab_v, wout_v, sem, scr):
        _load_weights([(wpb_hbm, wpb_v), (wab_hbm, wab_v), (wout_hbm, wout_v)], sem)

        @pl.when(pl.program_id(0) == 0)
        def _():
            red_ref[...] = jnp.zeros_like(red_ref)

        gt = vec_ref[3:4, :]
        dh_v = dh_ref[...]
        red_ref[0:1, :] += jnp.sum(tm_ref[...].astype(F32) * dh_v, axis=0, keepdims=True)
        dtm = (gt * dh_v).astype(BF16)
        dtm_ref[...] = dtm
        dm = _dot_nt(dtm, wout_v[...])
        wts = _group_weights(_token_order((l0, l1, l2), scr))
        og = _token_order((o0, o1, o2), scr)
        ya = wts[0] * og[0] + wts[1] * og[1] + wts[2] * og[2]
        ypb = _dot(yp_ref[...], wpb_v[...])
        yab = _dot(ya.astype(BF16), wab_v[...])
        gp = gates_ref[:, :D].astype(F32)
        ga = gates_ref[:, D:].astype(F32)
        dgl_ref[:, :D] = (dm * ypb * gp * (1.0 - gp)).astype(BF16)
        dgl_ref[:, D:] = (dm * yab * ga * (1.0 - ga)).astype(BF16)
        dypb = (dm * gp).astype(BF16)
        dyab = (dm * ga).astype(BF16)
        dypb_ref[...] = dypb
        dyab_ref[...] = dyab
        dyp_ref[...] = _dot_nt(dypb, wpb_v[...])
        dya = _dot_nt(dyab, wab_v[...])
        row = lax.broadcasted_iota(jnp.int32, (GW, GW), 0) // HEAD_DIM
        col = lax.broadcasted_iota(jnp.int32, (GW, GW), 1) // HEAD_DIM
        ones = jnp.where(row == col, 1.0, 0.0).astype(F32)
        tot = jnp.dot(dya * ya, ones, preferred_element_type=F32, precision=lax.Precision.HIGHEST)
        for wg, do_ref, c_ref, r in zip(wts, (do0, do1, do2), (c0, c1, c2), DILATIONS):
            _to_residues(wg * dya, do_ref, slice(0, GW), r, scr)
            _to_residues(-(wg * tot), c_ref, slice(0, GW), r, scr)

    grp = _row_tile(TS, GW)
    res = [_res_spec(r) for r in DILATIONS]
    specs = (
        [_row_tile(TS, D), _row_tile(TS, D), _const((8, D)), _row_tile(TS, 2 * D), grp] + res * 2
        + [ANY, ANY, ANY],
        [_row_tile(TS, D), _row_tile(TS, 2 * D), _row_tile(TS, D), _row_tile(TS, D), grp]
        + res * 2 + [_const((8, D))],
        [jax.ShapeDtypeStruct((S, D), BF16), jax.ShapeDtypeStruct((S, 2 * D), BF16),
         jax.ShapeDtypeStruct((S, D), BF16), jax.ShapeDtypeStruct((S, D), BF16), jax.ShapeDtypeStruct((S, GW), F32)]
        + [_res_shape(S, r, BF16) for r in DILATIONS] + [_res_shape(S, r, F32) for r in DILATIONS]
        + [jax.ShapeDtypeStruct((8, D), F32)],
        [pltpu.VMEM((GW, D), BF16), pltpu.VMEM((GW, D), BF16), pltpu.VMEM((D, D), BF16),
         pltpu.SemaphoreType.DMA((3,)), pltpu.VMEM(RES_SCRATCH, F32)])
    in_specs, out_specs, out_shape, scratch = specs if rider is None else rider.specs(*specs)
    outs = pl.pallas_call(
        body, name=name, grid=(S // TS,), in_specs=in_specs, out_specs=out_specs, out_shape=out_shape,
        scratch_shapes=scratch, compiler_params=_params(),
    )(dh, tm, vec, gates, ypool, *o3, *lse3, wpb, wab, wout, *(rider.arrays if rider else []))
    return outs if rider is None else (outs[:12], outs[12:])


def _ada_mod(c_all, w, b, name):
    def body(c_ref, w_ref, b_ref, cond_ref, mod_ref):
        cv = c_ref[...]
        cond = cv * jax.nn.sigmoid(cv)
        cond_ref[...] = cond
        mod_ref[...] = jnp.dot(cond, w_ref[...], preferred_element_type=F32,
                               precision=lax.Precision.HIGHEST) + b_ref[...]

    return pl.pallas_call(
        body, name=name,
        out_shape=[jax.ShapeDtypeStruct(c_all.shape, F32), jax.ShapeDtypeStruct((c_all.shape[0], w.shape[1]), F32)],
        compiler_params=_params(),
    )(c_all, w, b)


def _adamw_math(w, g, m, v):
    m = ADAM_B1 * m + (1.0 - ADAM_B1) * g
    v = ADAM_B2 * v + (1.0 - ADAM_B2) * (g * g)
    m_hat = m / (1.0 - ADAM_B1 ** ADAM_STEP)
    v_hat = v / (1.0 - ADAM_B2 ** ADAM_STEP)
    delta = -ADAM_LR * (m_hat / (jnp.sqrt(v_hat) + ADAM_EPS) + ADAM_WD * w)
    return delta, m, v


def _adamw_many(ws, gs, ms, vs, name):
    n = len(ws)

    def body(*refs):
        for k in range(n):
            w_ref, g_ref, m_ref, v_ref = (refs[j * n + k] for j in range(4))
            d_ref, mo_ref, vo_ref = (refs[(4 + j) * n + k] for j in range(3))
            d_ref[...], mo_ref[...], vo_ref[...] = _adamw_math(w_ref[...], g_ref[...], m_ref[...], v_ref[...])

    outs = pl.pallas_call(
        body, name=name, out_shape=[jax.ShapeDtypeStruct(t.shape, F32) for t in ws] * 3,
        compiler_params=_params(),
    )(*ws, *gs, *ms, *vs)
    return outs[:n], outs[n:2 * n], outs[2 * n:]


def _ada_grad_adamw(cond_t, dmod, w, m, v, name, tr=256):
    R, C = w.shape
    nb = dmod.shape[0]

    def body(ct_ref, dm_ref, w_ref, m_ref, v_ref, g_ref, d_ref, mo_ref, vo_ref):
        ct = ct_ref[...]
        dm = dm_ref[...]
        g = jnp.zeros((tr, C), F32)
        for bi in range(nb):
            g = g + ct[:, bi:bi + 1] * dm[bi:bi + 1, :]
        g_ref[...] = g
        d_ref[...], mo_ref[...], vo_ref[...] = _adamw_math(w_ref[...], g, m_ref[...], v_ref[...])

    spec = _row_tile(tr, C)
    out = jax.ShapeDtypeStruct((R, C), F32)
    return pl.pallas_call(
        body, name=name, grid=(R // tr,),
        in_specs=[_row_tile(tr, nb), _const((nb, C)), spec, spec, spec],
        out_specs=[spec] * 4, out_shape=[out] * 4,
        compiler_params=_params(),
    )(cond_t, dmod, w, m, v)


def _row_step(rows, cap=256):
    for cand in range(cap, 15, -16):
        if rows % cand == 0:
            return cand
    return rows


def _slot_sum(x_ref):
    acc = x_ref[0].astype(F32)
    for k in range(1, x_ref.shape[0]):
        acc = acc + x_ref[k].astype(F32)
    return acc


def _sum_slots(x, name, out_dtype=F32):
    n, R, C = x.shape
    tr = _row_step(R)

    def body(x_ref, o_ref):
        o_ref[...] = _slot_sum(x_ref).astype(out_dtype)

    return pl.pallas_call(
        body, name=name, grid=(R // tr,),
        in_specs=[pl.BlockSpec((n, tr, C), lambda i: (0, i, 0))],
        out_specs=_row_tile(tr, C), out_shape=jax.ShapeDtypeStruct((R, C), out_dtype),
        compiler_params=_params(),
    )(x)


def _sum_pair(core, g, recv, name):
    _, _, R, C = g.shape
    tr = _row_step(R, cap=1024)

    def body(core_ref, g_ref, r_ref, o_ref):
        o_ref[...] = (g_ref[...].astype(F32) + r_ref[...].astype(F32)).astype(BF16)

    return pl.pallas_call(
        body, name=name, out_shape=jax.ShapeDtypeStruct((4, R, C), BF16),
        grid_spec=pltpu.PrefetchScalarGridSpec(
            num_scalar_prefetch=1, grid=(4, R // tr),
            in_specs=[pl.BlockSpec((None, None, tr, C), lambda k, i, core_ref: (k, core_ref[0], i, 0)),
                      pl.BlockSpec((None, tr, C), lambda k, i, core_ref: (k, i, 0))],
            out_specs=pl.BlockSpec((None, tr, C), lambda k, i, core_ref: (k, i, 0))),
        compiler_params=_params(),
    )(core, g, recv)


def _sum_adamw(chip, own, recv, w, m, v, name):
    _, R, C = own.shape
    tr = _row_step(R, cap=512)

    def body(chip_ref, own_ref, r_ref, w_ref, m_ref, v_ref, g_ref, d_ref, mo_ref, vo_ref):
        g = own_ref[...].astype(F32) + _slot_sum(r_ref)
        g_ref[...] = g
        d_ref[...], mo_ref[...], vo_ref[...] = _adamw_math(w_ref[...], g, m_ref[...], v_ref[...])

    spec = pl.BlockSpec((tr, C), lambda i, chip_ref: (i, 0))
    out = jax.ShapeDtypeStruct((R, C), F32)
    return pl.pallas_call(
        body, name=name, out_shape=[out] * 4,
        grid_spec=pltpu.PrefetchScalarGridSpec(
            num_scalar_prefetch=1, grid=(R // tr,),
            in_specs=[pl.BlockSpec((None, tr, C), lambda i, chip_ref: (chip_ref[0], i, 0)),
                      pl.BlockSpec((3, tr, C), lambda i, chip_ref: (0, i, 0)), spec, spec, spec],
            out_specs=[spec] * 4),
        compiler_params=_params(),
    )(chip, own, recv, w, m, v)


def _place():
    return lax.axis_index("x"), lax.axis_index("y"), lax.axis_index("c")


def _gather_steps(x_refs, out_refs, send_sems, recv_sems):
    n = len(x_refs)
    x, y, c = _place()
    me, sibling = (x, y, c), (x, y, 1 - c)
    chips = [(1 - x, y), (x, 1 - y), (1 - x, 1 - y)]

    def rows(a, px, py, pc):
        return out_refs[a].at[4 * px + 2 * py + pc]

    def copy(a, k, block, to, src=None):
        return pltpu.make_async_remote_copy(
            src_ref=rows(a, *block) if src is None else src, dst_ref=rows(a, *block),
            send_sem=send_sems.at[a, k], recv_sem=recv_sems.at[a, k], device_id=to, device_id_type=MESH)

    def first(a):
        return [copy(a, 0, me, sibling, src=x_refs[a])] + [
            copy(a, 1 + j, me, (*chip, c), src=x_refs[a]) for j, chip in enumerate(chips)]

    def passed(a, j):
        return copy(a, 4 + j, (*chips[j], c), sibling)

    def start():
        for a in range(n):
            for cp in first(a):
                cp.start()

    def relay():
        for j, chip in enumerate(chips):
            for a in range(n):
                copy(a, 1 + j, (*chip, c), me).wait_recv()
                passed(a, j).start()

    def finish():
        for a in range(n):
            copy(a, 0, sibling, me).wait_recv()
            for j, chip in enumerate(chips):
                copy(a, 4 + j, (*chip, 1 - c), me).wait_recv()
        for a in range(n):
            for cp in first(a) + [passed(a, j) for j in range(3)]:
                cp.wait_send()

    return start, relay, finish


def _all_gather(arrs, name, own=True):
    n = len(arrs)

    def body(*refs):
        x_refs, out_refs = refs[:n], refs[n:2 * n]
        send_sems, recv_sems, local_sems = refs[2 * n:]
        me = 4 * lax.axis_index("x") + 2 * lax.axis_index("y") + lax.axis_index("c")
        mine = [pltpu.make_async_copy(x_refs[a], out_refs[a].at[me], local_sems.at[a]) for a in range(n)] if own else []
        for cp in mine:
            cp.start()
        for step in _gather_steps(x_refs, out_refs, send_sems, recv_sems):
            step()
        for cp in mine:
            cp.wait()

    return pl.pallas_call(
        body, name=name, out_shape=[jax.ShapeDtypeStruct((N_DEV,) + t.shape, t.dtype) for t in arrs],
        in_specs=[ANY] * n, out_specs=[ANY] * n,
        scratch_shapes=[pltpu.SemaphoreType.DMA((n, 7)), pltpu.SemaphoreType.DMA((n, 7)),
                        pltpu.SemaphoreType.DMA((n,))],
    )(*arrs)


def _pair_exchange(arrs, name):
    n = len(arrs)

    def body(*refs):
        g_refs, out_refs = refs[:n], refs[n:2 * n]
        send_sems, recv_sems = refs[2 * n:]
        x, y, c = _place()
        give = [pltpu.make_async_remote_copy(
            src_ref=g_refs[a].at[pl.ds(0, 4), 1 - c], dst_ref=out_refs[a], send_sem=send_sems.at[a],
            recv_sem=recv_sems.at[a], device_id=(x, y, 1 - c), device_id_type=MESH) for a in range(n)]
        for cp in give:
            cp.start()
        for cp in give:
            cp.wait()

    return pl.pallas_call(
        body, name=name,
        out_shape=[jax.ShapeDtypeStruct((4,) + t.shape[2:], t.dtype) for t in arrs],
        in_specs=[ANY] * n, out_specs=[ANY] * n,
        scratch_shapes=[pltpu.SemaphoreType.DMA((n,)), pltpu.SemaphoreType.DMA((n,))],
    )(*arrs)


def _chip_exchange_steps(p_refs, out_refs, send_sems, recv_sems):
    x, y, c = _place()
    chips = [(1 - x, y), (x, 1 - y), (1 - x, 1 - y)]

    def copies():
        return [pltpu.make_async_remote_copy(
            src_ref=p_refs[a].at[2 * px + py], dst_ref=out_refs[a].at[j], send_sem=send_sems.at[a, j],
            recv_sem=recv_sems.at[a, j], device_id=(px, py, c), device_id_type=MESH)
            for a in range(len(p_refs)) for j, (px, py) in enumerate(chips)]

    def start():
        for cp in copies():
            cp.start()

    def finish():
        for cp in copies():
            cp.wait()

    return start, finish


def _chip_exchange(arrs, name):
    n = len(arrs)

    def body(*refs):
        for step in _chip_exchange_steps(refs[:n], refs[n:2 * n], *refs[2 * n:]):
            step()

    return pl.pallas_call(
        body, name=name, out_shape=[jax.ShapeDtypeStruct((3,) + t.shape[1:], t.dtype) for t in arrs],
        in_specs=[ANY] * n, out_specs=[ANY] * n,
        scratch_shapes=[pltpu.SemaphoreType.DMA((n, 3)), pltpu.SemaphoreType.DMA((n, 3))],
    )(*arrs)


class _Rider:
    def __init__(self, arrays, out_shape, sems, steps, relay_before_end=None):
        self.arrays, self.out_shape, self.scratch, self.steps = list(arrays), out_shape, sems, steps
        self.n = len(self.arrays)
        self.relay_before_end = relay_before_end

    def specs(self, in_specs, out_specs, out_shape, scratch):
        extra = [ANY] * self.n
        return in_specs + extra, out_specs + extra, out_shape + self.out_shape, scratch + self.scratch

    def split(self, refs, n_in, n_out):
        k = self.n
        a, b = n_in + k, n_in + k + n_out
        return refs[:n_in] + refs[a:b] + refs[b + k:-2], (refs[n_in:a], refs[b:b + k], refs[-2:])

    def head(self, mine, step):
        pl.when(step == 0)(self.steps(mine[0], mine[1], *mine[2])[0])

    def tail(self, mine, step, nsteps):
        steps = self.steps(mine[0], mine[1], *mine[2])
        if self.relay_before_end is not None:
            pl.when(step == nsteps - 1 - self.relay_before_end)(steps[1])
        pl.when(step == nsteps - 1)(steps[-1])


def _gather_rider(arrs, relay_before_end=4):
    n = len(arrs)
    return _Rider(arrs, [jax.ShapeDtypeStruct((N_DEV,) + t.shape, t.dtype) for t in arrs],
                  [pltpu.SemaphoreType.DMA((n, 7)), pltpu.SemaphoreType.DMA((n, 7))], _gather_steps,
                  relay_before_end)


def _chip_exchange_rider(arrs):
    n = len(arrs)
    return _Rider(arrs, [jax.ShapeDtypeStruct((3,) + t.shape[1:], t.dtype) for t in arrs],
                  [pltpu.SemaphoreType.DMA((n, 3)), pltpu.SemaphoreType.DMA((n, 3))], _chip_exchange_steps)


def _rope_tables(positions):
    inv_freq = ROPE_THETA ** (-jnp.arange(0, HEAD_DIM, 2, dtype=F32) / HEAD_DIM)
    ang = positions.astype(F32)[:, None] * inv_freq
    cos, sin = jnp.cos(ang), jnp.sin(ang)
    return jnp.tile(cos, (1, 4)), jnp.tile(jnp.concatenate([-sin, sin], axis=1), (1, 2))


def _vec(g, shift, scale, gate):
    z = jnp.zeros_like(g)
    return jnp.stack([g, shift, scale, gate, z, z, z, z])


class _GradReducer:
    def __init__(self):
        self.core = lax.axis_index("c").astype(jnp.int32).reshape(1)
        self.own, self.others, self.waiting, self.riding = {}, {}, [], []

    def pair(self, named):
        keys = list(named)
        mine = [named[k].reshape((4, 2) + named[k].shape[1:]) for k in keys]
        theirs = _pair_exchange(mine, "reduce_pair_" + keys[0])
        for k, g, r in zip(keys, mine, theirs):
            self.own[k] = _sum_pair(self.core, g, r, "sum_pair_" + k)
        self.waiting += keys

    def rider(self):
        self.riding, self.waiting = self.waiting, []
        return _chip_exchange_rider([self.own[k] for k in self.riding])

    def landed(self, results):
        self.others.update(zip(self.riding, results))

    def flush(self, name):
        keys, self.waiting = self.waiting, []
        self.others.update(zip(keys, _chip_exchange([self.own[k] for k in keys], name)))


def _by_owner(g):
    if g.ndim == 3:
        return g if g.shape[0] == N_DEV else g.reshape(N_DEV, g.shape[1] * g.shape[0] // N_DEV, g.shape[2])
    return g.reshape(N_DEV, g.shape[0] // N_DEV, g.shape[1])


def _local_step(x, target, positions, mod, small, W, late=None, red=None):
    S, D = x.shape
    sh1, sc1, gt1, sh2, sc2, gt2, sh3, sc3, gt3 = (mod[k] for k in range(9))
    v1 = _vec(small["g1"], sh1, sc1, gt1)
    v2 = _vec(small["g2"], sh2, sc2, gt2)
    v3 = _vec(small["g3"], sh3, sc3, gt3)
    vf = _vec(small["gf"], small["gf"], small["gf"], small["gf"])
    cos, sin = _rope_tables(positions)
    wbd = jax.scipy.linalg.block_diag(*[small["w_pool"][k] for k in range(4)]).astype(BF16)
    pscale = small["pool_scale"].reshape(1, GW)

    if late is None:
        h1, u1, ab1, act1, f1 = _ffn_fwd(x, v1, W["w1in"], W["w1out"], "ffn1_fwd")
    else:
        (h1, u1, ab1, act1, f1), landed = _ffn_fwd(x, v1, W["w1in"], W["w1out"], "ffn1_fwd", rider=late[0])
        W = {**W, **late[1](landed)}
    u2, p, gates, *qkv = _mix_in_fwd(h1, v2, cos, sin, W["win"], "mix_in_fwd")
    dpool, ypool = _pool_fwd(p, wbd, pscale, "pool_fwd")
    o3, lse3 = [], []
    for gi in range(len(DILATIONS)):
        o, lse = _attn_fwd(qkv[gi], qkv[3 + gi], qkv[6 + gi], f"attn_fwd_{gi}")
        o3.append(o)
        lse3.append(lse)
    h2, ya, merged, tm = _mix_out_fwd(h1, v2, gates, ypool, o3, lse3, W["wpb"], W["wab"], W["wout"], "mix_out_fwd")
    dh3, u3, ab3, act3, f3, loss_blk, dgf = _ffn_fwd(h2, v3, W["w2in"], W["w2out"], "ffn2_fwd", final=(target, vf))

    dh2, dab3, df3, red3 = _ffn_bwd(dh3, h2, f3, ab3, v3, W["w2in"], W["w2out"], "ffn2_bwd")
    half_f = ab3.shape[2] // 2
    G = {"w2in": _by_owner(_wgrad(dab3, u3, "wgrad_2in", tm=half_f)),
         "w2out": _by_owner(_wgrad(act3, df3, "wgrad_2out", tm=half_f))}
    mix_out_args = (dh2, tm, v2, gates, ypool, o3, lse3, W["wpb"], W["wab"], W["wout"], "mix_out_bwd")
    if red is None:
        mix_out = _mix_out_bwd(*mix_out_args)
    else:
        red.pair({k: G[k] for k in ("w2in", "w2out")})
        mix_out, landed = _mix_out_bwd(*mix_out_args, rider=red.rider())
        red.landed(landed)
    (dtm, dgl, dypb, dyab, dyp, do0, do1, do2, c0, c1, c2, red2o) = mix_out
    dq3, dk3, dv3 = [], [], []
    for gi, (do, ct) in enumerate(zip((do0, do1, do2), (c0, c1, c2))):
        dq, dk, dv = _attn_bwd(qkv[gi], qkv[3 + gi], qkv[6 + gi], do, lse3[gi], ct, f"attn_bwd_{gi}")
        dq3.append(dq)
        dk3.append(dk)
        dv3.append(dv)
    dp, dwbd, dps = _pool_bwd(dyp, dpool, wbd, pscale, "pool_bwd")
    dh1, dproj, red2i = _mix_in_bwd(dh2, h1, v2, cos, sin, dp, dq3 + dk3 + dv3, dgl, W["win"], "mix_in_bwd")
    G["win"] = _by_owner(_wgrad(dproj, u2, "wgrad_in", tm=1152))
    G["wpb"] = _full_to_cols(_wgrad(ypool, dypb, "wgrad_pb"))
    G["wab"] = _full_to_cols(_wgrad(ya, dyab, "wgrad_ab"))
    G["wout"] = _by_owner(_wgrad(merged, dtm, "wgrad_out"))
    if red is not None:
        red.pair({k: G[k] for k in ("win", "wpb", "wab", "wout")})
    dx, dab1, df1, red1 = _ffn_bwd(dh1, x, f1, ab1, v1, W["w1in"], W["w1out"], "ffn1_bwd")
    if red is None:
        G["w1in"] = _by_owner(_wgrad(dab1, u1, "wgrad_1in", tm=half_f))
        G["w1out"] = _by_owner(_wgrad(act1, df1, "wgrad_1out", tm=half_f))
    else:
        g1in, landed = _wgrad(dab1, u1, "wgrad_1in", tm=half_f, rider=red.rider())
        red.landed(landed)
        G["w1in"] = _by_owner(g1in)
        red.pair({"w1in": G["w1in"]})
        g1out, landed = _wgrad(act1, df1, "wgrad_1out", tm=half_f, rider=red.rider())
        red.landed(landed)
        G["w1out"] = _by_owner(g1out)
        red.pair({"w1out": G["w1out"]})
        red.flush("reduce_chips_w1out")
    dmod = jnp.stack([red1[1], red1[2], red1[0], red2i[1], red2i[2], red2o[0], red3[1], red3[2], red3[0]])
    dsmall = {
        "g1": red1[3], "g2": red2i[3], "g3": red3[3], "gf": dgf[0],
        "w_pool": jnp.stack([dwbd[k * 64:(k + 1) * 64, k * 64:(k + 1) * 64] for k in range(4)]),
        "pool_scale": dps[0],
    }
    return loss_blk[0, 0], dx, G, dmod, dsmall


SHARDED = ("w_ffn1_in", "w_ffn1_out", "w_in", "w_pool_branch", "w_attn_branch", "w_out", "w_ffn2_in", "w_ffn2_out")
TRANSPOSED = ("w_ffn1_in", "w_in", "w_ffn2_in")
FIRST = ("w_ffn1_in", "w_ffn1_out")
LATER = tuple(n for n in SHARDED if n not in FIRST)
GRAD_KEY = dict(w_ffn1_in="w1in", w_ffn1_out="w1out", w_in="win", w_pool_branch="wpb", w_attn_branch="wab",
                w_out="wout", w_ffn2_in="w2in", w_ffn2_out="w2out")


def _cols_to_full(g):
    return jnp.concatenate([g[j] for j in range(N_DEV)], axis=1)


def _full_to_cols(t):
    c = t.shape[1] // N_DEV
    return jnp.stack([t[:, j * c:(j + 1) * c] for j in range(N_DEV)])


SMALL = (("b_ada", 72), ("g_norm_ffn1", 8), ("g_norm_mix", 8), ("g_norm_ffn2", 8), ("g_final", 8),
         ("w_pool", 128), ("pool_scale", 8))


def _pack_small(vals, loss):
    rows = []
    for name, nrows in SMALL:
        t = vals[name].reshape(-1, 128)
        rows.append(jnp.pad(t, ((0, nrows - t.shape[0]), (0, 0))))
    rows.append(jnp.full((8, 128), loss, F32))
    return jnp.concatenate(rows)


def _unpack_small(slab, shapes):
    out, off = {}, 0
    for name, nrows in SMALL:
        used = 1
        for d in shapes[name]:
            used *= d
        out[name] = slab[off:off + used // 128].reshape(shapes[name])
        off += nrows
    return out, slab[off, 0]


def _as_2d(t):
    return t.reshape(-1, t.shape[-1])


def kernel(x, c, positions, w_ada, b_ada, g_norm_ffn1, w_ffn1_in, w_ffn1_out, g_norm_mix, w_in, w_pool, pool_scale, w_pool_branch, w_attn_branch, w_out, g_norm_ffn2, w_ffn2_in, w_ffn2_out, g_final, loss_target, m_w_ada, m_b_ada, m_g_norm_ffn1, m_w_ffn1_in, m_w_ffn1_out, m_g_norm_mix, m_w_in, m_w_pool, m_pool_scale, m_w_pool_branch, m_w_attn_branch, m_w_out, m_g_norm_ffn2, m_w_ffn2_in, m_w_ffn2_out, m_g_final, v_w_ada, v_b_ada, v_g_norm_ffn1, v_w_ffn1_in, v_w_ffn1_out, v_g_norm_mix, v_w_in, v_w_pool, v_pool_scale, v_w_pool_branch, v_w_attn_branch, v_w_out, v_g_norm_ffn2, v_w_ffn2_in, v_w_ffn2_out, v_g_final):
    names = ["w_ada", "b_ada", "g_norm_ffn1", "w_ffn1_in", "w_ffn1_out", "g_norm_mix", "w_in", "w_pool", "pool_scale",
             "w_pool_branch", "w_attn_branch", "w_out", "g_norm_ffn2", "w_ffn2_in", "w_ffn2_out", "g_final"]
    w = dict(w_ada=w_ada, b_ada=b_ada, g_norm_ffn1=g_norm_ffn1, w_ffn1_in=w_ffn1_in, w_ffn1_out=w_ffn1_out,
             g_norm_mix=g_norm_mix, w_in=w_in, w_pool=w_pool, pool_scale=pool_scale, w_pool_branch=w_pool_branch,
             w_attn_branch=w_attn_branch, w_out=w_out, g_norm_ffn2=g_norm_ffn2, w_ffn2_in=w_ffn2_in,
             w_ffn2_out=w_ffn2_out, g_final=g_final)
    m = dict(w_ada=m_w_ada, b_ada=m_b_ada, g_norm_ffn1=m_g_norm_ffn1, w_ffn1_in=m_w_ffn1_in, w_ffn1_out=m_w_ffn1_out,
             g_norm_mix=m_g_norm_mix, w_in=m_w_in, w_pool=m_w_pool, pool_scale=m_pool_scale,
             w_pool_branch=m_w_pool_branch, w_attn_branch=m_w_attn_branch, w_out=m_w_out, g_norm_ffn2=m_g_norm_ffn2,
             w_ffn2_in=m_w_ffn2_in, w_ffn2_out=m_w_ffn2_out, g_final=m_g_final)
    v = dict(w_ada=v_w_ada, b_ada=v_b_ada, g_norm_ffn1=v_g_norm_ffn1, w_ffn1_in=v_w_ffn1_in, w_ffn1_out=v_w_ffn1_out,
             g_norm_mix=v_g_norm_mix, w_in=v_w_in, w_pool=v_w_pool, pool_scale=v_pool_scale,
             w_pool_branch=v_w_pool_branch, w_attn_branch=v_w_attn_branch, w_out=v_w_out, g_norm_ffn2=v_g_norm_ffn2,
             w_ffn2_in=v_w_ffn2_in, w_ffn2_out=v_w_ffn2_out, g_final=v_g_final)
    shapes = {n: w[n].shape for n in names}
    me = 4 * lax.axis_index("x") + 2 * lax.axis_index("y") + lax.axis_index("c")
    D = x.shape[-1]
    n_mod = w_ada.shape[-1] * N_DEV // D

    (c_all,) = _all_gather([c.reshape(D // 128, 128)], "gather_c")
    ada_cols = w_ada.shape[-1]
    b_mine = lax.dynamic_slice_in_dim(b_ada, me * ada_cols, ada_cols, axis=1)
    cond, mod_part = _ada_mod(c_all.reshape(N_DEV, D), w_ada[0], b_mine, "ada_mod")
    (mod_all,) = _all_gather([mod_part.reshape(-1, 128)], "gather_mod")
    mod_all = mod_all.reshape(N_DEV, N_DEV, ada_cols)
    mod = lax.dynamic_index_in_dim(mod_all, me, axis=1, keepdims=False).reshape(n_mod, D)

    def local(t, name):
        return t[name][0].T if name in TRANSPOSED else t[name][0]

    shards = {name: local(w, name).astype(BF16) for name in SHARDED}

    def gather_done(names, fulls):
        return {name: lax.dynamic_update_index_in_dim(full, shards[name], me, axis=0)
                for name, full in zip(names, fulls)}

    def ffn_weights(g, pre):
        return {"w%sin" % pre: g["w_ffn%s_in" % pre].reshape(2, -1, D),
                "w%sout" % pre: g["w_ffn%s_out" % pre].reshape(-1, D)}

    def later_weights(fulls):
        g = gather_done(LATER, fulls)
        return dict(win=g["w_in"].reshape(-1, D), wpb=_cols_to_full(g["w_pool_branch"]),
                    wab=_cols_to_full(g["w_attn_branch"]), wout=g["w_out"].reshape(D, D), **ffn_weights(g, "2"))

    W = ffn_weights(gather_done(FIRST, _all_gather([shards[n] for n in FIRST], "gather_ffn1", own=False)), "1")
    small = dict(g1=g_norm_ffn1[0], g2=g_norm_mix[0], g3=g_norm_ffn2[0], gf=g_final, w_pool=w_pool[0],
                 pool_scale=pool_scale[0])
    red = _GradReducer()
    loss_part, dx, G, dmod, dsmall = _local_step(
        x[0], loss_target[0], positions[0], mod, small, W,
        late=(_gather_rider([shards[n] for n in LATER]), later_weights), red=red)
    chip = (2 * lax.axis_index("x") + lax.axis_index("y")).astype(jnp.int32).reshape(1)

    part = _pack_small(dict(b_ada=dmod, g_norm_ffn1=dsmall["g1"], g_norm_mix=dsmall["g2"], g_norm_ffn2=dsmall["g3"],
                            g_final=dsmall["gf"], w_pool=dsmall["w_pool"], pool_scale=dsmall["pool_scale"]),
                       loss_part)
    (parts,) = _all_gather([part], "gather_small")
    gsmall, loss = _unpack_small(_sum_slots(parts, "sum_small"), shapes)
    rows_mine = ada_cols // 128
    dmod_mine = lax.dynamic_slice_in_dim(parts, me * rows_mine, rows_mine, axis=1).reshape(N_DEV, ada_cols)

    grads, delta, new_m, new_v = {}, {}, {}, {}
    grads["w_ada"], delta["w_ada"], new_m["w_ada"], new_v["w_ada"] = (
        t[None] for t in _ada_grad_adamw(cond.T, dmod_mine, w_ada[0], m_w_ada[0], v_w_ada[0], "ada_grad_adamw"))
    for name in SHARDED:
        key = GRAD_KEY[name]
        res = _sum_adamw(chip, red.own[key], red.others[key], local(w, name), local(m, name), local(v, name),
                         "adamw_" + name)
        grads[name], delta[name], new_m[name], new_v[name] = (
            (t.T if name in TRANSPOSED else t)[None] for t in res)
    small_names = [name for name, _ in SMALL]
    res = _adamw_many(*([_as_2d(t[name]) for name in small_names] for t in (w, gsmall, m, v)), "adamw_small")
    for dst, vals in zip((delta, new_m, new_v), res):
        dst.update({name: t.reshape(shapes[name]) for name, t in zip(small_names, vals)})
    grads.update(gsmall)

    return (loss, dx[None], *[grads[n] for n in names], *[delta[n] for n in names],
            *[new_m[n] for n in names], *[new_v[n] for n in names])
```

```python
import functools

import jax
import jax.numpy as jnp
from jax import lax
from jax.experimental import pallas as pl
from jax.experimental.pallas import tpu as pltpu

F32 = jnp.float32
BF16 = jnp.bfloat16
MESH = pl.DeviceIdType.MESH
ANY = pl.BlockSpec(memory_space=pl.ANY)

N_DEV = 8
EPS = 1e-6
HEAD_DIM = 64
HEADS = 4
GW = HEADS * HEAD_DIM
DILATIONS = (1, 4, 16)
BAND = 128
QB = 128
POOL_WINDOWS = (2, 4, 8, 16)
HALO = 16
ROPE_THETA = 10000.0

ADAM_LR = 0.001
ADAM_B1 = 0.9
ADAM_B2 = 0.999
ADAM_EPS = 1e-08
ADAM_WD = 0.01
ADAM_STEP = 10

VMEM_LIMIT = 56 * 1024 * 1024
TS = 512
FFN_TS = 256
FFN_CHUNKS = (2816,)

NT = (((1,), (1,)), ((), ()))
TN = (((0,), (0,)), ((), ()))


def _params(**kw):
    return pltpu.CompilerParams(vmem_limit_bytes=VMEM_LIMIT, **kw)


def _dot(a, b):
    return jnp.dot(a, b, preferred_element_type=F32)


def _dot_nt(a, b):
    return lax.dot_general(a, b, NT, preferred_element_type=F32)


def _dot_tn(a, b):
    return lax.dot_general(a, b, TN, preferred_element_type=F32)


def _load_weights(pairs, sem):
    @pl.when(pl.program_id(0) == 0)
    def _():
        copies = [pltpu.make_async_copy(src, dst, sem.at[i]) for i, (src, dst) in enumerate(pairs)]
        for cp in copies:
            cp.start()
        for cp in copies:
            cp.wait()


def _norm_mod(x, g, sc, sh):
    r = lax.rsqrt(jnp.mean(x * x, axis=-1, keepdims=True) + EPS)
    xn = x * r
    y = xn * g
    return r, xn, y, y * (1.0 + sc) + sh


def _norm_mod_bwd(du, r, xn, y, g, sc):
    dsh = jnp.sum(du, axis=0, keepdims=True)
    dsc = jnp.sum(du * y, axis=0, keepdims=True)
    dy = du * (1.0 + sc)
    dg = jnp.sum(dy * xn, axis=0, keepdims=True)
    dxn = dy * g
    dx = r * (dxn - xn * jnp.mean(dxn * xn, axis=-1, keepdims=True))
    return dx, dsh, dsc, dg


def _row_tile(ts, width):
    return pl.BlockSpec((ts, width), lambda i: (i, 0))


def _const(shape):
    return pl.BlockSpec(shape, lambda *_: (0,) * len(shape))


def _ffn_chunks(Fd):
    assert sum(FFN_CHUNKS) == Fd
    edges = [sum(FFN_CHUNKS[:k]) for k in range(len(FFN_CHUNKS) + 1)]
    return [slice(a, b) for a, b in zip(edges[:-1], edges[1:])]
def _final_tile(x, g, target):
    r = lax.rsqrt(jnp.mean(x * x, axis=-1, keepdims=True) + EPS)
    xn = x * r
    err = xn * g - target
    loss = 0.5 * jnp.sum(jnp.mean(err * err, axis=-1, keepdims=True))
    dy = err * (1.0 / x.shape[-1])
    dg = jnp.sum(dy * xn, axis=0, keepdims=True)
    dxn = dy * g
    return r * (dxn - xn * jnp.mean(dxn * xn, axis=-1, keepdims=True)), loss, dg


def _ffn_fwd(h, vec, win, wout, name, rider=None, final=None):
    TS = FFN_TS
    S, D = h.shape
    _, Fd, _ = win.shape
    n_in, n_out = (6, 7) if final else (4, 5)

    def body(*refs):
        if rider is None:
            return compute(*refs)
        host, mine = rider.split(refs, n_in, n_out)
        rider.head(mine, pl.program_id(0))
        compute(*host)
        rider.tail(mine, pl.program_id(0), S // TS)

    def compute(*refs):
        h_ref, vec_ref, win_hbm, wout_hbm = refs[:4]
        hn_ref, u_ref, ab_ref, act_ref, f_ref = refs[n_in:n_in + 5]
        win_v, wout_v, sem = refs[n_in + n_out:]
        _load_weights([(win_hbm, win_v), (wout_hbm, wout_v)], sem)
        x = h_ref[...]
        g, sh, sc, gt = (vec_ref[k:k + 1, :] for k in range(4))
        _, _, _, u = _norm_mod(x, g, sc, sh)
        ub = u.astype(BF16)
        u_ref[...] = ub
        acc = jnp.zeros((TS, D), F32)
        for sl in _ffn_chunks(Fd):
            a = _dot_nt(ub, win_v[0, sl, :])
            b = _dot_nt(ub, win_v[1, sl, :])
            act = ((a * jax.nn.sigmoid(a)) * b).astype(BF16)
            ab_ref[0, :, sl] = a.astype(BF16)
            ab_ref[1, :, sl] = b.astype(BF16)
            act_ref[:, sl] = act
            acc = acc + _dot(act, wout_v[sl, :])
        f_ref[...] = acc.astype(BF16)
        hn = x + (0.5 * gt) * acc
        if not final:
            hn_ref[...] = hn
            return
        t_ref, gf_ref = refs[4:6]
        loss_ref, dgf_ref = refs[n_in + 5:n_in + 7]

        @pl.when(pl.program_id(0) == 0)
        def _():
            loss_ref[...] = jnp.zeros_like(loss_ref)
            dgf_ref[...] = jnp.zeros_like(dgf_ref)

        hn_ref[...], loss, dg = _final_tile(hn, gf_ref[0:1, :], t_ref[...])
        loss_ref[...] += loss
        dgf_ref[0:1, :] += dg

    specs = (
        [_row_tile(TS, D), _const((8, D)), ANY, ANY] + ([_row_tile(TS, D), _const((8, D))] if final else []),
        [_row_tile(TS, D), _row_tile(TS, D), pl.BlockSpec((2, TS, Fd), lambda i: (0, i, 0)),
         _row_tile(TS, Fd), _row_tile(TS, D)] + ([_const((8, 128)), _const((8, D))] if final else []),
        [jax.ShapeDtypeStruct((S, D), F32), jax.ShapeDtypeStruct((S, D), BF16),
         jax.ShapeDtypeStruct((2, S, Fd), BF16), jax.ShapeDtypeStruct((S, Fd), BF16),
         jax.ShapeDtypeStruct((S, D), BF16)]
        + ([jax.ShapeDtypeStruct((8, 128), F32), jax.ShapeDtypeStruct((8, D), F32)] if final else []),
        [pltpu.VMEM(win.shape, BF16), pltpu.VMEM(wout.shape, BF16), pltpu.SemaphoreType.DMA((2,))])
    in_specs, out_specs, out_shape, scratch = specs if rider is None else rider.specs(*specs)
    outs = pl.pallas_call(
        body, name=name, grid=(S // TS,), in_specs=in_specs, out_specs=out_specs, out_shape=out_shape,
        scratch_shapes=scratch, compiler_params=_params(),
    )(h, vec, win, wout, *(final or ()), *(rider.arrays if rider else []))
    return outs if rider is None else (outs[:n_out], outs[n_out:])


def _ffn_bwd(dh, h, f, ab, vec, win, wout, name):
    TS = FFN_TS
    S, D = h.shape
    _, Fd, _ = win.shape

    def body(dh_ref, h_ref, f_ref, ab_ref, vec_ref, win_hbm, wout_hbm,
             dhp_ref, dab_ref, df_ref, red_ref, win_v, wout_v, sem):
        _load_weights([(win_hbm, win_v), (wout_hbm, wout_v)], sem)

        @pl.when(pl.program_id(0) == 0)
        def _():
            red_ref[...] = jnp.zeros_like(red_ref)

        dh_v = dh_ref[...]
        x = h_ref[...]
        g, sh, sc, gt = (vec_ref[k:k + 1, :] for k in range(4))
        dgt = jnp.sum((0.5 * f_ref[...].astype(F32)) * dh_v, axis=0, keepdims=True)
        dfb = ((0.5 * gt) * dh_v).astype(BF16)
        df_ref[...] = dfb
        du = jnp.zeros((TS, D), F32)
        for sl in _ffn_chunks(Fd):
            dact = _dot_nt(dfb, wout_v[sl, :])
            av = ab_ref[0, :, sl].astype(F32)
            bv = ab_ref[1, :, sl].astype(F32)
            sg = jax.nn.sigmoid(av)
            da = (dact * bv * (sg * (1.0 + av * (1.0 - sg)))).astype(BF16)
            db = (dact * (av * sg)).astype(BF16)
            dab_ref[0, :, sl] = da
            dab_ref[1, :, sl] = db
            du = du + _dot(da, win_v[0, sl, :]) + _dot(db, win_v[1, sl, :])
        r, xn, y, _ = _norm_mod(x, g, sc, sh)
        dx, dsh, dsc, dg = _norm_mod_bwd(du, r, xn, y, g, sc)
        dhp_ref[...] = dh_v + dx
        red_ref[0:1, :] += dgt
        red_ref[1:2, :] += dsh
        red_ref[2:3, :] += dsc
        red_ref[3:4, :] += dg

    ab_spec = pl.BlockSpec((2, TS, Fd), lambda i: (0, i, 0))
    return pl.pallas_call(
        body, name=name, grid=(S // TS,),
        in_specs=[_row_tile(TS, D), _row_tile(TS, D), _row_tile(TS, D), ab_spec, _const((8, D)), ANY, ANY],
        out_specs=[_row_tile(TS, D), ab_spec, _row_tile(TS, D), _const((8, D))],
        out_shape=[jax.ShapeDtypeStruct((S, D), F32), jax.ShapeDtypeStruct((2, S, Fd), BF16),
                   jax.ShapeDtypeStruct((S, D), BF16), jax.ShapeDtypeStruct((8, D), F32)],
        scratch_shapes=[pltpu.VMEM(win.shape, BF16), pltpu.VMEM(wout.shape, BF16), pltpu.SemaphoreType.DMA((2,))],
        compiler_params=_params(),
    )(dh, h, f, ab, vec, win, wout)


def _wgrad(x, y, name, tm=None, ts=2048, rider=None):
    xb = x.ndim == 3
    nb = x.shape[0] if xb else 0
    S, M = x.shape[-2:]
    N = y.shape[-1]
    tm = tm or M
    ts = min(ts, S)
    nk = S // ts
    grid = (max(nb, 1), M // tm, nk)

    def body(*refs):
        if rider is None:
            return compute(*refs)
        host, mine = rider.split(refs, 2, 1)
        step = (pl.program_id(0) * grid[1] + pl.program_id(1)) * grid[2] + pl.program_id(2)
        rider.head(mine, step)
        compute(*host)
        rider.tail(mine, step, grid[0] * grid[1] * grid[2])

    def compute(x_ref, y_ref, o_ref, acc):
        k = pl.program_id(2)

        @pl.when(k == 0)
        def _():
            acc[...] = jnp.zeros_like(acc)

        acc[...] += _dot_tn(x_ref[...], y_ref[...])

        @pl.when(k == nk - 1)
        def _():
            o_ref[...] = acc[...].astype(BF16)

    x_spec = (pl.BlockSpec((None, ts, tm), lambda b, i, k: (b, k, i)) if xb
              else pl.BlockSpec((ts, tm), lambda b, i, k: (k, i)))
    y_spec = pl.BlockSpec((ts, N), lambda b, i, k: (k, 0))
    if xb:
        o_spec, o_shape = pl.BlockSpec((None, tm, N), lambda b, i, k: (b, i, 0)), (nb, M, N)
    else:
        o_spec, o_shape = pl.BlockSpec((tm, N), lambda b, i, k: (i, 0)), (M, N)
    specs = ([x_spec, y_spec], [o_spec], [jax.ShapeDtypeStruct(o_shape, BF16)], [pltpu.VMEM((tm, N), F32)])
    in_specs, out_specs, out_shape, scratch = specs if rider is None else rider.specs(*specs)
    outs = pl.pallas_call(
        body, name=name, grid=grid, in_specs=in_specs, out_specs=out_specs, out_shape=out_shape,
        scratch_shapes=scratch, compiler_params=_params(),
    )(x, y, *(rider.arrays if rider else []))
    return outs[0] if rider is None else (outs[0], outs[1:])


P_OFF, Q_OFF, K_OFF, V_OFF, G_OFF = 0, 256, 1024, 1792, 2560
IN_WIDTH = 4608


def _first_half_mask(ts):
    lane = lax.broadcasted_iota(jnp.int32, (ts, 128), 1)
    return (lane % HEAD_DIM) < (HEAD_DIM // 2)


def _rope(t, cos, sin_signed, first, sign):
    partner = jnp.where(first, pltpu.roll(t, 96, 1), pltpu.roll(t, 32, 1))
    return t * cos + sign * (partner * sin_signed)


def _res_spec(r):
    return pl.BlockSpec((r, TS // r, GW), lambda i: (0, i, 0))


def _res_shape(S, r, dtype):
    return jax.ShapeDtypeStruct((r, S // r, GW), dtype)


def _to_residues(piece, out_ref, lanes, r, scr):
    if r == 1:
        out_ref[0, :, lanes] = piece.astype(out_ref.dtype)
        return
    for h in range(piece.shape[1] // 128):
        scr[h] = piece[:, h * 128:(h + 1) * 128]
        at = slice(lanes.start + h * 128, lanes.start + (h + 1) * 128)
        for res in range(r):
            out_ref[res, :, at] = scr[h, pl.ds(res, TS // r, stride=r), :].astype(out_ref.dtype)


def _from_residues(in_ref, lanes, r, scr):
    if r == 1:
        return in_ref[0, :, lanes].astype(F32)
    halves = (lanes.stop - lanes.start) // 128
    for h in range(halves):
        at = slice(lanes.start + h * 128, lanes.start + (h + 1) * 128)
        for res in range(r):
            scr[h, pl.ds(res, TS // r, stride=r), :] = in_ref[res, :, at].astype(F32)
    return scr[0] if halves == 1 else jnp.concatenate([scr[0], scr[1]], axis=1)


RES_SCRATCH = (2, TS, 128)


def _mix_in_fwd(h, vec, cos, sin, win, name):
    S, D = h.shape

    def body(h_ref, vec_ref, cos_ref, sin_ref, win_hbm, u_ref, p_ref, gates_ref, *rest):
        qkv_refs, (win_v, sem, scr) = rest[:9], rest[9:]
        _load_weights([(win_hbm, win_v)], sem)
        g, sh, sc = (vec_ref[k:k + 1, :] for k in range(3))
        _, _, _, u = _norm_mod(h_ref[...], g, sc, sh)
        ub = u.astype(BF16)
        u_ref[...] = ub
        p_ref[...] = _dot_nt(ub, win_v[P_OFF:Q_OFF, :])
        cosv, sinv = cos_ref[...], sin_ref[...]
        first = _first_half_mask(TS)
        for which, off in enumerate((Q_OFF, K_OFF, V_OFF)):
            t = _dot_nt(ub, win_v[off:off + 3 * GW, :])
            for gi in range(3):
                for half in range(2):
                    c0 = gi * GW + half * 128
                    piece = t[:, c0:c0 + 128]
                    if which < 2:
                        piece = _rope(piece, cosv, sinv, first, 1.0)
                    _to_residues(piece, qkv_refs[which * 3 + gi], slice(half * 128, (half + 1) * 128),
                                 DILATIONS[gi], scr)
        gates_ref[...] = jax.nn.sigmoid(_dot_nt(ub, win_v[G_OFF:IN_WIDTH, :])).astype(BF16)

    return pl.pallas_call(
        body, name=name, grid=(S // TS,),
        in_specs=[_row_tile(TS, D), _const((8, D)), _row_tile(TS, 128), _row_tile(TS, 128), ANY],
        out_specs=[_row_tile(TS, D), _row_tile(TS, GW), _row_tile(TS, 2 * D)] + [_res_spec(r) for r in DILATIONS] * 3,
        out_shape=[jax.ShapeDtypeStruct((S, D), BF16), jax.ShapeDtypeStruct((S, GW), F32),
                   jax.ShapeDtypeStruct((S, 2 * D), BF16)] + [_res_shape(S, r, BF16) for r in DILATIONS] * 3,
        scratch_shapes=[pltpu.VMEM((IN_WIDTH, D), BF16), pltpu.SemaphoreType.DMA((1,)), pltpu.VMEM(RES_SCRATCH, F32)],
        compiler_params=_params(),
    )(h, vec, cos, sin, win)


def _mix_in_bwd(dh, h, vec, cos, sin, dp, dqkv, dgl, win, name):
    S, D = h.shape

    def body(dh_ref, h_ref, vec_ref, cos_ref, sin_ref, dp_ref, *rest):
        dqkv_refs = rest[:9]
        dgl_ref, win_hbm, dhp_ref, dproj_ref, red_ref, win_v, sem, scr = rest[9:]
        _load_weights([(win_hbm, win_v)], sem)

        @pl.when(pl.program_id(0) == 0)
        def _():
            red_ref[...] = jnp.zeros_like(red_ref)

        cosv, sinv = cos_ref[...], sin_ref[...]
        first = _first_half_mask(TS)
        dproj_ref[:, P_OFF:Q_OFF] = dp_ref[...].astype(BF16)
        for which, off in enumerate((Q_OFF, K_OFF, V_OFF)):
            for gi in range(3):
                for half in range(2):
                    piece = _from_residues(dqkv_refs[which * 3 + gi], slice(half * 128, (half + 1) * 128),
                                           DILATIONS[gi], scr)
                    if which < 2:
                        piece = _rope(piece, cosv, sinv, first, -1.0)
                    c0 = off + gi * GW + half * 128
                    dproj_ref[:, c0:c0 + 128] = piece.astype(BF16)
        dproj_ref[:, G_OFF:IN_WIDTH] = dgl_ref[...]
        du = _dot(dproj_ref[...], win_v[...])
        g, sh, sc = (vec_ref[k:k + 1, :] for k in range(3))
        r, xn, y, _ = _norm_mod(h_ref[...], g, sc, sh)
        dx, dsh, dsc, dg = _norm_mod_bwd(du, r, xn, y, g, sc)
        dhp_ref[...] = dh_ref[...] + dx
        red_ref[1:2, :] += dsh
        red_ref[2:3, :] += dsc
        red_ref[3:4, :] += dg

    return pl.pallas_call(
        body, name=name, grid=(S // TS,),
        in_specs=[_row_tile(TS, D), _row_tile(TS, D), _const((8, D)), _row_tile(TS, 128), _row_tile(TS, 128),
                  _row_tile(TS, GW)] + [_res_spec(r) for r in DILATIONS] * 3 + [_row_tile(TS, 2 * D), ANY],
        out_specs=[_row_tile(TS, D), _row_tile(TS, IN_WIDTH), _const((8, D))],
        out_shape=[jax.ShapeDtypeStruct((S, D), F32), jax.ShapeDtypeStruct((S, IN_WIDTH), BF16),
                   jax.ShapeDtypeStruct((8, D), F32)],
        scratch_shapes=[pltpu.VMEM((IN_WIDTH, D), BF16), pltpu.SemaphoreType.DMA((1,)), pltpu.VMEM(RES_SCRATCH, F32)],
        compiler_params=_params(),
    )(dh, h, vec, cos, sin, dp, *dqkv, dgl, win)


def _pool_lanes(rows):
    lane = lax.broadcasted_iota(jnp.int32, (rows, GW), 1)
    return lane // HEAD_DIM


def _pool_window(rows):
    grp = _pool_lanes(rows)
    w = jnp.full((rows, GW), POOL_WINDOWS[0], jnp.int32)
    for k in range(1, len(POOL_WINDOWS)):
        w = jnp.where(grp == k, POOL_WINDOWS[k], w)
    return grp, w


def _pool_fwd(p, wbd, scale, name, ts=512):
    S = p.shape[0]
    ext = ts + HALO

    def body(pc_ref, ph_ref, wbd_ref, sc_ref, d_ref, y_ref):
        i = pl.program_id(0)
        cur = pc_ref[...]
        halo = jnp.where(i > 0, ph_ref[...], 0.0)
        s = jnp.concatenate([halo, cur], axis=0)
        grp, w = _pool_window(ext)
        sel = jnp.zeros((ext, GW), F32)
        for k, wk in enumerate(POOL_WINDOWS):
            s = s + pltpu.roll(s, wk // 2, 0)
            sel = jnp.where(grp == k, s, sel)
        t = i * ts + lax.broadcasted_iota(jnp.int32, (ts, GW), 0)
        count = jnp.minimum(t + 1, w[HALO:]).astype(F32)
        d = (sel[HALO:] / count - cur).astype(BF16)
        d_ref[...] = d
        y_ref[...] = (_dot(d, wbd_ref[...]) * sc_ref[...]).astype(BF16)

    return pl.pallas_call(
        body, name=name, grid=(S // ts,),
        in_specs=[_row_tile(ts, GW),
                  pl.BlockSpec((HALO, GW), lambda i: (jnp.maximum(i * (ts // HALO) - 1, 0), 0)),
                  _const((GW, GW)), _const((1, GW))],
        out_specs=[_row_tile(ts, GW), _row_tile(ts, GW)],
        out_shape=[jax.ShapeDtypeStruct((S, GW), BF16), jax.ShapeDtypeStruct((S, GW), BF16)],
        compiler_params=_params(),
    )(p, p, wbd, scale)


def _pool_bwd(dy, d, wbd, scale, name, ts=512):
    S = dy.shape[0]
    ext = ts + HALO
    nsteps = S // ts
    last_halo = S // HALO - 1

    def body(dyc_ref, dyh_ref, d_ref, wbd_ref, sc_ref, dp_ref, dw_ref, ds_ref):
        i = pl.program_id(0)

        @pl.when(i == 0)
        def _():
            dw_ref[...] = jnp.zeros_like(dw_ref)
            ds_ref[...] = jnp.zeros_like(ds_ref)

        dyc = dyc_ref[...]
        dyh = jnp.where(i < nsteps - 1, dyh_ref[...], 0.0)
        dys = (jnp.concatenate([dyc, dyh], axis=0) * sc_ref[...]).astype(BF16)
        dd = _dot_nt(dys, wbd_ref[...])
        grp, w = _pool_window(ext)
        t = i * ts + lax.broadcasted_iota(jnp.int32, (ext, GW), 0)
        s = dd / jnp.minimum(t + 1, w).astype(F32)
        sel = jnp.zeros((ext, GW), F32)
        for k, wk in enumerate(POOL_WINDOWS):
            s = s + pltpu.roll(s, ext - wk // 2, 0)
            sel = jnp.where(grp == k, s, sel)
        dp_ref[...] = sel[:ts] - dd[:ts]
        dv = d_ref[...]
        z = _dot(dv, wbd_ref[...])
        ds_ref[0:1, :] += jnp.sum(dyc * z, axis=0, keepdims=True)
        dw_ref[...] += _dot_tn(dv, dys[:ts])

    return pl.pallas_call(
        body, name=name, grid=(nsteps,),
        in_specs=[_row_tile(ts, GW),
                  pl.BlockSpec((HALO, GW), lambda i: (jnp.minimum((i + 1) * (ts // HALO), last_halo), 0)),
                  _row_tile(ts, GW), _const((GW, GW)), _const((1, GW))],
        out_specs=[_row_tile(ts, GW), _const((GW, GW)), _const((8, GW))],
        out_shape=[jax.ShapeDtypeStruct((S, GW), F32), jax.ShapeDtypeStruct((GW, GW), F32),
                   jax.ShapeDtypeStruct((8, GW), F32)],
        compiler_params=_params(),
    )(dy, dy, d, wbd, scale)


def _head_id(rows):
    return lax.broadcasted_iota(jnp.int32, (rows, GW), 1) // HEAD_DIM


def _stack_heads(t, hid):
    return jnp.concatenate([jnp.where(hid == h, t, jnp.zeros_like(t)) for h in range(HEADS)], axis=0)


def _unstack_heads(t_all, hid):
    out = jnp.zeros((QB, GW), F32)
    for h in range(HEADS):
        out = jnp.where(hid == h, t_all[h * QB:(h + 1) * QB], out)
    return out


def _band_mask(n):
    row = lax.broadcasted_iota(jnp.int32, (HEADS * QB, 2 * QB), 0) % QB
    col = lax.broadcasted_iota(jnp.int32, (HEADS * QB, 2 * QB), 1)
    rel = row + QB - col
    return (rel >= 0) & (rel <= BAND) & ((col >= QB) | (n > 0))


MAX_STREAMS = 8


def _streams(r, nb):
    if r > 1:
        ns = min(r, MAX_STREAMS)
        return nb, [(lambda rb, l=l: ns * rb + l, 0) for l in range(ns)]
    ns = min(MAX_STREAMS, nb)
    return nb // ns, [(lambda rb: 0, l * (nb // ns)) for l in range(ns)]


def _attn_fwd(q, k, v, name):
    r, L, _ = q.shape
    nbs, streams = _streams(r, L // QB)
    ns = len(streams)
    grid = (max(r // ns, 1), nbs)

    def cur(res, off):
        return pl.BlockSpec((None, QB, GW), lambda rb, n: (res(rb), n + off, 0))

    def prev(res, off):
        return pl.BlockSpec((None, QB, GW), lambda rb, n: (res(rb), jnp.maximum(n + off - 1, 0), 0))

    def body(*refs):
        n = pl.program_id(1)
        hid = _head_id(QB)
        o_ref, lse_ref = refs[5 * len(streams):]
        for l, (_, off) in enumerate(streams):
            q_ref, kp_ref, kc_ref, vp_ref, vc_ref = refs[5 * l:5 * l + 5]
            qs = _stack_heads(q_ref[...], hid)
            kc = jnp.concatenate([kp_ref[...], kc_ref[...]], axis=0)
            vc = jnp.concatenate([vp_ref[...], vc_ref[...]], axis=0)
            s = _dot_nt(qs, kc) * (HEAD_DIM ** -0.5)
            s = jnp.where(_band_mask(n + off), s, -jnp.inf)
            m = jnp.max(s, axis=-1, keepdims=True)
            e = jnp.exp(s - m)
            den = jnp.sum(e, axis=-1, keepdims=True)
            lse = m + jnp.log(den)
            pr = (e * (1.0 / den)).astype(BF16)
            o_ref[l] = _unstack_heads(_dot(pr, vc), hid).astype(BF16)
            lse_ref[l] = _unstack_heads(jnp.broadcast_to(lse, (HEADS * QB, GW)), hid)

    in_specs, args = [], []
    for res, off in streams:
        in_specs += [cur(res, off), prev(res, off), cur(res, off), prev(res, off), cur(res, off)]
        args += [q, k, k, v, v]
    out = jax.ShapeDtypeStruct((ns * grid[0], nbs * QB, GW), F32)
    both = pl.BlockSpec((ns, QB, GW), lambda rb, n: (rb, n, 0))
    o, lse = pl.pallas_call(
        body, name=name, grid=grid, in_specs=in_specs, out_specs=[both, both],
        out_shape=[jax.ShapeDtypeStruct(out.shape, BF16), out],
        compiler_params=_params(),
    )(*args)
    return o.reshape(q.shape), lse.reshape(q.shape)


def _head_rows(t_full, hid):
    return jnp.concatenate(
        [jnp.max(jnp.where(hid == h, t_full, -jnp.inf), axis=-1, keepdims=True) for h in range(HEADS)], axis=0)


def _attn_bwd(q, k, v, do, lse, cterm, name):
    r, L, _ = q.shape
    nbs, streams = _streams(r, L // QB)
    ns = len(streams)
    parts = r == 1

    def spec(res, index):
        return pl.BlockSpec((None, QB, GW), lambda rb, n: (res(rb), index(n), 0))

    def body(*refs):
        dq_ref, dk_ref, dv_ref, carry_k, carry_v, seam_k, seam_v = refs[8 * ns:]
        n = pl.program_id(1)

        @pl.when(n == 0)
        def _():
            carry_k[...] = jnp.zeros_like(carry_k)
            carry_v[...] = jnp.zeros_like(carry_v)

        @pl.when(n < nbs)
        def _():
            hid = _head_id(QB)
            for l, (_, off) in enumerate(streams):
                q_ref, do_ref, lse_ref, c_ref, kp_ref, kc_ref, vp_ref, vc_ref = refs[8 * l:8 * l + 8]
                qs = _stack_heads(q_ref[...], hid)
                dos = _stack_heads(do_ref[...], hid)
                kc = jnp.concatenate([kp_ref[...], kc_ref[...]], axis=0)
                vc = jnp.concatenate([vp_ref[...], vc_ref[...]], axis=0)
                s = _dot_nt(qs, kc) * (HEAD_DIM ** -0.5)
                s = jnp.where(_band_mask(n + off), s, -jnp.inf)
                p = jnp.exp(s - _head_rows(lse_ref[...], hid))
                dp = _dot_nt(dos, vc)
                ds = (p * (dp + _head_rows(c_ref[...], hid)) * (HEAD_DIM ** -0.5)).astype(BF16)
                dq_ref[l] = _unstack_heads(_dot(ds, kc), hid).astype(BF16)
                dkc = _dot_tn(ds, qs)
                dvc = _dot_tn(p.astype(BF16), dos)
                if parts and l > 0:
                    @pl.when(n == 0)
                    def _():
                        seam_k[l] = dkc[:QB]
                        seam_v[l] = dvc[:QB]
                dk_ref[l] = (carry_k[l] + dkc[:QB]).astype(BF16)
                dv_ref[l] = (carry_v[l] + dvc[:QB]).astype(BF16)
                carry_k[l] = dkc[QB:]
                carry_v[l] = dvc[QB:]

        @pl.when(n == nbs)
        def _():
            for l in range(ns):
                if parts and l + 1 < ns:
                    dk_ref[l] = (carry_k[l] + seam_k[l + 1]).astype(BF16)
                    dv_ref[l] = (carry_v[l] + seam_v[l + 1]).astype(BF16)
                else:
                    dk_ref[l] = carry_k[l].astype(BF16)
                    dv_ref[l] = carry_v[l].astype(BF16)

    in_specs, args = [], []
    for res, off in streams:
        qside = functools.partial(lambda n, off: jnp.minimum(n, nbs - 1) + off, off=off)
        kprev = functools.partial(lambda n, off: jnp.maximum(jnp.minimum(n, nbs) - 1 + off, 0), off=off)
        in_specs += [spec(res, qside)] * 4 + [spec(res, kprev), spec(res, qside)] * 2
        args += [q, do, lse, cterm, k, k, v, v]
    out = jax.ShapeDtypeStruct((ns * max(r // ns, 1), nbs * QB, GW), BF16)
    qout = pl.BlockSpec((ns, QB, GW), lambda rb, n: (rb, jnp.minimum(n, nbs - 1), 0))
    kout = pl.BlockSpec((ns, QB, GW), lambda rb, n: (rb, jnp.maximum(n - 1, 0), 0))
    buf = pltpu.VMEM((ns, QB, GW), F32)
    outs = pl.pallas_call(
        body, name=name, grid=(max(r // ns, 1), nbs + 1),
        in_specs=in_specs, out_specs=[qout, kout, kout], out_shape=[out, out, out],
        scratch_shapes=[buf, buf, buf, buf],
        compiler_params=_params(),
    )(*args)
    return [t.reshape(q.shape) for t in outs]


def _token_order(refs, scr):
    return [_from_residues(ref, slice(0, GW), r, scr) for ref, r in zip(refs, DILATIONS)]


def _group_weights(lses):
    l0, l1, l2 = lses
    m = jnp.maximum(jnp.maximum(l0, l1), l2)
    e = [jnp.exp(l - m) for l in (l0, l1, l2)]
    den = e[0] + e[1] + e[2]
    return [ei / den for ei in e]


def _mix_out_fwd(h, vec, gates, ypool, o3, lse3, wpb, wab, wout, name):
    S, D = h.shape

    def body(h_ref, vec_ref, gates_ref, yp_ref, o0, o1, o2, l0, l1, l2, wpb_hbm, wab_hbm, wout_hbm,
             hn_ref, ya_ref, merged_ref, tm_ref, wpb_v, wab_v, wout_v, sem, scr):
        _load_weights([(wpb_hbm, wpb_v), (wab_hbm, wab_v), (wout_hbm, wout_v)], sem)
        gt = vec_ref[3:4, :]
        wts = _group_weights(_token_order((l0, l1, l2), scr))
        og = _token_order((o0, o1, o2), scr)
        ya = (wts[0] * og[0] + wts[1] * og[1] + wts[2] * og[2]).astype(BF16)
        ya_ref[...] = ya
        merged = (gates_ref[:, :D].astype(F32) * _dot(yp_ref[...], wpb_v[...])
                  + gates_ref[:, D:].astype(F32) * _dot(ya, wab_v[...])).astype(BF16)
        merged_ref[...] = merged
        tm = _dot(merged, wout_v[...])
        tm_ref[...] = tm.astype(BF16)
        hn_ref[...] = h_ref[...] + gt * tm

    grp = _row_tile(TS, GW)
    res = [_res_spec(r) for r in DILATIONS]
    return pl.pallas_call(
        body, name=name, grid=(S // TS,),
        in_specs=[_row_tile(TS, D), _const((8, D)), _row_tile(TS, 2 * D), grp] + res * 2 + [ANY, ANY, ANY],
        out_specs=[_row_tile(TS, D), grp, _row_tile(TS, D), _row_tile(TS, D)],
        out_shape=[jax.ShapeDtypeStruct((S, D), F32), jax.ShapeDtypeStruct((S, GW), BF16),
                   jax.ShapeDtypeStruct((S, D), BF16), jax.ShapeDtypeStruct((S, D), BF16)],
        scratch_shapes=[pltpu.VMEM((GW, D), BF16), pltpu.VMEM((GW, D), BF16), pltpu.VMEM((D, D), BF16),
                        pltpu.SemaphoreType.DMA((3,)), pltpu.VMEM(RES_SCRATCH, F32)],
        compiler_params=_params(),
    )(h, vec, gates, ypool, *o3, *lse3, wpb, wab, wout)


def _mix_out_bwd(dh, tm, vec, gates, ypool, o3, lse3, wpb, wab, wout, name, rider=None):
    S, D = dh.shape

    def body(*refs):
        if rider is None:
            return compute(*refs)
        host, mine = rider.split(refs, 14, 12)
        rider.head(mine, pl.program_id(0))
        compute(*host)
        rider.tail(mine, pl.program_id(0), S // TS)

    def compute(dh_ref, tm_ref, vec_ref, gates_ref, yp_ref, o0, o1, o2, l0, l1, l2, wpb_hbm, wab_hbm, wout_hbm,
                dtm_ref, dgl_ref, dypb_ref, dyab_ref, dyp_ref, do0, do1, do2, c0, c1, c2, red_ref,
                wpb_v, wab_v, wout_v, sem, scr):
        _load_weights([(wpb_hbm, wpb_v), (wab_hbm, wab_v), (wout_hbm, wout_v)], sem)

        @pl.when(pl.program_id(0) == 0)
        def _():
            red_ref[...] = jnp.zeros_like(red_ref)

        gt = vec_ref[3:4, :]
        dh_v = dh_ref[...]
        red_ref[0:1, :] += jnp.sum(tm_ref[...].astype(F32) * dh_v, axis=0, keepdims=True)
        dtm = (gt * dh_v).astype(BF16)
        dtm_ref[...] = dtm
        dm = _dot_nt(dtm, wout_v[...])
        wts = _group_weights(_token_order((l0, l1, l2), scr))
        og = _token_order((o0, o1, o2), scr)
        ya = wts[0] * og[0] + wts[1] * og[1] + wts[2] * og[2]
        ypb = _dot(yp_ref[...], wpb_v[...])
        yab = _dot(ya.astype(BF16), wab_v[...])
        gp = gates_ref[:, :D].astype(F32)
        ga = gates_ref[:, D:].astype(F32)
        dgl_ref[:, :D] = (dm * ypb * gp * (1.0 - gp)).astype(BF16)
        dgl_ref[:, D:] = (dm * yab * ga * (1.0 - ga)).astype(BF16)
        dypb = (dm * gp).astype(BF16)
        dyab = (dm * ga).astype(BF16)
        dypb_ref[...] = dypb
        dyab_ref[...] = dyab
        dyp_ref[...] = _dot_nt(dypb, wpb_v[...])
        dya = _dot_nt(dyab, wab_v[...])
        row = lax.broadcasted_iota(jnp.int32, (GW, GW), 0) // HEAD_DIM
        col = lax.broadcasted_iota(jnp.int32, (GW, GW), 1) // HEAD_DIM
        ones = jnp.where(row == col, 1.0, 0.0).astype(F32)
        tot = jnp.dot(dya * ya, ones, preferred_element_type=F32, precision=lax.Precision.HIGHEST)
        for wg, do_ref, c_ref, r in zip(wts, (do0, do1, do2), (c0, c1, c2), DILATIONS):
            _to_residues(wg * dya, do_ref, slice(0, GW), r, scr)
            _to_residues(-(wg * tot), c_ref, slice(0, GW), r, scr)

    grp = _row_tile(TS, GW)
    res = [_res_spec(r) for r in DILATIONS]
    specs = (
        [_row_tile(TS, D), _row_tile(TS, D), _const((8, D)), _row_tile(TS, 2 * D), grp] + res * 2
        + [ANY, ANY, ANY],
        [_row_tile(TS, D), _row_tile(TS, 2 * D), _row_tile(TS, D), _row_tile(TS, D), grp]
        + res * 2 + [_const((8, D))],
        [jax.ShapeDtypeStruct((S, D), BF16), jax.ShapeDtypeStruct((S, 2 * D), BF16),
         jax.ShapeDtypeStruct((S, D), BF16), jax.ShapeDtypeStruct((S, D), BF16), jax.ShapeDtypeStruct((S, GW), F32)]
        + [_res_shape(S, r, BF16) for r in DILATIONS] + [_res_shape(S, r, F32) for r in DILATIONS]
        + [jax.ShapeDtypeStruct((8, D), F32)],
        [pltpu.VMEM((GW, D), BF16), pltpu.VMEM((GW, D), BF16), pltpu.VMEM((D, D), BF16),
         pltpu.SemaphoreType.DMA((3,)), pltpu.VMEM(RES_SCRATCH, F32)])
    in_specs, out_specs, out_shape, scratch = specs if rider is None else rider.specs(*specs)
    outs = pl.pallas_call(
        body, name=name, grid=(S // TS,), in_specs=in_specs, out_specs=out_specs, out_shape=out_shape,
        scratch_shapes=scratch, compiler_params=_params(),
    )(dh, tm, vec, gates, ypool, *o3, *lse3, wpb, wab, wout, *(rider.arrays if rider else []))
    return outs if rider is None else (outs[:12], outs[12:])


def _ada_mod(c_all, w, b, name):
    def body(c_ref, w_ref, b_ref, cond_ref, mod_ref):
        cv = c_ref[...]
        cond = cv * jax.nn.sigmoid(cv)
        cond_ref[...] = cond
        mod_ref[...] = jnp.dot(cond, w_ref[...], preferred_element_type=F32,
                               precision=lax.Precision.HIGHEST) + b_ref[...]

    return pl.pallas_call(
        body, name=name,
        out_shape=[jax.ShapeDtypeStruct(c_all.shape, F32), jax.ShapeDtypeStruct((c_all.shape[0], w.shape[1]), F32)],
        compiler_params=_params(),
    )(c_all, w, b)


def _adamw_math(w, g, m, v):
    m = ADAM_B1 * m + (1.0 - ADAM_B1) * g
    v = ADAM_B2 * v + (1.0 - ADAM_B2) * (g * g)
    m_hat = m / (1.0 - ADAM_B1 ** ADAM_STEP)
    v_hat = v / (1.0 - ADAM_B2 ** ADAM_STEP)
    delta = -ADAM_LR * (m_hat / (jnp.sqrt(v_hat) + ADAM_EPS) + ADAM_WD * w)
    return delta, m, v


def _adamw_many(ws, gs, ms, vs, name):
    n = len(ws)

    def body(*refs):
        for k in range(n):
            w_ref, g_ref, m_ref, v_ref = (refs[j * n + k] for j in range(4))
            d_ref, mo_ref, vo_ref = (refs[(4 + j) * n + k] for j in range(3))
            d_ref[...], mo_ref[...], vo_ref[...] = _adamw_math(w_ref[...], g_ref[...], m_ref[...], v_ref[...])

    outs = pl.pallas_call(
        body, name=name, out_shape=[jax.ShapeDtypeStruct(t.shape, F32) for t in ws] * 3,
        compiler_params=_params(),
    )(*ws, *gs, *ms, *vs)
    return outs[:n], outs[n:2 * n], outs[2 * n:]


def _ada_grad_adamw(cond_t, dmod, w, m, v, name, tr=256):
    R, C = w.shape
    nb = dmod.shape[0]

    def body(ct_ref, dm_ref, w_ref, m_ref, v_ref, g_ref, d_ref, mo_ref, vo_ref):
        ct = ct_ref[...]
        dm = dm_ref[...]
        g = jnp.zeros((tr, C), F32)
        for bi in range(nb):
            g = g + ct[:, bi:bi + 1] * dm[bi:bi + 1, :]
        g_ref[...] = g
        d_ref[...], mo_ref[...], vo_ref[...] = _adamw_math(w_ref[...], g, m_ref[...], v_ref[...])

    spec = _row_tile(tr, C)
    out = jax.ShapeDtypeStruct((R, C), F32)
    return pl.pallas_call(
        body, name=name, grid=(R // tr,),
        in_specs=[_row_tile(tr, nb), _const((nb, C)), spec, spec, spec],
        out_specs=[spec] * 4, out_shape=[out] * 4,
        compiler_params=_params(),
    )(cond_t, dmod, w, m, v)


def _row_step(rows, cap=256):
    for cand in range(cap, 15, -16):
        if rows % cand == 0:
            return cand
    return rows


def _slot_sum(x_ref):
    acc = x_ref[0].astype(F32)
    for k in range(1, x_ref.shape[0]):
        acc = acc + x_ref[k].astype(F32)
    return acc


def _sum_slots(x, name, out_dtype=F32):
    n, R, C = x.shape
    tr = _row_step(R)

    def body(x_ref, o_ref):
        o_ref[...] = _slot_sum(x_ref).astype(out_dtype)

    return pl.pallas_call(
        body, name=name, grid=(R // tr,),
        in_specs=[pl.BlockSpec((n, tr, C), lambda i: (0, i, 0))],
        out_specs=_row_tile(tr, C), out_shape=jax.ShapeDtypeStruct((R, C), out_dtype),
        compiler_params=_params(),
    )(x)


def _sum_pair(core, g, recv, name):
    _, _, R, C = g.shape
    tr = _row_step(R, cap=1024)

    def body(core_ref, g_ref, r_ref, o_ref):
        o_ref[...] = (g_ref[...].astype(F32) + r_ref[...].astype(F32)).astype(BF16)

    return pl.pallas_call(
        body, name=name, out_shape=jax.ShapeDtypeStruct((4, R, C), BF16),
        grid_spec=pltpu.PrefetchScalarGridSpec(
            num_scalar_prefetch=1, grid=(4, R // tr),
            in_specs=[pl.BlockSpec((None, None, tr, C), lambda k, i, core_ref: (k, core_ref[0], i, 0)),
                      pl.BlockSpec((None, tr, C), lambda k, i, core_ref: (k, i, 0))],
            out_specs=pl.BlockSpec((None, tr, C), lambda k, i, core_ref: (k, i, 0))),
        compiler_params=_params(),
    )(core, g, recv)


def _sum_adamw(chip, own, recv, w, m, v, name):
    _, R, C = own.shape
    tr = _row_step(R, cap=512)

    def body(chip_ref, own_ref, r_ref, w_ref, m_ref, v_ref, g_ref, d_ref, mo_ref, vo_ref):
        g = own_ref[...].astype(F32) + _slot_sum(r_ref)
        g_ref[...] = g
        d_ref[...], mo_ref[...], vo_ref[...] = _adamw_math(w_ref[...], g, m_ref[...], v_ref[...])

    spec = pl.BlockSpec((tr, C), lambda i, chip_ref: (i, 0))
    out = jax.ShapeDtypeStruct((R, C), F32)
    return pl.pallas_call(
        body, name=name, out_shape=[out] * 4,
        grid_spec=pltpu.PrefetchScalarGridSpec(
            num_scalar_prefetch=1, grid=(R // tr,),
            in_specs=[pl.BlockSpec((None, tr, C), lambda i, chip_ref: (chip_ref[0], i, 0)),
                      pl.BlockSpec((3, tr, C), lambda i, chip_ref: (0, i, 0)), spec, spec, spec],
            out_specs=[spec] * 4),
        compiler_params=_params(),
    )(chip, own, recv, w, m, v)


def _place():
    return lax.axis_index("x"), lax.axis_index("y"), lax.axis_index("c")


def _gather_steps(x_refs, out_refs, send_sems, recv_sems):
    n = len(x_refs)
    x, y, c = _place()
    me, sibling = (x, y, c), (x, y, 1 - c)
    chips = [(1 - x, y), (x, 1 - y), (1 - x, 1 - y)]

    def rows(a, px, py, pc):
        return out_refs[a].at[4 * px + 2 * py + pc]

    def copy(a, k, block, to, src=None):
        return pltpu.make_async_remote_copy(
            src_ref=rows(a, *block) if src is None else src, dst_ref=rows(a, *block),
            send_sem=send_sems.at[a, k], recv_sem=recv_sems.at[a, k], device_id=to, device_id_type=MESH)

    def first(a):
        return [copy(a, 0, me, sibling, src=x_refs[a])] + [
            copy(a, 1 + j, me, (*chip, c), src=x_refs[a]) for j, chip in enumerate(chips)]

    def passed(a, j):
        return copy(a, 4 + j, (*chips[j], c), sibling)

    def start():
        for a in range(n):
            for cp in first(a):
                cp.start()

    def relay():
        for j, chip in enumerate(chips):
            for a in range(n):
                copy(a, 1 + j, (*chip, c), me).wait_recv()
                passed(a, j).start()

    def finish():
        for a in range(n):
            copy(a, 0, sibling, me).wait_recv()
            for j, chip in enumerate(chips):
                copy(a, 4 + j, (*chip, 1 - c), me).wait_recv()
        for a in range(n):
            for cp in first(a) + [passed(a, j) for j in range(3)]:
                cp.wait_send()

    return start, relay, finish


def _all_gather_tree(arrs, name):
    n = len(arrs)

    def body(*refs):
        x_refs, out_refs = refs[:n], refs[n:2 * n]
        send_sems, recv_sems = refs[2 * n:]
        x, y, c = _place()
        me, sibling = (x, y, c), (x, y, 1 - c)
        xn, yn, dg = (1 - x, y), (x, 1 - y), (1 - x, 1 - y)

        def rows(a, px, py, pc):
            return out_refs[a].at[4 * px + 2 * py + pc]

        def copy(a, k, block, to, src=None):
            return pltpu.make_async_remote_copy(
                src_ref=rows(a, *block) if src is None else src, dst_ref=rows(a, *block),
                send_sem=send_sems.at[a, k], recv_sem=recv_sems.at[a, k], device_id=to, device_id_type=MESH)

        def own(a):
            return [copy(a, 0, me, sibling, src=x_refs[a]), copy(a, 1, me, (*xn, c), src=x_refs[a]),
                    copy(a, 2, me, (*yn, c), src=x_refs[a])]

        def north_hands_on(a):
            return copy(a, 3, (*xn, c), (*yn, c))

        def south_hands_on(a):
            return copy(a, 3, (*yn, c), (*xn, c))

        def to_sibling(a):
            return [copy(a, 4, (*xn, c), sibling), copy(a, 5, (*yn, c), sibling), copy(a, 6, (*dg, c), sibling)]

        for a in range(n):
            for cp in own(a):
                cp.start()
        for a in range(n):
            copy(a, 1, (*xn, c), me).wait_recv()
            to_sibling(a)[0].start()

        @pl.when(c == 1)
        def _():
            for a in range(n):
                north_hands_on(a).start()

        for a in range(n):
            copy(a, 2, (*yn, c), me).wait_recv()
            to_sibling(a)[1].start()

        @pl.when(c == 0)
        def _():
            for a in range(n):
                south_hands_on(a).start()

        for a in range(n):
            copy(a, 3, (*dg, c), me).wait_recv()
            to_sibling(a)[2].start()
        for a in range(n):
            copy(a, 0, sibling, me).wait_recv()
            copy(a, 4, (*xn, 1 - c), me).wait_recv()
            copy(a, 5, (*yn, 1 - c), me).wait_recv()
            copy(a, 6, (*dg, 1 - c), me).wait_recv()
        for a in range(n):
            for cp in own(a) + to_sibling(a):
                cp.wait_send()

        @pl.when(c == 1)
        def _():
            for a in range(n):
                north_hands_on(a).wait_send()

        @pl.when(c == 0)
        def _():
            for a in range(n):
                south_hands_on(a).wait_send()

    return pl.pallas_call(
        body, name=name, out_shape=[jax.ShapeDtypeStruct((N_DEV,) + t.shape, t.dtype) for t in arrs],
        in_specs=[ANY] * n, out_specs=[ANY] * n,
        scratch_shapes=[pltpu.SemaphoreType.DMA((n, 7)), pltpu.SemaphoreType.DMA((n, 7))],
    )(*arrs)


def _all_gather(arrs, name, own=True):
    n = len(arrs)

    def body(*refs):
        x_refs, out_refs = refs[:n], refs[n:2 * n]
        send_sems, recv_sems, local_sems = refs[2 * n:]
        me = 4 * lax.axis_index("x") + 2 * lax.axis_index("y") + lax.axis_index("c")
        mine = [pltpu.make_async_copy(x_refs[a], out_refs[a].at[me], local_sems.at[a]) for a in range(n)] if own else []
        for cp in mine:
            cp.start()
        for step in _gather_steps(x_refs, out_refs, send_sems, recv_sems):
            step()
        for cp in mine:
            cp.wait()

    return pl.pallas_call(
        body, name=name, out_shape=[jax.ShapeDtypeStruct((N_DEV,) + t.shape, t.dtype) for t in arrs],
        in_specs=[ANY] * n, out_specs=[ANY] * n,
        scratch_shapes=[pltpu.SemaphoreType.DMA((n, 7)), pltpu.SemaphoreType.DMA((n, 7)),
                        pltpu.SemaphoreType.DMA((n,))],
    )(*arrs)


def _pair_exchange(arrs, name):
    n = len(arrs)

    def body(*refs):
        g_refs, out_refs = refs[:n], refs[n:2 * n]
        send_sems, recv_sems = refs[2 * n:]
        x, y, c = _place()
        give = [pltpu.make_async_remote_copy(
            src_ref=g_refs[a].at[pl.ds(0, 4), 1 - c], dst_ref=out_refs[a], send_sem=send_sems.at[a],
            recv_sem=recv_sems.at[a], device_id=(x, y, 1 - c), device_id_type=MESH) for a in range(n)]
        for cp in give:
            cp.start()
        for cp in give:
            cp.wait()

    return pl.pallas_call(
        body, name=name,
        out_shape=[jax.ShapeDtypeStruct((4,) + t.shape[2:], t.dtype) for t in arrs],
        in_specs=[ANY] * n, out_specs=[ANY] * n,
        scratch_shapes=[pltpu.SemaphoreType.DMA((n,)), pltpu.SemaphoreType.DMA((n,))],
    )(*arrs)


def _chip_exchange_steps(p_refs, out_refs, send_sems, recv_sems):
    x, y, c = _place()
    chips = [(1 - x, y), (x, 1 - y), (1 - x, 1 - y)]

    def copies():
        return [pltpu.make_async_remote_copy(
            src_ref=p_refs[a].at[2 * px + py], dst_ref=out_refs[a].at[j], send_sem=send_sems.at[a, j],
            recv_sem=recv_sems.at[a, j], device_id=(px, py, c), device_id_type=MESH)
            for a in range(len(p_refs)) for j, (px, py) in enumerate(chips)]

    def start():
        for cp in copies():
            cp.start()

    def finish():
        for cp in copies():
            cp.wait()

    return start, finish


def _chip_exchange(arrs, name):
    n = len(arrs)

    def body(*refs):
        for step in _chip_exchange_steps(refs[:n], refs[n:2 * n], *refs[2 * n:]):
            step()

    return pl.pallas_call(
        body, name=name, out_shape=[jax.ShapeDtypeStruct((3,) + t.shape[1:], t.dtype) for t in arrs],
        in_specs=[ANY] * n, out_specs=[ANY] * n,
        scratch_shapes=[pltpu.SemaphoreType.DMA((n, 3)), pltpu.SemaphoreType.DMA((n, 3))],
    )(*arrs)


class _Rider:
    def __init__(self, arrays, out_shape, sems, steps, relay_before_end=None):
        self.arrays, self.out_shape, self.scratch, self.steps = list(arrays), out_shape, sems, steps
        self.n = len(self.arrays)
        self.relay_before_end = relay_before_end

    def specs(self, in_specs, out_specs, out_shape, scratch):
        extra = [ANY] * self.n
        return in_specs + extra, out_specs + extra, out_shape + self.out_shape, scratch + self.scratch

    def split(self, refs, n_in, n_out):
        k = self.n
        a, b = n_in + k, n_in + k + n_out
        return refs[:n_in] + refs[a:b] + refs[b + k:-2], (refs[n_in:a], refs[b:b + k], refs[-2:])

    def head(self, mine, step):
        pl.when(step == 0)(self.steps(mine[0], mine[1], *mine[2])[0])

    def tail(self, mine, step, nsteps):
        steps = self.steps(mine[0], mine[1], *mine[2])
        if self.relay_before_end is not None:
            pl.when(step == nsteps - 1 - self.relay_before_end)(steps[1])
        pl.when(step == nsteps - 1)(steps[-1])


def _gather_rider(arrs, relay_before_end=4):
    n = len(arrs)
    return _Rider(arrs, [jax.ShapeDtypeStruct((N_DEV,) + t.shape, t.dtype) for t in arrs],
                  [pltpu.SemaphoreType.DMA((n, 7)), pltpu.SemaphoreType.DMA((n, 7))], _gather_steps,
                  relay_before_end)


def _chip_exchange_rider(arrs):
    n = len(arrs)
    return _Rider(arrs, [jax.ShapeDtypeStruct((3,) + t.shape[1:], t.dtype) for t in arrs],
                  [pltpu.SemaphoreType.DMA((n, 3)), pltpu.SemaphoreType.DMA((n, 3))], _chip_exchange_steps)


def _rope_tables(positions):
    inv_freq = ROPE_THETA ** (-jnp.arange(0, HEAD_DIM, 2, dtype=F32) / HEAD_DIM)
    ang = positions.astype(F32)[:, None] * inv_freq
    cos, sin = jnp.cos(ang), jnp.sin(ang)
    return jnp.tile(cos, (1, 4)), jnp.tile(jnp.concatenate([-sin, sin], axis=1), (1, 2))


def _vec(g, shift, scale, gate):
    z = jnp.zeros_like(g)
    return jnp.stack([g, shift, scale, gate, z, z, z, z])


class _GradReducer:
    def __init__(self):
        self.core = lax.axis_index("c").astype(jnp.int32).reshape(1)
        self.own, self.others, self.waiting, self.riding = {}, {}, [], []

    def pair(self, named):
        keys = list(named)
        mine = [named[k].reshape((4, 2) + named[k].shape[1:]) for k in keys]
        theirs = _pair_exchange(mine, "reduce_pair_" + keys[0])
        for k, g, r in zip(keys, mine, theirs):
            self.own[k] = _sum_pair(self.core, g, r, "sum_pair_" + k)
        self.waiting += keys

    def rider(self):
        self.riding, self.waiting = self.waiting, []
        return _chip_exchange_rider([self.own[k] for k in self.riding])

    def landed(self, results):
        self.others.update(zip(self.riding, results))

    def flush(self, name):
        keys, self.waiting = self.waiting, []
        self.others.update(zip(keys, _chip_exchange([self.own[k] for k in keys], name)))


def _by_owner(g):
    if g.ndim == 3:
        return g if g.shape[0] == N_DEV else g.reshape(N_DEV, g.shape[1] * g.shape[0] // N_DEV, g.shape[2])
    return g.reshape(N_DEV, g.shape[0] // N_DEV, g.shape[1])


def _local_step(x, target, positions, mod, small, W, late=None, red=None):
    S, D = x.shape
    sh1, sc1, gt1, sh2, sc2, gt2, sh3, sc3, gt3 = (mod[k] for k in range(9))
    v1 = _vec(small["g1"], sh1, sc1, gt1)
    v2 = _vec(small["g2"], sh2, sc2, gt2)
    v3 = _vec(small["g3"], sh3, sc3, gt3)
    vf = _vec(small["gf"], small["gf"], small["gf"], small["gf"])
    cos, sin = _rope_tables(positions)
    wbd = jax.scipy.linalg.block_diag(*[small["w_pool"][k] for k in range(4)]).astype(BF16)
    pscale = small["pool_scale"].reshape(1, GW)

    if late is None:
        h1, u1, ab1, act1, f1 = _ffn_fwd(x, v1, W["w1in"], W["w1out"], "ffn1_fwd")
    else:
        (h1, u1, ab1, act1, f1), landed = _ffn_fwd(x, v1, W["w1in"], W["w1out"], "ffn1_fwd", rider=late[0])
        W = {**W, **late[1](landed)}
    u2, p, gates, *qkv = _mix_in_fwd(h1, v2, cos, sin, W["win"], "mix_in_fwd")
    dpool, ypool = _pool_fwd(p, wbd, pscale, "pool_fwd")
    o3, lse3 = [], []
    for gi in range(len(DILATIONS)):
        o, lse = _attn_fwd(qkv[gi], qkv[3 + gi], qkv[6 + gi], f"attn_fwd_{gi}")
        o3.append(o)
        lse3.append(lse)
    h2, ya, merged, tm = _mix_out_fwd(h1, v2, gates, ypool, o3, lse3, W["wpb"], W["wab"], W["wout"], "mix_out_fwd")
    dh3, u3, ab3, act3, f3, loss_blk, dgf = _ffn_fwd(h2, v3, W["w2in"], W["w2out"], "ffn2_fwd", final=(target, vf))

    dh2, dab3, df3, red3 = _ffn_bwd(dh3, h2, f3, ab3, v3, W["w2in"], W["w2out"], "ffn2_bwd")
    half_f = ab3.shape[2] // 2
    G = {"w2in": _by_owner(_wgrad(dab3, u3, "wgrad_2in", tm=half_f)),
         "w2out": _by_owner(_wgrad(act3, df3, "wgrad_2out", tm=half_f))}
    mix_out_args = (dh2, tm, v2, gates, ypool, o3, lse3, W["wpb"], W["wab"], W["wout"], "mix_out_bwd")
    if red is None:
        mix_out = _mix_out_bwd(*mix_out_args)
    else:
        red.pair({k: G[k] for k in ("w2in", "w2out")})
        mix_out, landed = _mix_out_bwd(*mix_out_args, rider=red.rider())
        red.landed(landed)
    (dtm, dgl, dypb, dyab, dyp, do0, do1, do2, c0, c1, c2, red2o) = mix_out
    dq3, dk3, dv3 = [], [], []
    for gi, (do, ct) in enumerate(zip((do0, do1, do2), (c0, c1, c2))):
        dq, dk, dv = _attn_bwd(qkv[gi], qkv[3 + gi], qkv[6 + gi], do, lse3[gi], ct, f"attn_bwd_{gi}")
        dq3.append(dq)
        dk3.append(dk)
        dv3.append(dv)
    dp, dwbd, dps = _pool_bwd(dyp, dpool, wbd, pscale, "pool_bwd")
    dh1, dproj, red2i = _mix_in_bwd(dh2, h1, v2, cos, sin, dp, dq3 + dk3 + dv3, dgl, W["win"], "mix_in_bwd")
    G["win"] = _by_owner(_wgrad(dproj, u2, "wgrad_in", tm=1152))
    G["wpb"] = _full_to_cols(_wgrad(ypool, dypb, "wgrad_pb"))
    G["wab"] = _full_to_cols(_wgrad(ya, dyab, "wgrad_ab"))
    G["wout"] = _by_owner(_wgrad(merged, dtm, "wgrad_out"))
    if red is not None:
        red.pair({k: G[k] for k in ("win", "wpb", "wab", "wout")})
    dx, dab1, df1, red1 = _ffn_bwd(dh1, x, f1, ab1, v1, W["w1in"], W["w1out"], "ffn1_bwd")
    if red is None:
        G["w1in"] = _by_owner(_wgrad(dab1, u1, "wgrad_1in", tm=half_f))
        G["w1out"] = _by_owner(_wgrad(act1, df1, "wgrad_1out", tm=half_f))
    else:
        g1in, landed = _wgrad(dab1, u1, "wgrad_1in", tm=half_f, rider=red.rider())
        red.landed(landed)
        G["w1in"] = _by_owner(g1in)
        red.pair({"w1in": G["w1in"]})
        g1out, landed = _wgrad(act1, df1, "wgrad_1out", tm=half_f, rider=red.rider())
        red.landed(landed)
        G["w1out"] = _by_owner(g1out)
        red.pair({"w1out": G["w1out"]})
        red.flush("reduce_chips_w1out")
    dmod = jnp.stack([red1[1], red1[2], red1[0], red2i[1], red2i[2], red2o[0], red3[1], red3[2], red3[0]])
    dsmall = {
        "g1": red1[3], "g2": red2i[3], "g3": red3[3], "gf": dgf[0],
        "w_pool": jnp.stack([dwbd[k * 64:(k + 1) * 64, k * 64:(k + 1) * 64] for k in range(4)]),
        "pool_scale": dps[0],
    }
    return loss_blk[0, 0], dx, G, dmod, dsmall


SHARDED = ("w_ffn1_in", "w_ffn1_out", "w_in", "w_pool_branch", "w_attn_branch", "w_out", "w_ffn2_in", "w_ffn2_out")
TRANSPOSED = ("w_ffn1_in", "w_in", "w_ffn2_in")
FIRST = ("w_ffn1_in", "w_ffn1_out")
LATER = tuple(n for n in SHARDED if n not in FIRST)
GRAD_KEY = dict(w_ffn1_in="w1in", w_ffn1_out="w1out", w_in="win", w_pool_branch="wpb", w_attn_branch="wab",
                w_out="wout", w_ffn2_in="w2in", w_ffn2_out="w2out")


def _cols_to_full(g):
    return jnp.concatenate([g[j] for j in range(N_DEV)], axis=1)


def _full_to_cols(t):
    c = t.shape[1] // N_DEV
    return jnp.stack([t[:, j * c:(j + 1) * c] for j in range(N_DEV)])


SMALL = (("b_ada", 72), ("g_norm_ffn1", 8), ("g_norm_mix", 8), ("g_norm_ffn2", 8), ("g_final", 8),
         ("w_pool", 128), ("pool_scale", 8))


def _pack_small(vals, loss):
    rows = []
    for name, nrows in SMALL:
        t = vals[name].reshape(-1, 128)
        rows.append(jnp.pad(t, ((0, nrows - t.shape[0]), (0, 0))))
    rows.append(jnp.full((8, 128), loss, F32))
    return jnp.concatenate(rows)


def _unpack_small(slab, shapes):
    out, off = {}, 0
    for name, nrows in SMALL:
        used = 1
        for d in shapes[name]:
            used *= d
        out[name] = slab[off:off + used // 128].reshape(shapes[name])
        off += nrows
    return out, slab[off, 0]


def _as_2d(t):
    return t.reshape(-1, t.shape[-1])


def kernel(x, c, positions, w_ada, b_ada, g_norm_ffn1, w_ffn1_in, w_ffn1_out, g_norm_mix, w_in, w_pool, pool_scale, w_pool_branch, w_attn_branch, w_out, g_norm_ffn2, w_ffn2_in, w_ffn2_out, g_final, loss_target, m_w_ada, m_b_ada, m_g_norm_ffn1, m_w_ffn1_in, m_w_ffn1_out, m_g_norm_mix, m_w_in, m_w_pool, m_pool_scale, m_w_pool_branch, m_w_attn_branch, m_w_out, m_g_norm_ffn2, m_w_ffn2_in, m_w_ffn2_out, m_g_final, v_w_ada, v_b_ada, v_g_norm_ffn1, v_w_ffn1_in, v_w_ffn1_out, v_g_norm_mix, v_w_in, v_w_pool, v_pool_scale, v_w_pool_branch, v_w_attn_branch, v_w_out, v_g_norm_ffn2, v_w_ffn2_in, v_w_ffn2_out, v_g_final):
    names = ["w_ada", "b_ada", "g_norm_ffn1", "w_ffn1_in", "w_ffn1_out", "g_norm_mix", "w_in", "w_pool", "pool_scale",
             "w_pool_branch", "w_attn_branch", "w_out", "g_norm_ffn2", "w_ffn2_in", "w_ffn2_out", "g_final"]
    w = dict(w_ada=w_ada, b_ada=b_ada, g_norm_ffn1=g_norm_ffn1, w_ffn1_in=w_ffn1_in, w_ffn1_out=w_ffn1_out,
             g_norm_mix=g_norm_mix, w_in=w_in, w_pool=w_pool, pool_scale=pool_scale, w_pool_branch=w_pool_branch,
             w_attn_branch=w_attn_branch, w_out=w_out, g_norm_ffn2=g_norm_ffn2, w_ffn2_in=w_ffn2_in,
             w_ffn2_out=w_ffn2_out, g_final=g_final)
    m = dict(w_ada=m_w_ada, b_ada=m_b_ada, g_norm_ffn1=m_g_norm_ffn1, w_ffn1_in=m_w_ffn1_in, w_ffn1_out=m_w_ffn1_out,
             g_norm_mix=m_g_norm_mix, w_in=m_w_in, w_pool=m_w_pool, pool_scale=m_pool_scale,
             w_pool_branch=m_w_pool_branch, w_attn_branch=m_w_attn_branch, w_out=m_w_out, g_norm_ffn2=m_g_norm_ffn2,
             w_ffn2_in=m_w_ffn2_in, w_ffn2_out=m_w_ffn2_out, g_final=m_g_final)
    v = dict(w_ada=v_w_ada, b_ada=v_b_ada, g_norm_ffn1=v_g_norm_ffn1, w_ffn1_in=v_w_ffn1_in, w_ffn1_out=v_w_ffn1_out,
             g_norm_mix=v_g_norm_mix, w_in=v_w_in, w_pool=v_w_pool, pool_scale=v_pool_scale,
             w_pool_branch=v_w_pool_branch, w_attn_branch=v_w_attn_branch, w_out=v_w_out, g_norm_ffn2=v_g_norm_ffn2,
             w_ffn2_in=v_w_ffn2_in, w_ffn2_out=v_w_ffn2_out, g_final=v_g_final)
    shapes = {n: w[n].shape for n in names}
    me = 4 * lax.axis_index("x") + 2 * lax.axis_index("y") + lax.axis_index("c")
    D = x.shape[-1]
    n_mod = w_ada.shape[-1] * N_DEV // D

    (c_all,) = _all_gather([c.reshape(D // 128, 128)], "gather_c")
    ada_cols = w_ada.shape[-1]
    b_mine = lax.dynamic_slice_in_dim(b_ada, me * ada_cols, ada_cols, axis=1)
    cond, mod_part = _ada_mod(c_all.reshape(N_DEV, D), w_ada[0], b_mine, "ada_mod")
    (mod_all,) = _all_gather([mod_part.reshape(-1, 128)], "gather_mod")
    mod_all = mod_all.reshape(N_DEV, N_DEV, ada_cols)
    mod = lax.dynamic_index_in_dim(mod_all, me, axis=1, keepdims=False).reshape(n_mod, D)

    def local(t, name):
        return t[name][0].T if name in TRANSPOSED else t[name][0]

    shards = {name: local(w, name).astype(BF16) for name in SHARDED}

    def gather_done(names, fulls):
        return {name: lax.dynamic_update_index_in_dim(full, shards[name], me, axis=0)
                for name, full in zip(names, fulls)}

    def ffn_weights(g, pre):
        return {"w%sin" % pre: g["w_ffn%s_in" % pre].reshape(2, -1, D),
                "w%sout" % pre: g["w_ffn%s_out" % pre].reshape(-1, D)}

    def later_weights(fulls):
        g = gather_done(LATER, fulls)
        return dict(win=g["w_in"].reshape(-1, D), wpb=_cols_to_full(g["w_pool_branch"]),
                    wab=_cols_to_full(g["w_attn_branch"]), wout=g["w_out"].reshape(D, D), **ffn_weights(g, "2"))

    W = ffn_weights(gather_done(FIRST, _all_gather_tree([shards[n] for n in FIRST], "gather_ffn1")), "1")
    small = dict(g1=g_norm_ffn1[0], g2=g_norm_mix[0], g3=g_norm_ffn2[0], gf=g_final, w_pool=w_pool[0],
                 pool_scale=pool_scale[0])
    red = _GradReducer()
    loss_part, dx, G, dmod, dsmall = _local_step(
        x[0], loss_target[0], positions[0], mod, small, W,
        late=(_gather_rider([shards[n] for n in LATER]), later_weights), red=red)
    chip = (2 * lax.axis_index("x") + lax.axis_index("y")).astype(jnp.int32).reshape(1)

    part = _pack_small(dict(b_ada=dmod, g_norm_ffn1=dsmall["g1"], g_norm_mix=dsmall["g2"], g_norm_ffn2=dsmall["g3"],
                            g_final=dsmall["gf"], w_pool=dsmall["w_pool"], pool_scale=dsmall["pool_scale"]),
                       loss_part)
    (parts,) = _all_gather([part], "gather_small")
    gsmall, loss = _unpack_small(_sum_slots(parts, "sum_small"), shapes)
    rows_mine = ada_cols // 128
    dmod_mine = lax.dynamic_slice_in_dim(parts, me * rows_mine, rows_mine, axis=1).reshape(N_DEV, ada_cols)

    grads, delta, new_m, new_v = {}, {}, {}, {}
    grads["w_ada"], delta["w_ada"], new_m["w_ada"], new_v["w_ada"] = (
        t[None] for t in _ada_grad_adamw(cond.T, dmod_mine, w_ada[0], m_w_ada[0], v_w_ada[0], "ada_grad_adamw"))
    for name in SHARDED:
        key = GRAD_KEY[name]
        res = _sum_adamw(chip, red.own[key], red.others[key], local(w, name), local(m, name), local(v, name),
                         "adamw_" + name)
        grads[name], delta[name], new_m[name], new_v[name] = (
            (t.T if name in TRANSPOSED else t)[None] for t in res)
    small_names = [name for name, _ in SMALL]
    res = _adamw_many(*([_as_2d(t[name]) for name in small_names] for t in (w, gsmall, m, v)), "adamw_small")
    for dst, vals in zip((delta, new_m, new_v), res):
        dst.update({name: t.reshape(shapes[name]) for name, t in zip(small_names, vals)})
    grads.update(gsmall)

    return (loss, dx[None], *[grads[n] for n in names], *[delta[n] for n in names],
            *[new_m[n] for n in names], *[new_v[n] for n in names])
```

```python
import functools

import jax
import jax.numpy as jnp
from jax import lax
from jax.experimental import pallas as pl
from jax.experimental.pallas import tpu as pltpu

F32 = jnp.float32
BF16 = jnp.bfloat16
MESH = pl.DeviceIdType.MESH
ANY = pl.BlockSpec(memory_space=pl.ANY)

N_DEV = 8
EPS = 1e-6
HEAD_DIM = 64
HEADS = 4
GW = HEADS * HEAD_DIM
DILATIONS = (1, 4, 16)
BAND = 128
QB = 128
POOL_WINDOWS = (2, 4, 8, 16)
HALO = 16
ROPE_THETA = 10000.0

ADAM_LR = 0.001
ADAM_B1 = 0.9
ADAM_B2 = 0.999
ADAM_EPS = 1e-08
ADAM_WD = 0.01
ADAM_STEP = 10

VMEM_LIMIT = 56 * 1024 * 1024
TS = 512
FFN_TS = 256
FFN_CHUNKS = (2816,)

NT = (((1,), (1,)), ((), ()))
TN = (((0,), (0,)), ((), ()))


def _params(**kw):
    return pltpu.CompilerParams(vmem_limit_bytes=VMEM_LIMIT, **kw)


def _dot(a, b):
    return jnp.dot(a, b, preferred_element_type=F32)


def _dot_nt(a, b):
    return lax.dot_general(a, b, NT, preferred_element_type=F32)


def _dot_tn(a, b):
    return lax.dot_general(a, b, TN, preferred_element_type=F32)


def _load_weights(pairs, sem):
    @pl.when(pl.program_id(0) == 0)
    def _():
        copies = [pltpu.make_async_copy(src, dst, sem.at[i]) for i, (src, dst) in enumerate(pairs)]
        for cp in copies:
            cp.start()
        for cp in copies:
            cp.wait()


def _norm_mod(x, g, sc, sh):
    r = lax.rsqrt(jnp.mean(x * x, axis=-1, keepdims=True) + EPS)
    xn = x * r
    y = xn * g
    return r, xn, y, y * (1.0 + sc) + sh


def _norm_mod_bwd(du, r, xn, y, g, sc):
    dsh = jnp.sum(du, axis=0, keepdims=True)
    dsc = jnp.sum(du * y, axis=0, keepdims=True)
    dy = du * (1.0 + sc)
    dg = jnp.sum(dy * xn, axis=0, keepdims=True)
    dxn = dy * g
    dx = r * (dxn - xn * jnp.mean(dxn * xn, axis=-1, keepdims=True))
    return dx, dsh, dsc, dg


def _row_tile(ts, width):
    return pl.BlockSpec((ts, width), lambda i: (i, 0))


def _const(shape):
    return pl.BlockSpec(shape, lambda *_: (0,) * len(shape))


def _ffn_chunks(Fd):
    assert sum(FFN_CHUNKS) == Fd
    edges = [sum(FFN_CHUNKS[:k]) for k in range(len(FFN_CHUNKS) + 1)]
    return [slice(a, b) for a, b in zip(edges[:-1], edges[1:])]
def _final_tile(x, g, target):
    r = lax.rsqrt(jnp.mean(x * x, axis=-1, keepdims=True) + EPS)
    xn = x * r
    err = xn * g - target
    loss = 0.5 * jnp.sum(jnp.mean(err * err, axis=-1, keepdims=True))
    dy = err * (1.0 / x.shape[-1])
    dg = jnp.sum(dy * xn, axis=0, keepdims=True)
    dxn = dy * g
    return r * (dxn - xn * jnp.mean(dxn * xn, axis=-1, keepdims=True)), loss, dg


def _ffn_fwd(h, vec, win, wout, name, rider=None, final=None):
    TS = FFN_TS
    S, D = h.shape
    _, Fd, _ = win.shape
    n_in, n_out = (6, 7) if final else (4, 5)

    def body(*refs):
        if rider is None:
            return compute(*refs)
        host, mine = rider.split(refs, n_in, n_out)
        rider.head(mine, pl.program_id(0))
        compute(*host)
        rider.tail(mine, pl.program_id(0), S // TS)

    def compute(*refs):
        h_ref, vec_ref, win_hbm, wout_hbm = refs[:4]
        hn_ref, u_ref, ab_ref, act_ref, f_ref = refs[n_in:n_in + 5]
        win_v, wout_v, sem = refs[n_in + n_out:]
        _load_weights([(win_hbm, win_v), (wout_hbm, wout_v)], sem)
        x = h_ref[...]
        g, sh, sc, gt = (vec_ref[k:k + 1, :] for k in range(4))
        _, _, _, u = _norm_mod(x, g, sc, sh)
        ub = u.astype(BF16)
        u_ref[...] = ub
        acc = jnp.zeros((TS, D), F32)
        for sl in _ffn_chunks(Fd):
            a = _dot_nt(ub, win_v[0, sl, :])
            b = _dot_nt(ub, win_v[1, sl, :])
            act = ((a * jax.nn.sigmoid(a)) * b).astype(BF16)
            ab_ref[0, :, sl] = a.astype(BF16)
            ab_ref[1, :, sl] = b.astype(BF16)
            act_ref[:, sl] = act
            acc = acc + _dot(act, wout_v[sl, :])
        f_ref[...] = acc.astype(BF16)
        hn = x + (0.5 * gt) * acc
        if not final:
            hn_ref[...] = hn
            return
        t_ref, gf_ref = refs[4:6]
        loss_ref, dgf_ref = refs[n_in + 5:n_in + 7]

        @pl.when(pl.program_id(0) == 0)
        def _():
            loss_ref[...] = jnp.zeros_like(loss_ref)
            dgf_ref[...] = jnp.zeros_like(dgf_ref)

        hn_ref[...], loss, dg = _final_tile(hn, gf_ref[0:1, :], t_ref[...])
        loss_ref[...] += loss
        dgf_ref[0:1, :] += dg

    specs = (
        [_row_tile(TS, D), _const((8, D)), ANY, ANY] + ([_row_tile(TS, D), _const((8, D))] if final else []),
        [_row_tile(TS, D), _row_tile(TS, D), pl.BlockSpec((2, TS, Fd), lambda i: (0, i, 0)),
         _row_tile(TS, Fd), _row_tile(TS, D)] + ([_const((8, 128)), _const((8, D))] if final else []),
        [jax.ShapeDtypeStruct((S, D), F32), jax.ShapeDtypeStruct((S, D), BF16),
         jax.ShapeDtypeStruct((2, S, Fd), BF16), jax.ShapeDtypeStruct((S, Fd), BF16),
         jax.ShapeDtypeStruct((S, D), BF16)]
        + ([jax.ShapeDtypeStruct((8, 128), F32), jax.ShapeDtypeStruct((8, D), F32)] if final else []),
        [pltpu.VMEM(win.shape, BF16), pltpu.VMEM(wout.shape, BF16), pltpu.SemaphoreType.DMA((2,))])
    in_specs, out_specs, out_shape, scratch = specs if rider is None else rider.specs(*specs)
    outs = pl.pallas_call(
        body, name=name, grid=(S // TS,), in_specs=in_specs, out_specs=out_specs, out_shape=out_shape,
        scratch_shapes=scratch, compiler_params=_params(),
    )(h, vec, win, wout, *(final or ()), *(rider.arrays if rider else []))
    return outs if rider is None else (outs[:n_out], outs[n_out:])


def _ffn_bwd(dh, h, f, ab, vec, win, wout, name):
    TS = FFN_TS
    S, D = h.shape
    _, Fd, _ = win.shape

    def body(dh_ref, h_ref, f_ref, ab_ref, vec_ref, win_hbm, wout_hbm,
             dhp_ref, dab_ref, df_ref, red_ref, win_v, wout_v, sem):
        _load_weights([(win_hbm, win_v), (wout_hbm, wout_v)], sem)

        @pl.when(pl.program_id(0) == 0)
        def _():
            red_ref[...] = jnp.zeros_like(red_ref)

        dh_v = dh_ref[...]
        x = h_ref[...]
        g, sh, sc, gt = (vec_ref[k:k + 1, :] for k in range(4))
        dgt = jnp.sum((0.5 * f_ref[...].astype(F32)) * dh_v, axis=0, keepdims=True)
        dfb = ((0.5 * gt) * dh_v).astype(BF16)
        df_ref[...] = dfb
        du = jnp.zeros((TS, D), F32)
        for sl in _ffn_chunks(Fd):
            dact = _dot_nt(dfb, wout_v[sl, :])
            av = ab_ref[0, :, sl].astype(F32)
            bv = ab_ref[1, :, sl].astype(F32)
            sg = jax.nn.sigmoid(av)
            da = (dact * bv * (sg * (1.0 + av * (1.0 - sg)))).astype(BF16)
            db = (dact * (av * sg)).astype(BF16)
            dab_ref[0, :, sl] = da
            dab_ref[1, :, sl] = db
            du = du + _dot(da, win_v[0, sl, :]) + _dot(db, win_v[1, sl, :])
        r, xn, y, _ = _norm_mod(x, g, sc, sh)
        dx, dsh, dsc, dg = _norm_mod_bwd(du, r, xn, y, g, sc)
        dhp_ref[...] = dh_v + dx
        red_ref[0:1, :] += dsh
        red_ref[1:2, :] += dsc
        red_ref[2:3, :] += dgt
        red_ref[3:4, :] += dg

    ab_spec = pl.BlockSpec((2, TS, Fd), lambda i: (0, i, 0))
    return pl.pallas_call(
        body, name=name, grid=(S // TS,),
        in_specs=[_row_tile(TS, D), _row_tile(TS, D), _row_tile(TS, D), ab_spec, _const((8, D)), ANY, ANY],
        out_specs=[_row_tile(TS, D), ab_spec, _row_tile(TS, D), _const((8, D))],
        out_shape=[jax.ShapeDtypeStruct((S, D), F32), jax.ShapeDtypeStruct((2, S, Fd), BF16),
                   jax.ShapeDtypeStruct((S, D), BF16), jax.ShapeDtypeStruct((8, D), F32)],
        scratch_shapes=[pltpu.VMEM(win.shape, BF16), pltpu.VMEM(wout.shape, BF16), pltpu.SemaphoreType.DMA((2,))],
        compiler_params=_params(),
    )(dh, h, f, ab, vec, win, wout)


def _wgrad(x, y, name, tm=None, ts=2048, rider=None):
    xb = x.ndim == 3
    nb = x.shape[0] if xb else 0
    S, M = x.shape[-2:]
    N = y.shape[-1]
    tm = tm or M
    ts = min(ts, S)
    nk = S // ts
    grid = (max(nb, 1), M // tm, nk)

    def body(*refs):
        if rider is None:
            return compute(*refs)
        host, mine = rider.split(refs, 2, 1)
        step = (pl.program_id(0) * grid[1] + pl.program_id(1)) * grid[2] + pl.program_id(2)
        rider.head(mine, step)
        compute(*host)
        rider.tail(mine, step, grid[0] * grid[1] * grid[2])

    def compute(x_ref, y_ref, o_ref, acc):
        k = pl.program_id(2)

        @pl.when(k == 0)
        def _():
            acc[...] = jnp.zeros_like(acc)

        acc[...] += _dot_tn(x_ref[...], y_ref[...])

        @pl.when(k == nk - 1)
        def _():
            o_ref[...] = acc[...].astype(BF16)

    x_spec = (pl.BlockSpec((None, ts, tm), lambda b, i, k: (b, k, i)) if xb
              else pl.BlockSpec((ts, tm), lambda b, i, k: (k, i)))
    y_spec = pl.BlockSpec((ts, N), lambda b, i, k: (k, 0))
    if xb:
        o_spec, o_shape = pl.BlockSpec((None, tm, N), lambda b, i, k: (b, i, 0)), (nb, M, N)
    else:
        o_spec, o_shape = pl.BlockSpec((tm, N), lambda b, i, k: (i, 0)), (M, N)
    specs = ([x_spec, y_spec], [o_spec], [jax.ShapeDtypeStruct(o_shape, BF16)], [pltpu.VMEM((tm, N), F32)])
    in_specs, out_specs, out_shape, scratch = specs if rider is None else rider.specs(*specs)
    outs = pl.pallas_call(
        body, name=name, grid=grid, in_specs=in_specs, out_specs=out_specs, out_shape=out_shape,
        scratch_shapes=scratch, compiler_params=_params(),
    )(x, y, *(rider.arrays if rider else []))
    return outs[0] if rider is None else (outs[0], outs[1:])


P_OFF, Q_OFF, K_OFF, V_OFF, G_OFF = 0, 256, 1024, 1792, 2560
IN_WIDTH = 4608


def _first_half_mask(ts):
    lane = lax.broadcasted_iota(jnp.int32, (ts, 128), 1)
    return (lane % HEAD_DIM) < (HEAD_DIM // 2)


def _rope(t, cos, sin_signed, first, sign):
    partner = jnp.where(first, pltpu.roll(t, 96, 1), pltpu.roll(t, 32, 1))
    return t * cos + sign * (partner * sin_signed)


def _res_spec(r):
    return pl.BlockSpec((r, TS // r, GW), lambda i: (0, i, 0))


def _res_shape(S, r, dtype):
    return jax.ShapeDtypeStruct((r, S // r, GW), dtype)


def _to_residues(piece, out_ref, lanes, r, scr):
    if r == 1:
        out_ref[0, :, lanes] = piece.astype(out_ref.dtype)
        return
    for h in range(piece.shape[1] // 128):
        scr[h] = piece[:, h * 128:(h + 1) * 128]
        at = slice(lanes.start + h * 128, lanes.start + (h + 1) * 128)
        for res in range(r):
            out_ref[res, :, at] = scr[h, pl.ds(res, TS // r, stride=r), :].astype(out_ref.dtype)


def _from_residues(in_ref, lanes, r, scr):
    if r == 1:
        return in_ref[0, :, lanes].astype(F32)
    halves = (lanes.stop - lanes.start) // 128
    for h in range(halves):
        at = slice(lanes.start + h * 128, lanes.start + (h + 1) * 128)
        for res in range(r):
            scr[h, pl.ds(res, TS // r, stride=r), :] = in_ref[res, :, at].astype(F32)
    return scr[0] if halves == 1 else jnp.concatenate([scr[0], scr[1]], axis=1)


RES_SCRATCH = (2, TS, 128)


def _mix_in_fwd(h, vec, cos, sin, win, name):
    S, D = h.shape

    def body(h_ref, vec_ref, cos_ref, sin_ref, win_hbm, u_ref, p_ref, gates_ref, *rest):
        qkv_refs, (win_v, sem, scr) = rest[:9], rest[9:]
        _load_weights([(win_hbm, win_v)], sem)
        g, sh, sc = (vec_ref[k:k + 1, :] for k in range(3))
        _, _, _, u = _norm_mod(h_ref[...], g, sc, sh)
        ub = u.astype(BF16)
        u_ref[...] = ub
        p_ref[...] = _dot_nt(ub, win_v[P_OFF:Q_OFF, :])
        cosv, sinv = cos_ref[...], sin_ref[...]
        first = _first_half_mask(TS)
        for which, off in enumerate((Q_OFF, K_OFF, V_OFF)):
            t = _dot_nt(ub, win_v[off:off + 3 * GW, :])
            for gi in range(3):
                for half in range(2):
                    c0 = gi * GW + half * 128
                    piece = t[:, c0:c0 + 128]
                    if which < 2:
                        piece = _rope(piece, cosv, sinv, first, 1.0)
                    _to_residues(piece, qkv_refs[which * 3 + gi], slice(half * 128, (half + 1) * 128),
                                 DILATIONS[gi], scr)
        gates_ref[...] = jax.nn.sigmoid(_dot_nt(ub, win_v[G_OFF:IN_WIDTH, :])).astype(BF16)

    return pl.pallas_call(
        body, name=name, grid=(S // TS,),
        in_specs=[_row_tile(TS, D), _const((8, D)), _row_tile(TS, 128), _row_tile(TS, 128), ANY],
        out_specs=[_row_tile(TS, D), _row_tile(TS, GW), _row_tile(TS, 2 * D)] + [_res_spec(r) for r in DILATIONS] * 3,
        out_shape=[jax.ShapeDtypeStruct((S, D), BF16), jax.ShapeDtypeStruct((S, GW), F32),
                   jax.ShapeDtypeStruct((S, 2 * D), BF16)] + [_res_shape(S, r, BF16) for r in DILATIONS] * 3,
        scratch_shapes=[pltpu.VMEM((IN_WIDTH, D), BF16), pltpu.SemaphoreType.DMA((1,)), pltpu.VMEM(RES_SCRATCH, F32)],
        compiler_params=_params(),
    )(h, vec, cos, sin, win)


def _mix_in_bwd(dh, h, vec, cos, sin, dp, dqkv, dgl, win, name):
    S, D = h.shape

    def body(dh_ref, h_ref, vec_ref, cos_ref, sin_ref, dp_ref, *rest):
        dqkv_refs = rest[:9]
        dgl_ref, win_hbm, dhp_ref, dproj_ref, red_ref, win_v, sem, scr = rest[9:]
        _load_weights([(win_hbm, win_v)], sem)

        @pl.when(pl.program_id(0) == 0)
        def _():
            red_ref[...] = jnp.zeros_like(red_ref)

        cosv, sinv = cos_ref[...], sin_ref[...]
        first = _first_half_mask(TS)
        dproj_ref[:, P_OFF:Q_OFF] = dp_ref[...].astype(BF16)
        for which, off in enumerate((Q_OFF, K_OFF, V_OFF)):
            for gi in range(3):
                for half in range(2):
                    piece = _from_residues(dqkv_refs[which * 3 + gi], slice(half * 128, (half + 1) * 128),
                                           DILATIONS[gi], scr)
                    if which < 2:
                        piece = _rope(piece, cosv, sinv, first, -1.0)
                    c0 = off + gi * GW + half * 128
                    dproj_ref[:, c0:c0 + 128] = piece.astype(BF16)
        dproj_ref[:, G_OFF:IN_WIDTH] = dgl_ref[...]
        du = _dot(dproj_ref[...], win_v[...])
        g, sh, sc = (vec_ref[k:k + 1, :] for k in range(3))
        r, xn, y, _ = _norm_mod(h_ref[...], g, sc, sh)
        dx, dsh, dsc, dg = _norm_mod_bwd(du, r, xn, y, g, sc)
        dhp_ref[...] = dh_ref[...] + dx
        red_ref[0:1, :] += dsh
        red_ref[1:2, :] += dsc
        red_ref[3:4, :] += dg

    return pl.pallas_call(
        body, name=name, grid=(S // TS,),
        in_specs=[_row_tile(TS, D), _row_tile(TS, D), _const((8, D)), _row_tile(TS, 128), _row_tile(TS, 128),
                  _row_tile(TS, GW)] + [_res_spec(r) for r in DILATIONS] * 3 + [_row_tile(TS, 2 * D), ANY],
        out_specs=[_row_tile(TS, D), _row_tile(TS, IN_WIDTH), _const((8, D))],
        out_shape=[jax.ShapeDtypeStruct((S, D), F32), jax.ShapeDtypeStruct((S, IN_WIDTH), BF16),
                   jax.ShapeDtypeStruct((8, D), F32)],
        scratch_shapes=[pltpu.VMEM((IN_WIDTH, D), BF16), pltpu.SemaphoreType.DMA((1,)), pltpu.VMEM(RES_SCRATCH, F32)],
        compiler_params=_params(),
    )(dh, h, vec, cos, sin, dp, *dqkv, dgl, win)


def _pool_lanes(rows):
    lane = lax.broadcasted_iota(jnp.int32, (rows, GW), 1)
    return lane // HEAD_DIM


def _pool_window(rows):
    grp = _pool_lanes(rows)
    w = jnp.full((rows, GW), POOL_WINDOWS[0], jnp.int32)
    for k in range(1, len(POOL_WINDOWS)):
        w = jnp.where(grp == k, POOL_WINDOWS[k], w)
    return grp, w


def _pool_fwd(p, wbd, scale, name, ts=512):
    S = p.shape[0]
    ext = ts + HALO

    def body(pc_ref, ph_ref, wbd_ref, sc_ref, d_ref, y_ref):
        i = pl.program_id(0)
        cur = pc_ref[...]
        halo = jnp.where(i > 0, ph_ref[...], 0.0)
        s = jnp.concatenate([halo, cur], axis=0)
        grp, w = _pool_window(ext)
        sel = jnp.zeros((ext, GW), F32)
        for k, wk in enumerate(POOL_WINDOWS):
            s = s + pltpu.roll(s, wk // 2, 0)
            sel = jnp.where(grp == k, s, sel)
        t = i * ts + lax.broadcasted_iota(jnp.int32, (ts, GW), 0)
        count = jnp.minimum(t + 1, w[HALO:]).astype(F32)
        d = (sel[HALO:] / count - cur).astype(BF16)
        d_ref[...] = d
        y_ref[...] = (_dot(d, wbd_ref[...]) * sc_ref[...]).astype(BF16)

    return pl.pallas_call(
        body, name=name, grid=(S // ts,),
        in_specs=[_row_tile(ts, GW),
                  pl.BlockSpec((HALO, GW), lambda i: (jnp.maximum(i * (ts // HALO) - 1, 0), 0)),
                  _const((GW, GW)), _const((1, GW))],
        out_specs=[_row_tile(ts, GW), _row_tile(ts, GW)],
        out_shape=[jax.ShapeDtypeStruct((S, GW), BF16), jax.ShapeDtypeStruct((S, GW), BF16)],
        compiler_params=_params(),
    )(p, p, wbd, scale)


def _pool_bwd(dy, d, wbd, scale, name, ts=512):
    S = dy.shape[0]
    ext = ts + HALO
    nsteps = S // ts
    last_halo = S // HALO - 1

    def body(dyc_ref, dyh_ref, d_ref, wbd_ref, sc_ref, dp_ref, dw_ref, ds_ref):
        i = pl.program_id(0)

        @pl.when(i == 0)
        def _():
            dw_ref[...] = jnp.zeros_like(dw_ref)
            ds_ref[...] = jnp.zeros_like(ds_ref)

        dyc = dyc_ref[...]
        dyh = jnp.where(i < nsteps - 1, dyh_ref[...], 0.0)
        dys = (jnp.concatenate([dyc, dyh], axis=0) * sc_ref[...]).astype(BF16)
        dd = _dot_nt(dys, wbd_ref[...])
        grp, w = _pool_window(ext)
        t = i * ts + lax.broadcasted_iota(jnp.int32, (ext, GW), 0)
        s = dd / jnp.minimum(t + 1, w).astype(F32)
        sel = jnp.zeros((ext, GW), F32)
        for k, wk in enumerate(POOL_WINDOWS):
            s = s + pltpu.roll(s, ext - wk // 2, 0)
            sel = jnp.where(grp == k, s, sel)
        dp_ref[...] = sel[:ts] - dd[:ts]
        dv = d_ref[...]
        z = _dot(dv, wbd_ref[...])
        ds_ref[0:1, :] += jnp.sum(dyc * z, axis=0, keepdims=True)
        dw_ref[...] += _dot_tn(dv, dys[:ts])

    return pl.pallas_call(
        body, name=name, grid=(nsteps,),
        in_specs=[_row_tile(ts, GW),
                  pl.BlockSpec((HALO, GW), lambda i: (jnp.minimum((i + 1) * (ts // HALO), last_halo), 0)),
                  _row_tile(ts, GW), _const((GW, GW)), _const((1, GW))],
        out_specs=[_row_tile(ts, GW), _const((GW, GW)), _const((8, GW))],
        out_shape=[jax.ShapeDtypeStruct((S, GW), F32), jax.ShapeDtypeStruct((GW, GW), F32),
                   jax.ShapeDtypeStruct((8, GW), F32)],
        compiler_params=_params(),
    )(dy, dy, d, wbd, scale)


def _head_id(rows):
    return lax.broadcasted_iota(jnp.int32, (rows, GW), 1) // HEAD_DIM


def _stack_heads(t, hid):
    return jnp.concatenate([jnp.where(hid == h, t, jnp.zeros_like(t)) for h in range(HEADS)], axis=0)


def _unstack_heads(t_all, hid):
    out = jnp.zeros((QB, GW), F32)
    for h in range(HEADS):
        out = jnp.where(hid == h, t_all[h * QB:(h + 1) * QB], out)
    return out


def _band_mask(n):
    row = lax.broadcasted_iota(jnp.int32, (HEADS * QB, 2 * QB), 0) % QB
    col = lax.broadcasted_iota(jnp.int32, (HEADS * QB, 2 * QB), 1)
    rel = row + QB - col
    return (rel >= 0) & (rel <= BAND) & ((col >= QB) | (n > 0))


MAX_STREAMS = 8


def _streams(r, nb):
    if r > 1:
        ns = min(r, MAX_STREAMS)
        return nb, [(lambda rb, l=l: ns * rb + l, 0) for l in range(ns)]
    ns = min(MAX_STREAMS, nb)
    return nb // ns, [(lambda rb: 0, l * (nb // ns)) for l in range(ns)]


def _attn_fwd(q, k, v, name):
    r, L, _ = q.shape
    nbs, streams = _streams(r, L // QB)
    ns = len(streams)
    grid = (max(r // ns, 1), nbs)

    def cur(res, off):
        return pl.BlockSpec((None, QB, GW), lambda rb, n: (res(rb), n + off, 0))

    def prev(res, off):
        return pl.BlockSpec((None, QB, GW), lambda rb, n: (res(rb), jnp.maximum(n + off - 1, 0), 0))

    def body(*refs):
        n = pl.program_id(1)
        hid = _head_id(QB)
        o_ref, lse_ref = refs[5 * len(streams):]
        for l, (_, off) in enumerate(streams):
            q_ref, kp_ref, kc_ref, vp_ref, vc_ref = refs[5 * l:5 * l + 5]
            qs = _stack_heads(q_ref[...], hid)
            kc = jnp.concatenate([kp_ref[...], kc_ref[...]], axis=0)
            vc = jnp.concatenate([vp_ref[...], vc_ref[...]], axis=0)
            s = _dot_nt(qs, kc) * (HEAD_DIM ** -0.5)
            s = jnp.where(_band_mask(n + off), s, -jnp.inf)
            m = jnp.max(s, axis=-1, keepdims=True)
            e = jnp.exp(s - m)
            den = jnp.sum(e, axis=-1, keepdims=True)
            lse = m + jnp.log(den)
            pr = (e * (1.0 / den)).astype(BF16)
            o_ref[l] = _unstack_heads(_dot(pr, vc), hid).astype(BF16)
            lse_ref[l] = _unstack_heads(jnp.broadcast_to(lse, (HEADS * QB, GW)), hid)

    in_specs, args = [], []
    for res, off in streams:
        in_specs += [cur(res, off), prev(res, off), cur(res, off), prev(res, off), cur(res, off)]
        args += [q, k, k, v, v]
    out = jax.ShapeDtypeStruct((ns * grid[0], nbs * QB, GW), F32)
    both = pl.BlockSpec((ns, QB, GW), lambda rb, n: (rb, n, 0))
    o, lse = pl.pallas_call(
        body, name=name, grid=grid, in_specs=in_specs, out_specs=[both, both],
        out_shape=[jax.ShapeDtypeStruct(out.shape, BF16), out],
        compiler_params=_params(),
    )(*args)
    return o.reshape(q.shape), lse.reshape(q.shape)


def _head_rows(t_full, hid):
    return jnp.concatenate(
        [jnp.max(jnp.where(hid == h, t_full, -jnp.inf), axis=-1, keepdims=True) for h in range(HEADS)], axis=0)


def _attn_bwd(q, k, v, do, lse, cterm, name):
    r, L, _ = q.shape
    nbs, streams = _streams(r, L // QB)
    ns = len(streams)
    parts = r == 1

    def spec(res, index):
        return pl.BlockSpec((None, QB, GW), lambda rb, n: (res(rb), index(n), 0))

    def body(*refs):
        dq_ref, dk_ref, dv_ref, carry_k, carry_v, seam_k, seam_v = refs[8 * ns:]
        n = pl.program_id(1)

        @pl.when(n == 0)
        def _():
            carry_k[...] = jnp.zeros_like(carry_k)
            carry_v[...] = jnp.zeros_like(carry_v)

        @pl.when(n < nbs)
        def _():
            hid = _head_id(QB)
            for l, (_, off) in enumerate(streams):
                q_ref, do_ref, lse_ref, c_ref, kp_ref, kc_ref, vp_ref, vc_ref = refs[8 * l:8 * l + 8]
                qs = _stack_heads(q_ref[...], hid)
                dos = _stack_heads(do_ref[...], hid)
                kc = jnp.concatenate([kp_ref[...], kc_ref[...]], axis=0)
                vc = jnp.concatenate([vp_ref[...], vc_ref[...]], axis=0)
                s = _dot_nt(qs, kc) * (HEAD_DIM ** -0.5)
                s = jnp.where(_band_mask(n + off), s, -jnp.inf)
                p = jnp.exp(s - _head_rows(lse_ref[...], hid))
                dp = _dot_nt(dos, vc)
                ds = (p * (dp + _head_rows(c_ref[...], hid)) * (HEAD_DIM ** -0.5)).astype(BF16)
                dq_ref[l] = _unstack_heads(_dot(ds, kc), hid).astype(BF16)
                dkc = _dot_tn(ds, qs)
                dvc = _dot_tn(p.astype(BF16), dos)
                if parts and l > 0:
                    @pl.when(n == 0)
                    def _():
                        seam_k[l] = dkc[:QB]
                        seam_v[l] = dvc[:QB]
                dk_ref[l] = (carry_k[l] + dkc[:QB]).astype(BF16)
                dv_ref[l] = (carry_v[l] + dvc[:QB]).astype(BF16)
                carry_k[l] = dkc[QB:]
                carry_v[l] = dvc[QB:]

        @pl.when(n == nbs)
        def _():
            for l in range(ns):
                if parts and l + 1 < ns:
                    dk_ref[l] = (carry_k[l] + seam_k[l + 1]).astype(BF16)
                    dv_ref[l] = (carry_v[l] + seam_v[l + 1]).astype(BF16)
                else:
                    dk_ref[l] = carry_k[l].astype(BF16)
                    dv_ref[l] = carry_v[l].astype(BF16)

    in_specs, args = [], []
    for res, off in streams:
        qside = functools.partial(lambda n, off: jnp.minimum(n, nbs - 1) + off, off=off)
        kprev = functools.partial(lambda n, off: jnp.maximum(jnp.minimum(n, nbs) - 1 + off, 0), off=off)
        in_specs += [spec(res, qside)] * 4 + [spec(res, kprev), spec(res, qside)] * 2
        args += [q, do, lse, cterm, k, k, v, v]
    out = jax.ShapeDtypeStruct((ns * max(r // ns, 1), nbs * QB, GW), BF16)
    qout = pl.BlockSpec((ns, QB, GW), lambda rb, n: (rb, jnp.minimum(n, nbs - 1), 0))
    kout = pl.BlockSpec((ns, QB, GW), lambda rb, n: (rb, jnp.maximum(n - 1, 0), 0))
    buf = pltpu.VMEM((ns, QB, GW), F32)
    outs = pl.pallas_call(
        body, name=name, grid=(max(r // ns, 1), nbs + 1),
        in_specs=in_specs, out_specs=[qout, kout, kout], out_shape=[out, out, out],
        scratch_shapes=[buf, buf, buf, buf],
        compiler_params=_params(),
    )(*args)
    return [t.reshape(q.shape) for t in outs]


def _token_order(refs, scr):
    return [_from_residues(ref, slice(0, GW), r, scr) for ref, r in zip(refs, DILATIONS)]


def _group_weights(lses):
    l0, l1, l2 = lses
    m = jnp.maximum(jnp.maximum(l0, l1), l2)
    e = [jnp.exp(l - m) for l in (l0, l1, l2)]
    den = e[0] + e[1] + e[2]
    return [ei / den for ei in e]


def _mix_out_fwd(h, vec, gates, ypool, o3, lse3, wpb, wab, wout, name):
    S, D = h.shape

    def body(h_ref, vec_ref, gates_ref, yp_ref, o0, o1, o2, l0, l1, l2, wpb_hbm, wab_hbm, wout_hbm,
             hn_ref, ya_ref, merged_ref, tm_ref, wpb_v, wab_v, wout_v, sem, scr):
        _load_weights([(wpb_hbm, wpb_v), (wab_hbm, wab_v), (wout_hbm, wout_v)], sem)
        gt = vec_ref[3:4, :]
        wts = _group_weights(_token_order((l0, l1, l2), scr))
        og = _token_order((o0, o1, o2), scr)
        ya = (wts[0] * og[0] + wts[1] * og[1] + wts[2] * og[2]).astype(BF16)
        ya_ref[...] = ya
        merged = (gates_ref[:, :D].astype(F32) * _dot(yp_ref[...], wpb_v[...])
                  + gates_ref[:, D:].astype(F32) * _dot(ya, wab_v[...])).astype(BF16)
        merged_ref[...] = merged
        tm = _dot(merged, wout_v[...])
        tm_ref[...] = tm.astype(BF16)
        hn_ref[...] = h_ref[...] + gt * tm

    grp = _row_tile(TS, GW)
    res = [_res_spec(r) for r in DILATIONS]
    return pl.pallas_call(
        body, name=name, grid=(S // TS,),
        in_specs=[_row_tile(TS, D), _const((8, D)), _row_tile(TS, 2 * D), grp] + res * 2 + [ANY, ANY, ANY],
        out_specs=[_row_tile(TS, D), grp, _row_tile(TS, D), _row_tile(TS, D)],
        out_shape=[jax.ShapeDtypeStruct((S, D), F32), jax.ShapeDtypeStruct((S, GW), BF16),
                   jax.ShapeDtypeStruct((S, D), BF16), jax.ShapeDtypeStruct((S, D), BF16)],
        scratch_shapes=[pltpu.VMEM((GW, D), BF16), pltpu.VMEM((GW, D), BF16), pltpu.VMEM((D, D), BF16),
                        pltpu.SemaphoreType.DMA((3,)), pltpu.VMEM(RES_SCRATCH, F32)],
        compiler_params=_params(),
    )(h, vec, gates, ypool, *o3, *lse3, wpb, wab, wout)


def _mix_out_bwd(dh, tm, vec, gates, ypool, o3, lse3, wpb, wab, wout, name, rider=None):
    S, D = dh.shape

    def body(*refs):
        if rider is None:
            return compute(*refs)
        host, mine = rider.split(refs, 14, 12)
        rider.head(mine, pl.program_id(0))
        compute(*host)
        rider.tail(mine, pl.program_id(0), S // TS)

    def compute(dh_ref, tm_ref, vec_ref, gates_ref, yp_ref, o0, o1, o2, l0, l1, l2, wpb_hbm, wab_hbm, wout_hbm,
                dtm_ref, dgl_ref, dypb_ref, dyab_ref, dyp_ref, do0, do1, do2, c0, c1, c2, red_ref,
                wpb_v, wab_v, wout_v, sem, scr):
        _load_weights([(wpb_hbm, wpb_v), (wab_hbm, wab_v), (wout_hbm, wout_v)], sem)

        @pl.when(pl.program_id(0) == 0)
        def _():
            red_ref[...] = jnp.zeros_like(red_ref)

        gt = vec_ref[3:4, :]
        dh_v = dh_ref[...]
        red_ref[2:3, :] += jnp.sum(tm_ref[...].astype(F32) * dh_v, axis=0, keepdims=True)
        dtm = (gt * dh_v).astype(BF16)
        dtm_ref[...] = dtm
        dm = _dot_nt(dtm, wout_v[...])
        wts = _group_weights(_token_order((l0, l1, l2), scr))
        og = _token_order((o0, o1, o2), scr)
        ya = wts[0] * og[0] + wts[1] * og[1] + wts[2] * og[2]
        ypb = _dot(yp_ref[...], wpb_v[...])
        yab = _dot(ya.astype(BF16), wab_v[...])
        gp = gates_ref[:, :D].astype(F32)
        ga = gates_ref[:, D:].astype(F32)
        dgl_ref[:, :D] = (dm * ypb * gp * (1.0 - gp)).astype(BF16)
        dgl_ref[:, D:] = (dm * yab * ga * (1.0 - ga)).astype(BF16)
        dypb = (dm * gp).astype(BF16)
        dyab = (dm * ga).astype(BF16)
        dypb_ref[...] = dypb
        dyab_ref[...] = dyab
        dyp_ref[...] = _dot_nt(dypb, wpb_v[...])
        dya = _dot_nt(dyab, wab_v[...])
        row = lax.broadcasted_iota(jnp.int32, (GW, GW), 0) // HEAD_DIM
        col = lax.broadcasted_iota(jnp.int32, (GW, GW), 1) // HEAD_DIM
        ones = jnp.where(row == col, 1.0, 0.0).astype(F32)
        tot = jnp.dot(dya * ya, ones, preferred_element_type=F32, precision=lax.Precision.HIGHEST)
        for wg, do_ref, c_ref, r in zip(wts, (do0, do1, do2), (c0, c1, c2), DILATIONS):
            _to_residues(wg * dya, do_ref, slice(0, GW), r, scr)
            _to_residues(-(wg * tot), c_ref, slice(0, GW), r, scr)

    grp = _row_tile(TS, GW)
    res = [_res_spec(r) for r in DILATIONS]
    specs = (
        [_row_tile(TS, D), _row_tile(TS, D), _const((8, D)), _row_tile(TS, 2 * D), grp] + res * 2
        + [ANY, ANY, ANY],
        [_row_tile(TS, D), _row_tile(TS, 2 * D), _row_tile(TS, D), _row_tile(TS, D), grp]
        + res * 2 + [_const((8, D))],
        [jax.ShapeDtypeStruct((S, D), BF16), jax.ShapeDtypeStruct((S, 2 * D), BF16),
         jax.ShapeDtypeStruct((S, D), BF16), jax.ShapeDtypeStruct((S, D), BF16), jax.ShapeDtypeStruct((S, GW), F32)]
        + [_res_shape(S, r, BF16) for r in DILATIONS] + [_res_shape(S, r, F32) for r in DILATIONS]
        + [jax.ShapeDtypeStruct((8, D), F32)],
        [pltpu.VMEM((GW, D), BF16), pltpu.VMEM((GW, D), BF16), pltpu.VMEM((D, D), BF16),
         pltpu.SemaphoreType.DMA((3,)), pltpu.VMEM(RES_SCRATCH, F32)])
    in_specs, out_specs, out_shape, scratch = specs if rider is None else rider.specs(*specs)
    outs = pl.pallas_call(
        body, name=name, grid=(S // TS,), in_specs=in_specs, out_specs=out_specs, out_shape=out_shape,
        scratch_shapes=scratch, compiler_params=_params(),
    )(dh, tm, vec, gates, ypool, *o3, *lse3, wpb, wab, wout, *(rider.arrays if rider else []))
    return outs if rider is None else (outs[:12], outs[12:])


def _ada_mod(c_all, w, b, name):
    def body(c_ref, w_ref, b_ref, cond_ref, mod_ref):
        cv = c_ref[...]
        cond = cv * jax.nn.sigmoid(cv)
        cond_ref[...] = cond
        mod_ref[...] = jnp.dot(cond, w_ref[...], preferred_element_type=F32,
                               precision=lax.Precision.HIGHEST) + b_ref[...]

    return pl.pallas_call(
        body, name=name,
        out_shape=[jax.ShapeDtypeStruct(c_all.shape, F32), jax.ShapeDtypeStruct((c_all.shape[0], w.shape[1]), F32)],
        compiler_params=_params(),
    )(c_all, w, b)


def _adamw_math(w, g, m, v):
    m = ADAM_B1 * m + (1.0 - ADAM_B1) * g
    v = ADAM_B2 * v + (1.0 - ADAM_B2) * (g * g)
    m_hat = m / (1.0 - ADAM_B1 ** ADAM_STEP)
    v_hat = v / (1.0 - ADAM_B2 ** ADAM_STEP)
    delta = -ADAM_LR * (m_hat / (jnp.sqrt(v_hat) + ADAM_EPS) + ADAM_WD * w)
    return delta, m, v


def _adamw_many(ws, gs, ms, vs, name):
    n = len(ws)

    def body(*refs):
        for k in range(n):
            w_ref, g_ref, m_ref, v_ref = (refs[j * n + k] for j in range(4))
            d_ref, mo_ref, vo_ref = (refs[(4 + j) * n + k] for j in range(3))
            d_ref[...], mo_ref[...], vo_ref[...] = _adamw_math(w_ref[...], g_ref[...], m_ref[...], v_ref[...])

    outs = pl.pallas_call(
        body, name=name, out_shape=[jax.ShapeDtypeStruct(t.shape, F32) for t in ws] * 3,
        compiler_params=_params(),
    )(*ws, *gs, *ms, *vs)
    return outs[:n], outs[n:2 * n], outs[2 * n:]


def _ada_grad_adamw(cond_t, dmod, w, m, v, name, tr=256):
    R, C = w.shape
    nb = dmod.shape[0]

    def body(ct_ref, dm_ref, w_ref, m_ref, v_ref, g_ref, d_ref, mo_ref, vo_ref):
        ct = ct_ref[...]
        dm = dm_ref[...]
        g = jnp.zeros((tr, C), F32)
        for bi in range(nb):
            g = g + ct[:, bi:bi + 1] * dm[bi:bi + 1, :]
        g_ref[...] = g
        d_ref[...], mo_ref[...], vo_ref[...] = _adamw_math(w_ref[...], g, m_ref[...], v_ref[...])

    spec = _row_tile(tr, C)
    out = jax.ShapeDtypeStruct((R, C), F32)
    return pl.pallas_call(
        body, name=name, grid=(R // tr,),
        in_specs=[_row_tile(tr, nb), _const((nb, C)), spec, spec, spec],
        out_specs=[spec] * 4, out_shape=[out] * 4,
        compiler_params=_params(),
    )(cond_t, dmod, w, m, v)


def _row_step(rows, cap=256):
    for cand in range(cap, 15, -16):
        if rows % cand == 0:
            return cand
    return rows


def _slot_sum(x_ref):
    acc = x_ref[0].astype(F32)
    for k in range(1, x_ref.shape[0]):
        acc = acc + x_ref[k].astype(F32)
    return acc


def _sum_slots(x, name, out_dtype=F32):
    n, R, C = x.shape
    tr = _row_step(R)

    def body(x_ref, o_ref):
        o_ref[...] = _slot_sum(x_ref).astype(out_dtype)

    return pl.pallas_call(
        body, name=name, grid=(R // tr,),
        in_specs=[pl.BlockSpec((n, tr, C), lambda i: (0, i, 0))],
        out_specs=_row_tile(tr, C), out_shape=jax.ShapeDtypeStruct((R, C), out_dtype),
        compiler_params=_params(),
    )(x)


def _sum_pair(core, g, recv, name):
    _, _, R, C = g.shape
    tr = _row_step(R, cap=1024)

    def body(core_ref, g_ref, r_ref, o_ref):
        o_ref[...] = (g_ref[...].astype(F32) + r_ref[...].astype(F32)).astype(BF16)

    return pl.pallas_call(
        body, name=name, out_shape=jax.ShapeDtypeStruct((4, R, C), BF16),
        grid_spec=pltpu.PrefetchScalarGridSpec(
            num_scalar_prefetch=1, grid=(4, R // tr),
            in_specs=[pl.BlockSpec((None, None, tr, C), lambda k, i, core_ref: (k, core_ref[0], i, 0)),
                      pl.BlockSpec((None, tr, C), lambda k, i, core_ref: (k, i, 0))],
            out_specs=pl.BlockSpec((None, tr, C), lambda k, i, core_ref: (k, i, 0))),
        compiler_params=_params(),
    )(core, g, recv)


def _sum_adamw(chip, own, recv, w, m, v, name):
    _, R, C = own.shape
    tr = _row_step(R, cap=512)

    def body(chip_ref, own_ref, r_ref, w_ref, m_ref, v_ref, g_ref, d_ref, mo_ref, vo_ref):
        g = own_ref[...].astype(F32) + _slot_sum(r_ref)
        g_ref[...] = g
        d_ref[...], mo_ref[...], vo_ref[...] = _adamw_math(w_ref[...], g, m_ref[...], v_ref[...])

    spec = pl.BlockSpec((tr, C), lambda i, chip_ref: (i, 0))
    out = jax.ShapeDtypeStruct((R, C), F32)
    return pl.pallas_call(
        body, name=name, out_shape=[out] * 4,
        grid_spec=pltpu.PrefetchScalarGridSpec(
            num_scalar_prefetch=1, grid=(R // tr,),
            in_specs=[pl.BlockSpec((None, tr, C), lambda i, chip_ref: (chip_ref[0], i, 0)),
                      pl.BlockSpec((3, tr, C), lambda i, chip_ref: (0, i, 0)), spec, spec, spec],
            out_specs=[spec] * 4),
        compiler_params=_params(),
    )(chip, own, recv, w, m, v)


def _place():
    return lax.axis_index("x"), lax.axis_index("y"), lax.axis_index("c")


def _gather_steps(x_refs, out_refs, send_sems, recv_sems):
    n = len(x_refs)
    x, y, c = _place()
    me, sibling = (x, y, c), (x, y, 1 - c)
    chips = [(1 - x, y), (x, 1 - y), (1 - x, 1 - y)]

    def rows(a, px, py, pc):
        return out_refs[a].at[4 * px + 2 * py + pc]

    def copy(a, k, block, to, src=None):
        return pltpu.make_async_remote_copy(
            src_ref=rows(a, *block) if src is None else src, dst_ref=rows(a, *block),
            send_sem=send_sems.at[a, k], recv_sem=recv_sems.at[a, k], device_id=to, device_id_type=MESH)

    def first(a):
        return [copy(a, 0, me, sibling, src=x_refs[a])] + [
            copy(a, 1 + j, me, (*chip, c), src=x_refs[a]) for j, chip in enumerate(chips)]

    def passed(a, j):
        return copy(a, 4 + j, (*chips[j], c), sibling)

    def start():
        for a in range(n):
            for cp in first(a):
                cp.start()

    def relay():
        for j, chip in enumerate(chips):
            for a in range(n):
                copy(a, 1 + j, (*chip, c), me).wait_recv()
                passed(a, j).start()

    def finish():
        for a in range(n):
            copy(a, 0, sibling, me).wait_recv()
            for j, chip in enumerate(chips):
                copy(a, 4 + j, (*chip, 1 - c), me).wait_recv()
        for a in range(n):
            for cp in first(a) + [passed(a, j) for j in range(3)]:
                cp.wait_send()

    return start, relay, finish


def _all_gather_tree(arrs, name):
    n = len(arrs)

    def body(*refs):
        x_refs, out_refs = refs[:n], refs[n:2 * n]
        send_sems, recv_sems = refs[2 * n:]
        x, y, c = _place()
        me, sibling = (x, y, c), (x, y, 1 - c)
        xn, yn, dg = (1 - x, y), (x, 1 - y), (1 - x, 1 - y)

        def rows(a, px, py, pc):
            return out_refs[a].at[4 * px + 2 * py + pc]

        def copy(a, k, block, to, src=None):
            return pltpu.make_async_remote_copy(
                src_ref=rows(a, *block) if src is None else src, dst_ref=rows(a, *block),
                send_sem=send_sems.at[a, k], recv_sem=recv_sems.at[a, k], device_id=to, device_id_type=MESH)

        def own(a):
            return [copy(a, 0, me, sibling, src=x_refs[a]), copy(a, 1, me, (*xn, c), src=x_refs[a]),
                    copy(a, 2, me, (*yn, c), src=x_refs[a])]

        def north_hands_on(a):
            return copy(a, 3, (*xn, c), (*yn, c))

        def south_hands_on(a):
            return copy(a, 3, (*yn, c), (*xn, c))

        def to_sibling(a):
            return [copy(a, 4, (*xn, c), sibling), copy(a, 5, (*yn, c), sibling), copy(a, 6, (*dg, c), sibling)]

        for a in range(n):
            for cp in own(a):
                cp.start()
        for a in range(n):
            copy(a, 1, (*xn, c), me).wait_recv()
            to_sibling(a)[0].start()

        @pl.when(c == 1)
        def _():
            for a in range(n):
                north_hands_on(a).start()

        for a in range(n):
            copy(a, 2, (*yn, c), me).wait_recv()
            to_sibling(a)[1].start()

        @pl.when(c == 0)
        def _():
            for a in range(n):
                south_hands_on(a).start()

        for a in range(n):
            copy(a, 3, (*dg, c), me).wait_recv()
            to_sibling(a)[2].start()
        for a in range(n):
            copy(a, 0, sibling, me).wait_recv()
            copy(a, 4, (*xn, 1 - c), me).wait_recv()
            copy(a, 5, (*yn, 1 - c), me).wait_recv()
            copy(a, 6, (*dg, 1 - c), me).wait_recv()
        for a in range(n):
            for cp in own(a) + to_sibling(a):
                cp.wait_send()

        @pl.when(c == 1)
        def _():
            for a in range(n):
                north_hands_on(a).wait_send()

        @pl.when(c == 0)
        def _():
            for a in range(n):
                south_hands_on(a).wait_send()

    return pl.pallas_call(
        body, name=name, out_shape=[jax.ShapeDtypeStruct((N_DEV,) + t.shape, t.dtype) for t in arrs],
        in_specs=[ANY] * n, out_specs=[ANY] * n,
        scratch_shapes=[pltpu.SemaphoreType.DMA((n, 7)), pltpu.SemaphoreType.DMA((n, 7))],
    )(*arrs)


def _all_gather(arrs, name, own=True):
    n = len(arrs)

    def body(*refs):
        x_refs, out_refs = refs[:n], refs[n:2 * n]
        send_sems, recv_sems, local_sems = refs[2 * n:]
        me = 4 * lax.axis_index("x") + 2 * lax.axis_index("y") + lax.axis_index("c")
        mine = [pltpu.make_async_copy(x_refs[a], out_refs[a].at[me], local_sems.at[a]) for a in range(n)] if own else []
        for cp in mine:
            cp.start()
        for step in _gather_steps(x_refs, out_refs, send_sems, recv_sems):
            step()
        for cp in mine:
            cp.wait()

    return pl.pallas_call(
        body, name=name, out_shape=[jax.ShapeDtypeStruct((N_DEV,) + t.shape, t.dtype) for t in arrs],
        in_specs=[ANY] * n, out_specs=[ANY] * n,
        scratch_shapes=[pltpu.SemaphoreType.DMA((n, 7)), pltpu.SemaphoreType.DMA((n, 7)),
                        pltpu.SemaphoreType.DMA((n,))],
    )(*arrs)


def _pair_exchange_steps(g_refs, out_refs, send_sems, recv_sems):
    x, y, c = _place()

    def give():
        return [pltpu.make_async_remote_copy(
            src_ref=g_refs[a].at[pl.ds(0, 4), 1 - c], dst_ref=out_refs[a], send_sem=send_sems.at[a],
            recv_sem=recv_sems.at[a], device_id=(x, y, 1 - c), device_id_type=MESH) for a in range(len(g_refs))]

    def start():
        for cp in give():
            cp.start()

    def finish():
        for cp in give():
            cp.wait()

    return start, finish


def _pair_exchange(arrs, name):
    n = len(arrs)

    def body(*refs):
        for step in _pair_exchange_steps(refs[:n], refs[n:2 * n], *refs[2 * n:]):
            step()

    return pl.pallas_call(
        body, name=name,
        out_shape=[jax.ShapeDtypeStruct((4,) + t.shape[2:], t.dtype) for t in arrs],
        in_specs=[ANY] * n, out_specs=[ANY] * n,
        scratch_shapes=[pltpu.SemaphoreType.DMA((n,)), pltpu.SemaphoreType.DMA((n,))],
    )(*arrs)


def _chip_exchange_steps(p_refs, out_refs, send_sems, recv_sems):
    x, y, c = _place()
    chips = [(1 - x, y), (x, 1 - y), (1 - x, 1 - y)]

    def copies():
        return [pltpu.make_async_remote_copy(
            src_ref=p_refs[a].at[2 * px + py], dst_ref=out_refs[a].at[j], send_sem=send_sems.at[a, j],
            recv_sem=recv_sems.at[a, j], device_id=(px, py, c), device_id_type=MESH)
            for a in range(len(p_refs)) for j, (px, py) in enumerate(chips)]

    def start():
        for cp in copies():
            cp.start()

    def finish():
        for cp in copies():
            cp.wait()

    return start, finish


def _chip_exchange(arrs, name):
    n = len(arrs)

    def body(*refs):
        for step in _chip_exchange_steps(refs[:n], refs[n:2 * n], *refs[2 * n:]):
            step()

    return pl.pallas_call(
        body, name=name, out_shape=[jax.ShapeDtypeStruct((3,) + t.shape[1:], t.dtype) for t in arrs],
        in_specs=[ANY] * n, out_specs=[ANY] * n,
        scratch_shapes=[pltpu.SemaphoreType.DMA((n, 3)), pltpu.SemaphoreType.DMA((n, 3))],
    )(*arrs)


class _Rider:
    def __init__(self, arrays, out_shape, sems, steps, relay_before_end=None):
        self.arrays, self.out_shape, self.scratch, self.steps = list(arrays), out_shape, sems, steps
        self.n = len(self.arrays)
        self.relay_before_end = relay_before_end

    def specs(self, in_specs, out_specs, out_shape, scratch):
        extra = [ANY] * self.n
        return in_specs + extra, out_specs + extra, out_shape + self.out_shape, scratch + self.scratch

    def split(self, refs, n_in, n_out):
        k = self.n
        a, b = n_in + k, n_in + k + n_out
        return refs[:n_in] + refs[a:b] + refs[b + k:-2], (refs[n_in:a], refs[b:b + k], refs[-2:])

    def head(self, mine, step):
        pl.when(step == 0)(self.steps(mine[0], mine[1], *mine[2])[0])

    def tail(self, mine, step, nsteps):
        steps = self.steps(mine[0], mine[1], *mine[2])
        if self.relay_before_end is not None:
            pl.when(step == nsteps - 1 - self.relay_before_end)(steps[1])
        pl.when(step == nsteps - 1)(steps[-1])


def _gather_rider(arrs, relay_before_end=4):
    n = len(arrs)
    return _Rider(arrs, [jax.ShapeDtypeStruct((N_DEV,) + t.shape, t.dtype) for t in arrs],
                  [pltpu.SemaphoreType.DMA((n, 7)), pltpu.SemaphoreType.DMA((n, 7))], _gather_steps,
                  relay_before_end)


def _pair_exchange_rider(arrs):
    n = len(arrs)
    return _Rider(arrs, [jax.ShapeDtypeStruct((4,) + t.shape[2:], t.dtype) for t in arrs],
                  [pltpu.SemaphoreType.DMA((n,)), pltpu.SemaphoreType.DMA((n,))], _pair_exchange_steps)


def _chip_exchange_rider(arrs):
    n = len(arrs)
    return _Rider(arrs, [jax.ShapeDtypeStruct((3,) + t.shape[1:], t.dtype) for t in arrs],
                  [pltpu.SemaphoreType.DMA((n, 3)), pltpu.SemaphoreType.DMA((n, 3))], _chip_exchange_steps)


def _rope_tables(positions):
    inv_freq = ROPE_THETA ** (-jnp.arange(0, HEAD_DIM, 2, dtype=F32) / HEAD_DIM)
    ang = positions.astype(F32)[:, None] * inv_freq
    cos, sin = jnp.cos(ang), jnp.sin(ang)
    return jnp.tile(cos, (1, 4)), jnp.tile(jnp.concatenate([-sin, sin], axis=1), (1, 2))


class _GradReducer:
    def __init__(self):
        self.core = lax.axis_index("c").astype(jnp.int32).reshape(1)
        self.own, self.others, self.waiting, self.riding = {}, {}, [], []

    def pair(self, named):
        mine = self._split(named)
        self._summed(mine, _pair_exchange(list(mine.values()), "reduce_pair_" + next(iter(named))))

    def pair_rider(self, named):
        self.pairing = self._split(named)
        return _pair_exchange_rider(list(self.pairing.values()))

    def pair_landed(self, results):
        self._summed(self.pairing, results)

    @staticmethod
    def _split(named):
        return {k: g.reshape((4, 2) + g.shape[1:]) for k, g in named.items()}

    def _summed(self, mine, theirs):
        for (k, g), r in zip(mine.items(), theirs):
            self.own[k] = _sum_pair(self.core, g, r, "sum_pair_" + k)
        self.waiting += list(mine)

    def rider(self):
        self.riding, self.waiting = self.waiting, []
        return _chip_exchange_rider([self.own[k] for k in self.riding])

    def landed(self, results):
        self.others.update(zip(self.riding, results))

    def flush(self, name):
        keys, self.waiting = self.waiting, []
        self.others.update(zip(keys, _chip_exchange([self.own[k] for k in keys], name)))


def _by_owner(g):
    if g.ndim == 3:
        return g if g.shape[0] == N_DEV else g.reshape(N_DEV, g.shape[1] * g.shape[0] // N_DEV, g.shape[2])
    return g.reshape(N_DEV, g.shape[0] // N_DEV, g.shape[1])


def _local_step(x, target, positions, mod, small, W, late=None, red=None):
    S, D = x.shape
    gains =jnp.stack([small["g1"], small["g2"], small["g3"]])[:, None, :]
    v1, v2, v3 = jnp.pad(jnp.concatenate([gains, mod.reshape(3, 3, D)], axis=1), ((0, 0), (0, 4), (0, 0)))
    vf = jnp.pad(small["gf"][None], ((0, 7), (0, 0)))
    cos, sin = _rope_tables(positions)
    wbd = jax.scipy.linalg.block_diag(*[small["w_pool"][k] for k in range(4)]).astype(BF16)
    pscale = small["pool_scale"].reshape(1, GW)

    if late is None:
        h1, u1, ab1, act1, f1 = _ffn_fwd(x, v1, W["w1in"], W["w1out"], "ffn1_fwd")
    else:
        (h1, u1, ab1, act1, f1), landed = _ffn_fwd(x, v1, W["w1in"], W["w1out"], "ffn1_fwd", rider=late[0])
        W = {**W, **late[1](landed)}
    u2, p, gates, *qkv = _mix_in_fwd(h1, v2, cos, sin, W["win"], "mix_in_fwd")
    dpool, ypool = _pool_fwd(p, wbd, pscale, "pool_fwd")
    o3, lse3 = [], []
    for gi in range(len(DILATIONS)):
        o, lse = _attn_fwd(qkv[gi], qkv[3 + gi], qkv[6 + gi], f"attn_fwd_{gi}")
        o3.append(o)
        lse3.append(lse)
    h2, ya, merged, tm = _mix_out_fwd(h1, v2, gates, ypool, o3, lse3, W["wpb"], W["wab"], W["wout"], "mix_out_fwd")
    dh3, u3, ab3, act3, f3, loss_blk, dgf = _ffn_fwd(h2, v3, W["w2in"], W["w2out"], "ffn2_fwd", final=(target, vf))

    dh2, dab3, df3, red3 = _ffn_bwd(dh3, h2, f3, ab3, v3, W["w2in"], W["w2out"], "ffn2_bwd")
    half_f = ab3.shape[2] // 2
    G = {"w2in": _by_owner(_wgrad(dab3, u3, "wgrad_2in", tm=half_f))}
    mix_out_args = (dh2, tm, v2, gates, ypool, o3, lse3, W["wpb"], W["wab"], W["wout"], "mix_out_bwd")
    if red is None:
        G["w2out"] = _by_owner(_wgrad(act3, df3, "wgrad_2out", tm=half_f))
        mix_out = _mix_out_bwd(*mix_out_args)
    else:
        g2out, landed = _wgrad(act3, df3, "wgrad_2out", tm=half_f, rider=red.pair_rider({"w2in": G["w2in"]}))
        red.pair_landed(landed)
        G["w2out"] = _by_owner(g2out)
        red.pair({"w2out": G["w2out"]})
        mix_out, landed = _mix_out_bwd(*mix_out_args, rider=red.rider())
        red.landed(landed)
    (dtm, dgl, dypb, dyab, dyp, do0, do1, do2, c0, c1, c2, red2o) = mix_out
    dq3, dk3, dv3 = [], [], []
    for gi, (do, ct) in enumerate(zip((do0, do1, do2), (c0, c1, c2))):
        dq, dk, dv = _attn_bwd(qkv[gi], qkv[3 + gi], qkv[6 + gi], do, lse3[gi], ct, f"attn_bwd_{gi}")
        dq3.append(dq)
        dk3.append(dk)
        dv3.append(dv)
    dp, dwbd, dps = _pool_bwd(dyp, dpool, wbd, pscale, "pool_bwd")
    dh1, dproj, red2i = _mix_in_bwd(dh2, h1, v2, cos, sin, dp, dq3 + dk3 + dv3, dgl, W["win"], "mix_in_bwd")
    G["win"] = _by_owner(_wgrad(dproj, u2, "wgrad_in", tm=1152))
    G["wpb"] = _full_to_cols(_wgrad(ypool, dypb, "wgrad_pb"))
    G["wab"] = _full_to_cols(_wgrad(ya, dyab, "wgrad_ab"))
    if red is None:
        G["wout"] = _by_owner(_wgrad(merged, dtm, "wgrad_out"))
    else:
        gout, landed = _wgrad(merged, dtm, "wgrad_out", rider=red.pair_rider({k: G[k] for k in ("win", "wpb", "wab")}))
        red.pair_landed(landed)
        G["wout"] = _by_owner(gout)
        red.pair({"wout": G["wout"]})
    dx, dab1, df1, red1 = _ffn_bwd(dh1, x, f1, ab1, v1, W["w1in"], W["w1out"], "ffn1_bwd")
    if red is None:
        G["w1in"] = _by_owner(_wgrad(dab1, u1, "wgrad_1in", tm=half_f))
        G["w1out"] = _by_owner(_wgrad(act1, df1, "wgrad_1out", tm=half_f))
    else:
        g1in, landed = _wgrad(dab1, u1, "wgrad_1in", tm=half_f, rider=red.rider())
        red.landed(landed)
        G["w1in"] = _by_owner(g1in)
        red.pair({"w1in": G["w1in"]})
        g1out, landed = _wgrad(act1, df1, "wgrad_1out", tm=half_f, rider=red.rider())
        red.landed(landed)
        G["w1out"] = _by_owner(g1out)
        red.pair({"w1out": G["w1out"]})
        red.flush("reduce_chips_w1out")
    dmod = jnp.concatenate([red1[:3], (red2i + red2o)[:3], red3[:3]])
    dsmall = {
        "g1": red1[3], "g2": red2i[3], "g3": red3[3], "gf": dgf[0],
        "w_pool": jnp.stack([dwbd[k * 64:(k + 1) * 64, k * 64:(k + 1) * 64] for k in range(4)]),
        "pool_scale": dps[0],
    }
    return loss_blk[0, 0], dx, G, dmod, dsmall


SHARDED = ("w_ffn1_in", "w_ffn1_out", "w_in", "w_pool_branch", "w_attn_branch", "w_out", "w_ffn2_in", "w_ffn2_out")
TRANSPOSED = ("w_ffn1_in", "w_in", "w_ffn2_in")
FIRST = ("w_ffn1_in", "w_ffn1_out")
LATER = tuple(n for n in SHARDED if n not in FIRST)
GRAD_KEY = dict(w_ffn1_in="w1in", w_ffn1_out="w1out", w_in="win", w_pool_branch="wpb", w_attn_branch="wab",
                w_out="wout", w_ffn2_in="w2in", w_ffn2_out="w2out")


def _cols_to_full(g):
    return g.transpose(1, 0, 2).reshape(g.shape[1], N_DEV * g.shape[2])


def _full_to_cols(t):
    return t.reshape(t.shape[0], N_DEV, t.shape[1] // N_DEV).transpose(1, 0, 2)


SMALL = (("b_ada", 72), ("g_norm_ffn1", 8), ("g_norm_mix", 8), ("g_norm_ffn2", 8), ("g_final", 8),
         ("w_pool", 128), ("pool_scale", 8))


def _pack_small(vals, loss):
    rows = []
    for name, nrows in SMALL:
        t = vals[name].reshape(-1, 128)
        rows.append(jnp.pad(t, ((0, nrows - t.shape[0]), (0, 0))))
    rows.append(jnp.full((8, 128), loss, F32))
    return jnp.concatenate(rows)


def _unpack_small(slab, shapes):
    out, off = {}, 0
    for name, nrows in SMALL:
        used = 1
        for d in shapes[name]:
            used *= d
        out[name] = slab[off:off + used // 128].reshape(shapes[name])
        off += nrows
    return out, slab[off, 0]


def _as_2d(t):
    return t.reshape(-1, t.shape[-1])


def kernel(x, c, positions, w_ada, b_ada, g_norm_ffn1, w_ffn1_in, w_ffn1_out, g_norm_mix, w_in, w_pool, pool_scale, w_pool_branch, w_attn_branch, w_out, g_norm_ffn2, w_ffn2_in, w_ffn2_out, g_final, loss_target, m_w_ada, m_b_ada, m_g_norm_ffn1, m_w_ffn1_in, m_w_ffn1_out, m_g_norm_mix, m_w_in, m_w_pool, m_pool_scale, m_w_pool_branch, m_w_attn_branch, m_w_out, m_g_norm_ffn2, m_w_ffn2_in, m_w_ffn2_out, m_g_final, v_w_ada, v_b_ada, v_g_norm_ffn1, v_w_ffn1_in, v_w_ffn1_out, v_g_norm_mix, v_w_in, v_w_pool, v_pool_scale, v_w_pool_branch, v_w_attn_branch, v_w_out, v_g_norm_ffn2, v_w_ffn2_in, v_w_ffn2_out, v_g_final):
    names = ["w_ada", "b_ada", "g_norm_ffn1", "w_ffn1_in", "w_ffn1_out", "g_norm_mix", "w_in", "w_pool", "pool_scale",
             "w_pool_branch", "w_attn_branch", "w_out", "g_norm_ffn2", "w_ffn2_in", "w_ffn2_out", "g_final"]
    w = dict(w_ada=w_ada, b_ada=b_ada, g_norm_ffn1=g_norm_ffn1, w_ffn1_in=w_ffn1_in, w_ffn1_out=w_ffn1_out,
             g_norm_mix=g_norm_mix, w_in=w_in, w_pool=w_pool, pool_scale=pool_scale, w_pool_branch=w_pool_branch,
             w_attn_branch=w_attn_branch, w_out=w_out, g_norm_ffn2=g_norm_ffn2, w_ffn2_in=w_ffn2_in,
             w_ffn2_out=w_ffn2_out, g_final=g_final)
    m = dict(w_ada=m_w_ada, b_ada=m_b_ada, g_norm_ffn1=m_g_norm_ffn1, w_ffn1_in=m_w_ffn1_in, w_ffn1_out=m_w_ffn1_out,
             g_norm_mix=m_g_norm_mix, w_in=m_w_in, w_pool=m_w_pool, pool_scale=m_pool_scale,
             w_pool_branch=m_w_pool_branch, w_attn_branch=m_w_attn_branch, w_out=m_w_out, g_norm_ffn2=m_g_norm_ffn2,
             w_ffn2_in=m_w_ffn2_in, w_ffn2_out=m_w_ffn2_out, g_final=m_g_final)
    v = dict(w_ada=v_w_ada, b_ada=v_b_ada, g_norm_ffn1=v_g_norm_ffn1, w_ffn1_in=v_w_ffn1_in, w_ffn1_out=v_w_ffn1_out,
             g_norm_mix=v_g_norm_mix, w_in=v_w_in, w_pool=v_w_pool, pool_scale=v_pool_scale,
             w_pool_branch=v_w_pool_branch, w_attn_branch=v_w_attn_branch, w_out=v_w_out, g_norm_ffn2=v_g_norm_ffn2,
             w_ffn2_in=v_w_ffn2_in, w_ffn2_out=v_w_ffn2_out, g_final=v_g_final)
    shapes = {n: w[n].shape for n in names}
    me = 4 * lax.axis_index("x") + 2 * lax.axis_index("y") + lax.axis_index("c")
    D = x.shape[-1]
    n_mod = w_ada.shape[-1] * N_DEV // D

    (c_all,) = _all_gather([c.reshape(D // 128, 128)], "gather_c")
    ada_cols = w_ada.shape[-1]
    b_mine = lax.dynamic_slice_in_dim(b_ada, me * ada_cols, ada_cols, axis=1)
    cond, mod_part = _ada_mod(c_all.reshape(N_DEV, D), w_ada[0], b_mine, "ada_mod")
    (mod_all,) = _all_gather([mod_part.reshape(-1, 128)], "gather_mod")
    mod_all = mod_all.reshape(N_DEV, N_DEV, ada_cols)
    mod = lax.dynamic_index_in_dim(mod_all, me, axis=1, keepdims=False).reshape(n_mod, D)

    def local(t, name):
        return t[name][0].T if name in TRANSPOSED else t[name][0]

    shards = {name: local(w, name).astype(BF16) for name in SHARDED}

    def gather_done(names, fulls):
        return {name: lax.dynamic_update_index_in_dim(full, shards[name], me, axis=0)
                for name, full in zip(names, fulls)}

    def ffn_weights(g, pre):
        return {"w%sin" % pre: g["w_ffn%s_in" % pre].reshape(2, -1, D),
                "w%sout" % pre: g["w_ffn%s_out" % pre].reshape(-1, D)}

    def later_weights(fulls):
        g = gather_done(LATER, fulls)
        return dict(win=g["w_in"].reshape(-1, D), wpb=_cols_to_full(g["w_pool_branch"]),
                    wab=_cols_to_full(g["w_attn_branch"]), wout=g["w_out"].reshape(D, D), **ffn_weights(g, "2"))

    W = ffn_weights(gather_done(FIRST, _all_gather_tree([shards[n] for n in FIRST], "gather_ffn1")), "1")
    small = dict(g1=g_norm_ffn1[0], g2=g_norm_mix[0], g3=g_norm_ffn2[0], gf=g_final, w_pool=w_pool[0],
                 pool_scale=pool_scale[0])
    red = _GradReducer()
    loss_part, dx, G, dmod, dsmall = _local_step(
        x[0], loss_target[0], positions[0], mod, small, W,
        late=(_gather_rider([shards[n] for n in LATER]), later_weights), red=red)
    chip = (2 * lax.axis_index("x") + lax.axis_index("y")).astype(jnp.int32).reshape(1)

    part = _pack_small(dict(b_ada=dmod, g_norm_ffn1=dsmall["g1"], g_norm_mix=dsmall["g2"], g_norm_ffn2=dsmall["g3"],
                            g_final=dsmall["gf"], w_pool=dsmall["w_pool"], pool_scale=dsmall["pool_scale"]),
                       loss_part)
    (parts,) = _all_gather([part], "gather_small")
    gsmall, loss = _unpack_small(_sum_slots(parts, "sum_small"), shapes)
    rows_mine = ada_cols // 128
    dmod_mine = lax.dynamic_slice_in_dim(parts, me * rows_mine, rows_mine, axis=1).reshape(N_DEV, ada_cols)

    grads, delta, new_m, new_v = {}, {}, {}, {}
    grads["w_ada"], delta["w_ada"], new_m["w_ada"], new_v["w_ada"] = (
        t[None] for t in _ada_grad_adamw(cond.T, dmod_mine, w_ada[0], m_w_ada[0], v_w_ada[0], "ada_grad_adamw"))
    for name in SHARDED:
        key = GRAD_KEY[name]
        res = _sum_adamw(chip, red.own[key], red.others[key], local(w, name), local(m, name), local(v, name),
                         "adamw_" + name)
        grads[name], delta[name], new_m[name], new_v[name] = (
            (t.T if name in TRANSPOSED else t)[None] for t in res)
    small_names = [name for name, _ in SMALL]
    res = _adamw_many(*([_as_2d(t[name]) for name in small_names] for t in (w, gsmall, m, v)), "adamw_small")
    for dst, vals in zip((delta, new_m, new_v), res):
        dst.update({name: t.reshape(shapes[name]) for name, t in zip(small_names, vals)})
    grads.update(gsmall)

    return (loss, dx[None], *[grads[n] for n in names], *[delta[n] for n in names],
            *[new_m[n] for n in names], *[new_v[n] for n in names])
```

```python
import functools

import jax
import jax.numpy as jnp
from jax import lax
from jax.experimental import pallas as pl
from jax.experimental.pallas import tpu as pltpu

F32 = jnp.float32
BF16 = jnp.bfloat16
MESH = pl.DeviceIdType.MESH
ANY = pl.BlockSpec(memory_space=pl.ANY)

N_DEV = 8
EPS = 1e-6
HEAD_DIM = 64
HEADS = 4
GW = HEADS * HEAD_DIM
DILATIONS = (1, 4, 16)
BAND = 128
QB = 128
POOL_WINDOWS = (2, 4, 8, 16)
HALO = 16
ROPE_THETA = 10000.0

ADAM_LR = 0.001
ADAM_B1 = 0.9
ADAM_B2 = 0.999
ADAM_EPS = 1e-08
ADAM_WD = 0.01
ADAM_STEP = 10

VMEM_LIMIT = 56 * 1024 * 1024
TS = 512
FFN_TS = 256
FFN_CHUNKS = (2816,)

NT = (((1,), (1,)), ((), ()))
TN = (((0,), (0,)), ((), ()))


def _params(**kw):
    return pltpu.CompilerParams(vmem_limit_bytes=VMEM_LIMIT, **kw)


def _dot(a, b):
    return jnp.dot(a, b, preferred_element_type=F32)


def _dot_nt(a, b):
    return lax.dot_general(a, b, NT, preferred_element_type=F32)


def _dot_tn(a, b):
    return lax.dot_general(a, b, TN, preferred_element_type=F32)


def _load_weights(pairs, sem):
    @pl.when(pl.program_id(0) == 0)
    def _():
        copies = [pltpu.make_async_copy(src, dst, sem.at[i]) for i, (src, dst) in enumerate(pairs)]
        for cp in copies:
            cp.start()
        for cp in copies:
            cp.wait()


def _norm_mod(x, g, sc, sh):
    r = lax.rsqrt(jnp.mean(x * x, axis=-1, keepdims=True) + EPS)
    xn = x * r
    y = xn * g
    return r, xn, y, y * (1.0 + sc) + sh


def _norm_mod_bwd(du, r, xn, y, g, sc):
    dsh = jnp.sum(du, axis=0, keepdims=True)
    dsc = jnp.sum(du * y, axis=0, keepdims=True)
    dy = du * (1.0 + sc)
    dg = jnp.sum(dy * xn, axis=0, keepdims=True)
    dxn = dy * g
    dx = r * (dxn - xn * jnp.mean(dxn * xn, axis=-1, keepdims=True))
    return dx, dsh, dsc, dg


def _row_tile(ts, width):
    return pl.BlockSpec((ts, width), lambda i: (i, 0))


def _const(shape):
    return pl.BlockSpec(shape, lambda *_: (0,) * len(shape))


def _ffn_chunks(Fd):
    assert sum(FFN_CHUNKS) == Fd
    edges = [sum(FFN_CHUNKS[:k]) for k in range(len(FFN_CHUNKS) + 1)]
    return [slice(a, b) for a, b in zip(edges[:-1], edges[1:])]
def _final_tile(x, g, target):
    r = lax.rsqrt(jnp.mean(x * x, axis=-1, keepdims=True) + EPS)
    xn = x * r
    err = xn * g - target
    loss = 0.5 * jnp.sum(jnp.mean(err * err, axis=-1, keepdims=True))
    dy = err * (1.0 / x.shape[-1])
    dg = jnp.sum(dy * xn, axis=0, keepdims=True)
    dxn = dy * g
    return r * (dxn - xn * jnp.mean(dxn * xn, axis=-1, keepdims=True)), loss, dg


def _ffn_fwd(h, vec, win, wout, name, rider=None, final=None):
    TS = FFN_TS
    S, D = h.shape
    _, Fd, _ = win.shape
    n_in, n_out = (6, 7) if final else (4, 5)

    def body(*refs):
        if rider is None:
            return compute(*refs)
        host, mine = rider.split(refs, n_in, n_out)
        rider.head(mine, pl.program_id(0))
        compute(*host)
        rider.tail(mine, pl.program_id(0), S // TS)

    def compute(*refs):
        h_ref, vec_ref, win_hbm, wout_hbm = refs[:4]
        hn_ref, u_ref, ab_ref, act_ref, f_ref = refs[n_in:n_in + 5]
        win_v, wout_v, sem = refs[n_in + n_out:]
        _load_weights([(win_hbm, win_v), (wout_hbm, wout_v)], sem)
        x = h_ref[...]
        g, sh, sc, gt = (vec_ref[k:k + 1, :] for k in range(4))
        _, _, _, u = _norm_mod(x, g, sc, sh)
        ub = u.astype(BF16)
        u_ref[...] = ub
        acc = jnp.zeros((TS, D), F32)
        for sl in _ffn_chunks(Fd):
            a = _dot_nt(ub, win_v[0, sl, :])
            b = _dot_nt(ub, win_v[1, sl, :])
            act = ((a * jax.nn.sigmoid(a)) * b).astype(BF16)
            ab_ref[0, :, sl] = a.astype(BF16)
            ab_ref[1, :, sl] = b.astype(BF16)
            act_ref[:, sl] = act
            acc = acc + _dot(act, wout_v[sl, :])
        f_ref[...] = acc.astype(BF16)
        hn = x + (0.5 * gt) * acc
        if not final:
            hn_ref[...] = hn
            return
        t_ref, gf_ref = refs[4:6]
        loss_ref, dgf_ref = refs[n_in + 5:n_in + 7]

        @pl.when(pl.program_id(0) == 0)
        def _():
            loss_ref[...] = jnp.zeros_like(loss_ref)
            dgf_ref[...] = jnp.zeros_like(dgf_ref)

        hn_ref[...], loss, dg = _final_tile(hn, gf_ref[0:1, :], t_ref[...])
        loss_ref[...] += loss
        dgf_ref[0:1, :] += dg

    specs = (
        [_row_tile(TS, D), _const((8, D)), ANY, ANY] + ([_row_tile(TS, D), _const((8, D))] if final else []),
        [_row_tile(TS, D), _row_tile(TS, D), pl.BlockSpec((2, TS, Fd), lambda i: (0, i, 0)),
         _row_tile(TS, Fd), _row_tile(TS, D)] + ([_const((8, 128)), _const((8, D))] if final else []),
        [jax.ShapeDtypeStruct((S, D), F32), jax.ShapeDtypeStruct((S, D), BF16),
         jax.ShapeDtypeStruct((2, S, Fd), BF16), jax.ShapeDtypeStruct((S, Fd), BF16),
         jax.ShapeDtypeStruct((S, D), BF16)]
        + ([jax.ShapeDtypeStruct((8, 128), F32), jax.ShapeDtypeStruct((8, D), F32)] if final else []),
        [pltpu.VMEM(win.shape, BF16), pltpu.VMEM(wout.shape, BF16), pltpu.SemaphoreType.DMA((2,))])
    in_specs, out_specs, out_shape, scratch = specs if rider is None else rider.specs(*specs)
    outs = pl.pallas_call(
        body, name=name, grid=(S // TS,), in_specs=in_specs, out_specs=out_specs, out_shape=out_shape,
        scratch_shapes=scratch, compiler_params=_params(),
    )(h, vec, win, wout, *(final or ()), *(rider.arrays if rider else []))
    return outs if rider is None else (outs[:n_out], outs[n_out:])


def _ffn_bwd(dh, h, f, ab, vec, win, wout, name):
    TS = FFN_TS
    S, D = h.shape
    _, Fd, _ = win.shape

    def body(dh_ref, h_ref, f_ref, ab_ref, vec_ref, win_hbm, wout_hbm,
             dhp_ref, dab_ref, df_ref, red_ref, win_v, wout_v, sem):
        _load_weights([(win_hbm, win_v), (wout_hbm, wout_v)], sem)

        @pl.when(pl.program_id(0) == 0)
        def _():
            red_ref[...] = jnp.zeros_like(red_ref)

        dh_v = dh_ref[...]
        x = h_ref[...]
        g, sh, sc, gt = (vec_ref[k:k + 1, :] for k in range(4))
        dgt = jnp.sum((0.5 * f_ref[...].astype(F32)) * dh_v, axis=0, keepdims=True)
        dfb = ((0.5 * gt) * dh_v).astype(BF16)
        df_ref[...] = dfb
        du = jnp.zeros((TS, D), F32)
        for sl in _ffn_chunks(Fd):
            dact = _dot_nt(dfb, wout_v[sl, :])
            av = ab_ref[0, :, sl].astype(F32)
            bv = ab_ref[1, :, sl].astype(F32)
            sg = jax.nn.sigmoid(av)
            da = (dact * bv * (sg * (1.0 + av * (1.0 - sg)))).astype(BF16)
            db = (dact * (av * sg)).astype(BF16)
            dab_ref[0, :, sl] = da
            dab_ref[1, :, sl] = db
            du = du + _dot(da, win_v[0, sl, :]) + _dot(db, win_v[1, sl, :])
        r, xn, y, _ = _norm_mod(x, g, sc, sh)
        dx, dsh, dsc, dg = _norm_mod_bwd(du, r, xn, y, g, sc)
        dhp_ref[...] = dh_v + dx
        red_ref[0:1, :] += dsh
        red_ref[1:2, :] += dsc
        red_ref[2:3, :] += dgt
        red_ref[3:4, :] += dg

    ab_spec = pl.BlockSpec((2, TS, Fd), lambda i: (0, i, 0))
    return pl.pallas_call(
        body, name=name, grid=(S // TS,),
        in_specs=[_row_tile(TS, D), _row_tile(TS, D), _row_tile(TS, D), ab_spec, _const((8, D)), ANY, ANY],
        out_specs=[_row_tile(TS, D), ab_spec, _row_tile(TS, D), _const((8, D))],
        out_shape=[jax.ShapeDtypeStruct((S, D), F32), jax.ShapeDtypeStruct((2, S, Fd), BF16),
                   jax.ShapeDtypeStruct((S, D), BF16), jax.ShapeDtypeStruct((8, D), F32)],
        scratch_shapes=[pltpu.VMEM(win.shape, BF16), pltpu.VMEM(wout.shape, BF16), pltpu.SemaphoreType.DMA((2,))],
        compiler_params=_params(),
    )(dh, h, f, ab, vec, win, wout)


def _wgrad(x, y, name, tm=None, ts=2048, rider=None):
    xb = x.ndim == 3
    nb = x.shape[0] if xb else 0
    S, M = x.shape[-2:]
    N = y.shape[-1]
    tm = tm or M
    ts = min(ts, S)
    nk = S // ts
    grid = (max(nb, 1), M // tm, nk)

    def body(*refs):
        if rider is None:
            return compute(*refs)
        host, mine = rider.split(refs, 2, 1)
        step = (pl.program_id(0) * grid[1] + pl.program_id(1)) * grid[2] + pl.program_id(2)
        rider.head(mine, step)
        compute(*host)
        rider.tail(mine, step, grid[0] * grid[1] * grid[2])

    def compute(x_ref, y_ref, o_ref, acc):
        k = pl.program_id(2)

        @pl.when(k == 0)
        def _():
            acc[...] = jnp.zeros_like(acc)

        acc[...] += _dot_tn(x_ref[...], y_ref[...])

        @pl.when(k == nk - 1)
        def _():
            o_ref[...] = acc[...].astype(BF16)

    x_spec = (pl.BlockSpec((None, ts, tm), lambda b, i, k: (b, k, i)) if xb
              else pl.BlockSpec((ts, tm), lambda b, i, k: (k, i)))
    y_spec = pl.BlockSpec((ts, N), lambda b, i, k: (k, 0))
    if xb:
        o_spec, o_shape = pl.BlockSpec((None, tm, N), lambda b, i, k: (b, i, 0)), (nb, M, N)
    else:
        o_spec, o_shape = pl.BlockSpec((tm, N), lambda b, i, k: (i, 0)), (M, N)
    specs = ([x_spec, y_spec], [o_spec], [jax.ShapeDtypeStruct(o_shape, BF16)], [pltpu.VMEM((tm, N), F32)])
    in_specs, out_specs, out_shape, scratch = specs if rider is None else rider.specs(*specs)
    outs = pl.pallas_call(
        body, name=name, grid=grid, in_specs=in_specs, out_specs=out_specs, out_shape=out_shape,
        scratch_shapes=scratch, compiler_params=_params(),
    )(x, y, *(rider.arrays if rider else []))
    return outs[0] if rider is None else (outs[0], outs[1:])


P_OFF, Q_OFF, K_OFF, V_OFF, G_OFF = 0, 256, 1024, 1792, 2560
IN_WIDTH = 4608


def _first_half_mask(ts):
    lane = lax.broadcasted_iota(jnp.int32, (ts, 128), 1)
    return (lane % HEAD_DIM) < (HEAD_DIM // 2)


def _rope(t, cos, sin_signed, first, sign):
    partner = jnp.where(first, pltpu.roll(t, 96, 1), pltpu.roll(t, 32, 1))
    return t * cos + sign * (partner * sin_signed)


def _res_spec(r):
    return pl.BlockSpec((r, TS // r, GW), lambda i: (0, i, 0))


def _res_shape(S, r, dtype):
    return jax.ShapeDtypeStruct((r, S // r, GW), dtype)


def _to_residues(piece, out_ref, lanes, r, scr):
    if r == 1:
        out_ref[0, :, lanes] = piece.astype(out_ref.dtype)
        return
    for h in range(piece.shape[1] // 128):
        scr[h] = piece[:, h * 128:(h + 1) * 128]
        at = slice(lanes.start + h * 128, lanes.start + (h + 1) * 128)
        for res in range(r):
            out_ref[res, :, at] = scr[h, pl.ds(res, TS // r, stride=r), :].astype(out_ref.dtype)


def _from_residues(in_ref, lanes, r, scr):
    if r == 1:
        return in_ref[0, :, lanes].astype(F32)
    halves = (lanes.stop - lanes.start) // 128
    for h in range(halves):
        at = slice(lanes.start + h * 128, lanes.start + (h + 1) * 128)
        for res in range(r):
            scr[h, pl.ds(res, TS // r, stride=r), :] = in_ref[res, :, at].astype(F32)
    return scr[0] if halves == 1 else jnp.concatenate([scr[0], scr[1]], axis=1)


RES_SCRATCH = (2, TS, 128)


def _mix_in_fwd(h, vec, cos, sin, win, name):
    S, D = h.shape

    def body(h_ref, vec_ref, cos_ref, sin_ref, win_hbm, u_ref, p_ref, gates_ref, *rest):
        qkv_refs, (win_v, sem, scr) = rest[:9], rest[9:]
        _load_weights([(win_hbm, win_v)], sem)
        g, sh, sc = (vec_ref[k:k + 1, :] for k in range(3))
        _, _, _, u = _norm_mod(h_ref[...], g, sc, sh)
        ub = u.astype(BF16)
        u_ref[...] = ub
        p_ref[...] = _dot_nt(ub, win_v[P_OFF:Q_OFF, :])
        cosv, sinv = cos_ref[...], sin_ref[...]
        first = _first_half_mask(TS)
        for which, off in enumerate((Q_OFF, K_OFF, V_OFF)):
            t = _dot_nt(ub, win_v[off:off + 3 * GW, :])
            for gi in range(3):
                for half in range(2):
                    c0 = gi * GW + half * 128
                    piece = t[:, c0:c0 + 128]
                    if which < 2:
                        piece = _rope(piece, cosv, sinv, first, 1.0)
                    _to_residues(piece, qkv_refs[which * 3 + gi], slice(half * 128, (half + 1) * 128),
                                 DILATIONS[gi], scr)
        gates_ref[...] = jax.nn.sigmoid(_dot_nt(ub, win_v[G_OFF:IN_WIDTH, :])).astype(BF16)

    return pl.pallas_call(
        body, name=name, grid=(S // TS,),
        in_specs=[_row_tile(TS, D), _const((8, D)), _row_tile(TS, 128), _row_tile(TS, 128), ANY],
        out_specs=[_row_tile(TS, D), _row_tile(TS, GW), _row_tile(TS, 2 * D)] + [_res_spec(r) for r in DILATIONS] * 3,
        out_shape=[jax.ShapeDtypeStruct((S, D), BF16), jax.ShapeDtypeStruct((S, GW), F32),
                   jax.ShapeDtypeStruct((S, 2 * D), BF16)] + [_res_shape(S, r, BF16) for r in DILATIONS] * 3,
        scratch_shapes=[pltpu.VMEM((IN_WIDTH, D), BF16), pltpu.SemaphoreType.DMA((1,)), pltpu.VMEM(RES_SCRATCH, F32)],
        compiler_params=_params(),
    )(h, vec, cos, sin, win)


def _mix_in_bwd(dh, h, vec, cos, sin, dp, dqkv, dgl, win, name):
    S, D = h.shape

    def body(dh_ref, h_ref, vec_ref, cos_ref, sin_ref, dp_ref, *rest):
        dqkv_refs = rest[:9]
        dgl_ref, win_hbm, dhp_ref, dproj_ref, red_ref, win_v, sem, scr = rest[9:]
        _load_weights([(win_hbm, win_v)], sem)

        @pl.when(pl.program_id(0) == 0)
        def _():
            red_ref[...] = jnp.zeros_like(red_ref)

        cosv, sinv = cos_ref[...], sin_ref[...]
        first = _first_half_mask(TS)
        dproj_ref[:, P_OFF:Q_OFF] = dp_ref[...].astype(BF16)
        for which, off in enumerate((Q_OFF, K_OFF, V_OFF)):
            for gi in range(3):
                for half in range(2):
                    piece = _from_residues(dqkv_refs[which * 3 + gi], slice(half * 128, (half + 1) * 128),
                                           DILATIONS[gi], scr)
                    if which < 2:
                        piece = _rope(piece, cosv, sinv, first, -1.0)
                    c0 = off + gi * GW + half * 128
                    dproj_ref[:, c0:c0 + 128] = piece.astype(BF16)
        dproj_ref[:, G_OFF:IN_WIDTH] = dgl_ref[...]
        du = _dot(dproj_ref[...], win_v[...])
        g, sh, sc = (vec_ref[k:k + 1, :] for k in range(3))
        r, xn, y, _ = _norm_mod(h_ref[...], g, sc, sh)
        dx, dsh, dsc, dg = _norm_mod_bwd(du, r, xn, y, g, sc)
        dhp_ref[...] = dh_ref[...] + dx
        red_ref[0:1, :] += dsh
        red_ref[1:2, :] += dsc
        red_ref[3:4, :] += dg

    return pl.pallas_call(
        body, name=name, grid=(S // TS,),
        in_specs=[_row_tile(TS, D), _row_tile(TS, D), _const((8, D)), _row_tile(TS, 128), _row_tile(TS, 128),
                  _row_tile(TS, GW)] + [_res_spec(r) for r in DILATIONS] * 3 + [_row_tile(TS, 2 * D), ANY],
        out_specs=[_row_tile(TS, D), _row_tile(TS, IN_WIDTH), _const((8, D))],
        out_shape=[jax.ShapeDtypeStruct((S, D), F32), jax.ShapeDtypeStruct((S, IN_WIDTH), BF16),
                   jax.ShapeDtypeStruct((8, D), F32)],
        scratch_shapes=[pltpu.VMEM((IN_WIDTH, D), BF16), pltpu.SemaphoreType.DMA((1,)), pltpu.VMEM(RES_SCRATCH, F32)],
        compiler_params=_params(),
    )(dh, h, vec, cos, sin, dp, *dqkv, dgl, win)


def _pool_lanes(rows):
    lane = lax.broadcasted_iota(jnp.int32, (rows, GW), 1)
    return lane // HEAD_DIM


def _pool_window(rows):
    grp = _pool_lanes(rows)
    w = jnp.full((rows, GW), POOL_WINDOWS[0], jnp.int32)
    for k in range(1, len(POOL_WINDOWS)):
        w = jnp.where(grp == k, POOL_WINDOWS[k], w)
    return grp, w


def _pool_fwd(p, wbd, scale, name, ts=512):
    S = p.shape[0]
    ext = ts + HALO

    def body(pc_ref, ph_ref, wbd_ref, sc_ref, d_ref, y_ref):
        i = pl.program_id(0)
        cur = pc_ref[...]
        halo = jnp.where(i > 0, ph_ref[...], 0.0)
        s = jnp.concatenate([halo, cur], axis=0)
        grp, w = _pool_window(ext)
        sel = jnp.zeros((ext, GW), F32)
        for k, wk in enumerate(POOL_WINDOWS):
            s = s + pltpu.roll(s, wk // 2, 0)
            sel = jnp.where(grp == k, s, sel)
        t = i * ts + lax.broadcasted_iota(jnp.int32, (ts, GW), 0)
        count = jnp.minimum(t + 1, w[HALO:]).astype(F32)
        d = (sel[HALO:] / count - cur).astype(BF16)
        d_ref[...] = d
        y_ref[...] = (_dot(d, wbd_ref[...]) * sc_ref[...]).astype(BF16)

    return pl.pallas_call(
        body, name=name, grid=(S // ts,),
        in_specs=[_row_tile(ts, GW),
                  pl.BlockSpec((HALO, GW), lambda i: (jnp.maximum(i * (ts // HALO) - 1, 0), 0)),
                  _const((GW, GW)), _const((1, GW))],
        out_specs=[_row_tile(ts, GW), _row_tile(ts, GW)],
        out_shape=[jax.ShapeDtypeStruct((S, GW), BF16), jax.ShapeDtypeStruct((S, GW), BF16)],
        compiler_params=_params(),
    )(p, p, wbd, scale)


def _pool_bwd(dy, d, wbd, scale, name, ts=512):
    S = dy.shape[0]
    ext = ts + HALO
    nsteps = S // ts
    last_halo = S // HALO - 1

    def body(dyc_ref, dyh_ref, d_ref, wbd_ref, sc_ref, dp_ref, dw_ref, ds_ref):
        i = pl.program_id(0)

        @pl.when(i == 0)
        def _():
            dw_ref[...] = jnp.zeros_like(dw_ref)
            ds_ref[...] = jnp.zeros_like(ds_ref)

        dyc = dyc_ref[...]
        dyh = jnp.where(i < nsteps - 1, dyh_ref[...], 0.0)
        dys = (jnp.concatenate([dyc, dyh], axis=0) * sc_ref[...]).astype(BF16)
        dd = _dot_nt(dys, wbd_ref[...])
        grp, w = _pool_window(ext)
        t = i * ts + lax.broadcasted_iota(jnp.int32, (ext, GW), 0)
        s = dd / jnp.minimum(t + 1, w).astype(F32)
        sel = jnp.zeros((ext, GW), F32)
        for k, wk in enumerate(POOL_WINDOWS):
            s = s + pltpu.roll(s, ext - wk // 2, 0)
            sel = jnp.where(grp == k, s, sel)
        dp_ref[...] = sel[:ts] - dd[:ts]
        dv = d_ref[...]
        z = _dot(dv, wbd_ref[...])
        ds_ref[0:1, :] += jnp.sum(dyc * z, axis=0, keepdims=True)
        dw_ref[...] += _dot_tn(dv, dys[:ts])

    return pl.pallas_call(
        body, name=name, grid=(nsteps,),
        in_specs=[_row_tile(ts, GW),
                  pl.BlockSpec((HALO, GW), lambda i: (jnp.minimum((i + 1) * (ts // HALO), last_halo), 0)),
                  _row_tile(ts, GW), _const((GW, GW)), _const((1, GW))],
        out_specs=[_row_tile(ts, GW), _const((GW, GW)), _const((8, GW))],
        out_shape=[jax.ShapeDtypeStruct((S, GW), F32), jax.ShapeDtypeStruct((GW, GW), F32),
                   jax.ShapeDtypeStruct((8, GW), F32)],
        compiler_params=_params(),
    )(dy, dy, d, wbd, scale)


def _head_id(rows):
    return lax.broadcasted_iota(jnp.int32, (rows, GW), 1) // HEAD_DIM


def _stack_heads(t, hid):
    return jnp.concatenate([jnp.where(hid == h, t, jnp.zeros_like(t)) for h in range(HEADS)], axis=0)


def _unstack_heads(t_all, hid):
    out = jnp.zeros((QB, GW), F32)
    for h in range(HEADS):
        out = jnp.where(hid == h, t_all[h * QB:(h + 1) * QB], out)
    return out


def _band_mask(n):
    row = lax.broadcasted_iota(jnp.int32, (HEADS * QB, 2 * QB), 0) % QB
    col = lax.broadcasted_iota(jnp.int32, (HEADS * QB, 2 * QB), 1)
    rel = row + QB - col
    return (rel >= 0) & (rel <= BAND) & ((col >= QB) | (n > 0))


MAX_STREAMS = 8


def _streams(r, nb):
    if r > 1:
        ns = min(r, MAX_STREAMS)
        return nb, [(lambda rb, l=l: ns * rb + l, 0) for l in range(ns)]
    ns = min(MAX_STREAMS, nb)
    return nb // ns, [(lambda rb: 0, l * (nb // ns)) for l in range(ns)]


def _attn_fwd(q, k, v, name):
    r, L, _ = q.shape
    nbs, streams = _streams(r, L // QB)
    ns = len(streams)
    grid = (max(r // ns, 1), nbs)

    def cur(res, off):
        return pl.BlockSpec((None, QB, GW), lambda rb, n: (res(rb), n + off, 0))

    def prev(res, off):
        return pl.BlockSpec((None, QB, GW), lambda rb, n: (res(rb), jnp.maximum(n + off - 1, 0), 0))

    def body(*refs):
        n = pl.program_id(1)
        hid = _head_id(QB)
        o_ref, lse_ref = refs[5 * len(streams):]
        for l, (_, off) in enumerate(streams):
            q_ref, kp_ref, kc_ref, vp_ref, vc_ref = refs[5 * l:5 * l + 5]
            qs = _stack_heads(q_ref[...], hid)
            kc = jnp.concatenate([kp_ref[...], kc_ref[...]], axis=0)
            vc = jnp.concatenate([vp_ref[...], vc_ref[...]], axis=0)
            s = _dot_nt(qs, kc) * (HEAD_DIM ** -0.5)
            s = jnp.where(_band_mask(n + off), s, -jnp.inf)
            m = jnp.max(s, axis=-1, keepdims=True)
            e = jnp.exp(s - m)
            den = jnp.sum(e, axis=-1, keepdims=True)
            lse = m + jnp.log(den)
            pr = (e * (1.0 / den)).astype(BF16)
            o_ref[l] = _unstack_heads(_dot(pr, vc), hid).astype(BF16)
            lse_ref[l] = _unstack_heads(jnp.broadcast_to(lse, (HEADS * QB, GW)), hid)

    in_specs, args = [], []
    for res, off in streams:
        in_specs += [cur(res, off), prev(res, off), cur(res, off), prev(res, off), cur(res, off)]
        args += [q, k, k, v, v]
    out = jax.ShapeDtypeStruct((ns * grid[0], nbs * QB, GW), F32)
    both = pl.BlockSpec((ns, QB, GW), lambda rb, n: (rb, n, 0))
    o, lse = pl.pallas_call(
        body, name=name, grid=grid, in_specs=in_specs, out_specs=[both, both],
        out_shape=[jax.ShapeDtypeStruct(out.shape, BF16), out],
        compiler_params=_params(),
    )(*args)
    return o.reshape(q.shape), lse.reshape(q.shape)


def _head_rows(t_full, hid):
    return jnp.concatenate(
        [jnp.max(jnp.where(hid == h, t_full, -jnp.inf), axis=-1, keepdims=True) for h in range(HEADS)], axis=0)


def _attn_bwd(q, k, v, do, lse, cterm, name):
    r, L, _ = q.shape
    nbs, streams = _streams(r, L // QB)
    ns = len(streams)
    parts = r == 1

    def spec(res, index):
        return pl.BlockSpec((None, QB, GW), lambda rb, n: (res(rb), index(n), 0))

    def body(*refs):
        dq_ref, dk_ref, dv_ref, carry_k, carry_v, seam_k, seam_v = refs[8 * ns:]
        n = pl.program_id(1)

        @pl.when(n == 0)
        def _():
            carry_k[...] = jnp.zeros_like(carry_k)
            carry_v[...] = jnp.zeros_like(carry_v)

        @pl.when(n < nbs)
        def _():
            hid = _head_id(QB)
            for l, (_, off) in enumerate(streams):
                q_ref, do_ref, lse_ref, c_ref, kp_ref, kc_ref, vp_ref, vc_ref = refs[8 * l:8 * l + 8]
                qs = _stack_heads(q_ref[...], hid)
                dos = _stack_heads(do_ref[...], hid)
                kc = jnp.concatenate([kp_ref[...], kc_ref[...]], axis=0)
                vc = jnp.concatenate([vp_ref[...], vc_ref[...]], axis=0)
                s = _dot_nt(qs, kc) * (HEAD_DIM ** -0.5)
                s = jnp.where(_band_mask(n + off), s, -jnp.inf)
                p = jnp.exp(s - _head_rows(lse_ref[...], hid))
                dp = _dot_nt(dos, vc)
                ds = (p * (dp + _head_rows(c_ref[...], hid)) * (HEAD_DIM ** -0.5)).astype(BF16)
                dq_ref[l] = _unstack_heads(_dot(ds, kc), hid).astype(BF16)
                dkc = _dot_tn(ds, qs)
                dvc = _dot_tn(p.astype(BF16), dos)
                if parts and l > 0:
                    @pl.when(n == 0)
                    def _():
                        seam_k[l] = dkc[:QB]
                        seam_v[l] = dvc[:QB]
                dk_ref[l] = (carry_k[l] + dkc[:QB]).astype(BF16)
                dv_ref[l] = (carry_v[l] + dvc[:QB]).astype(BF16)
                carry_k[l] = dkc[QB:]
                carry_v[l] = dvc[QB:]

        @pl.when(n == nbs)
        def _():
            for l in range(ns):
                if parts and l + 1 < ns:
                    dk_ref[l] = (carry_k[l] + seam_k[l + 1]).astype(BF16)
                    dv_ref[l] = (carry_v[l] + seam_v[l + 1]).astype(BF16)
                else:
                    dk_ref[l] = carry_k[l].astype(BF16)
                    dv_ref[l] = carry_v[l].astype(BF16)

    in_specs, args = [], []
    for res, off in streams:
        qside = functools.partial(lambda n, off: jnp.minimum(n, nbs - 1) + off, off=off)
        kprev = functools.partial(lambda n, off: jnp.maximum(jnp.minimum(n, nbs) - 1 + off, 0), off=off)
        in_specs += [spec(res, qside)] * 4 + [spec(res, kprev), spec(res, qside)] * 2
        args += [q, do, lse, cterm, k, k, v, v]
    out = jax.ShapeDtypeStruct((ns * max(r // ns, 1), nbs * QB, GW), BF16)
    qout = pl.BlockSpec((ns, QB, GW), lambda rb, n: (rb, jnp.minimum(n, nbs - 1), 0))
    kout = pl.BlockSpec((ns, QB, GW), lambda rb, n: (rb, jnp.maximum(n - 1, 0), 0))
    buf = pltpu.VMEM((ns, QB, GW), F32)
    outs = pl.pallas_call(
        body, name=name, grid=(max(r // ns, 1), nbs + 1),
        in_specs=in_specs, out_specs=[qout, kout, kout], out_shape=[out, out, out],
        scratch_shapes=[buf, buf, buf, buf],
        compiler_params=_params(),
    )(*args)
    return [t.reshape(q.shape) for t in outs]


def _token_order(refs, scr):
    return [_from_residues(ref, slice(0, GW), r, scr) for ref, r in zip(refs, DILATIONS)]


def _group_weights(lses):
    l0, l1, l2 = lses
    m = jnp.maximum(jnp.maximum(l0, l1), l2)
    e = [jnp.exp(l - m) for l in (l0, l1, l2)]
    den = e[0] + e[1] + e[2]
    return [ei / den for ei in e]


def _mix_out_fwd(h, vec, gates, ypool, o3, lse3, wpb, wab, wout, name):
    S, D = h.shape

    def body(h_ref, vec_ref, gates_ref, yp_ref, o0, o1, o2, l0, l1, l2, wpb_hbm, wab_hbm, wout_hbm,
             hn_ref, ya_ref, merged_ref, tm_ref, wpb_v, wab_v, wout_v, sem, scr):
        _load_weights([(wpb_hbm, wpb_v), (wab_hbm, wab_v), (wout_hbm, wout_v)], sem)
        gt = vec_ref[3:4, :]
        wts = _group_weights(_token_order((l0, l1, l2), scr))
        og = _token_order((o0, o1, o2), scr)
        ya = (wts[0] * og[0] + wts[1] * og[1] + wts[2] * og[2]).astype(BF16)
        ya_ref[...] = ya
        merged = (gates_ref[:, :D].astype(F32) * _dot(yp_ref[...], wpb_v[...])
                  + gates_ref[:, D:].astype(F32) * _dot(ya, wab_v[...])).astype(BF16)
        merged_ref[...] = merged
        tm = _dot(merged, wout_v[...])
        tm_ref[...] = tm.astype(BF16)
        hn_ref[...] = h_ref[...] + gt * tm

    grp = _row_tile(TS, GW)
    res = [_res_spec(r) for r in DILATIONS]
    return pl.pallas_call(
        body, name=name, grid=(S // TS,),
        in_specs=[_row_tile(TS, D), _const((8, D)), _row_tile(TS, 2 * D), grp] + res * 2 + [ANY, ANY, ANY],
        out_specs=[_row_tile(TS, D), grp, _row_tile(TS, D), _row_tile(TS, D)],
        out_shape=[jax.ShapeDtypeStruct((S, D), F32), jax.ShapeDtypeStruct((S, GW), BF16),
                   jax.ShapeDtypeStruct((S, D), BF16), jax.ShapeDtypeStruct((S, D), BF16)],
        scratch_shapes=[pltpu.VMEM((GW, D), BF16), pltpu.VMEM((GW, D), BF16), pltpu.VMEM((D, D), BF16),
                        pltpu.SemaphoreType.DMA((3,)), pltpu.VMEM(RES_SCRATCH, F32)],
        compiler_params=_params(),
    )(h, vec, gates, ypool, *o3, *lse3, wpb, wab, wout)


def _mix_out_bwd(dh, tm, vec, gates, ypool, o3, lse3, wpb, wab, wout, name, rider=None):
    S, D = dh.shape

    def body(*refs):
        if rider is None:
            return compute(*refs)
        host, mine = rider.split(refs, 14, 12)
        rider.head(mine, pl.program_id(0))
        compute(*host)
        rider.tail(mine, pl.program_id(0), S // TS)

    def compute(dh_ref, tm_ref, vec_ref, gates_ref, yp_ref, o0, o1, o2, l0, l1, l2, wpb_hbm, wab_hbm, wout_hbm,
                dtm_ref, dgl_ref, dypb_ref, dyab_ref, dyp_ref, do0, do1, do2, c0, c1, c2, red_ref,
                wpb_v, wab_v, wout_v, sem, scr):
        _load_weights([(wpb_hbm, wpb_v), (wab_hbm, wab_v), (wout_hbm, wout_v)], sem)

        @pl.when(pl.program_id(0) == 0)
        def _():
            red_ref[...] = jnp.zeros_like(red_ref)

        gt = vec_ref[3:4, :]
        dh_v = dh_ref[...]
        red_ref[2:3, :] += jnp.sum(tm_ref[...].astype(F32) * dh_v, axis=0, keepdims=True)
        dtm = (gt * dh_v).astype(BF16)
        dtm_ref[...] = dtm
        dm = _dot_nt(dtm, wout_v[...])
        wts = _group_weights(_token_order((l0, l1, l2), scr))
        og = _token_order((o0, o1, o2), scr)
        ya = wts[0] * og[0] + wts[1] * og[1] + wts[2] * og[2]
        ypb = _dot(yp_ref[...], wpb_v[...])
        yab = _dot(ya.astype(BF16), wab_v[...])
        gp = gates_ref[:, :D].astype(F32)
        ga = gates_ref[:, D:].astype(F32)
        dgl_ref[:, :D] = (dm * ypb * gp * (1.0 - gp)).astype(BF16)
        dgl_ref[:, D:] = (dm * yab * ga * (1.0 - ga)).astype(BF16)
        dypb = (dm * gp).astype(BF16)
        dyab = (dm * ga).astype(BF16)
        dypb_ref[...] = dypb
        dyab_ref[...] = dyab
        dyp_ref[...] = _dot_nt(dypb, wpb_v[...])
        dya = _dot_nt(dyab, wab_v[...])
        row = lax.broadcasted_iota(jnp.int32, (GW, GW), 0) // HEAD_DIM
        col = lax.broadcasted_iota(jnp.int32, (GW, GW), 1) // HEAD_DIM
        ones = jnp.where(row == col, 1.0, 0.0).astype(F32)
        tot = jnp.dot(dya * ya, ones, preferred_element_type=F32, precision=lax.Precision.HIGHEST)
        for wg, do_ref, c_ref, r in zip(wts, (do0, do1, do2), (c0, c1, c2), DILATIONS):
            _to_residues(wg * dya, do_ref, slice(0, GW), r, scr)
            _to_residues(-(wg * tot), c_ref, slice(0, GW), r, scr)

    grp = _row_tile(TS, GW)
    res = [_res_spec(r) for r in DILATIONS]
    specs = (
        [_row_tile(TS, D), _row_tile(TS, D), _const((8, D)), _row_tile(TS, 2 * D), grp] + res * 2
        + [ANY, ANY, ANY],
        [_row_tile(TS, D), _row_tile(TS, 2 * D), _row_tile(TS, D), _row_tile(TS, D), grp]
        + res * 2 + [_const((8, D))],
        [jax.ShapeDtypeStruct((S, D), BF16), jax.ShapeDtypeStruct((S, 2 * D), BF16),
         jax.ShapeDtypeStruct((S, D), BF16), jax.ShapeDtypeStruct((S, D), BF16), jax.ShapeDtypeStruct((S, GW), F32)]
        + [_res_shape(S, r, BF16) for r in DILATIONS] + [_res_shape(S, r, F32) for r in DILATIONS]
        + [jax.ShapeDtypeStruct((8, D), F32)],
        [pltpu.VMEM((GW, D), BF16), pltpu.VMEM((GW, D), BF16), pltpu.VMEM((D, D), BF16),
         pltpu.SemaphoreType.DMA((3,)), pltpu.VMEM(RES_SCRATCH, F32)])
    in_specs, out_specs, out_shape, scratch = specs if rider is None else rider.specs(*specs)
    outs = pl.pallas_call(
        body, name=name, grid=(S // TS,), in_specs=in_specs, out_specs=out_specs, out_shape=out_shape,
        scratch_shapes=scratch, compiler_params=_params(),
    )(dh, tm, vec, gates, ypool, *o3, *lse3, wpb, wab, wout, *(rider.arrays if rider else []))
    return outs if rider is None else (outs[:12], outs[12:])


def _ada_mod(c_all, w, b, name):
    def body(c_ref, w_ref, b_ref, cond_ref, mod_ref):
        cv = c_ref[...]
        cond = cv * jax.nn.sigmoid(cv)
        cond_ref[...] = cond
        mod_ref[...] = jnp.dot(cond, w_ref[...], preferred_element_type=F32,
                               precision=lax.Precision.HIGHEST) + b_ref[...]

    return pl.pallas_call(
        body, name=name,
        out_shape=[jax.ShapeDtypeStruct(c_all.shape, F32), jax.ShapeDtypeStruct((c_all.shape[0], w.shape[1]), F32)],
        compiler_params=_params(),
    )(c_all, w, b)


def _adamw_math(w, g, m, v):
    m = ADAM_B1 * m + (1.0 - ADAM_B1) * g
    v = ADAM_B2 * v + (1.0 - ADAM_B2) * (g * g)
    m_hat = m / (1.0 - ADAM_B1 ** ADAM_STEP)
    v_hat = v / (1.0 - ADAM_B2 ** ADAM_STEP)
    delta = -ADAM_LR * (m_hat / (jnp.sqrt(v_hat) + ADAM_EPS) + ADAM_WD * w)
    return delta, m, v


def _adamw_many(ws, gs, ms, vs, name):
    n = len(ws)

    def body(*refs):
        for k in range(n):
            w_ref, g_ref, m_ref, v_ref = (refs[j * n + k] for j in range(4))
            d_ref, mo_ref, vo_ref = (refs[(4 + j) * n + k] for j in range(3))
            d_ref[...], mo_ref[...], vo_ref[...] = _adamw_math(w_ref[...], g_ref[...], m_ref[...], v_ref[...])

    outs = pl.pallas_call(
        body, name=name, out_shape=[jax.ShapeDtypeStruct(t.shape, F32) for t in ws] * 3,
        compiler_params=_params(),
    )(*ws, *gs, *ms, *vs)
    return outs[:n], outs[n:2 * n], outs[2 * n:]


def _ada_grad_adamw(cond_t, dmod, w, m, v, name, tr=256):
    R, C = w.shape
    nb = dmod.shape[0]

    def body(ct_ref, dm_ref, w_ref, m_ref, v_ref, g_ref, d_ref, mo_ref, vo_ref):
        ct = ct_ref[...]
        dm = dm_ref[...]
        g = jnp.zeros((tr, C), F32)
        for bi in range(nb):
            g = g + ct[:, bi:bi + 1] * dm[bi:bi + 1, :]
        g_ref[...] = g
        d_ref[...], mo_ref[...], vo_ref[...] = _adamw_math(w_ref[...], g, m_ref[...], v_ref[...])

    spec = _row_tile(tr, C)
    out = jax.ShapeDtypeStruct((R, C), F32)
    return pl.pallas_call(
        body, name=name, grid=(R // tr,),
        in_specs=[_row_tile(tr, nb), _const((nb, C)), spec, spec, spec],
        out_specs=[spec] * 4, out_shape=[out] * 4,
        compiler_params=_params(),
    )(cond_t, dmod, w, m, v)


def _row_step(rows, cap=256):
    for cand in range(cap, 15, -16):
        if rows % cand == 0:
            return cand
    return rows


def _slot_sum(x_ref):
    acc = x_ref[0].astype(F32)
    for k in range(1, x_ref.shape[0]):
        acc = acc + x_ref[k].astype(F32)
    return acc


def _sum_slots(x, name, out_dtype=F32):
    n, R, C = x.shape
    tr = _row_step(R)

    def body(x_ref, o_ref):
        o_ref[...] = _slot_sum(x_ref).astype(out_dtype)

    return pl.pallas_call(
        body, name=name, grid=(R // tr,),
        in_specs=[pl.BlockSpec((n, tr, C), lambda i: (0, i, 0))],
        out_specs=_row_tile(tr, C), out_shape=jax.ShapeDtypeStruct((R, C), out_dtype),
        compiler_params=_params(),
    )(x)


def _sum_pair(core, g, recv, name):
    _, _, R, C = g.shape
    tr = _row_step(R, cap=1024)

    def body(core_ref, g_ref, r_ref, o_ref):
        o_ref[...] = (g_ref[...].astype(F32) + r_ref[...].astype(F32)).astype(BF16)

    return pl.pallas_call(
        body, name=name, out_shape=jax.ShapeDtypeStruct((4, R, C), BF16),
        grid_spec=pltpu.PrefetchScalarGridSpec(
            num_scalar_prefetch=1, grid=(4, R // tr),
            in_specs=[pl.BlockSpec((None, None, tr, C), lambda k, i, core_ref: (k, core_ref[0], i, 0)),
                      pl.BlockSpec((None, tr, C), lambda k, i, core_ref: (k, i, 0))],
            out_specs=pl.BlockSpec((None, tr, C), lambda k, i, core_ref: (k, i, 0))),
        compiler_params=_params(),
    )(core, g, recv)


def _sum_adamw(chip, own, recv, w, m, v, name):
    _, R, C = own.shape
    tr = _row_step(R, cap=512)

    def body(chip_ref, own_ref, r_ref, w_ref, m_ref, v_ref, g_ref, d_ref, mo_ref, vo_ref):
        g = own_ref[...].astype(F32) + _slot_sum(r_ref)
        g_ref[...] = g
        d_ref[...], mo_ref[...], vo_ref[...] = _adamw_math(w_ref[...], g, m_ref[...], v_ref[...])

    spec = pl.BlockSpec((tr, C), lambda i, chip_ref: (i, 0))
    out = jax.ShapeDtypeStruct((R, C), F32)
    return pl.pallas_call(
        body, name=name, out_shape=[out] * 4,
        grid_spec=pltpu.PrefetchScalarGridSpec(
            num_scalar_prefetch=1, grid=(R // tr,),
            in_specs=[pl.BlockSpec((None, tr, C), lambda i, chip_ref: (chip_ref[0], i, 0)),
                      pl.BlockSpec((3, tr, C), lambda i, chip_ref: (0, i, 0)), spec, spec, spec],
            out_specs=[spec] * 4),
        compiler_params=_params(),
    )(chip, own, recv, w, m, v)


def _place():
    return lax.axis_index("x"), lax.axis_index("y"), lax.axis_index("c")


def _gather_steps(x_refs, out_refs, send_sems, recv_sems):
    n = len(x_refs)
    x, y, c = _place()
    me, sibling = (x, y, c), (x, y, 1 - c)
    chips = [(1 - x, y), (x, 1 - y), (1 - x, 1 - y)]

    def rows(a, px, py, pc):
        return out_refs[a].at[4 * px + 2 * py + pc]

    def copy(a, k, block, to, src=None):
        return pltpu.make_async_remote_copy(
            src_ref=rows(a, *block) if src is None else src, dst_ref=rows(a, *block),
            send_sem=send_sems.at[a, k], recv_sem=recv_sems.at[a, k], device_id=to, device_id_type=MESH)

    def first(a):
        return [copy(a, 0, me, sibling, src=x_refs[a])] + [
            copy(a, 1 + j, me, (*chip, c), src=x_refs[a]) for j, chip in enumerate(chips)]

    def passed(a, j):
        return copy(a, 4 + j, (*chips[j], c), sibling)

    def start():
        for a in range(n):
            for cp in first(a):
                cp.start()

    def relay():
        for j, chip in enumerate(chips):
            for a in range(n):
                copy(a, 1 + j, (*chip, c), me).wait_recv()
                passed(a, j).start()

    def finish():
        for a in range(n):
            copy(a, 0, sibling, me).wait_recv()
            for j, chip in enumerate(chips):
                copy(a, 4 + j, (*chip, 1 - c), me).wait_recv()
        for a in range(n):
            for cp in first(a) + [passed(a, j) for j in range(3)]:
                cp.wait_send()

    return start, relay, finish


def _all_gather_tree(arrs, name, gather=()):
    n, extra = len(arrs), len(gather)

    def body(*refs):
        x_refs, out_refs = refs[:n], refs[n + extra:2 * n + extra]
        sems = refs[2 * (n + extra):]
        send_sems, recv_sems = sems[:2]
        if extra:
            g_start, g_relay, g_finish = _small_gather_steps(
                refs[n:n + extra], refs[2 * n + extra:2 * (n + extra)], *sems[2:])
            g_start()
        x, y, c = _place()
        me, sibling = (x, y, c), (x, y, 1 - c)
        xn, yn, dg = (1 - x, y), (x, 1 - y), (1 - x, 1 - y)

        def rows(a, px, py, pc):
            return out_refs[a].at[4 * px + 2 * py + pc]

        def copy(a, k, block, to, src=None):
            return pltpu.make_async_remote_copy(
                src_ref=rows(a, *block) if src is None else src, dst_ref=rows(a, *block),
                send_sem=send_sems.at[a, k], recv_sem=recv_sems.at[a, k], device_id=to, device_id_type=MESH)

        def own(a):
            return [copy(a, 0, me, sibling, src=x_refs[a]), copy(a, 1, me, (*xn, c), src=x_refs[a]),
                    copy(a, 2, me, (*yn, c), src=x_refs[a])]

        def north_hands_on(a):
            return copy(a, 3, (*xn, c), (*yn, c))

        def south_hands_on(a):
            return copy(a, 3, (*yn, c), (*xn, c))

        def to_sibling(a):
            return [copy(a, 4, (*xn, c), sibling), copy(a, 5, (*yn, c), sibling), copy(a, 6, (*dg, c), sibling)]

        for a in range(n):
            for cp in own(a):
                cp.start()
        for a in range(n):
            copy(a, 1, (*xn, c), me).wait_recv()
            to_sibling(a)[0].start()

        @pl.when(c == 1)
        def _():
            for a in range(n):
                north_hands_on(a).start()

        for a in range(n):
            copy(a, 2, (*yn, c), me).wait_recv()
            to_sibling(a)[1].start()

        @pl.when(c == 0)
        def _():
            for a in range(n):
                south_hands_on(a).start()

        for a in range(n):
            copy(a, 3, (*dg, c), me).wait_recv()
            to_sibling(a)[2].start()
        for a in range(n):
            copy(a, 0, sibling, me).wait_recv()
            copy(a, 4, (*xn, 1 - c), me).wait_recv()
            copy(a, 5, (*yn, 1 - c), me).wait_recv()
            copy(a, 6, (*dg, 1 - c), me).wait_recv()
        for a in range(n):
            for cp in own(a) + to_sibling(a):
                cp.wait_send()

        @pl.when(c == 1)
        def _():
            for a in range(n):
                north_hands_on(a).wait_send()

        @pl.when(c == 0)
        def _():
            for a in range(n):
                south_hands_on(a).wait_send()

        if extra:
            g_relay()
            g_finish()

    return pl.pallas_call(
        body, name=name,
        out_shape=[jax.ShapeDtypeStruct((N_DEV,) + t.shape, t.dtype) for t in list(arrs) + list(gather)],
        in_specs=[ANY] * (n + extra), out_specs=[ANY] * (n + extra),
        scratch_shapes=[pltpu.SemaphoreType.DMA((n, 7)), pltpu.SemaphoreType.DMA((n, 7))]
        + (_small_gather_scratch(extra) if extra else []),
    )(*arrs, *gather)


def _all_gather(arrs, name, own=True):
    n = len(arrs)

    def body(*refs):
        x_refs, out_refs = refs[:n], refs[n:2 * n]
        send_sems, recv_sems, local_sems = refs[2 * n:]
        me = 4 * lax.axis_index("x") + 2 * lax.axis_index("y") + lax.axis_index("c")
        mine = [pltpu.make_async_copy(x_refs[a], out_refs[a].at[me], local_sems.at[a]) for a in range(n)] if own else []
        for cp in mine:
            cp.start()
        for step in _gather_steps(x_refs, out_refs, send_sems, recv_sems):
            step()
        for cp in mine:
            cp.wait()

    return pl.pallas_call(
        body, name=name, out_shape=[jax.ShapeDtypeStruct((N_DEV,) + t.shape, t.dtype) for t in arrs],
        in_specs=[ANY] * n, out_specs=[ANY] * n,
        scratch_shapes=[pltpu.SemaphoreType.DMA((n, 7)), pltpu.SemaphoreType.DMA((n, 7)),
                        pltpu.SemaphoreType.DMA((n,))],
    )(*arrs)


def _pair_exchange_steps(g_refs, out_refs, send_sems, recv_sems):
    x, y, c = _place()

    def give():
        return [pltpu.make_async_remote_copy(
            src_ref=g_refs[a].at[pl.ds(0, 4), 1 - c], dst_ref=out_refs[a], send_sem=send_sems.at[a],
            recv_sem=recv_sems.at[a], device_id=(x, y, 1 - c), device_id_type=MESH) for a in range(len(g_refs))]

    def start():
        for cp in give():
            cp.start()

    def finish():
        for cp in give():
            cp.wait()

    return start, finish


def _pair_exchange(arrs, name):
    n = len(arrs)

    def body(*refs):
        for step in _pair_exchange_steps(refs[:n], refs[n:2 * n], *refs[2 * n:]):
            step()

    return pl.pallas_call(
        body, name=name,
        out_shape=[jax.ShapeDtypeStruct((4,) + t.shape[2:], t.dtype) for t in arrs],
        in_specs=[ANY] * n, out_specs=[ANY] * n,
        scratch_shapes=[pltpu.SemaphoreType.DMA((n,)), pltpu.SemaphoreType.DMA((n,))],
    )(*arrs)


def _chip_exchange_steps(p_refs, out_refs, send_sems, recv_sems):
    x, y, c = _place()
    chips = [(1 - x, y), (x, 1 - y), (1 - x, 1 - y)]

    def copies():
        return [pltpu.make_async_remote_copy(
            src_ref=p_refs[a].at[2 * px + py], dst_ref=out_refs[a].at[j], send_sem=send_sems.at[a, j],
            recv_sem=recv_sems.at[a, j], device_id=(px, py, c), device_id_type=MESH)
            for a in range(len(p_refs)) for j, (px, py) in enumerate(chips)]

    def start():
        for cp in copies():
            cp.start()

    def finish():
        for cp in copies():
            cp.wait()

    return start, finish


def _small_gather_steps(x_refs, out_refs, send_sems, recv_sems, local_sems):
    me = 4 * lax.axis_index("x") + 2 * lax.axis_index("y") + lax.axis_index("c")
    start, relay, finish = _gather_steps(x_refs, out_refs, send_sems, recv_sems)

    def mine():
        return [pltpu.make_async_copy(x_refs[a], out_refs[a].at[me], local_sems.at[a]) for a in range(len(x_refs))]

    def start_all():
        for cp in mine():
            cp.start()
        start()

    def finish_all():
        finish()
        for cp in mine():
            cp.wait()

    return start_all, relay, finish_all


def _small_gather_scratch(k):
    return [pltpu.SemaphoreType.DMA((k, 7)), pltpu.SemaphoreType.DMA((k, 7)), pltpu.SemaphoreType.DMA((k,))]


def _chip_exchange(arrs, name, gather=()):
    n, k = len(arrs), len(gather)

    def body(*refs):
        ins, outs, sems = refs[:n + k], refs[n + k:2 * (n + k)], refs[2 * (n + k):]
        start, finish = _chip_exchange_steps(ins[:n], outs[:n], *sems[:2])
        if k:
            g_start, g_relay, g_finish = _small_gather_steps(ins[n:], outs[n:], *sems[2:])
            g_start()
        start()
        if k:
            g_relay()
        finish()
        if k:
            g_finish()

    return pl.pallas_call(
        body, name=name,
        out_shape=[jax.ShapeDtypeStruct((3,) + t.shape[1:], t.dtype) for t in arrs]
        + [jax.ShapeDtypeStruct((N_DEV,) + t.shape, t.dtype) for t in gather],
        in_specs=[ANY] * (n + k), out_specs=[ANY] * (n + k),
        scratch_shapes=[pltpu.SemaphoreType.DMA((n, 3)), pltpu.SemaphoreType.DMA((n, 3))]
        + (_small_gather_scratch(k) if k else []),
    )(*arrs, *gather)


class _Rider:
    def __init__(self, arrays, out_shape, sems, steps, relay_before_end=None):
        self.arrays, self.out_shape, self.scratch, self.steps = list(arrays), out_shape, sems, steps
        self.n = len(self.arrays)
        self.relay_before_end = relay_before_end

    def specs(self, in_specs, out_specs, out_shape, scratch):
        extra = [ANY] * self.n
        return in_specs + extra, out_specs + extra, out_shape + self.out_shape, scratch + self.scratch

    def split(self, refs, n_in, n_out):
        k = self.n
        a, b = n_in + k, n_in + k + n_out
        return refs[:n_in] + refs[a:b] + refs[b + k:-2], (refs[n_in:a], refs[b:b + k], refs[-2:])

    def head(self, mine, step):
        pl.when(step == 0)(self.steps(mine[0], mine[1], *mine[2])[0])

    def tail(self, mine, step, nsteps):
        steps = self.steps(mine[0], mine[1], *mine[2])
        if self.relay_before_end is not None:
            pl.when(step == nsteps - 1 - self.relay_before_end)(steps[1])
        pl.when(step == nsteps - 1)(steps[-1])


def _gather_rider(arrs, relay_before_end=4):
    n = len(arrs)
    return _Rider(arrs, [jax.ShapeDtypeStruct((N_DEV,) + t.shape, t.dtype) for t in arrs],
                  [pltpu.SemaphoreType.DMA((n, 7)), pltpu.SemaphoreType.DMA((n, 7))], _gather_steps,
                  relay_before_end)


def _pair_exchange_rider(arrs):
    n = len(arrs)
    return _Rider(arrs, [jax.ShapeDtypeStruct((4,) + t.shape[2:], t.dtype) for t in arrs],
                  [pltpu.SemaphoreType.DMA((n,)), pltpu.SemaphoreType.DMA((n,))], _pair_exchange_steps)


def _chip_exchange_rider(arrs):
    n = len(arrs)
    return _Rider(arrs, [jax.ShapeDtypeStruct((3,) + t.shape[1:], t.dtype) for t in arrs],
                  [pltpu.SemaphoreType.DMA((n, 3)), pltpu.SemaphoreType.DMA((n, 3))], _chip_exchange_steps)


def _rope_tables(positions):
    inv_freq = ROPE_THETA ** (-jnp.arange(0, HEAD_DIM, 2, dtype=F32) / HEAD_DIM)
    ang = positions.astype(F32)[:, None] * inv_freq
    cos, sin = jnp.cos(ang), jnp.sin(ang)
    return jnp.tile(cos, (1, 4)), jnp.tile(jnp.concatenate([-sin, sin], axis=1), (1, 2))


class _GradReducer:
    def __init__(self):
        self.core = lax.axis_index("c").astype(jnp.int32).reshape(1)
        self.own, self.others, self.waiting, self.riding = {}, {}, [], []

    def pair(self, named):
        mine = self._split(named)
        self._summed(mine, _pair_exchange(list(mine.values()), "reduce_pair_" + next(iter(named))))

    def pair_rider(self, named):
        self.pairing = self._split(named)
        return _pair_exchange_rider(list(self.pairing.values()))

    def pair_landed(self, results):
        self._summed(self.pairing, results)

    @staticmethod
    def _split(named):
        return {k: g.reshape((4, 2) + g.shape[1:]) for k, g in named.items()}

    def _summed(self, mine, theirs):
        for (k, g), r in zip(mine.items(), theirs):
            self.own[k] = _sum_pair(self.core, g, r, "sum_pair_" + k)
        self.waiting += list(mine)

    def rider(self):
        self.riding, self.waiting = self.waiting, []
        return _chip_exchange_rider([self.own[k] for k in self.riding])

    def landed(self, results):
        self.others.update(zip(self.riding, results))

    def flush(self, name, gather=()):
        keys, self.waiting = self.waiting, []
        res = _chip_exchange([self.own[k] for k in keys], name, gather=gather)
        self.others.update(zip(keys, res))
        return res[len(keys):]


def _by_owner(g):
    if g.ndim == 3:
        return g if g.shape[0] == N_DEV else g.reshape(N_DEV, g.shape[1] * g.shape[0] // N_DEV, g.shape[2])
    return g.reshape(N_DEV, g.shape[0] // N_DEV, g.shape[1])


def _local_step(x, target, positions, mod, small, W, late=None, red=None):
    S, D = x.shape
    gains =jnp.stack([small["g1"], small["g2"], small["g3"]])[:, None, :]
    v1, v2, v3 = jnp.pad(jnp.concatenate([gains, mod.reshape(3, 3, D)], axis=1), ((0, 0), (0, 4), (0, 0)))
    vf = jnp.pad(small["gf"][None], ((0, 7), (0, 0)))
    cos, sin = _rope_tables(positions)
    wbd = jax.scipy.linalg.block_diag(*[small["w_pool"][k] for k in range(4)]).astype(BF16)
    pscale = small["pool_scale"].reshape(1, GW)

    if late is None:
        h1, u1, ab1, act1, f1 = _ffn_fwd(x, v1, W["w1in"], W["w1out"], "ffn1_fwd")
    else:
        (h1, u1, ab1, act1, f1), landed = _ffn_fwd(x, v1, W["w1in"], W["w1out"], "ffn1_fwd", rider=late[0])
        W = {**W, **late[1](landed)}
    u2, p, gates, *qkv = _mix_in_fwd(h1, v2, cos, sin, W["win"], "mix_in_fwd")
    dpool, ypool = _pool_fwd(p, wbd, pscale, "pool_fwd")
    o3, lse3 = [], []
    for gi in range(len(DILATIONS)):
        o, lse = _attn_fwd(qkv[gi], qkv[3 + gi], qkv[6 + gi], f"attn_fwd_{gi}")
        o3.append(o)
        lse3.append(lse)
    h2, ya, merged, tm = _mix_out_fwd(h1, v2, gates, ypool, o3, lse3, W["wpb"], W["wab"], W["wout"], "mix_out_fwd")
    dh3, u3, ab3, act3, f3, loss_blk, dgf = _ffn_fwd(h2, v3, W["w2in"], W["w2out"], "ffn2_fwd", final=(target, vf))

    dh2, dab3, df3, red3 = _ffn_bwd(dh3, h2, f3, ab3, v3, W["w2in"], W["w2out"], "ffn2_bwd")
    half_f = ab3.shape[2] // 2
    G = {"w2in": _by_owner(_wgrad(dab3, u3, "wgrad_2in", tm=half_f))}
    mix_out_args = (dh2, tm, v2, gates, ypool, o3, lse3, W["wpb"], W["wab"], W["wout"], "mix_out_bwd")
    if red is None:
        G["w2out"] = _by_owner(_wgrad(act3, df3, "wgrad_2out", tm=half_f))
        mix_out = _mix_out_bwd(*mix_out_args)
    else:
        g2out, landed = _wgrad(act3, df3, "wgrad_2out", tm=half_f, rider=red.pair_rider({"w2in": G["w2in"]}))
        red.pair_landed(landed)
        G["w2out"] = _by_owner(g2out)
        red.pair({"w2out": G["w2out"]})
        mix_out, landed = _mix_out_bwd(*mix_out_args, rider=red.rider())
        red.landed(landed)
    (dtm, dgl, dypb, dyab, dyp, do0, do1, do2, c0, c1, c2, red2o) = mix_out
    dq3, dk3, dv3 = [], [], []
    for gi, (do, ct) in enumerate(zip((do0, do1, do2), (c0, c1, c2))):
        dq, dk, dv = _attn_bwd(qkv[gi], qkv[3 + gi], qkv[6 + gi], do, lse3[gi], ct, f"attn_bwd_{gi}")
        dq3.append(dq)
        dk3.append(dk)
        dv3.append(dv)
    dp, dwbd, dps = _pool_bwd(dyp, dpool, wbd, pscale, "pool_bwd")
    dh1, dproj, red2i = _mix_in_bwd(dh2, h1, v2, cos, sin, dp, dq3 + dk3 + dv3, dgl, W["win"], "mix_in_bwd")
    G["win"] = _by_owner(_wgrad(dproj, u2, "wgrad_in", tm=1152))
    G["wpb"] = _full_to_cols(_wgrad(ypool, dypb, "wgrad_pb"))
    G["wab"] = _full_to_cols(_wgrad(ya, dyab, "wgrad_ab"))
    if red is None:
        G["wout"] = _by_owner(_wgrad(merged, dtm, "wgrad_out"))
    else:
        gout, landed = _wgrad(merged, dtm, "wgrad_out", rider=red.pair_rider({k: G[k] for k in ("win", "wpb", "wab")}))
        red.pair_landed(landed)
        G["wout"] = _by_owner(gout)
        red.pair({"wout": G["wout"]})
    dx, dab1, df1, red1 = _ffn_bwd(dh1, x, f1, ab1, v1, W["w1in"], W["w1out"], "ffn1_bwd")
    dmod = jnp.concatenate([red1[:3], (red2i + red2o)[:3], red3[:3]])
    dsmall = {
        "g1": red1[3], "g2": red2i[3], "g3": red3[3], "gf": dgf[0],
        "w_pool": jnp.stack([dwbd[k * 64:(k + 1) * 64, k * 64:(k + 1) * 64] for k in range(4)]),
        "pool_scale": dps[0],
    }
    if red is None:
        G["w1in"] = _by_owner(_wgrad(dab1, u1, "wgrad_1in", tm=half_f))
        G["w1out"] = _by_owner(_wgrad(act1, df1, "wgrad_1out", tm=half_f))
        return loss_blk[0, 0], dx, G, dmod, dsmall
    g1in, landed = _wgrad(dab1, u1, "wgrad_1in", tm=half_f, rider=red.rider())
    red.landed(landed)
    red.pair({"w1in": _by_owner(g1in)})
    g1out, landed = _wgrad(act1, df1, "wgrad_1out", tm=half_f, rider=red.rider())
    red.landed(landed)
    red.pair({"w1out": _by_owner(g1out)})
    part = _pack_small(dict(b_ada=dmod, g_norm_ffn1=dsmall["g1"], g_norm_mix=dsmall["g2"], g_norm_ffn2=dsmall["g3"],
                            g_final=dsmall["gf"], w_pool=dsmall["w_pool"], pool_scale=dsmall["pool_scale"]),
                       loss_blk[0, 0])
    (parts,) = red.flush("reduce_chips_w1out", gather=[part])
    return dx, parts


SHARDED = ("w_ffn1_in", "w_ffn1_out", "w_in", "w_pool_branch", "w_attn_branch", "w_out", "w_ffn2_in", "w_ffn2_out")
TRANSPOSED = ("w_ffn1_in", "w_in", "w_ffn2_in")
FIRST = ("w_ffn1_in", "w_ffn1_out")
LATER = tuple(n for n in SHARDED if n not in FIRST)
GRAD_KEY = dict(w_ffn1_in="w1in", w_ffn1_out="w1out", w_in="win", w_pool_branch="wpb", w_attn_branch="wab",
                w_out="wout", w_ffn2_in="w2in", w_ffn2_out="w2out")


def _cols_to_full(g):
    return g.transpose(1, 0, 2).reshape(g.shape[1], N_DEV * g.shape[2])


def _full_to_cols(t):
    return t.reshape(t.shape[0], N_DEV, t.shape[1] // N_DEV).transpose(1, 0, 2)


SMALL = (("b_ada", 72), ("g_norm_ffn1", 8), ("g_norm_mix", 8), ("g_norm_ffn2", 8), ("g_final", 8),
         ("w_pool", 128), ("pool_scale", 8))


def _pack_small(vals, loss):
    rows = []
    for name, nrows in SMALL:
        t = vals[name].reshape(-1, 128)
        rows.append(jnp.pad(t, ((0, nrows - t.shape[0]), (0, 0))))
    rows.append(jnp.full((8, 128), loss, F32))
    return jnp.concatenate(rows)


def _unpack_small(slab, shapes):
    out, off = {}, 0
    for name, nrows in SMALL:
        used = 1
        for d in shapes[name]:
            used *= d
        out[name] = slab[off:off + used // 128].reshape(shapes[name])
        off += nrows
    return out, slab[off, 0]


def _as_2d(t):
    return t.reshape(-1, t.shape[-1])


def kernel(x, c, positions, w_ada, b_ada, g_norm_ffn1, w_ffn1_in, w_ffn1_out, g_norm_mix, w_in, w_pool, pool_scale, w_pool_branch, w_attn_branch, w_out, g_norm_ffn2, w_ffn2_in, w_ffn2_out, g_final, loss_target, m_w_ada, m_b_ada, m_g_norm_ffn1, m_w_ffn1_in, m_w_ffn1_out, m_g_norm_mix, m_w_in, m_w_pool, m_pool_scale, m_w_pool_branch, m_w_attn_branch, m_w_out, m_g_norm_ffn2, m_w_ffn2_in, m_w_ffn2_out, m_g_final, v_w_ada, v_b_ada, v_g_norm_ffn1, v_w_ffn1_in, v_w_ffn1_out, v_g_norm_mix, v_w_in, v_w_pool, v_pool_scale, v_w_pool_branch, v_w_attn_branch, v_w_out, v_g_norm_ffn2, v_w_ffn2_in, v_w_ffn2_out, v_g_final):
    names = ["w_ada", "b_ada", "g_norm_ffn1", "w_ffn1_in", "w_ffn1_out", "g_norm_mix", "w_in", "w_pool", "pool_scale",
             "w_pool_branch", "w_attn_branch", "w_out", "g_norm_ffn2", "w_ffn2_in", "w_ffn2_out", "g_final"]
    w = dict(w_ada=w_ada, b_ada=b_ada, g_norm_ffn1=g_norm_ffn1, w_ffn1_in=w_ffn1_in, w_ffn1_out=w_ffn1_out,
             g_norm_mix=g_norm_mix, w_in=w_in, w_pool=w_pool, pool_scale=pool_scale, w_pool_branch=w_pool_branch,
             w_attn_branch=w_attn_branch, w_out=w_out, g_norm_ffn2=g_norm_ffn2, w_ffn2_in=w_ffn2_in,
             w_ffn2_out=w_ffn2_out, g_final=g_final)
    m = dict(w_ada=m_w_ada, b_ada=m_b_ada, g_norm_ffn1=m_g_norm_ffn1, w_ffn1_in=m_w_ffn1_in, w_ffn1_out=m_w_ffn1_out,
             g_norm_mix=m_g_norm_mix, w_in=m_w_in, w_pool=m_w_pool, pool_scale=m_pool_scale,
             w_pool_branch=m_w_pool_branch, w_attn_branch=m_w_attn_branch, w_out=m_w_out, g_norm_ffn2=m_g_norm_ffn2,
             w_ffn2_in=m_w_ffn2_in, w_ffn2_out=m_w_ffn2_out, g_final=m_g_final)
    v = dict(w_ada=v_w_ada, b_ada=v_b_ada, g_norm_ffn1=v_g_norm_ffn1, w_ffn1_in=v_w_ffn1_in, w_ffn1_out=v_w_ffn1_out,
             g_norm_mix=v_g_norm_mix, w_in=v_w_in, w_pool=v_w_pool, pool_scale=v_pool_scale,
             w_pool_branch=v_w_pool_branch, w_attn_branch=v_w_attn_branch, w_out=v_w_out, g_norm_ffn2=v_g_norm_ffn2,
             w_ffn2_in=v_w_ffn2_in, w_ffn2_out=v_w_ffn2_out, g_final=v_g_final)
    shapes = {n: w[n].shape for n in names}
    me = 4 * lax.axis_index("x") + 2 * lax.axis_index("y") + lax.axis_index("c")
    D = x.shape[-1]
    n_mod = w_ada.shape[-1] * N_DEV // D

    def local(t, name):
        return t[name][0].T if name in TRANSPOSED else t[name][0]

    shards = {name: local(w, name).astype(BF16) for name in SHARDED}

    def gather_done(names, fulls):
        return {name: lax.dynamic_update_index_in_dim(full, shards[name], me, axis=0)
                for name, full in zip(names, fulls)}

    def ffn_weights(g, pre):
        return {"w%sin" % pre: g["w_ffn%s_in" % pre].reshape(2, -1, D),
                "w%sout" % pre: g["w_ffn%s_out" % pre].reshape(-1, D)}

    def later_weights(fulls):
        g = gather_done(LATER, fulls)
        return dict(win=g["w_in"].reshape(-1, D), wpb=_cols_to_full(g["w_pool_branch"]),
                    wab=_cols_to_full(g["w_attn_branch"]), wout=g["w_out"].reshape(D, D), **ffn_weights(g, "2"))

    *first, c_all = _all_gather_tree([shards[n] for n in FIRST], "gather_ffn1", gather=[c.reshape(D // 128, 128)])
    W = ffn_weights(gather_done(FIRST, first), "1")

    ada_cols = w_ada.shape[-1]
    b_mine = lax.dynamic_slice_in_dim(b_ada, me * ada_cols, ada_cols, axis=1)
    cond, mod_part = _ada_mod(c_all.reshape(N_DEV, D), w_ada[0], b_mine, "ada_mod")
    (mod_all,) = _all_gather([mod_part.reshape(-1, 128)], "gather_mod")
    mod_all = mod_all.reshape(N_DEV, N_DEV, ada_cols)
    mod = lax.dynamic_index_in_dim(mod_all, me, axis=1, keepdims=False).reshape(n_mod, D)

    small = dict(g1=g_norm_ffn1[0], g2=g_norm_mix[0], g3=g_norm_ffn2[0], gf=g_final, w_pool=w_pool[0],
                 pool_scale=pool_scale[0])
    red = _GradReducer()
    dx, parts = _local_step(
        x[0], loss_target[0], positions[0], mod, small, W,
        late=(_gather_rider([shards[n] for n in LATER]), later_weights), red=red)
    chip = (2 * lax.axis_index("x") + lax.axis_index("y")).astype(jnp.int32).reshape(1)

    gsmall, loss = _unpack_small(_sum_slots(parts, "sum_small"), shapes)
    rows_mine = ada_cols // 128
    dmod_mine = lax.dynamic_slice_in_dim(parts, me * rows_mine, rows_mine, axis=1).reshape(N_DEV, ada_cols)

    grads, delta, new_m, new_v = {}, {}, {}, {}
    grads["w_ada"], delta["w_ada"], new_m["w_ada"], new_v["w_ada"] = (
        t[None] for t in _ada_grad_adamw(cond.T, dmod_mine, w_ada[0], m_w_ada[0], v_w_ada[0], "ada_grad_adamw"))
    for name in SHARDED:
        key = GRAD_KEY[name]
        res = _sum_adamw(chip, red.own[key], red.others[key], local(w, name), local(m, name), local(v, name),
                         "adamw_" + name)
        grads[name], delta[name], new_m[name], new_v[name] = (
            (t.T if name in TRANSPOSED else t)[None] for t in res)
    small_names = [name for name, _ in SMALL]
    res = _adamw_many(*([_as_2d(t[name]) for name in small_names] for t in (w, gsmall, m, v)), "adamw_small")
    for dst, vals in zip((delta, new_m, new_v), res):
        dst.update({name: t.reshape(shapes[name]) for name, t in zip(small_names, vals)})
    grads.update(gsmall)

    return (loss, dx[None], *[grads[n] for n in names], *[delta[n] for n in names],
            *[new_m[n] for n in names], *[new_v[n] for n in names])
```

```python
import functools

import jax
import jax.numpy as jnp
from jax import lax
from jax.experimental import pallas as pl
from jax.experimental.pallas import tpu as pltpu

F32 = jnp.float32
BF16 = jnp.bfloat16
MESH = pl.DeviceIdType.MESH
ANY = pl.BlockSpec(memory_space=pl.ANY)

N_DEV = 8
EPS = 1e-6
HEAD_DIM = 64
HEADS = 4
GW = HEADS * HEAD_DIM
DILATIONS = (1, 4, 16)
BAND = 128
QB = 128
POOL_WINDOWS = (2, 4, 8, 16)
HALO = 16
ROPE_THETA = 10000.0

ADAM_LR = 0.001
ADAM_B1 = 0.9
ADAM_B2 = 0.999
ADAM_EPS = 1e-08
ADAM_WD = 0.01
ADAM_STEP = 10

VMEM_LIMIT = 56 * 1024 * 1024
TS = 512
FFN_TS = 256
FFN_CHUNKS = (2816,)

NT = (((1,), (1,)), ((), ()))
TN = (((0,), (0,)), ((), ()))


def _params(**kw):
    return pltpu.CompilerParams(vmem_limit_bytes=VMEM_LIMIT, **kw)


def _dot(a, b):
    return jnp.dot(a, b, preferred_element_type=F32)


def _dot_nt(a, b):
    return lax.dot_general(a, b, NT, preferred_element_type=F32)


def _dot_tn(a, b):
    return lax.dot_general(a, b, TN, preferred_element_type=F32)


def _load_weights(pairs, sem):
    @pl.when(pl.program_id(0) == 0)
    def _():
        copies = [pltpu.make_async_copy(src, dst, sem.at[i]) for i, (src, dst) in enumerate(pairs)]
        for cp in copies:
            cp.start()
        for cp in copies:
            cp.wait()


def _norm_mod(x, g, sc, sh):
    r = lax.rsqrt(jnp.mean(x * x, axis=-1, keepdims=True) + EPS)
    xn = x * r
    y = xn * g
    return r, xn, y, y * (1.0 + sc) + sh


def _norm_mod_bwd(du, r, xn, y, g, sc):
    dsh = jnp.sum(du, axis=0, keepdims=True)
    dsc = jnp.sum(du * y, axis=0, keepdims=True)
    dy = du * (1.0 + sc)
    dg = jnp.sum(dy * xn, axis=0, keepdims=True)
    dxn = dy * g
    dx = r * (dxn - xn * jnp.mean(dxn * xn, axis=-1, keepdims=True))
    return dx, dsh, dsc, dg


def _row_tile(ts, width):
    return pl.BlockSpec((ts, width), lambda i: (i, 0))


def _const(shape):
    return pl.BlockSpec(shape, lambda *_: (0,) * len(shape))


def _ffn_chunks(Fd):
    assert sum(FFN_CHUNKS) == Fd
    edges = [sum(FFN_CHUNKS[:k]) for k in range(len(FFN_CHUNKS) + 1)]
    return [slice(a, b) for a, b in zip(edges[:-1], edges[1:])]
def _final_tile(x, g, target):
    r = lax.rsqrt(jnp.mean(x * x, axis=-1, keepdims=True) + EPS)
    xn = x * r
    err = xn * g - target
    loss = 0.5 * jnp.sum(jnp.mean(err * err, axis=-1, keepdims=True))
    dy = err * (1.0 / x.shape[-1])
    dg = jnp.sum(dy * xn, axis=0, keepdims=True)
    dxn = dy * g
    return r * (dxn - xn * jnp.mean(dxn * xn, axis=-1, keepdims=True)), loss, dg


def _ffn_fwd(h, vec, win, wout, name, rider=None, final=None):
    TS = FFN_TS
    S, D = h.shape
    _, Fd, _ = win.shape
    n_in, n_out = (6, 7) if final else (4, 5)

    def body(*refs):
        if rider is None:
            return compute(*refs)
        host, mine = rider.split(refs, n_in, n_out)
        rider.head(mine, pl.program_id(0))
        compute(*host)
        rider.tail(mine, pl.program_id(0), S // TS)

    def compute(*refs):
        h_ref, vec_ref, win_hbm, wout_hbm = refs[:4]
        hn_ref, u_ref, ab_ref, act_ref, f_ref = refs[n_in:n_in + 5]
        win_v, wout_v, sem = refs[n_in + n_out:]
        _load_weights([(win_hbm, win_v), (wout_hbm, wout_v)], sem)
        x = h_ref[...]
        g, sh, sc, gt = (vec_ref[k:k + 1, :] for k in range(4))
        _, _, _, u = _norm_mod(x, g, sc, sh)
        ub = u.astype(BF16)
        u_ref[...] = ub
        acc = jnp.zeros((TS, D), F32)
        for sl in _ffn_chunks(Fd):
            a = _dot_nt(ub, win_v[0, sl, :])
            b = _dot_nt(ub, win_v[1, sl, :])
            act = ((a * jax.nn.sigmoid(a)) * b).astype(BF16)
            ab_ref[0, :, sl] = a.astype(BF16)
            ab_ref[1, :, sl] = b.astype(BF16)
            act_ref[:, sl] = act
            acc = acc + _dot(act, wout_v[sl, :])
        f_ref[...] = acc.astype(BF16)
        hn = x + (0.5 * gt) * acc
        if not final:
            hn_ref[...] = hn
            return
        t_ref, gf_ref = refs[4:6]
        loss_ref, dgf_ref = refs[n_in + 5:n_in + 7]

        @pl.when(pl.program_id(0) == 0)
        def _():
            loss_ref[...] = jnp.zeros_like(loss_ref)
            dgf_ref[...] = jnp.zeros_like(dgf_ref)

        hn_ref[...], loss, dg = _final_tile(hn, gf_ref[0:1, :], t_ref[...])
        loss_ref[...] += loss
        dgf_ref[0:1, :] += dg

    specs = (
        [_row_tile(TS, D), _const((8, D)), ANY, ANY] + ([_row_tile(TS, D), _const((8, D))] if final else []),
        [_row_tile(TS, D), _row_tile(TS, D), pl.BlockSpec((2, TS, Fd), lambda i: (0, i, 0)),
         _row_tile(TS, Fd), _row_tile(TS, D)] + ([_const((8, 128)), _const((8, D))] if final else []),
        [jax.ShapeDtypeStruct((S, D), F32), jax.ShapeDtypeStruct((S, D), BF16),
         jax.ShapeDtypeStruct((2, S, Fd), BF16), jax.ShapeDtypeStruct((S, Fd), BF16),
         jax.ShapeDtypeStruct((S, D), BF16)]
        + ([jax.ShapeDtypeStruct((8, 128), F32), jax.ShapeDtypeStruct((8, D), F32)] if final else []),
        [pltpu.VMEM(win.shape, BF16), pltpu.VMEM(wout.shape, BF16), pltpu.SemaphoreType.DMA((2,))])
    in_specs, out_specs, out_shape, scratch = specs if rider is None else rider.specs(*specs)
    outs = pl.pallas_call(
        body, name=name, grid=(S // TS,), in_specs=in_specs, out_specs=out_specs, out_shape=out_shape,
        scratch_shapes=scratch, compiler_params=_params(),
    )(h, vec, win, wout, *(final or ()), *(rider.arrays if rider else []))
    return outs if rider is None else (outs[:n_out], outs[n_out:])


def _ffn_bwd(dh, h, f, ab, vec, win, wout, name):
    TS = FFN_TS
    S, D = h.shape
    _, Fd, _ = win.shape

    def body(dh_ref, h_ref, f_ref, ab_ref, vec_ref, win_hbm, wout_hbm,
             dhp_ref, dab_ref, df_ref, red_ref, win_v, wout_v, sem):
        _load_weights([(win_hbm, win_v), (wout_hbm, wout_v)], sem)

        @pl.when(pl.program_id(0) == 0)
        def _():
            red_ref[...] = jnp.zeros_like(red_ref)

        dh_v = dh_ref[...]
        x = h_ref[...]
        g, sh, sc, gt = (vec_ref[k:k + 1, :] for k in range(4))
        dgt = jnp.sum((0.5 * f_ref[...].astype(F32)) * dh_v, axis=0, keepdims=True)
        dfb = ((0.5 * gt) * dh_v).astype(BF16)
        df_ref[...] = dfb
        du = jnp.zeros((TS, D), F32)
        for sl in _ffn_chunks(Fd):
            dact = _dot_nt(dfb, wout_v[sl, :])
            av = ab_ref[0, :, sl].astype(F32)
            bv = ab_ref[1, :, sl].astype(F32)
            sg = jax.nn.sigmoid(av)
            da = (dact * bv * (sg * (1.0 + av * (1.0 - sg)))).astype(BF16)
            db = (dact * (av * sg)).astype(BF16)
            dab_ref[0, :, sl] = da
            dab_ref[1, :, sl] = db
            du = du + _dot(da, win_v[0, sl, :]) + _dot(db, win_v[1, sl, :])
        r, xn, y, _ = _norm_mod(x, g, sc, sh)
        dx, dsh, dsc, dg = _norm_mod_bwd(du, r, xn, y, g, sc)
        dhp_ref[...] = dh_v + dx
        red_ref[0:1, :] += dsh
        red_ref[1:2, :] += dsc
        red_ref[2:3, :] += dgt
        red_ref[3:4, :] += dg

    ab_spec = pl.BlockSpec((2, TS, Fd), lambda i: (0, i, 0))
    return pl.pallas_call(
        body, name=name, grid=(S // TS,),
        in_specs=[_row_tile(TS, D), _row_tile(TS, D), _row_tile(TS, D), ab_spec, _const((8, D)), ANY, ANY],
        out_specs=[_row_tile(TS, D), ab_spec, _row_tile(TS, D), _const((8, D))],
        out_shape=[jax.ShapeDtypeStruct((S, D), F32), jax.ShapeDtypeStruct((2, S, Fd), BF16),
                   jax.ShapeDtypeStruct((S, D), BF16), jax.ShapeDtypeStruct((8, D), F32)],
        scratch_shapes=[pltpu.VMEM(win.shape, BF16), pltpu.VMEM(wout.shape, BF16), pltpu.SemaphoreType.DMA((2,))],
        compiler_params=_params(),
    )(dh, h, f, ab, vec, win, wout)


def _wgrad(x, y, name, tm=None, ts=2048, rider=None):
    xb = x.ndim == 3
    nb = x.shape[0] if xb else 0
    S, M = x.shape[-2:]
    N = y.shape[-1]
    tm = tm or M
    ts = min(ts, S)
    nk = S // ts
    grid = (max(nb, 1), M // tm, nk)

    def body(*refs):
        if rider is None:
            return compute(*refs)
        host, mine = rider.split(refs, 2, 1)
        step = (pl.program_id(0) * grid[1] + pl.program_id(1)) * grid[2] + pl.program_id(2)
        rider.head(mine, step)
        compute(*host)
        rider.tail(mine, step, grid[0] * grid[1] * grid[2])

    def compute(x_ref, y_ref, o_ref, acc):
        k = pl.program_id(2)

        @pl.when(k == 0)
        def _():
            acc[...] = jnp.zeros_like(acc)

        acc[...] += _dot_tn(x_ref[...], y_ref[...])

        @pl.when(k == nk - 1)
        def _():
            o_ref[...] = acc[...].astype(BF16)

    x_spec = (pl.BlockSpec((None, ts, tm), lambda b, i, k: (b, k, i)) if xb
              else pl.BlockSpec((ts, tm), lambda b, i, k: (k, i)))
    y_spec = pl.BlockSpec((ts, N), lambda b, i, k: (k, 0))
    if xb:
        o_spec, o_shape = pl.BlockSpec((None, tm, N), lambda b, i, k: (b, i, 0)), (nb, M, N)
    else:
        o_spec, o_shape = pl.BlockSpec((tm, N), lambda b, i, k: (i, 0)), (M, N)
    specs = ([x_spec, y_spec], [o_spec], [jax.ShapeDtypeStruct(o_shape, BF16)], [pltpu.VMEM((tm, N), F32)])
    in_specs, out_specs, out_shape, scratch = specs if rider is None else rider.specs(*specs)
    outs = pl.pallas_call(
        body, name=name, grid=grid, in_specs=in_specs, out_specs=out_specs, out_shape=out_shape,
        scratch_shapes=scratch, compiler_params=_params(),
    )(x, y, *(rider.arrays if rider else []))
    return outs[0] if rider is None else (outs[0], outs[1:])


P_OFF, Q_OFF, K_OFF, V_OFF, G_OFF = 0, 256, 1024, 1792, 2560
IN_WIDTH = 4608


def _first_half_mask(ts):
    lane = lax.broadcasted_iota(jnp.int32, (ts, 128), 1)
    return (lane % HEAD_DIM) < (HEAD_DIM // 2)


def _rope(t, cos, sin_signed, first, sign):
    partner = jnp.where(first, pltpu.roll(t, 96, 1), pltpu.roll(t, 32, 1))
    return t * cos + sign * (partner * sin_signed)


def _res_spec(r):
    return pl.BlockSpec((r, TS // r, GW), lambda i: (0, i, 0))


def _res_shape(S, r, dtype):
    return jax.ShapeDtypeStruct((r, S // r, GW), dtype)


def _to_residues(piece, out_ref, lanes, r, scr):
    if r == 1:
        out_ref[0, :, lanes] = piece.astype(out_ref.dtype)
        return
    for h in range(piece.shape[1] // 128):
        scr[h] = piece[:, h * 128:(h + 1) * 128]
        at = slice(lanes.start + h * 128, lanes.start + (h + 1) * 128)
        for res in range(r):
            out_ref[res, :, at] = scr[h, pl.ds(res, TS // r, stride=r), :].astype(out_ref.dtype)


def _from_residues(in_ref, lanes, r, scr):
    if r == 1:
        return in_ref[0, :, lanes].astype(F32)
    halves = (lanes.stop - lanes.start) // 128
    for h in range(halves):
        at = slice(lanes.start + h * 128, lanes.start + (h + 1) * 128)
        for res in range(r):
            scr[h, pl.ds(res, TS // r, stride=r), :] = in_ref[res, :, at].astype(F32)
    return scr[0] if halves == 1 else jnp.concatenate([scr[0], scr[1]], axis=1)


RES_SCRATCH = (2, TS, 128)


def _mix_in_fwd(h, vec, cos, sin, win, name):
    S, D = h.shape

    def body(h_ref, vec_ref, cos_ref, sin_ref, win_hbm, u_ref, p_ref, gates_ref, *rest):
        qkv_refs, (win_v, sem, scr) = rest[:9], rest[9:]
        _load_weights([(win_hbm, win_v)], sem)
        g, sh, sc = (vec_ref[k:k + 1, :] for k in range(3))
        _, _, _, u = _norm_mod(h_ref[...], g, sc, sh)
        ub = u.astype(BF16)
        u_ref[...] = ub
        p_ref[...] = _dot_nt(ub, win_v[P_OFF:Q_OFF, :])
        cosv, sinv = cos_ref[...], sin_ref[...]
        first = _first_half_mask(TS)
        for which, off in enumerate((Q_OFF, K_OFF, V_OFF)):
            t = _dot_nt(ub, win_v[off:off + 3 * GW, :])
            for gi in range(3):
                for half in range(2):
                    c0 = gi * GW + half * 128
                    piece = t[:, c0:c0 + 128]
                    if which < 2:
                        piece = _rope(piece, cosv, sinv, first, 1.0)
                    _to_residues(piece, qkv_refs[which * 3 + gi], slice(half * 128, (half + 1) * 128),
                                 DILATIONS[gi], scr)
        gates_ref[...] = jax.nn.sigmoid(_dot_nt(ub, win_v[G_OFF:IN_WIDTH, :])).astype(BF16)

    return pl.pallas_call(
        body, name=name, grid=(S // TS,),
        in_specs=[_row_tile(TS, D), _const((8, D)), _row_tile(TS, 128), _row_tile(TS, 128), ANY],
        out_specs=[_row_tile(TS, D), _row_tile(TS, GW), _row_tile(TS, 2 * D)] + [_res_spec(r) for r in DILATIONS] * 3,
        out_shape=[jax.ShapeDtypeStruct((S, D), BF16), jax.ShapeDtypeStruct((S, GW), F32),
                   jax.ShapeDtypeStruct((S, 2 * D), BF16)] + [_res_shape(S, r, BF16) for r in DILATIONS] * 3,
        scratch_shapes=[pltpu.VMEM((IN_WIDTH, D), BF16), pltpu.SemaphoreType.DMA((1,)), pltpu.VMEM(RES_SCRATCH, F32)],
        compiler_params=_params(),
    )(h, vec, cos, sin, win)


def _mix_in_bwd(dh, h, vec, cos, sin, dp, dqkv, dgl, win, name):
    S, D = h.shape

    def body(dh_ref, h_ref, vec_ref, cos_ref, sin_ref, dp_ref, *rest):
        dqkv_refs = rest[:9]
        dgl_ref, win_hbm, dhp_ref, dproj_ref, red_ref, win_v, sem, scr = rest[9:]
        _load_weights([(win_hbm, win_v)], sem)

        @pl.when(pl.program_id(0) == 0)
        def _():
            red_ref[...] = jnp.zeros_like(red_ref)

        cosv, sinv = cos_ref[...], sin_ref[...]
        first = _first_half_mask(TS)
        dproj_ref[:, P_OFF:Q_OFF] = dp_ref[...].astype(BF16)
        for which, off in enumerate((Q_OFF, K_OFF, V_OFF)):
            for gi in range(3):
                for half in range(2):
                    piece = _from_residues(dqkv_refs[which * 3 + gi], slice(half * 128, (half + 1) * 128),
                                           DILATIONS[gi], scr)
                    if which < 2:
                        piece = _rope(piece, cosv, sinv, first, -1.0)
                    c0 = off + gi * GW + half * 128
                    dproj_ref[:, c0:c0 + 128] = piece.astype(BF16)
        dproj_ref[:, G_OFF:IN_WIDTH] = dgl_ref[...]
        du = _dot(dproj_ref[...], win_v[...])
        g, sh, sc = (vec_ref[k:k + 1, :] for k in range(3))
        r, xn, y, _ = _norm_mod(h_ref[...], g, sc, sh)
        dx, dsh, dsc, dg = _norm_mod_bwd(du, r, xn, y, g, sc)
        dhp_ref[...] = dh_ref[...] + dx
        red_ref[0:1, :] += dsh
        red_ref[1:2, :] += dsc
        red_ref[3:4, :] += dg

    return pl.pallas_call(
        body, name=name, grid=(S // TS,),
        in_specs=[_row_tile(TS, D), _row_tile(TS, D), _const((8, D)), _row_tile(TS, 128), _row_tile(TS, 128),
                  _row_tile(TS, GW)] + [_res_spec(r) for r in DILATIONS] * 3 + [_row_tile(TS, 2 * D), ANY],
        out_specs=[_row_tile(TS, D), _row_tile(TS, IN_WIDTH), _const((8, D))],
        out_shape=[jax.ShapeDtypeStruct((S, D), F32), jax.ShapeDtypeStruct((S, IN_WIDTH), BF16),
                   jax.ShapeDtypeStruct((8, D), F32)],
        scratch_shapes=[pltpu.VMEM((IN_WIDTH, D), BF16), pltpu.SemaphoreType.DMA((1,)), pltpu.VMEM(RES_SCRATCH, F32)],
        compiler_params=_params(),
    )(dh, h, vec, cos, sin, dp, *dqkv, dgl, win)


def _pool_lanes(rows):
    lane = lax.broadcasted_iota(jnp.int32, (rows, GW), 1)
    return lane // HEAD_DIM


def _pool_window(rows):
    grp = _pool_lanes(rows)
    w = jnp.full((rows, GW), POOL_WINDOWS[0], jnp.int32)
    for k in range(1, len(POOL_WINDOWS)):
        w = jnp.where(grp == k, POOL_WINDOWS[k], w)
    return grp, w


def _pool_fwd(p, wbd, scale, name, ts=1024):
    S = p.shape[0]
    ext = ts + HALO

    def body(pc_ref, ph_ref, wbd_ref, sc_ref, d_ref, y_ref):
        i = pl.program_id(0)
        cur = pc_ref[...]
        halo = jnp.where(i > 0, ph_ref[...], 0.0)
        s = jnp.concatenate([halo, cur], axis=0)
        grp, w = _pool_window(ext)
        sel = jnp.zeros((ext, GW), F32)
        for k, wk in enumerate(POOL_WINDOWS):
            s = s + pltpu.roll(s, wk // 2, 0)
            sel = jnp.where(grp == k, s, sel)
        t = i * ts + lax.broadcasted_iota(jnp.int32, (ts, GW), 0)
        count = jnp.minimum(t + 1, w[HALO:]).astype(F32)
        d = (sel[HALO:] / count - cur).astype(BF16)
        d_ref[...] = d
        y_ref[...] = (_dot(d, wbd_ref[...]) * sc_ref[...]).astype(BF16)

    return pl.pallas_call(
        body, name=name, grid=(S // ts,),
        in_specs=[_row_tile(ts, GW),
                  pl.BlockSpec((HALO, GW), lambda i: (jnp.maximum(i * (ts // HALO) - 1, 0), 0)),
                  _const((GW, GW)), _const((1, GW))],
        out_specs=[_row_tile(ts, GW), _row_tile(ts, GW)],
        out_shape=[jax.ShapeDtypeStruct((S, GW), BF16), jax.ShapeDtypeStruct((S, GW), BF16)],
        compiler_params=_params(),
    )(p, p, wbd, scale)


def _pool_bwd(dy, d, wbd, scale, name, ts=1024):
    S = dy.shape[0]
    ext = ts + HALO
    nsteps = S // ts
    last_halo = S // HALO - 1

    def body(dyc_ref, dyh_ref, d_ref, wbd_ref, sc_ref, dp_ref, dw_ref, ds_ref):
        i = pl.program_id(0)

        @pl.when(i == 0)
        def _():
            dw_ref[...] = jnp.zeros_like(dw_ref)
            ds_ref[...] = jnp.zeros_like(ds_ref)

        dyc = dyc_ref[...]
        dyh = jnp.where(i < nsteps - 1, dyh_ref[...], 0.0)
        dys = (jnp.concatenate([dyc, dyh], axis=0) * sc_ref[...]).astype(BF16)
        dd = _dot_nt(dys, wbd_ref[...])
        grp, w = _pool_window(ext)
        t = i * ts + lax.broadcasted_iota(jnp.int32, (ext, GW), 0)
        s = dd / jnp.minimum(t + 1, w).astype(F32)
        sel = jnp.zeros((ext, GW), F32)
        for k, wk in enumerate(POOL_WINDOWS):
            s = s + pltpu.roll(s, ext - wk // 2, 0)
            sel = jnp.where(grp == k, s, sel)
        dp_ref[...] = sel[:ts] - dd[:ts]
        dv = d_ref[...]
        z = _dot(dv, wbd_ref[...])
        ds_ref[0:1, :] += jnp.sum(dyc * z, axis=0, keepdims=True)
        dw_ref[...] += _dot_tn(dv, dys[:ts])

    return pl.pallas_call(
        body, name=name, grid=(nsteps,),
        in_specs=[_row_tile(ts, GW),
                  pl.BlockSpec((HALO, GW), lambda i: (jnp.minimum((i + 1) * (ts // HALO), last_halo), 0)),
                  _row_tile(ts, GW), _const((GW, GW)), _const((1, GW))],
        out_specs=[_row_tile(ts, GW), _const((GW, GW)), _const((8, GW))],
        out_shape=[jax.ShapeDtypeStruct((S, GW), F32), jax.ShapeDtypeStruct((GW, GW), F32),
                   jax.ShapeDtypeStruct((8, GW), F32)],
        compiler_params=_params(),
    )(dy, dy, d, wbd, scale)


def _head_id(rows):
    return lax.broadcasted_iota(jnp.int32, (rows, GW), 1) // HEAD_DIM


def _stack_heads(t, hid):
    return jnp.concatenate([jnp.where(hid == h, t, jnp.zeros_like(t)) for h in range(HEADS)], axis=0)


def _unstack_heads(t_all, hid):
    out = jnp.zeros((QB, GW), F32)
    for h in range(HEADS):
        out = jnp.where(hid == h, t_all[h * QB:(h + 1) * QB], out)
    return out


def _band_mask(n):
    row = lax.broadcasted_iota(jnp.int32, (HEADS * QB, 2 * QB), 0) % QB
    col = lax.broadcasted_iota(jnp.int32, (HEADS * QB, 2 * QB), 1)
    rel = row + QB - col
    return (rel >= 0) & (rel <= BAND) & ((col >= QB) | (n > 0))


FWD_STREAMS, BWD_STREAMS = 16, 8


def _streams(r, nb, most):
    if r > 1:
        ns = min(r, most)
        return nb, [(lambda rb, l=l: ns * rb + l, 0) for l in range(ns)]
    ns = min(most, nb)
    return nb // ns, [(lambda rb: 0, l * (nb // ns)) for l in range(ns)]


def _attn_fwd(q, k, v, name):
    r, L, _ = q.shape
    nbs, streams = _streams(r, L // QB, FWD_STREAMS)
    ns = len(streams)
    grid = (max(r // ns, 1), nbs)

    def cur(res, off):
        return pl.BlockSpec((None, QB, GW), lambda rb, n: (res(rb), n + off, 0))

    def prev(res, off):
        return pl.BlockSpec((None, QB, GW), lambda rb, n: (res(rb), jnp.maximum(n + off - 1, 0), 0))

    def body(*refs):
        n = pl.program_id(1)
        hid = _head_id(QB)
        o_ref, lse_ref = refs[5 * len(streams):]
        for l, (_, off) in enumerate(streams):
            q_ref, kp_ref, kc_ref, vp_ref, vc_ref = refs[5 * l:5 * l + 5]
            qs = _stack_heads(q_ref[...], hid)
            kc = jnp.concatenate([kp_ref[...], kc_ref[...]], axis=0)
            vc = jnp.concatenate([vp_ref[...], vc_ref[...]], axis=0)
            s = _dot_nt(qs, kc) * (HEAD_DIM ** -0.5)
            s = jnp.where(_band_mask(n + off), s, -jnp.inf)
            m = jnp.max(s, axis=-1, keepdims=True)
            e = jnp.exp(s - m)
            den = jnp.sum(e, axis=-1, keepdims=True)
            lse = m + jnp.log(den)
            pr = (e * (1.0 / den)).astype(BF16)
            o_ref[l] = _unstack_heads(_dot(pr, vc), hid).astype(BF16)
            lse_ref[l] = _unstack_heads(jnp.broadcast_to(lse, (HEADS * QB, GW)), hid)

    in_specs, args = [], []
    for res, off in streams:
        in_specs += [cur(res, off), prev(res, off), cur(res, off), prev(res, off), cur(res, off)]
        args += [q, k, k, v, v]
    out = jax.ShapeDtypeStruct((ns * grid[0], nbs * QB, GW), F32)
    both = pl.BlockSpec((ns, QB, GW), lambda rb, n: (rb, n, 0))
    o, lse = pl.pallas_call(
        body, name=name, grid=grid, in_specs=in_specs, out_specs=[both, both],
        out_shape=[jax.ShapeDtypeStruct(out.shape, BF16), out],
        compiler_params=_params(),
    )(*args)
    return o.reshape(q.shape), lse.reshape(q.shape)


def _head_rows(t_full, hid):
    return jnp.concatenate(
        [jnp.max(jnp.where(hid == h, t_full, -jnp.inf), axis=-1, keepdims=True) for h in range(HEADS)], axis=0)


def _attn_bwd(q, k, v, do, lse, cterm, name):
    r, L, _ = q.shape
    nbs, streams = _streams(r, L // QB, BWD_STREAMS)
    ns = len(streams)
    parts = r == 1

    def spec(res, index):
        return pl.BlockSpec((None, QB, GW), lambda rb, n: (res(rb), index(n), 0))

    def body(*refs):
        dq_ref, dk_ref, dv_ref, carry_k, carry_v, seam_k, seam_v = refs[8 * ns:]
        n = pl.program_id(1)

        @pl.when(n == 0)
        def _():
            carry_k[...] = jnp.zeros_like(carry_k)
            carry_v[...] = jnp.zeros_like(carry_v)

        @pl.when(n < nbs)
        def _():
            hid = _head_id(QB)
            for l, (_, off) in enumerate(streams):
                q_ref, do_ref, lse_ref, c_ref, kp_ref, kc_ref, vp_ref, vc_ref = refs[8 * l:8 * l + 8]
                qs = _stack_heads(q_ref[...], hid)
                dos = _stack_heads(do_ref[...], hid)
                kc = jnp.concatenate([kp_ref[...], kc_ref[...]], axis=0)
                vc = jnp.concatenate([vp_ref[...], vc_ref[...]], axis=0)
                s = _dot_nt(qs, kc) * (HEAD_DIM ** -0.5)
                s = jnp.where(_band_mask(n + off), s, -jnp.inf)
                p = jnp.exp(s - _head_rows(lse_ref[...], hid))
                dp = _dot_nt(dos, vc)
                ds = (p * (dp + _head_rows(c_ref[...], hid)) * (HEAD_DIM ** -0.5)).astype(BF16)
                dq_ref[l] = _unstack_heads(_dot(ds, kc), hid).astype(BF16)
                dkc = _dot_tn(ds, qs)
                dvc = _dot_tn(p.astype(BF16), dos)
                if parts and l > 0:
                    @pl.when(n == 0)
                    def _():
                        seam_k[l] = dkc[:QB]
                        seam_v[l] = dvc[:QB]
                dk_ref[l] = (carry_k[l] + dkc[:QB]).astype(BF16)
                dv_ref[l] = (carry_v[l] + dvc[:QB]).astype(BF16)
                carry_k[l] = dkc[QB:]
                carry_v[l] = dvc[QB:]

        @pl.when(n == nbs)
        def _():
            for l in range(ns):
                if parts and l + 1 < ns:
                    dk_ref[l] = (carry_k[l] + seam_k[l + 1]).astype(BF16)
                    dv_ref[l] = (carry_v[l] + seam_v[l + 1]).astype(BF16)
                else:
                    dk_ref[l] = carry_k[l].astype(BF16)
                    dv_ref[l] = carry_v[l].astype(BF16)

    in_specs, args = [], []
    for res, off in streams:
        qside = functools.partial(lambda n, off: jnp.minimum(n, nbs - 1) + off, off=off)
        kprev = functools.partial(lambda n, off: jnp.maximum(jnp.minimum(n, nbs) - 1 + off, 0), off=off)
        in_specs += [spec(res, qside)] * 4 + [spec(res, kprev), spec(res, qside)] * 2
        args += [q, do, lse, cterm, k, k, v, v]
    out = jax.ShapeDtypeStruct((ns * max(r // ns, 1), nbs * QB, GW), BF16)
    qout = pl.BlockSpec((ns, QB, GW), lambda rb, n: (rb, jnp.minimum(n, nbs - 1), 0))
    kout = pl.BlockSpec((ns, QB, GW), lambda rb, n: (rb, jnp.maximum(n - 1, 0), 0))
    buf = pltpu.VMEM((ns, QB, GW), F32)
    outs = pl.pallas_call(
        body, name=name, grid=(max(r // ns, 1), nbs + 1),
        in_specs=in_specs, out_specs=[qout, kout, kout], out_shape=[out, out, out],
        scratch_shapes=[buf, buf, buf, buf],
        compiler_params=_params(),
    )(*args)
    return [t.reshape(q.shape) for t in outs]


def _token_order(refs, scr):
    return [_from_residues(ref, slice(0, GW), r, scr) for ref, r in zip(refs, DILATIONS)]


def _group_weights(lses):
    l0, l1, l2 = lses
    m = jnp.maximum(jnp.maximum(l0, l1), l2)
    e = [jnp.exp(l - m) for l in (l0, l1, l2)]
    den = e[0] + e[1] + e[2]
    return [ei / den for ei in e]


def _mix_out_fwd(h, vec, gates, ypool, o3, lse3, wpb, wab, wout, name):
    S, D = h.shape

    def body(h_ref, vec_ref, gates_ref, yp_ref, o0, o1, o2, l0, l1, l2, wpb_hbm, wab_hbm, wout_hbm,
             hn_ref, ya_ref, merged_ref, tm_ref, wpb_v, wab_v, wout_v, sem, scr):
        _load_weights([(wpb_hbm, wpb_v), (wab_hbm, wab_v), (wout_hbm, wout_v)], sem)
        gt = vec_ref[3:4, :]
        wts = _group_weights(_token_order((l0, l1, l2), scr))
        og = _token_order((o0, o1, o2), scr)
        ya = (wts[0] * og[0] + wts[1] * og[1] + wts[2] * og[2]).astype(BF16)
        ya_ref[...] = ya
        merged = (gates_ref[:, :D].astype(F32) * _dot(yp_ref[...], wpb_v[...])
                  + gates_ref[:, D:].astype(F32) * _dot(ya, wab_v[...])).astype(BF16)
        merged_ref[...] = merged
        tm = _dot(merged, wout_v[...])
        tm_ref[...] = tm.astype(BF16)
        hn_ref[...] = h_ref[...] + gt * tm

    grp = _row_tile(TS, GW)
    res = [_res_spec(r) for r in DILATIONS]
    return pl.pallas_call(
        body, name=name, grid=(S // TS,),
        in_specs=[_row_tile(TS, D), _const((8, D)), _row_tile(TS, 2 * D), grp] + res * 2 + [ANY, ANY, ANY],
        out_specs=[_row_tile(TS, D), grp, _row_tile(TS, D), _row_tile(TS, D)],
        out_shape=[jax.ShapeDtypeStruct((S, D), F32), jax.ShapeDtypeStruct((S, GW), BF16),
                   jax.ShapeDtypeStruct((S, D), BF16), jax.ShapeDtypeStruct((S, D), BF16)],
        scratch_shapes=[pltpu.VMEM((GW, D), BF16), pltpu.VMEM((GW, D), BF16), pltpu.VMEM((D, D), BF16),
                        pltpu.SemaphoreType.DMA((3,)), pltpu.VMEM(RES_SCRATCH, F32)],
        compiler_params=_params(),
    )(h, vec, gates, ypool, *o3, *lse3, wpb, wab, wout)


def _mix_out_bwd(dh, tm, vec, gates, ypool, o3, lse3, wpb, wab, wout, name, rider=None):
    S, D = dh.shape

    def body(*refs):
        if rider is None:
            return compute(*refs)
        host, mine = rider.split(refs, 14, 12)
        rider.head(mine, pl.program_id(0))
        compute(*host)
        rider.tail(mine, pl.program_id(0), S // TS)

    def compute(dh_ref, tm_ref, vec_ref, gates_ref, yp_ref, o0, o1, o2, l0, l1, l2, wpb_hbm, wab_hbm, wout_hbm,
                dtm_ref, dgl_ref, dypb_ref, dyab_ref, dyp_ref, do0, do1, do2, c0, c1, c2, red_ref,
                wpb_v, wab_v, wout_v, sem, scr):
        _load_weights([(wpb_hbm, wpb_v), (wab_hbm, wab_v), (wout_hbm, wout_v)], sem)

        @pl.when(pl.program_id(0) == 0)
        def _():
            red_ref[...] = jnp.zeros_like(red_ref)

        gt = vec_ref[3:4, :]
        dh_v = dh_ref[...]
        red_ref[2:3, :] += jnp.sum(tm_ref[...].astype(F32) * dh_v, axis=0, keepdims=True)
        dtm = (gt * dh_v).astype(BF16)
        dtm_ref[...] = dtm
        dm = _dot_nt(dtm, wout_v[...])
        wts = _group_weights(_token_order((l0, l1, l2), scr))
        og = _token_order((o0, o1, o2), scr)
        ya = wts[0] * og[0] + wts[1] * og[1] + wts[2] * og[2]
        ypb = _dot(yp_ref[...], wpb_v[...])
        yab = _dot(ya.astype(BF16), wab_v[...])
        gp = gates_ref[:, :D].astype(F32)
        ga = gates_ref[:, D:].astype(F32)
        dgl_ref[:, :D] = (dm * ypb * gp * (1.0 - gp)).astype(BF16)
        dgl_ref[:, D:] = (dm * yab * ga * (1.0 - ga)).astype(BF16)
        dypb = (dm * gp).astype(BF16)
        dyab = (dm * ga).astype(BF16)
        dypb_ref[...] = dypb
        dyab_ref[...] = dyab
        dyp_ref[...] = _dot_nt(dypb, wpb_v[...])
        dya = _dot_nt(dyab, wab_v[...])
        row = lax.broadcasted_iota(jnp.int32, (GW, GW), 0) // HEAD_DIM
        col = lax.broadcasted_iota(jnp.int32, (GW, GW), 1) // HEAD_DIM
        ones = jnp.where(row == col, 1.0, 0.0).astype(F32)
        tot = jnp.dot(dya * ya, ones, preferred_element_type=F32, precision=lax.Precision.HIGHEST)
        for wg, do_ref, c_ref, r in zip(wts, (do0, do1, do2), (c0, c1, c2), DILATIONS):
            _to_residues(wg * dya, do_ref, slice(0, GW), r, scr)
            _to_residues(-(wg * tot), c_ref, slice(0, GW), r, scr)

    grp = _row_tile(TS, GW)
    res = [_res_spec(r) for r in DILATIONS]
    specs = (
        [_row_tile(TS, D), _row_tile(TS, D), _const((8, D)), _row_tile(TS, 2 * D), grp] + res * 2
        + [ANY, ANY, ANY],
        [_row_tile(TS, D), _row_tile(TS, 2 * D), _row_tile(TS, D), _row_tile(TS, D), grp]
        + res * 2 + [_const((8, D))],
        [jax.ShapeDtypeStruct((S, D), BF16), jax.ShapeDtypeStruct((S, 2 * D), BF16),
         jax.ShapeDtypeStruct((S, D), BF16), jax.ShapeDtypeStruct((S, D), BF16), jax.ShapeDtypeStruct((S, GW), F32)]
        + [_res_shape(S, r, BF16) for r in DILATIONS] + [_res_shape(S, r, F32) for r in DILATIONS]
        + [jax.ShapeDtypeStruct((8, D), F32)],
        [pltpu.VMEM((GW, D), BF16), pltpu.VMEM((GW, D), BF16), pltpu.VMEM((D, D), BF16),
         pltpu.SemaphoreType.DMA((3,)), pltpu.VMEM(RES_SCRATCH, F32)])
    in_specs, out_specs, out_shape, scratch = specs if rider is None else rider.specs(*specs)
    outs = pl.pallas_call(
        body, name=name, grid=(S // TS,), in_specs=in_specs, out_specs=out_specs, out_shape=out_shape,
        scratch_shapes=scratch, compiler_params=_params(),
    )(dh, tm, vec, gates, ypool, *o3, *lse3, wpb, wab, wout, *(rider.arrays if rider else []))
    return outs if rider is None else (outs[:12], outs[12:])


def _ada_mod(c_all, w, b, name):
    def body(c_ref, w_ref, b_ref, cond_ref, mod_ref):
        cv = c_ref[...]
        cond = cv * jax.nn.sigmoid(cv)
        cond_ref[...] = cond
        mod_ref[...] = jnp.dot(cond, w_ref[...], preferred_element_type=F32,
                               precision=lax.Precision.HIGHEST) + b_ref[...]

    return pl.pallas_call(
        body, name=name,
        out_shape=[jax.ShapeDtypeStruct(c_all.shape, F32), jax.ShapeDtypeStruct((c_all.shape[0], w.shape[1]), F32)],
        compiler_params=_params(),
    )(c_all, w, b)


def _adamw_math(w, g, m, v):
    m = ADAM_B1 * m + (1.0 - ADAM_B1) * g
    v = ADAM_B2 * v + (1.0 - ADAM_B2) * (g * g)
    m_hat = m / (1.0 - ADAM_B1 ** ADAM_STEP)
    v_hat = v / (1.0 - ADAM_B2 ** ADAM_STEP)
    delta = -ADAM_LR * (m_hat / (jnp.sqrt(v_hat) + ADAM_EPS) + ADAM_WD * w)
    return delta, m, v


def _adamw_many(ws, gs, ms, vs, name):
    n = len(ws)

    def body(*refs):
        for k in range(n):
            w_ref, g_ref, m_ref, v_ref = (refs[j * n + k] for j in range(4))
            d_ref, mo_ref, vo_ref = (refs[(4 + j) * n + k] for j in range(3))
            d_ref[...], mo_ref[...], vo_ref[...] = _adamw_math(w_ref[...], g_ref[...], m_ref[...], v_ref[...])

    outs = pl.pallas_call(
        body, name=name, out_shape=[jax.ShapeDtypeStruct(t.shape, F32) for t in ws] * 3,
        compiler_params=_params(),
    )(*ws, *gs, *ms, *vs)
    return outs[:n], outs[n:2 * n], outs[2 * n:]


def _ada_grad_adamw(cond_t, dmod, w, m, v, name, tr=256):
    R, C = w.shape
    nb = dmod.shape[0]

    def body(ct_ref, dm_ref, w_ref, m_ref, v_ref, g_ref, d_ref, mo_ref, vo_ref):
        ct = ct_ref[...]
        dm = dm_ref[...]
        g = jnp.zeros((tr, C), F32)
        for bi in range(nb):
            g = g + ct[:, bi:bi + 1] * dm[bi:bi + 1, :]
        g_ref[...] = g
        d_ref[...], mo_ref[...], vo_ref[...] = _adamw_math(w_ref[...], g, m_ref[...], v_ref[...])

    spec = _row_tile(tr, C)
    out = jax.ShapeDtypeStruct((R, C), F32)
    return pl.pallas_call(
        body, name=name, grid=(R // tr,),
        in_specs=[_row_tile(tr, nb), _const((nb, C)), spec, spec, spec],
        out_specs=[spec] * 4, out_shape=[out] * 4,
        compiler_params=_params(),
    )(cond_t, dmod, w, m, v)


def _row_step(rows, cap=256):
    for cand in range(cap, 15, -16):
        if rows % cand == 0:
            return cand
    return rows


def _slot_sum(x_ref):
    acc = x_ref[0].astype(F32)
    for k in range(1, x_ref.shape[0]):
        acc = acc + x_ref[k].astype(F32)
    return acc


def _sum_slots(x, name, out_dtype=F32):
    n, R, C = x.shape
    tr = _row_step(R)

    def body(x_ref, o_ref):
        o_ref[...] = _slot_sum(x_ref).astype(out_dtype)

    return pl.pallas_call(
        body, name=name, grid=(R // tr,),
        in_specs=[pl.BlockSpec((n, tr, C), lambda i: (0, i, 0))],
        out_specs=_row_tile(tr, C), out_shape=jax.ShapeDtypeStruct((R, C), out_dtype),
        compiler_params=_params(),
    )(x)


def _sum_pair(core, g, recv, name):
    _, _, R, C = g.shape
    tr = _row_step(R, cap=1024)

    def body(core_ref, g_ref, r_ref, o_ref):
        o_ref[...] = (g_ref[...].astype(F32) + r_ref[...].astype(F32)).astype(BF16)

    return pl.pallas_call(
        body, name=name, out_shape=jax.ShapeDtypeStruct((4, R, C), BF16),
        grid_spec=pltpu.PrefetchScalarGridSpec(
            num_scalar_prefetch=1, grid=(4, R // tr),
            in_specs=[pl.BlockSpec((None, None, tr, C), lambda k, i, core_ref: (k, core_ref[0], i, 0)),
                      pl.BlockSpec((None, tr, C), lambda k, i, core_ref: (k, i, 0))],
            out_specs=pl.BlockSpec((None, tr, C), lambda k, i, core_ref: (k, i, 0))),
        compiler_params=_params(),
    )(core, g, recv)


def _sum_adamw(chip, own, recv, w, m, v, name):
    _, R, C = own.shape
    tr = _row_step(R, cap=512)

    def body(chip_ref, own_ref, r_ref, w_ref, m_ref, v_ref, g_ref, d_ref, mo_ref, vo_ref):
        g = own_ref[...].astype(F32) + _slot_sum(r_ref)
        g_ref[...] = g
        d_ref[...], mo_ref[...], vo_ref[...] = _adamw_math(w_ref[...], g, m_ref[...], v_ref[...])

    spec = pl.BlockSpec((tr, C), lambda i, chip_ref: (i, 0))
    out = jax.ShapeDtypeStruct((R, C), F32)
    return pl.pallas_call(
        body, name=name, out_shape=[out] * 4,
        grid_spec=pltpu.PrefetchScalarGridSpec(
            num_scalar_prefetch=1, grid=(R // tr,),
            in_specs=[pl.BlockSpec((None, tr, C), lambda i, chip_ref: (chip_ref[0], i, 0)),
                      pl.BlockSpec((3, tr, C), lambda i, chip_ref: (0, i, 0)), spec, spec, spec],
            out_specs=[spec] * 4),
        compiler_params=_params(),
    )(chip, own, recv, w, m, v)


def _place():
    return lax.axis_index("x"), lax.axis_index("y"), lax.axis_index("c")


def _gather_steps(x_refs, out_refs, send_sems, recv_sems):
    n = len(x_refs)
    x, y, c = _place()
    me, sibling = (x, y, c), (x, y, 1 - c)
    chips = [(1 - x, y), (x, 1 - y), (1 - x, 1 - y)]

    def rows(a, px, py, pc):
        return out_refs[a].at[4 * px + 2 * py + pc]

    def copy(a, k, block, to, src=None):
        return pltpu.make_async_remote_copy(
            src_ref=rows(a, *block) if src is None else src, dst_ref=rows(a, *block),
            send_sem=send_sems.at[a, k], recv_sem=recv_sems.at[a, k], device_id=to, device_id_type=MESH)

    def first(a):
        return [copy(a, 0, me, sibling, src=x_refs[a])] + [
            copy(a, 1 + j, me, (*chip, c), src=x_refs[a]) for j, chip in enumerate(chips)]

    def passed(a, j):
        return copy(a, 4 + j, (*chips[j], c), sibling)

    def start():
        for a in range(n):
            for cp in first(a):
                cp.start()

    def relay():
        for j, chip in enumerate(chips):
            for a in range(n):
                copy(a, 1 + j, (*chip, c), me).wait_recv()
                passed(a, j).start()

    def finish():
        for a in range(n):
            copy(a, 0, sibling, me).wait_recv()
            for j, chip in enumerate(chips):
                copy(a, 4 + j, (*chip, 1 - c), me).wait_recv()
        for a in range(n):
            for cp in first(a) + [passed(a, j) for j in range(3)]:
                cp.wait_send()

    return start, relay, finish


def _all_gather_tree(arrs, name, gather=()):
    n, extra = len(arrs), len(gather)

    def body(*refs):
        x_refs, out_refs = refs[:n], refs[n + extra:2 * n + extra]
        sems = refs[2 * (n + extra):]
        send_sems, recv_sems = sems[:2]
        if extra:
            g_start, g_relay, g_finish = _small_gather_steps(
                refs[n:n + extra], refs[2 * n + extra:2 * (n + extra)], *sems[2:])
            g_start()
        x, y, c = _place()
        me, sibling = (x, y, c), (x, y, 1 - c)
        xn, yn, dg = (1 - x, y), (x, 1 - y), (1 - x, 1 - y)

        def rows(a, px, py, pc):
            return out_refs[a].at[4 * px + 2 * py + pc]

        def copy(a, k, block, to, src=None):
            return pltpu.make_async_remote_copy(
                src_ref=rows(a, *block) if src is None else src, dst_ref=rows(a, *block),
                send_sem=send_sems.at[a, k], recv_sem=recv_sems.at[a, k], device_id=to, device_id_type=MESH)

        def own(a):
            return [copy(a, 0, me, sibling, src=x_refs[a]), copy(a, 1, me, (*xn, c), src=x_refs[a]),
                    copy(a, 2, me, (*yn, c), src=x_refs[a])]

        def north_hands_on(a):
            return copy(a, 3, (*xn, c), (*yn, c))

        def south_hands_on(a):
            return copy(a, 3, (*yn, c), (*xn, c))

        def to_sibling(a):
            return [copy(a, 4, (*xn, c), sibling), copy(a, 5, (*yn, c), sibling), copy(a, 6, (*dg, c), sibling)]

        for a in range(n):
            for cp in own(a):
                cp.start()
        for a in range(n):
            copy(a, 1, (*xn, c), me).wait_recv()
            to_sibling(a)[0].start()

        @pl.when(c == 1)
        def _():
            for a in range(n):
                north_hands_on(a).start()

        for a in range(n):
            copy(a, 2, (*yn, c), me).wait_recv()
            to_sibling(a)[1].start()

        @pl.when(c == 0)
        def _():
            for a in range(n):
                south_hands_on(a).start()

        for a in range(n):
            copy(a, 3, (*dg, c), me).wait_recv()
            to_sibling(a)[2].start()
        for a in range(n):
            copy(a, 0, sibling, me).wait_recv()
            copy(a, 4, (*xn, 1 - c), me).wait_recv()
            copy(a, 5, (*yn, 1 - c), me).wait_recv()
            copy(a, 6, (*dg, 1 - c), me).wait_recv()
        for a in range(n):
            for cp in own(a) + to_sibling(a):
                cp.wait_send()

        @pl.when(c == 1)
        def _():
            for a in range(n):
                north_hands_on(a).wait_send()

        @pl.when(c == 0)
        def _():
            for a in range(n):
                south_hands_on(a).wait_send()

        if extra:
            g_relay()
            g_finish()

    return pl.pallas_call(
        body, name=name,
        out_shape=[jax.ShapeDtypeStruct((N_DEV,) + t.shape, t.dtype) for t in list(arrs) + list(gather)],
        in_specs=[ANY] * (n + extra), out_specs=[ANY] * (n + extra),
        scratch_shapes=[pltpu.SemaphoreType.DMA((n, 7)), pltpu.SemaphoreType.DMA((n, 7))]
        + (_small_gather_scratch(extra) if extra else []),
    )(*arrs, *gather)


def _all_gather(arrs, name, own=True):
    n = len(arrs)

    def body(*refs):
        x_refs, out_refs = refs[:n], refs[n:2 * n]
        send_sems, recv_sems, local_sems = refs[2 * n:]
        me = 4 * lax.axis_index("x") + 2 * lax.axis_index("y") + lax.axis_index("c")
        mine = [pltpu.make_async_copy(x_refs[a], out_refs[a].at[me], local_sems.at[a]) for a in range(n)] if own else []
        for cp in mine:
            cp.start()
        for step in _gather_steps(x_refs, out_refs, send_sems, recv_sems):
            step()
        for cp in mine:
            cp.wait()

    return pl.pallas_call(
        body, name=name, out_shape=[jax.ShapeDtypeStruct((N_DEV,) + t.shape, t.dtype) for t in arrs],
        in_specs=[ANY] * n, out_specs=[ANY] * n,
        scratch_shapes=[pltpu.SemaphoreType.DMA((n, 7)), pltpu.SemaphoreType.DMA((n, 7)),
                        pltpu.SemaphoreType.DMA((n,))],
    )(*arrs)


def _pair_exchange_steps(g_refs, out_refs, send_sems, recv_sems):
    x, y, c = _place()

    def give():
        return [pltpu.make_async_remote_copy(
            src_ref=g_refs[a].at[pl.ds(0, 4), 1 - c], dst_ref=out_refs[a], send_sem=send_sems.at[a],
            recv_sem=recv_sems.at[a], device_id=(x, y, 1 - c), device_id_type=MESH) for a in range(len(g_refs))]

    def start():
        for cp in give():
            cp.start()

    def finish():
        for cp in give():
            cp.wait()

    return start, finish


def _pair_exchange(arrs, name):
    n = len(arrs)

    def body(*refs):
        for step in _pair_exchange_steps(refs[:n], refs[n:2 * n], *refs[2 * n:]):
            step()

    return pl.pallas_call(
        body, name=name,
        out_shape=[jax.ShapeDtypeStruct((4,) + t.shape[2:], t.dtype) for t in arrs],
        in_specs=[ANY] * n, out_specs=[ANY] * n,
        scratch_shapes=[pltpu.SemaphoreType.DMA((n,)), pltpu.SemaphoreType.DMA((n,))],
    )(*arrs)


def _chip_exchange_steps(p_refs, out_refs, send_sems, recv_sems):
    x, y, c = _place()
    chips = [(1 - x, y), (x, 1 - y), (1 - x, 1 - y)]

    def copies():
        return [pltpu.make_async_remote_copy(
            src_ref=p_refs[a].at[2 * px + py], dst_ref=out_refs[a].at[j], send_sem=send_sems.at[a, j],
            recv_sem=recv_sems.at[a, j], device_id=(px, py, c), device_id_type=MESH)
            for a in range(len(p_refs)) for j, (px, py) in enumerate(chips)]

    def start():
        for cp in copies():
            cp.start()

    def finish():
        for cp in copies():
            cp.wait()

    return start, finish


def _small_gather_steps(x_refs, out_refs, send_sems, recv_sems, local_sems):
    me = 4 * lax.axis_index("x") + 2 * lax.axis_index("y") + lax.axis_index("c")
    start, relay, finish = _gather_steps(x_refs, out_refs, send_sems, recv_sems)

    def mine():
        return [pltpu.make_async_copy(x_refs[a], out_refs[a].at[me], local_sems.at[a]) for a in range(len(x_refs))]

    def start_all():
        for cp in mine():
            cp.start()
        start()

    def finish_all():
        finish()
        for cp in mine():
            cp.wait()

    return start_all, relay, finish_all


def _small_gather_scratch(k):
    return [pltpu.SemaphoreType.DMA((k, 7)), pltpu.SemaphoreType.DMA((k, 7)), pltpu.SemaphoreType.DMA((k,))]


def _chip_exchange(arrs, name, gather=()):
    n, k = len(arrs), len(gather)

    def body(*refs):
        ins, outs, sems = refs[:n + k], refs[n + k:2 * (n + k)], refs[2 * (n + k):]
        start, finish = _chip_exchange_steps(ins[:n], outs[:n], *sems[:2])
        if k:
            g_start, g_relay, g_finish = _small_gather_steps(ins[n:], outs[n:], *sems[2:])
            g_start()
        start()
        if k:
            g_relay()
        finish()
        if k:
            g_finish()

    return pl.pallas_call(
        body, name=name,
        out_shape=[jax.ShapeDtypeStruct((3,) + t.shape[1:], t.dtype) for t in arrs]
        + [jax.ShapeDtypeStruct((N_DEV,) + t.shape, t.dtype) for t in gather],
        in_specs=[ANY] * (n + k), out_specs=[ANY] * (n + k),
        scratch_shapes=[pltpu.SemaphoreType.DMA((n, 3)), pltpu.SemaphoreType.DMA((n, 3))]
        + (_small_gather_scratch(k) if k else []),
    )(*arrs, *gather)


class _Rider:
    def __init__(self, arrays, out_shape, sems, steps, relay_before_end=None):
        self.arrays, self.out_shape, self.scratch, self.steps = list(arrays), out_shape, sems, steps
        self.n = len(self.arrays)
        self.relay_before_end = relay_before_end

    def specs(self, in_specs, out_specs, out_shape, scratch):
        extra = [ANY] * self.n
        return in_specs + extra, out_specs + extra, out_shape + self.out_shape, scratch + self.scratch

    def split(self, refs, n_in, n_out):
        k = self.n
        a, b = n_in + k, n_in + k + n_out
        return refs[:n_in] + refs[a:b] + refs[b + k:-2], (refs[n_in:a], refs[b:b + k], refs[-2:])

    def head(self, mine, step):
        pl.when(step == 0)(self.steps(mine[0], mine[1], *mine[2])[0])

    def tail(self, mine, step, nsteps):
        steps = self.steps(mine[0], mine[1], *mine[2])
        if self.relay_before_end is not None:
            pl.when(step == nsteps - 1 - self.relay_before_end)(steps[1])
        pl.when(step == nsteps - 1)(steps[-1])


def _gather_rider(arrs, relay_before_end=2):
    n = len(arrs)
    return _Rider(arrs, [jax.ShapeDtypeStruct((N_DEV,) + t.shape, t.dtype) for t in arrs],
                  [pltpu.SemaphoreType.DMA((n, 7)), pltpu.SemaphoreType.DMA((n, 7))], _gather_steps,
                  relay_before_end)


def _pair_exchange_rider(arrs):
    n = len(arrs)
    return _Rider(arrs, [jax.ShapeDtypeStruct((4,) + t.shape[2:], t.dtype) for t in arrs],
                  [pltpu.SemaphoreType.DMA((n,)), pltpu.SemaphoreType.DMA((n,))], _pair_exchange_steps)


def _chip_exchange_rider(arrs):
    n = len(arrs)
    return _Rider(arrs, [jax.ShapeDtypeStruct((3,) + t.shape[1:], t.dtype) for t in arrs],
                  [pltpu.SemaphoreType.DMA((n, 3)), pltpu.SemaphoreType.DMA((n, 3))], _chip_exchange_steps)


def _rope_tables(positions):
    inv_freq = ROPE_THETA ** (-jnp.arange(0, HEAD_DIM, 2, dtype=F32) / HEAD_DIM)
    ang = positions.astype(F32)[:, None] * inv_freq
    cos, sin = jnp.cos(ang), jnp.sin(ang)
    return jnp.tile(cos, (1, 4)), jnp.tile(jnp.concatenate([-sin, sin], axis=1), (1, 2))


class _GradReducer:
    def __init__(self):
        self.core = lax.axis_index("c").astype(jnp.int32).reshape(1)
        self.own, self.others, self.waiting, self.riding = {}, {}, [], []

    def pair(self, named):
        mine = self._split(named)
        self._summed(mine, _pair_exchange(list(mine.values()), "reduce_pair_" + next(iter(named))))

    def pair_rider(self, named):
        self.pairing = self._split(named)
        return _pair_exchange_rider(list(self.pairing.values()))

    def pair_landed(self, results):
        self._summed(self.pairing, results)

    @staticmethod
    def _split(named):
        return {k: g.reshape((4, 2) + g.shape[1:]) for k, g in named.items()}

    def _summed(self, mine, theirs):
        for (k, g), r in zip(mine.items(), theirs):
            self.own[k] = _sum_pair(self.core, g, r, "sum_pair_" + k)
        self.waiting += list(mine)

    def rider(self):
        self.riding, self.waiting = self.waiting, []
        return _chip_exchange_rider([self.own[k] for k in self.riding])

    def landed(self, results):
        self.others.update(zip(self.riding, results))

    def flush(self, name, gather=()):
        keys, self.waiting = self.waiting, []
        res = _chip_exchange([self.own[k] for k in keys], name, gather=gather)
        self.others.update(zip(keys, res))
        return res[len(keys):]


def _by_owner(g):
    if g.ndim == 3:
        return g if g.shape[0] == N_DEV else g.reshape(N_DEV, g.shape[1] * g.shape[0] // N_DEV, g.shape[2])
    return g.reshape(N_DEV, g.shape[0] // N_DEV, g.shape[1])


def _local_step(x, target, positions, mod, small, W, late=None, red=None):
    S, D = x.shape
    gains =jnp.stack([small["g1"], small["g2"], small["g3"]])[:, None, :]
    v1, v2, v3 = jnp.pad(jnp.concatenate([gains, mod.reshape(3, 3, D)], axis=1), ((0, 0), (0, 4), (0, 0)))
    vf = jnp.pad(small["gf"][None], ((0, 7), (0, 0)))
    cos, sin = _rope_tables(positions)
    wbd = jax.scipy.linalg.block_diag(*[small["w_pool"][k] for k in range(4)]).astype(BF16)
    pscale = small["pool_scale"].reshape(1, GW)

    if late is None:
        h1, u1, ab1, act1, f1 = _ffn_fwd(x, v1, W["w1in"], W["w1out"], "ffn1_fwd")
    else:
        (h1, u1, ab1, act1, f1), landed = _ffn_fwd(x, v1, W["w1in"], W["w1out"], "ffn1_fwd", rider=late[0])
        W = {**W, **late[1](landed)}
    u2, p, gates, *qkv = _mix_in_fwd(h1, v2, cos, sin, W["win"], "mix_in_fwd")
    dpool, ypool = _pool_fwd(p, wbd, pscale, "pool_fwd")
    o3, lse3 = [], []
    for gi in range(len(DILATIONS)):
        o, lse = _attn_fwd(qkv[gi], qkv[3 + gi], qkv[6 + gi], f"attn_fwd_{gi}")
        o3.append(o)
        lse3.append(lse)
    h2, ya, merged, tm = _mix_out_fwd(h1, v2, gates, ypool, o3, lse3, W["wpb"], W["wab"], W["wout"], "mix_out_fwd")
    dh3, u3, ab3, act3, f3, loss_blk, dgf = _ffn_fwd(h2, v3, W["w2in"], W["w2out"], "ffn2_fwd", final=(target, vf))

    dh2, dab3, df3, red3 = _ffn_bwd(dh3, h2, f3, ab3, v3, W["w2in"], W["w2out"], "ffn2_bwd")
    half_f = ab3.shape[2] // 2
    G = {"w2in": _by_owner(_wgrad(dab3, u3, "wgrad_2in", tm=half_f))}
    mix_out_args = (dh2, tm, v2, gates, ypool, o3, lse3, W["wpb"], W["wab"], W["wout"], "mix_out_bwd")
    if red is None:
        G["w2out"] = _by_owner(_wgrad(act3, df3, "wgrad_2out", tm=half_f))
        mix_out = _mix_out_bwd(*mix_out_args)
    else:
        g2out, landed = _wgrad(act3, df3, "wgrad_2out", tm=half_f, rider=red.pair_rider({"w2in": G["w2in"]}))
        red.pair_landed(landed)
        G["w2out"] = _by_owner(g2out)
        red.pair({"w2out": G["w2out"]})
        mix_out, landed = _mix_out_bwd(*mix_out_args, rider=red.rider())
        red.landed(landed)
    (dtm, dgl, dypb, dyab, dyp, do0, do1, do2, c0, c1, c2, red2o) = mix_out
    dq3, dk3, dv3 = [], [], []
    for gi, (do, ct) in enumerate(zip((do0, do1, do2), (c0, c1, c2))):
        dq, dk, dv = _attn_bwd(qkv[gi], qkv[3 + gi], qkv[6 + gi], do, lse3[gi], ct, f"attn_bwd_{gi}")
        dq3.append(dq)
        dk3.append(dk)
        dv3.append(dv)
    dp, dwbd, dps = _pool_bwd(dyp, dpool, wbd, pscale, "pool_bwd")
    dh1, dproj, red2i = _mix_in_bwd(dh2, h1, v2, cos, sin, dp, dq3 + dk3 + dv3, dgl, W["win"], "mix_in_bwd")
    G["win"] = _by_owner(_wgrad(dproj, u2, "wgrad_in", tm=1152))
    G["wpb"] = _full_to_cols(_wgrad(ypool, dypb, "wgrad_pb"))
    G["wab"] = _full_to_cols(_wgrad(ya, dyab, "wgrad_ab"))
    if red is None:
        G["wout"] = _by_owner(_wgrad(merged, dtm, "wgrad_out"))
    else:
        gout, landed = _wgrad(merged, dtm, "wgrad_out", rider=red.pair_rider({k: G[k] for k in ("win", "wpb", "wab")}))
        red.pair_landed(landed)
        G["wout"] = _by_owner(gout)
        red.pair({"wout": G["wout"]})
    dx, dab1, df1, red1 = _ffn_bwd(dh1, x, f1, ab1, v1, W["w1in"], W["w1out"], "ffn1_bwd")
    dmod = jnp.concatenate([red1[:3], (red2i + red2o)[:3], red3[:3]])
    dsmall = {
        "g1": red1[3], "g2": red2i[3], "g3": red3[3], "gf": dgf[0],
        "w_pool": jnp.stack([dwbd[k * 64:(k + 1) * 64, k * 64:(k + 1) * 64] for k in range(4)]),
        "pool_scale": dps[0],
    }
    if red is None:
        G["w1in"] = _by_owner(_wgrad(dab1, u1, "wgrad_1in", tm=half_f))
        G["w1out"] = _by_owner(_wgrad(act1, df1, "wgrad_1out", tm=half_f))
        return loss_blk[0, 0], dx, G, dmod, dsmall
    g1in, landed = _wgrad(dab1, u1, "wgrad_1in", tm=half_f, rider=red.rider())
    red.landed(landed)
    red.pair({"w1in": _by_owner(g1in)})
    g1out, landed = _wgrad(act1, df1, "wgrad_1out", tm=half_f, rider=red.rider())
    red.landed(landed)
    red.pair({"w1out": _by_owner(g1out)})
    part = _pack_small(dict(b_ada=dmod, g_norm_ffn1=dsmall["g1"], g_norm_mix=dsmall["g2"], g_norm_ffn2=dsmall["g3"],
                            g_final=dsmall["gf"], w_pool=dsmall["w_pool"], pool_scale=dsmall["pool_scale"]),
                       loss_blk[0, 0])
    (parts,) = red.flush("reduce_chips_w1out", gather=[part])
    return dx, parts


SHARDED = ("w_ffn1_in", "w_ffn1_out", "w_in", "w_pool_branch", "w_attn_branch", "w_out", "w_ffn2_in", "w_ffn2_out")
TRANSPOSED = ("w_ffn1_in", "w_in", "w_ffn2_in")
FIRST = ("w_ffn1_in", "w_ffn1_out")
LATER = tuple(n for n in SHARDED if n not in FIRST)
GRAD_KEY = dict(w_ffn1_in="w1in", w_ffn1_out="w1out", w_in="win", w_pool_branch="wpb", w_attn_branch="wab",
                w_out="wout", w_ffn2_in="w2in", w_ffn2_out="w2out")


def _cols_to_full(g):
    return g.transpose(1, 0, 2).reshape(g.shape[1], N_DEV * g.shape[2])


def _full_to_cols(t):
    return t.reshape(t.shape[0], N_DEV, t.shape[1] // N_DEV).transpose(1, 0, 2)


SMALL = (("b_ada", 72), ("g_norm_ffn1", 8), ("g_norm_mix", 8), ("g_norm_ffn2", 8), ("g_final", 8),
         ("w_pool", 128), ("pool_scale", 8))


def _pack_small(vals, loss):
    rows = []
    for name, nrows in SMALL:
        t = vals[name].reshape(-1, 128)
        rows.append(jnp.pad(t, ((0, nrows - t.shape[0]), (0, 0))))
    rows.append(jnp.full((8, 128), loss, F32))
    return jnp.concatenate(rows)


def _unpack_small(slab, shapes):
    out, off = {}, 0
    for name, nrows in SMALL:
        used = 1
        for d in shapes[name]:
            used *= d
        out[name] = slab[off:off + used // 128].reshape(shapes[name])
        off += nrows
    return out, slab[off, 0]


def _as_2d(t):
    return t.reshape(-1, t.shape[-1])


def kernel(x, c, positions, w_ada, b_ada, g_norm_ffn1, w_ffn1_in, w_ffn1_out, g_norm_mix, w_in, w_pool, pool_scale, w_pool_branch, w_attn_branch, w_out, g_norm_ffn2, w_ffn2_in, w_ffn2_out, g_final, loss_target, m_w_ada, m_b_ada, m_g_norm_ffn1, m_w_ffn1_in, m_w_ffn1_out, m_g_norm_mix, m_w_in, m_w_pool, m_pool_scale, m_w_pool_branch, m_w_attn_branch, m_w_out, m_g_norm_ffn2, m_w_ffn2_in, m_w_ffn2_out, m_g_final, v_w_ada, v_b_ada, v_g_norm_ffn1, v_w_ffn1_in, v_w_ffn1_out, v_g_norm_mix, v_w_in, v_w_pool, v_pool_scale, v_w_pool_branch, v_w_attn_branch, v_w_out, v_g_norm_ffn2, v_w_ffn2_in, v_w_ffn2_out, v_g_final):
    names = ["w_ada", "b_ada", "g_norm_ffn1", "w_ffn1_in", "w_ffn1_out", "g_norm_mix", "w_in", "w_pool", "pool_scale",
             "w_pool_branch", "w_attn_branch", "w_out", "g_norm_ffn2", "w_ffn2_in", "w_ffn2_out", "g_final"]
    w = dict(w_ada=w_ada, b_ada=b_ada, g_norm_ffn1=g_norm_ffn1, w_ffn1_in=w_ffn1_in, w_ffn1_out=w_ffn1_out,
             g_norm_mix=g_norm_mix, w_in=w_in, w_pool=w_pool, pool_scale=pool_scale, w_pool_branch=w_pool_branch,
             w_attn_branch=w_attn_branch, w_out=w_out, g_norm_ffn2=g_norm_ffn2, w_ffn2_in=w_ffn2_in,
             w_ffn2_out=w_ffn2_out, g_final=g_final)
    m = dict(w_ada=m_w_ada, b_ada=m_b_ada, g_norm_ffn1=m_g_norm_ffn1, w_ffn1_in=m_w_ffn1_in, w_ffn1_out=m_w_ffn1_out,
             g_norm_mix=m_g_norm_mix, w_in=m_w_in, w_pool=m_w_pool, pool_scale=m_pool_scale,
             w_pool_branch=m_w_pool_branch, w_attn_branch=m_w_attn_branch, w_out=m_w_out, g_norm_ffn2=m_g_norm_ffn2,
             w_ffn2_in=m_w_ffn2_in, w_ffn2_out=m_w_ffn2_out, g_final=m_g_final)
    v = dict(w_ada=v_w_ada, b_ada=v_b_ada, g_norm_ffn1=v_g_norm_ffn1, w_ffn1_in=v_w_ffn1_in, w_ffn1_out=v_w_ffn1_out,
             g_norm_mix=v_g_norm_mix, w_in=v_w_in, w_pool=v_w_pool, pool_scale=v_pool_scale,
             w_pool_branch=v_w_pool_branch, w_attn_branch=v_w_attn_branch, w_out=v_w_out, g_norm_ffn2=v_g_norm_ffn2,
             w_ffn2_in=v_w_ffn2_in, w_ffn2_out=v_w_ffn2_out, g_final=v_g_final)
    shapes = {n: w[n].shape for n in names}
    me = 4 * lax.axis_index("x") + 2 * lax.axis_index("y") + lax.axis_index("c")
    D = x.shape[-1]
    n_mod = w_ada.shape[-1] * N_DEV // D

    def local(t, name):
        return t[name][0].T if name in TRANSPOSED else t[name][0]

    shards = {name: local(w, name).astype(BF16) for name in SHARDED}

    def gather_done(names, fulls):
        return {name: lax.dynamic_update_index_in_dim(full, shards[name], me, axis=0)
                for name, full in zip(names, fulls)}

    def ffn_weights(g, pre):
        return {"w%sin" % pre: g["w_ffn%s_in" % pre].reshape(2, -1, D),
                "w%sout" % pre: g["w_ffn%s_out" % pre].reshape(-1, D)}

    def later_weights(fulls):
        g = gather_done(LATER, fulls)
        return dict(win=g["w_in"].reshape(-1, D), wpb=_cols_to_full(g["w_pool_branch"]),
                    wab=_cols_to_full(g["w_attn_branch"]), wout=g["w_out"].reshape(D, D), **ffn_weights(g, "2"))

    *first, c_all = _all_gather_tree([shards[n] for n in FIRST], "gather_ffn1", gather=[c.reshape(D // 128, 128)])
    W = ffn_weights(gather_done(FIRST, first), "1")

    ada_cols = w_ada.shape[-1]
    b_mine = lax.dynamic_slice_in_dim(b_ada, me * ada_cols, ada_cols, axis=1)
    cond, mod_part = _ada_mod(c_all.reshape(N_DEV, D), w_ada[0], b_mine, "ada_mod")
    (mod_all,) = _all_gather([mod_part.reshape(-1, 128)], "gather_mod")
    mod_all = mod_all.reshape(N_DEV, N_DEV, ada_cols)
    mod = lax.dynamic_index_in_dim(mod_all, me, axis=1, keepdims=False).reshape(n_mod, D)

    small = dict(g1=g_norm_ffn1[0], g2=g_norm_mix[0], g3=g_norm_ffn2[0], gf=g_final, w_pool=w_pool[0],
                 pool_scale=pool_scale[0])
    red = _GradReducer()
    dx, parts = _local_step(
        x[0], loss_target[0], positions[0], mod, small, W,
        late=(_gather_rider([shards[n] for n in LATER]), later_weights), red=red)
    chip = (2 * lax.axis_index("x") + lax.axis_index("y")).astype(jnp.int32).reshape(1)

    gsmall, loss = _unpack_small(_sum_slots(parts, "sum_small"), shapes)
    rows_mine = ada_cols // 128
    dmod_mine = lax.dynamic_slice_in_dim(parts, me * rows_mine, rows_mine, axis=1).reshape(N_DEV, ada_cols)

    grads, delta, new_m, new_v = {}, {}, {}, {}
    grads["w_ada"], delta["w_ada"], new_m["w_ada"], new_v["w_ada"] = (
        t[None] for t in _ada_grad_adamw(cond.T, dmod_mine, w_ada[0], m_w_ada[0], v_w_ada[0], "ada_grad_adamw"))
    for name in SHARDED:
        key = GRAD_KEY[name]
        res = _sum_adamw(chip, red.own[key], red.others[key], local(w, name), local(m, name), local(v, name),
                         "adamw_" + name)
        grads[name], delta[name], new_m[name], new_v[name] = (
            (t.T if name in TRANSPOSED else t)[None] for t in res)
    small_names = [name for name, _ in SMALL]
    res = _adamw_many(*([_as_2d(t[name]) for name in small_names] for t in (w, gsmall, m, v)), "adamw_small")
    for dst, vals in zip((delta, new_m, new_v), res):
        dst.update({name: t.reshape(shapes[name]) for name, t in zip(small_names, vals)})
    grads.update(gsmall)

    return (loss, dx[None], *[grads[n] for n in names], *[delta[n] for n in names],
            *[new_m[n] for n in names], *[new_v[n] for n in names])
```

```python
import functools

import jax
import jax.numpy as jnp
from jax import lax
from jax.experimental import pallas as pl
from jax.experimental.pallas import tpu as pltpu

F32 = jnp.float32
BF16 = jnp.bfloat16
MESH = pl.DeviceIdType.MESH
ANY = pl.BlockSpec(memory_space=pl.ANY)

N_DEV = 8
EPS = 1e-6
HEAD_DIM = 64
HEADS = 4
GW = HEADS * HEAD_DIM
DILATIONS = (1, 4, 16)
BAND = 128
QB = 128
POOL_WINDOWS = (2, 4, 8, 16)
HALO = 16
ROPE_THETA = 10000.0

ADAM_LR = 0.001
ADAM_B1 = 0.9
ADAM_B2 = 0.999
ADAM_EPS = 1e-08
ADAM_WD = 0.01
ADAM_STEP = 10

VMEM_LIMIT = 56 * 1024 * 1024
TS = 512
FFN_TS = 256
FFN_CHUNKS = (2816,)

NT = (((1,), (1,)), ((), ()))
TN = (((0,), (0,)), ((), ()))


def _params(**kw):
    return pltpu.CompilerParams(vmem_limit_bytes=VMEM_LIMIT, **kw)


def _dot(a, b):
    return jnp.dot(a, b, preferred_element_type=F32)


def _dot_nt(a, b):
    return lax.dot_general(a, b, NT, preferred_element_type=F32)


def _dot_tn(a, b):
    return lax.dot_general(a, b, TN, preferred_element_type=F32)


def _load_weights(pairs, sem):
    @pl.when(pl.program_id(0) == 0)
    def _():
        copies = [pltpu.make_async_copy(src, dst, sem.at[i]) for i, (src, dst) in enumerate(pairs)]
        for cp in copies:
            cp.start()
        for cp in copies:
            cp.wait()


def _norm_mod(x, g, sc, sh):
    r = lax.rsqrt(jnp.mean(x * x, axis=-1, keepdims=True) + EPS)
    xn = x * r
    y = xn * g
    return r, xn, y, y * (1.0 + sc) + sh


def _norm_mod_bwd(du, r, xn, y, g, sc):
    dsh = jnp.sum(du, axis=0, keepdims=True)
    dsc = jnp.sum(du * y, axis=0, keepdims=True)
    dy = du * (1.0 + sc)
    dg = jnp.sum(dy * xn, axis=0, keepdims=True)
    dxn = dy * g
    dx = r * (dxn - xn * jnp.mean(dxn * xn, axis=-1, keepdims=True))
    return dx, dsh, dsc, dg


def _row_tile(ts, width):
    return pl.BlockSpec((ts, width), lambda i: (i, 0))


def _const(shape):
    return pl.BlockSpec(shape, lambda *_: (0,) * len(shape))


def _ffn_chunks(Fd):
    assert sum(FFN_CHUNKS) == Fd
    edges = [sum(FFN_CHUNKS[:k]) for k in range(len(FFN_CHUNKS) + 1)]
    return [slice(a, b) for a, b in zip(edges[:-1], edges[1:])]
def _final_tile(x, g, target):
    r = lax.rsqrt(jnp.mean(x * x, axis=-1, keepdims=True) + EPS)
    xn = x * r
    err = xn * g - target
    loss = 0.5 * jnp.sum(jnp.mean(err * err, axis=-1, keepdims=True))
    dy = err * (1.0 / x.shape[-1])
    dg = jnp.sum(dy * xn, axis=0, keepdims=True)
    dxn = dy * g
    return r * (dxn - xn * jnp.mean(dxn * xn, axis=-1, keepdims=True)), loss, dg


def _ffn_fwd(h, vec, win, wout, name, rider=None, final=None):
    TS = FFN_TS
    S, D = h.shape
    _, Fd, _ = win.shape
    n_in, n_out = (6, 7) if final else (4, 5)

    def body(*refs):
        if rider is None:
            return compute(*refs)
        host, mine = rider.split(refs, n_in, n_out)
        rider.head(mine, pl.program_id(0))
        compute(*host)
        rider.tail(mine, pl.program_id(0), S // TS)

    def compute(*refs):
        h_ref, vec_ref, win_hbm, wout_hbm = refs[:4]
        hn_ref, u_ref, ab_ref, act_ref, f_ref = refs[n_in:n_in + 5]
        win_v, wout_v, sem = refs[n_in + n_out:]
        _load_weights([(win_hbm, win_v), (wout_hbm, wout_v)], sem)
        x = h_ref[...]
        g, sh, sc, gt = (vec_ref[k:k + 1, :] for k in range(4))
        _, _, _, u = _norm_mod(x, g, sc, sh)
        ub = u.astype(BF16)
        u_ref[...] = ub
        acc = jnp.zeros((TS, D), F32)
        for sl in _ffn_chunks(Fd):
            a = _dot_nt(ub, win_v[0, sl, :])
            b = _dot_nt(ub, win_v[1, sl, :])
            act = ((a * jax.nn.sigmoid(a)) * b).astype(BF16)
            ab_ref[0, :, sl] = a.astype(BF16)
            ab_ref[1, :, sl] = b.astype(BF16)
            act_ref[:, sl] = act
            acc = acc + _dot(act, wout_v[sl, :])
        f_ref[...] = acc.astype(BF16)
        hn = x + (0.5 * gt) * acc
        if not final:
            hn_ref[...] = hn
            return
        t_ref, gf_ref = refs[4:6]
        loss_ref, dgf_ref = refs[n_in + 5:n_in + 7]

        @pl.when(pl.program_id(0) == 0)
        def _():
            loss_ref[...] = jnp.zeros_like(loss_ref)
            dgf_ref[...] = jnp.zeros_like(dgf_ref)

        hn_ref[...], loss, dg = _final_tile(hn, gf_ref[0:1, :], t_ref[...])
        loss_ref[...] += loss
        dgf_ref[0:1, :] += dg

    specs = (
        [_row_tile(TS, D), _const((8, D)), ANY, ANY] + ([_row_tile(TS, D), _const((8, D))] if final else []),
        [_row_tile(TS, D), _row_tile(TS, D), pl.BlockSpec((2, TS, Fd), lambda i: (0, i, 0)),
         _row_tile(TS, Fd), _row_tile(TS, D)] + ([_const((8, 128)), _const((8, D))] if final else []),
        [jax.ShapeDtypeStruct((S, D), F32), jax.ShapeDtypeStruct((S, D), BF16),
         jax.ShapeDtypeStruct((2, S, Fd), BF16), jax.ShapeDtypeStruct((S, Fd), BF16),
         jax.ShapeDtypeStruct((S, D), BF16)]
        + ([jax.ShapeDtypeStruct((8, 128), F32), jax.ShapeDtypeStruct((8, D), F32)] if final else []),
        [pltpu.VMEM(win.shape, BF16), pltpu.VMEM(wout.shape, BF16), pltpu.SemaphoreType.DMA((2,))])
    in_specs, out_specs, out_shape, scratch = specs if rider is None else rider.specs(*specs)
    outs = pl.pallas_call(
        body, name=name, grid=(S // TS,), in_specs=in_specs, out_specs=out_specs, out_shape=out_shape,
        scratch_shapes=scratch, compiler_params=_params(),
    )(h, vec, win, wout, *(final or ()), *(rider.arrays if rider else []))
    return outs if rider is None else (outs[:n_out], outs[n_out:])


def _ffn_bwd(dh, h, f, ab, vec, win, wout, name):
    TS = FFN_TS
    S, D = h.shape
    _, Fd, _ = win.shape

    def body(dh_ref, h_ref, f_ref, ab_ref, vec_ref, win_hbm, wout_hbm,
             dhp_ref, dab_ref, df_ref, red_ref, win_v, wout_v, sem):
        _load_weights([(win_hbm, win_v), (wout_hbm, wout_v)], sem)

        @pl.when(pl.program_id(0) == 0)
        def _():
            red_ref[...] = jnp.zeros_like(red_ref)

        dh_v = dh_ref[...]
        x = h_ref[...]
        g, sh, sc, gt = (vec_ref[k:k + 1, :] for k in range(4))
        dgt = jnp.sum((0.5 * f_ref[...].astype(F32)) * dh_v, axis=0, keepdims=True)
        dfb = ((0.5 * gt) * dh_v).astype(BF16)
        df_ref[...] = dfb
        du = jnp.zeros((TS, D), F32)
        for sl in _ffn_chunks(Fd):
            dact = _dot_nt(dfb, wout_v[sl, :])
            av = ab_ref[0, :, sl].astype(F32)
            bv = ab_ref[1, :, sl].astype(F32)
            sg = jax.nn.sigmoid(av)
            da = (dact * bv * (sg * (1.0 + av * (1.0 - sg)))).astype(BF16)
            db = (dact * (av * sg)).astype(BF16)
            dab_ref[0, :, sl] = da
            dab_ref[1, :, sl] = db
            du = du + _dot(da, win_v[0, sl, :]) + _dot(db, win_v[1, sl, :])
        r, xn, y, _ = _norm_mod(x, g, sc, sh)
        dx, dsh, dsc, dg = _norm_mod_bwd(du, r, xn, y, g, sc)
        dhp_ref[...] = dh_v + dx
        red_ref[0:1, :] += dsh
        red_ref[1:2, :] += dsc
        red_ref[2:3, :] += dgt
        red_ref[3:4, :] += dg

    ab_spec = pl.BlockSpec((2, TS, Fd), lambda i: (0, i, 0))
    return pl.pallas_call(
        body, name=name, grid=(S // TS,),
        in_specs=[_row_tile(TS, D), _row_tile(TS, D), _row_tile(TS, D), ab_spec, _const((8, D)), ANY, ANY],
        out_specs=[_row_tile(TS, D), ab_spec, _row_tile(TS, D), _const((8, D))],
        out_shape=[jax.ShapeDtypeStruct((S, D), F32), jax.ShapeDtypeStruct((2, S, Fd), BF16),
                   jax.ShapeDtypeStruct((S, D), BF16), jax.ShapeDtypeStruct((8, D), F32)],
        scratch_shapes=[pltpu.VMEM(win.shape, BF16), pltpu.VMEM(wout.shape, BF16), pltpu.SemaphoreType.DMA((2,))],
        compiler_params=_params(),
    )(dh, h, f, ab, vec, win, wout)


def _wgrad(x, y, name, tm=None, ts=2048, rider=None):
    xb = x.ndim == 3
    nb = x.shape[0] if xb else 0
    S, M = x.shape[-2:]
    N = y.shape[-1]
    tm = tm or M
    ts = min(ts, S)
    nk = S // ts
    grid = (max(nb, 1), M // tm, nk)

    def body(*refs):
        if rider is None:
            return compute(*refs)
        host, mine = rider.split(refs, 2, 1)
        step = (pl.program_id(0) * grid[1] + pl.program_id(1)) * grid[2] + pl.program_id(2)
        rider.head(mine, step)
        compute(*host)
        rider.tail(mine, step, grid[0] * grid[1] * grid[2])

    def compute(x_ref, y_ref, o_ref, acc):
        k = pl.program_id(2)

        @pl.when(k == 0)
        def _():
            acc[...] = jnp.zeros_like(acc)

        acc[...] += _dot_tn(x_ref[...], y_ref[...])

        @pl.when(k == nk - 1)
        def _():
            o_ref[...] = acc[...].astype(BF16)

    x_spec = (pl.BlockSpec((None, ts, tm), lambda b, i, k: (b, k, i)) if xb
              else pl.BlockSpec((ts, tm), lambda b, i, k: (k, i)))
    y_spec = pl.BlockSpec((ts, N), lambda b, i, k: (k, 0))
    if xb:
        o_spec, o_shape = pl.BlockSpec((None, tm, N), lambda b, i, k: (b, i, 0)), (nb, M, N)
    else:
        o_spec, o_shape = pl.BlockSpec((tm, N), lambda b, i, k: (i, 0)), (M, N)
    specs = ([x_spec, y_spec], [o_spec], [jax.ShapeDtypeStruct(o_shape, BF16)], [pltpu.VMEM((tm, N), F32)])
    in_specs, out_specs, out_shape, scratch = specs if rider is None else rider.specs(*specs)
    outs = pl.pallas_call(
        body, name=name, grid=grid, in_specs=in_specs, out_specs=out_specs, out_shape=out_shape,
        scratch_shapes=scratch, compiler_params=_params(),
    )(x, y, *(rider.arrays if rider else []))
    return outs[0] if rider is None else (outs[0], outs[1:])


P_OFF, Q_OFF, K_OFF, V_OFF, G_OFF = 0, 256, 1024, 1792, 2560
IN_WIDTH = 4608


def _first_half_mask(ts):
    lane = lax.broadcasted_iota(jnp.int32, (ts, 128), 1)
    return (lane % HEAD_DIM) < (HEAD_DIM // 2)


def _rope(t, cos, sin_signed, first, sign):
    partner = jnp.where(first, pltpu.roll(t, 96, 1), pltpu.roll(t, 32, 1))
    return t * cos + sign * (partner * sin_signed)


def _res_spec(r):
    return pl.BlockSpec((r, TS // r, GW), lambda i: (0, i, 0))


def _res_shape(S, r, dtype):
    return jax.ShapeDtypeStruct((r, S // r, GW), dtype)


def _to_residues(piece, out_ref, lanes, r, scr):
    if r == 1:
        out_ref[0, :, lanes] = piece.astype(out_ref.dtype)
        return
    for h in range(piece.shape[1] // 128):
        scr[h] = piece[:, h * 128:(h + 1) * 128]
        at = slice(lanes.start + h * 128, lanes.start + (h + 1) * 128)
        for res in range(r):
            out_ref[res, :, at] = scr[h, pl.ds(res, TS // r, stride=r), :].astype(out_ref.dtype)


def _from_residues(in_ref, lanes, r, scr):
    if r == 1:
        return in_ref[0, :, lanes].astype(F32)
    halves = (lanes.stop - lanes.start) // 128
    for h in range(halves):
        at = slice(lanes.start + h * 128, lanes.start + (h + 1) * 128)
        for res in range(r):
            scr[h, pl.ds(res, TS // r, stride=r), :] = in_ref[res, :, at].astype(F32)
    return scr[0] if halves == 1 else jnp.concatenate([scr[0], scr[1]], axis=1)


RES_SCRATCH = (2, TS, 128)


def _mix_in_fwd(h, vec, cos, sin, win, name):
    S, D = h.shape

    def body(h_ref, vec_ref, cos_ref, sin_ref, win_hbm, u_ref, p_ref, gates_ref, *rest):
        qkv_refs, (win_v, sem, scr) = rest[:9], rest[9:]
        _load_weights([(win_hbm, win_v)], sem)
        g, sh, sc = (vec_ref[k:k + 1, :] for k in range(3))
        _, _, _, u = _norm_mod(h_ref[...], g, sc, sh)
        ub = u.astype(BF16)
        u_ref[...] = ub
        p_ref[...] = _dot_nt(ub, win_v[P_OFF:Q_OFF, :])
        cosv, sinv = cos_ref[...], sin_ref[...]
        first = _first_half_mask(TS)
        for which, off in enumerate((Q_OFF, K_OFF, V_OFF)):
            t = _dot_nt(ub, win_v[off:off + 3 * GW, :])
            for gi in range(3):
                for half in range(2):
                    c0 = gi * GW + half * 128
                    piece = t[:, c0:c0 + 128]
                    if which < 2:
                        piece = _rope(piece, cosv, sinv, first, 1.0)
                    _to_residues(piece, qkv_refs[which * 3 + gi], slice(half * 128, (half + 1) * 128),
                                 DILATIONS[gi], scr)
        gates_ref[...] = jax.nn.sigmoid(_dot_nt(ub, win_v[G_OFF:IN_WIDTH, :])).astype(BF16)

    return pl.pallas_call(
        body, name=name, grid=(S // TS,),
        in_specs=[_row_tile(TS, D), _const((8, D)), _row_tile(TS, 128), _row_tile(TS, 128), ANY],
        out_specs=[_row_tile(TS, D), _row_tile(TS, GW), _row_tile(TS, 2 * D)] + [_res_spec(r) for r in DILATIONS] * 3,
        out_shape=[jax.ShapeDtypeStruct((S, D), BF16), jax.ShapeDtypeStruct((S, GW), F32),
                   jax.ShapeDtypeStruct((S, 2 * D), BF16)] + [_res_shape(S, r, BF16) for r in DILATIONS] * 3,
        scratch_shapes=[pltpu.VMEM((IN_WIDTH, D), BF16), pltpu.SemaphoreType.DMA((1,)), pltpu.VMEM(RES_SCRATCH, F32)],
        compiler_params=_params(),
    )(h, vec, cos, sin, win)


def _mix_in_bwd(dh, h, vec, cos, sin, dp, dqkv, dgl, win, name):
    S, D = h.shape

    def body(dh_ref, h_ref, vec_ref, cos_ref, sin_ref, dp_ref, *rest):
        dqkv_refs = rest[:9]
        dgl_ref, win_hbm, dhp_ref, dproj_ref, red_ref, win_v, sem, scr = rest[9:]
        _load_weights([(win_hbm, win_v)], sem)

        @pl.when(pl.program_id(0) == 0)
        def _():
            red_ref[...] = jnp.zeros_like(red_ref)

        cosv, sinv = cos_ref[...], sin_ref[...]
        first = _first_half_mask(TS)
        dproj_ref[:, P_OFF:Q_OFF] = dp_ref[...].astype(BF16)
        for which, off in enumerate((Q_OFF, K_OFF, V_OFF)):
            for gi in range(3):
                for half in range(2):
                    piece = _from_residues(dqkv_refs[which * 3 + gi], slice(half * 128, (half + 1) * 128),
                                           DILATIONS[gi], scr)
                    if which < 2:
                        piece = _rope(piece, cosv, sinv, first, -1.0)
                    c0 = off + gi * GW + half * 128
                    dproj_ref[:, c0:c0 + 128] = piece.astype(BF16)
        dproj_ref[:, G_OFF:IN_WIDTH] = dgl_ref[...]
        du = _dot(dproj_ref[...], win_v[...])
        g, sh, sc = (vec_ref[k:k + 1, :] for k in range(3))
        r, xn, y, _ = _norm_mod(h_ref[...], g, sc, sh)
        dx, dsh, dsc, dg = _norm_mod_bwd(du, r, xn, y, g, sc)
        dhp_ref[...] = dh_ref[...] + dx
        red_ref[0:1, :] += dsh
        red_ref[1:2, :] += dsc
        red_ref[3:4, :] += dg

    return pl.pallas_call(
        body, name=name, grid=(S // TS,),
        in_specs=[_row_tile(TS, D), _row_tile(TS, D), _const((8, D)), _row_tile(TS, 128), _row_tile(TS, 128),
                  _row_tile(TS, GW)] + [_res_spec(r) for r in DILATIONS] * 3 + [_row_tile(TS, 2 * D), ANY],
        out_specs=[_row_tile(TS, D), _row_tile(TS, IN_WIDTH), _const((8, D))],
        out_shape=[jax.ShapeDtypeStruct((S, D), F32), jax.ShapeDtypeStruct((S, IN_WIDTH), BF16),
                   jax.ShapeDtypeStruct((8, D), F32)],
        scratch_shapes=[pltpu.VMEM((IN_WIDTH, D), BF16), pltpu.SemaphoreType.DMA((1,)), pltpu.VMEM(RES_SCRATCH, F32)],
        compiler_params=_params(),
    )(dh, h, vec, cos, sin, dp, *dqkv, dgl, win)


def _pool_lanes(rows):
    lane = lax.broadcasted_iota(jnp.int32, (rows, GW), 1)
    return lane // HEAD_DIM


def _pool_window(rows):
    grp = _pool_lanes(rows)
    w = jnp.full((rows, GW), POOL_WINDOWS[0], jnp.int32)
    for k in range(1, len(POOL_WINDOWS)):
        w = jnp.where(grp == k, POOL_WINDOWS[k], w)
    return grp, w


def _pool_fwd(p, wbd, scale, name, ts=1024):
    S = p.shape[0]
    ext = ts + HALO

    def body(pc_ref, ph_ref, wbd_ref, sc_ref, d_ref, y_ref):
        i = pl.program_id(0)
        cur = pc_ref[...]
        halo = jnp.where(i > 0, ph_ref[...], 0.0)
        s = jnp.concatenate([halo, cur], axis=0)
        grp, w = _pool_window(ext)
        sel = jnp.zeros((ext, GW), F32)
        for k, wk in enumerate(POOL_WINDOWS):
            s = s + pltpu.roll(s, wk // 2, 0)
            sel = jnp.where(grp == k, s, sel)
        t = i * ts + lax.broadcasted_iota(jnp.int32, (ts, GW), 0)
        count = jnp.minimum(t + 1, w[HALO:]).astype(F32)
        d = (sel[HALO:] / count - cur).astype(BF16)
        d_ref[...] = d
        y_ref[...] = (_dot(d, wbd_ref[...]) * sc_ref[...]).astype(BF16)

    return pl.pallas_call(
        body, name=name, grid=(S // ts,),
        in_specs=[_row_tile(ts, GW),
                  pl.BlockSpec((HALO, GW), lambda i: (jnp.maximum(i * (ts // HALO) - 1, 0), 0)),
                  _const((GW, GW)), _const((1, GW))],
        out_specs=[_row_tile(ts, GW), _row_tile(ts, GW)],
        out_shape=[jax.ShapeDtypeStruct((S, GW), BF16), jax.ShapeDtypeStruct((S, GW), BF16)],
        compiler_params=_params(),
    )(p, p, wbd, scale)


def _pool_bwd(dy, d, wbd, scale, name, ts=1024):
    S = dy.shape[0]
    ext = ts + HALO
    nsteps = S // ts
    last_halo = S // HALO - 1

    def body(dyc_ref, dyh_ref, d_ref, wbd_ref, sc_ref, dp_ref, dw_ref, ds_ref):
        i = pl.program_id(0)

        @pl.when(i == 0)
        def _():
            dw_ref[...] = jnp.zeros_like(dw_ref)
            ds_ref[...] = jnp.zeros_like(ds_ref)

        dyc = dyc_ref[...]
        dyh = jnp.where(i < nsteps - 1, dyh_ref[...], 0.0)
        dys = (jnp.concatenate([dyc, dyh], axis=0) * sc_ref[...]).astype(BF16)
        dd = _dot_nt(dys, wbd_ref[...])
        grp, w = _pool_window(ext)
        t = i * ts + lax.broadcasted_iota(jnp.int32, (ext, GW), 0)
        s = dd / jnp.minimum(t + 1, w).astype(F32)
        sel = jnp.zeros((ext, GW), F32)
        for k, wk in enumerate(POOL_WINDOWS):
            s = s + pltpu.roll(s, ext - wk // 2, 0)
            sel = jnp.where(grp == k, s, sel)
        dp_ref[...] = sel[:ts] - dd[:ts]
        dv = d_ref[...]
        z = _dot(dv, wbd_ref[...])
        ds_ref[0:1, :] += jnp.sum(dyc * z, axis=0, keepdims=True)
        dw_ref[...] += _dot_tn(dv, dys[:ts])

    return pl.pallas_call(
        body, name=name, grid=(nsteps,),
        in_specs=[_row_tile(ts, GW),
                  pl.BlockSpec((HALO, GW), lambda i: (jnp.minimum((i + 1) * (ts // HALO), last_halo), 0)),
                  _row_tile(ts, GW), _const((GW, GW)), _const((1, GW))],
        out_specs=[_row_tile(ts, GW), _const((GW, GW)), _const((8, GW))],
        out_shape=[jax.ShapeDtypeStruct((S, GW), F32), jax.ShapeDtypeStruct((GW, GW), F32),
                   jax.ShapeDtypeStruct((8, GW), F32)],
        compiler_params=_params(),
    )(dy, dy, d, wbd, scale)


def _head_id(rows):
    return lax.broadcasted_iota(jnp.int32, (rows, GW), 1) // HEAD_DIM


def _stack_heads(t, hid):
    return jnp.concatenate([jnp.where(hid == h, t, jnp.zeros_like(t)) for h in range(HEADS)], axis=0)


def _unstack_heads(t_all, hid):
    out = jnp.zeros((QB, GW), F32)
    for h in range(HEADS):
        out = jnp.where(hid == h, t_all[h * QB:(h + 1) * QB], out)
    return out


def _band_mask(n):
    row = lax.broadcasted_iota(jnp.int32, (HEADS * QB, 2 * QB), 0) % QB
    col = lax.broadcasted_iota(jnp.int32, (HEADS * QB, 2 * QB), 1)
    rel = row + QB - col
    return (rel >= 0) & (rel <= BAND) & ((col >= QB) | (n > 0))


FWD_STREAMS, BWD_STREAMS = 16, 8


def _streams(r, nb, most):
    if r > 1:
        ns = min(r, most)
        return nb, [(lambda rb, l=l: ns * rb + l, 0) for l in range(ns)]
    ns = min(most, nb)
    return nb // ns, [(lambda rb: 0, l * (nb // ns)) for l in range(ns)]


def _attn_fwd(q, k, v, name):
    r, L, _ = q.shape
    nbs, streams = _streams(r, L // QB, FWD_STREAMS)
    ns = len(streams)
    grid = (max(r // ns, 1), nbs)

    def cur(res, off):
        return pl.BlockSpec((None, QB, GW), lambda rb, n: (res(rb), n + off, 0))

    def prev(res, off):
        return pl.BlockSpec((None, QB, GW), lambda rb, n: (res(rb), jnp.maximum(n + off - 1, 0), 0))

    def body(*refs):
        n = pl.program_id(1)
        hid = _head_id(QB)
        o_ref, lse_ref = refs[5 * len(streams):]
        for l, (_, off) in enumerate(streams):
            q_ref, kp_ref, kc_ref, vp_ref, vc_ref = refs[5 * l:5 * l + 5]
            qs = _stack_heads(q_ref[...], hid)
            kc = jnp.concatenate([kp_ref[...], kc_ref[...]], axis=0)
            vc = jnp.concatenate([vp_ref[...], vc_ref[...]], axis=0)
            s = _dot_nt(qs, kc) * (HEAD_DIM ** -0.5)
            s = jnp.where(_band_mask(n + off), s, -jnp.inf)
            m = jnp.max(s, axis=-1, keepdims=True)
            e = jnp.exp(s - m)
            den = jnp.sum(e, axis=-1, keepdims=True)
            lse = m + jnp.log(den)
            pr = (e * (1.0 / den)).astype(BF16)
            o_ref[l] = _unstack_heads(_dot(pr, vc), hid).astype(BF16)
            lse_ref[l] = _unstack_heads(jnp.broadcast_to(lse, (HEADS * QB, GW)), hid)

    in_specs, args = [], []
    for res, off in streams:
        in_specs += [cur(res, off), prev(res, off), cur(res, off), prev(res, off), cur(res, off)]
        args += [q, k, k, v, v]
    out = jax.ShapeDtypeStruct((ns * grid[0], nbs * QB, GW), F32)
    both = pl.BlockSpec((ns, QB, GW), lambda rb, n: (rb, n, 0))
    o, lse = pl.pallas_call(
        body, name=name, grid=grid, in_specs=in_specs, out_specs=[both, both],
        out_shape=[jax.ShapeDtypeStruct(out.shape, BF16), out],
        compiler_params=_params(),
    )(*args)
    return o.reshape(q.shape), lse.reshape(q.shape)


def _head_rows(t_full, hid):
    return jnp.concatenate(
        [jnp.max(jnp.where(hid == h, t_full, -jnp.inf), axis=-1, keepdims=True) for h in range(HEADS)], axis=0)


def _attn_bwd(q, k, v, do, lse, cterm, name):
    r, L, _ = q.shape
    nbs, streams = _streams(r, L // QB, BWD_STREAMS)
    ns = len(streams)
    parts = r == 1

    def spec(res, index):
        return pl.BlockSpec((None, QB, GW), lambda rb, n: (res(rb), index(n), 0))

    def body(*refs):
        dq_ref, dk_ref, dv_ref, carry_k, carry_v, seam_k, seam_v = refs[8 * ns:]
        n = pl.program_id(1)

        @pl.when(n == 0)
        def _():
            carry_k[...] = jnp.zeros_like(carry_k)
            carry_v[...] = jnp.zeros_like(carry_v)

        @pl.when(n < nbs)
        def _():
            hid = _head_id(QB)
            for l, (_, off) in enumerate(streams):
                q_ref, do_ref, lse_ref, c_ref, kp_ref, kc_ref, vp_ref, vc_ref = refs[8 * l:8 * l + 8]
                qs = _stack_heads(q_ref[...], hid)
                dos = _stack_heads(do_ref[...], hid)
                kc = jnp.concatenate([kp_ref[...], kc_ref[...]], axis=0)
                vc = jnp.concatenate([vp_ref[...], vc_ref[...]], axis=0)
                s = _dot_nt(qs, kc) * (HEAD_DIM ** -0.5)
                s = jnp.where(_band_mask(n + off), s, -jnp.inf)
                p = jnp.exp(s - _head_rows(lse_ref[...], hid))
                dp = _dot_nt(dos, vc)
                ds = (p * (dp + _head_rows(c_ref[...], hid)) * (HEAD_DIM ** -0.5)).astype(BF16)
                dq_ref[l] = _unstack_heads(_dot(ds, kc), hid).astype(BF16)
                dkc = _dot_tn(ds, qs)
                dvc = _dot_tn(p.astype(BF16), dos)
                if parts and l > 0:
                    @pl.when(n == 0)
                    def _():
                        seam_k[l] = dkc[:QB]
                        seam_v[l] = dvc[:QB]
                dk_ref[l] = (carry_k[l] + dkc[:QB]).astype(BF16)
                dv_ref[l] = (carry_v[l] + dvc[:QB]).astype(BF16)
                carry_k[l] = dkc[QB:]
                carry_v[l] = dvc[QB:]

        @pl.when(n == nbs)
        def _():
            for l in range(ns):
                if parts and l + 1 < ns:
                    dk_ref[l] = (carry_k[l] + seam_k[l + 1]).astype(BF16)
                    dv_ref[l] = (carry_v[l] + seam_v[l + 1]).astype(BF16)
                else:
                    dk_ref[l] = carry_k[l].astype(BF16)
                    dv_ref[l] = carry_v[l].astype(BF16)

    in_specs, args = [], []
    for res, off in streams:
        qside = functools.partial(lambda n, off: jnp.minimum(n, nbs - 1) + off, off=off)
        kprev = functools.partial(lambda n, off: jnp.maximum(jnp.minimum(n, nbs) - 1 + off, 0), off=off)
        in_specs += [spec(res, qside)] * 4 + [spec(res, kprev), spec(res, qside)] * 2
        args += [q, do, lse, cterm, k, k, v, v]
    out = jax.ShapeDtypeStruct((ns * max(r // ns, 1), nbs * QB, GW), BF16)
    qout = pl.BlockSpec((ns, QB, GW), lambda rb, n: (rb, jnp.minimum(n, nbs - 1), 0))
    kout = pl.BlockSpec((ns, QB, GW), lambda rb, n: (rb, jnp.maximum(n - 1, 0), 0))
    buf = pltpu.VMEM((ns, QB, GW), F32)
    outs = pl.pallas_call(
        body, name=name, grid=(max(r // ns, 1), nbs + 1),
        in_specs=in_specs, out_specs=[qout, kout, kout], out_shape=[out, out, out],
        scratch_shapes=[buf, buf, buf, buf],
        compiler_params=_params(),
    )(*args)
    return [t.reshape(q.shape) for t in outs]


def _token_order(refs, scr):
    return [_from_residues(ref, slice(0, GW), r, scr) for ref, r in zip(refs, DILATIONS)]


def _group_weights(lses):
    l0, l1, l2 = lses
    m = jnp.maximum(jnp.maximum(l0, l1), l2)
    e = [jnp.exp(l - m) for l in (l0, l1, l2)]
    den = e[0] + e[1] + e[2]
    return [ei / den for ei in e]


def _mix_out_fwd(h, vec, gates, ypool, o3, lse3, wpb, wab, wout, name):
    S, D = h.shape

    def body(h_ref, vec_ref, gates_ref, yp_ref, o0, o1, o2, l0, l1, l2, wpb_hbm, wab_hbm, wout_hbm,
             hn_ref, ya_ref, merged_ref, tm_ref, wpb_v, wab_v, wout_v, sem, scr):
        _load_weights([(wpb_hbm, wpb_v), (wab_hbm, wab_v), (wout_hbm, wout_v)], sem)
        gt = vec_ref[3:4, :]
        wts = _group_weights(_token_order((l0, l1, l2), scr))
        og = _token_order((o0, o1, o2), scr)
        ya = (wts[0] * og[0] + wts[1] * og[1] + wts[2] * og[2]).astype(BF16)
        ya_ref[...] = ya
        merged = (gates_ref[:, :D].astype(F32) * _dot(yp_ref[...], wpb_v[...])
                  + gates_ref[:, D:].astype(F32) * _dot(ya, wab_v[...])).astype(BF16)
        merged_ref[...] = merged
        tm = _dot(merged, wout_v[...])
        tm_ref[...] = tm.astype(BF16)
        hn_ref[...] = h_ref[...] + gt * tm

    grp = _row_tile(TS, GW)
    res = [_res_spec(r) for r in DILATIONS]
    return pl.pallas_call(
        body, name=name, grid=(S // TS,),
        in_specs=[_row_tile(TS, D), _const((8, D)), _row_tile(TS, 2 * D), grp] + res * 2 + [ANY, ANY, ANY],
        out_specs=[_row_tile(TS, D), grp, _row_tile(TS, D), _row_tile(TS, D)],
        out_shape=[jax.ShapeDtypeStruct((S, D), F32), jax.ShapeDtypeStruct((S, GW), BF16),
                   jax.ShapeDtypeStruct((S, D), BF16), jax.ShapeDtypeStruct((S, D), BF16)],
        scratch_shapes=[pltpu.VMEM((GW, D), BF16), pltpu.VMEM((GW, D), BF16), pltpu.VMEM((D, D), BF16),
                        pltpu.SemaphoreType.DMA((3,)), pltpu.VMEM(RES_SCRATCH, F32)],
        compiler_params=_params(),
    )(h, vec, gates, ypool, *o3, *lse3, wpb, wab, wout)


def _mix_out_bwd(dh, tm, vec, gates, ypool, o3, lse3, wpb, wab, wout, name, rider=None):
    S, D = dh.shape

    def body(*refs):
        if rider is None:
            return compute(*refs)
        host, mine = rider.split(refs, 14, 12)
        rider.head(mine, pl.program_id(0))
        compute(*host)
        rider.tail(mine, pl.program_id(0), S // TS)

    def compute(dh_ref, tm_ref, vec_ref, gates_ref, yp_ref, o0, o1, o2, l0, l1, l2, wpb_hbm, wab_hbm, wout_hbm,
                dtm_ref, dgl_ref, dypb_ref, dyab_ref, dyp_ref, do0, do1, do2, c0, c1, c2, red_ref,
                wpb_v, wab_v, wout_v, sem, scr):
        _load_weights([(wpb_hbm, wpb_v), (wab_hbm, wab_v), (wout_hbm, wout_v)], sem)

        @pl.when(pl.program_id(0) == 0)
        def _():
            red_ref[...] = jnp.zeros_like(red_ref)

        gt = vec_ref[3:4, :]
        dh_v = dh_ref[...]
        red_ref[2:3, :] += jnp.sum(tm_ref[...].astype(F32) * dh_v, axis=0, keepdims=True)
        dtm = (gt * dh_v).astype(BF16)
        dtm_ref[...] = dtm
        dm = _dot_nt(dtm, wout_v[...])
        wts = _group_weights(_token_order((l0, l1, l2), scr))
        og = _token_order((o0, o1, o2), scr)
        ya = wts[0] * og[0] + wts[1] * og[1] + wts[2] * og[2]
        ypb = _dot(yp_ref[...], wpb_v[...])
        yab = _dot(ya.astype(BF16), wab_v[...])
        gp = gates_ref[:, :D].astype(F32)
        ga = gates_ref[:, D:].astype(F32)
        dgl_ref[:, :D] = (dm * ypb * gp * (1.0 - gp)).astype(BF16)
        dgl_ref[:, D:] = (dm * yab * ga * (1.0 - ga)).astype(BF16)
        dypb = (dm * gp).astype(BF16)
        dyab = (dm * ga).astype(BF16)
        dypb_ref[...] = dypb
        dyab_ref[...] = dyab
        dyp_ref[...] = _dot_nt(dypb, wpb_v[...])
        dya = _dot_nt(dyab, wab_v[...])
        row = lax.broadcasted_iota(jnp.int32, (GW, GW), 0) // HEAD_DIM
        col = lax.broadcasted_iota(jnp.int32, (GW, GW), 1) // HEAD_DIM
        ones = jnp.where(row == col, 1.0, 0.0).astype(F32)
        tot = jnp.dot(dya * ya, ones, preferred_element_type=F32, precision=lax.Precision.HIGHEST)
        for wg, do_ref, c_ref, r in zip(wts, (do0, do1, do2), (c0, c1, c2), DILATIONS):
            _to_residues(wg * dya, do_ref, slice(0, GW), r, scr)
            _to_residues(-(wg * tot), c_ref, slice(0, GW), r, scr)

    grp = _row_tile(TS, GW)
    res = [_res_spec(r) for r in DILATIONS]
    specs = (
        [_row_tile(TS, D), _row_tile(TS, D), _const((8, D)), _row_tile(TS, 2 * D), grp] + res * 2
        + [ANY, ANY, ANY],
        [_row_tile(TS, D), _row_tile(TS, 2 * D), _row_tile(TS, D), _row_tile(TS, D), grp]
        + res * 2 + [_const((8, D))],
        [jax.ShapeDtypeStruct((S, D), BF16), jax.ShapeDtypeStruct((S, 2 * D), BF16),
         jax.ShapeDtypeStruct((S, D), BF16), jax.ShapeDtypeStruct((S, D), BF16), jax.ShapeDtypeStruct((S, GW), F32)]
        + [_res_shape(S, r, BF16) for r in DILATIONS] + [_res_shape(S, r, F32) for r in DILATIONS]
        + [jax.ShapeDtypeStruct((8, D), F32)],
        [pltpu.VMEM((GW, D), BF16), pltpu.VMEM((GW, D), BF16), pltpu.VMEM((D, D), BF16),
         pltpu.SemaphoreType.DMA((3,)), pltpu.VMEM(RES_SCRATCH, F32)])
    in_specs, out_specs, out_shape, scratch = specs if rider is None else rider.specs(*specs)
    outs = pl.pallas_call(
        body, name=name, grid=(S // TS,), in_specs=in_specs, out_specs=out_specs, out_shape=out_shape,
        scratch_shapes=scratch, compiler_params=_params(),
    )(dh, tm, vec, gates, ypool, *o3, *lse3, wpb, wab, wout, *(rider.arrays if rider else []))
    return outs if rider is None else (outs[:12], outs[12:])


def _ada_mod(c_all, w, b, name):
    def body(c_ref, w_ref, b_ref, cond_ref, mod_ref):
        cv = c_ref[...]
        cond = cv * jax.nn.sigmoid(cv)
        cond_ref[...] = cond
        mod_ref[...] = jnp.dot(cond, w_ref[...], preferred_element_type=F32,
                               precision=lax.Precision.HIGHEST) + b_ref[...]

    return pl.pallas_call(
        body, name=name,
        out_shape=[jax.ShapeDtypeStruct(c_all.shape, F32), jax.ShapeDtypeStruct((c_all.shape[0], w.shape[1]), F32)],
        compiler_params=_params(),
    )(c_all, w, b)


def _adamw_math(w, g, m, v):
    m = ADAM_B1 * m + (1.0 - ADAM_B1) * g
    v = ADAM_B2 * v + (1.0 - ADAM_B2) * (g * g)
    m_hat = m / (1.0 - ADAM_B1 ** ADAM_STEP)
    v_hat = v / (1.0 - ADAM_B2 ** ADAM_STEP)
    delta = -ADAM_LR * (m_hat / (jnp.sqrt(v_hat) + ADAM_EPS) + ADAM_WD * w)
    return delta, m, v


def _adamw_many(ws, gs, ms, vs, name):
    n = len(ws)

    def body(*refs):
        for k in range(n):
            w_ref, g_ref, m_ref, v_ref = (refs[j * n + k] for j in range(4))
            d_ref, mo_ref, vo_ref = (refs[(4 + j) * n + k] for j in range(3))
            d_ref[...], mo_ref[...], vo_ref[...] = _adamw_math(w_ref[...], g_ref[...], m_ref[...], v_ref[...])

    outs = pl.pallas_call(
        body, name=name, out_shape=[jax.ShapeDtypeStruct(t.shape, F32) for t in ws] * 3,
        compiler_params=_params(),
    )(*ws, *gs, *ms, *vs)
    return outs[:n], outs[n:2 * n], outs[2 * n:]


def _ada_grad_adamw(cond_t, dmod, w, m, v, name, tr=256):
    R, C = w.shape
    nb = dmod.shape[0]

    def body(ct_ref, dm_ref, w_ref, m_ref, v_ref, g_ref, d_ref, mo_ref, vo_ref):
        ct = ct_ref[...]
        dm = dm_ref[...]
        g = jnp.zeros((tr, C), F32)
        for bi in range(nb):
            g = g + ct[:, bi:bi + 1] * dm[bi:bi + 1, :]
        g_ref[...] = g
        d_ref[...], mo_ref[...], vo_ref[...] = _adamw_math(w_ref[...], g, m_ref[...], v_ref[...])

    spec = _row_tile(tr, C)
    out = jax.ShapeDtypeStruct((R, C), F32)
    return pl.pallas_call(
        body, name=name, grid=(R // tr,),
        in_specs=[_row_tile(tr, nb), _const((nb, C)), spec, spec, spec],
        out_specs=[spec] * 4, out_shape=[out] * 4,
        compiler_params=_params(),
    )(cond_t, dmod, w, m, v)


def _row_step(rows, cap=256):
    for cand in range(cap, 15, -16):
        if rows % cand == 0:
            return cand
    return rows


def _slot_sum(x_ref):
    acc = x_ref[0].astype(F32)
    for k in range(1, x_ref.shape[0]):
        acc = acc + x_ref[k].astype(F32)
    return acc


def _sum_slots(x, name, out_dtype=F32):
    n, R, C = x.shape
    tr = _row_step(R)

    def body(x_ref, o_ref):
        o_ref[...] = _slot_sum(x_ref).astype(out_dtype)

    return pl.pallas_call(
        body, name=name, grid=(R // tr,),
        in_specs=[pl.BlockSpec((n, tr, C), lambda i: (0, i, 0))],
        out_specs=_row_tile(tr, C), out_shape=jax.ShapeDtypeStruct((R, C), out_dtype),
        compiler_params=_params(),
    )(x)


def _sum_pair(core, g, recv, name):
    _, _, R, C = g.shape
    tr = _row_step(R, cap=1024)

    def body(core_ref, g_ref, r_ref, o_ref):
        o_ref[...] = (g_ref[...].astype(F32) + r_ref[...].astype(F32)).astype(BF16)

    return pl.pallas_call(
        body, name=name, out_shape=jax.ShapeDtypeStruct((4, R, C), BF16),
        grid_spec=pltpu.PrefetchScalarGridSpec(
            num_scalar_prefetch=1, grid=(4, R // tr),
            in_specs=[pl.BlockSpec((None, None, tr, C), lambda k, i, core_ref: (k, core_ref[0], i, 0)),
                      pl.BlockSpec((None, tr, C), lambda k, i, core_ref: (k, i, 0))],
            out_specs=pl.BlockSpec((None, tr, C), lambda k, i, core_ref: (k, i, 0))),
        compiler_params=_params(),
    )(core, g, recv)


def _sum_adamw(chip, own, recv, w, m, v, name):
    _, R, C = own.shape
    tr = _row_step(R, cap=512)

    def body(chip_ref, own_ref, r_ref, w_ref, m_ref, v_ref, g_ref, d_ref, mo_ref, vo_ref):
        g = own_ref[...].astype(F32) + _slot_sum(r_ref)
        g_ref[...] = g
        d_ref[...], mo_ref[...], vo_ref[...] = _adamw_math(w_ref[...], g, m_ref[...], v_ref[...])

    spec = pl.BlockSpec((tr, C), lambda i, chip_ref: (i, 0))
    out = jax.ShapeDtypeStruct((R, C), F32)
    return pl.pallas_call(
        body, name=name, out_shape=[out] * 4,
        grid_spec=pltpu.PrefetchScalarGridSpec(
            num_scalar_prefetch=1, grid=(R // tr,),
            in_specs=[pl.BlockSpec((None, tr, C), lambda i, chip_ref: (chip_ref[0], i, 0)),
                      pl.BlockSpec((3, tr, C), lambda i, chip_ref: (0, i, 0)), spec, spec, spec],
            out_specs=[spec] * 4),
        compiler_params=_params(),
    )(chip, own, recv, w, m, v)


def _place():
    return lax.axis_index("x"), lax.axis_index("y"), lax.axis_index("c")


def _gather_steps(x_refs, out_refs, send_sems, recv_sems):
    n = len(x_refs)
    x, y, c = _place()
    me, sibling = (x, y, c), (x, y, 1 - c)
    chips = [(1 - x, y), (x, 1 - y), (1 - x, 1 - y)]

    def rows(a, px, py, pc):
        return out_refs[a].at[4 * px + 2 * py + pc]

    def copy(a, k, block, to, src=None):
        return pltpu.make_async_remote_copy(
            src_ref=rows(a, *block) if src is None else src, dst_ref=rows(a, *block),
            send_sem=send_sems.at[a, k], recv_sem=recv_sems.at[a, k], device_id=to, device_id_type=MESH)

    def first(a):
        return [copy(a, 0, me, sibling, src=x_refs[a])] + [
            copy(a, 1 + j, me, (*chip, c), src=x_refs[a]) for j, chip in enumerate(chips)]

    def passed(a, j):
        return copy(a, 4 + j, (*chips[j], c), sibling)

    def start():
        for a in range(n):
            for cp in first(a):
                cp.start()

    def relay():
        for j, chip in enumerate(chips):
            for a in range(n):
                copy(a, 1 + j, (*chip, c), me).wait_recv()
                passed(a, j).start()

    def finish():
        for a in range(n):
            copy(a, 0, sibling, me).wait_recv()
            for j, chip in enumerate(chips):
                copy(a, 4 + j, (*chip, 1 - c), me).wait_recv()
        for a in range(n):
            for cp in first(a) + [passed(a, j) for j in range(3)]:
                cp.wait_send()

    return start, relay, finish


def _gather_tree_steps(x_refs, out_refs, send_sems, recv_sems):
    n = len(x_refs)
    x, y, c = _place()
    me, sibling = (x, y, c), (x, y, 1 - c)
    xn, yn, dg = (1 - x, y), (x, 1 - y), (1 - x, 1 - y)

    def rows(a, px, py, pc):
        return out_refs[a].at[4 * px + 2 * py + pc]

    def copy(a, k, block, to, src=None):
        return pltpu.make_async_remote_copy(
            src_ref=rows(a, *block) if src is None else src, dst_ref=rows(a, *block),
            send_sem=send_sems.at[a, k], recv_sem=recv_sems.at[a, k], device_id=to, device_id_type=MESH)

    def own(a):
        return [copy(a, 0, me, sibling, src=x_refs[a]), copy(a, 1, me, (*xn, c), src=x_refs[a]),
                copy(a, 2, me, (*yn, c), src=x_refs[a])]

    def north_hands_on(a):
        return copy(a, 3, (*xn, c), (*yn, c))

    def south_hands_on(a):
        return copy(a, 3, (*yn, c), (*xn, c))

    def to_sibling(a):
        return [copy(a, 4, (*xn, c), sibling), copy(a, 5, (*yn, c), sibling), copy(a, 6, (*dg, c), sibling)]

    def start():
        for a in range(n):
            for cp in own(a):
                cp.start()

    def relay_neighbours():
        for a in range(n):
            copy(a, 1, (*xn, c), me).wait_recv()
            to_sibling(a)[0].start()

        @pl.when(c == 1)
        def _():
            for a in range(n):
                north_hands_on(a).start()

        for a in range(n):
            copy(a, 2, (*yn, c), me).wait_recv()
            to_sibling(a)[1].start()

        @pl.when(c == 0)
        def _():
            for a in range(n):
                south_hands_on(a).start()

    def relay_diagonal():
        for a in range(n):
            copy(a, 3, (*dg, c), me).wait_recv()
            to_sibling(a)[2].start()

    def finish():
        for a in range(n):
            copy(a, 0, sibling, me).wait_recv()
            copy(a, 4, (*xn, 1 - c), me).wait_recv()
            copy(a, 5, (*yn, 1 - c), me).wait_recv()
            copy(a, 6, (*dg, 1 - c), me).wait_recv()
        for a in range(n):
            for cp in own(a) + to_sibling(a):
                cp.wait_send()

        @pl.when(c == 1)
        def _():
            for a in range(n):
                north_hands_on(a).wait_send()

        @pl.when(c == 0)
        def _():
            for a in range(n):
                south_hands_on(a).wait_send()

    return start, relay_neighbours, relay_diagonal, finish


def _all_gather_tree(arrs, name, gather=()):
    n, extra = len(arrs), len(gather)

    def body(*refs):
        sems = refs[2 * (n + extra):]
        if extra:
            g_start, g_relay, g_finish = _small_gather_steps(
                refs[n:n + extra], refs[2 * n + extra:2 * (n + extra)], *sems[2:])
            g_start()
        for step in _gather_tree_steps(refs[:n], refs[n + extra:2 * n + extra], *sems[:2]):
            step()
        if extra:
            g_relay()
            g_finish()

    return pl.pallas_call(
        body, name=name,
        out_shape=[jax.ShapeDtypeStruct((N_DEV,) + t.shape, t.dtype) for t in list(arrs) + list(gather)],
        in_specs=[ANY] * (n + extra), out_specs=[ANY] * (n + extra),
        scratch_shapes=[pltpu.SemaphoreType.DMA((n, 7)), pltpu.SemaphoreType.DMA((n, 7))]
        + (_small_gather_scratch(extra) if extra else []),
    )(*arrs, *gather)


def _all_gather(arrs, name, own=True):
    n = len(arrs)

    def body(*refs):
        x_refs, out_refs = refs[:n], refs[n:2 * n]
        send_sems, recv_sems, local_sems = refs[2 * n:]
        me = 4 * lax.axis_index("x") + 2 * lax.axis_index("y") + lax.axis_index("c")
        mine = [pltpu.make_async_copy(x_refs[a], out_refs[a].at[me], local_sems.at[a]) for a in range(n)] if own else []
        for cp in mine:
            cp.start()
        for step in _gather_steps(x_refs, out_refs, send_sems, recv_sems):
            step()
        for cp in mine:
            cp.wait()

    return pl.pallas_call(
        body, name=name, out_shape=[jax.ShapeDtypeStruct((N_DEV,) + t.shape, t.dtype) for t in arrs],
        in_specs=[ANY] * n, out_specs=[ANY] * n,
        scratch_shapes=[pltpu.SemaphoreType.DMA((n, 7)), pltpu.SemaphoreType.DMA((n, 7)),
                        pltpu.SemaphoreType.DMA((n,))],
    )(*arrs)


def _pair_exchange_steps(g_refs, out_refs, send_sems, recv_sems):
    x, y, c = _place()

    def give():
        return [pltpu.make_async_remote_copy(
            src_ref=g_refs[a].at[pl.ds(0, 4), 1 - c], dst_ref=out_refs[a], send_sem=send_sems.at[a],
            recv_sem=recv_sems.at[a], device_id=(x, y, 1 - c), device_id_type=MESH) for a in range(len(g_refs))]

    def start():
        for cp in give():
            cp.start()

    def finish():
        for cp in give():
            cp.wait()

    return start, finish


def _pair_exchange(arrs, name):
    n = len(arrs)

    def body(*refs):
        for step in _pair_exchange_steps(refs[:n], refs[n:2 * n], *refs[2 * n:]):
            step()

    return pl.pallas_call(
        body, name=name,
        out_shape=[jax.ShapeDtypeStruct((4,) + t.shape[2:], t.dtype) for t in arrs],
        in_specs=[ANY] * n, out_specs=[ANY] * n,
        scratch_shapes=[pltpu.SemaphoreType.DMA((n,)), pltpu.SemaphoreType.DMA((n,))],
    )(*arrs)


def _chip_exchange_steps(p_refs, out_refs, send_sems, recv_sems):
    x, y, c = _place()
    chips = [(1 - x, y), (x, 1 - y), (1 - x, 1 - y)]

    def copies():
        return [pltpu.make_async_remote_copy(
            src_ref=p_refs[a].at[2 * px + py], dst_ref=out_refs[a].at[j], send_sem=send_sems.at[a, j],
            recv_sem=recv_sems.at[a, j], device_id=(px, py, c), device_id_type=MESH)
            for a in range(len(p_refs)) for j, (px, py) in enumerate(chips)]

    def start():
        for cp in copies():
            cp.start()

    def finish():
        for cp in copies():
            cp.wait()

    return start, finish


def _small_gather_steps(x_refs, out_refs, send_sems, recv_sems, local_sems):
    me = 4 * lax.axis_index("x") + 2 * lax.axis_index("y") + lax.axis_index("c")
    start, relay, finish = _gather_steps(x_refs, out_refs, send_sems, recv_sems)

    def mine():
        return [pltpu.make_async_copy(x_refs[a], out_refs[a].at[me], local_sems.at[a]) for a in range(len(x_refs))]

    def start_all():
        for cp in mine():
            cp.start()
        start()

    def finish_all():
        finish()
        for cp in mine():
            cp.wait()

    return start_all, relay, finish_all


def _small_gather_scratch(k):
    return [pltpu.SemaphoreType.DMA((k, 7)), pltpu.SemaphoreType.DMA((k, 7)), pltpu.SemaphoreType.DMA((k,))]


def _chip_exchange(arrs, name, gather=()):
    n, k = len(arrs), len(gather)

    def body(*refs):
        ins, outs, sems = refs[:n + k], refs[n + k:2 * (n + k)], refs[2 * (n + k):]
        start, finish = _chip_exchange_steps(ins[:n], outs[:n], *sems[:2])
        if k:
            g_start, g_relay, g_finish = _small_gather_steps(ins[n:], outs[n:], *sems[2:])
            g_start()
        start()
        if k:
            g_relay()
        finish()
        if k:
            g_finish()

    return pl.pallas_call(
        body, name=name,
        out_shape=[jax.ShapeDtypeStruct((3,) + t.shape[1:], t.dtype) for t in arrs]
        + [jax.ShapeDtypeStruct((N_DEV,) + t.shape, t.dtype) for t in gather],
        in_specs=[ANY] * (n + k), out_specs=[ANY] * (n + k),
        scratch_shapes=[pltpu.SemaphoreType.DMA((n, 3)), pltpu.SemaphoreType.DMA((n, 3))]
        + (_small_gather_scratch(k) if k else []),
    )(*arrs, *gather)


class _Rider:
    def __init__(self, arrays, out_shape, sems, steps, relay_at=()):
        self.arrays, self.out_shape, self.scratch, self.steps = list(arrays), out_shape, sems, steps
        self.n = len(self.arrays)
        self.relay_at = relay_at

    def specs(self, in_specs, out_specs, out_shape, scratch):
        extra = [ANY] * self.n
        return in_specs + extra, out_specs + extra, out_shape + self.out_shape, scratch + self.scratch

    def split(self, refs, n_in, n_out):
        k = self.n
        a, b = n_in + k, n_in + k + n_out
        return refs[:n_in] + refs[a:b] + refs[b + k:-2], (refs[n_in:a], refs[b:b + k], refs[-2:])

    def head(self, mine, step):
        pl.when(step == 0)(self.steps(mine[0], mine[1], *mine[2])[0])

    def tail(self, mine, step, nsteps):
        steps = self.steps(mine[0], mine[1], *mine[2])
        for relay, frac in zip(steps[1:-1], self.relay_at):
            pl.when(step == min(int(frac * nsteps), nsteps - 1))(relay)
        pl.when(step == nsteps - 1)(steps[-1])


def _gather_rider(arrs, relay_at=(0.5, 0.78)):
    n = len(arrs)
    return _Rider(arrs, [jax.ShapeDtypeStruct((N_DEV,) + t.shape, t.dtype) for t in arrs],
                  [pltpu.SemaphoreType.DMA((n, 7)), pltpu.SemaphoreType.DMA((n, 7))], _gather_tree_steps,
                  relay_at)


def _pair_exchange_rider(arrs):
    n = len(arrs)
    return _Rider(arrs, [jax.ShapeDtypeStruct((4,) + t.shape[2:], t.dtype) for t in arrs],
                  [pltpu.SemaphoreType.DMA((n,)), pltpu.SemaphoreType.DMA((n,))], _pair_exchange_steps)


def _chip_exchange_rider(arrs):
    n = len(arrs)
    return _Rider(arrs, [jax.ShapeDtypeStruct((3,) + t.shape[1:], t.dtype) for t in arrs],
                  [pltpu.SemaphoreType.DMA((n, 3)), pltpu.SemaphoreType.DMA((n, 3))], _chip_exchange_steps)


def _rope_tables(positions):
    inv_freq = ROPE_THETA ** (-jnp.arange(0, HEAD_DIM, 2, dtype=F32) / HEAD_DIM)
    ang = positions.astype(F32)[:, None] * inv_freq
    cos, sin = jnp.cos(ang), jnp.sin(ang)
    return jnp.tile(cos, (1, 4)), jnp.tile(jnp.concatenate([-sin, sin], axis=1), (1, 2))


class _GradReducer:
    def __init__(self):
        self.core = lax.axis_index("c").astype(jnp.int32).reshape(1)
        self.own, self.others, self.waiting, self.riding = {}, {}, [], []

    def pair(self, named):
        mine = self._split(named)
        self._summed(mine, _pair_exchange(list(mine.values()), "reduce_pair_" + next(iter(named))))

    def pair_rider(self, named):
        self.pairing = self._split(named)
        return _pair_exchange_rider(list(self.pairing.values()))

    def pair_landed(self, results):
        self._summed(self.pairing, results)

    @staticmethod
    def _split(named):
        return {k: g.reshape((4, 2) + g.shape[1:]) for k, g in named.items()}

    def _summed(self, mine, theirs):
        for (k, g), r in zip(mine.items(), theirs):
            self.own[k] = _sum_pair(self.core, g, r, "sum_pair_" + k)
        self.waiting += list(mine)

    def rider(self):
        self.riding, self.waiting = self.waiting, []
        return _chip_exchange_rider([self.own[k] for k in self.riding])

    def landed(self, results):
        self.others.update(zip(self.riding, results))

    def flush(self, name, gather=()):
        keys, self.waiting = self.waiting, []
        res = _chip_exchange([self.own[k] for k in keys], name, gather=gather)
        self.others.update(zip(keys, res))
        return res[len(keys):]


def _by_owner(g):
    if g.ndim == 3:
        return g if g.shape[0] == N_DEV else g.reshape(N_DEV, g.shape[1] * g.shape[0] // N_DEV, g.shape[2])
    return g.reshape(N_DEV, g.shape[0] // N_DEV, g.shape[1])


def _local_step(x, target, positions, mod, small, W, late=None, red=None):
    S, D = x.shape
    gains =jnp.stack([small["g1"], small["g2"], small["g3"]])[:, None, :]
    v1, v2, v3 = jnp.pad(jnp.concatenate([gains, mod.reshape(3, 3, D)], axis=1), ((0, 0), (0, 4), (0, 0)))
    vf = jnp.pad(small["gf"][None], ((0, 7), (0, 0)))
    cos, sin = _rope_tables(positions)
    wbd = jax.scipy.linalg.block_diag(*[small["w_pool"][k] for k in range(4)]).astype(BF16)
    pscale = small["pool_scale"].reshape(1, GW)

    if late is None:
        h1, u1, ab1, act1, f1 = _ffn_fwd(x, v1, W["w1in"], W["w1out"], "ffn1_fwd")
    else:
        (h1, u1, ab1, act1, f1), landed = _ffn_fwd(x, v1, W["w1in"], W["w1out"], "ffn1_fwd", rider=late[0])
        W = {**W, **late[1](landed)}
    u2, p, gates, *qkv = _mix_in_fwd(h1, v2, cos, sin, W["win"], "mix_in_fwd")
    dpool, ypool = _pool_fwd(p, wbd, pscale, "pool_fwd")
    o3, lse3 = [], []
    for gi in range(len(DILATIONS)):
        o, lse = _attn_fwd(qkv[gi], qkv[3 + gi], qkv[6 + gi], f"attn_fwd_{gi}")
        o3.append(o)
        lse3.append(lse)
    h2, ya, merged, tm = _mix_out_fwd(h1, v2, gates, ypool, o3, lse3, W["wpb"], W["wab"], W["wout"], "mix_out_fwd")
    dh3, u3, ab3, act3, f3, loss_blk, dgf = _ffn_fwd(h2, v3, W["w2in"], W["w2out"], "ffn2_fwd", final=(target, vf))

    dh2, dab3, df3, red3 = _ffn_bwd(dh3, h2, f3, ab3, v3, W["w2in"], W["w2out"], "ffn2_bwd")
    half_f = ab3.shape[2] // 2
    G = {"w2in": _by_owner(_wgrad(dab3, u3, "wgrad_2in", tm=half_f))}
    mix_out_args = (dh2, tm, v2, gates, ypool, o3, lse3, W["wpb"], W["wab"], W["wout"], "mix_out_bwd")
    if red is None:
        G["w2out"] = _by_owner(_wgrad(act3, df3, "wgrad_2out", tm=half_f))
        mix_out = _mix_out_bwd(*mix_out_args)
    else:
        g2out, landed = _wgrad(act3, df3, "wgrad_2out", tm=half_f, rider=red.pair_rider({"w2in": G["w2in"]}))
        red.pair_landed(landed)
        G["w2out"] = _by_owner(g2out)
        red.pair({"w2out": G["w2out"]})
        mix_out, landed = _mix_out_bwd(*mix_out_args, rider=red.rider())
        red.landed(landed)
    (dtm, dgl, dypb, dyab, dyp, do0, do1, do2, c0, c1, c2, red2o) = mix_out
    dq3, dk3, dv3 = [], [], []
    for gi, (do, ct) in enumerate(zip((do0, do1, do2), (c0, c1, c2))):
        dq, dk, dv = _attn_bwd(qkv[gi], qkv[3 + gi], qkv[6 + gi], do, lse3[gi], ct, f"attn_bwd_{gi}")
        dq3.append(dq)
        dk3.append(dk)
        dv3.append(dv)
    dp, dwbd, dps = _pool_bwd(dyp, dpool, wbd, pscale, "pool_bwd")
    dh1, dproj, red2i = _mix_in_bwd(dh2, h1, v2, cos, sin, dp, dq3 + dk3 + dv3, dgl, W["win"], "mix_in_bwd")
    G["win"] = _by_owner(_wgrad(dproj, u2, "wgrad_in", tm=1152))
    G["wpb"] = _full_to_cols(_wgrad(ypool, dypb, "wgrad_pb"))
    G["wab"] = _full_to_cols(_wgrad(ya, dyab, "wgrad_ab"))
    if red is None:
        G["wout"] = _by_owner(_wgrad(merged, dtm, "wgrad_out"))
    else:
        gout, landed = _wgrad(merged, dtm, "wgrad_out", rider=red.pair_rider({k: G[k] for k in ("win", "wpb", "wab")}))
        red.pair_landed(landed)
        G["wout"] = _by_owner(gout)
        red.pair({"wout": G["wout"]})
    dx, dab1, df1, red1 = _ffn_bwd(dh1, x, f1, ab1, v1, W["w1in"], W["w1out"], "ffn1_bwd")
    dmod = jnp.concatenate([red1[:3], (red2i + red2o)[:3], red3[:3]])
    dsmall = {
        "g1": red1[3], "g2": red2i[3], "g3": red3[3], "gf": dgf[0],
        "w_pool": jnp.stack([dwbd[k * 64:(k + 1) * 64, k * 64:(k + 1) * 64] for k in range(4)]),
        "pool_scale": dps[0],
    }
    if red is None:
        G["w1in"] = _by_owner(_wgrad(dab1, u1, "wgrad_1in", tm=half_f))
        G["w1out"] = _by_owner(_wgrad(act1, df1, "wgrad_1out", tm=half_f))
        return loss_blk[0, 0], dx, G, dmod, dsmall
    g1in, landed = _wgrad(dab1, u1, "wgrad_1in", tm=half_f, rider=red.rider())
    red.landed(landed)
    red.pair({"w1in": _by_owner(g1in)})
    g1out, landed = _wgrad(act1, df1, "wgrad_1out", tm=half_f, rider=red.rider())
    red.landed(landed)
    red.pair({"w1out": _by_owner(g1out)})
    part = _pack_small(dict(b_ada=dmod, g_norm_ffn1=dsmall["g1"], g_norm_mix=dsmall["g2"], g_norm_ffn2=dsmall["g3"],
                            g_final=dsmall["gf"], w_pool=dsmall["w_pool"], pool_scale=dsmall["pool_scale"]),
                       loss_blk[0, 0])
    (parts,) = red.flush("reduce_chips_w1out", gather=[part])
    return dx, parts


SHARDED = ("w_ffn1_in", "w_ffn1_out", "w_in", "w_pool_branch", "w_attn_branch", "w_out", "w_ffn2_in", "w_ffn2_out")
TRANSPOSED = ("w_ffn1_in", "w_in", "w_ffn2_in")
FIRST = ("w_ffn1_in", "w_ffn1_out")
LATER = tuple(n for n in SHARDED if n not in FIRST)
GRAD_KEY = dict(w_ffn1_in="w1in", w_ffn1_out="w1out", w_in="win", w_pool_branch="wpb", w_attn_branch="wab",
                w_out="wout", w_ffn2_in="w2in", w_ffn2_out="w2out")


def _cols_to_full(g):
    return g.transpose(1, 0, 2).reshape(g.shape[1], N_DEV * g.shape[2])


def _full_to_cols(t):
    return t.reshape(t.shape[0], N_DEV, t.shape[1] // N_DEV).transpose(1, 0, 2)


SMALL = (("b_ada", 72), ("g_norm_ffn1", 8), ("g_norm_mix", 8), ("g_norm_ffn2", 8), ("g_final", 8),
         ("w_pool", 128), ("pool_scale", 8))


def _pack_small(vals, loss):
    rows = []
    for name, nrows in SMALL:
        t = vals[name].reshape(-1, 128)
        rows.append(jnp.pad(t, ((0, nrows - t.shape[0]), (0, 0))))
    rows.append(jnp.full((8, 128), loss, F32))
    return jnp.concatenate(rows)


def _unpack_small(slab, shapes):
    out, off = {}, 0
    for name, nrows in SMALL:
        used = 1
        for d in shapes[name]:
            used *= d
        out[name] = slab[off:off + used // 128].reshape(shapes[name])
        off += nrows
    return out, slab[off, 0]


def _as_2d(t):
    return t.reshape(-1, t.shape[-1])


def kernel(x, c, positions, w_ada, b_ada, g_norm_ffn1, w_ffn1_in, w_ffn1_out, g_norm_mix, w_in, w_pool, pool_scale, w_pool_branch, w_attn_branch, w_out, g_norm_ffn2, w_ffn2_in, w_ffn2_out, g_final, loss_target, m_w_ada, m_b_ada, m_g_norm_ffn1, m_w_ffn1_in, m_w_ffn1_out, m_g_norm_mix, m_w_in, m_w_pool, m_pool_scale, m_w_pool_branch, m_w_attn_branch, m_w_out, m_g_norm_ffn2, m_w_ffn2_in, m_w_ffn2_out, m_g_final, v_w_ada, v_b_ada, v_g_norm_ffn1, v_w_ffn1_in, v_w_ffn1_out, v_g_norm_mix, v_w_in, v_w_pool, v_pool_scale, v_w_pool_branch, v_w_attn_branch, v_w_out, v_g_norm_ffn2, v_w_ffn2_in, v_w_ffn2_out, v_g_final):
    names = ["w_ada", "b_ada", "g_norm_ffn1", "w_ffn1_in", "w_ffn1_out", "g_norm_mix", "w_in", "w_pool", "pool_scale",
             "w_pool_branch", "w_attn_branch", "w_out", "g_norm_ffn2", "w_ffn2_in", "w_ffn2_out", "g_final"]
    w = dict(w_ada=w_ada, b_ada=b_ada, g_norm_ffn1=g_norm_ffn1, w_ffn1_in=w_ffn1_in, w_ffn1_out=w_ffn1_out,
             g_norm_mix=g_norm_mix, w_in=w_in, w_pool=w_pool, pool_scale=pool_scale, w_pool_branch=w_pool_branch,
             w_attn_branch=w_attn_branch, w_out=w_out, g_norm_ffn2=g_norm_ffn2, w_ffn2_in=w_ffn2_in,
             w_ffn2_out=w_ffn2_out, g_final=g_final)
    m = dict(w_ada=m_w_ada, b_ada=m_b_ada, g_norm_ffn1=m_g_norm_ffn1, w_ffn1_in=m_w_ffn1_in, w_ffn1_out=m_w_ffn1_out,
             g_norm_mix=m_g_norm_mix, w_in=m_w_in, w_pool=m_w_pool, pool_scale=m_pool_scale,
             w_pool_branch=m_w_pool_branch, w_attn_branch=m_w_attn_branch, w_out=m_w_out, g_norm_ffn2=m_g_norm_ffn2,
             w_ffn2_in=m_w_ffn2_in, w_ffn2_out=m_w_ffn2_out, g_final=m_g_final)
    v = dict(w_ada=v_w_ada, b_ada=v_b_ada, g_norm_ffn1=v_g_norm_ffn1, w_ffn1_in=v_w_ffn1_in, w_ffn1_out=v_w_ffn1_out,
             g_norm_mix=v_g_norm_mix, w_in=v_w_in, w_pool=v_w_pool, pool_scale=v_pool_scale,
             w_pool_branch=v_w_pool_branch, w_attn_branch=v_w_attn_branch, w_out=v_w_out, g_norm_ffn2=v_g_norm_ffn2,
             w_ffn2_in=v_w_ffn2_in, w_ffn2_out=v_w_ffn2_out, g_final=v_g_final)
    shapes = {n: w[n].shape for n in names}
    me = 4 * lax.axis_index("x") + 2 * lax.axis_index("y") + lax.axis_index("c")
    D = x.shape[-1]
    n_mod = w_ada.shape[-1] * N_DEV // D

    def local(t, name):
        return t[name][0].T if name in TRANSPOSED else t[name][0]

    shards = {name: local(w, name).astype(BF16) for name in SHARDED}

    def gather_done(names, fulls):
        return {name: lax.dynamic_update_index_in_dim(full, shards[name], me, axis=0)
                for name, full in zip(names, fulls)}

    def ffn_weights(g, pre):
        return {"w%sin" % pre: g["w_ffn%s_in" % pre].reshape(2, -1, D),
                "w%sout" % pre: g["w_ffn%s_out" % pre].reshape(-1, D)}

    def later_weights(fulls):
        g = gather_done(LATER, fulls)
        return dict(win=g["w_in"].reshape(-1, D), wpb=_cols_to_full(g["w_pool_branch"]),
                    wab=_cols_to_full(g["w_attn_branch"]), wout=g["w_out"].reshape(D, D), **ffn_weights(g, "2"))

    *first, c_all = _all_gather_tree([shards[n] for n in FIRST], "gather_ffn1", gather=[c.reshape(D // 128, 128)])
    W = ffn_weights(gather_done(FIRST, first), "1")

    ada_cols = w_ada.shape[-1]
    b_mine = lax.dynamic_slice_in_dim(b_ada, me * ada_cols, ada_cols, axis=1)
    cond, mod_part = _ada_mod(c_all.reshape(N_DEV, D), w_ada[0], b_mine, "ada_mod")
    (mod_all,) = _all_gather([mod_part.reshape(-1, 128)], "gather_mod")
    mod_all = mod_all.reshape(N_DEV, N_DEV, ada_cols)
    mod = lax.dynamic_index_in_dim(mod_all, me, axis=1, keepdims=False).reshape(n_mod, D)

    small = dict(g1=g_norm_ffn1[0], g2=g_norm_mix[0], g3=g_norm_ffn2[0], gf=g_final, w_pool=w_pool[0],
                 pool_scale=pool_scale[0])
    red = _GradReducer()
    dx, parts = _local_step(
        x[0], loss_target[0], positions[0], mod, small, W,
        late=(_gather_rider([shards[n] for n in LATER]), later_weights), red=red)
    chip = (2 * lax.axis_index("x") + lax.axis_index("y")).astype(jnp.int32).reshape(1)

    gsmall, loss = _unpack_small(_sum_slots(parts, "sum_small"), shapes)
    rows_mine = ada_cols // 128
    dmod_mine = lax.dynamic_slice_in_dim(parts, me * rows_mine, rows_mine, axis=1).reshape(N_DEV, ada_cols)

    grads, delta, new_m, new_v = {}, {}, {}, {}
    grads["w_ada"], delta["w_ada"], new_m["w_ada"], new_v["w_ada"] = (
        t[None] for t in _ada_grad_adamw(cond.T, dmod_mine, w_ada[0], m_w_ada[0], v_w_ada[0], "ada_grad_adamw"))
    for name in SHARDED:
        key = GRAD_KEY[name]
        res = _sum_adamw(chip, red.own[key], red.others[key], local(w, name), local(m, name), local(v, name),
                         "adamw_" + name)
        grads[name], delta[name], new_m[name], new_v[name] = (
            (t.T if name in TRANSPOSED else t)[None] for t in res)
    small_names = [name for name, _ in SMALL]
    res = _adamw_many(*([_as_2d(t[name]) for name in small_names] for t in (w, gsmall, m, v)), "adamw_small")
    for dst, vals in zip((delta, new_m, new_v), res):
        dst.update({name: t.reshape(shapes[name]) for name, t in zip(small_names, vals)})
    grads.update(gsmall)

    return (loss, dx[None], *[grads[n] for n in names], *[delta[n] for n in names],
            *[new_m[n] for n in names], *[new_v[n] for n in names])
```

```python
import functools

import jax
import jax.numpy as jnp
from jax import lax
from jax.experimental import pallas as pl
from jax.experimental.pallas import tpu as pltpu

F32 = jnp.float32
BF16 = jnp.bfloat16
MESH = pl.DeviceIdType.MESH
ANY = pl.BlockSpec(memory_space=pl.ANY)

N_DEV = 8
EPS = 1e-6
HEAD_DIM = 64
HEADS = 4
GW = HEADS * HEAD_DIM
DILATIONS = (1, 4, 16)
BAND = 128
QB = 128
POOL_WINDOWS = (2, 4, 8, 16)
HALO = 16
ROPE_THETA = 10000.0

ADAM_LR = 0.001
ADAM_B1 = 0.9
ADAM_B2 = 0.999
ADAM_EPS = 1e-08
ADAM_WD = 0.01
ADAM_STEP = 10

VMEM_LIMIT = 56 * 1024 * 1024
TS = 512
FFN_TS = 256
FFN_CHUNKS = (2816,)

NT = (((1,), (1,)), ((), ()))
TN = (((0,), (0,)), ((), ()))


def _params(**kw):
    return pltpu.CompilerParams(vmem_limit_bytes=VMEM_LIMIT, **kw)


def _dot(a, b):
    return jnp.dot(a, b, preferred_element_type=F32)


def _dot_nt(a, b):
    return lax.dot_general(a, b, NT, preferred_element_type=F32)


def _dot_tn(a, b):
    return lax.dot_general(a, b, TN, preferred_element_type=F32)


def _load_weights(pairs, sem):
    @pl.when(pl.program_id(0) == 0)
    def _():
        copies = [pltpu.make_async_copy(src, dst, sem.at[i]) for i, (src, dst) in enumerate(pairs)]
        for cp in copies:
            cp.start()
        for cp in copies:
            cp.wait()


def _norm_mod(x, g, sc, sh):
    r = lax.rsqrt(jnp.mean(x * x, axis=-1, keepdims=True) + EPS)
    xn = x * r
    y = xn * g
    return r, xn, y, y * (1.0 + sc) + sh


def _norm_mod_bwd(du, r, xn, y, g, sc):
    dsh = jnp.sum(du, axis=0, keepdims=True)
    dsc = jnp.sum(du * y, axis=0, keepdims=True)
    dy = du * (1.0 + sc)
    dg = jnp.sum(dy * xn, axis=0, keepdims=True)
    dxn = dy * g
    dx = r * (dxn - xn * jnp.mean(dxn * xn, axis=-1, keepdims=True))
    return dx, dsh, dsc, dg


def _row_tile(ts, width):
    return pl.BlockSpec((ts, width), lambda i: (i, 0))


def _const(shape):
    return pl.BlockSpec(shape, lambda *_: (0,) * len(shape))


def _ffn_chunks(Fd):
    assert sum(FFN_CHUNKS) == Fd
    edges = [sum(FFN_CHUNKS[:k]) for k in range(len(FFN_CHUNKS) + 1)]
    return [slice(a, b) for a, b in zip(edges[:-1], edges[1:])]
def _final_tile(x, g, target):
    r = lax.rsqrt(jnp.mean(x * x, axis=-1, keepdims=True) + EPS)
    xn = x * r
    err = xn * g - target
    loss = 0.5 * jnp.sum(jnp.mean(err * err, axis=-1, keepdims=True))
    dy = err * (1.0 / x.shape[-1])
    dg = jnp.sum(dy * xn, axis=0, keepdims=True)
    dxn = dy * g
    return r * (dxn - xn * jnp.mean(dxn * xn, axis=-1, keepdims=True)), loss, dg


def _ffn_fwd(h, vec, win, wout, name, rider=None, final=None):
    TS = FFN_TS
    S, D = h.shape
    _, Fd, _ = win.shape
    n_in, n_out = (6, 7) if final else (4, 5)

    def body(*refs):
        if rider is None:
            return compute(*refs)
        host, mine = rider.split(refs, n_in, n_out)
        rider.head(mine, pl.program_id(0))
        compute(*host)
        rider.tail(mine, pl.program_id(0), S // TS)

    def compute(*refs):
        h_ref, vec_ref, win_hbm, wout_hbm = refs[:4]
        hn_ref, u_ref, ab_ref, act_ref, f_ref = refs[n_in:n_in + 5]
        win_v, wout_v, sem = refs[n_in + n_out:]
        _load_weights([(win_hbm, win_v), (wout_hbm, wout_v)], sem)
        x = h_ref[...]
        g, sh, sc, gt = (vec_ref[k:k + 1, :] for k in range(4))
        _, _, _, u = _norm_mod(x, g, sc, sh)
        ub = u.astype(BF16)
        u_ref[...] = ub
        acc = jnp.zeros((TS, D), F32)
        for sl in _ffn_chunks(Fd):
            a = _dot_nt(ub, win_v[0, sl, :])
            b = _dot_nt(ub, win_v[1, sl, :])
            act = ((a * jax.nn.sigmoid(a)) * b).astype(BF16)
            ab_ref[0, :, sl] = a.astype(BF16)
            ab_ref[1, :, sl] = b.astype(BF16)
            act_ref[:, sl] = act
            acc = acc + _dot(act, wout_v[sl, :])
        f_ref[...] = acc.astype(BF16)
        hn = x + (0.5 * gt) * acc
        if not final:
            hn_ref[...] = hn
            return
        t_ref, gf_ref = refs[4:6]
        loss_ref, dgf_ref = refs[n_in + 5:n_in + 7]

        @pl.when(pl.program_id(0) == 0)
        def _():
            loss_ref[...] = jnp.zeros_like(loss_ref)
            dgf_ref[...] = jnp.zeros_like(dgf_ref)

        hn_ref[...], loss, dg = _final_tile(hn, gf_ref[0:1, :], t_ref[...])
        loss_ref[...] += loss
        dgf_ref[0:1, :] += dg

    specs = (
        [_row_tile(TS, D), _const((8, D)), ANY, ANY] + ([_row_tile(TS, D), _const((8, D))] if final else []),
        [_row_tile(TS, D), _row_tile(TS, D), pl.BlockSpec((2, TS, Fd), lambda i: (0, i, 0)),
         _row_tile(TS, Fd), _row_tile(TS, D)] + ([_const((8, 128)), _const((8, D))] if final else []),
        [jax.ShapeDtypeStruct((S, D), F32), jax.ShapeDtypeStruct((S, D), BF16),
         jax.ShapeDtypeStruct((2, S, Fd), BF16), jax.ShapeDtypeStruct((S, Fd), BF16),
         jax.ShapeDtypeStruct((S, D), BF16)]
        + ([jax.ShapeDtypeStruct((8, 128), F32), jax.ShapeDtypeStruct((8, D), F32)] if final else []),
        [pltpu.VMEM(win.shape, BF16), pltpu.VMEM(wout.shape, BF16), pltpu.SemaphoreType.DMA((2,))])
    in_specs, out_specs, out_shape, scratch = specs if rider is None else rider.specs(*specs)
    outs = pl.pallas_call(
        body, name=name, grid=(S // TS,), in_specs=in_specs, out_specs=out_specs, out_shape=out_shape,
        scratch_shapes=scratch, compiler_params=_params(),
    )(h, vec, win, wout, *(final or ()), *(rider.arrays if rider else []))
    return outs if rider is None else (outs[:n_out], outs[n_out:])


def _ffn_bwd(dh, h, f, ab, vec, win, wout, name):
    TS = FFN_TS
    S, D = h.shape
    _, Fd, _ = win.shape

    def body(dh_ref, h_ref, f_ref, ab_ref, vec_ref, win_hbm, wout_hbm,
             dhp_ref, dab_ref, df_ref, red_ref, win_v, wout_v, sem):
        _load_weights([(win_hbm, win_v), (wout_hbm, wout_v)], sem)

        @pl.when(pl.program_id(0) == 0)
        def _():
            red_ref[...] = jnp.zeros_like(red_ref)

        dh_v = dh_ref[...]
        x = h_ref[...]
        g, sh, sc, gt = (vec_ref[k:k + 1, :] for k in range(4))
        dgt = jnp.sum((0.5 * f_ref[...].astype(F32)) * dh_v, axis=0, keepdims=True)
        dfb = ((0.5 * gt) * dh_v).astype(BF16)
        df_ref[...] = dfb
        du = jnp.zeros((TS, D), F32)
        for sl in _ffn_chunks(Fd):
            dact = _dot_nt(dfb, wout_v[sl, :])
            av = ab_ref[0, :, sl].astype(F32)
            bv = ab_ref[1, :, sl].astype(F32)
            sg = jax.nn.sigmoid(av)
            da = (dact * bv * (sg * (1.0 + av * (1.0 - sg)))).astype(BF16)
            db = (dact * (av * sg)).astype(BF16)
            dab_ref[0, :, sl] = da
            dab_ref[1, :, sl] = db
            du = du + _dot(da, win_v[0, sl, :]) + _dot(db, win_v[1, sl, :])
        r, xn, y, _ = _norm_mod(x, g, sc, sh)
        dx, dsh, dsc, dg = _norm_mod_bwd(du, r, xn, y, g, sc)
        dhp_ref[...] = dh_v + dx
        red_ref[0:1, :] += dsh
        red_ref[1:2, :] += dsc
        red_ref[2:3, :] += dgt
        red_ref[3:4, :] += dg

    ab_spec = pl.BlockSpec((2, TS, Fd), lambda i: (0, i, 0))
    return pl.pallas_call(
        body, name=name, grid=(S // TS,),
        in_specs=[_row_tile(TS, D), _row_tile(TS, D), _row_tile(TS, D), ab_spec, _const((8, D)), ANY, ANY],
        out_specs=[_row_tile(TS, D), ab_spec, _row_tile(TS, D), _const((8, D))],
        out_shape=[jax.ShapeDtypeStruct((S, D), F32), jax.ShapeDtypeStruct((2, S, Fd), BF16),
                   jax.ShapeDtypeStruct((S, D), BF16), jax.ShapeDtypeStruct((8, D), F32)],
        scratch_shapes=[pltpu.VMEM(win.shape, BF16), pltpu.VMEM(wout.shape, BF16), pltpu.SemaphoreType.DMA((2,))],
        compiler_params=_params(),
    )(dh, h, f, ab, vec, win, wout)


def _wgrad(x, y, name, tm=None, ts=2048, rider=None):
    xb = x.ndim == 3
    nb = x.shape[0] if xb else 0
    S, M = x.shape[-2:]
    N = y.shape[-1]
    tm = tm or M
    ts = min(ts, S)
    nk = S // ts
    grid = (max(nb, 1), M // tm, nk)

    def body(*refs):
        if rider is None:
            return compute(*refs)
        host, mine = rider.split(refs, 2, 1)
        step = (pl.program_id(0) * grid[1] + pl.program_id(1)) * grid[2] + pl.program_id(2)
        rider.head(mine, step)
        compute(*host)
        rider.tail(mine, step, grid[0] * grid[1] * grid[2])

    def compute(x_ref, y_ref, o_ref, acc):
        k = pl.program_id(2)

        @pl.when(k == 0)
        def _():
            acc[...] = jnp.zeros_like(acc)

        acc[...] += _dot_tn(x_ref[...], y_ref[...])

        @pl.when(k == nk - 1)
        def _():
            o_ref[...] = acc[...].astype(BF16)

    x_spec = (pl.BlockSpec((None, ts, tm), lambda b, i, k: (b, k, i)) if xb
              else pl.BlockSpec((ts, tm), lambda b, i, k: (k, i)))
    y_spec = pl.BlockSpec((ts, N), lambda b, i, k: (k, 0))
    if xb:
        o_spec, o_shape = pl.BlockSpec((None, tm, N), lambda b, i, k: (b, i, 0)), (nb, M, N)
    else:
        o_spec, o_shape = pl.BlockSpec((tm, N), lambda b, i, k: (i, 0)), (M, N)
    specs = ([x_spec, y_spec], [o_spec], [jax.ShapeDtypeStruct(o_shape, BF16)], [pltpu.VMEM((tm, N), F32)])
    in_specs, out_specs, out_shape, scratch = specs if rider is None else rider.specs(*specs)
    outs = pl.pallas_call(
        body, name=name, grid=grid, in_specs=in_specs, out_specs=out_specs, out_shape=out_shape,
        scratch_shapes=scratch, compiler_params=_params(),
    )(x, y, *(rider.arrays if rider else []))
    return outs[0] if rider is None else (outs[0], outs[1:])


P_OFF, Q_OFF, K_OFF, V_OFF, G_OFF = 0, 256, 1024, 1792, 2560
IN_WIDTH = 4608


def _first_half_mask(ts):
    lane = lax.broadcasted_iota(jnp.int32, (ts, 128), 1)
    return (lane % HEAD_DIM) < (HEAD_DIM // 2)


def _rope(t, cos, sin_signed, first, sign):
    partner = jnp.where(first, pltpu.roll(t, 96, 1), pltpu.roll(t, 32, 1))
    return t * cos + sign * (partner * sin_signed)


def _res_spec(r):
    return pl.BlockSpec((r, TS // r, GW), lambda i: (0, i, 0))


def _res_shape(S, r, dtype):
    return jax.ShapeDtypeStruct((r, S // r, GW), dtype)


def _to_residues(piece, out_ref, lanes, r, scr):
    if r == 1:
        out_ref[0, :, lanes] = piece.astype(out_ref.dtype)
        return
    for h in range(piece.shape[1] // 128):
        scr[h] = piece[:, h * 128:(h + 1) * 128]
        at = slice(lanes.start + h * 128, lanes.start + (h + 1) * 128)
        for res in range(r):
            out_ref[res, :, at] = scr[h, pl.ds(res, TS // r, stride=r), :].astype(out_ref.dtype)


def _from_residues(in_ref, lanes, r, scr):
    if r == 1:
        return in_ref[0, :, lanes].astype(F32)
    halves = (lanes.stop - lanes.start) // 128
    for h in range(halves):
        at = slice(lanes.start + h * 128, lanes.start + (h + 1) * 128)
        for res in range(r):
            scr[h, pl.ds(res, TS // r, stride=r), :] = in_ref[res, :, at].astype(F32)
    return scr[0] if halves == 1 else jnp.concatenate([scr[0], scr[1]], axis=1)


RES_SCRATCH = (2, TS, 128)


def _mix_in_fwd(h, vec, cos, sin, win, name):
    S, D = h.shape

    def body(h_ref, vec_ref, cos_ref, sin_ref, win_hbm, u_ref, p_ref, gates_ref, *rest):
        qkv_refs, (win_v, sem, scr) = rest[:9], rest[9:]
        _load_weights([(win_hbm, win_v)], sem)
        g, sh, sc = (vec_ref[k:k + 1, :] for k in range(3))
        _, _, _, u = _norm_mod(h_ref[...], g, sc, sh)
        ub = u.astype(BF16)
        u_ref[...] = ub
        p_ref[...] = _dot_nt(ub, win_v[P_OFF:Q_OFF, :])
        cosv, sinv = cos_ref[...], sin_ref[...]
        first = _first_half_mask(TS)
        for which, off in enumerate((Q_OFF, K_OFF, V_OFF)):
            t = _dot_nt(ub, win_v[off:off + 3 * GW, :])
            for gi in range(3):
                for half in range(2):
                    c0 = gi * GW + half * 128
                    piece = t[:, c0:c0 + 128]
                    if which < 2:
                        piece = _rope(piece, cosv, sinv, first, 1.0)
                    _to_residues(piece, qkv_refs[which * 3 + gi], slice(half * 128, (half + 1) * 128),
                                 DILATIONS[gi], scr)
        gates_ref[...] = jax.nn.sigmoid(_dot_nt(ub, win_v[G_OFF:IN_WIDTH, :])).astype(BF16)

    return pl.pallas_call(
        body, name=name, grid=(S // TS,),
        in_specs=[_row_tile(TS, D), _const((8, D)), _row_tile(TS, 128), _row_tile(TS, 128), ANY],
        out_specs=[_row_tile(TS, D), _row_tile(TS, GW), _row_tile(TS, 2 * D)] + [_res_spec(r) for r in DILATIONS] * 3,
        out_shape=[jax.ShapeDtypeStruct((S, D), BF16), jax.ShapeDtypeStruct((S, GW), F32),
                   jax.ShapeDtypeStruct((S, 2 * D), BF16)] + [_res_shape(S, r, BF16) for r in DILATIONS] * 3,
        scratch_shapes=[pltpu.VMEM((IN_WIDTH, D), BF16), pltpu.SemaphoreType.DMA((1,)), pltpu.VMEM(RES_SCRATCH, F32)],
        compiler_params=_params(),
    )(h, vec, cos, sin, win)


def _mix_in_bwd(dh, h, vec, cos, sin, dp, dqkv, dgl, win, name):
    S, D = h.shape

    def body(dh_ref, h_ref, vec_ref, cos_ref, sin_ref, dp_ref, *rest):
        dqkv_refs = rest[:9]
        dgl_ref, win_hbm, dhp_ref, dproj_ref, red_ref, win_v, sem, scr = rest[9:]
        _load_weights([(win_hbm, win_v)], sem)

        @pl.when(pl.program_id(0) == 0)
        def _():
            red_ref[...] = jnp.zeros_like(red_ref)

        cosv, sinv = cos_ref[...], sin_ref[...]
        first = _first_half_mask(TS)
        dproj_ref[:, P_OFF:Q_OFF] = dp_ref[...].astype(BF16)
        for which, off in enumerate((Q_OFF, K_OFF, V_OFF)):
            for gi in range(3):
                for half in range(2):
                    piece = _from_residues(dqkv_refs[which * 3 + gi], slice(half * 128, (half + 1) * 128),
                                           DILATIONS[gi], scr)
                    if which < 2:
                        piece = _rope(piece, cosv, sinv, first, -1.0)
                    c0 = off + gi * GW + half * 128
                    dproj_ref[:, c0:c0 + 128] = piece.astype(BF16)
        dproj_ref[:, G_OFF:IN_WIDTH] = dgl_ref[...]
        du = _dot(dproj_ref[...], win_v[...])
        g, sh, sc = (vec_ref[k:k + 1, :] for k in range(3))
        r, xn, y, _ = _norm_mod(h_ref[...], g, sc, sh)
        dx, dsh, dsc, dg = _norm_mod_bwd(du, r, xn, y, g, sc)
        dhp_ref[...] = dh_ref[...] + dx
        red_ref[0:1, :] += dsh
        red_ref[1:2, :] += dsc
        red_ref[3:4, :] += dg

    return pl.pallas_call(
        body, name=name, grid=(S // TS,),
        in_specs=[_row_tile(TS, D), _row_tile(TS, D), _const((8, D)), _row_tile(TS, 128), _row_tile(TS, 128),
                  _row_tile(TS, GW)] + [_res_spec(r) for r in DILATIONS] * 3 + [_row_tile(TS, 2 * D), ANY],
        out_specs=[_row_tile(TS, D), _row_tile(TS, IN_WIDTH), _const((8, D))],
        out_shape=[jax.ShapeDtypeStruct((S, D), F32), jax.ShapeDtypeStruct((S, IN_WIDTH), BF16),
                   jax.ShapeDtypeStruct((8, D), F32)],
        scratch_shapes=[pltpu.VMEM((IN_WIDTH, D), BF16), pltpu.SemaphoreType.DMA((1,)), pltpu.VMEM(RES_SCRATCH, F32)],
        compiler_params=_params(),
    )(dh, h, vec, cos, sin, dp, *dqkv, dgl, win)


def _pool_lanes(rows):
    lane = lax.broadcasted_iota(jnp.int32, (rows, GW), 1)
    return lane // HEAD_DIM


def _pool_window(rows):
    grp = _pool_lanes(rows)
    w = jnp.full((rows, GW), POOL_WINDOWS[0], jnp.int32)
    for k in range(1, len(POOL_WINDOWS)):
        w = jnp.where(grp == k, POOL_WINDOWS[k], w)
    return grp, w


def _pool_fwd(p, wbd, scale, name, ts=1024):
    S = p.shape[0]
    ext = ts + HALO

    def body(pc_ref, ph_ref, wbd_ref, sc_ref, d_ref, y_ref):
        i = pl.program_id(0)
        cur = pc_ref[...]
        halo = jnp.where(i > 0, ph_ref[...], 0.0)
        s = jnp.concatenate([halo, cur], axis=0)
        grp, w = _pool_window(ext)
        sel = jnp.zeros((ext, GW), F32)
        for k, wk in enumerate(POOL_WINDOWS):
            s = s + pltpu.roll(s, wk // 2, 0)
            sel = jnp.where(grp == k, s, sel)
        t = i * ts + lax.broadcasted_iota(jnp.int32, (ts, GW), 0)
        count = jnp.minimum(t + 1, w[HALO:]).astype(F32)
        d = (sel[HALO:] / count - cur).astype(BF16)
        d_ref[...] = d
        y_ref[...] = (_dot(d, wbd_ref[...]) * sc_ref[...]).astype(BF16)

    return pl.pallas_call(
        body, name=name, grid=(S // ts,),
        in_specs=[_row_tile(ts, GW),
                  pl.BlockSpec((HALO, GW), lambda i: (jnp.maximum(i * (ts // HALO) - 1, 0), 0)),
                  _const((GW, GW)), _const((1, GW))],
        out_specs=[_row_tile(ts, GW), _row_tile(ts, GW)],
        out_shape=[jax.ShapeDtypeStruct((S, GW), BF16), jax.ShapeDtypeStruct((S, GW), BF16)],
        compiler_params=_params(),
    )(p, p, wbd, scale)


def _pool_bwd(dy, d, wbd, scale, name, ts=1024):
    S = dy.shape[0]
    ext = ts + HALO
    nsteps = S // ts
    last_halo = S // HALO - 1

    def body(dyc_ref, dyh_ref, d_ref, wbd_ref, sc_ref, dp_ref, dw_ref, ds_ref):
        i = pl.program_id(0)

        @pl.when(i == 0)
        def _():
            dw_ref[...] = jnp.zeros_like(dw_ref)
            ds_ref[...] = jnp.zeros_like(ds_ref)

        dyc = dyc_ref[...]
        dyh = jnp.where(i < nsteps - 1, dyh_ref[...], 0.0)
        dys = (jnp.concatenate([dyc, dyh], axis=0) * sc_ref[...]).astype(BF16)
        dd = _dot_nt(dys, wbd_ref[...])
        grp, w = _pool_window(ext)
        t = i * ts + lax.broadcasted_iota(jnp.int32, (ext, GW), 0)
        s = dd / jnp.minimum(t + 1, w).astype(F32)
        sel = jnp.zeros((ext, GW), F32)
        for k, wk in enumerate(POOL_WINDOWS):
            s = s + pltpu.roll(s, ext - wk // 2, 0)
            sel = jnp.where(grp == k, s, sel)
        dp_ref[...] = sel[:ts] - dd[:ts]
        dv = d_ref[...]
        z = _dot(dv, wbd_ref[...])
        ds_ref[0:1, :] += jnp.sum(dyc * z, axis=0, keepdims=True)
        dw_ref[...] += _dot_tn(dv, dys[:ts])

    return pl.pallas_call(
        body, name=name, grid=(nsteps,),
        in_specs=[_row_tile(ts, GW),
                  pl.BlockSpec((HALO, GW), lambda i: (jnp.minimum((i + 1) * (ts // HALO), last_halo), 0)),
                  _row_tile(ts, GW), _const((GW, GW)), _const((1, GW))],
        out_specs=[_row_tile(ts, GW), _const((GW, GW)), _const((8, GW))],
        out_shape=[jax.ShapeDtypeStruct((S, GW), F32), jax.ShapeDtypeStruct((GW, GW), F32),
                   jax.ShapeDtypeStruct((8, GW), F32)],
        compiler_params=_params(),
    )(dy, dy, d, wbd, scale)


def _head_id(rows):
    return lax.broadcasted_iota(jnp.int32, (rows, GW), 1) // HEAD_DIM


def _stack_heads(t, hid):
    return jnp.concatenate([jnp.where(hid == h, t, jnp.zeros_like(t)) for h in range(HEADS)], axis=0)


def _unstack_heads(t_all, hid):
    out = jnp.zeros((QB, GW), F32)
    for h in range(HEADS):
        out = jnp.where(hid == h, t_all[h * QB:(h + 1) * QB], out)
    return out


def _band_mask(n):
    row = lax.broadcasted_iota(jnp.int32, (HEADS * QB, 2 * QB), 0) % QB
    col = lax.broadcasted_iota(jnp.int32, (HEADS * QB, 2 * QB), 1)
    rel = row + QB - col
    return (rel >= 0) & (rel <= BAND) & ((col >= QB) | (n > 0))


FWD_STREAMS, BWD_STREAMS = 16, 8


def _streams(r, nb, most):
    if r > 1:
        ns = min(r, most)
        return nb, [(lambda rb, l=l: ns * rb + l, 0) for l in range(ns)]
    ns = min(most, nb)
    return nb // ns, [(lambda rb: 0, l * (nb // ns)) for l in range(ns)]


def _attn_fwd(q, k, v, name):
    r, L, _ = q.shape
    nbs, streams = _streams(r, L // QB, FWD_STREAMS)
    ns = len(streams)
    grid = (max(r // ns, 1), nbs)

    def cur(res, off):
        return pl.BlockSpec((None, QB, GW), lambda rb, n: (res(rb), n + off, 0))

    def prev(res, off):
        return pl.BlockSpec((None, QB, GW), lambda rb, n: (res(rb), jnp.maximum(n + off - 1, 0), 0))

    def body(*refs):
        n = pl.program_id(1)
        hid = _head_id(QB)
        o_ref, lse_ref = refs[5 * len(streams):]
        for l, (_, off) in enumerate(streams):
            q_ref, kp_ref, kc_ref, vp_ref, vc_ref = refs[5 * l:5 * l + 5]
            qs = _stack_heads(q_ref[...], hid)
            kc = jnp.concatenate([kp_ref[...], kc_ref[...]], axis=0)
            vc = jnp.concatenate([vp_ref[...], vc_ref[...]], axis=0)
            s = _dot_nt(qs, kc) * (HEAD_DIM ** -0.5)
            s = jnp.where(_band_mask(n + off), s, -jnp.inf)
            m = jnp.max(s, axis=-1, keepdims=True)
            e = jnp.exp(s - m)
            den = jnp.sum(e, axis=-1, keepdims=True)
            lse = m + jnp.log(den)
            pr = (e * (1.0 / den)).astype(BF16)
            o_ref[l] = _unstack_heads(_dot(pr, vc), hid).astype(BF16)
            lse_ref[l] = _unstack_heads(jnp.broadcast_to(lse, (HEADS * QB, GW)), hid)

    in_specs, args = [], []
    for res, off in streams:
        in_specs += [cur(res, off), prev(res, off), cur(res, off), prev(res, off), cur(res, off)]
        args += [q, k, k, v, v]
    out = jax.ShapeDtypeStruct((ns * grid[0], nbs * QB, GW), F32)
    both = pl.BlockSpec((ns, QB, GW), lambda rb, n: (rb, n, 0))
    o, lse = pl.pallas_call(
        body, name=name, grid=grid, in_specs=in_specs, out_specs=[both, both],
        out_shape=[jax.ShapeDtypeStruct(out.shape, BF16), out],
        compiler_params=_params(),
    )(*args)
    return o.reshape(q.shape), lse.reshape(q.shape)


def _head_rows(t_full, hid):
    return jnp.concatenate(
        [jnp.max(jnp.where(hid == h, t_full, -jnp.inf), axis=-1, keepdims=True) for h in range(HEADS)], axis=0)


def _attn_bwd(q, k, v, do, lse, cterm, name):
    r, L, _ = q.shape
    nbs, streams = _streams(r, L // QB, BWD_STREAMS)
    ns = len(streams)
    parts = r == 1

    def spec(res, index):
        return pl.BlockSpec((None, QB, GW), lambda rb, n: (res(rb), index(n), 0))

    def body(*refs):
        dq_ref, dk_ref, dv_ref, carry_k, carry_v, seam_k, seam_v = refs[8 * ns:]
        n = pl.program_id(1)

        @pl.when(n == 0)
        def _():
            carry_k[...] = jnp.zeros_like(carry_k)
            carry_v[...] = jnp.zeros_like(carry_v)

        @pl.when(n < nbs)
        def _():
            hid = _head_id(QB)
            for l, (_, off) in enumerate(streams):
                q_ref, do_ref, lse_ref, c_ref, kp_ref, kc_ref, vp_ref, vc_ref = refs[8 * l:8 * l + 8]
                qs = _stack_heads(q_ref[...], hid)
                dos = _stack_heads(do_ref[...], hid)
                kc = jnp.concatenate([kp_ref[...], kc_ref[...]], axis=0)
                vc = jnp.concatenate([vp_ref[...], vc_ref[...]], axis=0)
                s = _dot_nt(qs, kc) * (HEAD_DIM ** -0.5)
                s = jnp.where(_band_mask(n + off), s, -jnp.inf)
                p = jnp.exp(s - _head_rows(lse_ref[...], hid))
                dp = _dot_nt(dos, vc)
                ds = (p * (dp + _head_rows(c_ref[...], hid)) * (HEAD_DIM ** -0.5)).astype(BF16)
                dq_ref[l] = _unstack_heads(_dot(ds, kc), hid).astype(BF16)
                dkc = _dot_tn(ds, qs)
                dvc = _dot_tn(p.astype(BF16), dos)
                if parts and l > 0:
                    @pl.when(n == 0)
                    def _():
                        seam_k[l] = dkc[:QB]
                        seam_v[l] = dvc[:QB]
                dk_ref[l] = (carry_k[l] + dkc[:QB]).astype(BF16)
                dv_ref[l] = (carry_v[l] + dvc[:QB]).astype(BF16)
                carry_k[l] = dkc[QB:]
                carry_v[l] = dvc[QB:]

        @pl.when(n == nbs)
        def _():
            for l in range(ns):
                if parts and l + 1 < ns:
                    dk_ref[l] = (carry_k[l] + seam_k[l + 1]).astype(BF16)
                    dv_ref[l] = (carry_v[l] + seam_v[l + 1]).astype(BF16)
                else:
                    dk_ref[l] = carry_k[l].astype(BF16)
                    dv_ref[l] = carry_v[l].astype(BF16)

    in_specs, args = [], []
    for res, off in streams:
        qside = functools.partial(lambda n, off: jnp.minimum(n, nbs - 1) + off, off=off)
        kprev = functools.partial(lambda n, off: jnp.maximum(jnp.minimum(n, nbs) - 1 + off, 0), off=off)
        in_specs += [spec(res, qside)] * 4 + [spec(res, kprev), spec(res, qside)] * 2
        args += [q, do, lse, cterm, k, k, v, v]
    out = jax.ShapeDtypeStruct((ns * max(r // ns, 1), nbs * QB, GW), BF16)
    qout = pl.BlockSpec((ns, QB, GW), lambda rb, n: (rb, jnp.minimum(n, nbs - 1), 0))
    kout = pl.BlockSpec((ns, QB, GW), lambda rb, n: (rb, jnp.maximum(n - 1, 0), 0))
    buf = pltpu.VMEM((ns, QB, GW), F32)
    outs = pl.pallas_call(
        body, name=name, grid=(max(r // ns, 1), nbs + 1),
        in_specs=in_specs, out_specs=[qout, kout, kout], out_shape=[out, out, out],
        scratch_shapes=[buf, buf, buf, buf],
        compiler_params=_params(),
    )(*args)
    return [t.reshape(q.shape) for t in outs]


def _token_order(refs, scr):
    return [_from_residues(ref, slice(0, GW), r, scr) for ref, r in zip(refs, DILATIONS)]


def _group_weights(lses):
    l0, l1, l2 = lses
    m = jnp.maximum(jnp.maximum(l0, l1), l2)
    e = [jnp.exp(l - m) for l in (l0, l1, l2)]
    den = e[0] + e[1] + e[2]
    return [ei / den for ei in e]


def _mix_out_fwd(h, vec, gates, ypool, o3, lse3, wpb, wab, wout, name):
    S, D = h.shape

    def body(h_ref, vec_ref, gates_ref, yp_ref, o0, o1, o2, l0, l1, l2, wpb_hbm, wab_hbm, wout_hbm,
             hn_ref, ya_ref, merged_ref, tm_ref, wpb_v, wab_v, wout_v, sem, scr):
        _load_weights([(wpb_hbm, wpb_v), (wab_hbm, wab_v), (wout_hbm, wout_v)], sem)
        gt = vec_ref[3:4, :]
        wts = _group_weights(_token_order((l0, l1, l2), scr))
        og = _token_order((o0, o1, o2), scr)
        ya = (wts[0] * og[0] + wts[1] * og[1] + wts[2] * og[2]).astype(BF16)
        ya_ref[...] = ya
        merged = (gates_ref[:, :D].astype(F32) * _dot(yp_ref[...], wpb_v[...])
                  + gates_ref[:, D:].astype(F32) * _dot(ya, wab_v[...])).astype(BF16)
        merged_ref[...] = merged
        tm = _dot(merged, wout_v[...])
        tm_ref[...] = tm.astype(BF16)
        hn_ref[...] = h_ref[...] + gt * tm

    grp = _row_tile(TS, GW)
    res = [_res_spec(r) for r in DILATIONS]
    return pl.pallas_call(
        body, name=name, grid=(S // TS,),
        in_specs=[_row_tile(TS, D), _const((8, D)), _row_tile(TS, 2 * D), grp] + res * 2 + [ANY, ANY, ANY],
        out_specs=[_row_tile(TS, D), grp, _row_tile(TS, D), _row_tile(TS, D)],
        out_shape=[jax.ShapeDtypeStruct((S, D), F32), jax.ShapeDtypeStruct((S, GW), BF16),
                   jax.ShapeDtypeStruct((S, D), BF16), jax.ShapeDtypeStruct((S, D), BF16)],
        scratch_shapes=[pltpu.VMEM((GW, D), BF16), pltpu.VMEM((GW, D), BF16), pltpu.VMEM((D, D), BF16),
                        pltpu.SemaphoreType.DMA((3,)), pltpu.VMEM(RES_SCRATCH, F32)],
        compiler_params=_params(),
    )(h, vec, gates, ypool, *o3, *lse3, wpb, wab, wout)


def _mix_out_bwd(dh, tm, vec, gates, ypool, o3, lse3, wpb, wab, wout, name, rider=None):
    S, D = dh.shape

    def body(*refs):
        if rider is None:
            return compute(*refs)
        host, mine = rider.split(refs, 14, 12)
        rider.head(mine, pl.program_id(0))
        compute(*host)
        rider.tail(mine, pl.program_id(0), S // TS)

    def compute(dh_ref, tm_ref, vec_ref, gates_ref, yp_ref, o0, o1, o2, l0, l1, l2, wpb_hbm, wab_hbm, wout_hbm,
                dtm_ref, dgl_ref, dypb_ref, dyab_ref, dyp_ref, do0, do1, do2, c0, c1, c2, red_ref,
                wpb_v, wab_v, wout_v, sem, scr):
        _load_weights([(wpb_hbm, wpb_v), (wab_hbm, wab_v), (wout_hbm, wout_v)], sem)

        @pl.when(pl.program_id(0) == 0)
        def _():
            red_ref[...] = jnp.zeros_like(red_ref)

        gt = vec_ref[3:4, :]
        dh_v = dh_ref[...]
        red_ref[2:3, :] += jnp.sum(tm_ref[...].astype(F32) * dh_v, axis=0, keepdims=True)
        dtm = (gt * dh_v).astype(BF16)
        dtm_ref[...] = dtm
        dm = _dot_nt(dtm, wout_v[...])
        wts = _group_weights(_token_order((l0, l1, l2), scr))
        og = _token_order((o0, o1, o2), scr)
        ya = wts[0] * og[0] + wts[1] * og[1] + wts[2] * og[2]
        ypb = _dot(yp_ref[...], wpb_v[...])
        yab = _dot(ya.astype(BF16), wab_v[...])
        gp = gates_ref[:, :D].astype(F32)
        ga = gates_ref[:, D:].astype(F32)
        dgl_ref[:, :D] = (dm * ypb * gp * (1.0 - gp)).astype(BF16)
        dgl_ref[:, D:] = (dm * yab * ga * (1.0 - ga)).astype(BF16)
        dypb = (dm * gp).astype(BF16)
        dyab = (dm * ga).astype(BF16)
        dypb_ref[...] = dypb
        dyab_ref[...] = dyab
        dyp_ref[...] = _dot_nt(dypb, wpb_v[...])
        dya = _dot_nt(dyab, wab_v[...])
        row = lax.broadcasted_iota(jnp.int32, (GW, GW), 0) // HEAD_DIM
        col = lax.broadcasted_iota(jnp.int32, (GW, GW), 1) // HEAD_DIM
        ones = jnp.where(row == col, 1.0, 0.0).astype(F32)
        tot = jnp.dot(dya * ya, ones, preferred_element_type=F32, precision=lax.Precision.HIGHEST)
        for wg, do_ref, c_ref, r in zip(wts, (do0, do1, do2), (c0, c1, c2), DILATIONS):
            _to_residues(wg * dya, do_ref, slice(0, GW), r, scr)
            _to_residues(-(wg * tot), c_ref, slice(0, GW), r, scr)

    grp = _row_tile(TS, GW)
    res = [_res_spec(r) for r in DILATIONS]
    specs = (
        [_row_tile(TS, D), _row_tile(TS, D), _const((8, D)), _row_tile(TS, 2 * D), grp] + res * 2
        + [ANY, ANY, ANY],
        [_row_tile(TS, D), _row_tile(TS, 2 * D), _row_tile(TS, D), _row_tile(TS, D), grp]
        + res * 2 + [_const((8, D))],
        [jax.ShapeDtypeStruct((S, D), BF16), jax.ShapeDtypeStruct((S, 2 * D), BF16),
         jax.ShapeDtypeStruct((S, D), BF16), jax.ShapeDtypeStruct((S, D), BF16), jax.ShapeDtypeStruct((S, GW), F32)]
        + [_res_shape(S, r, BF16) for r in DILATIONS] + [_res_shape(S, r, F32) for r in DILATIONS]
        + [jax.ShapeDtypeStruct((8, D), F32)],
        [pltpu.VMEM((GW, D), BF16), pltpu.VMEM((GW, D), BF16), pltpu.VMEM((D, D), BF16),
         pltpu.SemaphoreType.DMA((3,)), pltpu.VMEM(RES_SCRATCH, F32)])
    in_specs, out_specs, out_shape, scratch = specs if rider is None else rider.specs(*specs)
    outs = pl.pallas_call(
        body, name=name, grid=(S // TS,), in_specs=in_specs, out_specs=out_specs, out_shape=out_shape,
        scratch_shapes=scratch, compiler_params=_params(),
    )(dh, tm, vec, gates, ypool, *o3, *lse3, wpb, wab, wout, *(rider.arrays if rider else []))
    return outs if rider is None else (outs[:12], outs[12:])


def _ada_mod(c_all, w, b, name):
    def body(c_ref, w_ref, b_ref, cond_ref, mod_ref):
        cv = c_ref[...]
        cond = cv * jax.nn.sigmoid(cv)
        cond_ref[...] = cond
        mod_ref[...] = jnp.dot(cond, w_ref[...], preferred_element_type=F32,
                               precision=lax.Precision.HIGHEST) + b_ref[...]

    return pl.pallas_call(
        body, name=name,
        out_shape=[jax.ShapeDtypeStruct(c_all.shape, F32), jax.ShapeDtypeStruct((c_all.shape[0], w.shape[1]), F32)],
        compiler_params=_params(),
    )(c_all, w, b)


def _adamw_math(w, g, m, v):
    m = ADAM_B1 * m + (1.0 - ADAM_B1) * g
    v = ADAM_B2 * v + (1.0 - ADAM_B2) * (g * g)
    m_hat = m / (1.0 - ADAM_B1 ** ADAM_STEP)
    v_hat = v / (1.0 - ADAM_B2 ** ADAM_STEP)
    delta = -ADAM_LR * (m_hat / (jnp.sqrt(v_hat) + ADAM_EPS) + ADAM_WD * w)
    return delta, m, v


def _adamw_many(ws, gs, ms, vs, name):
    n = len(ws)

    def body(*refs):
        for k in range(n):
            w_ref, g_ref, m_ref, v_ref = (refs[j * n + k] for j in range(4))
            d_ref, mo_ref, vo_ref = (refs[(4 + j) * n + k] for j in range(3))
            d_ref[...], mo_ref[...], vo_ref[...] = _adamw_math(w_ref[...], g_ref[...], m_ref[...], v_ref[...])

    outs = pl.pallas_call(
        body, name=name, out_shape=[jax.ShapeDtypeStruct(t.shape, F32) for t in ws] * 3,
        compiler_params=_params(),
    )(*ws, *gs, *ms, *vs)
    return outs[:n], outs[n:2 * n], outs[2 * n:]


def _ada_grad_adamw(cond_t, dmod, w, m, v, name, tr=256):
    R, C = w.shape
    nb = dmod.shape[0]

    def body(ct_ref, dm_ref, w_ref, m_ref, v_ref, g_ref, d_ref, mo_ref, vo_ref):
        ct = ct_ref[...]
        dm = dm_ref[...]
        g = jnp.zeros((tr, C), F32)
        for bi in range(nb):
            g = g + ct[:, bi:bi + 1] * dm[bi:bi + 1, :]
        g_ref[...] = g
        d_ref[...], mo_ref[...], vo_ref[...] = _adamw_math(w_ref[...], g, m_ref[...], v_ref[...])

    spec = _row_tile(tr, C)
    out = jax.ShapeDtypeStruct((R, C), F32)
    return pl.pallas_call(
        body, name=name, grid=(R // tr,),
        in_specs=[_row_tile(tr, nb), _const((nb, C)), spec, spec, spec],
        out_specs=[spec] * 4, out_shape=[out] * 4,
        compiler_params=_params(),
    )(cond_t, dmod, w, m, v)


def _row_step(rows, cap=256):
    for cand in range(cap, 15, -16):
        if rows % cand == 0:
            return cand
    return rows


def _slot_sum(x_ref):
    acc = x_ref[0].astype(F32)
    for k in range(1, x_ref.shape[0]):
        acc = acc + x_ref[k].astype(F32)
    return acc


def _sum_slots(x, name, out_dtype=F32):
    n, R, C = x.shape
    tr = _row_step(R)

    def body(x_ref, o_ref):
        o_ref[...] = _slot_sum(x_ref).astype(out_dtype)

    return pl.pallas_call(
        body, name=name, grid=(R // tr,),
        in_specs=[pl.BlockSpec((n, tr, C), lambda i: (0, i, 0))],
        out_specs=_row_tile(tr, C), out_shape=jax.ShapeDtypeStruct((R, C), out_dtype),
        compiler_params=_params(),
    )(x)


def _sum_pair(core, g, recv, name):
    _, _, R, C = g.shape
    tr = _row_step(R, cap=1024)

    def body(core_ref, g_ref, r_ref, o_ref):
        o_ref[...] = (g_ref[...].astype(F32) + r_ref[...].astype(F32)).astype(BF16)

    return pl.pallas_call(
        body, name=name, out_shape=jax.ShapeDtypeStruct((4, R, C), BF16),
        grid_spec=pltpu.PrefetchScalarGridSpec(
            num_scalar_prefetch=1, grid=(4, R // tr),
            in_specs=[pl.BlockSpec((None, None, tr, C), lambda k, i, core_ref: (k, core_ref[0], i, 0)),
                      pl.BlockSpec((None, tr, C), lambda k, i, core_ref: (k, i, 0))],
            out_specs=pl.BlockSpec((None, tr, C), lambda k, i, core_ref: (k, i, 0))),
        compiler_params=_params(),
    )(core, g, recv)


def _sum_adamw(chip, own, recv, w, m, v, name):
    _, R, C = own.shape
    tr = _row_step(R, cap=512)

    def body(chip_ref, own_ref, r_ref, w_ref, m_ref, v_ref, g_ref, d_ref, mo_ref, vo_ref):
        g = own_ref[...].astype(F32) + _slot_sum(r_ref)
        g_ref[...] = g
        d_ref[...], mo_ref[...], vo_ref[...] = _adamw_math(w_ref[...], g, m_ref[...], v_ref[...])

    spec = pl.BlockSpec((tr, C), lambda i, chip_ref: (i, 0))
    out = jax.ShapeDtypeStruct((R, C), F32)
    return pl.pallas_call(
        body, name=name, out_shape=[out] * 4,
        grid_spec=pltpu.PrefetchScalarGridSpec(
            num_scalar_prefetch=1, grid=(R // tr,),
            in_specs=[pl.BlockSpec((None, tr, C), lambda i, chip_ref: (chip_ref[0], i, 0)),
                      pl.BlockSpec((3, tr, C), lambda i, chip_ref: (0, i, 0)), spec, spec, spec],
            out_specs=[spec] * 4),
        compiler_params=_params(),
    )(chip, own, recv, w, m, v)


def _sum_adamw_many(chip, entries, name, rider=None):
    n = len(entries)
    C = entries[0][0].shape[2]
    rows = [e[0].shape[1] for e in entries]
    trs = [_row_step(r, cap=64) for r in rows]
    nblk = [r // t for r, t in zip(rows, trs)]
    starts = [sum(nblk[:k]) for k in range(n)]
    nsteps = sum(nblk)

    def block(k, i):
        return jnp.clip(i - starts[k], 0, nblk[k] - 1)

    def compute(chip_ref, *refs):
        i = pl.program_id(0)
        for k in range(n):
            own_ref, r_ref, w_ref, m_ref, v_ref = refs[5 * k:5 * k + 5]
            g_ref, d_ref, mo_ref, vo_ref = refs[5 * n + 4 * k:5 * n + 4 * k + 4]

            @pl.when((i >= starts[k]) & (i < starts[k] + nblk[k]))
            def _():
                g = own_ref[...].astype(F32) + _slot_sum(r_ref)
                g_ref[...] = g
                d_ref[...], mo_ref[...], vo_ref[...] = _adamw_math(w_ref[...], g, m_ref[...], v_ref[...])

    def body(chip_ref, *refs):
        if rider is None:
            return compute(chip_ref, *refs)
        host, mine = rider.split(refs, 5 * n, 4 * n)
        rider.head(mine, pl.program_id(0))
        compute(chip_ref, *host)
        rider.tail(mine, pl.program_id(0), nsteps)

    in_specs, out_specs, out_shape, args = [], [], [], []
    for k, (own, recv, w, m, v) in enumerate(entries):
        plain = pl.BlockSpec((trs[k], C), lambda i, chip_ref, k=k: (block(k, i), 0))
        in_specs += [pl.BlockSpec((None, trs[k], C), lambda i, chip_ref, k=k: (chip_ref[0], block(k, i), 0)),
                     pl.BlockSpec((3, trs[k], C), lambda i, chip_ref, k=k: (0, block(k, i), 0)), plain, plain, plain]
        out_specs += [plain] * 4
        out_shape += [jax.ShapeDtypeStruct((rows[k], C), F32)] * 4
        args += [own, recv, w, m, v]
    scratch = []
    if rider is not None:
        in_specs, out_specs, out_shape, scratch = rider.specs(in_specs, out_specs, out_shape, scratch)
    outs = pl.pallas_call(
        body, name=name, out_shape=out_shape,
        grid_spec=pltpu.PrefetchScalarGridSpec(
            num_scalar_prefetch=1, grid=(nsteps,), in_specs=in_specs, out_specs=out_specs, scratch_shapes=scratch),
        compiler_params=_params(),
    )(chip, *args, *(rider.arrays if rider else []))
    per_weight = [outs[4 * k:4 * k + 4] for k in range(n)]
    return per_weight if rider is None else (per_weight, outs[4 * n:])


def _place():
    return lax.axis_index("x"), lax.axis_index("y"), lax.axis_index("c")


def _gather_steps(x_refs, out_refs, send_sems, recv_sems):
    n = len(x_refs)
    x, y, c = _place()
    me, sibling = (x, y, c), (x, y, 1 - c)
    chips = [(1 - x, y), (x, 1 - y), (1 - x, 1 - y)]

    def rows(a, px, py, pc):
        return out_refs[a].at[4 * px + 2 * py + pc]

    def copy(a, k, block, to, src=None):
        return pltpu.make_async_remote_copy(
            src_ref=rows(a, *block) if src is None else src, dst_ref=rows(a, *block),
            send_sem=send_sems.at[a, k], recv_sem=recv_sems.at[a, k], device_id=to, device_id_type=MESH)

    def first(a):
        return [copy(a, 0, me, sibling, src=x_refs[a])] + [
            copy(a, 1 + j, me, (*chip, c), src=x_refs[a]) for j, chip in enumerate(chips)]

    def passed(a, j):
        return copy(a, 4 + j, (*chips[j], c), sibling)

    def start():
        for a in range(n):
            for cp in first(a):
                cp.start()

    def relay():
        for j, chip in enumerate(chips):
            for a in range(n):
                copy(a, 1 + j, (*chip, c), me).wait_recv()
                passed(a, j).start()

    def finish():
        for a in range(n):
            copy(a, 0, sibling, me).wait_recv()
            for j, chip in enumerate(chips):
                copy(a, 4 + j, (*chip, 1 - c), me).wait_recv()
        for a in range(n):
            for cp in first(a) + [passed(a, j) for j in range(3)]:
                cp.wait_send()

    return start, relay, finish


def _gather_tree_steps(x_refs, out_refs, send_sems, recv_sems):
    n = len(x_refs)
    x, y, c = _place()
    me, sibling = (x, y, c), (x, y, 1 - c)
    xn, yn, dg = (1 - x, y), (x, 1 - y), (1 - x, 1 - y)

    def rows(a, px, py, pc):
        return out_refs[a].at[4 * px + 2 * py + pc]

    def copy(a, k, block, to, src=None):
        return pltpu.make_async_remote_copy(
            src_ref=rows(a, *block) if src is None else src, dst_ref=rows(a, *block),
            send_sem=send_sems.at[a, k], recv_sem=recv_sems.at[a, k], device_id=to, device_id_type=MESH)

    def own(a):
        return [copy(a, 0, me, sibling, src=x_refs[a]), copy(a, 1, me, (*xn, c), src=x_refs[a]),
                copy(a, 2, me, (*yn, c), src=x_refs[a])]

    def north_hands_on(a):
        return copy(a, 3, (*xn, c), (*yn, c))

    def south_hands_on(a):
        return copy(a, 3, (*yn, c), (*xn, c))

    def to_sibling(a):
        return [copy(a, 4, (*xn, c), sibling), copy(a, 5, (*yn, c), sibling), copy(a, 6, (*dg, c), sibling)]

    def start():
        for a in range(n):
            for cp in own(a):
                cp.start()

    def relay_neighbours():
        for a in range(n):
            copy(a, 1, (*xn, c), me).wait_recv()
            to_sibling(a)[0].start()

        @pl.when(c == 1)
        def _():
            for a in range(n):
                north_hands_on(a).start()

        for a in range(n):
            copy(a, 2, (*yn, c), me).wait_recv()
            to_sibling(a)[1].start()

        @pl.when(c == 0)
        def _():
            for a in range(n):
                south_hands_on(a).start()

    def relay_diagonal():
        for a in range(n):
            copy(a, 3, (*dg, c), me).wait_recv()
            to_sibling(a)[2].start()

    def finish():
        for a in range(n):
            copy(a, 0, sibling, me).wait_recv()
            copy(a, 4, (*xn, 1 - c), me).wait_recv()
            copy(a, 5, (*yn, 1 - c), me).wait_recv()
            copy(a, 6, (*dg, 1 - c), me).wait_recv()
        for a in range(n):
            for cp in own(a) + to_sibling(a):
                cp.wait_send()

        @pl.when(c == 1)
        def _():
            for a in range(n):
                north_hands_on(a).wait_send()

        @pl.when(c == 0)
        def _():
            for a in range(n):
                south_hands_on(a).wait_send()

    return start, relay_neighbours, relay_diagonal, finish


def _all_gather_tree(arrs, name, gather=()):
    n, extra = len(arrs), len(gather)

    def body(*refs):
        sems = refs[2 * (n + extra):]
        if extra:
            g_start, g_relay, g_finish = _small_gather_steps(
                refs[n:n + extra], refs[2 * n + extra:2 * (n + extra)], *sems[2:])
            g_start()
        for step in _gather_tree_steps(refs[:n], refs[n + extra:2 * n + extra], *sems[:2]):
            step()
        if extra:
            g_relay()
            g_finish()

    return pl.pallas_call(
        body, name=name,
        out_shape=[jax.ShapeDtypeStruct((N_DEV,) + t.shape, t.dtype) for t in list(arrs) + list(gather)],
        in_specs=[ANY] * (n + extra), out_specs=[ANY] * (n + extra),
        scratch_shapes=[pltpu.SemaphoreType.DMA((n, 7)), pltpu.SemaphoreType.DMA((n, 7))]
        + (_small_gather_scratch(extra) if extra else []),
    )(*arrs, *gather)


def _all_gather(arrs, name, own=True):
    n = len(arrs)

    def body(*refs):
        x_refs, out_refs = refs[:n], refs[n:2 * n]
        send_sems, recv_sems, local_sems = refs[2 * n:]
        me = 4 * lax.axis_index("x") + 2 * lax.axis_index("y") + lax.axis_index("c")
        mine = [pltpu.make_async_copy(x_refs[a], out_refs[a].at[me], local_sems.at[a]) for a in range(n)] if own else []
        for cp in mine:
            cp.start()
        for step in _gather_steps(x_refs, out_refs, send_sems, recv_sems):
            step()
        for cp in mine:
            cp.wait()

    return pl.pallas_call(
        body, name=name, out_shape=[jax.ShapeDtypeStruct((N_DEV,) + t.shape, t.dtype) for t in arrs],
        in_specs=[ANY] * n, out_specs=[ANY] * n,
        scratch_shapes=[pltpu.SemaphoreType.DMA((n, 7)), pltpu.SemaphoreType.DMA((n, 7)),
                        pltpu.SemaphoreType.DMA((n,))],
    )(*arrs)


def _pair_exchange_steps(g_refs, out_refs, send_sems, recv_sems):
    x, y, c = _place()

    def give():
        return [pltpu.make_async_remote_copy(
            src_ref=g_refs[a].at[pl.ds(0, 4), 1 - c], dst_ref=out_refs[a], send_sem=send_sems.at[a],
            recv_sem=recv_sems.at[a], device_id=(x, y, 1 - c), device_id_type=MESH) for a in range(len(g_refs))]

    def start():
        for cp in give():
            cp.start()

    def finish():
        for cp in give():
            cp.wait()

    return start, finish


def _pair_exchange(arrs, name):
    n = len(arrs)

    def body(*refs):
        for step in _pair_exchange_steps(refs[:n], refs[n:2 * n], *refs[2 * n:]):
            step()

    return pl.pallas_call(
        body, name=name,
        out_shape=[jax.ShapeDtypeStruct((4,) + t.shape[2:], t.dtype) for t in arrs],
        in_specs=[ANY] * n, out_specs=[ANY] * n,
        scratch_shapes=[pltpu.SemaphoreType.DMA((n,)), pltpu.SemaphoreType.DMA((n,))],
    )(*arrs)


def _chip_exchange_steps(p_refs, out_refs, send_sems, recv_sems):
    x, y, c = _place()
    chips = [(1 - x, y), (x, 1 - y), (1 - x, 1 - y)]

    def copies():
        return [pltpu.make_async_remote_copy(
            src_ref=p_refs[a].at[2 * px + py], dst_ref=out_refs[a].at[j], send_sem=send_sems.at[a, j],
            recv_sem=recv_sems.at[a, j], device_id=(px, py, c), device_id_type=MESH)
            for a in range(len(p_refs)) for j, (px, py) in enumerate(chips)]

    def start():
        for cp in copies():
            cp.start()

    def finish():
        for cp in copies():
            cp.wait()

    return start, finish


def _small_gather_steps(x_refs, out_refs, send_sems, recv_sems, local_sems):
    me = 4 * lax.axis_index("x") + 2 * lax.axis_index("y") + lax.axis_index("c")
    start, relay, finish = _gather_steps(x_refs, out_refs, send_sems, recv_sems)

    def mine():
        return [pltpu.make_async_copy(x_refs[a], out_refs[a].at[me], local_sems.at[a]) for a in range(len(x_refs))]

    def start_all():
        for cp in mine():
            cp.start()
        start()

    def finish_all():
        finish()
        for cp in mine():
            cp.wait()

    return start_all, relay, finish_all


def _small_gather_scratch(k):
    return [pltpu.SemaphoreType.DMA((k, 7)), pltpu.SemaphoreType.DMA((k, 7)), pltpu.SemaphoreType.DMA((k,))]


def _chip_exchange(arrs, name, gather=()):
    n, k = len(arrs), len(gather)

    def body(*refs):
        ins, outs, sems = refs[:n + k], refs[n + k:2 * (n + k)], refs[2 * (n + k):]
        start, finish = _chip_exchange_steps(ins[:n], outs[:n], *sems[:2])
        if k:
            g_start, g_relay, g_finish = _small_gather_steps(ins[n:], outs[n:], *sems[2:])
            g_start()
        start()
        if k:
            g_relay()
        finish()
        if k:
            g_finish()

    return pl.pallas_call(
        body, name=name,
        out_shape=[jax.ShapeDtypeStruct((3,) + t.shape[1:], t.dtype) for t in arrs]
        + [jax.ShapeDtypeStruct((N_DEV,) + t.shape, t.dtype) for t in gather],
        in_specs=[ANY] * (n + k), out_specs=[ANY] * (n + k),
        scratch_shapes=[pltpu.SemaphoreType.DMA((n, 3)), pltpu.SemaphoreType.DMA((n, 3))]
        + (_small_gather_scratch(k) if k else []),
    )(*arrs, *gather)


class _Rider:
    def __init__(self, arrays, out_shape, sems, steps, relay_at=()):
        self.arrays, self.out_shape, self.scratch, self.steps = list(arrays), out_shape, sems, steps
        self.n = len(self.arrays)
        self.relay_at = relay_at

    def specs(self, in_specs, out_specs, out_shape, scratch):
        extra = [ANY] * self.n
        return in_specs + extra, out_specs + extra, out_shape + self.out_shape, scratch + self.scratch

    def split(self, refs, n_in, n_out):
        k, s = self.n, len(self.scratch)
        a, b = n_in + k, n_in + k + n_out
        return refs[:n_in] + refs[a:b] + refs[b + k:len(refs) - s], (refs[n_in:a], refs[b:b + k], refs[len(refs) - s:])

    def head(self, mine, step):
        pl.when(step == 0)(self.steps(mine[0], mine[1], *mine[2])[0])

    def tail(self, mine, step, nsteps):
        steps = self.steps(mine[0], mine[1], *mine[2])
        for relay, frac in zip(steps[1:-1], self.relay_at):
            pl.when(step == min(int(frac * nsteps), nsteps - 1))(relay)
        pl.when(step == nsteps - 1)(steps[-1])


def _gather_rider(arrs, relay_at=(0.5, 0.78)):
    n = len(arrs)
    return _Rider(arrs, [jax.ShapeDtypeStruct((N_DEV,) + t.shape, t.dtype) for t in arrs],
                  [pltpu.SemaphoreType.DMA((n, 7)), pltpu.SemaphoreType.DMA((n, 7))], _gather_tree_steps,
                  relay_at)


def _last_exchange_rider(arrs, gather):
    n, k = len(arrs), len(gather)

    def steps(in_refs, out_refs, *sems):
        ex_start, ex_finish = _chip_exchange_steps(in_refs[:n], out_refs[:n], *sems[:2])
        g_start, g_relay, g_finish = _small_gather_steps(in_refs[n:], out_refs[n:], *sems[2:])

        def start():
            g_start()
            ex_start()

        def finish():
            ex_finish()
            g_finish()

        return start, g_relay, finish

    return _Rider(list(arrs) + list(gather),
                  [jax.ShapeDtypeStruct((3,) + t.shape[1:], t.dtype) for t in arrs]
                  + [jax.ShapeDtypeStruct((N_DEV,) + t.shape, t.dtype) for t in gather],
                  [pltpu.SemaphoreType.DMA((n, 3)), pltpu.SemaphoreType.DMA((n, 3))] + _small_gather_scratch(k),
                  steps, relay_at=(0.5,))


def _pair_exchange_rider(arrs):
    n = len(arrs)
    return _Rider(arrs, [jax.ShapeDtypeStruct((4,) + t.shape[2:], t.dtype) for t in arrs],
                  [pltpu.SemaphoreType.DMA((n,)), pltpu.SemaphoreType.DMA((n,))], _pair_exchange_steps)


def _chip_exchange_rider(arrs):
    n = len(arrs)
    return _Rider(arrs, [jax.ShapeDtypeStruct((3,) + t.shape[1:], t.dtype) for t in arrs],
                  [pltpu.SemaphoreType.DMA((n, 3)), pltpu.SemaphoreType.DMA((n, 3))], _chip_exchange_steps)


def _rope_tables(positions):
    inv_freq = ROPE_THETA ** (-jnp.arange(0, HEAD_DIM, 2, dtype=F32) / HEAD_DIM)
    ang = positions.astype(F32)[:, None] * inv_freq
    cos, sin = jnp.cos(ang), jnp.sin(ang)
    return jnp.tile(cos, (1, 4)), jnp.tile(jnp.concatenate([-sin, sin], axis=1), (1, 2))


class _GradReducer:
    def __init__(self):
        self.core = lax.axis_index("c").astype(jnp.int32).reshape(1)
        self.own, self.others, self.waiting, self.riding = {}, {}, [], []

    def pair(self, named):
        mine = self._split(named)
        self._summed(mine, _pair_exchange(list(mine.values()), "reduce_pair_" + next(iter(named))))

    def pair_rider(self, named):
        self.pairing = self._split(named)
        return _pair_exchange_rider(list(self.pairing.values()))

    def pair_landed(self, results):
        self._summed(self.pairing, results)

    @staticmethod
    def _split(named):
        return {k: g.reshape((4, 2) + g.shape[1:]) for k, g in named.items()}

    def _summed(self, mine, theirs):
        for (k, g), r in zip(mine.items(), theirs):
            self.own[k] = _sum_pair(self.core, g, r, "sum_pair_" + k)
        self.waiting += list(mine)

    def rider(self):
        self.riding, self.waiting = self.waiting, []
        return _chip_exchange_rider([self.own[k] for k in self.riding])

    def landed(self, results):
        self.others.update(zip(self.riding, results))

    def last_rider(self, gather):
        self.riding, self.waiting = self.waiting, []
        return _last_exchange_rider([self.own[k] for k in self.riding], gather)

    def last_landed(self, results):
        self.others.update(zip(self.riding, results))
        return results[len(self.riding):]

    def flush(self, name, gather=()):
        keys, self.waiting = self.waiting, []
        res = _chip_exchange([self.own[k] for k in keys], name, gather=gather)
        self.others.update(zip(keys, res))
        return res[len(keys):]


def _by_owner(g):
    if g.ndim == 3:
        return g if g.shape[0] == N_DEV else g.reshape(N_DEV, g.shape[1] * g.shape[0] // N_DEV, g.shape[2])
    return g.reshape(N_DEV, g.shape[0] // N_DEV, g.shape[1])


def _local_step(x, target, positions, mod, small, W, late=None, red=None):
    S, D = x.shape
    gains =jnp.stack([small["g1"], small["g2"], small["g3"]])[:, None, :]
    v1, v2, v3 = jnp.pad(jnp.concatenate([gains, mod.reshape(3, 3, D)], axis=1), ((0, 0), (0, 4), (0, 0)))
    vf = jnp.pad(small["gf"][None], ((0, 7), (0, 0)))
    cos, sin = _rope_tables(positions)
    wbd = jax.scipy.linalg.block_diag(*[small["w_pool"][k] for k in range(4)]).astype(BF16)
    pscale = small["pool_scale"].reshape(1, GW)

    if late is None:
        h1, u1, ab1, act1, f1 = _ffn_fwd(x, v1, W["w1in"], W["w1out"], "ffn1_fwd")
    else:
        (h1, u1, ab1, act1, f1), landed = _ffn_fwd(x, v1, W["w1in"], W["w1out"], "ffn1_fwd", rider=late[0])
        W = {**W, **late[1](landed)}
    u2, p, gates, *qkv = _mix_in_fwd(h1, v2, cos, sin, W["win"], "mix_in_fwd")
    dpool, ypool = _pool_fwd(p, wbd, pscale, "pool_fwd")
    o3, lse3 = [], []
    for gi in range(len(DILATIONS)):
        o, lse = _attn_fwd(qkv[gi], qkv[3 + gi], qkv[6 + gi], f"attn_fwd_{gi}")
        o3.append(o)
        lse3.append(lse)
    h2, ya, merged, tm = _mix_out_fwd(h1, v2, gates, ypool, o3, lse3, W["wpb"], W["wab"], W["wout"], "mix_out_fwd")
    dh3, u3, ab3, act3, f3, loss_blk, dgf = _ffn_fwd(h2, v3, W["w2in"], W["w2out"], "ffn2_fwd", final=(target, vf))

    dh2, dab3, df3, red3 = _ffn_bwd(dh3, h2, f3, ab3, v3, W["w2in"], W["w2out"], "ffn2_bwd")
    half_f = ab3.shape[2] // 2
    G = {"w2in": _by_owner(_wgrad(dab3, u3, "wgrad_2in", tm=half_f))}
    mix_out_args = (dh2, tm, v2, gates, ypool, o3, lse3, W["wpb"], W["wab"], W["wout"], "mix_out_bwd")
    if red is None:
        G["w2out"] = _by_owner(_wgrad(act3, df3, "wgrad_2out", tm=half_f))
        mix_out = _mix_out_bwd(*mix_out_args)
    else:
        g2out, landed = _wgrad(act3, df3, "wgrad_2out", tm=half_f, rider=red.pair_rider({"w2in": G["w2in"]}))
        red.pair_landed(landed)
        G["w2out"] = _by_owner(g2out)
        red.pair({"w2out": G["w2out"]})
        mix_out, landed = _mix_out_bwd(*mix_out_args, rider=red.rider())
        red.landed(landed)
    (dtm, dgl, dypb, dyab, dyp, do0, do1, do2, c0, c1, c2, red2o) = mix_out
    dq3, dk3, dv3 = [], [], []
    for gi, (do, ct) in enumerate(zip((do0, do1, do2), (c0, c1, c2))):
        dq, dk, dv = _attn_bwd(qkv[gi], qkv[3 + gi], qkv[6 + gi], do, lse3[gi], ct, f"attn_bwd_{gi}")
        dq3.append(dq)
        dk3.append(dk)
        dv3.append(dv)
    dp, dwbd, dps = _pool_bwd(dyp, dpool, wbd, pscale, "pool_bwd")
    dh1, dproj, red2i = _mix_in_bwd(dh2, h1, v2, cos, sin, dp, dq3 + dk3 + dv3, dgl, W["win"], "mix_in_bwd")
    G["win"] = _by_owner(_wgrad(dproj, u2, "wgrad_in", tm=1152))
    G["wpb"] = _full_to_cols(_wgrad(ypool, dypb, "wgrad_pb"))
    G["wab"] = _full_to_cols(_wgrad(ya, dyab, "wgrad_ab"))
    if red is None:
        G["wout"] = _by_owner(_wgrad(merged, dtm, "wgrad_out"))
    else:
        gout, landed = _wgrad(merged, dtm, "wgrad_out", rider=red.pair_rider({k: G[k] for k in ("win", "wpb", "wab")}))
        red.pair_landed(landed)
        G["wout"] = _by_owner(gout)
        red.pair({"wout": G["wout"]})
    dx, dab1, df1, red1 = _ffn_bwd(dh1, x, f1, ab1, v1, W["w1in"], W["w1out"], "ffn1_bwd")
    dmod = jnp.concatenate([red1[:3], (red2i + red2o)[:3], red3[:3]])
    dsmall = {
        "g1": red1[3], "g2": red2i[3], "g3": red3[3], "gf": dgf[0],
        "w_pool": jnp.stack([dwbd[k * 64:(k + 1) * 64, k * 64:(k + 1) * 64] for k in range(4)]),
        "pool_scale": dps[0],
    }
    if red is None:
        G["w1in"] = _by_owner(_wgrad(dab1, u1, "wgrad_1in", tm=half_f))
        G["w1out"] = _by_owner(_wgrad(act1, df1, "wgrad_1out", tm=half_f))
        return loss_blk[0, 0], dx, G, dmod, dsmall
    g1in, landed = _wgrad(dab1, u1, "wgrad_1in", tm=half_f, rider=red.rider())
    red.landed(landed)
    red.pair({"w1in": _by_owner(g1in)})
    g1out, landed = _wgrad(act1, df1, "wgrad_1out", tm=half_f, rider=red.rider())
    red.landed(landed)
    red.pair({"w1out": _by_owner(g1out)})
    part = _pack_small(dict(b_ada=dmod, g_norm_ffn1=dsmall["g1"], g_norm_mix=dsmall["g2"], g_norm_ffn2=dsmall["g3"],
                            g_final=dsmall["gf"], w_pool=dsmall["w_pool"], pool_scale=dsmall["pool_scale"]),
                       loss_blk[0, 0])
    return dx, part


SHARDED = ("w_ffn1_in", "w_ffn1_out", "w_in", "w_pool_branch", "w_attn_branch", "w_out", "w_ffn2_in", "w_ffn2_out")
TRANSPOSED = ("w_ffn1_in", "w_in", "w_ffn2_in")
FIRST = ("w_ffn1_in", "w_ffn1_out")
LATER = tuple(n for n in SHARDED if n not in FIRST)
GRAD_KEY = dict(w_ffn1_in="w1in", w_ffn1_out="w1out", w_in="win", w_pool_branch="wpb", w_attn_branch="wab",
                w_out="wout", w_ffn2_in="w2in", w_ffn2_out="w2out")


def _cols_to_full(g):
    return g.transpose(1, 0, 2).reshape(g.shape[1], N_DEV * g.shape[2])


def _full_to_cols(t):
    return t.reshape(t.shape[0], N_DEV, t.shape[1] // N_DEV).transpose(1, 0, 2)


SMALL = (("b_ada", 72), ("g_norm_ffn1", 8), ("g_norm_mix", 8), ("g_norm_ffn2", 8), ("g_final", 8),
         ("w_pool", 128), ("pool_scale", 8))


def _pack_small(vals, loss):
    rows = []
    for name, nrows in SMALL:
        t = vals[name].reshape(-1, 128)
        rows.append(jnp.pad(t, ((0, nrows - t.shape[0]), (0, 0))))
    rows.append(jnp.full((8, 128), loss, F32))
    return jnp.concatenate(rows)


def _unpack_small(slab, shapes):
    out, off = {}, 0
    for name, nrows in SMALL:
        used = 1
        for d in shapes[name]:
            used *= d
        out[name] = slab[off:off + used // 128].reshape(shapes[name])
        off += nrows
    return out, slab[off, 0]


def _as_2d(t):
    return t.reshape(-1, t.shape[-1])


def kernel(x, c, positions, w_ada, b_ada, g_norm_ffn1, w_ffn1_in, w_ffn1_out, g_norm_mix, w_in, w_pool, pool_scale, w_pool_branch, w_attn_branch, w_out, g_norm_ffn2, w_ffn2_in, w_ffn2_out, g_final, loss_target, m_w_ada, m_b_ada, m_g_norm_ffn1, m_w_ffn1_in, m_w_ffn1_out, m_g_norm_mix, m_w_in, m_w_pool, m_pool_scale, m_w_pool_branch, m_w_attn_branch, m_w_out, m_g_norm_ffn2, m_w_ffn2_in, m_w_ffn2_out, m_g_final, v_w_ada, v_b_ada, v_g_norm_ffn1, v_w_ffn1_in, v_w_ffn1_out, v_g_norm_mix, v_w_in, v_w_pool, v_pool_scale, v_w_pool_branch, v_w_attn_branch, v_w_out, v_g_norm_ffn2, v_w_ffn2_in, v_w_ffn2_out, v_g_final):
    names = ["w_ada", "b_ada", "g_norm_ffn1", "w_ffn1_in", "w_ffn1_out", "g_norm_mix", "w_in", "w_pool", "pool_scale",
             "w_pool_branch", "w_attn_branch", "w_out", "g_norm_ffn2", "w_ffn2_in", "w_ffn2_out", "g_final"]
    w = dict(w_ada=w_ada, b_ada=b_ada, g_norm_ffn1=g_norm_ffn1, w_ffn1_in=w_ffn1_in, w_ffn1_out=w_ffn1_out,
             g_norm_mix=g_norm_mix, w_in=w_in, w_pool=w_pool, pool_scale=pool_scale, w_pool_branch=w_pool_branch,
             w_attn_branch=w_attn_branch, w_out=w_out, g_norm_ffn2=g_norm_ffn2, w_ffn2_in=w_ffn2_in,
             w_ffn2_out=w_ffn2_out, g_final=g_final)
    m = dict(w_ada=m_w_ada, b_ada=m_b_ada, g_norm_ffn1=m_g_norm_ffn1, w_ffn1_in=m_w_ffn1_in, w_ffn1_out=m_w_ffn1_out,
             g_norm_mix=m_g_norm_mix, w_in=m_w_in, w_pool=m_w_pool, pool_scale=m_pool_scale,
             w_pool_branch=m_w_pool_branch, w_attn_branch=m_w_attn_branch, w_out=m_w_out, g_norm_ffn2=m_g_norm_ffn2,
             w_ffn2_in=m_w_ffn2_in, w_ffn2_out=m_w_ffn2_out, g_final=m_g_final)
    v = dict(w_ada=v_w_ada, b_ada=v_b_ada, g_norm_ffn1=v_g_norm_ffn1, w_ffn1_in=v_w_ffn1_in, w_ffn1_out=v_w_ffn1_out,
             g_norm_mix=v_g_norm_mix, w_in=v_w_in, w_pool=v_w_pool, pool_scale=v_pool_scale,
             w_pool_branch=v_w_pool_branch, w_attn_branch=v_w_attn_branch, w_out=v_w_out, g_norm_ffn2=v_g_norm_ffn2,
             w_ffn2_in=v_w_ffn2_in, w_ffn2_out=v_w_ffn2_out, g_final=v_g_final)
    shapes = {n: w[n].shape for n in names}
    me = 4 * lax.axis_index("x") + 2 * lax.axis_index("y") + lax.axis_index("c")
    D = x.shape[-1]
    n_mod = w_ada.shape[-1] * N_DEV // D

    def local(t, name):
        return t[name][0].T if name in TRANSPOSED else t[name][0]

    shards = {name: local(w, name).astype(BF16) for name in SHARDED}

    def gather_done(names, fulls):
        return {name: lax.dynamic_update_index_in_dim(full, shards[name], me, axis=0)
                for name, full in zip(names, fulls)}

    def ffn_weights(g, pre):
        return {"w%sin" % pre: g["w_ffn%s_in" % pre].reshape(2, -1, D),
                "w%sout" % pre: g["w_ffn%s_out" % pre].reshape(-1, D)}

    def later_weights(fulls):
        g = gather_done(LATER, fulls)
        return dict(win=g["w_in"].reshape(-1, D), wpb=_cols_to_full(g["w_pool_branch"]),
                    wab=_cols_to_full(g["w_attn_branch"]), wout=g["w_out"].reshape(D, D), **ffn_weights(g, "2"))

    *first, c_all = _all_gather_tree([shards[n] for n in FIRST], "gather_ffn1", gather=[c.reshape(D // 128, 128)])
    W = ffn_weights(gather_done(FIRST, first), "1")

    ada_cols = w_ada.shape[-1]
    b_mine = lax.dynamic_slice_in_dim(b_ada, me * ada_cols, ada_cols, axis=1)
    cond, mod_part = _ada_mod(c_all.reshape(N_DEV, D), w_ada[0], b_mine, "ada_mod")
    (mod_all,) = _all_gather([mod_part.reshape(-1, 128)], "gather_mod")
    mod_all = mod_all.reshape(N_DEV, N_DEV, ada_cols)
    mod = lax.dynamic_index_in_dim(mod_all, me, axis=1, keepdims=False).reshape(n_mod, D)

    small = dict(g1=g_norm_ffn1[0], g2=g_norm_mix[0], g3=g_norm_ffn2[0], gf=g_final, w_pool=w_pool[0],
                 pool_scale=pool_scale[0])
    red = _GradReducer()
    dx, part = _local_step(
        x[0], loss_target[0], positions[0], mod, small, W,
        late=(_gather_rider([shards[n] for n in LATER]), later_weights), red=red)
    chip = (2 * lax.axis_index("x") + lax.axis_index("y")).astype(jnp.int32).reshape(1)
    grads, delta, new_m, new_v = {}, {}, {}, {}

    def adamw_args(name):
        key = GRAD_KEY[name]
        return red.own[key], red.others[key], local(w, name), local(m, name), local(v, name)

    def adamw_done(name, res):
        grads[name], delta[name], new_m[name], new_v[name] = (
            (t.T if name in TRANSPOSED else t)[None] for t in res)

    hosts = [name for name in LATER if w[name].shape[-1] == D or name in TRANSPOSED]
    done, landed = _sum_adamw_many(chip, [adamw_args(name) for name in hosts], "adamw_later",
                                   rider=red.last_rider([part]))
    (parts,) = red.last_landed(landed)
    for name, res in zip(hosts, done):
        adamw_done(name, res)
    for name in SHARDED:
        if name not in hosts:
            adamw_done(name, _sum_adamw(chip, *adamw_args(name), "adamw_" + name))

    gsmall, loss = _unpack_small(_sum_slots(parts, "sum_small"), shapes)
    rows_mine = ada_cols // 128
    dmod_mine = lax.dynamic_slice_in_dim(parts, me * rows_mine, rows_mine, axis=1).reshape(N_DEV, ada_cols)
    grads["w_ada"], delta["w_ada"], new_m["w_ada"], new_v["w_ada"] = (
        t[None] for t in _ada_grad_adamw(cond.T, dmod_mine, w_ada[0], m_w_ada[0], v_w_ada[0], "ada_grad_adamw"))
    small_names = [name for name, _ in SMALL]
    res = _adamw_many(*([_as_2d(t[name]) for name in small_names] for t in (w, gsmall, m, v)), "adamw_small")
    for dst, vals in zip((delta, new_m, new_v), res):
        dst.update({name: t.reshape(shapes[name]) for name, t in zip(small_names, vals)})
    grads.update(gsmall)

    return (loss, dx[None], *[grads[n] for n in names], *[delta[n] for n in names],
            *[new_m[n] for n in names], *[new_v[n] for n in names])
```

```python
import functools

import jax
import jax.numpy as jnp
from jax import lax
from jax.experimental import pallas as pl
from jax.experimental.pallas import tpu as pltpu

F32 = jnp.float32
BF16 = jnp.bfloat16
MESH = pl.DeviceIdType.MESH
ANY = pl.BlockSpec(memory_space=pl.ANY)

N_DEV = 8
EPS = 1e-6
HEAD_DIM = 64
HEADS = 4
GW = HEADS * HEAD_DIM
DILATIONS = (1, 4, 16)
BAND = 128
QB = 128
POOL_WINDOWS = (2, 4, 8, 16)
HALO = 16
ROPE_THETA = 10000.0

ADAM_LR = 0.001
ADAM_B1 = 0.9
ADAM_B2 = 0.999
ADAM_EPS = 1e-08
ADAM_WD = 0.01
ADAM_STEP = 10

VMEM_LIMIT = 56 * 1024 * 1024
TS = 512
FFN_TS = 256
FFN_CHUNKS = (2816,)

NT = (((1,), (1,)), ((), ()))
TN = (((0,), (0,)), ((), ()))


def _params(**kw):
    return pltpu.CompilerParams(vmem_limit_bytes=VMEM_LIMIT, **kw)


def _dot(a, b):
    return jnp.dot(a, b, preferred_element_type=F32)


def _dot_nt(a, b):
    return lax.dot_general(a, b, NT, preferred_element_type=F32)


def _dot_tn(a, b):
    return lax.dot_general(a, b, TN, preferred_element_type=F32)


def _load_weights(pairs, sem):
    @pl.when(pl.program_id(0) == 0)
    def _():
        copies = [pltpu.make_async_copy(src, dst, sem.at[i]) for i, (src, dst) in enumerate(pairs)]
        for cp in copies:
            cp.start()
        for cp in copies:
            cp.wait()


def _norm_mod(x, g, sc, sh):
    r = lax.rsqrt(jnp.mean(x * x, axis=-1, keepdims=True) + EPS)
    xn = x * r
    y = xn * g
    return r, xn, y, y * (1.0 + sc) + sh


def _norm_mod_bwd(du, r, xn, y, g, sc):
    dsh = jnp.sum(du, axis=0, keepdims=True)
    dsc = jnp.sum(du * y, axis=0, keepdims=True)
    dy = du * (1.0 + sc)
    dg = jnp.sum(dy * xn, axis=0, keepdims=True)
    dxn = dy * g
    dx = r * (dxn - xn * jnp.mean(dxn * xn, axis=-1, keepdims=True))
    return dx, dsh, dsc, dg


def _row_tile(ts, width):
    return pl.BlockSpec((ts, width), lambda i: (i, 0))


def _const(shape):
    return pl.BlockSpec(shape, lambda *_: (0,) * len(shape))


def _ffn_chunks(Fd):
    assert sum(FFN_CHUNKS) == Fd
    edges = [sum(FFN_CHUNKS[:k]) for k in range(len(FFN_CHUNKS) + 1)]
    return [slice(a, b) for a, b in zip(edges[:-1], edges[1:])]
def _final_tile(x, g, target):
    r = lax.rsqrt(jnp.mean(x * x, axis=-1, keepdims=True) + EPS)
    xn = x * r
    err = xn * g - target
    loss = 0.5 * jnp.sum(jnp.mean(err * err, axis=-1, keepdims=True))
    dy = err * (1.0 / x.shape[-1])
    dg = jnp.sum(dy * xn, axis=0, keepdims=True)
    dxn = dy * g
    return r * (dxn - xn * jnp.mean(dxn * xn, axis=-1, keepdims=True)), loss, dg


def _ffn_fwd(h, vec, win, wout, name, rider=None, final=None):
    TS = FFN_TS
    S, D = h.shape
    _, Fd, _ = win.shape
    n_in, n_out = (6, 7) if final else (4, 5)

    def body(*refs):
        if rider is None:
            return compute(*refs)
        host, mine = rider.split(refs, n_in, n_out)
        rider.head(mine, pl.program_id(0))
        compute(*host)
        rider.tail(mine, pl.program_id(0), S // TS)

    def compute(*refs):
        h_ref, vec_ref, win_hbm, wout_hbm = refs[:4]
        hn_ref, u_ref, ab_ref, act_ref, f_ref = refs[n_in:n_in + 5]
        win_v, wout_v, sem = refs[n_in + n_out:]
        _load_weights([(win_hbm, win_v), (wout_hbm, wout_v)], sem)
        x = h_ref[...]
        g, sh, sc, gt = (vec_ref[k:k + 1, :] for k in range(4))
        _, _, _, u = _norm_mod(x, g, sc, sh)
        ub = u.astype(BF16)
        u_ref[...] = ub
        acc = jnp.zeros((TS, D), F32)
        for sl in _ffn_chunks(Fd):
            a = _dot_nt(ub, win_v[0, sl, :])
            b = _dot_nt(ub, win_v[1, sl, :])
            act = ((a * jax.nn.sigmoid(a)) * b).astype(BF16)
            ab_ref[0, :, sl] = a.astype(BF16)
            ab_ref[1, :, sl] = b.astype(BF16)
            act_ref[:, sl] = act
            acc = acc + _dot(act, wout_v[sl, :])
        f_ref[...] = acc.astype(BF16)
        hn = x + (0.5 * gt) * acc
        if not final:
            hn_ref[...] = hn
            return
        t_ref, gf_ref = refs[4:6]
        loss_ref, dgf_ref = refs[n_in + 5:n_in + 7]

        @pl.when(pl.program_id(0) == 0)
        def _():
            loss_ref[...] = jnp.zeros_like(loss_ref)
            dgf_ref[...] = jnp.zeros_like(dgf_ref)

        hn_ref[...], loss, dg = _final_tile(hn, gf_ref[0:1, :], t_ref[...])
        loss_ref[...] += loss
        dgf_ref[0:1, :] += dg

    specs = (
        [_row_tile(TS, D), _const((8, D)), ANY, ANY] + ([_row_tile(TS, D), _const((8, D))] if final else []),
        [_row_tile(TS, D), _row_tile(TS, D), pl.BlockSpec((2, TS, Fd), lambda i: (0, i, 0)),
         _row_tile(TS, Fd), _row_tile(TS, D)] + ([_const((8, 128)), _const((8, D))] if final else []),
        [jax.ShapeDtypeStruct((S, D), F32), jax.ShapeDtypeStruct((S, D), BF16),
         jax.ShapeDtypeStruct((2, S, Fd), BF16), jax.ShapeDtypeStruct((S, Fd), BF16),
         jax.ShapeDtypeStruct((S, D), BF16)]
        + ([jax.ShapeDtypeStruct((8, 128), F32), jax.ShapeDtypeStruct((8, D), F32)] if final else []),
        [pltpu.VMEM(win.shape, BF16), pltpu.VMEM(wout.shape, BF16), pltpu.SemaphoreType.DMA((2,))])
    in_specs, out_specs, out_shape, scratch = specs if rider is None else rider.specs(*specs)
    outs = pl.pallas_call(
        body, name=name, grid=(S // TS,), in_specs=in_specs, out_specs=out_specs, out_shape=out_shape,
        scratch_shapes=scratch, compiler_params=_params(),
    )(h, vec, win, wout, *(final or ()), *(rider.arrays if rider else []))
    return outs if rider is None else (outs[:n_out], outs[n_out:])


def _ffn_bwd(dh, h, f, ab, vec, win, wout, name):
    TS = FFN_TS
    S, D = h.shape
    _, Fd, _ = win.shape

    def body(dh_ref, h_ref, f_ref, ab_ref, vec_ref, win_hbm, wout_hbm,
             dhp_ref, dab_ref, df_ref, red_ref, win_v, wout_v, sem):
        _load_weights([(win_hbm, win_v), (wout_hbm, wout_v)], sem)

        @pl.when(pl.program_id(0) == 0)
        def _():
            red_ref[...] = jnp.zeros_like(red_ref)

        dh_v = dh_ref[...]
        x = h_ref[...]
        g, sh, sc, gt = (vec_ref[k:k + 1, :] for k in range(4))
        dgt = jnp.sum((0.5 * f_ref[...].astype(F32)) * dh_v, axis=0, keepdims=True)
        dfb = ((0.5 * gt) * dh_v).astype(BF16)
        df_ref[...] = dfb
        du = jnp.zeros((TS, D), F32)
        for sl in _ffn_chunks(Fd):
            dact = _dot_nt(dfb, wout_v[sl, :])
            av = ab_ref[0, :, sl].astype(F32)
            bv = ab_ref[1, :, sl].astype(F32)
            sg = jax.nn.sigmoid(av)
            da = (dact * bv * (sg * (1.0 + av * (1.0 - sg)))).astype(BF16)
            db = (dact * (av * sg)).astype(BF16)
            dab_ref[0, :, sl] = da
            dab_ref[1, :, sl] = db
            du = du + _dot(da, win_v[0, sl, :]) + _dot(db, win_v[1, sl, :])
        r, xn, y, _ = _norm_mod(x, g, sc, sh)
        dx, dsh, dsc, dg = _norm_mod_bwd(du, r, xn, y, g, sc)
        dhp_ref[...] = dh_v + dx
        red_ref[0:1, :] += dsh
        red_ref[1:2, :] += dsc
        red_ref[2:3, :] += dgt
        red_ref[3:4, :] += dg

    ab_spec = pl.BlockSpec((2, TS, Fd), lambda i: (0, i, 0))
    return pl.pallas_call(
        body, name=name, grid=(S // TS,),
        in_specs=[_row_tile(TS, D), _row_tile(TS, D), _row_tile(TS, D), ab_spec, _const((8, D)), ANY, ANY],
        out_specs=[_row_tile(TS, D), ab_spec, _row_tile(TS, D), _const((8, D))],
        out_shape=[jax.ShapeDtypeStruct((S, D), F32), jax.ShapeDtypeStruct((2, S, Fd), BF16),
                   jax.ShapeDtypeStruct((S, D), BF16), jax.ShapeDtypeStruct((8, D), F32)],
        scratch_shapes=[pltpu.VMEM(win.shape, BF16), pltpu.VMEM(wout.shape, BF16), pltpu.SemaphoreType.DMA((2,))],
        compiler_params=_params(),
    )(dh, h, f, ab, vec, win, wout)


def _wgrad(x, y, name, tm=None, ts=2048, rider=None):
    xb = x.ndim == 3
    nb = x.shape[0] if xb else 0
    S, M = x.shape[-2:]
    N = y.shape[-1]
    tm = tm or M
    ts = min(ts, S)
    nk = S // ts
    grid = (max(nb, 1), M // tm, nk)

    def body(*refs):
        if rider is None:
            return compute(*refs)
        host, mine = rider.split(refs, 2, 1)
        step = (pl.program_id(0) * grid[1] + pl.program_id(1)) * grid[2] + pl.program_id(2)
        rider.head(mine, step)
        compute(*host)
        rider.tail(mine, step, grid[0] * grid[1] * grid[2])

    def compute(x_ref, y_ref, o_ref, acc):
        k = pl.program_id(2)

        @pl.when(k == 0)
        def _():
            acc[...] = jnp.zeros_like(acc)

        acc[...] += _dot_tn(x_ref[...], y_ref[...])

        @pl.when(k == nk - 1)
        def _():
            o_ref[...] = acc[...].astype(BF16)

    x_spec = (pl.BlockSpec((None, ts, tm), lambda b, i, k: (b, k, i)) if xb
              else pl.BlockSpec((ts, tm), lambda b, i, k: (k, i)))
    y_spec = pl.BlockSpec((ts, N), lambda b, i, k: (k, 0))
    if xb:
        o_spec, o_shape = pl.BlockSpec((None, tm, N), lambda b, i, k: (b, i, 0)), (nb, M, N)
    else:
        o_spec, o_shape = pl.BlockSpec((tm, N), lambda b, i, k: (i, 0)), (M, N)
    specs = ([x_spec, y_spec], [o_spec], [jax.ShapeDtypeStruct(o_shape, BF16)], [pltpu.VMEM((tm, N), F32)])
    in_specs, out_specs, out_shape, scratch = specs if rider is None else rider.specs(*specs)
    outs = pl.pallas_call(
        body, name=name, grid=grid, in_specs=in_specs, out_specs=out_specs, out_shape=out_shape,
        scratch_shapes=scratch, compiler_params=_params(),
    )(x, y, *(rider.arrays if rider else []))
    return outs[0] if rider is None else (outs[0], outs[1:])


P_OFF, Q_OFF, K_OFF, V_OFF, G_OFF = 0, 256, 1024, 1792, 2560
IN_WIDTH = 4608


def _first_half_mask(ts):
    lane = lax.broadcasted_iota(jnp.int32, (ts, 128), 1)
    return (lane % HEAD_DIM) < (HEAD_DIM // 2)


def _rope(t, cos, sin_signed, first, sign):
    partner = jnp.where(first, pltpu.roll(t, 96, 1), pltpu.roll(t, 32, 1))
    return t * cos + sign * (partner * sin_signed)


def _res_spec(r):
    return pl.BlockSpec((r, TS // r, GW), lambda i: (0, i, 0))


def _res_shape(S, r, dtype):
    return jax.ShapeDtypeStruct((r, S // r, GW), dtype)


def _to_residues(piece, out_ref, lanes, r, scr):
    if r == 1:
        out_ref[0, :, lanes] = piece.astype(out_ref.dtype)
        return
    for h in range(piece.shape[1] // 128):
        scr[h] = piece[:, h * 128:(h + 1) * 128]
        at = slice(lanes.start + h * 128, lanes.start + (h + 1) * 128)
        for res in range(r):
            out_ref[res, :, at] = scr[h, pl.ds(res, TS // r, stride=r), :].astype(out_ref.dtype)


def _from_residues(in_ref, lanes, r, scr):
    if r == 1:
        return in_ref[0, :, lanes].astype(F32)
    halves = (lanes.stop - lanes.start) // 128
    for h in range(halves):
        at = slice(lanes.start + h * 128, lanes.start + (h + 1) * 128)
        for res in range(r):
            scr[h, pl.ds(res, TS // r, stride=r), :] = in_ref[res, :, at].astype(F32)
    return scr[0] if halves == 1 else jnp.concatenate([scr[0], scr[1]], axis=1)


RES_SCRATCH = (2, TS, 128)


def _mix_in_fwd(h, vec, cos, sin, win, name):
    S, D = h.shape

    def body(h_ref, vec_ref, cos_ref, sin_ref, win_hbm, u_ref, p_ref, gates_ref, *rest):
        qkv_refs, (win_v, sem, scr) = rest[:9], rest[9:]
        _load_weights([(win_hbm, win_v)], sem)
        g, sh, sc = (vec_ref[k:k + 1, :] for k in range(3))
        _, _, _, u = _norm_mod(h_ref[...], g, sc, sh)
        ub = u.astype(BF16)
        u_ref[...] = ub
        p_ref[...] = _dot_nt(ub, win_v[P_OFF:Q_OFF, :])
        cosv, sinv = cos_ref[...], sin_ref[...]
        first = _first_half_mask(TS)
        for which, off in enumerate((Q_OFF, K_OFF, V_OFF)):
            t = _dot_nt(ub, win_v[off:off + 3 * GW, :])
            for gi in range(3):
                for half in range(2):
                    c0 = gi * GW + half * 128
                    piece = t[:, c0:c0 + 128]
                    if which < 2:
                        piece = _rope(piece, cosv, sinv, first, 1.0)
                    _to_residues(piece, qkv_refs[which * 3 + gi], slice(half * 128, (half + 1) * 128),
                                 DILATIONS[gi], scr)
        gates_ref[...] = jax.nn.sigmoid(_dot_nt(ub, win_v[G_OFF:IN_WIDTH, :])).astype(BF16)

    return pl.pallas_call(
        body, name=name, grid=(S // TS,),
        in_specs=[_row_tile(TS, D), _const((8, D)), _row_tile(TS, 128), _row_tile(TS, 128), ANY],
        out_specs=[_row_tile(TS, D), _row_tile(TS, GW), _row_tile(TS, 2 * D)] + [_res_spec(r) for r in DILATIONS] * 3,
        out_shape=[jax.ShapeDtypeStruct((S, D), BF16), jax.ShapeDtypeStruct((S, GW), F32),
                   jax.ShapeDtypeStruct((S, 2 * D), BF16)] + [_res_shape(S, r, BF16) for r in DILATIONS] * 3,
        scratch_shapes=[pltpu.VMEM((IN_WIDTH, D), BF16), pltpu.SemaphoreType.DMA((1,)), pltpu.VMEM(RES_SCRATCH, F32)],
        compiler_params=_params(),
    )(h, vec, cos, sin, win)


def _mix_in_bwd(dh, h, vec, cos, sin, dp, dqkv, dgl, win, name):
    S, D = h.shape

    def body(dh_ref, h_ref, vec_ref, cos_ref, sin_ref, dp_ref, *rest):
        dqkv_refs = rest[:9]
        dgl_ref, win_hbm, dhp_ref, dproj_ref, red_ref, win_v, sem, scr = rest[9:]
        _load_weights([(win_hbm, win_v)], sem)

        @pl.when(pl.program_id(0) == 0)
        def _():
            red_ref[...] = jnp.zeros_like(red_ref)

        cosv, sinv = cos_ref[...], sin_ref[...]
        first = _first_half_mask(TS)
        dproj_ref[:, P_OFF:Q_OFF] = dp_ref[...].astype(BF16)
        for which, off in enumerate((Q_OFF, K_OFF, V_OFF)):
            for gi in range(3):
                for half in range(2):
                    piece = _from_residues(dqkv_refs[which * 3 + gi], slice(half * 128, (half + 1) * 128),
                                           DILATIONS[gi], scr)
                    if which < 2:
                        piece = _rope(piece, cosv, sinv, first, -1.0)
                    c0 = off + gi * GW + half * 128
                    dproj_ref[:, c0:c0 + 128] = piece.astype(BF16)
        dproj_ref[:, G_OFF:IN_WIDTH] = dgl_ref[...]
        du = _dot(dproj_ref[...], win_v[...])
        g, sh, sc = (vec_ref[k:k + 1, :] for k in range(3))
        r, xn, y, _ = _norm_mod(h_ref[...], g, sc, sh)
        dx, dsh, dsc, dg = _norm_mod_bwd(du, r, xn, y, g, sc)
        dhp_ref[...] = dh_ref[...] + dx
        red_ref[0:1, :] += dsh
        red_ref[1:2, :] += dsc
        red_ref[3:4, :] += dg

    return pl.pallas_call(
        body, name=name, grid=(S // TS,),
        in_specs=[_row_tile(TS, D), _row_tile(TS, D), _const((8, D)), _row_tile(TS, 128), _row_tile(TS, 128),
                  _row_tile(TS, GW)] + [_res_spec(r) for r in DILATIONS] * 3 + [_row_tile(TS, 2 * D), ANY],
        out_specs=[_row_tile(TS, D), _row_tile(TS, IN_WIDTH), _const((8, D))],
        out_shape=[jax.ShapeDtypeStruct((S, D), F32), jax.ShapeDtypeStruct((S, IN_WIDTH), BF16),
                   jax.ShapeDtypeStruct((8, D), F32)],
        scratch_shapes=[pltpu.VMEM((IN_WIDTH, D), BF16), pltpu.SemaphoreType.DMA((1,)), pltpu.VMEM(RES_SCRATCH, F32)],
        compiler_params=_params(),
    )(dh, h, vec, cos, sin, dp, *dqkv, dgl, win)


def _pool_lanes(rows):
    lane = lax.broadcasted_iota(jnp.int32, (rows, GW), 1)
    return lane // HEAD_DIM


def _pool_window(rows):
    grp = _pool_lanes(rows)
    w = jnp.full((rows, GW), POOL_WINDOWS[0], jnp.int32)
    for k in range(1, len(POOL_WINDOWS)):
        w = jnp.where(grp == k, POOL_WINDOWS[k], w)
    return grp, w


def _pool_fwd(p, wbd, scale, name, ts=1024):
    S = p.shape[0]
    ext = ts + HALO

    def body(pc_ref, ph_ref, wbd_ref, sc_ref, d_ref, y_ref):
        i = pl.program_id(0)
        cur = pc_ref[...]
        halo = jnp.where(i > 0, ph_ref[...], 0.0)
        s = jnp.concatenate([halo, cur], axis=0)
        grp, w = _pool_window(ext)
        sel = jnp.zeros((ext, GW), F32)
        for k, wk in enumerate(POOL_WINDOWS):
            s = s + pltpu.roll(s, wk // 2, 0)
            sel = jnp.where(grp == k, s, sel)
        t = i * ts + lax.broadcasted_iota(jnp.int32, (ts, GW), 0)
        count = jnp.minimum(t + 1, w[HALO:]).astype(F32)
        d = (sel[HALO:] / count - cur).astype(BF16)
        d_ref[...] = d
        y_ref[...] = (_dot(d, wbd_ref[...]) * sc_ref[...]).astype(BF16)

    return pl.pallas_call(
        body, name=name, grid=(S // ts,),
        in_specs=[_row_tile(ts, GW),
                  pl.BlockSpec((HALO, GW), lambda i: (jnp.maximum(i * (ts // HALO) - 1, 0), 0)),
                  _const((GW, GW)), _const((1, GW))],
        out_specs=[_row_tile(ts, GW), _row_tile(ts, GW)],
        out_shape=[jax.ShapeDtypeStruct((S, GW), BF16), jax.ShapeDtypeStruct((S, GW), BF16)],
        compiler_params=_params(),
    )(p, p, wbd, scale)


def _pool_bwd(dy, d, wbd, scale, name, ts=1024):
    S = dy.shape[0]
    ext = ts + HALO
    nsteps = S // ts
    last_halo = S // HALO - 1

    def body(dyc_ref, dyh_ref, d_ref, wbd_ref, sc_ref, dp_ref, dw_ref, ds_ref):
        i = pl.program_id(0)

        @pl.when(i == 0)
        def _():
            dw_ref[...] = jnp.zeros_like(dw_ref)
            ds_ref[...] = jnp.zeros_like(ds_ref)

        dyc = dyc_ref[...]
        dyh = jnp.where(i < nsteps - 1, dyh_ref[...], 0.0)
        dys = (jnp.concatenate([dyc, dyh], axis=0) * sc_ref[...]).astype(BF16)
        dd = _dot_nt(dys, wbd_ref[...])
        grp, w = _pool_window(ext)
        t = i * ts + lax.broadcasted_iota(jnp.int32, (ext, GW), 0)
        s = dd / jnp.minimum(t + 1, w).astype(F32)
        sel = jnp.zeros((ext, GW), F32)
        for k, wk in enumerate(POOL_WINDOWS):
            s = s + pltpu.roll(s, ext - wk // 2, 0)
            sel = jnp.where(grp == k, s, sel)
        dp_ref[...] = sel[:ts] - dd[:ts]
        dv = d_ref[...]
        z = _dot(dv, wbd_ref[...])
        ds_ref[0:1, :] += jnp.sum(dyc * z, axis=0, keepdims=True)
        dw_ref[...] += _dot_tn(dv, dys[:ts])

    return pl.pallas_call(
        body, name=name, grid=(nsteps,),
        in_specs=[_row_tile(ts, GW),
                  pl.BlockSpec((HALO, GW), lambda i: (jnp.minimum((i + 1) * (ts // HALO), last_halo), 0)),
                  _row_tile(ts, GW), _const((GW, GW)), _const((1, GW))],
        out_specs=[_row_tile(ts, GW), _const((GW, GW)), _const((8, GW))],
        out_shape=[jax.ShapeDtypeStruct((S, GW), F32), jax.ShapeDtypeStruct((GW, GW), F32),
                   jax.ShapeDtypeStruct((8, GW), F32)],
        compiler_params=_params(),
    )(dy, dy, d, wbd, scale)


def _head_id(rows):
    return lax.broadcasted_iota(jnp.int32, (rows, GW), 1) // HEAD_DIM


def _stack_heads(t, hid):
    return jnp.concatenate([jnp.where(hid == h, t, jnp.zeros_like(t)) for h in range(HEADS)], axis=0)


def _unstack_heads(t_all, hid):
    out = jnp.zeros((QB, GW), F32)
    for h in range(HEADS):
        out = jnp.where(hid == h, t_all[h * QB:(h + 1) * QB], out)
    return out


def _band_masks():
    row = lax.broadcasted_iota(jnp.int32, (HEADS * QB, 2 * QB), 0) % QB
    col = lax.broadcasted_iota(jnp.int32, (HEADS * QB, 2 * QB), 1)
    rel = row + QB - col
    band = (rel >= 0) & (rel <= BAND)
    return band, band & (col >= QB)


def _stream_mask(masks, n, off):
    band, first = masks
    return band if off > 0 else first | (band & (n > 0))


FWD_STREAMS, BWD_STREAMS = 16, 8


def _streams(r, nb, most):
    if r > 1:
        ns = min(r, most)
        return nb, [(lambda rb, l=l: ns * rb + l, 0) for l in range(ns)]
    ns = min(most, nb)
    return nb // ns, [(lambda rb: 0, l * (nb // ns)) for l in range(ns)]


def _attn_fwd(q, k, v, name):
    r, L, _ = q.shape
    nbs, streams = _streams(r, L // QB, FWD_STREAMS)
    ns = len(streams)
    grid = (max(r // ns, 1), nbs)

    def cur(res, off):
        return pl.BlockSpec((None, QB, GW), lambda rb, n: (res(rb), n + off, 0))

    def prev(res, off):
        return pl.BlockSpec((None, QB, GW), lambda rb, n: (res(rb), jnp.maximum(n + off - 1, 0), 0))

    def body(*refs):
        n = pl.program_id(1)
        hid = _head_id(QB)
        masks = _band_masks()
        o_ref, lse_ref = refs[5 * len(streams):]
        for l, (_, off) in enumerate(streams):
            q_ref, kp_ref, kc_ref, vp_ref, vc_ref = refs[5 * l:5 * l + 5]
            qs = _stack_heads(q_ref[...], hid)
            kc = jnp.concatenate([kp_ref[...], kc_ref[...]], axis=0)
            vc = jnp.concatenate([vp_ref[...], vc_ref[...]], axis=0)
            s = _dot_nt(qs, kc) * (HEAD_DIM ** -0.5)
            s = jnp.where(_stream_mask(masks, n, off), s, -jnp.inf)
            m = jnp.max(s, axis=-1, keepdims=True)
            e = jnp.exp(s - m)
            den = jnp.sum(e, axis=-1, keepdims=True)
            lse = m + jnp.log(den)
            pr = (e * (1.0 / den)).astype(BF16)
            o_ref[l] = _unstack_heads(_dot(pr, vc), hid).astype(BF16)
            lse_ref[l] = _unstack_heads(jnp.broadcast_to(lse, (HEADS * QB, GW)), hid)

    in_specs, args = [], []
    for res, off in streams:
        in_specs += [cur(res, off), prev(res, off), cur(res, off), prev(res, off), cur(res, off)]
        args += [q, k, k, v, v]
    out = jax.ShapeDtypeStruct((ns * grid[0], nbs * QB, GW), F32)
    both = pl.BlockSpec((ns, QB, GW), lambda rb, n: (rb, n, 0))
    o, lse = pl.pallas_call(
        body, name=name, grid=grid, in_specs=in_specs, out_specs=[both, both],
        out_shape=[jax.ShapeDtypeStruct(out.shape, BF16), out],
        compiler_params=_params(),
    )(*args)
    return o.reshape(q.shape), lse.reshape(q.shape)


def _head_rows(t_full, hid):
    del hid
    return jnp.concatenate([t_full[:, h * HEAD_DIM:h * HEAD_DIM + 1] for h in range(HEADS)], axis=0)


def _attn_bwd(q, k, v, do, lse, cterm, name):
    r, L, _ = q.shape
    nbs, streams = _streams(r, L // QB, BWD_STREAMS)
    ns = len(streams)
    parts = r == 1

    def spec(res, index):
        return pl.BlockSpec((None, QB, GW), lambda rb, n: (res(rb), index(n), 0))

    def body(*refs):
        dq_ref, dk_ref, dv_ref, carry_k, carry_v, seam_k, seam_v = refs[8 * ns:]
        n = pl.program_id(1)

        @pl.when(n == 0)
        def _():
            carry_k[...] = jnp.zeros_like(carry_k)
            carry_v[...] = jnp.zeros_like(carry_v)

        @pl.when(n < nbs)
        def _():
            hid = _head_id(QB)
            masks = _band_masks()
            for l, (_, off) in enumerate(streams):
                q_ref, do_ref, lse_ref, c_ref, kp_ref, kc_ref, vp_ref, vc_ref = refs[8 * l:8 * l + 8]
                qs = _stack_heads(q_ref[...], hid)
                dos = _stack_heads(do_ref[...], hid)
                kc = jnp.concatenate([kp_ref[...], kc_ref[...]], axis=0)
                vc = jnp.concatenate([vp_ref[...], vc_ref[...]], axis=0)
                s = _dot_nt(qs, kc) * (HEAD_DIM ** -0.5)
                s = jnp.where(_stream_mask(masks, n, off), s, -jnp.inf)
                p = jnp.exp(s - _head_rows(lse_ref[...], hid))
                dp = _dot_nt(dos, vc)
                ds = (p * (dp + _head_rows(c_ref[...], hid)) * (HEAD_DIM ** -0.5)).astype(BF16)
                dq_ref[l] = _unstack_heads(_dot(ds, kc), hid).astype(BF16)
                dkc = _dot_tn(ds, qs)
                dvc = _dot_tn(p.astype(BF16), dos)
                if parts and l > 0:
                    @pl.when(n == 0)
                    def _():
                        seam_k[l] = dkc[:QB]
                        seam_v[l] = dvc[:QB]
                dk_ref[l] = (carry_k[l] + dkc[:QB]).astype(BF16)
                dv_ref[l] = (carry_v[l] + dvc[:QB]).astype(BF16)
                carry_k[l] = dkc[QB:]
                carry_v[l] = dvc[QB:]

        @pl.when(n == nbs)
        def _():
            for l in range(ns):
                if parts and l + 1 < ns:
                    dk_ref[l] = (carry_k[l] + seam_k[l + 1]).astype(BF16)
                    dv_ref[l] = (carry_v[l] + seam_v[l + 1]).astype(BF16)
                else:
                    dk_ref[l] = carry_k[l].astype(BF16)
                    dv_ref[l] = carry_v[l].astype(BF16)

    in_specs, args = [], []
    for res, off in streams:
        qside = functools.partial(lambda n, off: jnp.minimum(n, nbs - 1) + off, off=off)
        kprev = functools.partial(lambda n, off: jnp.maximum(jnp.minimum(n, nbs) - 1 + off, 0), off=off)
        in_specs += [spec(res, qside)] * 4 + [spec(res, kprev), spec(res, qside)] * 2
        args += [q, do, lse, cterm, k, k, v, v]
    out = jax.ShapeDtypeStruct((ns * max(r // ns, 1), nbs * QB, GW), BF16)
    qout = pl.BlockSpec((ns, QB, GW), lambda rb, n: (rb, jnp.minimum(n, nbs - 1), 0))
    kout = pl.BlockSpec((ns, QB, GW), lambda rb, n: (rb, jnp.maximum(n - 1, 0), 0))
    buf = pltpu.VMEM((ns, QB, GW), F32)
    outs = pl.pallas_call(
        body, name=name, grid=(max(r // ns, 1), nbs + 1),
        in_specs=in_specs, out_specs=[qout, kout, kout], out_shape=[out, out, out],
        scratch_shapes=[buf, buf, buf, buf],
        compiler_params=_params(),
    )(*args)
    return [t.reshape(q.shape) for t in outs]


def _token_order(refs, scr):
    return [_from_residues(ref, slice(0, GW), r, scr) for ref, r in zip(refs, DILATIONS)]


def _group_weights(lses):
    l0, l1, l2 = lses
    m = jnp.maximum(jnp.maximum(l0, l1), l2)
    e = [jnp.exp(l - m) for l in (l0, l1, l2)]
    den = e[0] + e[1] + e[2]
    return [ei / den for ei in e]


def _mix_out_fwd(h, vec, gates, ypool, o3, lse3, wpb, wab, wout, name):
    S, D = h.shape

    def body(h_ref, vec_ref, gates_ref, yp_ref, o0, o1, o2, l0, l1, l2, wpb_hbm, wab_hbm, wout_hbm,
             hn_ref, ya_ref, merged_ref, tm_ref, wpb_v, wab_v, wout_v, sem, scr):
        _load_weights([(wpb_hbm, wpb_v), (wab_hbm, wab_v), (wout_hbm, wout_v)], sem)
        gt = vec_ref[3:4, :]
        wts = _group_weights(_token_order((l0, l1, l2), scr))
        og = _token_order((o0, o1, o2), scr)
        ya = (wts[0] * og[0] + wts[1] * og[1] + wts[2] * og[2]).astype(BF16)
        ya_ref[...] = ya
        merged = (gates_ref[:, :D].astype(F32) * _dot(yp_ref[...], wpb_v[...])
                  + gates_ref[:, D:].astype(F32) * _dot(ya, wab_v[...])).astype(BF16)
        merged_ref[...] = merged
        tm = _dot(merged, wout_v[...])
        tm_ref[...] = tm.astype(BF16)
        hn_ref[...] = h_ref[...] + gt * tm

    grp = _row_tile(TS, GW)
    res = [_res_spec(r) for r in DILATIONS]
    return pl.pallas_call(
        body, name=name, grid=(S // TS,),
        in_specs=[_row_tile(TS, D), _const((8, D)), _row_tile(TS, 2 * D), grp] + res * 2 + [ANY, ANY, ANY],
        out_specs=[_row_tile(TS, D), grp, _row_tile(TS, D), _row_tile(TS, D)],
        out_shape=[jax.ShapeDtypeStruct((S, D), F32), jax.ShapeDtypeStruct((S, GW), BF16),
                   jax.ShapeDtypeStruct((S, D), BF16), jax.ShapeDtypeStruct((S, D), BF16)],
        scratch_shapes=[pltpu.VMEM((GW, D), BF16), pltpu.VMEM((GW, D), BF16), pltpu.VMEM((D, D), BF16),
                        pltpu.SemaphoreType.DMA((3,)), pltpu.VMEM(RES_SCRATCH, F32)],
        compiler_params=_params(),
    )(h, vec, gates, ypool, *o3, *lse3, wpb, wab, wout)


def _mix_out_bwd(dh, tm, vec, gates, ypool, o3, lse3, wpb, wab, wout, name, rider=None):
    S, D = dh.shape

    def body(*refs):
        if rider is None:
            return compute(*refs)
        host, mine = rider.split(refs, 14, 12)
        rider.head(mine, pl.program_id(0))
        compute(*host)
        rider.tail(mine, pl.program_id(0), S // TS)

    def compute(dh_ref, tm_ref, vec_ref, gates_ref, yp_ref, o0, o1, o2, l0, l1, l2, wpb_hbm, wab_hbm, wout_hbm,
                dtm_ref, dgl_ref, dypb_ref, dyab_ref, dyp_ref, do0, do1, do2, c0, c1, c2, red_ref,
                wpb_v, wab_v, wout_v, sem, scr):
        _load_weights([(wpb_hbm, wpb_v), (wab_hbm, wab_v), (wout_hbm, wout_v)], sem)

        @pl.when(pl.program_id(0) == 0)
        def _():
            red_ref[...] = jnp.zeros_like(red_ref)

        gt = vec_ref[3:4, :]
        dh_v = dh_ref[...]
        red_ref[2:3, :] += jnp.sum(tm_ref[...].astype(F32) * dh_v, axis=0, keepdims=True)
        dtm = (gt * dh_v).astype(BF16)
        dtm_ref[...] = dtm
        dm = _dot_nt(dtm, wout_v[...])
        wts = _group_weights(_token_order((l0, l1, l2), scr))
        og = _token_order((o0, o1, o2), scr)
        ya = wts[0] * og[0] + wts[1] * og[1] + wts[2] * og[2]
        ypb = _dot(yp_ref[...], wpb_v[...])
        yab = _dot(ya.astype(BF16), wab_v[...])
        gp = gates_ref[:, :D].astype(F32)
        ga = gates_ref[:, D:].astype(F32)
        dgl_ref[:, :D] = (dm * ypb * gp * (1.0 - gp)).astype(BF16)
        dgl_ref[:, D:] = (dm * yab * ga * (1.0 - ga)).astype(BF16)
        dypb = (dm * gp).astype(BF16)
        dyab = (dm * ga).astype(BF16)
        dypb_ref[...] = dypb
        dyab_ref[...] = dyab
        dyp_ref[...] = _dot_nt(dypb, wpb_v[...])
        dya = _dot_nt(dyab, wab_v[...])
        row = lax.broadcasted_iota(jnp.int32, (GW, GW), 0) // HEAD_DIM
        col = lax.broadcasted_iota(jnp.int32, (GW, GW), 1) // HEAD_DIM
        ones = jnp.where(row == col, 1.0, 0.0).astype(F32)
        tot = jnp.dot(dya * ya, ones, preferred_element_type=F32, precision=lax.Precision.HIGHEST)
        for wg, do_ref, c_ref, r in zip(wts, (do0, do1, do2), (c0, c1, c2), DILATIONS):
            _to_residues(wg * dya, do_ref, slice(0, GW), r, scr)
            _to_residues(-(wg * tot), c_ref, slice(0, GW), r, scr)

    grp = _row_tile(TS, GW)
    res = [_res_spec(r) for r in DILATIONS]
    specs = (
        [_row_tile(TS, D), _row_tile(TS, D), _const((8, D)), _row_tile(TS, 2 * D), grp] + res * 2
        + [ANY, ANY, ANY],
        [_row_tile(TS, D), _row_tile(TS, 2 * D), _row_tile(TS, D), _row_tile(TS, D), grp]
        + res * 2 + [_const((8, D))],
        [jax.ShapeDtypeStruct((S, D), BF16), jax.ShapeDtypeStruct((S, 2 * D), BF16),
         jax.ShapeDtypeStruct((S, D), BF16), jax.ShapeDtypeStruct((S, D), BF16), jax.ShapeDtypeStruct((S, GW), F32)]
        + [_res_shape(S, r, BF16) for r in DILATIONS] + [_res_shape(S, r, F32) for r in DILATIONS]
        + [jax.ShapeDtypeStruct((8, D), F32)],
        [pltpu.VMEM((GW, D), BF16), pltpu.VMEM((GW, D), BF16), pltpu.VMEM((D, D), BF16),
         pltpu.SemaphoreType.DMA((3,)), pltpu.VMEM(RES_SCRATCH, F32)])
    in_specs, out_specs, out_shape, scratch = specs if rider is None else rider.specs(*specs)
    outs = pl.pallas_call(
        body, name=name, grid=(S // TS,), in_specs=in_specs, out_specs=out_specs, out_shape=out_shape,
        scratch_shapes=scratch, compiler_params=_params(),
    )(dh, tm, vec, gates, ypool, *o3, *lse3, wpb, wab, wout, *(rider.arrays if rider else []))
    return outs if rider is None else (outs[:12], outs[12:])


def _ada_mod(c_all, w, b, name):
    def body(c_ref, w_ref, b_ref, cond_ref, mod_ref):
        cv = c_ref[...]
        cond = cv * jax.nn.sigmoid(cv)
        cond_ref[...] = cond
        mod_ref[...] = jnp.dot(cond, w_ref[...], preferred_element_type=F32,
                               precision=lax.Precision.HIGHEST) + b_ref[...]

    return pl.pallas_call(
        body, name=name,
        out_shape=[jax.ShapeDtypeStruct(c_all.shape, F32), jax.ShapeDtypeStruct((c_all.shape[0], w.shape[1]), F32)],
        compiler_params=_params(),
    )(c_all, w, b)


def _adamw_math(w, g, m, v):
    m = ADAM_B1 * m + (1.0 - ADAM_B1) * g
    v = ADAM_B2 * v + (1.0 - ADAM_B2) * (g * g)
    m_hat = m / (1.0 - ADAM_B1 ** ADAM_STEP)
    v_hat = v / (1.0 - ADAM_B2 ** ADAM_STEP)
    delta = -ADAM_LR * (m_hat / (jnp.sqrt(v_hat) + ADAM_EPS) + ADAM_WD * w)
    return delta, m, v


def _adamw_many(ws, gs, ms, vs, name):
    n = len(ws)

    def body(*refs):
        for k in range(n):
            w_ref, g_ref, m_ref, v_ref = (refs[j * n + k] for j in range(4))
            d_ref, mo_ref, vo_ref = (refs[(4 + j) * n + k] for j in range(3))
            d_ref[...], mo_ref[...], vo_ref[...] = _adamw_math(w_ref[...], g_ref[...], m_ref[...], v_ref[...])

    outs = pl.pallas_call(
        body, name=name, out_shape=[jax.ShapeDtypeStruct(t.shape, F32) for t in ws] * 3,
        compiler_params=_params(),
    )(*ws, *gs, *ms, *vs)
    return outs[:n], outs[n:2 * n], outs[2 * n:]


def _ada_grad_adamw(cond_t, dmod, w, m, v, name, tr=256):
    R, C = w.shape
    nb = dmod.shape[0]

    def body(ct_ref, dm_ref, w_ref, m_ref, v_ref, g_ref, d_ref, mo_ref, vo_ref):
        ct = ct_ref[...]
        dm = dm_ref[...]
        g = jnp.zeros((tr, C), F32)
        for bi in range(nb):
            g = g + ct[:, bi:bi + 1] * dm[bi:bi + 1, :]
        g_ref[...] = g
        d_ref[...], mo_ref[...], vo_ref[...] = _adamw_math(w_ref[...], g, m_ref[...], v_ref[...])

    spec = _row_tile(tr, C)
    out = jax.ShapeDtypeStruct((R, C), F32)
    return pl.pallas_call(
        body, name=name, grid=(R // tr,),
        in_specs=[_row_tile(tr, nb), _const((nb, C)), spec, spec, spec],
        out_specs=[spec] * 4, out_shape=[out] * 4,
        compiler_params=_params(),
    )(cond_t, dmod, w, m, v)


def _row_step(rows, cap=256):
    for cand in range(cap, 15, -16):
        if rows % cand == 0:
            return cand
    return rows


def _slot_sum(x_ref):
    acc = x_ref[0].astype(F32)
    for k in range(1, x_ref.shape[0]):
        acc = acc + x_ref[k].astype(F32)
    return acc


def _sum_slots(x, name, out_dtype=F32):
    n, R, C = x.shape
    tr = _row_step(R)

    def body(x_ref, o_ref):
        o_ref[...] = _slot_sum(x_ref).astype(out_dtype)

    return pl.pallas_call(
        body, name=name, grid=(R // tr,),
        in_specs=[pl.BlockSpec((n, tr, C), lambda i: (0, i, 0))],
        out_specs=_row_tile(tr, C), out_shape=jax.ShapeDtypeStruct((R, C), out_dtype),
        compiler_params=_params(),
    )(x)


def _sum_pair(core, g, recv, name):
    _, _, R, C = g.shape
    tr = _row_step(R, cap=1024)

    def body(core_ref, g_ref, r_ref, o_ref):
        o_ref[...] = (g_ref[...].astype(F32) + r_ref[...].astype(F32)).astype(BF16)

    return pl.pallas_call(
        body, name=name, out_shape=jax.ShapeDtypeStruct((4, R, C), BF16),
        grid_spec=pltpu.PrefetchScalarGridSpec(
            num_scalar_prefetch=1, grid=(4, R // tr),
            in_specs=[pl.BlockSpec((None, None, tr, C), lambda k, i, core_ref: (k, core_ref[0], i, 0)),
                      pl.BlockSpec((None, tr, C), lambda k, i, core_ref: (k, i, 0))],
            out_specs=pl.BlockSpec((None, tr, C), lambda k, i, core_ref: (k, i, 0))),
        compiler_params=_params(),
    )(core, g, recv)


def _sum_adamw(chip, own, recv, w, m, v, name):
    _, R, C = own.shape
    tr = _row_step(R, cap=512)

    def body(chip_ref, own_ref, r_ref, w_ref, m_ref, v_ref, g_ref, d_ref, mo_ref, vo_ref):
        g = own_ref[...].astype(F32) + _slot_sum(r_ref)
        g_ref[...] = g
        d_ref[...], mo_ref[...], vo_ref[...] = _adamw_math(w_ref[...], g, m_ref[...], v_ref[...])

    spec = pl.BlockSpec((tr, C), lambda i, chip_ref: (i, 0))
    out = jax.ShapeDtypeStruct((R, C), F32)
    return pl.pallas_call(
        body, name=name, out_shape=[out] * 4,
        grid_spec=pltpu.PrefetchScalarGridSpec(
            num_scalar_prefetch=1, grid=(R // tr,),
            in_specs=[pl.BlockSpec((None, tr, C), lambda i, chip_ref: (chip_ref[0], i, 0)),
                      pl.BlockSpec((3, tr, C), lambda i, chip_ref: (0, i, 0)), spec, spec, spec],
            out_specs=[spec] * 4),
        compiler_params=_params(),
    )(chip, own, recv, w, m, v)


def _place():
    return lax.axis_index("x"), lax.axis_index("y"), lax.axis_index("c")


def _gather_steps(x_refs, out_refs, send_sems, recv_sems):
    n = len(x_refs)
    x, y, c = _place()
    me, sibling = (x, y, c), (x, y, 1 - c)
    chips = [(1 - x, y), (x, 1 - y), (1 - x, 1 - y)]

    def rows(a, px, py, pc):
        return out_refs[a].at[4 * px + 2 * py + pc]

    def copy(a, k, block, to, src=None):
        return pltpu.make_async_remote_copy(
            src_ref=rows(a, *block) if src is None else src, dst_ref=rows(a, *block),
            send_sem=send_sems.at[a, k], recv_sem=recv_sems.at[a, k], device_id=to, device_id_type=MESH)

    def first(a):
        return [copy(a, 0, me, sibling, src=x_refs[a])] + [
            copy(a, 1 + j, me, (*chip, c), src=x_refs[a]) for j, chip in enumerate(chips)]

    def passed(a, j):
        return copy(a, 4 + j, (*chips[j], c), sibling)

    def start():
        for a in range(n):
            for cp in first(a):
                cp.start()

    def relay():
        for j, chip in enumerate(chips):
            for a in range(n):
                copy(a, 1 + j, (*chip, c), me).wait_recv()
                passed(a, j).start()

    def finish():
        for a in range(n):
            copy(a, 0, sibling, me).wait_recv()
            for j, chip in enumerate(chips):
                copy(a, 4 + j, (*chip, 1 - c), me).wait_recv()
        for a in range(n):
            for cp in first(a) + [passed(a, j) for j in range(3)]:
                cp.wait_send()

    return start, relay, finish


def _gather_tree_steps(x_refs, out_refs, send_sems, recv_sems):
    n = len(x_refs)
    x, y, c = _place()
    me, sibling = (x, y, c), (x, y, 1 - c)
    xn, yn, dg = (1 - x, y), (x, 1 - y), (1 - x, 1 - y)

    def rows(a, px, py, pc):
        return out_refs[a].at[4 * px + 2 * py + pc]

    def copy(a, k, block, to, src=None):
        return pltpu.make_async_remote_copy(
            src_ref=rows(a, *block) if src is None else src, dst_ref=rows(a, *block),
            send_sem=send_sems.at[a, k], recv_sem=recv_sems.at[a, k], device_id=to, device_id_type=MESH)

    def own(a):
        return [copy(a, 0, me, sibling, src=x_refs[a]), copy(a, 1, me, (*xn, c), src=x_refs[a]),
                copy(a, 2, me, (*yn, c), src=x_refs[a])]

    def north_hands_on(a):
        return copy(a, 3, (*xn, c), (*yn, c))

    def south_hands_on(a):
        return copy(a, 3, (*yn, c), (*xn, c))

    def to_sibling(a):
        return [copy(a, 4, (*xn, c), sibling), copy(a, 5, (*yn, c), sibling), copy(a, 6, (*dg, c), sibling)]

    def start():
        for a in range(n):
            for cp in own(a):
                cp.start()

    def relay_neighbours():
        for a in range(n):
            copy(a, 1, (*xn, c), me).wait_recv()
            to_sibling(a)[0].start()

        @pl.when(c == 1)
        def _():
            for a in range(n):
                north_hands_on(a).start()

        for a in range(n):
            copy(a, 2, (*yn, c), me).wait_recv()
            to_sibling(a)[1].start()

        @pl.when(c == 0)
        def _():
            for a in range(n):
                south_hands_on(a).start()

    def relay_diagonal():
        for a in range(n):
            copy(a, 3, (*dg, c), me).wait_recv()
            to_sibling(a)[2].start()

    def finish():
        for a in range(n):
            copy(a, 0, sibling, me).wait_recv()
            copy(a, 4, (*xn, 1 - c), me).wait_recv()
            copy(a, 5, (*yn, 1 - c), me).wait_recv()
            copy(a, 6, (*dg, 1 - c), me).wait_recv()
        for a in range(n):
            for cp in own(a) + to_sibling(a):
                cp.wait_send()

        @pl.when(c == 1)
        def _():
            for a in range(n):
                north_hands_on(a).wait_send()

        @pl.when(c == 0)
        def _():
            for a in range(n):
                south_hands_on(a).wait_send()

    return start, relay_neighbours, relay_diagonal, finish


def _all_gather_tree(arrs, name, gather=()):
    n, extra = len(arrs), len(gather)

    def body(*refs):
        sems = refs[2 * (n + extra):]
        if extra:
            g_start, g_relay, g_finish = _small_gather_steps(
                refs[n:n + extra], refs[2 * n + extra:2 * (n + extra)], *sems[2:])
            g_start()
        for step in _gather_tree_steps(refs[:n], refs[n + extra:2 * n + extra], *sems[:2]):
            step()
        if extra:
            g_relay()
            g_finish()

    return pl.pallas_call(
        body, name=name,
        out_shape=[jax.ShapeDtypeStruct((N_DEV,) + t.shape, t.dtype) for t in list(arrs) + list(gather)],
        in_specs=[ANY] * (n + extra), out_specs=[ANY] * (n + extra),
        scratch_shapes=[pltpu.SemaphoreType.DMA((n, 7)), pltpu.SemaphoreType.DMA((n, 7))]
        + (_small_gather_scratch(extra) if extra else []),
    )(*arrs, *gather)


def _all_gather(arrs, name, own=True):
    n = len(arrs)

    def body(*refs):
        x_refs, out_refs = refs[:n], refs[n:2 * n]
        send_sems, recv_sems, local_sems = refs[2 * n:]
        me = 4 * lax.axis_index("x") + 2 * lax.axis_index("y") + lax.axis_index("c")
        mine = [pltpu.make_async_copy(x_refs[a], out_refs[a].at[me], local_sems.at[a]) for a in range(n)] if own else []
        for cp in mine:
            cp.start()
        for step in _gather_steps(x_refs, out_refs, send_sems, recv_sems):
            step()
        for cp in mine:
            cp.wait()

    return pl.pallas_call(
        body, name=name, out_shape=[jax.ShapeDtypeStruct((N_DEV,) + t.shape, t.dtype) for t in arrs],
        in_specs=[ANY] * n, out_specs=[ANY] * n,
        scratch_shapes=[pltpu.SemaphoreType.DMA((n, 7)), pltpu.SemaphoreType.DMA((n, 7)),
                        pltpu.SemaphoreType.DMA((n,))],
    )(*arrs)


def _pair_exchange_steps(g_refs, out_refs, send_sems, recv_sems):
    x, y, c = _place()

    def give():
        return [pltpu.make_async_remote_copy(
            src_ref=g_refs[a].at[pl.ds(0, 4), 1 - c], dst_ref=out_refs[a], send_sem=send_sems.at[a],
            recv_sem=recv_sems.at[a], device_id=(x, y, 1 - c), device_id_type=MESH) for a in range(len(g_refs))]

    def start():
        for cp in give():
            cp.start()

    def finish():
        for cp in give():
            cp.wait()

    return start, finish


def _pair_exchange(arrs, name):
    n = len(arrs)

    def body(*refs):
        for step in _pair_exchange_steps(refs[:n], refs[n:2 * n], *refs[2 * n:]):
            step()

    return pl.pallas_call(
        body, name=name,
        out_shape=[jax.ShapeDtypeStruct((4,) + t.shape[2:], t.dtype) for t in arrs],
        in_specs=[ANY] * n, out_specs=[ANY] * n,
        scratch_shapes=[pltpu.SemaphoreType.DMA((n,)), pltpu.SemaphoreType.DMA((n,))],
    )(*arrs)


def _chip_exchange_steps(p_refs, out_refs, send_sems, recv_sems):
    x, y, c = _place()
    chips = [(1 - x, y), (x, 1 - y), (1 - x, 1 - y)]

    def copies():
        return [pltpu.make_async_remote_copy(
            src_ref=p_refs[a].at[2 * px + py], dst_ref=out_refs[a].at[j], send_sem=send_sems.at[a, j],
            recv_sem=recv_sems.at[a, j], device_id=(px, py, c), device_id_type=MESH)
            for a in range(len(p_refs)) for j, (px, py) in enumerate(chips)]

    def start():
        for cp in copies():
            cp.start()

    def finish():
        for cp in copies():
            cp.wait()

    return start, finish


def _small_gather_steps(x_refs, out_refs, send_sems, recv_sems, local_sems):
    me = 4 * lax.axis_index("x") + 2 * lax.axis_index("y") + lax.axis_index("c")
    start, relay, finish = _gather_steps(x_refs, out_refs, send_sems, recv_sems)

    def mine():
        return [pltpu.make_async_copy(x_refs[a], out_refs[a].at[me], local_sems.at[a]) for a in range(len(x_refs))]

    def start_all():
        for cp in mine():
            cp.start()
        start()

    def finish_all():
        finish()
        for cp in mine():
            cp.wait()

    return start_all, relay, finish_all


def _small_gather_scratch(k):
    return [pltpu.SemaphoreType.DMA((k, 7)), pltpu.SemaphoreType.DMA((k, 7)), pltpu.SemaphoreType.DMA((k,))]


def _chip_exchange(arrs, name, gather=()):
    n, k = len(arrs), len(gather)

    def body(*refs):
        ins, outs, sems = refs[:n + k], refs[n + k:2 * (n + k)], refs[2 * (n + k):]
        start, finish = _chip_exchange_steps(ins[:n], outs[:n], *sems[:2])
        if k:
            g_start, g_relay, g_finish = _small_gather_steps(ins[n:], outs[n:], *sems[2:])
            g_start()
        start()
        if k:
            g_relay()
        finish()
        if k:
            g_finish()

    return pl.pallas_call(
        body, name=name,
        out_shape=[jax.ShapeDtypeStruct((3,) + t.shape[1:], t.dtype) for t in arrs]
        + [jax.ShapeDtypeStruct((N_DEV,) + t.shape, t.dtype) for t in gather],
        in_specs=[ANY] * (n + k), out_specs=[ANY] * (n + k),
        scratch_shapes=[pltpu.SemaphoreType.DMA((n, 3)), pltpu.SemaphoreType.DMA((n, 3))]
        + (_small_gather_scratch(k) if k else []),
    )(*arrs, *gather)


class _Rider:
    def __init__(self, arrays, out_shape, sems, steps, relay_at=()):
        self.arrays, self.out_shape, self.scratch, self.steps = list(arrays), out_shape, sems, steps
        self.n = len(self.arrays)
        self.relay_at = relay_at

    def specs(self, in_specs, out_specs, out_shape, scratch):
        extra = [ANY] * self.n
        return in_specs + extra, out_specs + extra, out_shape + self.out_shape, scratch + self.scratch

    def split(self, refs, n_in, n_out):
        k = self.n
        a, b = n_in + k, n_in + k + n_out
        return refs[:n_in] + refs[a:b] + refs[b + k:-2], (refs[n_in:a], refs[b:b + k], refs[-2:])

    def head(self, mine, step):
        pl.when(step == 0)(self.steps(mine[0], mine[1], *mine[2])[0])

    def tail(self, mine, step, nsteps):
        steps = self.steps(mine[0], mine[1], *mine[2])
        for relay, frac in zip(steps[1:-1], self.relay_at):
            pl.when(step == min(int(frac * nsteps), nsteps - 1))(relay)
        pl.when(step == nsteps - 1)(steps[-1])


def _gather_rider(arrs, relay_at=(0.5, 0.78)):
    n = len(arrs)
    return _Rider(arrs, [jax.ShapeDtypeStruct((N_DEV,) + t.shape, t.dtype) for t in arrs],
                  [pltpu.SemaphoreType.DMA((n, 7)), pltpu.SemaphoreType.DMA((n, 7))], _gather_tree_steps,
                  relay_at)


def _pair_exchange_rider(arrs):
    n = len(arrs)
    return _Rider(arrs, [jax.ShapeDtypeStruct((4,) + t.shape[2:], t.dtype) for t in arrs],
                  [pltpu.SemaphoreType.DMA((n,)), pltpu.SemaphoreType.DMA((n,))], _pair_exchange_steps)


def _chip_exchange_rider(arrs):
    n = len(arrs)
    return _Rider(arrs, [jax.ShapeDtypeStruct((3,) + t.shape[1:], t.dtype) for t in arrs],
                  [pltpu.SemaphoreType.DMA((n, 3)), pltpu.SemaphoreType.DMA((n, 3))], _chip_exchange_steps)


def _rope_tables(positions):
    inv_freq = ROPE_THETA ** (-jnp.arange(0, HEAD_DIM, 2, dtype=F32) / HEAD_DIM)
    ang = positions.astype(F32)[:, None] * inv_freq
    cos, sin = jnp.cos(ang), jnp.sin(ang)
    return jnp.tile(cos, (1, 4)), jnp.tile(jnp.concatenate([-sin, sin], axis=1), (1, 2))


class _GradReducer:
    def __init__(self):
        self.core = lax.axis_index("c").astype(jnp.int32).reshape(1)
        self.own, self.others, self.waiting, self.riding = {}, {}, [], []

    def pair(self, named):
        mine = self._split(named)
        self._summed(mine, _pair_exchange(list(mine.values()), "reduce_pair_" + next(iter(named))))

    def pair_rider(self, named):
        self.pairing = self._split(named)
        return _pair_exchange_rider(list(self.pairing.values()))

    def pair_landed(self, results):
        self._summed(self.pairing, results)

    @staticmethod
    def _split(named):
        return {k: g.reshape((4, 2) + g.shape[1:]) for k, g in named.items()}

    def _summed(self, mine, theirs):
        for (k, g), r in zip(mine.items(), theirs):
            self.own[k] = _sum_pair(self.core, g, r, "sum_pair_" + k)
        self.waiting += list(mine)

    def rider(self):
        self.riding, self.waiting = self.waiting, []
        return _chip_exchange_rider([self.own[k] for k in self.riding])

    def landed(self, results):
        self.others.update(zip(self.riding, results))

    def flush(self, name, gather=()):
        keys, self.waiting = self.waiting, []
        res = _chip_exchange([self.own[k] for k in keys], name, gather=gather)
        self.others.update(zip(keys, res))
        return res[len(keys):]


def _by_owner(g):
    if g.ndim == 3:
        return g if g.shape[0] == N_DEV else g.reshape(N_DEV, g.shape[1] * g.shape[0] // N_DEV, g.shape[2])
    return g.reshape(N_DEV, g.shape[0] // N_DEV, g.shape[1])


def _local_step(x, target, positions, mod, small, W, late=None, red=None):
    S, D = x.shape
    gains =jnp.stack([small["g1"], small["g2"], small["g3"]])[:, None, :]
    v1, v2, v3 = jnp.pad(jnp.concatenate([gains, mod.reshape(3, 3, D)], axis=1), ((0, 0), (0, 4), (0, 0)))
    vf = jnp.pad(small["gf"][None], ((0, 7), (0, 0)))
    cos, sin = _rope_tables(positions)
    wbd = jax.scipy.linalg.block_diag(*[small["w_pool"][k] for k in range(4)]).astype(BF16)
    pscale = small["pool_scale"].reshape(1, GW)

    if late is None:
        h1, u1, ab1, act1, f1 = _ffn_fwd(x, v1, W["w1in"], W["w1out"], "ffn1_fwd")
    else:
        (h1, u1, ab1, act1, f1), landed = _ffn_fwd(x, v1, W["w1in"], W["w1out"], "ffn1_fwd", rider=late[0])
        W = {**W, **late[1](landed)}
    u2, p, gates, *qkv = _mix_in_fwd(h1, v2, cos, sin, W["win"], "mix_in_fwd")
    dpool, ypool = _pool_fwd(p, wbd, pscale, "pool_fwd")
    o3, lse3 = [], []
    for gi in range(len(DILATIONS)):
        o, lse = _attn_fwd(qkv[gi], qkv[3 + gi], qkv[6 + gi], f"attn_fwd_{gi}")
        o3.append(o)
        lse3.append(lse)
    h2, ya, merged, tm = _mix_out_fwd(h1, v2, gates, ypool, o3, lse3, W["wpb"], W["wab"], W["wout"], "mix_out_fwd")
    dh3, u3, ab3, act3, f3, loss_blk, dgf = _ffn_fwd(h2, v3, W["w2in"], W["w2out"], "ffn2_fwd", final=(target, vf))

    dh2, dab3, df3, red3 = _ffn_bwd(dh3, h2, f3, ab3, v3, W["w2in"], W["w2out"], "ffn2_bwd")
    half_f = ab3.shape[2] // 2
    G = {"w2in": _by_owner(_wgrad(dab3, u3, "wgrad_2in", tm=half_f))}
    mix_out_args = (dh2, tm, v2, gates, ypool, o3, lse3, W["wpb"], W["wab"], W["wout"], "mix_out_bwd")
    if red is None:
        G["w2out"] = _by_owner(_wgrad(act3, df3, "wgrad_2out", tm=half_f))
        mix_out = _mix_out_bwd(*mix_out_args)
    else:
        g2out, landed = _wgrad(act3, df3, "wgrad_2out", tm=half_f, rider=red.pair_rider({"w2in": G["w2in"]}))
        red.pair_landed(landed)
        G["w2out"] = _by_owner(g2out)
        red.pair({"w2out": G["w2out"]})
        mix_out, landed = _mix_out_bwd(*mix_out_args, rider=red.rider())
        red.landed(landed)
    (dtm, dgl, dypb, dyab, dyp, do0, do1, do2, c0, c1, c2, red2o) = mix_out
    dq3, dk3, dv3 = [], [], []
    for gi, (do, ct) in enumerate(zip((do0, do1, do2), (c0, c1, c2))):
        dq, dk, dv = _attn_bwd(qkv[gi], qkv[3 + gi], qkv[6 + gi], do, lse3[gi], ct, f"attn_bwd_{gi}")
        dq3.append(dq)
        dk3.append(dk)
        dv3.append(dv)
    dp, dwbd, dps = _pool_bwd(dyp, dpool, wbd, pscale, "pool_bwd")
    dh1, dproj, red2i = _mix_in_bwd(dh2, h1, v2, cos, sin, dp, dq3 + dk3 + dv3, dgl, W["win"], "mix_in_bwd")
    G["win"] = _by_owner(_wgrad(dproj, u2, "wgrad_in", tm=1152))
    G["wpb"] = _full_to_cols(_wgrad(ypool, dypb, "wgrad_pb"))
    G["wab"] = _full_to_cols(_wgrad(ya, dyab, "wgrad_ab"))
    if red is None:
        G["wout"] = _by_owner(_wgrad(merged, dtm, "wgrad_out"))
    else:
        gout, landed = _wgrad(merged, dtm, "wgrad_out", rider=red.pair_rider({k: G[k] for k in ("win", "wpb", "wab")}))
        red.pair_landed(landed)
        G["wout"] = _by_owner(gout)
        red.pair({"wout": G["wout"]})
    dx, dab1, df1, red1 = _ffn_bwd(dh1, x, f1, ab1, v1, W["w1in"], W["w1out"], "ffn1_bwd")
    dmod = jnp.concatenate([red1[:3], (red2i + red2o)[:3], red3[:3]])
    dsmall = {
        "g1": red1[3], "g2": red2i[3], "g3": red3[3], "gf": dgf[0],
        "w_pool": jnp.stack([dwbd[k * 64:(k + 1) * 64, k * 64:(k + 1) * 64] for k in range(4)]),
        "pool_scale": dps[0],
    }
    if red is None:
        G["w1in"] = _by_owner(_wgrad(dab1, u1, "wgrad_1in", tm=half_f))
        G["w1out"] = _by_owner(_wgrad(act1, df1, "wgrad_1out", tm=half_f))
        return loss_blk[0, 0], dx, G, dmod, dsmall
    g1in, landed = _wgrad(dab1, u1, "wgrad_1in", tm=half_f, rider=red.rider())
    red.landed(landed)
    red.pair({"w1in": _by_owner(g1in)})
    g1out, landed = _wgrad(act1, df1, "wgrad_1out", tm=half_f, rider=red.rider())
    red.landed(landed)
    red.pair({"w1out": _by_owner(g1out)})
    part = _pack_small(dict(b_ada=dmod, g_norm_ffn1=dsmall["g1"], g_norm_mix=dsmall["g2"], g_norm_ffn2=dsmall["g3"],
                            g_final=dsmall["gf"], w_pool=dsmall["w_pool"], pool_scale=dsmall["pool_scale"]),
                       loss_blk[0, 0])
    (parts,) = red.flush("reduce_chips_w1out", gather=[part])
    return dx, parts


SHARDED = ("w_ffn1_in", "w_ffn1_out", "w_in", "w_pool_branch", "w_attn_branch", "w_out", "w_ffn2_in", "w_ffn2_out")
TRANSPOSED = ("w_ffn1_in", "w_in", "w_ffn2_in")
FIRST = ("w_ffn1_in", "w_ffn1_out")
LATER = tuple(n for n in SHARDED if n not in FIRST)
GRAD_KEY = dict(w_ffn1_in="w1in", w_ffn1_out="w1out", w_in="win", w_pool_branch="wpb", w_attn_branch="wab",
                w_out="wout", w_ffn2_in="w2in", w_ffn2_out="w2out")


def _cols_to_full(g):
    return g.transpose(1, 0, 2).reshape(g.shape[1], N_DEV * g.shape[2])


def _full_to_cols(t):
    return t.reshape(t.shape[0], N_DEV, t.shape[1] // N_DEV).transpose(1, 0, 2)


SMALL = (("b_ada", 72), ("g_norm_ffn1", 8), ("g_norm_mix", 8), ("g_norm_ffn2", 8), ("g_final", 8),
         ("w_pool", 128), ("pool_scale", 8))


def _pack_small(vals, loss):
    rows = []
    for name, nrows in SMALL:
        t = vals[name].reshape(-1, 128)
        rows.append(jnp.pad(t, ((0, nrows - t.shape[0]), (0, 0))))
    rows.append(jnp.full((8, 128), loss, F32))
    return jnp.concatenate(rows)


def _unpack_small(slab, shapes):
    out, off = {}, 0
    for name, nrows in SMALL:
        used = 1
        for d in shapes[name]:
            used *= d
        out[name] = slab[off:off + used // 128].reshape(shapes[name])
        off += nrows
    return out, slab[off, 0]


def _as_2d(t):
    return t.reshape(-1, t.shape[-1])


def kernel(x, c, positions, w_ada, b_ada, g_norm_ffn1, w_ffn1_in, w_ffn1_out, g_norm_mix, w_in, w_pool, pool_scale, w_pool_branch, w_attn_branch, w_out, g_norm_ffn2, w_ffn2_in, w_ffn2_out, g_final, loss_target, m_w_ada, m_b_ada, m_g_norm_ffn1, m_w_ffn1_in, m_w_ffn1_out, m_g_norm_mix, m_w_in, m_w_pool, m_pool_scale, m_w_pool_branch, m_w_attn_branch, m_w_out, m_g_norm_ffn2, m_w_ffn2_in, m_w_ffn2_out, m_g_final, v_w_ada, v_b_ada, v_g_norm_ffn1, v_w_ffn1_in, v_w_ffn1_out, v_g_norm_mix, v_w_in, v_w_pool, v_pool_scale, v_w_pool_branch, v_w_attn_branch, v_w_out, v_g_norm_ffn2, v_w_ffn2_in, v_w_ffn2_out, v_g_final):
    names = ["w_ada", "b_ada", "g_norm_ffn1", "w_ffn1_in", "w_ffn1_out", "g_norm_mix", "w_in", "w_pool", "pool_scale",
             "w_pool_branch", "w_attn_branch", "w_out", "g_norm_ffn2", "w_ffn2_in", "w_ffn2_out", "g_final"]
    w = dict(w_ada=w_ada, b_ada=b_ada, g_norm_ffn1=g_norm_ffn1, w_ffn1_in=w_ffn1_in, w_ffn1_out=w_ffn1_out,
             g_norm_mix=g_norm_mix, w_in=w_in, w_pool=w_pool, pool_scale=pool_scale, w_pool_branch=w_pool_branch,
             w_attn_branch=w_attn_branch, w_out=w_out, g_norm_ffn2=g_norm_ffn2, w_ffn2_in=w_ffn2_in,
             w_ffn2_out=w_ffn2_out, g_final=g_final)
    m = dict(w_ada=m_w_ada, b_ada=m_b_ada, g_norm_ffn1=m_g_norm_ffn1, w_ffn1_in=m_w_ffn1_in, w_ffn1_out=m_w_ffn1_out,
             g_norm_mix=m_g_norm_mix, w_in=m_w_in, w_pool=m_w_pool, pool_scale=m_pool_scale,
             w_pool_branch=m_w_pool_branch, w_attn_branch=m_w_attn_branch, w_out=m_w_out, g_norm_ffn2=m_g_norm_ffn2,
             w_ffn2_in=m_w_ffn2_in, w_ffn2_out=m_w_ffn2_out, g_final=m_g_final)
    v = dict(w_ada=v_w_ada, b_ada=v_b_ada, g_norm_ffn1=v_g_norm_ffn1, w_ffn1_in=v_w_ffn1_in, w_ffn1_out=v_w_ffn1_out,
             g_norm_mix=v_g_norm_mix, w_in=v_w_in, w_pool=v_w_pool, pool_scale=v_pool_scale,
             w_pool_branch=v_w_pool_branch, w_attn_branch=v_w_attn_branch, w_out=v_w_out, g_norm_ffn2=v_g_norm_ffn2,
             w_ffn2_in=v_w_ffn2_in, w_ffn2_out=v_w_ffn2_out, g_final=v_g_final)
    shapes = {n: w[n].shape for n in names}
    me = 4 * lax.axis_index("x") + 2 * lax.axis_index("y") + lax.axis_index("c")
    D = x.shape[-1]
    n_mod = w_ada.shape[-1] * N_DEV // D

    def local(t, name):
        return t[name][0].T if name in TRANSPOSED else t[name][0]

    shards = {name: local(w, name).astype(BF16) for name in SHARDED}

    def gather_done(names, fulls):
        return {name: lax.dynamic_update_index_in_dim(full, shards[name], me, axis=0)
                for name, full in zip(names, fulls)}

    def ffn_weights(g, pre):
        return {"w%sin" % pre: g["w_ffn%s_in" % pre].reshape(2, -1, D),
                "w%sout" % pre: g["w_ffn%s_out" % pre].reshape(-1, D)}

    def later_weights(fulls):
        g = gather_done(LATER, fulls)
        return dict(win=g["w_in"].reshape(-1, D), wpb=_cols_to_full(g["w_pool_branch"]),
                    wab=_cols_to_full(g["w_attn_branch"]), wout=g["w_out"].reshape(D, D), **ffn_weights(g, "2"))

    *first, c_all = _all_gather_tree([shards[n] for n in FIRST], "gather_ffn1", gather=[c.reshape(D // 128, 128)])
    W = ffn_weights(gather_done(FIRST, first), "1")

    ada_cols = w_ada.shape[-1]
    b_mine = lax.dynamic_slice_in_dim(b_ada, me * ada_cols, ada_cols, axis=1)
    cond, mod_part = _ada_mod(c_all.reshape(N_DEV, D), w_ada[0], b_mine, "ada_mod")
    (mod_all,) = _all_gather([mod_part.reshape(-1, 128)], "gather_mod")
    mod_all = mod_all.reshape(N_DEV, N_DEV, ada_cols)
    mod = lax.dynamic_index_in_dim(mod_all, me, axis=1, keepdims=False).reshape(n_mod, D)

    small = dict(g1=g_norm_ffn1[0], g2=g_norm_mix[0], g3=g_norm_ffn2[0], gf=g_final, w_pool=w_pool[0],
                 pool_scale=pool_scale[0])
    red = _GradReducer()
    dx, parts = _local_step(
        x[0], loss_target[0], positions[0], mod, small, W,
        late=(_gather_rider([shards[n] for n in LATER]), later_weights), red=red)
    chip = (2 * lax.axis_index("x") + lax.axis_index("y")).astype(jnp.int32).reshape(1)

    gsmall, loss = _unpack_small(_sum_slots(parts, "sum_small"), shapes)
    rows_mine = ada_cols // 128
    dmod_mine = lax.dynamic_slice_in_dim(parts, me * rows_mine, rows_mine, axis=1).reshape(N_DEV, ada_cols)

    grads, delta, new_m, new_v = {}, {}, {}, {}
    grads["w_ada"], delta["w_ada"], new_m["w_ada"], new_v["w_ada"] = (
        t[None] for t in _ada_grad_adamw(cond.T, dmod_mine, w_ada[0], m_w_ada[0], v_w_ada[0], "ada_grad_adamw"))
    for name in SHARDED:
        key = GRAD_KEY[name]
        res = _sum_adamw(chip, red.own[key], red.others[key], local(w, name), local(m, name), local(v, name),
                         "adamw_" + name)
        grads[name], delta[name], new_m[name], new_v[name] = (
            (t.T if name in TRANSPOSED else t)[None] for t in res)
    small_names = [name for name, _ in SMALL]
    res = _adamw_many(*([_as_2d(t[name]) for name in small_names] for t in (w, gsmall, m, v)), "adamw_small")
    for dst, vals in zip((delta, new_m, new_v), res):
        dst.update({name: t.reshape(shapes[name]) for name, t in zip(small_names, vals)})
    grads.update(gsmall)

    return (loss, dx[None], *[grads[n] for n in names], *[delta[n] for n in names],
            *[new_m[n] for n in names], *[new_v[n] for n in names])
```

```python
import functools

import jax
import jax.numpy as jnp
from jax import lax
from jax.experimental import pallas as pl
from jax.experimental.pallas import tpu as pltpu

F32 = jnp.float32
BF16 = jnp.bfloat16
MESH = pl.DeviceIdType.MESH
ANY = pl.BlockSpec(memory_space=pl.ANY)

N_DEV = 8
EPS = 1e-6
HEAD_DIM = 64
HEADS = 4
GW = HEADS * HEAD_DIM
DILATIONS = (1, 4, 16)
BAND = 128
QB = 128
POOL_WINDOWS = (2, 4, 8, 16)
HALO = 16
ROPE_THETA = 10000.0

ADAM_LR = 0.001
ADAM_B1 = 0.9
ADAM_B2 = 0.999
ADAM_EPS = 1e-08
ADAM_WD = 0.01
ADAM_STEP = 10

VMEM_LIMIT = 56 * 1024 * 1024
TS = 512
FFN_TS = 256
FFN_CHUNKS = (2816,)

NT = (((1,), (1,)), ((), ()))
TN = (((0,), (0,)), ((), ()))


def _params(**kw):
    return pltpu.CompilerParams(vmem_limit_bytes=VMEM_LIMIT, **kw)


def _dot(a, b):
    return jnp.dot(a, b, preferred_element_type=F32)


def _dot_nt(a, b):
    return lax.dot_general(a, b, NT, preferred_element_type=F32)


def _dot_tn(a, b):
    return lax.dot_general(a, b, TN, preferred_element_type=F32)


def _load_weights(pairs, sem):
    @pl.when(pl.program_id(0) == 0)
    def _():
        copies = [pltpu.make_async_copy(src, dst, sem.at[i]) for i, (src, dst) in enumerate(pairs)]
        for cp in copies:
            cp.start()
        for cp in copies:
            cp.wait()


def _norm_mod(x, g, sc, sh):
    r = lax.rsqrt(jnp.mean(x * x, axis=-1, keepdims=True) + EPS)
    xn = x * r
    y = xn * g
    return r, xn, y, y * (1.0 + sc) + sh


def _norm_mod_bwd(du, r, xn, y, g, sc):
    dsh = jnp.sum(du, axis=0, keepdims=True)
    dsc = jnp.sum(du * y, axis=0, keepdims=True)
    dy = du * (1.0 + sc)
    dg = jnp.sum(dy * xn, axis=0, keepdims=True)
    dxn = dy * g
    dx = r * (dxn - xn * jnp.mean(dxn * xn, axis=-1, keepdims=True))
    return dx, dsh, dsc, dg


def _row_tile(ts, width):
    return pl.BlockSpec((ts, width), lambda i: (i, 0))


def _const(shape):
    return pl.BlockSpec(shape, lambda *_: (0,) * len(shape))


def _ffn_chunks(Fd):
    assert sum(FFN_CHUNKS) == Fd
    edges = [sum(FFN_CHUNKS[:k]) for k in range(len(FFN_CHUNKS) + 1)]
    return [slice(a, b) for a, b in zip(edges[:-1], edges[1:])]
def _final_tile(x, g, target):
    r = lax.rsqrt(jnp.mean(x * x, axis=-1, keepdims=True) + EPS)
    xn = x * r
    err = xn * g - target
    loss = 0.5 * jnp.sum(jnp.mean(err * err, axis=-1, keepdims=True))
    dy = err * (1.0 / x.shape[-1])
    dg = jnp.sum(dy * xn, axis=0, keepdims=True)
    dxn = dy * g
    return r * (dxn - xn * jnp.mean(dxn * xn, axis=-1, keepdims=True)), loss, dg


def _ffn_fwd(h, vec, win, wout, name, rider=None, final=None):
    TS = FFN_TS
    S, D = h.shape
    _, Fd, _ = win.shape
    n_in, n_out = (6, 7) if final else (4, 5)

    def body(*refs):
        if rider is None:
            return compute(*refs)
        host, mine = rider.split(refs, n_in, n_out)
        rider.head(mine, pl.program_id(0))
        compute(*host)
        rider.tail(mine, pl.program_id(0), S // TS)

    def compute(*refs):
        h_ref, vec_ref, win_hbm, wout_hbm = refs[:4]
        hn_ref, u_ref, ab_ref, act_ref, f_ref = refs[n_in:n_in + 5]
        win_v, wout_v, sem = refs[n_in + n_out:]
        _load_weights([(win_hbm, win_v), (wout_hbm, wout_v)], sem)
        x = h_ref[...]
        g, sh, sc, gt = (vec_ref[k:k + 1, :] for k in range(4))
        _, _, _, u = _norm_mod(x, g, sc, sh)
        ub = u.astype(BF16)
        u_ref[...] = ub
        acc = jnp.zeros((TS, D), F32)
        for sl in _ffn_chunks(Fd):
            a = _dot_nt(ub, win_v[0, sl, :])
            b = _dot_nt(ub, win_v[1, sl, :])
            act = ((a * jax.nn.sigmoid(a)) * b).astype(BF16)
            ab_ref[0, :, sl] = a.astype(BF16)
            ab_ref[1, :, sl] = b.astype(BF16)
            act_ref[:, sl] = act
            acc = acc + _dot(act, wout_v[sl, :])
        f_ref[...] = acc.astype(BF16)
        hn = x + (0.5 * gt) * acc
        if not final:
            hn_ref[...] = hn
            return
        t_ref, gf_ref = refs[4:6]
        loss_ref, dgf_ref = refs[n_in + 5:n_in + 7]

        @pl.when(pl.program_id(0) == 0)
        def _():
            loss_ref[...] = jnp.zeros_like(loss_ref)
            dgf_ref[...] = jnp.zeros_like(dgf_ref)

        hn_ref[...], loss, dg = _final_tile(hn, gf_ref[0:1, :], t_ref[...])
        loss_ref[...] += loss
        dgf_ref[0:1, :] += dg

    specs = (
        [_row_tile(TS, D), _const((8, D)), ANY, ANY] + ([_row_tile(TS, D), _const((8, D))] if final else []),
        [_row_tile(TS, D), _row_tile(TS, D), pl.BlockSpec((2, TS, Fd), lambda i: (0, i, 0)),
         _row_tile(TS, Fd), _row_tile(TS, D)] + ([_const((8, 128)), _const((8, D))] if final else []),
        [jax.ShapeDtypeStruct((S, D), F32), jax.ShapeDtypeStruct((S, D), BF16),
         jax.ShapeDtypeStruct((2, S, Fd), BF16), jax.ShapeDtypeStruct((S, Fd), BF16),
         jax.ShapeDtypeStruct((S, D), BF16)]
        + ([jax.ShapeDtypeStruct((8, 128), F32), jax.ShapeDtypeStruct((8, D), F32)] if final else []),
        [pltpu.VMEM(win.shape, BF16), pltpu.VMEM(wout.shape, BF16), pltpu.SemaphoreType.DMA((2,))])
    in_specs, out_specs, out_shape, scratch = specs if rider is None else rider.specs(*specs)
    outs = pl.pallas_call(
        body, name=name, grid=(S // TS,), in_specs=in_specs, out_specs=out_specs, out_shape=out_shape,
        scratch_shapes=scratch, compiler_params=_params(),
    )(h, vec, win, wout, *(final or ()), *(rider.arrays if rider else []))
    return outs if rider is None else (outs[:n_out], outs[n_out:])


def _ffn_bwd(dh, h, f, ab, vec, win, wout, name):
    TS = FFN_TS
    S, D = h.shape
    _, Fd, _ = win.shape

    def body(dh_ref, h_ref, f_ref, ab_ref, vec_ref, win_hbm, wout_hbm,
             dhp_ref, dab_ref, df_ref, red_ref, win_v, wout_v, sem):
        _load_weights([(win_hbm, win_v), (wout_hbm, wout_v)], sem)

        @pl.when(pl.program_id(0) == 0)
        def _():
            red_ref[...] = jnp.zeros_like(red_ref)

        dh_v = dh_ref[...]
        x = h_ref[...]
        g, sh, sc, gt = (vec_ref[k:k + 1, :] for k in range(4))
        dgt = jnp.sum((0.5 * f_ref[...].astype(F32)) * dh_v, axis=0, keepdims=True)
        dfb = ((0.5 * gt) * dh_v).astype(BF16)
        df_ref[...] = dfb
        du = jnp.zeros((TS, D), F32)
        for sl in _ffn_chunks(Fd):
            dact = _dot_nt(dfb, wout_v[sl, :])
            av = ab_ref[0, :, sl].astype(F32)
            bv = ab_ref[1, :, sl].astype(F32)
            sg = jax.nn.sigmoid(av)
            da = (dact * bv * (sg * (1.0 + av * (1.0 - sg)))).astype(BF16)
            db = (dact * (av * sg)).astype(BF16)
            dab_ref[0, :, sl] = da
            dab_ref[1, :, sl] = db
            du = du + _dot(da, win_v[0, sl, :]) + _dot(db, win_v[1, sl, :])
        r, xn, y, _ = _norm_mod(x, g, sc, sh)
        dx, dsh, dsc, dg = _norm_mod_bwd(du, r, xn, y, g, sc)
        dhp_ref[...] = dh_v + dx
        red_ref[0:1, :] += dsh
        red_ref[1:2, :] += dsc
        red_ref[2:3, :] += dgt
        red_ref[3:4, :] += dg

    ab_spec = pl.BlockSpec((2, TS, Fd), lambda i: (0, i, 0))
    return pl.pallas_call(
        body, name=name, grid=(S // TS,),
        in_specs=[_row_tile(TS, D), _row_tile(TS, D), _row_tile(TS, D), ab_spec, _const((8, D)), ANY, ANY],
        out_specs=[_row_tile(TS, D), ab_spec, _row_tile(TS, D), _const((8, D))],
        out_shape=[jax.ShapeDtypeStruct((S, D), F32), jax.ShapeDtypeStruct((2, S, Fd), BF16),
                   jax.ShapeDtypeStruct((S, D), BF16), jax.ShapeDtypeStruct((8, D), F32)],
        scratch_shapes=[pltpu.VMEM(win.shape, BF16), pltpu.VMEM(wout.shape, BF16), pltpu.SemaphoreType.DMA((2,))],
        compiler_params=_params(),
    )(dh, h, f, ab, vec, win, wout)


def _wgrad(x, y, name, tm=None, ts=2048, rider=None):
    xb = x.ndim == 3
    nb = x.shape[0] if xb else 0
    S, M = x.shape[-2:]
    N = y.shape[-1]
    tm = tm or M
    ts = min(ts, S)
    nk = S // ts
    grid = (max(nb, 1), M // tm, nk)

    def body(*refs):
        if rider is None:
            return compute(*refs)
        host, mine = rider.split(refs, 2, 1)
        step = (pl.program_id(0) * grid[1] + pl.program_id(1)) * grid[2] + pl.program_id(2)
        rider.head(mine, step)
        compute(*host)
        rider.tail(mine, step, grid[0] * grid[1] * grid[2])

    def compute(x_ref, y_ref, o_ref, acc):
        k = pl.program_id(2)

        @pl.when(k == 0)
        def _():
            acc[...] = jnp.zeros_like(acc)

        acc[...] += _dot_tn(x_ref[...], y_ref[...])

        @pl.when(k == nk - 1)
        def _():
            o_ref[...] = acc[...].astype(BF16)

    x_spec = (pl.BlockSpec((None, ts, tm), lambda b, i, k: (b, k, i)) if xb
              else pl.BlockSpec((ts, tm), lambda b, i, k: (k, i)))
    y_spec = pl.BlockSpec((ts, N), lambda b, i, k: (k, 0))
    if xb:
        o_spec, o_shape = pl.BlockSpec((None, tm, N), lambda b, i, k: (b, i, 0)), (nb, M, N)
    else:
        o_spec, o_shape = pl.BlockSpec((tm, N), lambda b, i, k: (i, 0)), (M, N)
    specs = ([x_spec, y_spec], [o_spec], [jax.ShapeDtypeStruct(o_shape, BF16)], [pltpu.VMEM((tm, N), F32)])
    in_specs, out_specs, out_shape, scratch = specs if rider is None else rider.specs(*specs)
    outs = pl.pallas_call(
        body, name=name, grid=grid, in_specs=in_specs, out_specs=out_specs, out_shape=out_shape,
        scratch_shapes=scratch, compiler_params=_params(),
    )(x, y, *(rider.arrays if rider else []))
    return outs[0] if rider is None else (outs[0], outs[1:])


P_OFF, Q_OFF, K_OFF, V_OFF, G_OFF = 0, 256, 1024, 1792, 2560
IN_WIDTH = 4608


def _first_half_mask(ts):
    lane = lax.broadcasted_iota(jnp.int32, (ts, 128), 1)
    return (lane % HEAD_DIM) < (HEAD_DIM // 2)


def _rope(t, cos, sin_signed, first, sign):
    partner = jnp.where(first, pltpu.roll(t, 96, 1), pltpu.roll(t, 32, 1))
    return t * cos + sign * (partner * sin_signed)


def _res_spec(r):
    return pl.BlockSpec((r, TS // r, GW), lambda i: (0, i, 0))


def _res_shape(S, r, dtype):
    return jax.ShapeDtypeStruct((r, S // r, GW), dtype)


def _to_residues(piece, out_ref, lanes, r, scr):
    if r == 1:
        out_ref[0, :, lanes] = piece.astype(out_ref.dtype)
        return
    for h in range(piece.shape[1] // 128):
        scr[h] = piece[:, h * 128:(h + 1) * 128]
        at = slice(lanes.start + h * 128, lanes.start + (h + 1) * 128)
        for res in range(r):
            out_ref[res, :, at] = scr[h, pl.ds(res, TS // r, stride=r), :].astype(out_ref.dtype)


def _from_residues(in_ref, lanes, r, scr):
    if r == 1:
        return in_ref[0, :, lanes].astype(F32)
    halves = (lanes.stop - lanes.start) // 128
    for h in range(halves):
        at = slice(lanes.start + h * 128, lanes.start + (h + 1) * 128)
        for res in range(r):
            scr[h, pl.ds(res, TS // r, stride=r), :] = in_ref[res, :, at].astype(F32)
    return scr[0] if halves == 1 else jnp.concatenate([scr[0], scr[1]], axis=1)


RES_SCRATCH = (2, TS, 128)


def _mix_in_fwd(h, vec, cos, sin, win, name):
    S, D = h.shape

    def body(h_ref, vec_ref, cos_ref, sin_ref, win_hbm, u_ref, p_ref, gates_ref, *rest):
        qkv_refs, (win_v, sem, scr) = rest[:9], rest[9:]
        _load_weights([(win_hbm, win_v)], sem)
        g, sh, sc = (vec_ref[k:k + 1, :] for k in range(3))
        _, _, _, u = _norm_mod(h_ref[...], g, sc, sh)
        ub = u.astype(BF16)
        u_ref[...] = ub
        p_ref[...] = _dot_nt(ub, win_v[P_OFF:Q_OFF, :])
        cosv, sinv = cos_ref[...], sin_ref[...]
        first = _first_half_mask(TS)
        for which, off in enumerate((Q_OFF, K_OFF, V_OFF)):
            t = _dot_nt(ub, win_v[off:off + 3 * GW, :])
            for gi in range(3):
                for half in range(2):
                    c0 = gi * GW + half * 128
                    piece = t[:, c0:c0 + 128]
                    if which < 2:
                        piece = _rope(piece, cosv, sinv, first, 1.0)
                    _to_residues(piece, qkv_refs[which * 3 + gi], slice(half * 128, (half + 1) * 128),
                                 DILATIONS[gi], scr)
        gates_ref[...] = jax.nn.sigmoid(_dot_nt(ub, win_v[G_OFF:IN_WIDTH, :])).astype(BF16)

    return pl.pallas_call(
        body, name=name, grid=(S // TS,),
        in_specs=[_row_tile(TS, D), _const((8, D)), _row_tile(TS, 128), _row_tile(TS, 128), ANY],
        out_specs=[_row_tile(TS, D), _row_tile(TS, GW), _row_tile(TS, 2 * D)] + [_res_spec(r) for r in DILATIONS] * 3,
        out_shape=[jax.ShapeDtypeStruct((S, D), BF16), jax.ShapeDtypeStruct((S, GW), F32),
                   jax.ShapeDtypeStruct((S, 2 * D), BF16)] + [_res_shape(S, r, BF16) for r in DILATIONS] * 3,
        scratch_shapes=[pltpu.VMEM((IN_WIDTH, D), BF16), pltpu.SemaphoreType.DMA((1,)), pltpu.VMEM(RES_SCRATCH, F32)],
        compiler_params=_params(),
    )(h, vec, cos, sin, win)


def _mix_in_bwd(dh, h, vec, cos, sin, dp, dqkv, dgl, win, name):
    S, D = h.shape

    def body(dh_ref, h_ref, vec_ref, cos_ref, sin_ref, dp_ref, *rest):
        dqkv_refs = rest[:9]
        dgl_ref, win_hbm, dhp_ref, dproj_ref, red_ref, win_v, sem, scr = rest[9:]
        _load_weights([(win_hbm, win_v)], sem)

        @pl.when(pl.program_id(0) == 0)
        def _():
            red_ref[...] = jnp.zeros_like(red_ref)

        cosv, sinv = cos_ref[...], sin_ref[...]
        first = _first_half_mask(TS)
        dproj_ref[:, P_OFF:Q_OFF] = dp_ref[...].astype(BF16)
        for which, off in enumerate((Q_OFF, K_OFF, V_OFF)):
            for gi in range(3):
                for half in range(2):
                    piece = _from_residues(dqkv_refs[which * 3 + gi], slice(half * 128, (half + 1) * 128),
                                           DILATIONS[gi], scr)
                    if which < 2:
                        piece = _rope(piece, cosv, sinv, first, -1.0)
                    c0 = off + gi * GW + half * 128
                    dproj_ref[:, c0:c0 + 128] = piece.astype(BF16)
        dproj_ref[:, G_OFF:IN_WIDTH] = dgl_ref[...]
        du = _dot(dproj_ref[...], win_v[...])
        g, sh, sc = (vec_ref[k:k + 1, :] for k in range(3))
        r, xn, y, _ = _norm_mod(h_ref[...], g, sc, sh)
        dx, dsh, dsc, dg = _norm_mod_bwd(du, r, xn, y, g, sc)
        dhp_ref[...] = dh_ref[...] + dx
        red_ref[0:1, :] += dsh
        red_ref[1:2, :] += dsc
        red_ref[3:4, :] += dg

    return pl.pallas_call(
        body, name=name, grid=(S // TS,),
        in_specs=[_row_tile(TS, D), _row_tile(TS, D), _const((8, D)), _row_tile(TS, 128), _row_tile(TS, 128),
                  _row_tile(TS, GW)] + [_res_spec(r) for r in DILATIONS] * 3 + [_row_tile(TS, 2 * D), ANY],
        out_specs=[_row_tile(TS, D), _row_tile(TS, IN_WIDTH), _const((8, D))],
        out_shape=[jax.ShapeDtypeStruct((S, D), F32), jax.ShapeDtypeStruct((S, IN_WIDTH), BF16),
                   jax.ShapeDtypeStruct((8, D), F32)],
        scratch_shapes=[pltpu.VMEM((IN_WIDTH, D), BF16), pltpu.SemaphoreType.DMA((1,)), pltpu.VMEM(RES_SCRATCH, F32)],
        compiler_params=_params(),
    )(dh, h, vec, cos, sin, dp, *dqkv, dgl, win)


def _pool_lanes(rows):
    lane = lax.broadcasted_iota(jnp.int32, (rows, GW), 1)
    return lane // HEAD_DIM


def _pool_window(rows):
    grp = _pool_lanes(rows)
    w = jnp.full((rows, GW), POOL_WINDOWS[0], jnp.int32)
    for k in range(1, len(POOL_WINDOWS)):
        w = jnp.where(grp == k, POOL_WINDOWS[k], w)
    return grp, w


def _pool_fwd(p, wbd, scale, name, ts=1024):
    S = p.shape[0]
    ext = ts + HALO

    def body(pc_ref, ph_ref, wbd_ref, sc_ref, d_ref, y_ref):
        i = pl.program_id(0)
        cur = pc_ref[...]
        halo = jnp.where(i > 0, ph_ref[...], 0.0)
        s = jnp.concatenate([halo, cur], axis=0)
        grp, w = _pool_window(ext)
        sel = jnp.zeros((ext, GW), F32)
        for k, wk in enumerate(POOL_WINDOWS):
            s = s + pltpu.roll(s, wk // 2, 0)
            sel = jnp.where(grp == k, s, sel)
        t = i * ts + lax.broadcasted_iota(jnp.int32, (ts, GW), 0)
        count = jnp.minimum(t + 1, w[HALO:]).astype(F32)
        d = (sel[HALO:] / count - cur).astype(BF16)
        d_ref[...] = d
        y_ref[...] = (_dot(d, wbd_ref[...]) * sc_ref[...]).astype(BF16)

    return pl.pallas_call(
        body, name=name, grid=(S // ts,),
        in_specs=[_row_tile(ts, GW),
                  pl.BlockSpec((HALO, GW), lambda i: (jnp.maximum(i * (ts // HALO) - 1, 0), 0)),
                  _const((GW, GW)), _const((1, GW))],
        out_specs=[_row_tile(ts, GW), _row_tile(ts, GW)],
        out_shape=[jax.ShapeDtypeStruct((S, GW), BF16), jax.ShapeDtypeStruct((S, GW), BF16)],
        compiler_params=_params(),
    )(p, p, wbd, scale)


def _pool_bwd(dy, d, wbd, scale, name, ts=1024):
    S = dy.shape[0]
    ext = ts + HALO
    nsteps = S // ts
    last_halo = S // HALO - 1

    def body(dyc_ref, dyh_ref, d_ref, wbd_ref, sc_ref, dp_ref, dw_ref, ds_ref):
        i = pl.program_id(0)

        @pl.when(i == 0)
        def _():
            dw_ref[...] = jnp.zeros_like(dw_ref)
            ds_ref[...] = jnp.zeros_like(ds_ref)

        dyc = dyc_ref[...]
        dyh = jnp.where(i < nsteps - 1, dyh_ref[...], 0.0)
        dys = (jnp.concatenate([dyc, dyh], axis=0) * sc_ref[...]).astype(BF16)
        dd = _dot_nt(dys, wbd_ref[...])
        grp, w = _pool_window(ext)
        t = i * ts + lax.broadcasted_iota(jnp.int32, (ext, GW), 0)
        s = dd / jnp.minimum(t + 1, w).astype(F32)
        sel = jnp.zeros((ext, GW), F32)
        for k, wk in enumerate(POOL_WINDOWS):
            s = s + pltpu.roll(s, ext - wk // 2, 0)
            sel = jnp.where(grp == k, s, sel)
        dp_ref[...] = sel[:ts] - dd[:ts]
        dv = d_ref[...]
        z = _dot(dv, wbd_ref[...])
        ds_ref[0:1, :] += jnp.sum(dyc * z, axis=0, keepdims=True)
        dw_ref[...] += _dot_tn(dv, dys[:ts])

    return pl.pallas_call(
        body, name=name, grid=(nsteps,),
        in_specs=[_row_tile(ts, GW),
                  pl.BlockSpec((HALO, GW), lambda i: (jnp.minimum((i + 1) * (ts // HALO), last_halo), 0)),
                  _row_tile(ts, GW), _const((GW, GW)), _const((1, GW))],
        out_specs=[_row_tile(ts, GW), _const((GW, GW)), _const((8, GW))],
        out_shape=[jax.ShapeDtypeStruct((S, GW), F32), jax.ShapeDtypeStruct((GW, GW), F32),
                   jax.ShapeDtypeStruct((8, GW), F32)],
        compiler_params=_params(),
    )(dy, dy, d, wbd, scale)


def _head_id(rows):
    return lax.broadcasted_iota(jnp.int32, (rows, GW), 1) // HEAD_DIM


def _stack_heads(t, hid):
    return jnp.concatenate([jnp.where(hid == h, t, jnp.zeros_like(t)) for h in range(HEADS)], axis=0)


def _unstack_heads(t_all, hid):
    out = jnp.zeros((QB, GW), F32)
    for h in range(HEADS):
        out = jnp.where(hid == h, t_all[h * QB:(h + 1) * QB], out)
    return out


def _band_masks():
    row = lax.broadcasted_iota(jnp.int32, (HEADS * QB, 2 * QB), 0) % QB
    col = lax.broadcasted_iota(jnp.int32, (HEADS * QB, 2 * QB), 1)
    rel = row + QB - col
    band = (rel >= 0) & (rel <= BAND)
    return band, band & (col >= QB)


def _stream_mask(masks, n, off):
    band, first = masks
    return band if off > 0 else first | (band & (n > 0))


FWD_STREAMS, BWD_STREAMS = 16, 8


def _streams(r, nb, most):
    residues = min(r, most)
    parts = min(most // residues, nb)
    nbs = nb // parts
    return nbs, parts, [(lambda rb, l=l: residues * rb + l, p * nbs) for l in range(residues) for p in range(parts)]


def _attn_fwd(q, k, v, name):
    r, L, _ = q.shape
    nbs, parts, streams = _streams(r, L // QB, FWD_STREAMS)
    ns = len(streams)
    grid = (r * parts // ns, nbs)

    def cur(res, off):
        return pl.BlockSpec((None, QB, GW), lambda rb, n: (res(rb), n + off, 0))

    def prev(res, off):
        return pl.BlockSpec((None, QB, GW), lambda rb, n: (res(rb), jnp.maximum(n + off - 1, 0), 0))

    def body(*refs):
        n = pl.program_id(1)
        hid = _head_id(QB)
        masks = _band_masks()
        o_ref, lse_ref = refs[5 * len(streams):]
        for l, (_, off) in enumerate(streams):
            q_ref, kp_ref, kc_ref, vp_ref, vc_ref = refs[5 * l:5 * l + 5]
            qs = _stack_heads(q_ref[...], hid)
            kc = jnp.concatenate([kp_ref[...], kc_ref[...]], axis=0)
            vc = jnp.concatenate([vp_ref[...], vc_ref[...]], axis=0)
            s = _dot_nt(qs, kc) * (HEAD_DIM ** -0.5)
            s = jnp.where(_stream_mask(masks, n, off), s, -jnp.inf)
            m = jnp.max(s, axis=-1, keepdims=True)
            e = jnp.exp(s - m)
            den = jnp.sum(e, axis=-1, keepdims=True)
            lse = m + jnp.log(den)
            pr = (e * (1.0 / den)).astype(BF16)
            o_ref[l] = _unstack_heads(_dot(pr, vc), hid).astype(BF16)
            lse_ref[l] = _unstack_heads(jnp.broadcast_to(lse, (HEADS * QB, GW)), hid)

    in_specs, args = [], []
    for res, off in streams:
        in_specs += [cur(res, off), prev(res, off), cur(res, off), prev(res, off), cur(res, off)]
        args += [q, k, k, v, v]
    out = jax.ShapeDtypeStruct((ns * grid[0], nbs * QB, GW), F32)
    both = pl.BlockSpec((ns, QB, GW), lambda rb, n: (rb, n, 0))
    o, lse = pl.pallas_call(
        body, name=name, grid=grid, in_specs=in_specs, out_specs=[both, both],
        out_shape=[jax.ShapeDtypeStruct(out.shape, BF16), out],
        compiler_params=_params(),
    )(*args)
    return o.reshape(q.shape), lse.reshape(q.shape)


def _head_rows(t_full, hid):
    del hid
    return jnp.concatenate([t_full[:, h * HEAD_DIM:h * HEAD_DIM + 1] for h in range(HEADS)], axis=0)


def _attn_bwd(q, k, v, do, lse, cterm, name):
    r, L, _ = q.shape
    nbs, parts, streams = _streams(r, L // QB, BWD_STREAMS if r == 1 else min(BWD_STREAMS, r))
    ns = len(streams)

    def spec(res, index):
        return pl.BlockSpec((None, QB, GW), lambda rb, n: (res(rb), index(n), 0))

    def body(*refs):
        dq_ref, dk_ref, dv_ref, carry_k, carry_v, seam_k, seam_v = refs[8 * ns:]
        n = pl.program_id(1)

        @pl.when(n == 0)
        def _():
            carry_k[...] = jnp.zeros_like(carry_k)
            carry_v[...] = jnp.zeros_like(carry_v)

        @pl.when(n < nbs)
        def _():
            hid = _head_id(QB)
            masks = _band_masks()
            for l, (_, off) in enumerate(streams):
                q_ref, do_ref, lse_ref, c_ref, kp_ref, kc_ref, vp_ref, vc_ref = refs[8 * l:8 * l + 8]
                qs = _stack_heads(q_ref[...], hid)
                dos = _stack_heads(do_ref[...], hid)
                kc = jnp.concatenate([kp_ref[...], kc_ref[...]], axis=0)
                vc = jnp.concatenate([vp_ref[...], vc_ref[...]], axis=0)
                s = _dot_nt(qs, kc) * (HEAD_DIM ** -0.5)
                s = jnp.where(_stream_mask(masks, n, off), s, -jnp.inf)
                p = jnp.exp(s - _head_rows(lse_ref[...], hid))
                dp = _dot_nt(dos, vc)
                ds = (p * (dp + _head_rows(c_ref[...], hid)) * (HEAD_DIM ** -0.5)).astype(BF16)
                dq_ref[l] = _unstack_heads(_dot(ds, kc), hid).astype(BF16)
                dkc = _dot_tn(ds, qs)
                dvc = _dot_tn(p.astype(BF16), dos)
                if l % parts > 0:
                    @pl.when(n == 0)
                    def _():
                        seam_k[l] = dkc[:QB]
                        seam_v[l] = dvc[:QB]
                dk_ref[l] = (carry_k[l] + dkc[:QB]).astype(BF16)
                dv_ref[l] = (carry_v[l] + dvc[:QB]).astype(BF16)
                carry_k[l] = dkc[QB:]
                carry_v[l] = dvc[QB:]

        @pl.when(n == nbs)
        def _():
            for l in range(ns):
                if l % parts + 1 < parts:
                    dk_ref[l] = (carry_k[l] + seam_k[l + 1]).astype(BF16)
                    dv_ref[l] = (carry_v[l] + seam_v[l + 1]).astype(BF16)
                else:
                    dk_ref[l] = carry_k[l].astype(BF16)
                    dv_ref[l] = carry_v[l].astype(BF16)

    in_specs, args = [], []
    for res, off in streams:
        qside = functools.partial(lambda n, off: jnp.minimum(n, nbs - 1) + off, off=off)
        kprev = functools.partial(lambda n, off: jnp.maximum(jnp.minimum(n, nbs) - 1 + off, 0), off=off)
        in_specs += [spec(res, qside)] * 4 + [spec(res, kprev), spec(res, qside)] * 2
        args += [q, do, lse, cterm, k, k, v, v]
    out = jax.ShapeDtypeStruct((r * parts, nbs * QB, GW), BF16)
    qout = pl.BlockSpec((ns, QB, GW), lambda rb, n: (rb, jnp.minimum(n, nbs - 1), 0))
    kout = pl.BlockSpec((ns, QB, GW), lambda rb, n: (rb, jnp.maximum(n - 1, 0), 0))
    buf = pltpu.VMEM((ns, QB, GW), F32)
    outs = pl.pallas_call(
        body, name=name, grid=(r * parts // ns, nbs + 1),
        in_specs=in_specs, out_specs=[qout, kout, kout], out_shape=[out, out, out],
        scratch_shapes=[buf, buf, buf, buf],
        compiler_params=_params(),
    )(*args)
    return [t.reshape(q.shape) for t in outs]


def _token_order(refs, scr):
    return [_from_residues(ref, slice(0, GW), r, scr) for ref, r in zip(refs, DILATIONS)]


def _group_weights(lses):
    l0, l1, l2 = lses
    m = jnp.maximum(jnp.maximum(l0, l1), l2)
    e = [jnp.exp(l - m) for l in (l0, l1, l2)]
    den = e[0] + e[1] + e[2]
    return [ei / den for ei in e]


def _mix_out_fwd(h, vec, gates, ypool, o3, lse3, wpb, wab, wout, name):
    S, D = h.shape

    def body(h_ref, vec_ref, gates_ref, yp_ref, o0, o1, o2, l0, l1, l2, wpb_hbm, wab_hbm, wout_hbm,
             hn_ref, ya_ref, merged_ref, tm_ref, wpb_v, wab_v, wout_v, sem, scr):
        _load_weights([(wpb_hbm, wpb_v), (wab_hbm, wab_v), (wout_hbm, wout_v)], sem)
        gt = vec_ref[3:4, :]
        wts = _group_weights(_token_order((l0, l1, l2), scr))
        og = _token_order((o0, o1, o2), scr)
        ya = (wts[0] * og[0] + wts[1] * og[1] + wts[2] * og[2]).astype(BF16)
        ya_ref[...] = ya
        merged = (gates_ref[:, :D].astype(F32) * _dot(yp_ref[...], wpb_v[...])
                  + gates_ref[:, D:].astype(F32) * _dot(ya, wab_v[...])).astype(BF16)
        merged_ref[...] = merged
        tm = _dot(merged, wout_v[...])
        tm_ref[...] = tm.astype(BF16)
        hn_ref[...] = h_ref[...] + gt * tm

    grp = _row_tile(TS, GW)
    res = [_res_spec(r) for r in DILATIONS]
    return pl.pallas_call(
        body, name=name, grid=(S // TS,),
        in_specs=[_row_tile(TS, D), _const((8, D)), _row_tile(TS, 2 * D), grp] + res * 2 + [ANY, ANY, ANY],
        out_specs=[_row_tile(TS, D), grp, _row_tile(TS, D), _row_tile(TS, D)],
        out_shape=[jax.ShapeDtypeStruct((S, D), F32), jax.ShapeDtypeStruct((S, GW), BF16),
                   jax.ShapeDtypeStruct((S, D), BF16), jax.ShapeDtypeStruct((S, D), BF16)],
        scratch_shapes=[pltpu.VMEM((GW, D), BF16), pltpu.VMEM((GW, D), BF16), pltpu.VMEM((D, D), BF16),
                        pltpu.SemaphoreType.DMA((3,)), pltpu.VMEM(RES_SCRATCH, F32)],
        compiler_params=_params(),
    )(h, vec, gates, ypool, *o3, *lse3, wpb, wab, wout)


def _mix_out_bwd(dh, tm, vec, gates, ypool, o3, lse3, wpb, wab, wout, name, rider=None):
    S, D = dh.shape

    def body(*refs):
        if rider is None:
            return compute(*refs)
        host, mine = rider.split(refs, 14, 12)
        rider.head(mine, pl.program_id(0))
        compute(*host)
        rider.tail(mine, pl.program_id(0), S // TS)

    def compute(dh_ref, tm_ref, vec_ref, gates_ref, yp_ref, o0, o1, o2, l0, l1, l2, wpb_hbm, wab_hbm, wout_hbm,
                dtm_ref, dgl_ref, dypb_ref, dyab_ref, dyp_ref, do0, do1, do2, c0, c1, c2, red_ref,
                wpb_v, wab_v, wout_v, sem, scr):
        _load_weights([(wpb_hbm, wpb_v), (wab_hbm, wab_v), (wout_hbm, wout_v)], sem)

        @pl.when(pl.program_id(0) == 0)
        def _():
            red_ref[...] = jnp.zeros_like(red_ref)

        gt = vec_ref[3:4, :]
        dh_v = dh_ref[...]
        red_ref[2:3, :] += jnp.sum(tm_ref[...].astype(F32) * dh_v, axis=0, keepdims=True)
        dtm = (gt * dh_v).astype(BF16)
        dtm_ref[...] = dtm
        dm = _dot_nt(dtm, wout_v[...])
        wts = _group_weights(_token_order((l0, l1, l2), scr))
        og = _token_order((o0, o1, o2), scr)
        ya = wts[0] * og[0] + wts[1] * og[1] + wts[2] * og[2]
        ypb = _dot(yp_ref[...], wpb_v[...])
        yab = _dot(ya.astype(BF16), wab_v[...])
        gp = gates_ref[:, :D].astype(F32)
        ga = gates_ref[:, D:].astype(F32)
        dgl_ref[:, :D] = (dm * ypb * gp * (1.0 - gp)).astype(BF16)
        dgl_ref[:, D:] = (dm * yab * ga * (1.0 - ga)).astype(BF16)
        dypb = (dm * gp).astype(BF16)
        dyab = (dm * ga).astype(BF16)
        dypb_ref[...] = dypb
        dyab_ref[...] = dyab
        dyp_ref[...] = _dot_nt(dypb, wpb_v[...])
        dya = _dot_nt(dyab, wab_v[...])
        row = lax.broadcasted_iota(jnp.int32, (GW, GW), 0) // HEAD_DIM
        col = lax.broadcasted_iota(jnp.int32, (GW, GW), 1) // HEAD_DIM
        ones = jnp.where(row == col, 1.0, 0.0).astype(F32)
        tot = jnp.dot(dya * ya, ones, preferred_element_type=F32, precision=lax.Precision.HIGHEST)
        for wg, do_ref, c_ref, r in zip(wts, (do0, do1, do2), (c0, c1, c2), DILATIONS):
            _to_residues(wg * dya, do_ref, slice(0, GW), r, scr)
            _to_residues(-(wg * tot), c_ref, slice(0, GW), r, scr)

    grp = _row_tile(TS, GW)
    res = [_res_spec(r) for r in DILATIONS]
    specs = (
        [_row_tile(TS, D), _row_tile(TS, D), _const((8, D)), _row_tile(TS, 2 * D), grp] + res * 2
        + [ANY, ANY, ANY],
        [_row_tile(TS, D), _row_tile(TS, 2 * D), _row_tile(TS, D), _row_tile(TS, D), grp]
        + res * 2 + [_const((8, D))],
        [jax.ShapeDtypeStruct((S, D), BF16), jax.ShapeDtypeStruct((S, 2 * D), BF16),
         jax.ShapeDtypeStruct((S, D), BF16), jax.ShapeDtypeStruct((S, D), BF16), jax.ShapeDtypeStruct((S, GW), F32)]
        + [_res_shape(S, r, BF16) for r in DILATIONS] + [_res_shape(S, r, F32) for r in DILATIONS]
        + [jax.ShapeDtypeStruct((8, D), F32)],
        [pltpu.VMEM((GW, D), BF16), pltpu.VMEM((GW, D), BF16), pltpu.VMEM((D, D), BF16),
         pltpu.SemaphoreType.DMA((3,)), pltpu.VMEM(RES_SCRATCH, F32)])
    in_specs, out_specs, out_shape, scratch = specs if rider is None else rider.specs(*specs)
    outs = pl.pallas_call(
        body, name=name, grid=(S // TS,), in_specs=in_specs, out_specs=out_specs, out_shape=out_shape,
        scratch_shapes=scratch, compiler_params=_params(),
    )(dh, tm, vec, gates, ypool, *o3, *lse3, wpb, wab, wout, *(rider.arrays if rider else []))
    return outs if rider is None else (outs[:12], outs[12:])


def _ada_mod(c_all, w, b, name):
    def body(c_ref, w_ref, b_ref, cond_ref, mod_ref):
        cv = c_ref[...]
        cond = cv * jax.nn.sigmoid(cv)
        cond_ref[...] = cond
        mod_ref[...] = jnp.dot(cond, w_ref[...], preferred_element_type=F32,
                               precision=lax.Precision.HIGHEST) + b_ref[...]

    return pl.pallas_call(
        body, name=name,
        out_shape=[jax.ShapeDtypeStruct(c_all.shape, F32), jax.ShapeDtypeStruct((c_all.shape[0], w.shape[1]), F32)],
        compiler_params=_params(),
    )(c_all, w, b)


def _adamw_math(w, g, m, v):
    m = ADAM_B1 * m + (1.0 - ADAM_B1) * g
    v = ADAM_B2 * v + (1.0 - ADAM_B2) * (g * g)
    m_hat = m / (1.0 - ADAM_B1 ** ADAM_STEP)
    v_hat = v / (1.0 - ADAM_B2 ** ADAM_STEP)
    delta = -ADAM_LR * (m_hat / (jnp.sqrt(v_hat) + ADAM_EPS) + ADAM_WD * w)
    return delta, m, v


def _adamw_many(ws, gs, ms, vs, name):
    n = len(ws)

    def body(*refs):
        for k in range(n):
            w_ref, g_ref, m_ref, v_ref = (refs[j * n + k] for j in range(4))
            d_ref, mo_ref, vo_ref = (refs[(4 + j) * n + k] for j in range(3))
            d_ref[...], mo_ref[...], vo_ref[...] = _adamw_math(w_ref[...], g_ref[...], m_ref[...], v_ref[...])

    outs = pl.pallas_call(
        body, name=name, out_shape=[jax.ShapeDtypeStruct(t.shape, F32) for t in ws] * 3,
        compiler_params=_params(),
    )(*ws, *gs, *ms, *vs)
    return outs[:n], outs[n:2 * n], outs[2 * n:]


def _ada_grad_adamw(cond_t, dmod, w, m, v, name, tr=256):
    R, C = w.shape
    nb = dmod.shape[0]

    def body(ct_ref, dm_ref, w_ref, m_ref, v_ref, g_ref, d_ref, mo_ref, vo_ref):
        ct = ct_ref[...]
        dm = dm_ref[...]
        g = jnp.zeros((tr, C), F32)
        for bi in range(nb):
            g = g + ct[:, bi:bi + 1] * dm[bi:bi + 1, :]
        g_ref[...] = g
        d_ref[...], mo_ref[...], vo_ref[...] = _adamw_math(w_ref[...], g, m_ref[...], v_ref[...])

    spec = _row_tile(tr, C)
    out = jax.ShapeDtypeStruct((R, C), F32)
    return pl.pallas_call(
        body, name=name, grid=(R // tr,),
        in_specs=[_row_tile(tr, nb), _const((nb, C)), spec, spec, spec],
        out_specs=[spec] * 4, out_shape=[out] * 4,
        compiler_params=_params(),
    )(cond_t, dmod, w, m, v)


def _row_step(rows, cap=256):
    for cand in range(cap, 15, -16):
        if rows % cand == 0:
            return cand
    return rows


def _slot_sum(x_ref):
    acc = x_ref[0].astype(F32)
    for k in range(1, x_ref.shape[0]):
        acc = acc + x_ref[k].astype(F32)
    return acc


def _sum_slots(x, name, out_dtype=F32):
    n, R, C = x.shape
    tr = _row_step(R)

    def body(x_ref, o_ref):
        o_ref[...] = _slot_sum(x_ref).astype(out_dtype)

    return pl.pallas_call(
        body, name=name, grid=(R // tr,),
        in_specs=[pl.BlockSpec((n, tr, C), lambda i: (0, i, 0))],
        out_specs=_row_tile(tr, C), out_shape=jax.ShapeDtypeStruct((R, C), out_dtype),
        compiler_params=_params(),
    )(x)


def _sum_pair(core, g, recv, name):
    _, _, R, C = g.shape
    tr = _row_step(R, cap=1024)

    def body(core_ref, g_ref, r_ref, o_ref):
        o_ref[...] = (g_ref[...].astype(F32) + r_ref[...].astype(F32)).astype(BF16)

    return pl.pallas_call(
        body, name=name, out_shape=jax.ShapeDtypeStruct((4, R, C), BF16),
        grid_spec=pltpu.PrefetchScalarGridSpec(
            num_scalar_prefetch=1, grid=(4, R // tr),
            in_specs=[pl.BlockSpec((None, None, tr, C), lambda k, i, core_ref: (k, core_ref[0], i, 0)),
                      pl.BlockSpec((None, tr, C), lambda k, i, core_ref: (k, i, 0))],
            out_specs=pl.BlockSpec((None, tr, C), lambda k, i, core_ref: (k, i, 0))),
        compiler_params=_params(),
    )(core, g, recv)


def _sum_adamw(chip, own, recv, w, m, v, name):
    _, R, C = own.shape
    tr = _row_step(R, cap=512)

    def body(chip_ref, own_ref, r_ref, w_ref, m_ref, v_ref, g_ref, d_ref, mo_ref, vo_ref):
        g = own_ref[...].astype(F32) + _slot_sum(r_ref)
        g_ref[...] = g
        d_ref[...], mo_ref[...], vo_ref[...] = _adamw_math(w_ref[...], g, m_ref[...], v_ref[...])

    spec = pl.BlockSpec((tr, C), lambda i, chip_ref: (i, 0))
    out = jax.ShapeDtypeStruct((R, C), F32)
    return pl.pallas_call(
        body, name=name, out_shape=[out] * 4,
        grid_spec=pltpu.PrefetchScalarGridSpec(
            num_scalar_prefetch=1, grid=(R // tr,),
            in_specs=[pl.BlockSpec((None, tr, C), lambda i, chip_ref: (chip_ref[0], i, 0)),
                      pl.BlockSpec((3, tr, C), lambda i, chip_ref: (0, i, 0)), spec, spec, spec],
            out_specs=[spec] * 4),
        compiler_params=_params(),
    )(chip, own, recv, w, m, v)


def _place():
    return lax.axis_index("x"), lax.axis_index("y"), lax.axis_index("c")


def _gather_steps(x_refs, out_refs, send_sems, recv_sems):
    n = len(x_refs)
    x, y, c = _place()
    me, sibling = (x, y, c), (x, y, 1 - c)
    chips = [(1 - x, y), (x, 1 - y), (1 - x, 1 - y)]

    def rows(a, px, py, pc):
        return out_refs[a].at[4 * px + 2 * py + pc]

    def copy(a, k, block, to, src=None):
        return pltpu.make_async_remote_copy(
            src_ref=rows(a, *block) if src is None else src, dst_ref=rows(a, *block),
            send_sem=send_sems.at[a, k], recv_sem=recv_sems.at[a, k], device_id=to, device_id_type=MESH)

    def first(a):
        return [copy(a, 0, me, sibling, src=x_refs[a])] + [
            copy(a, 1 + j, me, (*chip, c), src=x_refs[a]) for j, chip in enumerate(chips)]

    def passed(a, j):
        return copy(a, 4 + j, (*chips[j], c), sibling)

    def start():
        for a in range(n):
            for cp in first(a):
                cp.start()

    def relay():
        for j, chip in enumerate(chips):
            for a in range(n):
                copy(a, 1 + j, (*chip, c), me).wait_recv()
                passed(a, j).start()

    def finish():
        for a in range(n):
            copy(a, 0, sibling, me).wait_recv()
            for j, chip in enumerate(chips):
                copy(a, 4 + j, (*chip, 1 - c), me).wait_recv()
        for a in range(n):
            for cp in first(a) + [passed(a, j) for j in range(3)]:
                cp.wait_send()

    return start, relay, finish


def _gather_tree_steps(x_refs, out_refs, send_sems, recv_sems):
    n = len(x_refs)
    x, y, c = _place()
    me, sibling = (x, y, c), (x, y, 1 - c)
    xn, yn, dg = (1 - x, y), (x, 1 - y), (1 - x, 1 - y)

    def rows(a, px, py, pc):
        return out_refs[a].at[4 * px + 2 * py + pc]

    def copy(a, k, block, to, src=None):
        return pltpu.make_async_remote_copy(
            src_ref=rows(a, *block) if src is None else src, dst_ref=rows(a, *block),
            send_sem=send_sems.at[a, k], recv_sem=recv_sems.at[a, k], device_id=to, device_id_type=MESH)

    def own(a):
        return [copy(a, 0, me, sibling, src=x_refs[a]), copy(a, 1, me, (*xn, c), src=x_refs[a]),
                copy(a, 2, me, (*yn, c), src=x_refs[a])]

    def north_hands_on(a):
        return copy(a, 3, (*xn, c), (*yn, c))

    def south_hands_on(a):
        return copy(a, 3, (*yn, c), (*xn, c))

    def to_sibling(a):
        return [copy(a, 4, (*xn, c), sibling), copy(a, 5, (*yn, c), sibling), copy(a, 6, (*dg, c), sibling)]

    def start():
        for a in range(n):
            for cp in own(a):
                cp.start()

    def relay_neighbours():
        for a in range(n):
            copy(a, 1, (*xn, c), me).wait_recv()
            to_sibling(a)[0].start()

        @pl.when(c == 1)
        def _():
            for a in range(n):
                north_hands_on(a).start()

        for a in range(n):
            copy(a, 2, (*yn, c), me).wait_recv()
            to_sibling(a)[1].start()

        @pl.when(c == 0)
        def _():
            for a in range(n):
                south_hands_on(a).start()

    def relay_diagonal():
        for a in range(n):
            copy(a, 3, (*dg, c), me).wait_recv()
            to_sibling(a)[2].start()

    def finish():
        for a in range(n):
            copy(a, 0, sibling, me).wait_recv()
            copy(a, 4, (*xn, 1 - c), me).wait_recv()
            copy(a, 5, (*yn, 1 - c), me).wait_recv()
            copy(a, 6, (*dg, 1 - c), me).wait_recv()
        for a in range(n):
            for cp in own(a) + to_sibling(a):
                cp.wait_send()

        @pl.when(c == 1)
        def _():
            for a in range(n):
                north_hands_on(a).wait_send()

        @pl.when(c == 0)
        def _():
            for a in range(n):
                south_hands_on(a).wait_send()

    return start, relay_neighbours, relay_diagonal, finish


def _all_gather_tree(arrs, name, gather=()):
    n, extra = len(arrs), len(gather)

    def body(*refs):
        sems = refs[2 * (n + extra):]
        if extra:
            g_start, g_relay, g_finish = _small_gather_steps(
                refs[n:n + extra], refs[2 * n + extra:2 * (n + extra)], *sems[2:])
            g_start()
        for step in _gather_tree_steps(refs[:n], refs[n + extra:2 * n + extra], *sems[:2]):
            step()
        if extra:
            g_relay()
            g_finish()

    return pl.pallas_call(
        body, name=name,
        out_shape=[jax.ShapeDtypeStruct((N_DEV,) + t.shape, t.dtype) for t in list(arrs) + list(gather)],
        in_specs=[ANY] * (n + extra), out_specs=[ANY] * (n + extra),
        scratch_shapes=[pltpu.SemaphoreType.DMA((n, 7)), pltpu.SemaphoreType.DMA((n, 7))]
        + (_small_gather_scratch(extra) if extra else []),
    )(*arrs, *gather)


def _all_gather(arrs, name, own=True):
    n = len(arrs)

    def body(*refs):
        x_refs, out_refs = refs[:n], refs[n:2 * n]
        send_sems, recv_sems, local_sems = refs[2 * n:]
        me = 4 * lax.axis_index("x") + 2 * lax.axis_index("y") + lax.axis_index("c")
        mine = [pltpu.make_async_copy(x_refs[a], out_refs[a].at[me], local_sems.at[a]) for a in range(n)] if own else []
        for cp in mine:
            cp.start()
        for step in _gather_steps(x_refs, out_refs, send_sems, recv_sems):
            step()
        for cp in mine:
            cp.wait()

    return pl.pallas_call(
        body, name=name, out_shape=[jax.ShapeDtypeStruct((N_DEV,) + t.shape, t.dtype) for t in arrs],
        in_specs=[ANY] * n, out_specs=[ANY] * n,
        scratch_shapes=[pltpu.SemaphoreType.DMA((n, 7)), pltpu.SemaphoreType.DMA((n, 7)),
                        pltpu.SemaphoreType.DMA((n,))],
    )(*arrs)


def _pair_exchange_steps(g_refs, out_refs, send_sems, recv_sems):
    x, y, c = _place()

    def give():
        return [pltpu.make_async_remote_copy(
            src_ref=g_refs[a].at[pl.ds(0, 4), 1 - c], dst_ref=out_refs[a], send_sem=send_sems.at[a],
            recv_sem=recv_sems.at[a], device_id=(x, y, 1 - c), device_id_type=MESH) for a in range(len(g_refs))]

    def start():
        for cp in give():
            cp.start()

    def finish():
        for cp in give():
            cp.wait()

    return start, finish


def _pair_exchange(arrs, name):
    n = len(arrs)

    def body(*refs):
        for step in _pair_exchange_steps(refs[:n], refs[n:2 * n], *refs[2 * n:]):
            step()

    return pl.pallas_call(
        body, name=name,
        out_shape=[jax.ShapeDtypeStruct((4,) + t.shape[2:], t.dtype) for t in arrs],
        in_specs=[ANY] * n, out_specs=[ANY] * n,
        scratch_shapes=[pltpu.SemaphoreType.DMA((n,)), pltpu.SemaphoreType.DMA((n,))],
    )(*arrs)


def _chip_exchange_steps(p_refs, out_refs, send_sems, recv_sems):
    x, y, c = _place()
    chips = [(1 - x, y), (x, 1 - y), (1 - x, 1 - y)]

    def copies():
        return [pltpu.make_async_remote_copy(
            src_ref=p_refs[a].at[2 * px + py], dst_ref=out_refs[a].at[j], send_sem=send_sems.at[a, j],
            recv_sem=recv_sems.at[a, j], device_id=(px, py, c), device_id_type=MESH)
            for a in range(len(p_refs)) for j, (px, py) in enumerate(chips)]

    def start():
        for cp in copies():
            cp.start()

    def finish():
        for cp in copies():
            cp.wait()

    return start, finish


def _small_gather_steps(x_refs, out_refs, send_sems, recv_sems, local_sems):
    me = 4 * lax.axis_index("x") + 2 * lax.axis_index("y") + lax.axis_index("c")
    start, relay, finish = _gather_steps(x_refs, out_refs, send_sems, recv_sems)

    def mine():
        return [pltpu.make_async_copy(x_refs[a], out_refs[a].at[me], local_sems.at[a]) for a in range(len(x_refs))]

    def start_all():
        for cp in mine():
            cp.start()
        start()

    def finish_all():
        finish()
        for cp in mine():
            cp.wait()

    return start_all, relay, finish_all


def _small_gather_scratch(k):
    return [pltpu.SemaphoreType.DMA((k, 7)), pltpu.SemaphoreType.DMA((k, 7)), pltpu.SemaphoreType.DMA((k,))]


def _chip_exchange(arrs, name, gather=()):
    n, k = len(arrs), len(gather)

    def body(*refs):
        ins, outs, sems = refs[:n + k], refs[n + k:2 * (n + k)], refs[2 * (n + k):]
        start, finish = _chip_exchange_steps(ins[:n], outs[:n], *sems[:2])
        if k:
            g_start, g_relay, g_finish = _small_gather_steps(ins[n:], outs[n:], *sems[2:])
            g_start()
        start()
        if k:
            g_relay()
        finish()
        if k:
            g_finish()

    return pl.pallas_call(
        body, name=name,
        out_shape=[jax.ShapeDtypeStruct((3,) + t.shape[1:], t.dtype) for t in arrs]
        + [jax.ShapeDtypeStruct((N_DEV,) + t.shape, t.dtype) for t in gather],
        in_specs=[ANY] * (n + k), out_specs=[ANY] * (n + k),
        scratch_shapes=[pltpu.SemaphoreType.DMA((n, 3)), pltpu.SemaphoreType.DMA((n, 3))]
        + (_small_gather_scratch(k) if k else []),
    )(*arrs, *gather)


class _Rider:
    def __init__(self, arrays, out_shape, sems, steps, relay_at=()):
        self.arrays, self.out_shape, self.scratch, self.steps = list(arrays), out_shape, sems, steps
        self.n = len(self.arrays)
        self.relay_at = relay_at

    def specs(self, in_specs, out_specs, out_shape, scratch):
        extra = [ANY] * self.n
        return in_specs + extra, out_specs + extra, out_shape + self.out_shape, scratch + self.scratch

    def split(self, refs, n_in, n_out):
        k = self.n
        a, b = n_in + k, n_in + k + n_out
        return refs[:n_in] + refs[a:b] + refs[b + k:-2], (refs[n_in:a], refs[b:b + k], refs[-2:])

    def head(self, mine, step):
        pl.when(step == 0)(self.steps(mine[0], mine[1], *mine[2])[0])

    def tail(self, mine, step, nsteps):
        steps = self.steps(mine[0], mine[1], *mine[2])
        for relay, frac in zip(steps[1:-1], self.relay_at):
            pl.when(step == min(int(frac * nsteps), nsteps - 1))(relay)
        pl.when(step == nsteps - 1)(steps[-1])


def _gather_rider(arrs, relay_at=(0.5, 0.78)):
    n = len(arrs)
    return _Rider(arrs, [jax.ShapeDtypeStruct((N_DEV,) + t.shape, t.dtype) for t in arrs],
                  [pltpu.SemaphoreType.DMA((n, 7)), pltpu.SemaphoreType.DMA((n, 7))], _gather_tree_steps,
                  relay_at)


def _pair_exchange_rider(arrs):
    n = len(arrs)
    return _Rider(arrs, [jax.ShapeDtypeStruct((4,) + t.shape[2:], t.dtype) for t in arrs],
                  [pltpu.SemaphoreType.DMA((n,)), pltpu.SemaphoreType.DMA((n,))], _pair_exchange_steps)


def _chip_exchange_rider(arrs):
    n = len(arrs)
    return _Rider(arrs, [jax.ShapeDtypeStruct((3,) + t.shape[1:], t.dtype) for t in arrs],
                  [pltpu.SemaphoreType.DMA((n, 3)), pltpu.SemaphoreType.DMA((n, 3))], _chip_exchange_steps)


def _rope_tables(positions):
    inv_freq = ROPE_THETA ** (-jnp.arange(0, HEAD_DIM, 2, dtype=F32) / HEAD_DIM)
    ang = positions.astype(F32)[:, None] * inv_freq
    cos, sin = jnp.cos(ang), jnp.sin(ang)
    return jnp.tile(cos, (1, 4)), jnp.tile(jnp.concatenate([-sin, sin], axis=1), (1, 2))


class _GradReducer:
    def __init__(self):
        self.core = lax.axis_index("c").astype(jnp.int32).reshape(1)
        self.own, self.others, self.waiting, self.riding = {}, {}, [], []

    def pair(self, named):
        mine = self._split(named)
        self._summed(mine, _pair_exchange(list(mine.values()), "reduce_pair_" + next(iter(named))))

    def pair_rider(self, named):
        self.pairing = self._split(named)
        return _pair_exchange_rider(list(self.pairing.values()))

    def pair_landed(self, results):
        self._summed(self.pairing, results)

    @staticmethod
    def _split(named):
        return {k: g.reshape((4, 2) + g.shape[1:]) for k, g in named.items()}

    def _summed(self, mine, theirs):
        for (k, g), r in zip(mine.items(), theirs):
            self.own[k] = _sum_pair(self.core, g, r, "sum_pair_" + k)
        self.waiting += list(mine)

    def rider(self):
        self.riding, self.waiting = self.waiting, []
        return _chip_exchange_rider([self.own[k] for k in self.riding])

    def landed(self, results):
        self.others.update(zip(self.riding, results))

    def flush(self, name, gather=()):
        keys, self.waiting = self.waiting, []
        res = _chip_exchange([self.own[k] for k in keys], name, gather=gather)
        self.others.update(zip(keys, res))
        return res[len(keys):]


def _by_owner(g):
    if g.ndim == 3:
        return g if g.shape[0] == N_DEV else g.reshape(N_DEV, g.shape[1] * g.shape[0] // N_DEV, g.shape[2])
    return g.reshape(N_DEV, g.shape[0] // N_DEV, g.shape[1])


def _local_step(x, target, positions, mod, small, W, late=None, red=None):
    S, D = x.shape
    gains =jnp.stack([small["g1"], small["g2"], small["g3"]])[:, None, :]
    v1, v2, v3 = jnp.pad(jnp.concatenate([gains, mod.reshape(3, 3, D)], axis=1), ((0, 0), (0, 4), (0, 0)))
    vf = jnp.pad(small["gf"][None], ((0, 7), (0, 0)))
    cos, sin = _rope_tables(positions)
    wbd = jax.scipy.linalg.block_diag(*[small["w_pool"][k] for k in range(4)]).astype(BF16)
    pscale = small["pool_scale"].reshape(1, GW)

    if late is None:
        h1, u1, ab1, act1, f1 = _ffn_fwd(x, v1, W["w1in"], W["w1out"], "ffn1_fwd")
    else:
        (h1, u1, ab1, act1, f1), landed = _ffn_fwd(x, v1, W["w1in"], W["w1out"], "ffn1_fwd", rider=late[0])
        W = {**W, **late[1](landed)}
    u2, p, gates, *qkv = _mix_in_fwd(h1, v2, cos, sin, W["win"], "mix_in_fwd")
    dpool, ypool = _pool_fwd(p, wbd, pscale, "pool_fwd")
    o3, lse3 = [], []
    for gi in range(len(DILATIONS)):
        o, lse = _attn_fwd(qkv[gi], qkv[3 + gi], qkv[6 + gi], f"attn_fwd_{gi}")
        o3.append(o)
        lse3.append(lse)
    h2, ya, merged, tm = _mix_out_fwd(h1, v2, gates, ypool, o3, lse3, W["wpb"], W["wab"], W["wout"], "mix_out_fwd")
    dh3, u3, ab3, act3, f3, loss_blk, dgf = _ffn_fwd(h2, v3, W["w2in"], W["w2out"], "ffn2_fwd", final=(target, vf))

    dh2, dab3, df3, red3 = _ffn_bwd(dh3, h2, f3, ab3, v3, W["w2in"], W["w2out"], "ffn2_bwd")
    half_f = ab3.shape[2] // 2
    G = {"w2in": _by_owner(_wgrad(dab3, u3, "wgrad_2in", tm=half_f))}
    mix_out_args = (dh2, tm, v2, gates, ypool, o3, lse3, W["wpb"], W["wab"], W["wout"], "mix_out_bwd")
    if red is None:
        G["w2out"] = _by_owner(_wgrad(act3, df3, "wgrad_2out", tm=half_f))
        mix_out = _mix_out_bwd(*mix_out_args)
    else:
        g2out, landed = _wgrad(act3, df3, "wgrad_2out", tm=half_f, rider=red.pair_rider({"w2in": G["w2in"]}))
        red.pair_landed(landed)
        G["w2out"] = _by_owner(g2out)
        red.pair({"w2out": G["w2out"]})
        mix_out, landed = _mix_out_bwd(*mix_out_args, rider=red.rider())
        red.landed(landed)
    (dtm, dgl, dypb, dyab, dyp, do0, do1, do2, c0, c1, c2, red2o) = mix_out
    dq3, dk3, dv3 = [], [], []
    for gi, (do, ct) in enumerate(zip((do0, do1, do2), (c0, c1, c2))):
        dq, dk, dv = _attn_bwd(qkv[gi], qkv[3 + gi], qkv[6 + gi], do, lse3[gi], ct, f"attn_bwd_{gi}")
        dq3.append(dq)
        dk3.append(dk)
        dv3.append(dv)
    dp, dwbd, dps = _pool_bwd(dyp, dpool, wbd, pscale, "pool_bwd")
    dh1, dproj, red2i = _mix_in_bwd(dh2, h1, v2, cos, sin, dp, dq3 + dk3 + dv3, dgl, W["win"], "mix_in_bwd")
    G["win"] = _by_owner(_wgrad(dproj, u2, "wgrad_in", tm=1152))
    G["wpb"] = _full_to_cols(_wgrad(ypool, dypb, "wgrad_pb"))
    G["wab"] = _full_to_cols(_wgrad(ya, dyab, "wgrad_ab"))
    if red is None:
        G["wout"] = _by_owner(_wgrad(merged, dtm, "wgrad_out"))
    else:
        gout, landed = _wgrad(merged, dtm, "wgrad_out", rider=red.pair_rider({k: G[k] for k in ("win", "wpb", "wab")}))
        red.pair_landed(landed)
        G["wout"] = _by_owner(gout)
        red.pair({"wout": G["wout"]})
    dx, dab1, df1, red1 = _ffn_bwd(dh1, x, f1, ab1, v1, W["w1in"], W["w1out"], "ffn1_bwd")
    dmod = jnp.concatenate([red1[:3], (red2i + red2o)[:3], red3[:3]])
    dsmall = {
        "g1": red1[3], "g2": red2i[3], "g3": red3[3], "gf": dgf[0],
        "w_pool": jnp.stack([dwbd[k * 64:(k + 1) * 64, k * 64:(k + 1) * 64] for k in range(4)]),
        "pool_scale": dps[0],
    }
    if red is None:
        G["w1in"] = _by_owner(_wgrad(dab1, u1, "wgrad_1in", tm=half_f))
        G["w1out"] = _by_owner(_wgrad(act1, df1, "wgrad_1out", tm=half_f))
        return loss_blk[0, 0], dx, G, dmod, dsmall
    g1in, landed = _wgrad(dab1, u1, "wgrad_1in", tm=half_f, rider=red.rider())
    red.landed(landed)
    red.pair({"w1in": _by_owner(g1in)})
    g1out, landed = _wgrad(act1, df1, "wgrad_1out", tm=half_f, rider=red.rider())
    red.landed(landed)
    red.pair({"w1out": _by_owner(g1out)})
    part = _pack_small(dict(b_ada=dmod, g_norm_ffn1=dsmall["g1"], g_norm_mix=dsmall["g2"], g_norm_ffn2=dsmall["g3"],
                            g_final=dsmall["gf"], w_pool=dsmall["w_pool"], pool_scale=dsmall["pool_scale"]),
                       loss_blk[0, 0])
    (parts,) = red.flush("reduce_chips_w1out", gather=[part])
    return dx, parts


SHARDED = ("w_ffn1_in", "w_ffn1_out", "w_in", "w_pool_branch", "w_attn_branch", "w_out", "w_ffn2_in", "w_ffn2_out")
TRANSPOSED = ("w_ffn1_in", "w_in", "w_ffn2_in")
FIRST = ("w_ffn1_in", "w_ffn1_out")
LATER = tuple(n for n in SHARDED if n not in FIRST)
GRAD_KEY = dict(w_ffn1_in="w1in", w_ffn1_out="w1out", w_in="win", w_pool_branch="wpb", w_attn_branch="wab",
                w_out="wout", w_ffn2_in="w2in", w_ffn2_out="w2out")


def _cols_to_full(g):
    return g.transpose(1, 0, 2).reshape(g.shape[1], N_DEV * g.shape[2])


def _full_to_cols(t):
    return t.reshape(t.shape[0], N_DEV, t.shape[1] // N_DEV).transpose(1, 0, 2)


SMALL = (("b_ada", 72), ("g_norm_ffn1", 8), ("g_norm_mix", 8), ("g_norm_ffn2", 8), ("g_final", 8),
         ("w_pool", 128), ("pool_scale", 8))


def _pack_small(vals, loss):
    rows = []
    for name, nrows in SMALL:
        t = vals[name].reshape(-1, 128)
        rows.append(jnp.pad(t, ((0, nrows - t.shape[0]), (0, 0))))
    rows.append(jnp.full((8, 128), loss, F32))
    return jnp.concatenate(rows)


def _unpack_small(slab, shapes):
    out, off = {}, 0
    for name, nrows in SMALL:
        used = 1
        for d in shapes[name]:
            used *= d
        out[name] = slab[off:off + used // 128].reshape(shapes[name])
        off += nrows
    return out, slab[off, 0]


def _as_2d(t):
    return t.reshape(-1, t.shape[-1])


def kernel(x, c, positions, w_ada, b_ada, g_norm_ffn1, w_ffn1_in, w_ffn1_out, g_norm_mix, w_in, w_pool, pool_scale, w_pool_branch, w_attn_branch, w_out, g_norm_ffn2, w_ffn2_in, w_ffn2_out, g_final, loss_target, m_w_ada, m_b_ada, m_g_norm_ffn1, m_w_ffn1_in, m_w_ffn1_out, m_g_norm_mix, m_w_in, m_w_pool, m_pool_scale, m_w_pool_branch, m_w_attn_branch, m_w_out, m_g_norm_ffn2, m_w_ffn2_in, m_w_ffn2_out, m_g_final, v_w_ada, v_b_ada, v_g_norm_ffn1, v_w_ffn1_in, v_w_ffn1_out, v_g_norm_mix, v_w_in, v_w_pool, v_pool_scale, v_w_pool_branch, v_w_attn_branch, v_w_out, v_g_norm_ffn2, v_w_ffn2_in, v_w_ffn2_out, v_g_final):
    names = ["w_ada", "b_ada", "g_norm_ffn1", "w_ffn1_in", "w_ffn1_out", "g_norm_mix", "w_in", "w_pool", "pool_scale",
             "w_pool_branch", "w_attn_branch", "w_out", "g_norm_ffn2", "w_ffn2_in", "w_ffn2_out", "g_final"]
    w = dict(w_ada=w_ada, b_ada=b_ada, g_norm_ffn1=g_norm_ffn1, w_ffn1_in=w_ffn1_in, w_ffn1_out=w_ffn1_out,
             g_norm_mix=g_norm_mix, w_in=w_in, w_pool=w_pool, pool_scale=pool_scale, w_pool_branch=w_pool_branch,
             w_attn_branch=w_attn_branch, w_out=w_out, g_norm_ffn2=g_norm_ffn2, w_ffn2_in=w_ffn2_in,
             w_ffn2_out=w_ffn2_out, g_final=g_final)
    m = dict(w_ada=m_w_ada, b_ada=m_b_ada, g_norm_ffn1=m_g_norm_ffn1, w_ffn1_in=m_w_ffn1_in, w_ffn1_out=m_w_ffn1_out,
             g_norm_mix=m_g_norm_mix, w_in=m_w_in, w_pool=m_w_pool, pool_scale=m_pool_scale,
             w_pool_branch=m_w_pool_branch, w_attn_branch=m_w_attn_branch, w_out=m_w_out, g_norm_ffn2=m_g_norm_ffn2,
             w_ffn2_in=m_w_ffn2_in, w_ffn2_out=m_w_ffn2_out, g_final=m_g_final)
    v = dict(w_ada=v_w_ada, b_ada=v_b_ada, g_norm_ffn1=v_g_norm_ffn1, w_ffn1_in=v_w_ffn1_in, w_ffn1_out=v_w_ffn1_out,
             g_norm_mix=v_g_norm_mix, w_in=v_w_in, w_pool=v_w_pool, pool_scale=v_pool_scale,
             w_pool_branch=v_w_pool_branch, w_attn_branch=v_w_attn_branch, w_out=v_w_out, g_norm_ffn2=v_g_norm_ffn2,
             w_ffn2_in=v_w_ffn2_in, w_ffn2_out=v_w_ffn2_out, g_final=v_g_final)
    shapes = {n: w[n].shape for n in names}
    me = 4 * lax.axis_index("x") + 2 * lax.axis_index("y") + lax.axis_index("c")
    D = x.shape[-1]
    n_mod = w_ada.shape[-1] * N_DEV // D

    def local(t, name):
        return t[name][0].T if name in TRANSPOSED else t[name][0]

    shards = {name: local(w, name).astype(BF16) for name in SHARDED}

    def gather_done(names, fulls):
        return {name: lax.dynamic_update_index_in_dim(full, shards[name], me, axis=0)
                for name, full in zip(names, fulls)}

    def ffn_weights(g, pre):
        return {"w%sin" % pre: g["w_ffn%s_in" % pre].reshape(2, -1, D),
                "w%sout" % pre: g["w_ffn%s_out" % pre].reshape(-1, D)}

    def later_weights(fulls):
        g = gather_done(LATER, fulls)
        return dict(win=g["w_in"].reshape(-1, D), wpb=_cols_to_full(g["w_pool_branch"]),
                    wab=_cols_to_full(g["w_attn_branch"]), wout=g["w_out"].reshape(D, D), **ffn_weights(g, "2"))

    *first, c_all = _all_gather_tree([shards[n] for n in FIRST], "gather_ffn1", gather=[c.reshape(D // 128, 128)])
    W = ffn_weights(gather_done(FIRST, first), "1")

    ada_cols = w_ada.shape[-1]
    b_mine = lax.dynamic_slice_in_dim(b_ada, me * ada_cols, ada_cols, axis=1)
    cond, mod_part = _ada_mod(c_all.reshape(N_DEV, D), w_ada[0], b_mine, "ada_mod")
    (mod_all,) = _all_gather([mod_part.reshape(-1, 128)], "gather_mod")
    mod_all = mod_all.reshape(N_DEV, N_DEV, ada_cols)
    mod = lax.dynamic_index_in_dim(mod_all, me, axis=1, keepdims=False).reshape(n_mod, D)

    small = dict(g1=g_norm_ffn1[0], g2=g_norm_mix[0], g3=g_norm_ffn2[0], gf=g_final, w_pool=w_pool[0],
                 pool_scale=pool_scale[0])
    red = _GradReducer()
    dx, parts = _local_step(
        x[0], loss_target[0], positions[0], mod, small, W,
        late=(_gather_rider([shards[n] for n in LATER]), later_weights), red=red)
    chip = (2 * lax.axis_index("x") + lax.axis_index("y")).astype(jnp.int32).reshape(1)

    gsmall, loss = _unpack_small(_sum_slots(parts, "sum_small"), shapes)
    rows_mine = ada_cols // 128
    dmod_mine = lax.dynamic_slice_in_dim(parts, me * rows_mine, rows_mine, axis=1).reshape(N_DEV, ada_cols)

    grads, delta, new_m, new_v = {}, {}, {}, {}
    grads["w_ada"], delta["w_ada"], new_m["w_ada"], new_v["w_ada"] = (
        t[None] for t in _ada_grad_adamw(cond.T, dmod_mine, w_ada[0], m_w_ada[0], v_w_ada[0], "ada_grad_adamw"))
    for name in SHARDED:
        key = GRAD_KEY[name]
        res = _sum_adamw(chip, red.own[key], red.others[key], local(w, name), local(m, name), local(v, name),
                         "adamw_" + name)
        grads[name], delta[name], new_m[name], new_v[name] = (
            (t.T if name in TRANSPOSED else t)[None] for t in res)
    small_names = [name for name, _ in SMALL]
    res = _adamw_many(*([_as_2d(t[name]) for name in small_names] for t in (w, gsmall, m, v)), "adamw_small")
    for dst, vals in zip((delta, new_m, new_v), res):
        dst.update({name: t.reshape(shapes[name]) for name, t in zip(small_names, vals)})
    grads.update(gsmall)

    return (loss, dx[None], *[grads[n] for n in names], *[delta[n] for n in names],
            *[new_m[n] for n in names], *[new_v[n] for n in names])
```

```python
import functools

import jax
import jax.numpy as jnp
from jax import lax
from jax.experimental import pallas as pl
from jax.experimental.pallas import tpu as pltpu

F32 = jnp.float32
BF16 = jnp.bfloat16
MESH = pl.DeviceIdType.MESH
ANY = pl.BlockSpec(memory_space=pl.ANY)

N_DEV = 8
EPS = 1e-6
HEAD_DIM = 64
HEADS = 4
GW = HEADS * HEAD_DIM
DILATIONS = (1, 4, 16)
BAND = 128
QB = 128
POOL_WINDOWS = (2, 4, 8, 16)
HALO = 16
ROPE_THETA = 10000.0

ADAM_LR = 0.001
ADAM_B1 = 0.9
ADAM_B2 = 0.999
ADAM_EPS = 1e-08
ADAM_WD = 0.01
ADAM_STEP = 10

VMEM_LIMIT = 56 * 1024 * 1024
TS = 512
FFN_TS = 256
FFN_CHUNKS = (2816,)

NT = (((1,), (1,)), ((), ()))
TN = (((0,), (0,)), ((), ()))


def _params(**kw):
    return pltpu.CompilerParams(vmem_limit_bytes=VMEM_LIMIT, **kw)


def _dot(a, b):
    return jnp.dot(a, b, preferred_element_type=F32)


def _dot_nt(a, b):
    return lax.dot_general(a, b, NT, preferred_element_type=F32)


def _dot_tn(a, b):
    return lax.dot_general(a, b, TN, preferred_element_type=F32)


def _load_weights(pairs, sem):
    @pl.when(pl.program_id(0) == 0)
    def _():
        copies = [pltpu.make_async_copy(src, dst, sem.at[i]) for i, (src, dst) in enumerate(pairs)]
        for cp in copies:
            cp.start()
        for cp in copies:
            cp.wait()


def _start_weights(pairs, sem):
    first = pl.program_id(0) == 0
    copies = [pltpu.make_async_copy(src, dst, sem.at[i]) for i, (src, dst) in enumerate(pairs)]

    @pl.when(first)
    def _():
        for cp in copies:
            cp.start()

    return [functools.partial(pl.when(first), cp.wait) for cp in copies]


def _norm_mod(x, g, sc, sh):
    r = lax.rsqrt(jnp.mean(x * x, axis=-1, keepdims=True) + EPS)
    xn = x * r
    y = xn * g
    return r, xn, y, y * (1.0 + sc) + sh


def _norm_mod_bwd(du, r, xn, y, g, sc):
    dsh = jnp.sum(du, axis=0, keepdims=True)
    dsc = jnp.sum(du * y, axis=0, keepdims=True)
    dy = du * (1.0 + sc)
    dg = jnp.sum(dy * xn, axis=0, keepdims=True)
    dxn = dy * g
    dx = r * (dxn - xn * jnp.mean(dxn * xn, axis=-1, keepdims=True))
    return dx, dsh, dsc, dg


def _row_tile(ts, width):
    return pl.BlockSpec((ts, width), lambda i: (i, 0))


def _const(shape):
    return pl.BlockSpec(shape, lambda *_: (0,) * len(shape))


def _ffn_chunks(Fd):
    assert sum(FFN_CHUNKS) == Fd
    edges = [sum(FFN_CHUNKS[:k]) for k in range(len(FFN_CHUNKS) + 1)]
    return [slice(a, b) for a, b in zip(edges[:-1], edges[1:])]
def _final_tile(x, g, target):
    r = lax.rsqrt(jnp.mean(x * x, axis=-1, keepdims=True) + EPS)
    xn = x * r
    err = xn * g - target
    loss = 0.5 * jnp.sum(jnp.mean(err * err, axis=-1, keepdims=True))
    dy = err * (1.0 / x.shape[-1])
    dg = jnp.sum(dy * xn, axis=0, keepdims=True)
    dxn = dy * g
    return r * (dxn - xn * jnp.mean(dxn * xn, axis=-1, keepdims=True)), loss, dg


def _ffn_fwd(h, vec, win, wout, name, rider=None, final=None):
    TS = FFN_TS
    S, D = h.shape
    _, Fd, _ = win.shape
    n_in, n_out = (6, 7) if final else (4, 5)

    def body(*refs):
        if rider is None:
            return compute(*refs)
        host, mine = rider.split(refs, n_in, n_out)
        rider.head(mine, pl.program_id(0))
        compute(*host)
        rider.tail(mine, pl.program_id(0), S // TS)

    def compute(*refs):
        h_ref, vec_ref, win_hbm, wout_hbm = refs[:4]
        hn_ref, u_ref, ab_ref, act_ref, f_ref = refs[n_in:n_in + 5]
        win_v, wout_v, sem = refs[n_in + n_out:]
        win_here, wout_here = _start_weights([(win_hbm, win_v), (wout_hbm, wout_v)], sem)
        x = h_ref[...]
        g, sh, sc, gt = (vec_ref[k:k + 1, :] for k in range(4))
        _, _, _, u = _norm_mod(x, g, sc, sh)
        ub = u.astype(BF16)
        u_ref[...] = ub
        acc = jnp.zeros((TS, D), F32)
        win_here()
        for k, sl in enumerate(_ffn_chunks(Fd)):
            a = _dot_nt(ub, win_v[0, sl, :])
            b = _dot_nt(ub, win_v[1, sl, :])
            act = ((a * jax.nn.sigmoid(a)) * b).astype(BF16)
            ab_ref[0, :, sl] = a.astype(BF16)
            ab_ref[1, :, sl] = b.astype(BF16)
            act_ref[:, sl] = act
            if k == 0:
                wout_here()
            acc = acc + _dot(act, wout_v[sl, :])
        f_ref[...] = acc.astype(BF16)
        hn = x + (0.5 * gt) * acc
        if not final:
            hn_ref[...] = hn
            return
        t_ref, gf_ref = refs[4:6]
        loss_ref, dgf_ref = refs[n_in + 5:n_in + 7]

        @pl.when(pl.program_id(0) == 0)
        def _():
            loss_ref[...] = jnp.zeros_like(loss_ref)
            dgf_ref[...] = jnp.zeros_like(dgf_ref)

        hn_ref[...], loss, dg = _final_tile(hn, gf_ref[0:1, :], t_ref[...])
        loss_ref[...] += loss
        dgf_ref[0:1, :] += dg

    specs = (
        [_row_tile(TS, D), _const((8, D)), ANY, ANY] + ([_row_tile(TS, D), _const((8, D))] if final else []),
        [_row_tile(TS, D), _row_tile(TS, D), pl.BlockSpec((2, TS, Fd), lambda i: (0, i, 0)),
         _row_tile(TS, Fd), _row_tile(TS, D)] + ([_const((8, 128)), _const((8, D))] if final else []),
        [jax.ShapeDtypeStruct((S, D), F32), jax.ShapeDtypeStruct((S, D), BF16),
         jax.ShapeDtypeStruct((2, S, Fd), BF16), jax.ShapeDtypeStruct((S, Fd), BF16),
         jax.ShapeDtypeStruct((S, D), BF16)]
        + ([jax.ShapeDtypeStruct((8, 128), F32), jax.ShapeDtypeStruct((8, D), F32)] if final else []),
        [pltpu.VMEM(win.shape, BF16), pltpu.VMEM(wout.shape, BF16), pltpu.SemaphoreType.DMA((2,))])
    in_specs, out_specs, out_shape, scratch = specs if rider is None else rider.specs(*specs)
    outs = pl.pallas_call(
        body, name=name, grid=(S // TS,), in_specs=in_specs, out_specs=out_specs, out_shape=out_shape,
        scratch_shapes=scratch, compiler_params=_params(),
    )(h, vec, win, wout, *(final or ()), *(rider.arrays if rider else []))
    return outs if rider is None else (outs[:n_out], outs[n_out:])


def _ffn_bwd(dh, h, f, ab, vec, win, wout, name):
    TS = FFN_TS
    S, D = h.shape
    _, Fd, _ = win.shape

    def body(dh_ref, h_ref, f_ref, ab_ref, vec_ref, win_hbm, wout_hbm,
             dhp_ref, dab_ref, df_ref, red_ref, win_v, wout_v, sem):
        win_here, wout_here = _start_weights([(win_hbm, win_v), (wout_hbm, wout_v)], sem)

        @pl.when(pl.program_id(0) == 0)
        def _():
            red_ref[...] = jnp.zeros_like(red_ref)

        dh_v = dh_ref[...]
        x = h_ref[...]
        g, sh, sc, gt = (vec_ref[k:k + 1, :] for k in range(4))
        dgt = jnp.sum((0.5 * f_ref[...].astype(F32)) * dh_v, axis=0, keepdims=True)
        dfb = ((0.5 * gt) * dh_v).astype(BF16)
        df_ref[...] = dfb
        du = jnp.zeros((TS, D), F32)
        wout_here()
        for k, sl in enumerate(_ffn_chunks(Fd)):
            dact = _dot_nt(dfb, wout_v[sl, :])
            av = ab_ref[0, :, sl].astype(F32)
            bv = ab_ref[1, :, sl].astype(F32)
            sg = jax.nn.sigmoid(av)
            da = (dact * bv * (sg * (1.0 + av * (1.0 - sg)))).astype(BF16)
            db = (dact * (av * sg)).astype(BF16)
            dab_ref[0, :, sl] = da
            dab_ref[1, :, sl] = db
            if k == 0:
                win_here()
            du = du + _dot(da, win_v[0, sl, :]) + _dot(db, win_v[1, sl, :])
        r, xn, y, _ = _norm_mod(x, g, sc, sh)
        dx, dsh, dsc, dg = _norm_mod_bwd(du, r, xn, y, g, sc)
        dhp_ref[...] = dh_v + dx
        red_ref[0:1, :] += dsh
        red_ref[1:2, :] += dsc
        red_ref[2:3, :] += dgt
        red_ref[3:4, :] += dg

    ab_spec = pl.BlockSpec((2, TS, Fd), lambda i: (0, i, 0))
    return pl.pallas_call(
        body, name=name, grid=(S // TS,),
        in_specs=[_row_tile(TS, D), _row_tile(TS, D), _row_tile(TS, D), ab_spec, _const((8, D)), ANY, ANY],
        out_specs=[_row_tile(TS, D), ab_spec, _row_tile(TS, D), _const((8, D))],
        out_shape=[jax.ShapeDtypeStruct((S, D), F32), jax.ShapeDtypeStruct((2, S, Fd), BF16),
                   jax.ShapeDtypeStruct((S, D), BF16), jax.ShapeDtypeStruct((8, D), F32)],
        scratch_shapes=[pltpu.VMEM(win.shape, BF16), pltpu.VMEM(wout.shape, BF16), pltpu.SemaphoreType.DMA((2,))],
        compiler_params=_params(),
    )(dh, h, f, ab, vec, win, wout)


def _wgrad(x, y, name, tm=None, ts=2048, rider=None):
    xb = x.ndim == 3
    nb = x.shape[0] if xb else 0
    S, M = x.shape[-2:]
    N = y.shape[-1]
    tm = tm or M
    ts = min(ts, S)
    nk = S // ts
    grid = (max(nb, 1), M // tm, nk)

    def body(*refs):
        if rider is None:
            return compute(*refs)
        host, mine = rider.split(refs, 2, 1)
        step = (pl.program_id(0) * grid[1] + pl.program_id(1)) * grid[2] + pl.program_id(2)
        rider.head(mine, step)
        compute(*host)
        rider.tail(mine, step, grid[0] * grid[1] * grid[2])

    def compute(x_ref, y_ref, o_ref, acc):
        k = pl.program_id(2)

        @pl.when(k == 0)
        def _():
            acc[...] = jnp.zeros_like(acc)

        acc[...] += _dot_tn(x_ref[...], y_ref[...])

        @pl.when(k == nk - 1)
        def _():
            o_ref[...] = acc[...].astype(BF16)

    x_spec = (pl.BlockSpec((None, ts, tm), lambda b, i, k: (b, k, i)) if xb
              else pl.BlockSpec((ts, tm), lambda b, i, k: (k, i)))
    y_spec = pl.BlockSpec((ts, N), lambda b, i, k: (k, 0))
    if xb:
        o_spec, o_shape = pl.BlockSpec((None, tm, N), lambda b, i, k: (b, i, 0)), (nb, M, N)
    else:
        o_spec, o_shape = pl.BlockSpec((tm, N), lambda b, i, k: (i, 0)), (M, N)
    specs = ([x_spec, y_spec], [o_spec], [jax.ShapeDtypeStruct(o_shape, BF16)], [pltpu.VMEM((tm, N), F32)])
    in_specs, out_specs, out_shape, scratch = specs if rider is None else rider.specs(*specs)
    outs = pl.pallas_call(
        body, name=name, grid=grid, in_specs=in_specs, out_specs=out_specs, out_shape=out_shape,
        scratch_shapes=scratch, compiler_params=_params(),
    )(x, y, *(rider.arrays if rider else []))
    return outs[0] if rider is None else (outs[0], outs[1:])


P_OFF, Q_OFF, K_OFF, V_OFF, G_OFF = 0, 256, 1024, 1792, 2560
IN_WIDTH = 4608


def _first_half_mask(ts):
    lane = lax.broadcasted_iota(jnp.int32, (ts, 128), 1)
    return (lane % HEAD_DIM) < (HEAD_DIM // 2)


def _rope(t, cos, sin_signed, first, sign):
    partner = jnp.where(first, pltpu.roll(t, 96, 1), pltpu.roll(t, 32, 1))
    return t * cos + sign * (partner * sin_signed)


def _res_spec(r):
    return pl.BlockSpec((r, TS // r, GW), lambda i: (0, i, 0))


def _res_shape(S, r, dtype):
    return jax.ShapeDtypeStruct((r, S // r, GW), dtype)


def _to_residues(piece, out_ref, lanes, r, scr):
    if r == 1:
        out_ref[0, :, lanes] = piece.astype(out_ref.dtype)
        return
    for h in range(piece.shape[1] // 128):
        scr[h] = piece[:, h * 128:(h + 1) * 128]
        at = slice(lanes.start + h * 128, lanes.start + (h + 1) * 128)
        for res in range(r):
            out_ref[res, :, at] = scr[h, pl.ds(res, TS // r, stride=r), :].astype(out_ref.dtype)


def _from_residues(in_ref, lanes, r, scr):
    if r == 1:
        return in_ref[0, :, lanes].astype(F32)
    halves = (lanes.stop - lanes.start) // 128
    for h in range(halves):
        at = slice(lanes.start + h * 128, lanes.start + (h + 1) * 128)
        for res in range(r):
            scr[h, pl.ds(res, TS // r, stride=r), :] = in_ref[res, :, at].astype(F32)
    return scr[0] if halves == 1 else jnp.concatenate([scr[0], scr[1]], axis=1)


RES_SCRATCH = (2, TS, 128)


def _mix_in_fwd(h, vec, cos, sin, win, name):
    S, D = h.shape

    def body(h_ref, vec_ref, cos_ref, sin_ref, win_hbm, u_ref, p_ref, gates_ref, *rest):
        qkv_refs, (win_v, sem, scr) = rest[:9], rest[9:]
        _load_weights([(win_hbm, win_v)], sem)
        g, sh, sc = (vec_ref[k:k + 1, :] for k in range(3))
        _, _, _, u = _norm_mod(h_ref[...], g, sc, sh)
        ub = u.astype(BF16)
        u_ref[...] = ub
        p_ref[...] = _dot_nt(ub, win_v[P_OFF:Q_OFF, :])
        cosv, sinv = cos_ref[...], sin_ref[...]
        first = _first_half_mask(TS)
        for which, off in enumerate((Q_OFF, K_OFF, V_OFF)):
            t = _dot_nt(ub, win_v[off:off + 3 * GW, :])
            for gi in range(3):
                for half in range(2):
                    c0 = gi * GW + half * 128
                    piece = t[:, c0:c0 + 128]
                    if which < 2:
                        piece = _rope(piece, cosv, sinv, first, 1.0)
                    _to_residues(piece, qkv_refs[which * 3 + gi], slice(half * 128, (half + 1) * 128),
                                 DILATIONS[gi], scr)
        gates_ref[...] = jax.nn.sigmoid(_dot_nt(ub, win_v[G_OFF:IN_WIDTH, :])).astype(BF16)

    return pl.pallas_call(
        body, name=name, grid=(S // TS,),
        in_specs=[_row_tile(TS, D), _const((8, D)), _row_tile(TS, 128), _row_tile(TS, 128), ANY],
        out_specs=[_row_tile(TS, D), _row_tile(TS, GW), _row_tile(TS, 2 * D)] + [_res_spec(r) for r in DILATIONS] * 3,
        out_shape=[jax.ShapeDtypeStruct((S, D), BF16), jax.ShapeDtypeStruct((S, GW), F32),
                   jax.ShapeDtypeStruct((S, 2 * D), BF16)] + [_res_shape(S, r, BF16) for r in DILATIONS] * 3,
        scratch_shapes=[pltpu.VMEM((IN_WIDTH, D), BF16), pltpu.SemaphoreType.DMA((1,)), pltpu.VMEM(RES_SCRATCH, F32)],
        compiler_params=_params(),
    )(h, vec, cos, sin, win)


def _mix_in_bwd(dh, h, vec, cos, sin, dp, dqkv, dgl, win, name):
    S, D = h.shape

    def body(dh_ref, h_ref, vec_ref, cos_ref, sin_ref, dp_ref, *rest):
        dqkv_refs = rest[:9]
        dgl_ref, win_hbm, dhp_ref, dproj_ref, red_ref, win_v, sem, scr = rest[9:]
        _load_weights([(win_hbm, win_v)], sem)

        @pl.when(pl.program_id(0) == 0)
        def _():
            red_ref[...] = jnp.zeros_like(red_ref)

        cosv, sinv = cos_ref[...], sin_ref[...]
        first = _first_half_mask(TS)
        dproj_ref[:, P_OFF:Q_OFF] = dp_ref[...].astype(BF16)
        for which, off in enumerate((Q_OFF, K_OFF, V_OFF)):
            for gi in range(3):
                for half in range(2):
                    piece = _from_residues(dqkv_refs[which * 3 + gi], slice(half * 128, (half + 1) * 128),
                                           DILATIONS[gi], scr)
                    if which < 2:
                        piece = _rope(piece, cosv, sinv, first, -1.0)
                    c0 = off + gi * GW + half * 128
                    dproj_ref[:, c0:c0 + 128] = piece.astype(BF16)
        dproj_ref[:, G_OFF:IN_WIDTH] = dgl_ref[...]
        du = _dot(dproj_ref[...], win_v[...])
        g, sh, sc = (vec_ref[k:k + 1, :] for k in range(3))
        r, xn, y, _ = _norm_mod(h_ref[...], g, sc, sh)
        dx, dsh, dsc, dg = _norm_mod_bwd(du, r, xn, y, g, sc)
        dhp_ref[...] = dh_ref[...] + dx
        red_ref[0:1, :] += dsh
        red_ref[1:2, :] += dsc
        red_ref[3:4, :] += dg

    return pl.pallas_call(
        body, name=name, grid=(S // TS,),
        in_specs=[_row_tile(TS, D), _row_tile(TS, D), _const((8, D)), _row_tile(TS, 128), _row_tile(TS, 128),
                  _row_tile(TS, GW)] + [_res_spec(r) for r in DILATIONS] * 3 + [_row_tile(TS, 2 * D), ANY],
        out_specs=[_row_tile(TS, D), _row_tile(TS, IN_WIDTH), _const((8, D))],
        out_shape=[jax.ShapeDtypeStruct((S, D), F32), jax.ShapeDtypeStruct((S, IN_WIDTH), BF16),
                   jax.ShapeDtypeStruct((8, D), F32)],
        scratch_shapes=[pltpu.VMEM((IN_WIDTH, D), BF16), pltpu.SemaphoreType.DMA((1,)), pltpu.VMEM(RES_SCRATCH, F32)],
        compiler_params=_params(),
    )(dh, h, vec, cos, sin, dp, *dqkv, dgl, win)


def _pool_lanes(rows):
    lane = lax.broadcasted_iota(jnp.int32, (rows, GW), 1)
    return lane // HEAD_DIM


def _pool_window(rows):
    grp = _pool_lanes(rows)
    w = jnp.full((rows, GW), POOL_WINDOWS[0], jnp.int32)
    for k in range(1, len(POOL_WINDOWS)):
        w = jnp.where(grp == k, POOL_WINDOWS[k], w)
    return grp, w


def _pool_fwd(p, wbd, scale, name, ts=1024):
    S = p.shape[0]
    ext = ts + HALO

    def body(pc_ref, ph_ref, wbd_ref, sc_ref, d_ref, y_ref):
        i = pl.program_id(0)
        cur = pc_ref[...]
        halo = jnp.where(i > 0, ph_ref[...], 0.0)
        s = jnp.concatenate([halo, cur], axis=0)
        grp, w = _pool_window(ext)
        sel = jnp.zeros((ext, GW), F32)
        for k, wk in enumerate(POOL_WINDOWS):
            s = s + pltpu.roll(s, wk // 2, 0)
            sel = jnp.where(grp == k, s, sel)
        t = i * ts + lax.broadcasted_iota(jnp.int32, (ts, GW), 0)
        count = jnp.minimum(t + 1, w[HALO:]).astype(F32)
        d = (sel[HALO:] / count - cur).astype(BF16)
        d_ref[...] = d
        y_ref[...] = (_dot(d, wbd_ref[...]) * sc_ref[...]).astype(BF16)

    return pl.pallas_call(
        body, name=name, grid=(S // ts,),
        in_specs=[_row_tile(ts, GW),
                  pl.BlockSpec((HALO, GW), lambda i: (jnp.maximum(i * (ts // HALO) - 1, 0), 0)),
                  _const((GW, GW)), _const((1, GW))],
        out_specs=[_row_tile(ts, GW), _row_tile(ts, GW)],
        out_shape=[jax.ShapeDtypeStruct((S, GW), BF16), jax.ShapeDtypeStruct((S, GW), BF16)],
        compiler_params=_params(),
    )(p, p, wbd, scale)


def _pool_bwd(dy, d, wbd, scale, name, ts=1024):
    S = dy.shape[0]
    ext = ts + HALO
    nsteps = S // ts
    last_halo = S // HALO - 1

    def body(dyc_ref, dyh_ref, d_ref, wbd_ref, sc_ref, dp_ref, dw_ref, ds_ref):
        i = pl.program_id(0)

        @pl.when(i == 0)
        def _():
            dw_ref[...] = jnp.zeros_like(dw_ref)
            ds_ref[...] = jnp.zeros_like(ds_ref)

        dyc = dyc_ref[...]
        dyh = jnp.where(i < nsteps - 1, dyh_ref[...], 0.0)
        dys = (jnp.concatenate([dyc, dyh], axis=0) * sc_ref[...]).astype(BF16)
        dd = _dot_nt(dys, wbd_ref[...])
        grp, w = _pool_window(ext)
        t = i * ts + lax.broadcasted_iota(jnp.int32, (ext, GW), 0)
        s = dd / jnp.minimum(t + 1, w).astype(F32)
        sel = jnp.zeros((ext, GW), F32)
        for k, wk in enumerate(POOL_WINDOWS):
            s = s + pltpu.roll(s, ext - wk // 2, 0)
            sel = jnp.where(grp == k, s, sel)
        dp_ref[...] = sel[:ts] - dd[:ts]
        dv = d_ref[...]
        z = _dot(dv, wbd_ref[...])
        ds_ref[0:1, :] += jnp.sum(dyc * z, axis=0, keepdims=True)
        dw_ref[...] += _dot_tn(dv, dys[:ts])

    return pl.pallas_call(
        body, name=name, grid=(nsteps,),
        in_specs=[_row_tile(ts, GW),
                  pl.BlockSpec((HALO, GW), lambda i: (jnp.minimum((i + 1) * (ts // HALO), last_halo), 0)),
                  _row_tile(ts, GW), _const((GW, GW)), _const((1, GW))],
        out_specs=[_row_tile(ts, GW), _const((GW, GW)), _const((8, GW))],
        out_shape=[jax.ShapeDtypeStruct((S, GW), F32), jax.ShapeDtypeStruct((GW, GW), F32),
                   jax.ShapeDtypeStruct((8, GW), F32)],
        compiler_params=_params(),
    )(dy, dy, d, wbd, scale)


def _head_id(rows):
    return lax.broadcasted_iota(jnp.int32, (rows, GW), 1) // HEAD_DIM


def _stack_heads(t, hid):
    return jnp.concatenate([jnp.where(hid == h, t, jnp.zeros_like(t)) for h in range(HEADS)], axis=0)


def _unstack_heads(t_all, hid):
    out = jnp.zeros((QB, GW), F32)
    for h in range(HEADS):
        out = jnp.where(hid == h, t_all[h * QB:(h + 1) * QB], out)
    return out


def _band_masks():
    row = lax.broadcasted_iota(jnp.int32, (HEADS * QB, 2 * QB), 0) % QB
    col = lax.broadcasted_iota(jnp.int32, (HEADS * QB, 2 * QB), 1)
    rel = row + QB - col
    band = (rel >= 0) & (rel <= BAND)
    return band, band & (col >= QB)


def _stream_mask(masks, n, off):
    band, first = masks
    return band if off > 0 else first | (band & (n > 0))


FWD_STREAMS, BWD_STREAMS = 16, 8


def _streams(r, nb, most):
    if r > 1:
        ns = min(r, most)
        return nb, [(lambda rb, l=l: ns * rb + l, 0) for l in range(ns)]
    ns = min(most, nb)
    return nb // ns, [(lambda rb: 0, l * (nb // ns)) for l in range(ns)]


def _attn_fwd(q, k, v, name):
    r, L, _ = q.shape
    nbs, streams = _streams(r, L // QB, FWD_STREAMS)
    ns = len(streams)
    grid = (max(r // ns, 1), nbs)

    def cur(res, off):
        return pl.BlockSpec((None, QB, GW), lambda rb, n: (res(rb), n + off, 0))

    def prev(res, off):
        return pl.BlockSpec((None, QB, GW), lambda rb, n: (res(rb), jnp.maximum(n + off - 1, 0), 0))

    def body(*refs):
        n = pl.program_id(1)
        hid = _head_id(QB)
        masks = _band_masks()
        o_ref, lse_ref = refs[5 * len(streams):]
        for l, (_, off) in enumerate(streams):
            q_ref, kp_ref, kc_ref, vp_ref, vc_ref = refs[5 * l:5 * l + 5]
            qs = _stack_heads(q_ref[...], hid)
            kc = jnp.concatenate([kp_ref[...], kc_ref[...]], axis=0)
            vc = jnp.concatenate([vp_ref[...], vc_ref[...]], axis=0)
            s = _dot_nt(qs, kc) * (HEAD_DIM ** -0.5)
            s = jnp.where(_stream_mask(masks, n, off), s, -jnp.inf)
            m = jnp.max(s, axis=-1, keepdims=True)
            e = jnp.exp(s - m)
            den = jnp.sum(e, axis=-1, keepdims=True)
            lse = m + jnp.log(den)
            pr = (e * (1.0 / den)).astype(BF16)
            o_ref[l] = _unstack_heads(_dot(pr, vc), hid).astype(BF16)
            lse_ref[l] = _unstack_heads(jnp.broadcast_to(lse, (HEADS * QB, GW)), hid)

    in_specs, args = [], []
    for res, off in streams:
        in_specs += [cur(res, off), prev(res, off), cur(res, off), prev(res, off), cur(res, off)]
        args += [q, k, k, v, v]
    out = jax.ShapeDtypeStruct((ns * grid[0], nbs * QB, GW), F32)
    both = pl.BlockSpec((ns, QB, GW), lambda rb, n: (rb, n, 0))
    o, lse = pl.pallas_call(
        body, name=name, grid=grid, in_specs=in_specs, out_specs=[both, both],
        out_shape=[jax.ShapeDtypeStruct(out.shape, BF16), out],
        compiler_params=_params(),
    )(*args)
    return o.reshape(q.shape), lse.reshape(q.shape)


def _head_rows(t_full, hid):
    del hid
    return jnp.concatenate([t_full[:, h * HEAD_DIM:h * HEAD_DIM + 1] for h in range(HEADS)], axis=0)


def _attn_bwd(q, k, v, do, lse, cterm, name):
    r, L, _ = q.shape
    nbs, streams = _streams(r, L // QB, BWD_STREAMS)
    ns = len(streams)
    parts = r == 1

    def spec(res, index):
        return pl.BlockSpec((None, QB, GW), lambda rb, n: (res(rb), index(n), 0))

    def body(*refs):
        dq_ref, dk_ref, dv_ref, carry_k, carry_v, seam_k, seam_v = refs[8 * ns:]
        n = pl.program_id(1)

        @pl.when(n == 0)
        def _():
            carry_k[...] = jnp.zeros_like(carry_k)
            carry_v[...] = jnp.zeros_like(carry_v)

        @pl.when(n < nbs)
        def _():
            hid = _head_id(QB)
            masks = _band_masks()
            for l, (_, off) in enumerate(streams):
                q_ref, do_ref, lse_ref, c_ref, kp_ref, kc_ref, vp_ref, vc_ref = refs[8 * l:8 * l + 8]
                qs = _stack_heads(q_ref[...], hid)
                dos = _stack_heads(do_ref[...], hid)
                kc = jnp.concatenate([kp_ref[...], kc_ref[...]], axis=0)
                vc = jnp.concatenate([vp_ref[...], vc_ref[...]], axis=0)
                s = _dot_nt(qs, kc) * (HEAD_DIM ** -0.5)
                s = jnp.where(_stream_mask(masks, n, off), s, -jnp.inf)
                p = jnp.exp(s - _head_rows(lse_ref[...], hid))
                dp = _dot_nt(dos, vc)
                ds = (p * (dp + _head_rows(c_ref[...], hid)) * (HEAD_DIM ** -0.5)).astype(BF16)
                dq_ref[l] = _unstack_heads(_dot(ds, kc), hid).astype(BF16)
                dkc = _dot_tn(ds, qs)
                dvc = _dot_tn(p.astype(BF16), dos)
                if parts and l > 0:
                    @pl.when(n == 0)
                    def _():
                        seam_k[l] = dkc[:QB]
                        seam_v[l] = dvc[:QB]
                dk_ref[l] = (carry_k[l] + dkc[:QB]).astype(BF16)
                dv_ref[l] = (carry_v[l] + dvc[:QB]).astype(BF16)
                carry_k[l] = dkc[QB:]
                carry_v[l] = dvc[QB:]

        @pl.when(n == nbs)
        def _():
            for l in range(ns):
                if parts and l + 1 < ns:
                    dk_ref[l] = (carry_k[l] + seam_k[l + 1]).astype(BF16)
                    dv_ref[l] = (carry_v[l] + seam_v[l + 1]).astype(BF16)
                else:
                    dk_ref[l] = carry_k[l].astype(BF16)
                    dv_ref[l] = carry_v[l].astype(BF16)

    in_specs, args = [], []
    for res, off in streams:
        qside = functools.partial(lambda n, off: jnp.minimum(n, nbs - 1) + off, off=off)
        kprev = functools.partial(lambda n, off: jnp.maximum(jnp.minimum(n, nbs) - 1 + off, 0), off=off)
        in_specs += [spec(res, qside)] * 4 + [spec(res, kprev), spec(res, qside)] * 2
        args += [q, do, lse, cterm, k, k, v, v]
    out = jax.ShapeDtypeStruct((ns * max(r // ns, 1), nbs * QB, GW), BF16)
    qout = pl.BlockSpec((ns, QB, GW), lambda rb, n: (rb, jnp.minimum(n, nbs - 1), 0))
    kout = pl.BlockSpec((ns, QB, GW), lambda rb, n: (rb, jnp.maximum(n - 1, 0), 0))
    buf = pltpu.VMEM((ns, QB, GW), F32)
    outs = pl.pallas_call(
        body, name=name, grid=(max(r // ns, 1), nbs + 1),
        in_specs=in_specs, out_specs=[qout, kout, kout], out_shape=[out, out, out],
        scratch_shapes=[buf, buf, buf, buf],
        compiler_params=_params(),
    )(*args)
    return [t.reshape(q.shape) for t in outs]


def _token_order(refs, scr):
    return [_from_residues(ref, slice(0, GW), r, scr) for ref, r in zip(refs, DILATIONS)]


def _group_weights(lses):
    l0, l1, l2 = lses
    m = jnp.maximum(jnp.maximum(l0, l1), l2)
    e = [jnp.exp(l - m) for l in (l0, l1, l2)]
    den = e[0] + e[1] + e[2]
    return [ei / den for ei in e]


def _mix_out_fwd(h, vec, gates, ypool, o3, lse3, wpb, wab, wout, name):
    S, D = h.shape

    def body(h_ref, vec_ref, gates_ref, yp_ref, o0, o1, o2, l0, l1, l2, wpb_hbm, wab_hbm, wout_hbm,
             hn_ref, ya_ref, merged_ref, tm_ref, wpb_v, wab_v, wout_v, sem, scr):
        _load_weights([(wpb_hbm, wpb_v), (wab_hbm, wab_v), (wout_hbm, wout_v)], sem)
        gt = vec_ref[3:4, :]
        wts = _group_weights(_token_order((l0, l1, l2), scr))
        og = _token_order((o0, o1, o2), scr)
        ya = (wts[0] * og[0] + wts[1] * og[1] + wts[2] * og[2]).astype(BF16)
        ya_ref[...] = ya
        merged = (gates_ref[:, :D].astype(F32) * _dot(yp_ref[...], wpb_v[...])
                  + gates_ref[:, D:].astype(F32) * _dot(ya, wab_v[...])).astype(BF16)
        merged_ref[...] = merged
        tm = _dot(merged, wout_v[...])
        tm_ref[...] = tm.astype(BF16)
        hn_ref[...] = h_ref[...] + gt * tm

    grp = _row_tile(TS, GW)
    res = [_res_spec(r) for r in DILATIONS]
    return pl.pallas_call(
        body, name=name, grid=(S // TS,),
        in_specs=[_row_tile(TS, D), _const((8, D)), _row_tile(TS, 2 * D), grp] + res * 2 + [ANY, ANY, ANY],
        out_specs=[_row_tile(TS, D), grp, _row_tile(TS, D), _row_tile(TS, D)],
        out_shape=[jax.ShapeDtypeStruct((S, D), F32), jax.ShapeDtypeStruct((S, GW), BF16),
                   jax.ShapeDtypeStruct((S, D), BF16), jax.ShapeDtypeStruct((S, D), BF16)],
        scratch_shapes=[pltpu.VMEM((GW, D), BF16), pltpu.VMEM((GW, D), BF16), pltpu.VMEM((D, D), BF16),
                        pltpu.SemaphoreType.DMA((3,)), pltpu.VMEM(RES_SCRATCH, F32)],
        compiler_params=_params(),
    )(h, vec, gates, ypool, *o3, *lse3, wpb, wab, wout)


def _mix_out_bwd(dh, tm, vec, gates, ypool, o3, lse3, wpb, wab, wout, name, rider=None):
    S, D = dh.shape

    def body(*refs):
        if rider is None:
            return compute(*refs)
        host, mine = rider.split(refs, 14, 12)
        rider.head(mine, pl.program_id(0))
        compute(*host)
        rider.tail(mine, pl.program_id(0), S // TS)

    def compute(dh_ref, tm_ref, vec_ref, gates_ref, yp_ref, o0, o1, o2, l0, l1, l2, wpb_hbm, wab_hbm, wout_hbm,
                dtm_ref, dgl_ref, dypb_ref, dyab_ref, dyp_ref, do0, do1, do2, c0, c1, c2, red_ref,
                wpb_v, wab_v, wout_v, sem, scr):
        _load_weights([(wpb_hbm, wpb_v), (wab_hbm, wab_v), (wout_hbm, wout_v)], sem)

        @pl.when(pl.program_id(0) == 0)
        def _():
            red_ref[...] = jnp.zeros_like(red_ref)

        gt = vec_ref[3:4, :]
        dh_v = dh_ref[...]
        red_ref[2:3, :] += jnp.sum(tm_ref[...].astype(F32) * dh_v, axis=0, keepdims=True)
        dtm = (gt * dh_v).astype(BF16)
        dtm_ref[...] = dtm
        dm = _dot_nt(dtm, wout_v[...])
        wts = _group_weights(_token_order((l0, l1, l2), scr))
        og = _token_order((o0, o1, o2), scr)
        ya = wts[0] * og[0] + wts[1] * og[1] + wts[2] * og[2]
        ypb = _dot(yp_ref[...], wpb_v[...])
        yab = _dot(ya.astype(BF16), wab_v[...])
        gp = gates_ref[:, :D].astype(F32)
        ga = gates_ref[:, D:].astype(F32)
        dgl_ref[:, :D] = (dm * ypb * gp * (1.0 - gp)).astype(BF16)
        dgl_ref[:, D:] = (dm * yab * ga * (1.0 - ga)).astype(BF16)
        dypb = (dm * gp).astype(BF16)
        dyab = (dm * ga).astype(BF16)
        dypb_ref[...] = dypb
        dyab_ref[...] = dyab
        dyp_ref[...] = _dot_nt(dypb, wpb_v[...])
        dya = _dot_nt(dyab, wab_v[...])
        row = lax.broadcasted_iota(jnp.int32, (GW, GW), 0) // HEAD_DIM
        col = lax.broadcasted_iota(jnp.int32, (GW, GW), 1) // HEAD_DIM
        ones = jnp.where(row == col, 1.0, 0.0).astype(F32)
        tot = jnp.dot(dya * ya, ones, preferred_element_type=F32, precision=lax.Precision.HIGHEST)
        for wg, do_ref, c_ref, r in zip(wts, (do0, do1, do2), (c0, c1, c2), DILATIONS):
            _to_residues(wg * dya, do_ref, slice(0, GW), r, scr)
            _to_residues(-(wg * tot), c_ref, slice(0, GW), r, scr)

    grp = _row_tile(TS, GW)
    res = [_res_spec(r) for r in DILATIONS]
    specs = (
        [_row_tile(TS, D), _row_tile(TS, D), _const((8, D)), _row_tile(TS, 2 * D), grp] + res * 2
        + [ANY, ANY, ANY],
        [_row_tile(TS, D), _row_tile(TS, 2 * D), _row_tile(TS, D), _row_tile(TS, D), grp]
        + res * 2 + [_const((8, D))],
        [jax.ShapeDtypeStruct((S, D), BF16), jax.ShapeDtypeStruct((S, 2 * D), BF16),
         jax.ShapeDtypeStruct((S, D), BF16), jax.ShapeDtypeStruct((S, D), BF16), jax.ShapeDtypeStruct((S, GW), F32)]
        + [_res_shape(S, r, BF16) for r in DILATIONS] + [_res_shape(S, r, F32) for r in DILATIONS]
        + [jax.ShapeDtypeStruct((8, D), F32)],
        [pltpu.VMEM((GW, D), BF16), pltpu.VMEM((GW, D), BF16), pltpu.VMEM((D, D), BF16),
         pltpu.SemaphoreType.DMA((3,)), pltpu.VMEM(RES_SCRATCH, F32)])
    in_specs, out_specs, out_shape, scratch = specs if rider is None else rider.specs(*specs)
    outs = pl.pallas_call(
        body, name=name, grid=(S // TS,), in_specs=in_specs, out_specs=out_specs, out_shape=out_shape,
        scratch_shapes=scratch, compiler_params=_params(),
    )(dh, tm, vec, gates, ypool, *o3, *lse3, wpb, wab, wout, *(rider.arrays if rider else []))
    return outs if rider is None else (outs[:12], outs[12:])


def _ada_mod(c_all, w, b, name):
    def body(c_ref, w_ref, b_ref, cond_ref, mod_ref):
        cv = c_ref[...]
        cond = cv * jax.nn.sigmoid(cv)
        cond_ref[...] = cond
        mod_ref[...] = jnp.dot(cond, w_ref[...], preferred_element_type=F32,
                               precision=lax.Precision.HIGHEST) + b_ref[...]

    return pl.pallas_call(
        body, name=name,
        out_shape=[jax.ShapeDtypeStruct(c_all.shape, F32), jax.ShapeDtypeStruct((c_all.shape[0], w.shape[1]), F32)],
        compiler_params=_params(),
    )(c_all, w, b)


def _adamw_math(w, g, m, v):
    m = ADAM_B1 * m + (1.0 - ADAM_B1) * g
    v = ADAM_B2 * v + (1.0 - ADAM_B2) * (g * g)
    m_hat = m / (1.0 - ADAM_B1 ** ADAM_STEP)
    v_hat = v / (1.0 - ADAM_B2 ** ADAM_STEP)
    delta = -ADAM_LR * (m_hat / (jnp.sqrt(v_hat) + ADAM_EPS) + ADAM_WD * w)
    return delta, m, v


def _adamw_many(ws, gs, ms, vs, name):
    n = len(ws)

    def body(*refs):
        for k in range(n):
            w_ref, g_ref, m_ref, v_ref = (refs[j * n + k] for j in range(4))
            d_ref, mo_ref, vo_ref = (refs[(4 + j) * n + k] for j in range(3))
            d_ref[...], mo_ref[...], vo_ref[...] = _adamw_math(w_ref[...], g_ref[...], m_ref[...], v_ref[...])

    outs = pl.pallas_call(
        body, name=name, out_shape=[jax.ShapeDtypeStruct(t.shape, F32) for t in ws] * 3,
        compiler_params=_params(),
    )(*ws, *gs, *ms, *vs)
    return outs[:n], outs[n:2 * n], outs[2 * n:]


def _ada_grad_adamw(cond_t, dmod, w, m, v, name, tr=256):
    R, C = w.shape
    nb = dmod.shape[0]

    def body(ct_ref, dm_ref, w_ref, m_ref, v_ref, g_ref, d_ref, mo_ref, vo_ref):
        ct = ct_ref[...]
        dm = dm_ref[...]
        g = jnp.zeros((tr, C), F32)
        for bi in range(nb):
            g = g + ct[:, bi:bi + 1] * dm[bi:bi + 1, :]
        g_ref[...] = g
        d_ref[...], mo_ref[...], vo_ref[...] = _adamw_math(w_ref[...], g, m_ref[...], v_ref[...])

    spec = _row_tile(tr, C)
    out = jax.ShapeDtypeStruct((R, C), F32)
    return pl.pallas_call(
        body, name=name, grid=(R // tr,),
        in_specs=[_row_tile(tr, nb), _const((nb, C)), spec, spec, spec],
        out_specs=[spec] * 4, out_shape=[out] * 4,
        compiler_params=_params(),
    )(cond_t, dmod, w, m, v)


def _row_step(rows, cap=256):
    for cand in range(cap, 15, -16):
        if rows % cand == 0:
            return cand
    return rows


def _slot_sum(x_ref):
    acc = x_ref[0].astype(F32)
    for k in range(1, x_ref.shape[0]):
        acc = acc + x_ref[k].astype(F32)
    return acc


def _sum_slots(x, name, out_dtype=F32):
    n, R, C = x.shape
    tr = _row_step(R)

    def body(x_ref, o_ref):
        o_ref[...] = _slot_sum(x_ref).astype(out_dtype)

    return pl.pallas_call(
        body, name=name, grid=(R // tr,),
        in_specs=[pl.BlockSpec((n, tr, C), lambda i: (0, i, 0))],
        out_specs=_row_tile(tr, C), out_shape=jax.ShapeDtypeStruct((R, C), out_dtype),
        compiler_params=_params(),
    )(x)


def _sum_pair(core, g, recv, name):
    _, _, R, C = g.shape
    tr = _row_step(R, cap=1024)

    def body(core_ref, g_ref, r_ref, o_ref):
        o_ref[...] = (g_ref[...].astype(F32) + r_ref[...].astype(F32)).astype(BF16)

    return pl.pallas_call(
        body, name=name, out_shape=jax.ShapeDtypeStruct((4, R, C), BF16),
        grid_spec=pltpu.PrefetchScalarGridSpec(
            num_scalar_prefetch=1, grid=(4, R // tr),
            in_specs=[pl.BlockSpec((None, None, tr, C), lambda k, i, core_ref: (k, core_ref[0], i, 0)),
                      pl.BlockSpec((None, tr, C), lambda k, i, core_ref: (k, i, 0))],
            out_specs=pl.BlockSpec((None, tr, C), lambda k, i, core_ref: (k, i, 0))),
        compiler_params=_params(),
    )(core, g, recv)


def _sum_adamw(chip, own, recv, w, m, v, name):
    _, R, C = own.shape
    tr = _row_step(R, cap=512)

    def body(chip_ref, own_ref, r_ref, w_ref, m_ref, v_ref, g_ref, d_ref, mo_ref, vo_ref):
        g = own_ref[...].astype(F32) + _slot_sum(r_ref)
        g_ref[...] = g
        d_ref[...], mo_ref[...], vo_ref[...] = _adamw_math(w_ref[...], g, m_ref[...], v_ref[...])

    spec = pl.BlockSpec((tr, C), lambda i, chip_ref: (i, 0))
    out = jax.ShapeDtypeStruct((R, C), F32)
    return pl.pallas_call(
        body, name=name, out_shape=[out] * 4,
        grid_spec=pltpu.PrefetchScalarGridSpec(
            num_scalar_prefetch=1, grid=(R // tr,),
            in_specs=[pl.BlockSpec((None, tr, C), lambda i, chip_ref: (chip_ref[0], i, 0)),
                      pl.BlockSpec((3, tr, C), lambda i, chip_ref: (0, i, 0)), spec, spec, spec],
            out_specs=[spec] * 4),
        compiler_params=_params(),
    )(chip, own, recv, w, m, v)


def _place():
    return lax.axis_index("x"), lax.axis_index("y"), lax.axis_index("c")


def _gather_steps(x_refs, out_refs, send_sems, recv_sems):
    n = len(x_refs)
    x, y, c = _place()
    me, sibling = (x, y, c), (x, y, 1 - c)
    chips = [(1 - x, y), (x, 1 - y), (1 - x, 1 - y)]

    def rows(a, px, py, pc):
        return out_refs[a].at[4 * px + 2 * py + pc]

    def copy(a, k, block, to, src=None):
        return pltpu.make_async_remote_copy(
            src_ref=rows(a, *block) if src is None else src, dst_ref=rows(a, *block),
            send_sem=send_sems.at[a, k], recv_sem=recv_sems.at[a, k], device_id=to, device_id_type=MESH)

    def first(a):
        return [copy(a, 0, me, sibling, src=x_refs[a])] + [
            copy(a, 1 + j, me, (*chip, c), src=x_refs[a]) for j, chip in enumerate(chips)]

    def passed(a, j):
        return copy(a, 4 + j, (*chips[j], c), sibling)

    def start():
        for a in range(n):
            for cp in first(a):
                cp.start()

    def relay():
        for j, chip in enumerate(chips):
            for a in range(n):
                copy(a, 1 + j, (*chip, c), me).wait_recv()
                passed(a, j).start()

    def finish():
        for a in range(n):
            copy(a, 0, sibling, me).wait_recv()
            for j, chip in enumerate(chips):
                copy(a, 4 + j, (*chip, 1 - c), me).wait_recv()
        for a in range(n):
            for cp in first(a) + [passed(a, j) for j in range(3)]:
                cp.wait_send()

    return start, relay, finish


def _gather_tree_steps(x_refs, out_refs, send_sems, recv_sems):
    n = len(x_refs)
    x, y, c = _place()
    me, sibling = (x, y, c), (x, y, 1 - c)
    xn, yn, dg = (1 - x, y), (x, 1 - y), (1 - x, 1 - y)

    def rows(a, px, py, pc):
        return out_refs[a].at[4 * px + 2 * py + pc]

    def copy(a, k, block, to, src=None):
        return pltpu.make_async_remote_copy(
            src_ref=rows(a, *block) if src is None else src, dst_ref=rows(a, *block),
            send_sem=send_sems.at[a, k], recv_sem=recv_sems.at[a, k], device_id=to, device_id_type=MESH)

    def own(a):
        return [copy(a, 0, me, sibling, src=x_refs[a]), copy(a, 1, me, (*xn, c), src=x_refs[a]),
                copy(a, 2, me, (*yn, c), src=x_refs[a])]

    def north_hands_on(a):
        return copy(a, 3, (*xn, c), (*yn, c))

    def south_hands_on(a):
        return copy(a, 3, (*yn, c), (*xn, c))

    def to_sibling(a):
        return [copy(a, 4, (*xn, c), sibling), copy(a, 5, (*yn, c), sibling), copy(a, 6, (*dg, c), sibling)]

    def start():
        for a in range(n):
            for cp in own(a):
                cp.start()

    def relay_neighbours():
        for a in range(n):
            copy(a, 1, (*xn, c), me).wait_recv()
            to_sibling(a)[0].start()

        @pl.when(c == 1)
        def _():
            for a in range(n):
                north_hands_on(a).start()

        for a in range(n):
            copy(a, 2, (*yn, c), me).wait_recv()
            to_sibling(a)[1].start()

        @pl.when(c == 0)
        def _():
            for a in range(n):
                south_hands_on(a).start()

    def relay_diagonal():
        for a in range(n):
            copy(a, 3, (*dg, c), me).wait_recv()
            to_sibling(a)[2].start()

    def finish():
        for a in range(n):
            copy(a, 0, sibling, me).wait_recv()
            copy(a, 4, (*xn, 1 - c), me).wait_recv()
            copy(a, 5, (*yn, 1 - c), me).wait_recv()
            copy(a, 6, (*dg, 1 - c), me).wait_recv()
        for a in range(n):
            for cp in own(a) + to_sibling(a):
                cp.wait_send()

        @pl.when(c == 1)
        def _():
            for a in range(n):
                north_hands_on(a).wait_send()

        @pl.when(c == 0)
        def _():
            for a in range(n):
                south_hands_on(a).wait_send()

    return start, relay_neighbours, relay_diagonal, finish


def _all_gather_tree(arrs, name, gather=()):
    n, extra = len(arrs), len(gather)

    def body(*refs):
        sems = refs[2 * (n + extra):]
        if extra:
            g_start, g_relay, g_finish = _small_gather_steps(
                refs[n:n + extra], refs[2 * n + extra:2 * (n + extra)], *sems[2:])
            g_start()
        for step in _gather_tree_steps(refs[:n], refs[n + extra:2 * n + extra], *sems[:2]):
            step()
        if extra:
            g_relay()
            g_finish()

    return pl.pallas_call(
        body, name=name,
        out_shape=[jax.ShapeDtypeStruct((N_DEV,) + t.shape, t.dtype) for t in list(arrs) + list(gather)],
        in_specs=[ANY] * (n + extra), out_specs=[ANY] * (n + extra),
        scratch_shapes=[pltpu.SemaphoreType.DMA((n, 7)), pltpu.SemaphoreType.DMA((n, 7))]
        + (_small_gather_scratch(extra) if extra else []),
    )(*arrs, *gather)


def _all_gather(arrs, name, own=True):
    n = len(arrs)

    def body(*refs):
        x_refs, out_refs = refs[:n], refs[n:2 * n]
        send_sems, recv_sems, local_sems = refs[2 * n:]
        me = 4 * lax.axis_index("x") + 2 * lax.axis_index("y") + lax.axis_index("c")
        mine = [pltpu.make_async_copy(x_refs[a], out_refs[a].at[me], local_sems.at[a]) for a in range(n)] if own else []
        for cp in mine:
            cp.start()
        for step in _gather_steps(x_refs, out_refs, send_sems, recv_sems):
            step()
        for cp in mine:
            cp.wait()

    return pl.pallas_call(
        body, name=name, out_shape=[jax.ShapeDtypeStruct((N_DEV,) + t.shape, t.dtype) for t in arrs],
        in_specs=[ANY] * n, out_specs=[ANY] * n,
        scratch_shapes=[pltpu.SemaphoreType.DMA((n, 7)), pltpu.SemaphoreType.DMA((n, 7)),
                        pltpu.SemaphoreType.DMA((n,))],
    )(*arrs)


def _pair_exchange_steps(g_refs, out_refs, send_sems, recv_sems):
    x, y, c = _place()

    def give():
        return [pltpu.make_async_remote_copy(
            src_ref=g_refs[a].at[pl.ds(0, 4), 1 - c], dst_ref=out_refs[a], send_sem=send_sems.at[a],
            recv_sem=recv_sems.at[a], device_id=(x, y, 1 - c), device_id_type=MESH) for a in range(len(g_refs))]

    def start():
        for cp in give():
            cp.start()

    def finish():
        for cp in give():
            cp.wait()

    return start, finish


def _pair_exchange(arrs, name):
    n = len(arrs)

    def body(*refs):
        for step in _pair_exchange_steps(refs[:n], refs[n:2 * n], *refs[2 * n:]):
            step()

    return pl.pallas_call(
        body, name=name,
        out_shape=[jax.ShapeDtypeStruct((4,) + t.shape[2:], t.dtype) for t in arrs],
        in_specs=[ANY] * n, out_specs=[ANY] * n,
        scratch_shapes=[pltpu.SemaphoreType.DMA((n,)), pltpu.SemaphoreType.DMA((n,))],
    )(*arrs)


def _chip_exchange_steps(p_refs, out_refs, send_sems, recv_sems):
    x, y, c = _place()
    chips = [(1 - x, y), (x, 1 - y), (1 - x, 1 - y)]

    def copies():
        return [pltpu.make_async_remote_copy(
            src_ref=p_refs[a].at[2 * px + py], dst_ref=out_refs[a].at[j], send_sem=send_sems.at[a, j],
            recv_sem=recv_sems.at[a, j], device_id=(px, py, c), device_id_type=MESH)
            for a in range(len(p_refs)) for j, (px, py) in enumerate(chips)]

    def start():
        for cp in copies():
            cp.start()

    def finish():
        for cp in copies():
            cp.wait()

    return start, finish


def _small_gather_steps(x_refs, out_refs, send_sems, recv_sems, local_sems):
    me = 4 * lax.axis_index("x") + 2 * lax.axis_index("y") + lax.axis_index("c")
    start, relay, finish = _gather_steps(x_refs, out_refs, send_sems, recv_sems)

    def mine():
        return [pltpu.make_async_copy(x_refs[a], out_refs[a].at[me], local_sems.at[a]) for a in range(len(x_refs))]

    def start_all():
        for cp in mine():
            cp.start()
        start()

    def finish_all():
        finish()
        for cp in mine():
            cp.wait()

    return start_all, relay, finish_all


def _small_gather_scratch(k):
    return [pltpu.SemaphoreType.DMA((k, 7)), pltpu.SemaphoreType.DMA((k, 7)), pltpu.SemaphoreType.DMA((k,))]


def _chip_exchange(arrs, name, gather=()):
    n, k = len(arrs), len(gather)

    def body(*refs):
        ins, outs, sems = refs[:n + k], refs[n + k:2 * (n + k)], refs[2 * (n + k):]
        start, finish = _chip_exchange_steps(ins[:n], outs[:n], *sems[:2])
        if k:
            g_start, g_relay, g_finish = _small_gather_steps(ins[n:], outs[n:], *sems[2:])
            g_start()
        start()
        if k:
            g_relay()
        finish()
        if k:
            g_finish()

    return pl.pallas_call(
        body, name=name,
        out_shape=[jax.ShapeDtypeStruct((3,) + t.shape[1:], t.dtype) for t in arrs]
        + [jax.ShapeDtypeStruct((N_DEV,) + t.shape, t.dtype) for t in gather],
        in_specs=[ANY] * (n + k), out_specs=[ANY] * (n + k),
        scratch_shapes=[pltpu.SemaphoreType.DMA((n, 3)), pltpu.SemaphoreType.DMA((n, 3))]
        + (_small_gather_scratch(k) if k else []),
    )(*arrs, *gather)


class _Rider:
    def __init__(self, arrays, out_shape, sems, steps, relay_at=()):
        self.arrays, self.out_shape, self.scratch, self.steps = list(arrays), out_shape, sems, steps
        self.n = len(self.arrays)
        self.relay_at = relay_at

    def specs(self, in_specs, out_specs, out_shape, scratch):
        extra = [ANY] * self.n
        return in_specs + extra, out_specs + extra, out_shape + self.out_shape, scratch + self.scratch

    def split(self, refs, n_in, n_out):
        k = self.n
        a, b = n_in + k, n_in + k + n_out
        return refs[:n_in] + refs[a:b] + refs[b + k:-2], (refs[n_in:a], refs[b:b + k], refs[-2:])

    def head(self, mine, step):
        pl.when(step == 0)(self.steps(mine[0], mine[1], *mine[2])[0])

    def tail(self, mine, step, nsteps):
        steps = self.steps(mine[0], mine[1], *mine[2])
        for relay, frac in zip(steps[1:-1], self.relay_at):
            pl.when(step == min(int(frac * nsteps), nsteps - 1))(relay)
        pl.when(step == nsteps - 1)(steps[-1])


def _gather_rider(arrs, relay_at=(0.5, 0.78)):
    n = len(arrs)
    return _Rider(arrs, [jax.ShapeDtypeStruct((N_DEV,) + t.shape, t.dtype) for t in arrs],
                  [pltpu.SemaphoreType.DMA((n, 7)), pltpu.SemaphoreType.DMA((n, 7))], _gather_tree_steps,
                  relay_at)


def _pair_exchange_rider(arrs):
    n = len(arrs)
    return _Rider(arrs, [jax.ShapeDtypeStruct((4,) + t.shape[2:], t.dtype) for t in arrs],
                  [pltpu.SemaphoreType.DMA((n,)), pltpu.SemaphoreType.DMA((n,))], _pair_exchange_steps)


def _chip_exchange_rider(arrs):
    n = len(arrs)
    return _Rider(arrs, [jax.ShapeDtypeStruct((3,) + t.shape[1:], t.dtype) for t in arrs],
                  [pltpu.SemaphoreType.DMA((n, 3)), pltpu.SemaphoreType.DMA((n, 3))], _chip_exchange_steps)


def _rope_tables(positions):
    inv_freq = ROPE_THETA ** (-jnp.arange(0, HEAD_DIM, 2, dtype=F32) / HEAD_DIM)
    ang = positions.astype(F32)[:, None] * inv_freq
    cos, sin = jnp.cos(ang), jnp.sin(ang)
    return jnp.tile(cos, (1, 4)), jnp.tile(jnp.concatenate([-sin, sin], axis=1), (1, 2))


class _GradReducer:
    def __init__(self):
        self.core = lax.axis_index("c").astype(jnp.int32).reshape(1)
        self.own, self.others, self.waiting, self.riding = {}, {}, [], []

    def pair(self, named):
        mine = self._split(named)
        self._summed(mine, _pair_exchange(list(mine.values()), "reduce_pair_" + next(iter(named))))

    def pair_rider(self, named):
        self.pairing = self._split(named)
        return _pair_exchange_rider(list(self.pairing.values()))

    def pair_landed(self, results):
        self._summed(self.pairing, results)

    @staticmethod
    def _split(named):
        return {k: g.reshape((4, 2) + g.shape[1:]) for k, g in named.items()}

    def _summed(self, mine, theirs):
        for (k, g), r in zip(mine.items(), theirs):
            self.own[k] = _sum_pair(self.core, g, r, "sum_pair_" + k)
        self.waiting += list(mine)

    def rider(self):
        self.riding, self.waiting = self.waiting, []
        return _chip_exchange_rider([self.own[k] for k in self.riding])

    def landed(self, results):
        self.others.update(zip(self.riding, results))

    def flush(self, name, gather=()):
        keys, self.waiting = self.waiting, []
        res = _chip_exchange([self.own[k] for k in keys], name, gather=gather)
        self.others.update(zip(keys, res))
        return res[len(keys):]


def _by_owner(g):
    if g.ndim == 3:
        return g if g.shape[0] == N_DEV else g.reshape(N_DEV, g.shape[1] * g.shape[0] // N_DEV, g.shape[2])
    return g.reshape(N_DEV, g.shape[0] // N_DEV, g.shape[1])


def _local_step(x, target, positions, mod, small, W, late=None, red=None):
    S, D = x.shape
    gains =jnp.stack([small["g1"], small["g2"], small["g3"]])[:, None, :]
    v1, v2, v3 = jnp.pad(jnp.concatenate([gains, mod.reshape(3, 3, D)], axis=1), ((0, 0), (0, 4), (0, 0)))
    vf = jnp.pad(small["gf"][None], ((0, 7), (0, 0)))
    cos, sin = _rope_tables(positions)
    wbd = jax.scipy.linalg.block_diag(*[small["w_pool"][k] for k in range(4)]).astype(BF16)
    pscale = small["pool_scale"].reshape(1, GW)

    if late is None:
        h1, u1, ab1, act1, f1 = _ffn_fwd(x, v1, W["w1in"], W["w1out"], "ffn1_fwd")
    else:
        (h1, u1, ab1, act1, f1), landed = _ffn_fwd(x, v1, W["w1in"], W["w1out"], "ffn1_fwd", rider=late[0])
        W = {**W, **late[1](landed)}
    u2, p, gates, *qkv = _mix_in_fwd(h1, v2, cos, sin, W["win"], "mix_in_fwd")
    dpool, ypool = _pool_fwd(p, wbd, pscale, "pool_fwd")
    o3, lse3 = [], []
    for gi in range(len(DILATIONS)):
        o, lse = _attn_fwd(qkv[gi], qkv[3 + gi], qkv[6 + gi], f"attn_fwd_{gi}")
        o3.append(o)
        lse3.append(lse)
    h2, ya, merged, tm = _mix_out_fwd(h1, v2, gates, ypool, o3, lse3, W["wpb"], W["wab"], W["wout"], "mix_out_fwd")
    dh3, u3, ab3, act3, f3, loss_blk, dgf = _ffn_fwd(h2, v3, W["w2in"], W["w2out"], "ffn2_fwd", final=(target, vf))

    dh2, dab3, df3, red3 = _ffn_bwd(dh3, h2, f3, ab3, v3, W["w2in"], W["w2out"], "ffn2_bwd")
    half_f = ab3.shape[2] // 2
    G = {"w2in": _by_owner(_wgrad(dab3, u3, "wgrad_2in", tm=half_f))}
    mix_out_args = (dh2, tm, v2, gates, ypool, o3, lse3, W["wpb"], W["wab"], W["wout"], "mix_out_bwd")
    if red is None:
        G["w2out"] = _by_owner(_wgrad(act3, df3, "wgrad_2out", tm=half_f))
        mix_out = _mix_out_bwd(*mix_out_args)
    else:
        g2out, landed = _wgrad(act3, df3, "wgrad_2out", tm=half_f, rider=red.pair_rider({"w2in": G["w2in"]}))
        red.pair_landed(landed)
        G["w2out"] = _by_owner(g2out)
        red.pair({"w2out": G["w2out"]})
        mix_out, landed = _mix_out_bwd(*mix_out_args, rider=red.rider())
        red.landed(landed)
    (dtm, dgl, dypb, dyab, dyp, do0, do1, do2, c0, c1, c2, red2o) = mix_out
    dq3, dk3, dv3 = [], [], []
    for gi, (do, ct) in enumerate(zip((do0, do1, do2), (c0, c1, c2))):
        dq, dk, dv = _attn_bwd(qkv[gi], qkv[3 + gi], qkv[6 + gi], do, lse3[gi], ct, f"attn_bwd_{gi}")
        dq3.append(dq)
        dk3.append(dk)
        dv3.append(dv)
    dp, dwbd, dps = _pool_bwd(dyp, dpool, wbd, pscale, "pool_bwd")
    dh1, dproj, red2i = _mix_in_bwd(dh2, h1, v2, cos, sin, dp, dq3 + dk3 + dv3, dgl, W["win"], "mix_in_bwd")
    G["win"] = _by_owner(_wgrad(dproj, u2, "wgrad_in", tm=1152))
    G["wpb"] = _full_to_cols(_wgrad(ypool, dypb, "wgrad_pb"))
    G["wab"] = _full_to_cols(_wgrad(ya, dyab, "wgrad_ab"))
    if red is None:
        G["wout"] = _by_owner(_wgrad(merged, dtm, "wgrad_out"))
    else:
        gout, landed = _wgrad(merged, dtm, "wgrad_out", rider=red.pair_rider({k: G[k] for k in ("win", "wpb", "wab")}))
        red.pair_landed(landed)
        G["wout"] = _by_owner(gout)
        red.pair({"wout": G["wout"]})
    dx, dab1, df1, red1 = _ffn_bwd(dh1, x, f1, ab1, v1, W["w1in"], W["w1out"], "ffn1_bwd")
    dmod = jnp.concatenate([red1[:3], (red2i + red2o)[:3], red3[:3]])
    dsmall = {
        "g1": red1[3], "g2": red2i[3], "g3": red3[3], "gf": dgf[0],
        "w_pool": jnp.stack([dwbd[k * 64:(k + 1) * 64, k * 64:(k + 1) * 64] for k in range(4)]),
        "pool_scale": dps[0],
    }
    if red is None:
        G["w1in"] = _by_owner(_wgrad(dab1, u1, "wgrad_1in", tm=half_f))
        G["w1out"] = _by_owner(_wgrad(act1, df1, "wgrad_1out", tm=half_f))
        return loss_blk[0, 0], dx, G, dmod, dsmall
    g1in, landed = _wgrad(dab1, u1, "wgrad_1in", tm=half_f, rider=red.rider())
    red.landed(landed)
    red.pair({"w1in": _by_owner(g1in)})
    g1out, landed = _wgrad(act1, df1, "wgrad_1out", tm=half_f, rider=red.rider())
    red.landed(landed)
    red.pair({"w1out": _by_owner(g1out)})
    part = _pack_small(dict(b_ada=dmod, g_norm_ffn1=dsmall["g1"], g_norm_mix=dsmall["g2"], g_norm_ffn2=dsmall["g3"],
                            g_final=dsmall["gf"], w_pool=dsmall["w_pool"], pool_scale=dsmall["pool_scale"]),
                       loss_blk[0, 0])
    (parts,) = red.flush("reduce_chips_w1out", gather=[part])
    return dx, parts


SHARDED = ("w_ffn1_in", "w_ffn1_out", "w_in", "w_pool_branch", "w_attn_branch", "w_out", "w_ffn2_in", "w_ffn2_out")
TRANSPOSED = ("w_ffn1_in", "w_in", "w_ffn2_in")
FIRST = ("w_ffn1_in", "w_ffn1_out")
LATER = tuple(n for n in SHARDED if n not in FIRST)
GRAD_KEY = dict(w_ffn1_in="w1in", w_ffn1_out="w1out", w_in="win", w_pool_branch="wpb", w_attn_branch="wab",
                w_out="wout", w_ffn2_in="w2in", w_ffn2_out="w2out")


def _cols_to_full(g):
    return g.transpose(1, 0, 2).reshape(g.shape[1], N_DEV * g.shape[2])


def _full_to_cols(t):
    return t.reshape(t.shape[0], N_DEV, t.shape[1] // N_DEV).transpose(1, 0, 2)


SMALL = (("b_ada", 72), ("g_norm_ffn1", 8), ("g_norm_mix", 8), ("g_norm_ffn2", 8), ("g_final", 8),
         ("w_pool", 128), ("pool_scale", 8))


def _pack_small(vals, loss):
    rows = []
    for name, nrows in SMALL:
        t = vals[name].reshape(-1, 128)
        rows.append(jnp.pad(t, ((0, nrows - t.shape[0]), (0, 0))))
    rows.append(jnp.full((8, 128), loss, F32))
    return jnp.concatenate(rows)


def _unpack_small(slab, shapes):
    out, off = {}, 0
    for name, nrows in SMALL:
        used = 1
        for d in shapes[name]:
            used *= d
        out[name] = slab[off:off + used // 128].reshape(shapes[name])
        off += nrows
    return out, slab[off, 0]


def _as_2d(t):
    return t.reshape(-1, t.shape[-1])


def kernel(x, c, positions, w_ada, b_ada, g_norm_ffn1, w_ffn1_in, w_ffn1_out, g_norm_mix, w_in, w_pool, pool_scale, w_pool_branch, w_attn_branch, w_out, g_norm_ffn2, w_ffn2_in, w_ffn2_out, g_final, loss_target, m_w_ada, m_b_ada, m_g_norm_ffn1, m_w_ffn1_in, m_w_ffn1_out, m_g_norm_mix, m_w_in, m_w_pool, m_pool_scale, m_w_pool_branch, m_w_attn_branch, m_w_out, m_g_norm_ffn2, m_w_ffn2_in, m_w_ffn2_out, m_g_final, v_w_ada, v_b_ada, v_g_norm_ffn1, v_w_ffn1_in, v_w_ffn1_out, v_g_norm_mix, v_w_in, v_w_pool, v_pool_scale, v_w_pool_branch, v_w_attn_branch, v_w_out, v_g_norm_ffn2, v_w_ffn2_in, v_w_ffn2_out, v_g_final):
    names = ["w_ada", "b_ada", "g_norm_ffn1", "w_ffn1_in", "w_ffn1_out", "g_norm_mix", "w_in", "w_pool", "pool_scale",
             "w_pool_branch", "w_attn_branch", "w_out", "g_norm_ffn2", "w_ffn2_in", "w_ffn2_out", "g_final"]
    w = dict(w_ada=w_ada, b_ada=b_ada, g_norm_ffn1=g_norm_ffn1, w_ffn1_in=w_ffn1_in, w_ffn1_out=w_ffn1_out,
             g_norm_mix=g_norm_mix, w_in=w_in, w_pool=w_pool, pool_scale=pool_scale, w_pool_branch=w_pool_branch,
             w_attn_branch=w_attn_branch, w_out=w_out, g_norm_ffn2=g_norm_ffn2, w_ffn2_in=w_ffn2_in,
             w_ffn2_out=w_ffn2_out, g_final=g_final)
    m = dict(w_ada=m_w_ada, b_ada=m_b_ada, g_norm_ffn1=m_g_norm_ffn1, w_ffn1_in=m_w_ffn1_in, w_ffn1_out=m_w_ffn1_out,
             g_norm_mix=m_g_norm_mix, w_in=m_w_in, w_pool=m_w_pool, pool_scale=m_pool_scale,
             w_pool_branch=m_w_pool_branch, w_attn_branch=m_w_attn_branch, w_out=m_w_out, g_norm_ffn2=m_g_norm_ffn2,
             w_ffn2_in=m_w_ffn2_in, w_ffn2_out=m_w_ffn2_out, g_final=m_g_final)
    v = dict(w_ada=v_w_ada, b_ada=v_b_ada, g_norm_ffn1=v_g_norm_ffn1, w_ffn1_in=v_w_ffn1_in, w_ffn1_out=v_w_ffn1_out,
             g_norm_mix=v_g_norm_mix, w_in=v_w_in, w_pool=v_w_pool, pool_scale=v_pool_scale,
             w_pool_branch=v_w_pool_branch, w_attn_branch=v_w_attn_branch, w_out=v_w_out, g_norm_ffn2=v_g_norm_ffn2,
             w_ffn2_in=v_w_ffn2_in, w_ffn2_out=v_w_ffn2_out, g_final=v_g_final)
    shapes = {n: w[n].shape for n in names}
    me = 4 * lax.axis_index("x") + 2 * lax.axis_index("y") + lax.axis_index("c")
    D = x.shape[-1]
    n_mod = w_ada.shape[-1] * N_DEV // D

    def local(t, name):
        return t[name][0].T if name in TRANSPOSED else t[name][0]

    shards = {name: local(w, name).astype(BF16) for name in SHARDED}

    def gather_done(names, fulls):
        return {name: lax.dynamic_update_index_in_dim(full, shards[name], me, axis=0)
                for name, full in zip(names, fulls)}

    def ffn_weights(g, pre):
        return {"w%sin" % pre: g["w_ffn%s_in" % pre].reshape(2, -1, D),
                "w%sout" % pre: g["w_ffn%s_out" % pre].reshape(-1, D)}

    def later_weights(fulls):
        g = gather_done(LATER, fulls)
        return dict(win=g["w_in"].reshape(-1, D), wpb=_cols_to_full(g["w_pool_branch"]),
                    wab=_cols_to_full(g["w_attn_branch"]), wout=g["w_out"].reshape(D, D), **ffn_weights(g, "2"))

    *first, c_all = _all_gather_tree([shards[n] for n in FIRST], "gather_ffn1", gather=[c.reshape(D // 128, 128)])
    W = ffn_weights(gather_done(FIRST, first), "1")

    ada_cols = w_ada.shape[-1]
    b_mine = lax.dynamic_slice_in_dim(b_ada, me * ada_cols, ada_cols, axis=1)
    cond, mod_part = _ada_mod(c_all.reshape(N_DEV, D), w_ada[0], b_mine, "ada_mod")
    (mod_all,) = _all_gather([mod_part.reshape(-1, 128)], "gather_mod")
    mod_all = mod_all.reshape(N_DEV, N_DEV, ada_cols)
    mod = lax.dynamic_index_in_dim(mod_all, me, axis=1, keepdims=False).reshape(n_mod, D)

    small = dict(g1=g_norm_ffn1[0], g2=g_norm_mix[0], g3=g_norm_ffn2[0], gf=g_final, w_pool=w_pool[0],
                 pool_scale=pool_scale[0])
    red = _GradReducer()
    dx, parts = _local_step(
        x[0], loss_target[0], positions[0], mod, small, W,
        late=(_gather_rider([shards[n] for n in LATER]), later_weights), red=red)
    chip = (2 * lax.axis_index("x") + lax.axis_index("y")).astype(jnp.int32).reshape(1)

    gsmall, loss = _unpack_small(_sum_slots(parts, "sum_small"), shapes)
    rows_mine = ada_cols // 128
    dmod_mine = lax.dynamic_slice_in_dim(parts, me * rows_mine, rows_mine, axis=1).reshape(N_DEV, ada_cols)

    grads, delta, new_m, new_v = {}, {}, {}, {}
    grads["w_ada"], delta["w_ada"], new_m["w_ada"], new_v["w_ada"] = (
        t[None] for t in _ada_grad_adamw(cond.T, dmod_mine, w_ada[0], m_w_ada[0], v_w_ada[0], "ada_grad_adamw"))
    for name in SHARDED:
        key = GRAD_KEY[name]
        res = _sum_adamw(chip, red.own[key], red.others[key], local(w, name), local(m, name), local(v, name),
                         "adamw_" + name)
        grads[name], delta[name], new_m[name], new_v[name] = (
            (t.T if name in TRANSPOSED else t)[None] for t in res)
    small_names = [name for name, _ in SMALL]
    res = _adamw_many(*([_as_2d(t[name]) for name in small_names] for t in (w, gsmall, m, v)), "adamw_small")
    for dst, vals in zip((delta, new_m, new_v), res):
        dst.update({name: t.reshape(shapes[name]) for name, t in zip(small_names, vals)})
    grads.update(gsmall)

    return (loss, dx[None], *[grads[n] for n in names], *[delta[n] for n in names],
            *[new_m[n] for n in names], *[new_v[n] for n in names])
```

```python
import functools

import jax
import jax.numpy as jnp
from jax import lax
from jax.experimental import pallas as pl
from jax.experimental.pallas import tpu as pltpu

F32 = jnp.float32
BF16 = jnp.bfloat16
MESH = pl.DeviceIdType.MESH
ANY = pl.BlockSpec(memory_space=pl.ANY)

N_DEV = 8
EPS = 1e-6
HEAD_DIM = 64
HEADS = 4
GW = HEADS * HEAD_DIM
DILATIONS = (1, 4, 16)
BAND = 128
QB = 128
POOL_WINDOWS = (2, 4, 8, 16)
HALO = 16
ROPE_THETA = 10000.0

ADAM_LR = 0.001
ADAM_B1 = 0.9
ADAM_B2 = 0.999
ADAM_EPS = 1e-08
ADAM_WD = 0.01
ADAM_STEP = 10

VMEM_LIMIT = 56 * 1024 * 1024
TS = 512
FFN_TS = 256
FFN_CHUNKS = (2816,)

NT = (((1,), (1,)), ((), ()))
TN = (((0,), (0,)), ((), ()))


def _params(**kw):
    return pltpu.CompilerParams(vmem_limit_bytes=VMEM_LIMIT, **kw)


def _dot(a, b):
    return jnp.dot(a, b, preferred_element_type=F32)


def _dot_nt(a, b):
    return lax.dot_general(a, b, NT, preferred_element_type=F32)


def _dot_tn(a, b):
    return lax.dot_general(a, b, TN, preferred_element_type=F32)


def _load_weights(pairs, sem):
    @pl.when(pl.program_id(0) == 0)
    def _():
        copies = [pltpu.make_async_copy(src, dst, sem.at[i]) for i, (src, dst) in enumerate(pairs)]
        for cp in copies:
            cp.start()
        for cp in copies:
            cp.wait()


def _norm_mod(x, g, sc, sh):
    r = lax.rsqrt(jnp.mean(x * x, axis=-1, keepdims=True) + EPS)
    xn = x * r
    y = xn * g
    return r, xn, y, y * (1.0 + sc) + sh


def _norm_mod_bwd(du, r, xn, y, g, sc):
    dsh = jnp.sum(du, axis=0, keepdims=True)
    dsc = jnp.sum(du * y, axis=0, keepdims=True)
    dy = du * (1.0 + sc)
    dg = jnp.sum(dy * xn, axis=0, keepdims=True)
    dxn = dy * g
    dx = r * (dxn - xn * jnp.mean(dxn * xn, axis=-1, keepdims=True))
    return dx, dsh, dsc, dg


def _row_tile(ts, width):
    return pl.BlockSpec((ts, width), lambda i: (i, 0))


def _const(shape):
    return pl.BlockSpec(shape, lambda *_: (0,) * len(shape))


def _ffn_chunks(Fd):
    assert sum(FFN_CHUNKS) == Fd
    edges = [sum(FFN_CHUNKS[:k]) for k in range(len(FFN_CHUNKS) + 1)]
    return [slice(a, b) for a, b in zip(edges[:-1], edges[1:])]
def _final_tile(x, g, target):
    r = lax.rsqrt(jnp.mean(x * x, axis=-1, keepdims=True) + EPS)
    xn = x * r
    err = xn * g - target
    loss = 0.5 * jnp.sum(jnp.mean(err * err, axis=-1, keepdims=True))
    dy = err * (1.0 / x.shape[-1])
    dg = jnp.sum(dy * xn, axis=0, keepdims=True)
    dxn = dy * g
    return r * (dxn - xn * jnp.mean(dxn * xn, axis=-1, keepdims=True)), loss, dg


def _ffn_fwd(h, vec, win, wout, name, rider=None, final=None):
    TS = FFN_TS
    S, D = h.shape
    _, Fd, _ = win.shape
    n_in, n_out = (6, 7) if final else (4, 5)

    def body(*refs):
        if rider is None:
            return compute(*refs)
        host, mine = rider.split(refs, n_in, n_out)
        rider.head(mine, pl.program_id(0))
        compute(*host)
        rider.tail(mine, pl.program_id(0), S // TS)

    def compute(*refs):
        h_ref, vec_ref, win_hbm, wout_hbm = refs[:4]
        hn_ref, u_ref, ab_ref, act_ref, f_ref = refs[n_in:n_in + 5]
        win_v, wout_v, sem = refs[n_in + n_out:]
        _load_weights([(win_hbm, win_v), (wout_hbm, wout_v)], sem)
        x = h_ref[...]
        g, sh, sc, gt = (vec_ref[k:k + 1, :] for k in range(4))
        _, _, _, u = _norm_mod(x, g, sc, sh)
        ub = u.astype(BF16)
        u_ref[...] = ub
        acc = jnp.zeros((TS, D), F32)
        for sl in _ffn_chunks(Fd):
            a = _dot_nt(ub, win_v[0, sl, :])
            b = _dot_nt(ub, win_v[1, sl, :])
            act = ((a * jax.nn.sigmoid(a)) * b).astype(BF16)
            ab_ref[0, :, sl] = a.astype(BF16)
            ab_ref[1, :, sl] = b.astype(BF16)
            act_ref[:, sl] = act
            acc = acc + _dot(act, wout_v[sl, :])
        f_ref[...] = acc.astype(BF16)
        hn = x + (0.5 * gt) * acc
        if not final:
            hn_ref[...] = hn
            return
        t_ref, gf_ref = refs[4:6]
        loss_ref, dgf_ref = refs[n_in + 5:n_in + 7]

        @pl.when(pl.program_id(0) == 0)
        def _():
            loss_ref[...] = jnp.zeros_like(loss_ref)
            dgf_ref[...] = jnp.zeros_like(dgf_ref)

        hn_ref[...], loss, dg = _final_tile(hn, gf_ref[0:1, :], t_ref[...])
        loss_ref[...] += loss
        dgf_ref[0:1, :] += dg

    specs = (
        [_row_tile(TS, D), _const((8, D)), ANY, ANY] + ([_row_tile(TS, D), _const((8, D))] if final else []),
        [_row_tile(TS, D), _row_tile(TS, D), pl.BlockSpec((2, TS, Fd), lambda i: (0, i, 0)),
         _row_tile(TS, Fd), _row_tile(TS, D)] + ([_const((8, 128)), _const((8, D))] if final else []),
        [jax.ShapeDtypeStruct((S, D), F32), jax.ShapeDtypeStruct((S, D), BF16),
         jax.ShapeDtypeStruct((2, S, Fd), BF16), jax.ShapeDtypeStruct((S, Fd), BF16),
         jax.ShapeDtypeStruct((S, D), BF16)]
        + ([jax.ShapeDtypeStruct((8, 128), F32), jax.ShapeDtypeStruct((8, D), F32)] if final else []),
        [pltpu.VMEM(win.shape, BF16), pltpu.VMEM(wout.shape, BF16), pltpu.SemaphoreType.DMA((2,))])
    in_specs, out_specs, out_shape, scratch = specs if rider is None else rider.specs(*specs)
    outs = pl.pallas_call(
        body, name=name, grid=(S // TS,), in_specs=in_specs, out_specs=out_specs, out_shape=out_shape,
        scratch_shapes=scratch, compiler_params=_params(),
    )(h, vec, win, wout, *(final or ()), *(rider.arrays if rider else []))
    return outs if rider is None else (outs[:n_out], outs[n_out:])


def _ffn_bwd(dh, h, f, ab, vec, win, wout, name):
    TS = FFN_TS
    S, D = h.shape
    _, Fd, _ = win.shape

    def body(dh_ref, h_ref, f_ref, ab_ref, vec_ref, win_hbm, wout_hbm,
             dhp_ref, dab_ref, df_ref, red_ref, win_v, wout_v, sem):
        _load_weights([(win_hbm, win_v), (wout_hbm, wout_v)], sem)

        @pl.when(pl.program_id(0) == 0)
        def _():
            red_ref[...] = jnp.zeros_like(red_ref)

        dh_v = dh_ref[...]
        x = h_ref[...]
        g, sh, sc, gt = (vec_ref[k:k + 1, :] for k in range(4))
        dgt = jnp.sum((0.5 * f_ref[...].astype(F32)) * dh_v, axis=0, keepdims=True)
        dfb = ((0.5 * gt) * dh_v).astype(BF16)
        df_ref[...] = dfb
        du = jnp.zeros((TS, D), F32)
        for sl in _ffn_chunks(Fd):
            dact = _dot_nt(dfb, wout_v[sl, :])
            av = ab_ref[0, :, sl].astype(F32)
            bv = ab_ref[1, :, sl].astype(F32)
            sg = jax.nn.sigmoid(av)
            da = (dact * bv * (sg * (1.0 + av * (1.0 - sg)))).astype(BF16)
            db = (dact * (av * sg)).astype(BF16)
            dab_ref[0, :, sl] = da
            dab_ref[1, :, sl] = db
            du = du + _dot(da, win_v[0, sl, :]) + _dot(db, win_v[1, sl, :])
        r, xn, y, _ = _norm_mod(x, g, sc, sh)
        dx, dsh, dsc, dg = _norm_mod_bwd(du, r, xn, y, g, sc)
        dhp_ref[...] = dh_v + dx
        red_ref[0:1, :] += dsh
        red_ref[1:2, :] += dsc
        red_ref[2:3, :] += dgt
        red_ref[3:4, :] += dg

    ab_spec = pl.BlockSpec((2, TS, Fd), lambda i: (0, i, 0))
    return pl.pallas_call(
        body, name=name, grid=(S // TS,),
        in_specs=[_row_tile(TS, D), _row_tile(TS, D), _row_tile(TS, D), ab_spec, _const((8, D)), ANY, ANY],
        out_specs=[_row_tile(TS, D), ab_spec, _row_tile(TS, D), _const((8, D))],
        out_shape=[jax.ShapeDtypeStruct((S, D), F32), jax.ShapeDtypeStruct((2, S, Fd), BF16),
                   jax.ShapeDtypeStruct((S, D), BF16), jax.ShapeDtypeStruct((8, D), F32)],
        scratch_shapes=[pltpu.VMEM(win.shape, BF16), pltpu.VMEM(wout.shape, BF16), pltpu.SemaphoreType.DMA((2,))],
        compiler_params=_params(),
    )(dh, h, f, ab, vec, win, wout)


def _wgrad(x, y, name, tm=None, ts=2048, rider=None):
    xb = x.ndim == 3
    nb = x.shape[0] if xb else 0
    S, M = x.shape[-2:]
    N = y.shape[-1]
    tm = tm or M
    ts = min(ts, S)
    nk = S // ts
    grid = (max(nb, 1), M // tm, nk)

    def body(*refs):
        if rider is None:
            return compute(*refs)
        host, mine = rider.split(refs, 2, 1)
        step = (pl.program_id(0) * grid[1] + pl.program_id(1)) * grid[2] + pl.program_id(2)
        rider.head(mine, step)
        compute(*host)
        rider.tail(mine, step, grid[0] * grid[1] * grid[2])

    def compute(x_ref, y_ref, o_ref, acc):
        k = pl.program_id(2)

        @pl.when(k == 0)
        def _():
            acc[...] = jnp.zeros_like(acc)

        acc[...] += _dot_tn(x_ref[...], y_ref[...])

        @pl.when(k == nk - 1)
        def _():
            o_ref[...] = acc[...].astype(BF16)

    x_spec = (pl.BlockSpec((None, ts, tm), lambda b, i, k: (b, k, i)) if xb
              else pl.BlockSpec((ts, tm), lambda b, i, k: (k, i)))
    y_spec = pl.BlockSpec((ts, N), lambda b, i, k: (k, 0))
    if xb:
        o_spec, o_shape = pl.BlockSpec((None, tm, N), lambda b, i, k: (b, i, 0)), (nb, M, N)
    else:
        o_spec, o_shape = pl.BlockSpec((tm, N), lambda b, i, k: (i, 0)), (M, N)
    specs = ([x_spec, y_spec], [o_spec], [jax.ShapeDtypeStruct(o_shape, BF16)], [pltpu.VMEM((tm, N), F32)])
    in_specs, out_specs, out_shape, scratch = specs if rider is None else rider.specs(*specs)
    outs = pl.pallas_call(
        body, name=name, grid=grid, in_specs=in_specs, out_specs=out_specs, out_shape=out_shape,
        scratch_shapes=scratch, compiler_params=_params(),
    )(x, y, *(rider.arrays if rider else []))
    return outs[0] if rider is None else (outs[0], outs[1:])


P_OFF, Q_OFF, K_OFF, V_OFF, G_OFF = 0, 256, 1024, 1792, 2560
IN_WIDTH = 4608


def _first_half_mask(ts):
    lane = lax.broadcasted_iota(jnp.int32, (ts, 128), 1)
    return (lane % HEAD_DIM) < (HEAD_DIM // 2)


def _rope(t, cos, sin_signed, first, sign):
    partner = jnp.where(first, pltpu.roll(t, 96, 1), pltpu.roll(t, 32, 1))
    return t * cos + sign * (partner * sin_signed)


def _res_spec(r):
    return pl.BlockSpec((r, TS // r, GW), lambda i: (0, i, 0))


def _res_shape(S, r, dtype):
    return jax.ShapeDtypeStruct((r, S // r, GW), dtype)


def _to_residues(piece, out_ref, lanes, r, scr):
    if r == 1:
        out_ref[0, :, lanes] = piece.astype(out_ref.dtype)
        return
    for h in range(piece.shape[1] // 128):
        scr[h] = piece[:, h * 128:(h + 1) * 128]
        at = slice(lanes.start + h * 128, lanes.start + (h + 1) * 128)
        for res in range(r):
            out_ref[res, :, at] = scr[h, pl.ds(res, TS // r, stride=r), :].astype(out_ref.dtype)


def _from_residues(in_ref, lanes, r, scr):
    if r == 1:
        return in_ref[0, :, lanes].astype(F32)
    halves = (lanes.stop - lanes.start) // 128
    for h in range(halves):
        at = slice(lanes.start + h * 128, lanes.start + (h + 1) * 128)
        for res in range(r):
            scr[h, pl.ds(res, TS // r, stride=r), :] = in_ref[res, :, at].astype(F32)
    return scr[0] if halves == 1 else jnp.concatenate([scr[0], scr[1]], axis=1)


RES_SCRATCH = (2, TS, 128)


def _mix_in_fwd(h, vec, cos, sin, win, name):
    S, D = h.shape

    def body(h_ref, vec_ref, cos_ref, sin_ref, win_hbm, u_ref, p_ref, gates_ref, *rest):
        qkv_refs, (win_v, sem, scr) = rest[:9], rest[9:]
        _load_weights([(win_hbm, win_v)], sem)
        g, sh, sc = (vec_ref[k:k + 1, :] for k in range(3))
        _, _, _, u = _norm_mod(h_ref[...], g, sc, sh)
        ub = u.astype(BF16)
        u_ref[...] = ub
        p_ref[...] = _dot_nt(ub, win_v[P_OFF:Q_OFF, :])
        cosv, sinv = cos_ref[...], sin_ref[...]
        first = _first_half_mask(TS)
        for which, off in enumerate((Q_OFF, K_OFF, V_OFF)):
            t = _dot_nt(ub, win_v[off:off + 3 * GW, :])
            for gi in range(3):
                for half in range(2):
                    c0 = gi * GW + half * 128
                    piece = t[:, c0:c0 + 128]
                    if which < 2:
                        piece = _rope(piece, cosv, sinv, first, 1.0)
                    _to_residues(piece, qkv_refs[which * 3 + gi], slice(half * 128, (half + 1) * 128),
                                 DILATIONS[gi], scr)
        gates_ref[...] = jax.nn.sigmoid(_dot_nt(ub, win_v[G_OFF:IN_WIDTH, :])).astype(BF16)

    return pl.pallas_call(
        body, name=name, grid=(S // TS,),
        in_specs=[_row_tile(TS, D), _const((8, D)), _row_tile(TS, 128), _row_tile(TS, 128), ANY],
        out_specs=[_row_tile(TS, D), _row_tile(TS, GW), _row_tile(TS, 2 * D)] + [_res_spec(r) for r in DILATIONS] * 3,
        out_shape=[jax.ShapeDtypeStruct((S, D), BF16), jax.ShapeDtypeStruct((S, GW), F32),
                   jax.ShapeDtypeStruct((S, 2 * D), BF16)] + [_res_shape(S, r, BF16) for r in DILATIONS] * 3,
        scratch_shapes=[pltpu.VMEM((IN_WIDTH, D), BF16), pltpu.SemaphoreType.DMA((1,)), pltpu.VMEM(RES_SCRATCH, F32)],
        compiler_params=_params(),
    )(h, vec, cos, sin, win)


def _mix_in_bwd(dh, h, vec, cos, sin, dp, dqkv, dgl, win, name):
    S, D = h.shape

    def body(dh_ref, h_ref, vec_ref, cos_ref, sin_ref, dp_ref, *rest):
        dqkv_refs = rest[:9]
        dgl_ref, win_hbm, dhp_ref, dproj_ref, red_ref, win_v, sem, scr = rest[9:]
        _load_weights([(win_hbm, win_v)], sem)

        @pl.when(pl.program_id(0) == 0)
        def _():
            red_ref[...] = jnp.zeros_like(red_ref)

        cosv, sinv = cos_ref[...], sin_ref[...]
        first = _first_half_mask(TS)
        dproj_ref[:, P_OFF:Q_OFF] = dp_ref[...].astype(BF16)
        for which, off in enumerate((Q_OFF, K_OFF, V_OFF)):
            for gi in range(3):
                for half in range(2):
                    piece = _from_residues(dqkv_refs[which * 3 + gi], slice(half * 128, (half + 1) * 128),
                                           DILATIONS[gi], scr)
                    if which < 2:
                        piece = _rope(piece, cosv, sinv, first, -1.0)
                    c0 = off + gi * GW + half * 128
                    dproj_ref[:, c0:c0 + 128] = piece.astype(BF16)
        dproj_ref[:, G_OFF:IN_WIDTH] = dgl_ref[...]
        du = _dot(dproj_ref[...], win_v[...])
        g, sh, sc = (vec_ref[k:k + 1, :] for k in range(3))
        r, xn, y, _ = _norm_mod(h_ref[...], g, sc, sh)
        dx, dsh, dsc, dg = _norm_mod_bwd(du, r, xn, y, g, sc)
        dhp_ref[...] = dh_ref[...] + dx
        red_ref[0:1, :] += dsh
        red_ref[1:2, :] += dsc
        red_ref[3:4, :] += dg

    return pl.pallas_call(
        body, name=name, grid=(S // TS,),
        in_specs=[_row_tile(TS, D), _row_tile(TS, D), _const((8, D)), _row_tile(TS, 128), _row_tile(TS, 128),
                  _row_tile(TS, GW)] + [_res_spec(r) for r in DILATIONS] * 3 + [_row_tile(TS, 2 * D), ANY],
        out_specs=[_row_tile(TS, D), _row_tile(TS, IN_WIDTH), _const((8, D))],
        out_shape=[jax.ShapeDtypeStruct((S, D), F32), jax.ShapeDtypeStruct((S, IN_WIDTH), BF16),
                   jax.ShapeDtypeStruct((8, D), F32)],
        scratch_shapes=[pltpu.VMEM((IN_WIDTH, D), BF16), pltpu.SemaphoreType.DMA((1,)), pltpu.VMEM(RES_SCRATCH, F32)],
        compiler_params=_params(),
    )(dh, h, vec, cos, sin, dp, *dqkv, dgl, win)


def _pool_lanes(rows):
    lane = lax.broadcasted_iota(jnp.int32, (rows, GW), 1)
    return lane // HEAD_DIM


def _pool_window(rows):
    grp = _pool_lanes(rows)
    w = jnp.full((rows, GW), POOL_WINDOWS[0], jnp.int32)
    for k in range(1, len(POOL_WINDOWS)):
        w = jnp.where(grp == k, POOL_WINDOWS[k], w)
    return grp, w


def _pool_fwd(p, wbd, scale, name, ts=1024):
    S = p.shape[0]
    ext = ts + HALO

    def body(pc_ref, ph_ref, wbd_ref, sc_ref, d_ref, y_ref):
        i = pl.program_id(0)
        cur = pc_ref[...]
        halo = jnp.where(i > 0, ph_ref[...], 0.0)
        s = jnp.concatenate([halo, cur], axis=0)
        grp, w = _pool_window(ext)
        sel = jnp.zeros((ext, GW), F32)
        for k, wk in enumerate(POOL_WINDOWS):
            s = s + pltpu.roll(s, wk // 2, 0)
            sel = jnp.where(grp == k, s, sel)
        t = i * ts + lax.broadcasted_iota(jnp.int32, (ts, GW), 0)
        count = jnp.minimum(t + 1, w[HALO:]).astype(F32)
        d = (sel[HALO:] / count - cur).astype(BF16)
        d_ref[...] = d
        y_ref[...] = (_dot(d, wbd_ref[...]) * sc_ref[...]).astype(BF16)

    return pl.pallas_call(
        body, name=name, grid=(S // ts,),
        in_specs=[_row_tile(ts, GW),
                  pl.BlockSpec((HALO, GW), lambda i: (jnp.maximum(i * (ts // HALO) - 1, 0), 0)),
                  _const((GW, GW)), _const((1, GW))],
        out_specs=[_row_tile(ts, GW), _row_tile(ts, GW)],
        out_shape=[jax.ShapeDtypeStruct((S, GW), BF16), jax.ShapeDtypeStruct((S, GW), BF16)],
        compiler_params=_params(),
    )(p, p, wbd, scale)


def _pool_bwd(dy, d, wbd, scale, name, ts=1024):
    S = dy.shape[0]
    ext = ts + HALO
    nsteps = S // ts
    last_halo = S // HALO - 1

    def body(dyc_ref, dyh_ref, d_ref, wbd_ref, sc_ref, dp_ref, dw_ref, ds_ref):
        i = pl.program_id(0)

        @pl.when(i == 0)
        def _():
            dw_ref[...] = jnp.zeros_like(dw_ref)
            ds_ref[...] = jnp.zeros_like(ds_ref)

        dyc = dyc_ref[...]
        dyh = jnp.where(i < nsteps - 1, dyh_ref[...], 0.0)
        dys = (jnp.concatenate([dyc, dyh], axis=0) * sc_ref[...]).astype(BF16)
        dd = _dot_nt(dys, wbd_ref[...])
        grp, w = _pool_window(ext)
        t = i * ts + lax.broadcasted_iota(jnp.int32, (ext, GW), 0)
        s = dd / jnp.minimum(t + 1, w).astype(F32)
        sel = jnp.zeros((ext, GW), F32)
        for k, wk in enumerate(POOL_WINDOWS):
            s = s + pltpu.roll(s, ext - wk // 2, 0)
            sel = jnp.where(grp == k, s, sel)
        dp_ref[...] = sel[:ts] - dd[:ts]
        dv = d_ref[...]
        z = _dot(dv, wbd_ref[...])
        ds_ref[0:1, :] += jnp.sum(dyc * z, axis=0, keepdims=True)
        dw_ref[...] += _dot_tn(dv, dys[:ts])

    return pl.pallas_call(
        body, name=name, grid=(nsteps,),
        in_specs=[_row_tile(ts, GW),
                  pl.BlockSpec((HALO, GW), lambda i: (jnp.minimum((i + 1) * (ts // HALO), last_halo), 0)),
                  _row_tile(ts, GW), _const((GW, GW)), _const((1, GW))],
        out_specs=[_row_tile(ts, GW), _const((GW, GW)), _const((8, GW))],
        out_shape=[jax.ShapeDtypeStruct((S, GW), F32), jax.ShapeDtypeStruct((GW, GW), F32),
                   jax.ShapeDtypeStruct((8, GW), F32)],
        compiler_params=_params(),
    )(dy, dy, d, wbd, scale)


def _head_id(rows):
    return lax.broadcasted_iota(jnp.int32, (rows, GW), 1) // HEAD_DIM


def _stack_heads(t, hid):
    return jnp.concatenate([jnp.where(hid == h, t, jnp.zeros_like(t)) for h in range(HEADS)], axis=0)


def _unstack_heads(t_all, hid):
    out = jnp.zeros((QB, GW), F32)
    for h in range(HEADS):
        out = jnp.where(hid == h, t_all[h * QB:(h + 1) * QB], out)
    return out


def _band_masks():
    row = lax.broadcasted_iota(jnp.int32, (HEADS * QB, 2 * QB), 0) % QB
    col = lax.broadcasted_iota(jnp.int32, (HEADS * QB, 2 * QB), 1)
    rel = row + QB - col
    band = (rel >= 0) & (rel <= BAND)
    return band, band & (col >= QB)


def _stream_mask(masks, n, off):
    band, first = masks
    return band if off > 0 else first | (band & (n > 0))


FWD_STREAMS, BWD_STREAMS = 16, 8


def _streams(r, nb, most):
    if r > 1:
        ns = min(r, most)
        return nb, [(lambda rb, l=l: ns * rb + l, 0) for l in range(ns)]
    ns = min(most, nb)
    return nb // ns, [(lambda rb: 0, l * (nb // ns)) for l in range(ns)]


def _attn_fwd(q, k, v, name):
    r, L, _ = q.shape
    nbs, streams = _streams(r, L // QB, FWD_STREAMS)
    ns = len(streams)
    grid = (max(r // ns, 1), nbs)

    def cur(res, off):
        return pl.BlockSpec((None, QB, GW), lambda rb, n: (res(rb), n + off, 0))

    def prev(res, off):
        return pl.BlockSpec((None, QB, GW), lambda rb, n: (res(rb), jnp.maximum(n + off - 1, 0), 0))

    def body(*refs):
        n = pl.program_id(1)
        hid = _head_id(QB)
        masks = _band_masks()
        o_ref, lse_ref = refs[5 * len(streams):]
        for l, (_, off) in enumerate(streams):
            q_ref, kp_ref, kc_ref, vp_ref, vc_ref = refs[5 * l:5 * l + 5]
            qs = _stack_heads(q_ref[...], hid)
            kc = jnp.concatenate([kp_ref[...], kc_ref[...]], axis=0)
            vc = jnp.concatenate([vp_ref[...], vc_ref[...]], axis=0)
            s = _dot_nt(qs, kc) * (HEAD_DIM ** -0.5)
            s = jnp.where(_stream_mask(masks, n, off), s, -jnp.inf)
            m = jnp.max(s, axis=-1, keepdims=True)
            e = jnp.exp(s - m)
            den = jnp.sum(e, axis=-1, keepdims=True)
            lse = m + jnp.log(den)
            pr = (e * (1.0 / den)).astype(BF16)
            o_ref[l] = _unstack_heads(_dot(pr, vc), hid).astype(BF16)
            lse_ref[l] = _unstack_heads(jnp.broadcast_to(lse, (HEADS * QB, GW)), hid)

    in_specs, args = [], []
    for res, off in streams:
        in_specs += [cur(res, off), prev(res, off), cur(res, off), prev(res, off), cur(res, off)]
        args += [q, k, k, v, v]
    out = jax.ShapeDtypeStruct((ns * grid[0], nbs * QB, GW), F32)
    both = pl.BlockSpec((ns, QB, GW), lambda rb, n: (rb, n, 0))
    o, lse = pl.pallas_call(
        body, name=name, grid=grid, in_specs=in_specs, out_specs=[both, both],
        out_shape=[jax.ShapeDtypeStruct(out.shape, BF16), out],
        compiler_params=_params(),
    )(*args)
    return o.reshape(q.shape), lse.reshape(q.shape)


def _head_rows(t_full, hid):
    del hid
    return jnp.concatenate([t_full[:, h * HEAD_DIM:h * HEAD_DIM + 1] for h in range(HEADS)], axis=0)


def _attn_bwd(q, k, v, do, lse, cterm, name):
    r, L, _ = q.shape
    nbs, streams = _streams(r, L // QB, BWD_STREAMS)
    ns = len(streams)
    parts = r == 1

    def spec(res, index):
        return pl.BlockSpec((None, QB, GW), lambda rb, n: (res(rb), index(n), 0))

    def body(*refs):
        dq_ref, dk_ref, dv_ref, carry_k, carry_v, seam_k, seam_v = refs[8 * ns:]
        n = pl.program_id(1)

        @pl.when(n == 0)
        def _():
            carry_k[...] = jnp.zeros_like(carry_k)
            carry_v[...] = jnp.zeros_like(carry_v)

        @pl.when(n < nbs)
        def _():
            hid = _head_id(QB)
            masks = _band_masks()
            for l, (_, off) in enumerate(streams):
                q_ref, do_ref, lse_ref, c_ref, kp_ref, kc_ref, vp_ref, vc_ref = refs[8 * l:8 * l + 8]
                qs = _stack_heads(q_ref[...], hid)
                dos = _stack_heads(do_ref[...], hid)
                kc = jnp.concatenate([kp_ref[...], kc_ref[...]], axis=0)
                vc = jnp.concatenate([vp_ref[...], vc_ref[...]], axis=0)
                s = _dot_nt(qs, kc) * (HEAD_DIM ** -0.5)
                s = jnp.where(_stream_mask(masks, n, off), s, -jnp.inf)
                p = jnp.exp(s - _head_rows(lse_ref[...], hid))
                dp = _dot_nt(dos, vc)
                ds = (p * (dp + _head_rows(c_ref[...], hid)) * (HEAD_DIM ** -0.5)).astype(BF16)
                dq_ref[l] = _unstack_heads(_dot(ds, kc), hid).astype(BF16)
                dkc = _dot_tn(ds, qs)
                dvc = _dot_tn(p.astype(BF16), dos)
                if parts and l > 0:
                    @pl.when(n == 0)
                    def _():
                        seam_k[l] = dkc[:QB]
                        seam_v[l] = dvc[:QB]
                dk_ref[l] = (carry_k[l] + dkc[:QB]).astype(BF16)
                dv_ref[l] = (carry_v[l] + dvc[:QB]).astype(BF16)
                carry_k[l] = dkc[QB:]
                carry_v[l] = dvc[QB:]

        @pl.when(n == nbs)
        def _():
            for l in range(ns):
                if parts and l + 1 < ns:
                    dk_ref[l] = (carry_k[l] + seam_k[l + 1]).astype(BF16)
                    dv_ref[l] = (carry_v[l] + seam_v[l + 1]).astype(BF16)
                else:
                    dk_ref[l] = carry_k[l].astype(BF16)
                    dv_ref[l] = carry_v[l].astype(BF16)

    in_specs, args = [], []
    for res, off in streams:
        qside = functools.partial(lambda n, off: jnp.minimum(n, nbs - 1) + off, off=off)
        kprev = functools.partial(lambda n, off: jnp.maximum(jnp.minimum(n, nbs) - 1 + off, 0), off=off)
        in_specs += [spec(res, qside)] * 4 + [spec(res, kprev), spec(res, qside)] * 2
        args += [q, do, lse, cterm, k, k, v, v]
    out = jax.ShapeDtypeStruct((ns * max(r // ns, 1), nbs * QB, GW), BF16)
    qout = pl.BlockSpec((ns, QB, GW), lambda rb, n: (rb, jnp.minimum(n, nbs - 1), 0))
    kout = pl.BlockSpec((ns, QB, GW), lambda rb, n: (rb, jnp.maximum(n - 1, 0), 0))
    buf = pltpu.VMEM((ns, QB, GW), F32)
    outs = pl.pallas_call(
        body, name=name, grid=(max(r // ns, 1), nbs + 1),
        in_specs=in_specs, out_specs=[qout, kout, kout], out_shape=[out, out, out],
        scratch_shapes=[buf, buf, buf, buf],
        compiler_params=_params(),
    )(*args)
    return [t.reshape(q.shape) for t in outs]


def _token_order(refs, scr):
    return [_from_residues(ref, slice(0, GW), r, scr) for ref, r in zip(refs, DILATIONS)]


def _group_weights(lses):
    l0, l1, l2 = lses
    m = jnp.maximum(jnp.maximum(l0, l1), l2)
    e = [jnp.exp(l - m) for l in (l0, l1, l2)]
    den = e[0] + e[1] + e[2]
    return [ei / den for ei in e]


def _mix_out_fwd(h, vec, gates, ypool, o3, lse3, wpb, wab, wout, name):
    S, D = h.shape

    def body(h_ref, vec_ref, gates_ref, yp_ref, o0, o1, o2, l0, l1, l2, wpb_hbm, wab_hbm, wout_hbm,
             hn_ref, ya_ref, merged_ref, tm_ref, wpb_v, wab_v, wout_v, sem, scr):
        _load_weights([(wpb_hbm, wpb_v), (wab_hbm, wab_v), (wout_hbm, wout_v)], sem)
        gt = vec_ref[3:4, :]
        wts = _group_weights(_token_order((l0, l1, l2), scr))
        og = _token_order((o0, o1, o2), scr)
        ya = (wts[0] * og[0] + wts[1] * og[1] + wts[2] * og[2]).astype(BF16)
        ya_ref[...] = ya
        merged = (gates_ref[:, :D].astype(F32) * _dot(yp_ref[...], wpb_v[...])
                  + gates_ref[:, D:].astype(F32) * _dot(ya, wab_v[...])).astype(BF16)
        merged_ref[...] = merged
        tm = _dot(merged, wout_v[...])
        tm_ref[...] = tm.astype(BF16)
        hn_ref[...] = h_ref[...] + gt * tm

    grp = _row_tile(TS, GW)
    res = [_res_spec(r) for r in DILATIONS]
    return pl.pallas_call(
        body, name=name, grid=(S // TS,),
        in_specs=[_row_tile(TS, D), _const((8, D)), _row_tile(TS, 2 * D), grp] + res * 2 + [ANY, ANY, ANY],
        out_specs=[_row_tile(TS, D), grp, _row_tile(TS, D), _row_tile(TS, D)],
        out_shape=[jax.ShapeDtypeStruct((S, D), F32), jax.ShapeDtypeStruct((S, GW), BF16),
                   jax.ShapeDtypeStruct((S, D), BF16), jax.ShapeDtypeStruct((S, D), BF16)],
        scratch_shapes=[pltpu.VMEM((GW, D), BF16), pltpu.VMEM((GW, D), BF16), pltpu.VMEM((D, D), BF16),
                        pltpu.SemaphoreType.DMA((3,)), pltpu.VMEM(RES_SCRATCH, F32)],
        compiler_params=_params(),
    )(h, vec, gates, ypool, *o3, *lse3, wpb, wab, wout)


def _mix_out_bwd(dh, tm, vec, gates, ypool, o3, lse3, wpb, wab, wout, name, rider=None):
    S, D = dh.shape

    def body(*refs):
        if rider is None:
            return compute(*refs)
        host, mine = rider.split(refs, 14, 12)
        rider.head(mine, pl.program_id(0))
        compute(*host)
        rider.tail(mine, pl.program_id(0), S // TS)

    def compute(dh_ref, tm_ref, vec_ref, gates_ref, yp_ref, o0, o1, o2, l0, l1, l2, wpb_hbm, wab_hbm, wout_hbm,
                dtm_ref, dgl_ref, dypb_ref, dyab_ref, dyp_ref, do0, do1, do2, c0, c1, c2, red_ref,
                wpb_v, wab_v, wout_v, sem, scr):
        _load_weights([(wpb_hbm, wpb_v), (wab_hbm, wab_v), (wout_hbm, wout_v)], sem)

        @pl.when(pl.program_id(0) == 0)
        def _():
            red_ref[...] = jnp.zeros_like(red_ref)

        gt = vec_ref[3:4, :]
        dh_v = dh_ref[...]
        red_ref[2:3, :] += jnp.sum(tm_ref[...].astype(F32) * dh_v, axis=0, keepdims=True)
        dtm = (gt * dh_v).astype(BF16)
        dtm_ref[...] = dtm
        dm = _dot_nt(dtm, wout_v[...])
        wts = _group_weights(_token_order((l0, l1, l2), scr))
        og = _token_order((o0, o1, o2), scr)
        ya = wts[0] * og[0] + wts[1] * og[1] + wts[2] * og[2]
        ypb = _dot(yp_ref[...], wpb_v[...])
        yab = _dot(ya.astype(BF16), wab_v[...])
        gp = gates_ref[:, :D].astype(F32)
        ga = gates_ref[:, D:].astype(F32)
        dgl_ref[:, :D] = (dm * ypb * gp * (1.0 - gp)).astype(BF16)
        dgl_ref[:, D:] = (dm * yab * ga * (1.0 - ga)).astype(BF16)
        dypb = (dm * gp).astype(BF16)
        dyab = (dm * ga).astype(BF16)
        dypb_ref[...] = dypb
        dyab_ref[...] = dyab
        dyp_ref[...] = _dot_nt(dypb, wpb_v[...])
        dya = _dot_nt(dyab, wab_v[...])
        row = lax.broadcasted_iota(jnp.int32, (GW, GW), 0) // HEAD_DIM
        col = lax.broadcasted_iota(jnp.int32, (GW, GW), 1) // HEAD_DIM
        ones = jnp.where(row == col, 1.0, 0.0).astype(BF16)
        prod = dya * ya
        hi = prod.astype(BF16)
        lo = (prod - hi.astype(F32)).astype(BF16)
        tot = _dot(hi, ones) + _dot(lo, ones)
        for wg, do_ref, c_ref, r in zip(wts, (do0, do1, do2), (c0, c1, c2), DILATIONS):
            _to_residues(wg * dya, do_ref, slice(0, GW), r, scr)
            _to_residues(-(wg * tot), c_ref, slice(0, GW), r, scr)

    grp = _row_tile(TS, GW)
    res = [_res_spec(r) for r in DILATIONS]
    specs = (
        [_row_tile(TS, D), _row_tile(TS, D), _const((8, D)), _row_tile(TS, 2 * D), grp] + res * 2
        + [ANY, ANY, ANY],
        [_row_tile(TS, D), _row_tile(TS, 2 * D), _row_tile(TS, D), _row_tile(TS, D), grp]
        + res * 2 + [_const((8, D))],
        [jax.ShapeDtypeStruct((S, D), BF16), jax.ShapeDtypeStruct((S, 2 * D), BF16),
         jax.ShapeDtypeStruct((S, D), BF16), jax.ShapeDtypeStruct((S, D), BF16), jax.ShapeDtypeStruct((S, GW), F32)]
        + [_res_shape(S, r, BF16) for r in DILATIONS] + [_res_shape(S, r, F32) for r in DILATIONS]
        + [jax.ShapeDtypeStruct((8, D), F32)],
        [pltpu.VMEM((GW, D), BF16), pltpu.VMEM((GW, D), BF16), pltpu.VMEM((D, D), BF16),
         pltpu.SemaphoreType.DMA((3,)), pltpu.VMEM(RES_SCRATCH, F32)])
    in_specs, out_specs, out_shape, scratch = specs if rider is None else rider.specs(*specs)
    outs = pl.pallas_call(
        body, name=name, grid=(S // TS,), in_specs=in_specs, out_specs=out_specs, out_shape=out_shape,
        scratch_shapes=scratch, compiler_params=_params(),
    )(dh, tm, vec, gates, ypool, *o3, *lse3, wpb, wab, wout, *(rider.arrays if rider else []))
    return outs if rider is None else (outs[:12], outs[12:])


def _ada_mod(c_all, w, b, name):
    def body(c_ref, w_ref, b_ref, cond_ref, mod_ref):
        cv = c_ref[...]
        cond = cv * jax.nn.sigmoid(cv)
        cond_ref[...] = cond
        mod_ref[...] = jnp.dot(cond, w_ref[...], preferred_element_type=F32,
                               precision=lax.Precision.HIGHEST) + b_ref[...]

    return pl.pallas_call(
        body, name=name,
        out_shape=[jax.ShapeDtypeStruct(c_all.shape, F32), jax.ShapeDtypeStruct((c_all.shape[0], w.shape[1]), F32)],
        compiler_params=_params(),
    )(c_all, w, b)


def _adamw_math(w, g, m, v):
    m = ADAM_B1 * m + (1.0 - ADAM_B1) * g
    v = ADAM_B2 * v + (1.0 - ADAM_B2) * (g * g)
    m_hat = m / (1.0 - ADAM_B1 ** ADAM_STEP)
    v_hat = v / (1.0 - ADAM_B2 ** ADAM_STEP)
    delta = -ADAM_LR * (m_hat / (jnp.sqrt(v_hat) + ADAM_EPS) + ADAM_WD * w)
    return delta, m, v


def _adamw_many(ws, gs, ms, vs, name):
    n = len(ws)

    def body(*refs):
        for k in range(n):
            w_ref, g_ref, m_ref, v_ref = (refs[j * n + k] for j in range(4))
            d_ref, mo_ref, vo_ref = (refs[(4 + j) * n + k] for j in range(3))
            d_ref[...], mo_ref[...], vo_ref[...] = _adamw_math(w_ref[...], g_ref[...], m_ref[...], v_ref[...])

    outs = pl.pallas_call(
        body, name=name, out_shape=[jax.ShapeDtypeStruct(t.shape, F32) for t in ws] * 3,
        compiler_params=_params(),
    )(*ws, *gs, *ms, *vs)
    return outs[:n], outs[n:2 * n], outs[2 * n:]


def _ada_grad_adamw(cond_t, dmod, w, m, v, name, tr=256):
    R, C = w.shape
    nb = dmod.shape[0]

    def body(ct_ref, dm_ref, w_ref, m_ref, v_ref, g_ref, d_ref, mo_ref, vo_ref):
        ct = ct_ref[...]
        dm = dm_ref[...]
        g = jnp.zeros((tr, C), F32)
        for bi in range(nb):
            g = g + ct[:, bi:bi + 1] * dm[bi:bi + 1, :]
        g_ref[...] = g
        d_ref[...], mo_ref[...], vo_ref[...] = _adamw_math(w_ref[...], g, m_ref[...], v_ref[...])

    spec = _row_tile(tr, C)
    out = jax.ShapeDtypeStruct((R, C), F32)
    return pl.pallas_call(
        body, name=name, grid=(R // tr,),
        in_specs=[_row_tile(tr, nb), _const((nb, C)), spec, spec, spec],
        out_specs=[spec] * 4, out_shape=[out] * 4,
        compiler_params=_params(),
    )(cond_t, dmod, w, m, v)


def _row_step(rows, cap=256):
    for cand in range(cap, 15, -16):
        if rows % cand == 0:
            return cand
    return rows


def _slot_sum(x_ref):
    acc = x_ref[0].astype(F32)
    for k in range(1, x_ref.shape[0]):
        acc = acc + x_ref[k].astype(F32)
    return acc


def _sum_slots(x, name, out_dtype=F32):
    n, R, C = x.shape
    tr = _row_step(R)

    def body(x_ref, o_ref):
        o_ref[...] = _slot_sum(x_ref).astype(out_dtype)

    return pl.pallas_call(
        body, name=name, grid=(R // tr,),
        in_specs=[pl.BlockSpec((n, tr, C), lambda i: (0, i, 0))],
        out_specs=_row_tile(tr, C), out_shape=jax.ShapeDtypeStruct((R, C), out_dtype),
        compiler_params=_params(),
    )(x)


def _sum_pair(core, g, recv, name):
    _, _, R, C = g.shape
    tr = _row_step(R, cap=1024)

    def body(core_ref, g_ref, r_ref, o_ref):
        o_ref[...] = (g_ref[...].astype(F32) + r_ref[...].astype(F32)).astype(BF16)

    return pl.pallas_call(
        body, name=name, out_shape=jax.ShapeDtypeStruct((4, R, C), BF16),
        grid_spec=pltpu.PrefetchScalarGridSpec(
            num_scalar_prefetch=1, grid=(4, R // tr),
            in_specs=[pl.BlockSpec((None, None, tr, C), lambda k, i, core_ref: (k, core_ref[0], i, 0)),
                      pl.BlockSpec((None, tr, C), lambda k, i, core_ref: (k, i, 0))],
            out_specs=pl.BlockSpec((None, tr, C), lambda k, i, core_ref: (k, i, 0))),
        compiler_params=_params(),
    )(core, g, recv)


def _sum_adamw(chip, own, recv, w, m, v, name):
    _, R, C = own.shape
    tr = _row_step(R, cap=512)

    def body(chip_ref, own_ref, r_ref, w_ref, m_ref, v_ref, g_ref, d_ref, mo_ref, vo_ref):
        g = own_ref[...].astype(F32) + _slot_sum(r_ref)
        g_ref[...] = g
        d_ref[...], mo_ref[...], vo_ref[...] = _adamw_math(w_ref[...], g, m_ref[...], v_ref[...])

    spec = pl.BlockSpec((tr, C), lambda i, chip_ref: (i, 0))
    out = jax.ShapeDtypeStruct((R, C), F32)
    return pl.pallas_call(
        body, name=name, out_shape=[out] * 4,
        grid_spec=pltpu.PrefetchScalarGridSpec(
            num_scalar_prefetch=1, grid=(R // tr,),
            in_specs=[pl.BlockSpec((None, tr, C), lambda i, chip_ref: (chip_ref[0], i, 0)),
                      pl.BlockSpec((3, tr, C), lambda i, chip_ref: (0, i, 0)), spec, spec, spec],
            out_specs=[spec] * 4),
        compiler_params=_params(),
    )(chip, own, recv, w, m, v)


def _place():
    return lax.axis_index("x"), lax.axis_index("y"), lax.axis_index("c")


def _gather_steps(x_refs, out_refs, send_sems, recv_sems):
    n = len(x_refs)
    x, y, c = _place()
    me, sibling = (x, y, c), (x, y, 1 - c)
    chips = [(1 - x, y), (x, 1 - y), (1 - x, 1 - y)]

    def rows(a, px, py, pc):
        return out_refs[a].at[4 * px + 2 * py + pc]

    def copy(a, k, block, to, src=None):
        return pltpu.make_async_remote_copy(
            src_ref=rows(a, *block) if src is None else src, dst_ref=rows(a, *block),
            send_sem=send_sems.at[a, k], recv_sem=recv_sems.at[a, k], device_id=to, device_id_type=MESH)

    def first(a):
        return [copy(a, 0, me, sibling, src=x_refs[a])] + [
            copy(a, 1 + j, me, (*chip, c), src=x_refs[a]) for j, chip in enumerate(chips)]

    def passed(a, j):
        return copy(a, 4 + j, (*chips[j], c), sibling)

    def start():
        for a in range(n):
            for cp in first(a):
                cp.start()

    def relay():
        for j, chip in enumerate(chips):
            for a in range(n):
                copy(a, 1 + j, (*chip, c), me).wait_recv()
                passed(a, j).start()

    def finish():
        for a in range(n):
            copy(a, 0, sibling, me).wait_recv()
            for j, chip in enumerate(chips):
                copy(a, 4 + j, (*chip, 1 - c), me).wait_recv()
        for a in range(n):
            for cp in first(a) + [passed(a, j) for j in range(3)]:
                cp.wait_send()

    return start, relay, finish


def _gather_tree_steps(x_refs, out_refs, send_sems, recv_sems):
    n = len(x_refs)
    x, y, c = _place()
    me, sibling = (x, y, c), (x, y, 1 - c)
    xn, yn, dg = (1 - x, y), (x, 1 - y), (1 - x, 1 - y)

    def rows(a, px, py, pc):
        return out_refs[a].at[4 * px + 2 * py + pc]

    def copy(a, k, block, to, src=None):
        return pltpu.make_async_remote_copy(
            src_ref=rows(a, *block) if src is None else src, dst_ref=rows(a, *block),
            send_sem=send_sems.at[a, k], recv_sem=recv_sems.at[a, k], device_id=to, device_id_type=MESH)

    def own(a):
        return [copy(a, 0, me, sibling, src=x_refs[a]), copy(a, 1, me, (*xn, c), src=x_refs[a]),
                copy(a, 2, me, (*yn, c), src=x_refs[a])]

    def north_hands_on(a):
        return copy(a, 3, (*xn, c), (*yn, c))

    def south_hands_on(a):
        return copy(a, 3, (*yn, c), (*xn, c))

    def to_sibling(a):
        return [copy(a, 4, (*xn, c), sibling), copy(a, 5, (*yn, c), sibling), copy(a, 6, (*dg, c), sibling)]

    def start():
        for a in range(n):
            for cp in own(a):
                cp.start()

    def relay_neighbours():
        for a in range(n):
            copy(a, 1, (*xn, c), me).wait_recv()
            to_sibling(a)[0].start()

        @pl.when(c == 1)
        def _():
            for a in range(n):
                north_hands_on(a).start()

        for a in range(n):
            copy(a, 2, (*yn, c), me).wait_recv()
            to_sibling(a)[1].start()

        @pl.when(c == 0)
        def _():
            for a in range(n):
                south_hands_on(a).start()

    def relay_diagonal():
        for a in range(n):
            copy(a, 3, (*dg, c), me).wait_recv()
            to_sibling(a)[2].start()

    def finish():
        for a in range(n):
            copy(a, 0, sibling, me).wait_recv()
            copy(a, 4, (*xn, 1 - c), me).wait_recv()
            copy(a, 5, (*yn, 1 - c), me).wait_recv()
            copy(a, 6, (*dg, 1 - c), me).wait_recv()
        for a in range(n):
            for cp in own(a) + to_sibling(a):
                cp.wait_send()

        @pl.when(c == 1)
        def _():
            for a in range(n):
                north_hands_on(a).wait_send()

        @pl.when(c == 0)
        def _():
            for a in range(n):
                south_hands_on(a).wait_send()

    return start, relay_neighbours, relay_diagonal, finish


def _all_gather_tree(arrs, name, gather=()):
    n, extra = len(arrs), len(gather)

    def body(*refs):
        sems = refs[2 * (n + extra):]
        if extra:
            g_start, g_relay, g_finish = _small_gather_steps(
                refs[n:n + extra], refs[2 * n + extra:2 * (n + extra)], *sems[2:])
            g_start()
        for step in _gather_tree_steps(refs[:n], refs[n + extra:2 * n + extra], *sems[:2]):
            step()
        if extra:
            g_relay()
            g_finish()

    return pl.pallas_call(
        body, name=name,
        out_shape=[jax.ShapeDtypeStruct((N_DEV,) + t.shape, t.dtype) for t in list(arrs) + list(gather)],
        in_specs=[ANY] * (n + extra), out_specs=[ANY] * (n + extra),
        scratch_shapes=[pltpu.SemaphoreType.DMA((n, 7)), pltpu.SemaphoreType.DMA((n, 7))]
        + (_small_gather_scratch(extra) if extra else []),
    )(*arrs, *gather)


def _all_gather(arrs, name, own=True):
    n = len(arrs)

    def body(*refs):
        x_refs, out_refs = refs[:n], refs[n:2 * n]
        send_sems, recv_sems, local_sems = refs[2 * n:]
        me = 4 * lax.axis_index("x") + 2 * lax.axis_index("y") + lax.axis_index("c")
        mine = [pltpu.make_async_copy(x_refs[a], out_refs[a].at[me], local_sems.at[a]) for a in range(n)] if own else []
        for cp in mine:
            cp.start()
        for step in _gather_steps(x_refs, out_refs, send_sems, recv_sems):
            step()
        for cp in mine:
            cp.wait()

    return pl.pallas_call(
        body, name=name, out_shape=[jax.ShapeDtypeStruct((N_DEV,) + t.shape, t.dtype) for t in arrs],
        in_specs=[ANY] * n, out_specs=[ANY] * n,
        scratch_shapes=[pltpu.SemaphoreType.DMA((n, 7)), pltpu.SemaphoreType.DMA((n, 7)),
                        pltpu.SemaphoreType.DMA((n,))],
    )(*arrs)


def _pair_exchange_steps(g_refs, out_refs, send_sems, recv_sems):
    x, y, c = _place()

    def give():
        return [pltpu.make_async_remote_copy(
            src_ref=g_refs[a].at[pl.ds(0, 4), 1 - c], dst_ref=out_refs[a], send_sem=send_sems.at[a],
            recv_sem=recv_sems.at[a], device_id=(x, y, 1 - c), device_id_type=MESH) for a in range(len(g_refs))]

    def start():
        for cp in give():
            cp.start()

    def finish():
        for cp in give():
            cp.wait()

    return start, finish


def _pair_exchange(arrs, name):
    n = len(arrs)

    def body(*refs):
        for step in _pair_exchange_steps(refs[:n], refs[n:2 * n], *refs[2 * n:]):
            step()

    return pl.pallas_call(
        body, name=name,
        out_shape=[jax.ShapeDtypeStruct((4,) + t.shape[2:], t.dtype) for t in arrs],
        in_specs=[ANY] * n, out_specs=[ANY] * n,
        scratch_shapes=[pltpu.SemaphoreType.DMA((n,)), pltpu.SemaphoreType.DMA((n,))],
    )(*arrs)


def _chip_exchange_steps(p_refs, out_refs, send_sems, recv_sems):
    x, y, c = _place()
    chips = [(1 - x, y), (x, 1 - y), (1 - x, 1 - y)]

    def copies():
        return [pltpu.make_async_remote_copy(
            src_ref=p_refs[a].at[2 * px + py], dst_ref=out_refs[a].at[j], send_sem=send_sems.at[a, j],
            recv_sem=recv_sems.at[a, j], device_id=(px, py, c), device_id_type=MESH)
            for a in range(len(p_refs)) for j, (px, py) in enumerate(chips)]

    def start():
        for cp in copies():
            cp.start()

    def finish():
        for cp in copies():
            cp.wait()

    return start, finish


def _small_gather_steps(x_refs, out_refs, send_sems, recv_sems, local_sems):
    me = 4 * lax.axis_index("x") + 2 * lax.axis_index("y") + lax.axis_index("c")
    start, relay, finish = _gather_steps(x_refs, out_refs, send_sems, recv_sems)

    def mine():
        return [pltpu.make_async_copy(x_refs[a], out_refs[a].at[me], local_sems.at[a]) for a in range(len(x_refs))]

    def start_all():
        for cp in mine():
            cp.start()
        start()

    def finish_all():
        finish()
        for cp in mine():
            cp.wait()

    return start_all, relay, finish_all


def _small_gather_scratch(k):
    return [pltpu.SemaphoreType.DMA((k, 7)), pltpu.SemaphoreType.DMA((k, 7)), pltpu.SemaphoreType.DMA((k,))]


def _chip_exchange(arrs, name, gather=()):
    n, k = len(arrs), len(gather)

    def body(*refs):
        ins, outs, sems = refs[:n + k], refs[n + k:2 * (n + k)], refs[2 * (n + k):]
        start, finish = _chip_exchange_steps(ins[:n], outs[:n], *sems[:2])
        if k:
            g_start, g_relay, g_finish = _small_gather_steps(ins[n:], outs[n:], *sems[2:])
            g_start()
        start()
        if k:
            g_relay()
        finish()
        if k:
            g_finish()

    return pl.pallas_call(
        body, name=name,
        out_shape=[jax.ShapeDtypeStruct((3,) + t.shape[1:], t.dtype) for t in arrs]
        + [jax.ShapeDtypeStruct((N_DEV,) + t.shape, t.dtype) for t in gather],
        in_specs=[ANY] * (n + k), out_specs=[ANY] * (n + k),
        scratch_shapes=[pltpu.SemaphoreType.DMA((n, 3)), pltpu.SemaphoreType.DMA((n, 3))]
        + (_small_gather_scratch(k) if k else []),
    )(*arrs, *gather)


class _Rider:
    def __init__(self, arrays, out_shape, sems, steps, relay_at=()):
        self.arrays, self.out_shape, self.scratch, self.steps = list(arrays), out_shape, sems, steps
        self.n = len(self.arrays)
        self.relay_at = relay_at

    def specs(self, in_specs, out_specs, out_shape, scratch):
        extra = [ANY] * self.n
        return in_specs + extra, out_specs + extra, out_shape + self.out_shape, scratch + self.scratch

    def split(self, refs, n_in, n_out):
        k = self.n
        a, b = n_in + k, n_in + k + n_out
        return refs[:n_in] + refs[a:b] + refs[b + k:-2], (refs[n_in:a], refs[b:b + k], refs[-2:])

    def head(self, mine, step):
        pl.when(step == 0)(self.steps(mine[0], mine[1], *mine[2])[0])

    def tail(self, mine, step, nsteps):
        steps = self.steps(mine[0], mine[1], *mine[2])
        for relay, frac in zip(steps[1:-1], self.relay_at):
            pl.when(step == min(int(frac * nsteps), nsteps - 1))(relay)
        pl.when(step == nsteps - 1)(steps[-1])


def _gather_rider(arrs, relay_at=(0.5, 0.78)):
    n = len(arrs)
    return _Rider(arrs, [jax.ShapeDtypeStruct((N_DEV,) + t.shape, t.dtype) for t in arrs],
                  [pltpu.SemaphoreType.DMA((n, 7)), pltpu.SemaphoreType.DMA((n, 7))], _gather_tree_steps,
                  relay_at)


def _pair_exchange_rider(arrs):
    n = len(arrs)
    return _Rider(arrs, [jax.ShapeDtypeStruct((4,) + t.shape[2:], t.dtype) for t in arrs],
                  [pltpu.SemaphoreType.DMA((n,)), pltpu.SemaphoreType.DMA((n,))], _pair_exchange_steps)


def _chip_exchange_rider(arrs):
    n = len(arrs)
    return _Rider(arrs, [jax.ShapeDtypeStruct((3,) + t.shape[1:], t.dtype) for t in arrs],
                  [pltpu.SemaphoreType.DMA((n, 3)), pltpu.SemaphoreType.DMA((n, 3))], _chip_exchange_steps)


def _rope_tables(positions):
    inv_freq = ROPE_THETA ** (-jnp.arange(0, HEAD_DIM, 2, dtype=F32) / HEAD_DIM)
    ang = positions.astype(F32)[:, None] * inv_freq
    cos, sin = jnp.cos(ang), jnp.sin(ang)
    return jnp.tile(cos, (1, 4)), jnp.tile(jnp.concatenate([-sin, sin], axis=1), (1, 2))


class _GradReducer:
    def __init__(self):
        self.core = lax.axis_index("c").astype(jnp.int32).reshape(1)
        self.own, self.others, self.waiting, self.riding = {}, {}, [], []

    def pair(self, named):
        mine = self._split(named)
        self._summed(mine, _pair_exchange(list(mine.values()), "reduce_pair_" + next(iter(named))))

    def pair_rider(self, named):
        self.pairing = self._split(named)
        return _pair_exchange_rider(list(self.pairing.values()))

    def pair_landed(self, results):
        self._summed(self.pairing, results)

    @staticmethod
    def _split(named):
        return {k: g.reshape((4, 2) + g.shape[1:]) for k, g in named.items()}

    def _summed(self, mine, theirs):
        for (k, g), r in zip(mine.items(), theirs):
            self.own[k] = _sum_pair(self.core, g, r, "sum_pair_" + k)
        self.waiting += list(mine)

    def rider(self):
        self.riding, self.waiting = self.waiting, []
        return _chip_exchange_rider([self.own[k] for k in self.riding])

    def landed(self, results):
        self.others.update(zip(self.riding, results))

    def flush(self, name, gather=()):
        keys, self.waiting = self.waiting, []
        res = _chip_exchange([self.own[k] for k in keys], name, gather=gather)
        self.others.update(zip(keys, res))
        return res[len(keys):]


def _by_owner(g):
    if g.ndim == 3:
        return g if g.shape[0] == N_DEV else g.reshape(N_DEV, g.shape[1] * g.shape[0] // N_DEV, g.shape[2])
    return g.reshape(N_DEV, g.shape[0] // N_DEV, g.shape[1])


def _local_step(x, target, positions, mod, small, W, late=None, red=None):
    S, D = x.shape
    gains =jnp.stack([small["g1"], small["g2"], small["g3"]])[:, None, :]
    v1, v2, v3 = jnp.pad(jnp.concatenate([gains, mod.reshape(3, 3, D)], axis=1), ((0, 0), (0, 4), (0, 0)))
    vf = jnp.pad(small["gf"][None], ((0, 7), (0, 0)))
    cos, sin = _rope_tables(positions)
    wbd = jax.scipy.linalg.block_diag(*[small["w_pool"][k] for k in range(4)]).astype(BF16)
    pscale = small["pool_scale"].reshape(1, GW)

    if late is None:
        h1, u1, ab1, act1, f1 = _ffn_fwd(x, v1, W["w1in"], W["w1out"], "ffn1_fwd")
    else:
        (h1, u1, ab1, act1, f1), landed = _ffn_fwd(x, v1, W["w1in"], W["w1out"], "ffn1_fwd", rider=late[0])
        W = {**W, **late[1](landed)}
    u2, p, gates, *qkv = _mix_in_fwd(h1, v2, cos, sin, W["win"], "mix_in_fwd")
    dpool, ypool = _pool_fwd(p, wbd, pscale, "pool_fwd")
    o3, lse3 = [], []
    for gi in range(len(DILATIONS)):
        o, lse = _attn_fwd(qkv[gi], qkv[3 + gi], qkv[6 + gi], f"attn_fwd_{gi}")
        o3.append(o)
        lse3.append(lse)
    h2, ya, merged, tm = _mix_out_fwd(h1, v2, gates, ypool, o3, lse3, W["wpb"], W["wab"], W["wout"], "mix_out_fwd")
    dh3, u3, ab3, act3, f3, loss_blk, dgf = _ffn_fwd(h2, v3, W["w2in"], W["w2out"], "ffn2_fwd", final=(target, vf))

    dh2, dab3, df3, red3 = _ffn_bwd(dh3, h2, f3, ab3, v3, W["w2in"], W["w2out"], "ffn2_bwd")
    half_f = ab3.shape[2] // 2
    G = {"w2in": _by_owner(_wgrad(dab3, u3, "wgrad_2in", tm=half_f))}
    mix_out_args = (dh2, tm, v2, gates, ypool, o3, lse3, W["wpb"], W["wab"], W["wout"], "mix_out_bwd")
    if red is None:
        G["w2out"] = _by_owner(_wgrad(act3, df3, "wgrad_2out", tm=half_f))
        mix_out = _mix_out_bwd(*mix_out_args)
    else:
        g2out, landed = _wgrad(act3, df3, "wgrad_2out", tm=half_f, rider=red.pair_rider({"w2in": G["w2in"]}))
        red.pair_landed(landed)
        G["w2out"] = _by_owner(g2out)
        red.pair({"w2out": G["w2out"]})
        mix_out, landed = _mix_out_bwd(*mix_out_args, rider=red.rider())
        red.landed(landed)
    (dtm, dgl, dypb, dyab, dyp, do0, do1, do2, c0, c1, c2, red2o) = mix_out
    dq3, dk3, dv3 = [], [], []
    for gi, (do, ct) in enumerate(zip((do0, do1, do2), (c0, c1, c2))):
        dq, dk, dv = _attn_bwd(qkv[gi], qkv[3 + gi], qkv[6 + gi], do, lse3[gi], ct, f"attn_bwd_{gi}")
        dq3.append(dq)
        dk3.append(dk)
        dv3.append(dv)
    dp, dwbd, dps = _pool_bwd(dyp, dpool, wbd, pscale, "pool_bwd")
    dh1, dproj, red2i = _mix_in_bwd(dh2, h1, v2, cos, sin, dp, dq3 + dk3 + dv3, dgl, W["win"], "mix_in_bwd")
    G["win"] = _by_owner(_wgrad(dproj, u2, "wgrad_in", tm=1152))
    G["wpb"] = _full_to_cols(_wgrad(ypool, dypb, "wgrad_pb"))
    G["wab"] = _full_to_cols(_wgrad(ya, dyab, "wgrad_ab"))
    if red is None:
        G["wout"] = _by_owner(_wgrad(merged, dtm, "wgrad_out"))
    else:
        gout, landed = _wgrad(merged, dtm, "wgrad_out", rider=red.pair_rider({k: G[k] for k in ("win", "wpb", "wab")}))
        red.pair_landed(landed)
        G["wout"] = _by_owner(gout)
        red.pair({"wout": G["wout"]})
    dx, dab1, df1, red1 = _ffn_bwd(dh1, x, f1, ab1, v1, W["w1in"], W["w1out"], "ffn1_bwd")
    dmod = jnp.concatenate([red1[:3], (red2i + red2o)[:3], red3[:3]])
    dsmall = {
        "g1": red1[3], "g2": red2i[3], "g3": red3[3], "gf": dgf[0],
        "w_pool": jnp.stack([dwbd[k * 64:(k + 1) * 64, k * 64:(k + 1) * 64] for k in range(4)]),
        "pool_scale": dps[0],
    }
    if red is None:
        G["w1in"] = _by_owner(_wgrad(dab1, u1, "wgrad_1in", tm=half_f))
        G["w1out"] = _by_owner(_wgrad(act1, df1, "wgrad_1out", tm=half_f))
        return loss_blk[0, 0], dx, G, dmod, dsmall
    g1in, landed = _wgrad(dab1, u1, "wgrad_1in", tm=half_f, rider=red.rider())
    red.landed(landed)
    red.pair({"w1in": _by_owner(g1in)})
    g1out, landed = _wgrad(act1, df1, "wgrad_1out", tm=half_f, rider=red.rider())
    red.landed(landed)
    red.pair({"w1out": _by_owner(g1out)})
    part = _pack_small(dict(b_ada=dmod, g_norm_ffn1=dsmall["g1"], g_norm_mix=dsmall["g2"], g_norm_ffn2=dsmall["g3"],
                            g_final=dsmall["gf"], w_pool=dsmall["w_pool"], pool_scale=dsmall["pool_scale"]),
                       loss_blk[0, 0])
    (parts,) = red.flush("reduce_chips_w1out", gather=[part])
    return dx, parts


SHARDED = ("w_ffn1_in", "w_ffn1_out", "w_in", "w_pool_branch", "w_attn_branch", "w_out", "w_ffn2_in", "w_ffn2_out")
TRANSPOSED = ("w_ffn1_in", "w_in", "w_ffn2_in")
FIRST = ("w_ffn1_in", "w_ffn1_out")
LATER = tuple(n for n in SHARDED if n not in FIRST)
GRAD_KEY = dict(w_ffn1_in="w1in", w_ffn1_out="w1out", w_in="win", w_pool_branch="wpb", w_attn_branch="wab",
                w_out="wout", w_ffn2_in="w2in", w_ffn2_out="w2out")


def _cols_to_full(g):
    return g.transpose(1, 0, 2).reshape(g.shape[1], N_DEV * g.shape[2])


def _full_to_cols(t):
    return t.reshape(t.shape[0], N_DEV, t.shape[1] // N_DEV).transpose(1, 0, 2)


SMALL = (("b_ada", 72), ("g_norm_ffn1", 8), ("g_norm_mix", 8), ("g_norm_ffn2", 8), ("g_final", 8),
         ("w_pool", 128), ("pool_scale", 8))


def _pack_small(vals, loss):
    rows = []
    for name, nrows in SMALL:
        t = vals[name].reshape(-1, 128)
        rows.append(jnp.pad(t, ((0, nrows - t.shape[0]), (0, 0))))
    rows.append(jnp.full((8, 128), loss, F32))
    return jnp.concatenate(rows)


def _unpack_small(slab, shapes):
    out, off = {}, 0
    for name, nrows in SMALL:
        used = 1
        for d in shapes[name]:
            used *= d
        out[name] = slab[off:off + used // 128].reshape(shapes[name])
        off += nrows
    return out, slab[off, 0]


def _as_2d(t):
    return t.reshape(-1, t.shape[-1])


def kernel(x, c, positions, w_ada, b_ada, g_norm_ffn1, w_ffn1_in, w_ffn1_out, g_norm_mix, w_in, w_pool, pool_scale, w_pool_branch, w_attn_branch, w_out, g_norm_ffn2, w_ffn2_in, w_ffn2_out, g_final, loss_target, m_w_ada, m_b_ada, m_g_norm_ffn1, m_w_ffn1_in, m_w_ffn1_out, m_g_norm_mix, m_w_in, m_w_pool, m_pool_scale, m_w_pool_branch, m_w_attn_branch, m_w_out, m_g_norm_ffn2, m_w_ffn2_in, m_w_ffn2_out, m_g_final, v_w_ada, v_b_ada, v_g_norm_ffn1, v_w_ffn1_in, v_w_ffn1_out, v_g_norm_mix, v_w_in, v_w_pool, v_pool_scale, v_w_pool_branch, v_w_attn_branch, v_w_out, v_g_norm_ffn2, v_w_ffn2_in, v_w_ffn2_out, v_g_final):
    names = ["w_ada", "b_ada", "g_norm_ffn1", "w_ffn1_in", "w_ffn1_out", "g_norm_mix", "w_in", "w_pool", "pool_scale",
             "w_pool_branch", "w_attn_branch", "w_out", "g_norm_ffn2", "w_ffn2_in", "w_ffn2_out", "g_final"]
    w = dict(w_ada=w_ada, b_ada=b_ada, g_norm_ffn1=g_norm_ffn1, w_ffn1_in=w_ffn1_in, w_ffn1_out=w_ffn1_out,
             g_norm_mix=g_norm_mix, w_in=w_in, w_pool=w_pool, pool_scale=pool_scale, w_pool_branch=w_pool_branch,
             w_attn_branch=w_attn_branch, w_out=w_out, g_norm_ffn2=g_norm_ffn2, w_ffn2_in=w_ffn2_in,
             w_ffn2_out=w_ffn2_out, g_final=g_final)
    m = dict(w_ada=m_w_ada, b_ada=m_b_ada, g_norm_ffn1=m_g_norm_ffn1, w_ffn1_in=m_w_ffn1_in, w_ffn1_out=m_w_ffn1_out,
             g_norm_mix=m_g_norm_mix, w_in=m_w_in, w_pool=m_w_pool, pool_scale=m_pool_scale,
             w_pool_branch=m_w_pool_branch, w_attn_branch=m_w_attn_branch, w_out=m_w_out, g_norm_ffn2=m_g_norm_ffn2,
             w_ffn2_in=m_w_ffn2_in, w_ffn2_out=m_w_ffn2_out, g_final=m_g_final)
    v = dict(w_ada=v_w_ada, b_ada=v_b_ada, g_norm_ffn1=v_g_norm_ffn1, w_ffn1_in=v_w_ffn1_in, w_ffn1_out=v_w_ffn1_out,
             g_norm_mix=v_g_norm_mix, w_in=v_w_in, w_pool=v_w_pool, pool_scale=v_pool_scale,
             w_pool_branch=v_w_pool_branch, w_attn_branch=v_w_attn_branch, w_out=v_w_out, g_norm_ffn2=v_g_norm_ffn2,
             w_ffn2_in=v_w_ffn2_in, w_ffn2_out=v_w_ffn2_out, g_final=v_g_final)
    shapes = {n: w[n].shape for n in names}
    me = 4 * lax.axis_index("x") + 2 * lax.axis_index("y") + lax.axis_index("c")
    D = x.shape[-1]
    n_mod = w_ada.shape[-1] * N_DEV // D

    def local(t, name):
        return t[name][0].T if name in TRANSPOSED else t[name][0]

    shards = {name: local(w, name).astype(BF16) for name in SHARDED}

    def gather_done(names, fulls):
        return {name: lax.dynamic_update_index_in_dim(full, shards[name], me, axis=0)
                for name, full in zip(names, fulls)}

    def ffn_weights(g, pre):
        return {"w%sin" % pre: g["w_ffn%s_in" % pre].reshape(2, -1, D),
                "w%sout" % pre: g["w_ffn%s_out" % pre].reshape(-1, D)}

    def later_weights(fulls):
        g = gather_done(LATER, fulls)
        return dict(win=g["w_in"].reshape(-1, D), wpb=_cols_to_full(g["w_pool_branch"]),
                    wab=_cols_to_full(g["w_attn_branch"]), wout=g["w_out"].reshape(D, D), **ffn_weights(g, "2"))

    *first, c_all = _all_gather_tree([shards[n] for n in FIRST], "gather_ffn1", gather=[c.reshape(D // 128, 128)])
    W = ffn_weights(gather_done(FIRST, first), "1")

    ada_cols = w_ada.shape[-1]
    b_mine = lax.dynamic_slice_in_dim(b_ada, me * ada_cols, ada_cols, axis=1)
    cond, mod_part = _ada_mod(c_all.reshape(N_DEV, D), w_ada[0], b_mine, "ada_mod")
    (mod_all,) = _all_gather([mod_part.reshape(-1, 128)], "gather_mod")
    mod_all = mod_all.reshape(N_DEV, N_DEV, ada_cols)
    mod = lax.dynamic_index_in_dim(mod_all, me, axis=1, keepdims=False).reshape(n_mod, D)

    small = dict(g1=g_norm_ffn1[0], g2=g_norm_mix[0], g3=g_norm_ffn2[0], gf=g_final, w_pool=w_pool[0],
                 pool_scale=pool_scale[0])
    red = _GradReducer()
    dx, parts = _local_step(
        x[0], loss_target[0], positions[0], mod, small, W,
        late=(_gather_rider([shards[n] for n in LATER]), later_weights), red=red)
    chip = (2 * lax.axis_index("x") + lax.axis_index("y")).astype(jnp.int32).reshape(1)

    gsmall, loss = _unpack_small(_sum_slots(parts, "sum_small"), shapes)
    rows_mine = ada_cols // 128
    dmod_mine = lax.dynamic_slice_in_dim(parts, me * rows_mine, rows_mine, axis=1).reshape(N_DEV, ada_cols)

    grads, delta, new_m, new_v = {}, {}, {}, {}
    grads["w_ada"], delta["w_ada"], new_m["w_ada"], new_v["w_ada"] = (
        t[None] for t in _ada_grad_adamw(cond.T, dmod_mine, w_ada[0], m_w_ada[0], v_w_ada[0], "ada_grad_adamw"))
    for name in SHARDED:
        key = GRAD_KEY[name]
        res = _sum_adamw(chip, red.own[key], red.others[key], local(w, name), local(m, name), local(v, name),
                         "adamw_" + name)
        grads[name], delta[name], new_m[name], new_v[name] = (
            (t.T if name in TRANSPOSED else t)[None] for t in res)
    small_names = [name for name, _ in SMALL]
    res = _adamw_many(*([_as_2d(t[name]) for name in small_names] for t in (w, gsmall, m, v)), "adamw_small")
    for dst, vals in zip((delta, new_m, new_v), res):
        dst.update({name: t.reshape(shapes[name]) for name, t in zip(small_names, vals)})
    grads.update(gsmall)

    return (loss, dx[None], *[grads[n] for n in names], *[delta[n] for n in names],
            *[new_m[n] for n in names], *[new_v[n] for n in names])
```
